```python
import math
import jax
import jax.numpy as jnp
from jax import lax
import numpy as np

D_MODEL = 1024
BATCH = 8
SEQ = 8192
DEPTH = 4

N_MEM = 256
N_MIXERS = 3
EPS = 1e-6

D_MIX = D_MODEL
XA_HEADS = 4
XA_HEAD_DIM = 128
D_XA = XA_HEADS * XA_HEAD_DIM
D_CAT = D_MIX + D_XA
D_GATE = D_CAT

GM_CHUNK = 128
GM_GROUPS = 8
GM_GROUP_DIM = D_MIX // GM_GROUPS

SC_WIDTH = 3

DN_HEADS = 8
DN_HEAD_DIM = D_MIX // DN_HEADS
DN_CONV = 4
DN_CHUNK = 64

N_LAYERS_A = (DEPTH + 2) // N_MIXERS
N_LAYERS_B = (DEPTH + 1) // N_MIXERS
N_LAYERS_C = DEPTH // N_MIXERS

A_IN = 2 * D_MIX + D_XA + D_GATE
B_IN = 3 * D_MIX + D_XA + D_GATE
C_IN = 3 * D_MIX + 2 * DN_HEADS + D_XA + D_GATE

kernel_name = "hybrid_gmlp_shortconv_gdn_memattn_trunk"


def rmsnorm(x, w):
    xf = x.astype(jnp.float32)
    y = xf * lax.rsqrt(jnp.mean(xf * xf, axis=-1, keepdims=True) + EPS)
    return (y * w.astype(jnp.float32)).astype(x.dtype)


def layernorm(x, w, b):
    xf = x.astype(jnp.float32)
    xc = xf - jnp.mean(xf, axis=-1, keepdims=True)
    y = xc * lax.rsqrt(jnp.mean(xc * xc, axis=-1, keepdims=True) + EPS)
    return (y * w.astype(jnp.float32) + b.astype(jnp.float32)).astype(x.dtype)


def l2norm(x):
    xf = x.astype(jnp.float32)
    return xf * lax.rsqrt(jnp.sum(xf * xf, axis=-1, keepdims=True) + EPS)


def causal_depthwise_conv(x, w):
    K = w.shape[0]
    S = x.shape[1]
    xp = jnp.pad(x, ((0, 0), (K - 1, 0), (0, 0)))
    y = xp[:, 0:S] * w[0]
    for k in range(1, K):
        y = y + xp[:, k:k + S] * w[k]
    return y


def memory_attention(q, mem_k, mem_v):
    B, S, _ = q.shape
    qh = q.reshape(B, S, XA_HEADS, XA_HEAD_DIM)
    s = jnp.einsum("bshd,bmhd->bhsm", qh, mem_k).astype(jnp.float32) * (XA_HEAD_DIM ** -0.5)
    p = jax.nn.softmax(s, axis=-1).astype(q.dtype)
    o = jnp.einsum("bhsm,bmhd->bshd", p, mem_v)
    return o.reshape(B, S, D_XA)


def chunked_causal_gmlp(u, v, ln_w, ln_b, w_s, b_s):
    B, S, _ = u.shape
    n_chunks = S // GM_CHUNK
    u = jax.nn.gelu(u)
    v = layernorm(jax.nn.gelu(v), ln_w, ln_b)
    vc = v.reshape(B, n_chunks, GM_CHUNK, GM_GROUPS, GM_GROUP_DIM)
    causal = jnp.tril(jnp.ones((GM_CHUNK, GM_CHUNK), dtype=bool))
    ws = jnp.where(causal[None], w_s, jnp.zeros_like(w_s))
    sp = jnp.einsum("gts,bnsgc->bntgc", ws, vc) + b_s.T[:, :, None]
    return u * sp.reshape(B, S, D_MIX)


def chunk_gated_delta_rule(q, k, v, g, beta):
    B, S, H, D = q.shape
    C = DN_CHUNK
    N = S // C

    def to_chunks(t):
        t = t.reshape((B, N, C, H) + t.shape[3:])
        return jnp.moveaxis(t, 3, 1)

    qc, kc, vc = to_chunks(q), to_chunks(k), to_chunks(v)
    gc, bc = to_chunks(g), to_chunks(beta)
    gcum = jnp.cumsum(gc, axis=-1)
    incl = jnp.tril(jnp.ones((C, C), dtype=bool))
    strict = jnp.tril(jnp.ones((C, C), dtype=bool), k=-1)
    decay = jnp.exp(jnp.where(incl, gcum[..., :, None] - gcum[..., None, :], -jnp.inf))
    kb = kc * bc[..., None]
    a_mat = jnp.where(strict, jnp.einsum("bhnid,bhnjd->bhnij", kb, kc) * decay, 0.0)
    eye = jnp.eye(C, dtype=q.dtype)
    t_mat = lax.linalg.triangular_solve(eye + a_mat, jnp.broadcast_to(eye, a_mat.shape),
                                        left_side=True, lower=True, unit_diagonal=True)
    u_c = jnp.einsum("bhnij,bhnjd->bhnid", t_mat, vc * bc[..., None])
    w_c = jnp.einsum("bhnij,bhnjd->bhnid", t_mat, kb * jnp.exp(gcum)[..., None])
    qk = jnp.einsum("bhnid,bhnjd->bhnij", qc, kc) * decay
    q_dec = qc * jnp.exp(gcum)[..., None]
    k_dec = kc * jnp.exp(gcum[..., -1:] - gcum)[..., None]
    g_last = jnp.exp(gcum[..., -1])
    xs = tuple(jnp.moveaxis(t, 2, 0) for t in (q_dec, qk, u_c, w_c, k_dec, g_last))

    def step(state, inp):
        q_i, qk_i, u_i, w_i, k_i, gl_i = inp
        v_new = u_i - jnp.einsum("bhcd,bhde->bhce", w_i, state)
        o_i = (jnp.einsum("bhcd,bhde->bhce", q_i, state)
               + jnp.einsum("bhij,bhje->bhie", qk_i, v_new))
        state = state * gl_i[..., None, None] + jnp.einsum("bhcd,bhce->bhde", k_i, v_new)
        return state, o_i

    state0 = jnp.zeros((B, H, D, D), dtype=q.dtype)
    _, o = lax.scan(step, state0, xs)
    o = jnp.transpose(o, (1, 0, 3, 2, 4))
    return o.reshape(B, S, H, D)


def gated_deltanet(qkv, a, b, conv_w, a_log, dt_bias, o_norm_w):
    B, S, _ = qkv.shape
    dtype = qkv.dtype
    qkv = jax.nn.silu(causal_depthwise_conv(qkv, conv_w))
    q, k, v = jnp.split(qkv, 3, axis=-1)
    shp = (B, S, DN_HEADS, DN_HEAD_DIM)
    q = l2norm(q.reshape(shp)) * (DN_HEAD_DIM ** -0.5)
    k = l2norm(k.reshape(shp))
    v = v.reshape(shp).astype(jnp.float32)
    beta = jax.nn.sigmoid(b.astype(jnp.float32))
    g = -jnp.exp(a_log.astype(jnp.float32)) * jax.nn.softplus(
        a.astype(jnp.float32) + dt_bias.astype(jnp.float32))
    o = chunk_gated_delta_rule(q, k, v, g, beta)
    o = rmsnorm(o, o_norm_w)
    return o.reshape(B, S, D_MIX).astype(dtype)


def branch_a(h, w_in, ln_w, ln_b, w_s, b_s, mem_k, mem_v):
    u, v, qx, z = jnp.split(h @ w_in, [D_MIX, 2 * D_MIX, 2 * D_MIX + D_XA], axis=-1)
    y = chunked_causal_gmlp(u, v, ln_w, ln_b, w_s, b_s)
    return jnp.concatenate([y, memory_attention(qx, mem_k, mem_v)], axis=-1) * jax.nn.silu(z)


def branch_b(h, w_in, conv_w, mem_k, mem_v):
    bg, cg, hv, qx, z = jnp.split(h @ w_in, [D_MIX, 2 * D_MIX, 3 * D_MIX, 3 * D_MIX + D_XA], axis=-1)
    y = bg * causal_depthwise_conv(cg * hv, conv_w)
    return jnp.concatenate([y, memory_attention(qx, mem_k, mem_v)], axis=-1) * jax.nn.silu(z)


def branch_c(h, w_in, conv_w, a_log, dt_bias, o_norm_w, mem_k, mem_v):
    c0 = 3 * D_MIX
    qkv, a, b, qx, z = jnp.split(
        h @ w_in, [c0, c0 + DN_HEADS, c0 + 2 * DN_HEADS, c0 + 2 * DN_HEADS + D_XA], axis=-1)
    y = gated_deltanet(qkv, a, b, conv_w, a_log, dt_bias, o_norm_w)
    return jnp.concatenate([y, memory_attention(qx, mem_k, mem_v)], axis=-1) * jax.nn.silu(z)


def _fwd_setup_inputs(seed: int = 0) -> dict:
    key = jax.random.key(seed)
    ks = jax.random.split(key, 20)
    f32 = jnp.float32

    def normal(k, shape, scale):
        return jax.random.normal(k, shape, f32) * scale

    def gain(k, shape):
        return 1.0 + 0.02 * jax.random.normal(k, shape, f32)

    a_coef = jax.random.uniform(ks[17], (N_LAYERS_C, DN_HEADS), f32, 1.0, 16.0)
    dt = jnp.exp(jax.random.uniform(ks[18], (N_LAYERS_C, DN_HEADS), f32,
                                    math.log(1e-3), math.log(1e-1)))
    return {
        "x": normal(ks[0], (BATCH, SEQ, D_MODEL), 1.0),
        "mem": normal(ks[1], (BATCH, N_MEM, D_MODEL), 1.0),
        "mem_norm_w": gain(ks[2], (D_MODEL,)),
        "w_mem_kv": normal(ks[3], (D_MODEL, 2 * D_XA), D_MODEL ** -0.5),
        "norm_pre": gain(ks[4], (DEPTH, D_MODEL)),
        "norm_post": gain(ks[5], (DEPTH, D_MODEL)),
        "w_out": normal(ks[6], (DEPTH, D_CAT, D_MODEL), D_CAT ** -0.5),
        "a_w_in": normal(ks[7], (N_LAYERS_A, D_MODEL, A_IN), D_MODEL ** -0.5),
        "a_ln_w": gain(ks[8], (N_LAYERS_A, D_MIX)),
        "a_ln_b": normal(ks[9], (N_LAYERS_A, D_MIX), 0.02),
        "a_w_s": normal(ks[10], (N_LAYERS_A, GM_GROUPS, GM_CHUNK, GM_CHUNK), 0.5 * GM_CHUNK ** -0.5),
        "a_b_s": gain(ks[11], (N_LAYERS_A, GM_GROUPS, GM_CHUNK)),
        "b_w_in": normal(ks[12], (N_LAYERS_B, D_MODEL, B_IN), D_MODEL ** -0.5),
        "b_conv_w": normal(ks[13], (N_LAYERS_B, SC_WIDTH, D_MIX), SC_WIDTH ** -0.5),
        "c_w_in": normal(ks[14], (N_LAYERS_C, D_MODEL, C_IN), D_MODEL ** -0.5),
        "c_conv_w": normal(ks[15], (N_LAYERS_C, DN_CONV, 3 * D_MIX), DN_CONV ** -0.5),
        "c_a_log": jnp.log(a_coef),
        "c_dt_bias": dt + jnp.log(-jnp.expm1(-dt)),
        "c_o_norm_w": gain(ks[16], (N_LAYERS_C, DN_HEAD_DIM)),
    }


def _fwd_reference(x, mem, mem_norm_w, w_mem_kv, norm_pre, norm_post, w_out,
              a_w_in, a_ln_w, a_ln_b, a_w_s, a_b_s,
              b_w_in, b_conv_w,
              c_w_in, c_conv_w, c_a_log, c_dt_bias, c_o_norm_w):
    B = mem.shape[0]
    mem_k, mem_v = jnp.split(rmsnorm(mem, mem_norm_w) @ w_mem_kv, 2, axis=-1)
    mem_k = mem_k.reshape(B, N_MEM, XA_HEADS, XA_HEAD_DIM)
    mem_v = mem_v.reshape(B, N_MEM, XA_HEADS, XA_HEAD_DIM)
    for i in range(DEPTH):
        kind, j = i % N_MIXERS, i // N_MIXERS
        h = rmsnorm(x, norm_pre[i])
        if kind == 0:
            y = branch_a(h, a_w_in[j], a_ln_w[j], a_ln_b[j], a_w_s[j], a_b_s[j], mem_k, mem_v)
        elif kind == 1:
            y = branch_b(h, b_w_in[j], b_conv_w[j], mem_k, mem_v)
        else:
            y = branch_c(h, c_w_in[j], c_conv_w[j], c_a_log[j], c_dt_bias[j], c_o_norm_w[j],
                         mem_k, mem_v)
        x = x + rmsnorm(y @ w_out[i], norm_post[i])
    return x


import jax as _jax
import jax.numpy as _jnp

TWIN_FORMAT = 'train_step'
FWD_PARAMS = ['x', 'mem', 'mem_norm_w', 'w_mem_kv', 'norm_pre', 'norm_post', 'w_out', 'a_w_in', 'a_ln_w', 'a_ln_b', 'a_w_s', 'a_b_s', 'b_w_in', 'b_conv_w', 'c_w_in', 'c_conv_w', 'c_a_log', 'c_dt_bias', 'c_o_norm_w']
TWIN_WEIGHTS = ['mem_norm_w', 'w_mem_kv', 'norm_pre', 'norm_post', 'w_out', 'a_w_in', 'a_ln_w', 'a_ln_b', 'a_w_s', 'a_b_s', 'b_w_in', 'b_conv_w', 'c_w_in', 'c_conv_w', 'c_a_log', 'c_dt_bias', 'c_o_norm_w']
TWIN_DIFF_INPUT = 'x'
TWIN_INPUTS = ['x', 'mem', 'mem_norm_w', 'w_mem_kv', 'norm_pre', 'norm_post', 'w_out', 'a_w_in', 'a_ln_w', 'a_ln_b', 'a_w_s', 'a_b_s', 'b_w_in', 'b_conv_w', 'c_w_in', 'c_conv_w', 'c_a_log', 'c_dt_bias', 'c_o_norm_w', 'loss_target', 'm_mem_norm_w', 'm_w_mem_kv', 'm_norm_pre', 'm_norm_post', 'm_w_out', 'm_a_w_in', 'm_a_ln_w', 'm_a_ln_b', 'm_a_w_s', 'm_a_b_s', 'm_b_w_in', 'm_b_conv_w', 'm_c_w_in', 'm_c_conv_w', 'm_c_a_log', 'm_c_dt_bias', 'm_c_o_norm_w', 'v_mem_norm_w', 'v_w_mem_kv', 'v_norm_pre', 'v_norm_post', 'v_w_out', 'v_a_w_in', 'v_a_ln_w', 'v_a_ln_b', 'v_a_w_s', 'v_a_b_s', 'v_b_w_in', 'v_b_conv_w', 'v_c_w_in', 'v_c_conv_w', 'v_c_a_log', 'v_c_dt_bias', 'v_c_o_norm_w']
TWIN_OUTPUTS = ['loss', 'grad_x', 'grad_mem_norm_w', 'grad_w_mem_kv', 'grad_norm_pre', 'grad_norm_post', 'grad_w_out', 'grad_a_w_in', 'grad_a_ln_w', 'grad_a_ln_b', 'grad_a_w_s', 'grad_a_b_s', 'grad_b_w_in', 'grad_b_conv_w', 'grad_c_w_in', 'grad_c_conv_w', 'grad_c_a_log', 'grad_c_dt_bias', 'grad_c_o_norm_w', 'delta_mem_norm_w', 'delta_w_mem_kv', 'delta_norm_pre', 'delta_norm_post', 'delta_w_out', 'delta_a_w_in', 'delta_a_ln_w', 'delta_a_ln_b', 'delta_a_w_s', 'delta_a_b_s', 'delta_b_w_in', 'delta_b_conv_w', 'delta_c_w_in', 'delta_c_conv_w', 'delta_c_a_log', 'delta_c_dt_bias', 'delta_c_o_norm_w', 'new_m_mem_norm_w', 'new_m_w_mem_kv', 'new_m_norm_pre', 'new_m_norm_post', 'new_m_w_out', 'new_m_a_w_in', 'new_m_a_ln_w', 'new_m_a_ln_b', 'new_m_a_w_s', 'new_m_a_b_s', 'new_m_b_w_in', 'new_m_b_conv_w', 'new_m_c_w_in', 'new_m_c_conv_w', 'new_m_c_a_log', 'new_m_c_dt_bias', 'new_m_c_o_norm_w', 'new_v_mem_norm_w', 'new_v_w_mem_kv', 'new_v_norm_pre', 'new_v_norm_post', 'new_v_w_out', 'new_v_a_w_in', 'new_v_a_ln_w', 'new_v_a_ln_b', 'new_v_a_w_s', 'new_v_a_b_s', 'new_v_b_w_in', 'new_v_b_conv_w', 'new_v_c_w_in', 'new_v_c_conv_w', 'new_v_c_a_log', 'new_v_c_dt_bias', 'new_v_c_o_norm_w']
TWIN_LEAF_KINDS = {'loss': 'loss', 'grad_x': 'grad_x', 'grad_mem_norm_w': 'grad_w', 'grad_w_mem_kv': 'grad_w', 'grad_norm_pre': 'grad_w', 'grad_norm_post': 'grad_w', 'grad_w_out': 'grad_w', 'grad_a_w_in': 'grad_w', 'grad_a_ln_w': 'grad_w', 'grad_a_ln_b': 'grad_w', 'grad_a_w_s': 'grad_w', 'grad_a_b_s': 'grad_w', 'grad_b_w_in': 'grad_w', 'grad_b_conv_w': 'grad_w', 'grad_c_w_in': 'grad_w', 'grad_c_conv_w': 'grad_w', 'grad_c_a_log': 'grad_w', 'grad_c_dt_bias': 'grad_w', 'grad_c_o_norm_w': 'grad_w', 'delta_mem_norm_w': 'delta_w', 'delta_w_mem_kv': 'delta_w', 'delta_norm_pre': 'delta_w', 'delta_norm_post': 'delta_w', 'delta_w_out': 'delta_w', 'delta_a_w_in': 'delta_w', 'delta_a_ln_w': 'delta_w', 'delta_a_ln_b': 'delta_w', 'delta_a_w_s': 'delta_w', 'delta_a_b_s': 'delta_w', 'delta_b_w_in': 'delta_w', 'delta_b_conv_w': 'delta_w', 'delta_c_w_in': 'delta_w', 'delta_c_conv_w': 'delta_w', 'delta_c_a_log': 'delta_w', 'delta_c_dt_bias': 'delta_w', 'delta_c_o_norm_w': 'delta_w', 'new_m_mem_norm_w': 'new_m', 'new_m_w_mem_kv': 'new_m', 'new_m_norm_pre': 'new_m', 'new_m_norm_post': 'new_m', 'new_m_w_out': 'new_m', 'new_m_a_w_in': 'new_m', 'new_m_a_ln_w': 'new_m', 'new_m_a_ln_b': 'new_m', 'new_m_a_w_s': 'new_m', 'new_m_a_b_s': 'new_m', 'new_m_b_w_in': 'new_m', 'new_m_b_conv_w': 'new_m', 'new_m_c_w_in': 'new_m', 'new_m_c_conv_w': 'new_m', 'new_m_c_a_log': 'new_m', 'new_m_c_dt_bias': 'new_m', 'new_m_c_o_norm_w': 'new_m', 'new_v_mem_norm_w': 'new_v', 'new_v_w_mem_kv': 'new_v', 'new_v_norm_pre': 'new_v', 'new_v_norm_post': 'new_v', 'new_v_w_out': 'new_v', 'new_v_a_w_in': 'new_v', 'new_v_a_ln_w': 'new_v', 'new_v_a_ln_b': 'new_v', 'new_v_a_w_s': 'new_v', 'new_v_a_b_s': 'new_v', 'new_v_b_w_in': 'new_v', 'new_v_b_conv_w': 'new_v', 'new_v_c_w_in': 'new_v', 'new_v_c_conv_w': 'new_v', 'new_v_c_a_log': 'new_v', 'new_v_c_dt_bias': 'new_v', 'new_v_c_o_norm_w': 'new_v'}


def _forward(args):
    return _fwd_reference(*[args[k] for k in FWD_PARAMS])


def _output_shape():
    def fwd():
        inp = _fwd_setup_inputs(0)
        return _fwd_reference(*[inp[k] for k in FWD_PARAMS])
    out = _jax.eval_shape(fwd)
    return out.shape, out.dtype

N_MICROBATCH = 1
ADAM_LR = 0.001
ADAM_B1 = 0.9
ADAM_B2 = 0.999
ADAM_EPS = 1e-08
ADAM_WD = 0.01
ADAM_STEP = 10
PER_EXAMPLE_BATCH_AXIS = {'x': 0, 'mem': 0, 'loss_target': 0}
SHARED_INPUTS = []
_WEIGHT_DTYPES = {'mem_norm_w': _jnp.float32, 'w_mem_kv': _jnp.float32, 'norm_pre': _jnp.float32, 'norm_post': _jnp.float32, 'w_out': _jnp.float32, 'a_w_in': _jnp.float32, 'a_ln_w': _jnp.float32, 'a_ln_b': _jnp.float32, 'a_w_s': _jnp.float32, 'a_b_s': _jnp.float32, 'b_w_in': _jnp.float32, 'b_conv_w': _jnp.float32, 'c_w_in': _jnp.float32, 'c_conv_w': _jnp.float32, 'c_a_log': _jnp.float32, 'c_dt_bias': _jnp.float32, 'c_o_norm_w': _jnp.float32}
MOMENT_SCALE = {'mem_norm_w': 3.948892e-01, 'w_mem_kv': 3.652664e-01, 'norm_pre': 2.035146e+00, 'norm_post': 6.394249e+01, 'w_out': 2.192955e+00, 'a_w_in': 1.119660e+00, 'a_ln_w': 4.308470e-01, 'a_ln_b': 4.359210e-01, 'a_w_s': 8.503570e-01, 'a_b_s': 1.131264e+00, 'b_w_in': 9.142600e-01, 'b_conv_w': 1.150952e+00, 'c_w_in': 8.127658e-01, 'c_conv_w': 1.147613e+00, 'c_a_log': 1.567177e+01, 'c_dt_bias': 1.537662e+01, 'c_o_norm_w': 7.324299e+00}


def _to_microbatches(a, axis):
    t = _jnp.moveaxis(a, axis, 0)
    t = t.reshape((N_MICROBATCH, t.shape[0] // N_MICROBATCH) + t.shape[1:])
    return _jnp.moveaxis(t, 1, axis + 1)


def setup_inputs(seed: int = 0) -> dict:
    inp = _fwd_setup_inputs(seed)
    key = _jax.random.fold_in(_jax.random.key(seed), 7919)
    shape, _ = _output_shape()
    out = dict(inp)
    out["loss_target"] = _jax.random.normal(_jax.random.fold_in(key, 0), shape, _jnp.float32)
    for i, name in enumerate(TWIN_WEIGHTS):
        w = inp[name].astype(_jnp.float32)
        if MOMENT_SCALE is None:
            s = _jnp.sqrt(_jnp.mean(_jnp.square(w)) + 1e-30)
        else:
            s = MOMENT_SCALE[name]
        km, kv = _jax.random.split(_jax.random.fold_in(key, i + 1))
        out[name] = w
        out["m_" + name] = s * _jax.random.normal(km, w.shape, _jnp.float32)
        out["v_" + name] = (s * s) * _jax.random.uniform(kv, w.shape, _jnp.float32, 0.5, 1.5)
    if N_MICROBATCH > 1:
        for name, axis in PER_EXAMPLE_BATCH_AXIS.items():
            out[name] = _to_microbatches(out[name], axis)
    return {'x': out['x'], 'mem': out['mem'], 'mem_norm_w': out['mem_norm_w'], 'w_mem_kv': out['w_mem_kv'], 'norm_pre': out['norm_pre'], 'norm_post': out['norm_post'], 'w_out': out['w_out'], 'a_w_in': out['a_w_in'], 'a_ln_w': out['a_ln_w'], 'a_ln_b': out['a_ln_b'], 'a_w_s': out['a_w_s'], 'a_b_s': out['a_b_s'], 'b_w_in': out['b_w_in'], 'b_conv_w': out['b_conv_w'], 'c_w_in': out['c_w_in'], 'c_conv_w': out['c_conv_w'], 'c_a_log': out['c_a_log'], 'c_dt_bias': out['c_dt_bias'], 'c_o_norm_w': out['c_o_norm_w'], 'loss_target': out['loss_target'], 'm_mem_norm_w': out['m_mem_norm_w'], 'm_w_mem_kv': out['m_w_mem_kv'], 'm_norm_pre': out['m_norm_pre'], 'm_norm_post': out['m_norm_post'], 'm_w_out': out['m_w_out'], 'm_a_w_in': out['m_a_w_in'], 'm_a_ln_w': out['m_a_ln_w'], 'm_a_ln_b': out['m_a_ln_b'], 'm_a_w_s': out['m_a_w_s'], 'm_a_b_s': out['m_a_b_s'], 'm_b_w_in': out['m_b_w_in'], 'm_b_conv_w': out['m_b_conv_w'], 'm_c_w_in': out['m_c_w_in'], 'm_c_conv_w': out['m_c_conv_w'], 'm_c_a_log': out['m_c_a_log'], 'm_c_dt_bias': out['m_c_dt_bias'], 'm_c_o_norm_w': out['m_c_o_norm_w'], 'v_mem_norm_w': out['v_mem_norm_w'], 'v_w_mem_kv': out['v_w_mem_kv'], 'v_norm_pre': out['v_norm_pre'], 'v_norm_post': out['v_norm_post'], 'v_w_out': out['v_w_out'], 'v_a_w_in': out['v_a_w_in'], 'v_a_ln_w': out['v_a_ln_w'], 'v_a_ln_b': out['v_a_ln_b'], 'v_a_w_s': out['v_a_w_s'], 'v_a_b_s': out['v_a_b_s'], 'v_b_w_in': out['v_b_w_in'], 'v_b_conv_w': out['v_b_conv_w'], 'v_c_w_in': out['v_c_w_in'], 'v_c_conv_w': out['v_c_conv_w'], 'v_c_a_log': out['v_c_a_log'], 'v_c_dt_bias': out['v_c_dt_bias'], 'v_c_o_norm_w': out['v_c_o_norm_w']}


def _loss(weights, diff, rest, loss_target):
    with _jax.named_scope("forward"):
        args = {**rest, TWIN_DIFF_INPUT: diff, **{k: w.astype(_WEIGHT_DTYPES[k]) for k, w in weights.items()}}
        y = _forward(args)
    with _jax.named_scope("loss_head"):
        err = _jnp.square(y.astype(_jnp.float32) - loss_target)
        return 0.5 * _jnp.sum(_jnp.mean(err, axis=-1)) if err.ndim else 0.5 * err


def _adamw(w, g, m, v):
    m = ADAM_B1 * m + (1.0 - ADAM_B1) * g
    v = ADAM_B2 * v + (1.0 - ADAM_B2) * _jnp.square(g)
    m_hat = m / (1.0 - ADAM_B1 ** ADAM_STEP)
    v_hat = v / (1.0 - ADAM_B2 ** ADAM_STEP)
    delta = -ADAM_LR * (m_hat / (_jnp.sqrt(v_hat) + ADAM_EPS) + ADAM_WD * w)
    return delta, m, v


def reference(x, mem, mem_norm_w, w_mem_kv, norm_pre, norm_post, w_out, a_w_in, a_ln_w, a_ln_b, a_w_s, a_b_s, b_w_in, b_conv_w, c_w_in, c_conv_w, c_a_log, c_dt_bias, c_o_norm_w, loss_target, m_mem_norm_w, m_w_mem_kv, m_norm_pre, m_norm_post, m_w_out, m_a_w_in, m_a_ln_w, m_a_ln_b, m_a_w_s, m_a_b_s, m_b_w_in, m_b_conv_w, m_c_w_in, m_c_conv_w, m_c_a_log, m_c_dt_bias, m_c_o_norm_w, v_mem_norm_w, v_w_mem_kv, v_norm_pre, v_norm_post, v_w_out, v_a_w_in, v_a_ln_w, v_a_ln_b, v_a_w_s, v_a_b_s, v_b_w_in, v_b_conv_w, v_c_w_in, v_c_conv_w, v_c_a_log, v_c_dt_bias, v_c_o_norm_w):
    given = dict(x=x, mem=mem, mem_norm_w=mem_norm_w, w_mem_kv=w_mem_kv, norm_pre=norm_pre, norm_post=norm_post, w_out=w_out, a_w_in=a_w_in, a_ln_w=a_ln_w, a_ln_b=a_ln_b, a_w_s=a_w_s, a_b_s=a_b_s, b_w_in=b_w_in, b_conv_w=b_conv_w, c_w_in=c_w_in, c_conv_w=c_conv_w, c_a_log=c_a_log, c_dt_bias=c_dt_bias, c_o_norm_w=c_o_norm_w, loss_target=loss_target, m_mem_norm_w=m_mem_norm_w, m_w_mem_kv=m_w_mem_kv, m_norm_pre=m_norm_pre, m_norm_post=m_norm_post, m_w_out=m_w_out, m_a_w_in=m_a_w_in, m_a_ln_w=m_a_ln_w, m_a_ln_b=m_a_ln_b, m_a_w_s=m_a_w_s, m_a_b_s=m_a_b_s, m_b_w_in=m_b_w_in, m_b_conv_w=m_b_conv_w, m_c_w_in=m_c_w_in, m_c_conv_w=m_c_conv_w, m_c_a_log=m_c_a_log, m_c_dt_bias=m_c_dt_bias, m_c_o_norm_w=m_c_o_norm_w, v_mem_norm_w=v_mem_norm_w, v_w_mem_kv=v_w_mem_kv, v_norm_pre=v_norm_pre, v_norm_post=v_norm_post, v_w_out=v_w_out, v_a_w_in=v_a_w_in, v_a_ln_w=v_a_ln_w, v_a_ln_b=v_a_ln_b, v_a_w_s=v_a_w_s, v_a_b_s=v_a_b_s, v_b_w_in=v_b_w_in, v_b_conv_w=v_b_conv_w, v_c_w_in=v_c_w_in, v_c_conv_w=v_c_conv_w, v_c_a_log=v_c_a_log, v_c_dt_bias=v_c_dt_bias, v_c_o_norm_w=v_c_o_norm_w)
    weights = {n: given[n] for n in TWIN_WEIGHTS}
    shared = {n: given[n] for n in SHARED_INPUTS}
    per_example = {n: given[n] for n in ['x', 'mem']}
    grad_fn = _jax.value_and_grad(_loss, argnums=(0, 1))

    def one_microbatch(ex, loss_target):
        ex = dict(ex)
        diff = ex.pop(TWIN_DIFF_INPUT)
        return grad_fn(weights, diff, {**shared, **ex}, loss_target)

    if N_MICROBATCH == 1:
        loss, (grad_w, grad_x) = one_microbatch(per_example, given["loss_target"])
    else:
        def body(carry, xs):
            loss_sum, grad_sum = carry
            l_k, (gw_k, gx_k) = one_microbatch(xs[0], xs[1])
            with _jax.named_scope("update"):
                return (loss_sum + l_k, _jax.tree.map(_jnp.add, grad_sum, gw_k)), gx_k

        init = (_jnp.zeros((), _jnp.float32), _jax.tree.map(_jnp.zeros_like, weights))
        (loss, grad_w), grad_x = _jax.lax.scan(body, init, (per_example, given["loss_target"]))
    with _jax.named_scope("update"):
        delta_w, new_m, new_v = {}, {}, {}
        for n in TWIN_WEIGHTS:
            delta_w[n], new_m[n], new_v[n] = _adamw(weights[n], grad_w[n], given["m_" + n], given["v_" + n])
    return (loss, grad_x, *[grad_w[n] for n in TWIN_WEIGHTS], *[delta_w[n] for n in TWIN_WEIGHTS],
            *[new_m[n] for n in TWIN_WEIGHTS], *[new_v[n] for n in TWIN_WEIGHTS])
```

```python
import functools

import jax
import jax.numpy as jnp
from jax import lax
from jax.experimental import pallas as pl
from jax.experimental.pallas import tpu as pltpu

F32 = jnp.float32
BF16 = jnp.bfloat16
HI = lax.Precision.HIGHEST
MESH = pl.DeviceIdType.MESH
SDS = jax.ShapeDtypeStruct

D = 1024
D_XA = 512
D_CAT = 1536
N_MEM = 256
HD = 128
DEPTH = 4
EPS = 1e-6
TT = 256
DN_C = 64
DN_TB = 256
HALO = 8
AB_PAD = 128
VMEM_LIMIT = 56 * 1024 * 1024

ADAM_LR, ADAM_B1, ADAM_B2, ADAM_EPS, ADAM_WD, ADAM_STEP = 0.001, 0.9, 0.999, 1e-08, 0.01, 10


def _params(n_grid, vmem=None):
    return pltpu.CompilerParams(dimension_semantics=("arbitrary",) * n_grid, vmem_limit_bytes=vmem)


def _rms(x, w):
    return x * lax.rsqrt(jnp.mean(x * x, axis=-1, keepdims=True) + EPS) * w


def _dot_nn(a, b):
    return jnp.dot(a.astype(BF16), b.astype(BF16), preferred_element_type=F32)


def _dot_nt(a, b):
    return lax.dot_general(a.astype(BF16), b.astype(BF16), (((1,), (1,)), ((), ())), preferred_element_type=F32)


def _dot_tn(a, b):
    return lax.dot_general(a.astype(BF16), b.astype(BF16), (((0,), (0,)), ((), ())), preferred_element_type=F32)


@jax.custom_vjp
def mm(a, b):
    return _dot_nn(a, b)


mm.defvjp(lambda a, b: (_dot_nn(a, b), (a, b)), lambda r, g: (_dot_nt(g, r[1]), _dot_tn(r[0], g)))


@jax.custom_vjp
def mm_nt(a, b):
    return _dot_nt(a, b)


mm_nt.defvjp(lambda a, b: (_dot_nt(a, b), (a, b)), lambda r, g: (_dot_nn(g, r[1]), _dot_tn(g, r[0])))


@jax.custom_vjp
def mm_tn(a, b):
    return _dot_tn(a, b)


mm_tn.defvjp(lambda a, b: (_dot_tn(a, b), (a, b)), lambda r, g: (_dot_nt(r[1], g), _dot_nn(r[0], g)))


def _dot_hi(a, b):
    return jnp.dot(a, b, precision=HI, preferred_element_type=F32)


def _row_spec(width, tile=TT):
    return pl.BlockSpec((tile, width), lambda i: (i, 0))


def _full_spec(shape):
    return pl.BlockSpec(shape, lambda *_: (0,) * len(shape))


def _inproj_fwd(x, nw, wm, wg, name):
    T, M, G = x.shape[0], wm.shape[1], wg.shape[1]

    def body(x_ref, nw_ref, wm_ref, wg_ref, pm_ref, pg_ref, h_ref):
        h = _rms(x_ref[...], nw_ref[...]).astype(BF16)
        h_ref[...] = h
        pm_ref[...] = jnp.dot(h, wm_ref[...], preferred_element_type=F32)
        pg_ref[...] = jnp.dot(h, wg_ref[...], preferred_element_type=F32)

    return pl.pallas_call(
        body, name=name, grid=(T // TT,),
        in_specs=[_row_spec(D), _full_spec((1, D)), _full_spec((D, M)), _full_spec((D, G))],
        out_specs=[_row_spec(M), _row_spec(G), _row_spec(D)],
        out_shape=[SDS((T, M), F32), SDS((T, G), F32), SDS((T, D), BF16)],
        compiler_params=_params(1, VMEM_LIMIT))(x, nw, wm, wg)


def _inproj_bwd(dpm, dpg, x, nw, wm, wg, dxc, name):
    T, M, G = x.shape[0], wm.shape[1], wg.shape[1]

    def body(dpm_ref, dpg_ref, x_ref, nw_ref, wm_ref, wg_ref, dxc_ref, dx_ref, dnw_ref):
        dh = _dot_nt(dpm_ref[...], wm_ref[...]) + _dot_nt(dpg_ref[...], wg_ref[...])
        _, vjp = jax.vjp(_rms, x_ref[...], nw_ref[...])
        dxr, dnw = vjp(dh)
        dx_ref[...] = dxc_ref[...] + dxr

        @pl.when(pl.program_id(0) == 0)
        def _():
            dnw_ref[...] = jnp.zeros_like(dnw_ref)
        dnw_ref[...] += dnw

    return pl.pallas_call(
        body, name=name, grid=(T // TT,),
        in_specs=[_row_spec(M), _row_spec(G), _row_spec(D), _full_spec((1, D)), _full_spec((D, M)),
                  _full_spec((D, G)), _row_spec(D)],
        out_specs=[_row_spec(D), _full_spec((1, D))],
        out_shape=[SDS((T, D), F32), SDS((1, D), F32)],
        compiler_params=_params(1, VMEM_LIMIT))(dpm, dpg, x, nw, wm, wg, dxc)


def _matmul_tn(a, b, name):
    T, K = a.shape
    N = b.shape[1]
    tn = 512 if N % 512 == 0 else (640 if N % 640 == 0 else N)
    tt = min(512, T)

    def body(a_ref, b_ref, o_ref):
        @pl.when(pl.program_id(1) == 0)
        def _():
            o_ref[...] = jnp.zeros_like(o_ref)
        o_ref[...] += _dot_tn(a_ref[...], b_ref[...])

    return pl.pallas_call(
        body, name=name, grid=(N // tn, T // tt),
        in_specs=[pl.BlockSpec((tt, K), lambda j, t: (t, 0)), pl.BlockSpec((tt, tn), lambda j, t: (t, j))],
        out_specs=pl.BlockSpec((K, tn), lambda j, t: (0, j)),
        out_shape=SDS((K, N), F32),
        compiler_params=_params(2, VMEM_LIMIT))(a, b)


def _memkv_fn(mem, w, wkv):
    return mm(_rms(mem, w), wkv)


def _memkv_fwd(mem, w, wkv):
    def body(mem_ref, w_ref, wkv_ref, kv_ref):
        kv_ref[...] = _memkv_fn(mem_ref[...], w_ref[...], wkv_ref[...])

    return pl.pallas_call(body, name="memkv_fwd", out_shape=SDS((N_MEM, 2 * D_XA), F32),
                          compiler_params=_params(0, VMEM_LIMIT))(mem, w, wkv)


def _memkv_bwd(mem, w, wkv, dkv):
    def body(mem_ref, w_ref, wkv_ref, dkv_ref, dw_ref, dwkv_ref):
        _, vjp = jax.vjp(functools.partial(_memkv_fn, mem_ref[...]), w_ref[...], wkv_ref[...].astype(F32))
        dw, dwkv = vjp(dkv_ref[...])
        dw_ref[...] = dw
        dwkv_ref[...] = dwkv

    return pl.pallas_call(body, name="memkv_bwd", out_shape=[SDS((1, D), F32), SDS((D, 2 * D_XA), F32)],
                          compiler_params=_params(0, VMEM_LIMIT))(mem, w, wkv, dkv)


def _attn_gate(ymix, qx, z, *kvs):
    outs = []
    for j in range(4):
        s = mm_nt(qx[:, j * HD:(j + 1) * HD], kvs[j]) * (HD ** -0.5)
        e = jnp.exp(s - lax.stop_gradient(jnp.max(s, axis=-1, keepdims=True)))
        outs.append(mm(e / jnp.sum(e, axis=-1, keepdims=True), kvs[4 + j]))
    return jnp.concatenate([ymix] + outs, axis=1) * jax.nn.silu(z)


def _kv_blocks(kv_ref):
    return [kv_ref[:, j * HD:(j + 1) * HD] for j in range(8)]


def _ag_fwd(ymix, pg, kv, name):
    T = ymix.shape[0]

    def body(ymix_ref, pg_ref, kv_ref, ycat_ref):
        ycat_ref[...] = _attn_gate(ymix_ref[...], pg_ref[:, :D_XA], pg_ref[:, D_XA:], *_kv_blocks(kv_ref)).astype(BF16)

    return pl.pallas_call(
        body, name=name, grid=(T // TT,),
        in_specs=[_row_spec(D), _row_spec(D_XA + D_CAT), _full_spec((N_MEM, 2 * D_XA))],
        out_specs=_row_spec(D_CAT), out_shape=SDS((T, D_CAT), BF16),
        compiler_params=_params(1, VMEM_LIMIT))(ymix, pg, kv)


def _ag_bwd(dycat, ymix, pg, kv, dkv_in, name):
    T = ymix.shape[0]

    def body(dycat_ref, ymix_ref, pg_ref, kv_ref, dkvin_ref, dymix_ref, dpg_ref, dkv_ref):
        _, vjp = jax.vjp(_attn_gate, ymix_ref[...], pg_ref[:, :D_XA], pg_ref[:, D_XA:], *_kv_blocks(kv_ref))
        g = vjp(dycat_ref[...])
        dymix_ref[...] = g[0]
        dpg_ref[:, :D_XA] = g[1]
        dpg_ref[:, D_XA:] = g[2]

        @pl.when(pl.program_id(0) == 0)
        def _():
            dkv_ref[...] = dkvin_ref[...]
        for j in range(8):
            dkv_ref[:, j * HD:(j + 1) * HD] += g[3 + j]

    return pl.pallas_call(
        body, name=name, grid=(T // TT,),
        in_specs=[_row_spec(D_CAT), _row_spec(D), _row_spec(D_XA + D_CAT), _full_spec((N_MEM, 2 * D_XA)),
                  _full_spec((N_MEM, 2 * D_XA))],
        out_specs=[_row_spec(D), _row_spec(D_XA + D_CAT), _full_spec((N_MEM, 2 * D_XA))],
        out_shape=[SDS((T, D), F32), SDS((T, D_XA + D_CAT), F32), SDS((N_MEM, 2 * D_XA), F32)],
        compiler_params=_params(1, VMEM_LIMIT))(dycat, ymix, pg, kv, dkv_in)


def _outproj_fwd(ycat, wo, x, nw, name):
    T = x.shape[0]

    def body(ycat_ref, wo_ref, x_ref, nw_ref, o_ref, xn_ref):
        o = jnp.dot(ycat_ref[...], wo_ref[...], preferred_element_type=F32)
        o_ref[...] = o
        xn_ref[...] = x_ref[...] + _rms(o, nw_ref[...])

    return pl.pallas_call(
        body, name=name, grid=(T // TT,),
        in_specs=[_row_spec(D_CAT), _full_spec((D_CAT, D)), _row_spec(D), _full_spec((1, D))],
        out_specs=[_row_spec(D), _row_spec(D)], out_shape=[SDS((T, D), F32), SDS((T, D), F32)],
        compiler_params=_params(1, VMEM_LIMIT))(ycat, wo, x, nw)


def _outproj_bwd(dxo, o, nw, wo, name):
    T = dxo.shape[0]

    def body(dxo_ref, o_ref, nw_ref, wo_ref, dycat_ref, dobf_ref, dnw_ref):
        _, vjp = jax.vjp(_rms, o_ref[...], nw_ref[...])
        do, dnw = vjp(dxo_ref[...])
        dobf = do.astype(BF16)
        dobf_ref[...] = dobf
        dycat_ref[...] = _dot_nt(dobf, wo_ref[...])

        @pl.when(pl.program_id(0) == 0)
        def _():
            dnw_ref[...] = jnp.zeros_like(dnw_ref)
        dnw_ref[...] += dnw

    return pl.pallas_call(
        body, name=name, grid=(T // TT,),
        in_specs=[_row_spec(D), _row_spec(D), _full_spec((1, D)), _full_spec((D_CAT, D))],
        out_specs=[_row_spec(D_CAT), _row_spec(D), _full_spec((1, D))],
        out_shape=[SDS((T, D_CAT), F32), SDS((T, D), BF16), SDS((1, D), F32)],
        compiler_params=_params(1, VMEM_LIMIT))(dxo, o, nw, wo)


def _loss_head(xl, target):
    T = xl.shape[0]

    def body(x_ref, t_ref, loss_ref, dx_ref):
        err = x_ref[...] - t_ref[...]
        dx_ref[...] = err * (1.0 / D)

        @pl.when(pl.program_id(0) == 0)
        def _():
            loss_ref[...] = jnp.zeros_like(loss_ref)
        part = jnp.sum(jnp.sum(err * err, axis=1, keepdims=True), axis=0, keepdims=True) * (0.5 / D)
        loss_ref[...] += jnp.broadcast_to(part, loss_ref.shape)

    return pl.pallas_call(
        body, name="loss_head", grid=(T // TT,),
        in_specs=[_row_spec(D), _row_spec(D)],
        out_specs=[_full_spec((8, 128)), _row_spec(D)], out_shape=[SDS((8, 128), F32), SDS((T, D), F32)],
        compiler_params=_params(1))(xl, target)


def _gmlp_pre(u, v, lnw, lnb):
    vg = jax.nn.gelu(v)
    xc = vg - jnp.mean(vg, axis=-1, keepdims=True)
    vl = xc * lax.rsqrt(jnp.mean(xc * xc, axis=-1, keepdims=True) + EPS) * lnw + lnb
    return jax.nn.gelu(u), vl


def _tril(n, strict=False):
    r = lax.broadcasted_iota(jnp.int32, (n, n), 0)
    c = lax.broadcasted_iota(jnp.int32, (n, n), 1)
    return (r > c) if strict else (r >= c)


def _gmlp_fwd(pm, lnw, lnb, ws, bs3, name):
    T = pm.shape[0]

    def body(pm_ref, lnw_ref, lnb_ref, ws_ref, bs_ref, y_ref):
        ug, vl = _gmlp_pre(pm_ref[:, :D], pm_ref[:, D:], lnw_ref[...], lnb_ref[...])
        mask = _tril(HD)
        for g in range(8):
            w = jnp.where(mask, ws_ref[g], 0.0)
            for c in range(TT // HD):
                rows, cols = slice(c * HD, (c + 1) * HD), slice(g * HD, (g + 1) * HD)
                y_ref[rows, cols] = ug[rows, cols] * (_dot_nn(w, vl[rows, cols]) + bs_ref[g])

    return pl.pallas_call(
        body, name=name, grid=(T // TT,),
        in_specs=[_row_spec(2 * D), _full_spec((1, D)), _full_spec((1, D)), _full_spec((8, HD, HD)),
                  _full_spec((8, HD, HD))],
        out_specs=_row_spec(D), out_shape=SDS((T, D), F32),
        compiler_params=_params(1, VMEM_LIMIT))(pm, lnw, lnb, ws, bs3)


def _gmlp_bwd(dy, pm, lnw, lnb, ws, bs3, name):
    T = pm.shape[0]
    n_t = T // TT

    def body(dy_ref, pm_ref, lnw_ref, lnb_ref, ws_ref, bs_ref, dpm_ref, dlnw_ref, dlnb_ref, dws_ref, dbs_ref,
             dug_scr, dvl_scr, dbs_scr):
        i = pl.program_id(0)

        @pl.when(i == 0)
        def _():
            dlnw_ref[...] = jnp.zeros_like(dlnw_ref)
            dlnb_ref[...] = jnp.zeros_like(dlnb_ref)
            dws_ref[...] = jnp.zeros_like(dws_ref)
            dbs_scr[...] = jnp.zeros_like(dbs_scr)

        (ug, vl), vjp = jax.vjp(_gmlp_pre, pm_ref[:, :D], pm_ref[:, D:], lnw_ref[...], lnb_ref[...])
        mask = _tril(HD)
        for g in range(8):
            w = jnp.where(mask, ws_ref[g], 0.0)
            dw = jnp.zeros((HD, HD), F32)
            db = jnp.zeros((HD, HD), F32)
            for c in range(TT // HD):
                rows, cols = slice(c * HD, (c + 1) * HD), slice(g * HD, (g + 1) * HD)
                dyb, vlb = dy_ref[rows, cols], vl[rows, cols]
                sp = _dot_nn(w, vlb) + bs_ref[g]
                dsp = dyb * ug[rows, cols]
                dug_scr[rows, cols] = dyb * sp
                dvl_scr[rows, cols] = _dot_tn(w, dsp)
                dw += _dot_nt(dsp, vlb)
                db += dsp
            dws_ref[g] += jnp.where(mask, dw, 0.0)
            dbs_scr[g] += db
        du, dv, dlnw, dlnb = vjp((dug_scr[...], dvl_scr[...]))
        dpm_ref[:, :D] = du
        dpm_ref[:, D:] = dv
        dlnw_ref[...] += dlnw
        dlnb_ref[...] += dlnb

        @pl.when(i == n_t - 1)
        def _():
            for g in range(8):
                dbs_ref[g] = jnp.broadcast_to(jnp.sum(dbs_scr[g], axis=1, keepdims=True), (HD, HD))

    return pl.pallas_call(
        body, name=name, grid=(n_t,),
        in_specs=[_row_spec(D), _row_spec(2 * D), _full_spec((1, D)), _full_spec((1, D)), _full_spec((8, HD, HD)),
                  _full_spec((8, HD, HD))],
        out_specs=[_row_spec(2 * D), _full_spec((1, D)), _full_spec((1, D)), _full_spec((8, HD, HD)),
                   _full_spec((8, HD, HD))],
        out_shape=[SDS((T, 2 * D), F32), SDS((1, D), F32), SDS((1, D), F32), SDS((8, HD, HD), F32),
                   SDS((8, HD, HD), F32)],
        scratch_shapes=[pltpu.VMEM((TT, D), F32), pltpu.VMEM((TT, D), F32), pltpu.VMEM((8, HD, HD), F32)],
        compiler_params=_params(1, VMEM_LIMIT))(dy, pm, lnw, lnb, ws, bs3)


def _prev_spec(width, T):
    return pl.BlockSpec((HALO, width), lambda i: (jnp.maximum(i * (TT // HALO) - 1, 0), 0))


def _next_spec(width, T):
    return pl.BlockSpec((HALO, width), lambda i: (jnp.minimum((i + 1) * (TT // HALO), T // HALO - 1), 0))


def _rows_before(ext, j):
    return ext[HALO:] if j == 0 else pltpu.roll(ext, j, 0)[HALO:]


def _rows_after(ext, j):
    n = ext.shape[0]
    return ext[:n - HALO] if j == 0 else pltpu.roll(ext, n - j, 0)[:n - HALO]


def _conv_apply(ext_s, w):
    K = w.shape[0]
    y = _rows_before(ext_s, K - 1) * w[0:1]
    for k in range(1, K):
        y = y + _rows_before(ext_s, K - 1 - k) * w[k:k + 1]
    return y


def _conv_grads(ext_s, ext_dy, w):
    K = w.shape[0]
    dy = ext_dy[:ext_dy.shape[0] - HALO]
    ds = _rows_after(ext_dy, K - 1) * w[0:1]
    dws = [jnp.sum(dy * _rows_before(ext_s, K - 1), axis=0, keepdims=True)]
    for k in range(1, K):
        ds = ds + _rows_after(ext_dy, K - 1 - k) * w[k:k + 1]
        dws.append(jnp.sum(dy * _rows_before(ext_s, K - 1 - k), axis=0, keepdims=True))
    return ds, jnp.concatenate(dws, axis=0)


def _sconv_fwd(pm, w, name):
    T = pm.shape[0]

    def body(pm_ref, prev_ref, w_ref, y_ref):
        s = pm_ref[:, D:2 * D] * pm_ref[:, 2 * D:]
        sp = jnp.where(pl.program_id(0) > 0, prev_ref[:, D:2 * D] * prev_ref[:, 2 * D:], 0.0)
        y_ref[...] = pm_ref[:, :D] * _conv_apply(jnp.concatenate([sp, s], axis=0), w_ref[...])

    return pl.pallas_call(
        body, name=name, grid=(T // TT,),
        in_specs=[_row_spec(3 * D), _prev_spec(3 * D, T), _full_spec((3, D))],
        out_specs=_row_spec(D), out_shape=SDS((T, D), F32),
        compiler_params=_params(1, VMEM_LIMIT))(pm, pm, w)


def _sconv_bwd(dy, pm, w, name):
    T = pm.shape[0]
    n_t = T // TT

    def body(dy_ref, dyn_ref, pm_ref, prev_ref, next_ref, w_ref, dpm_ref, dw_ref):
        i = pl.program_id(0)
        bg, cg, hv = pm_ref[:, :D], pm_ref[:, D:2 * D], pm_ref[:, 2 * D:]
        sp = jnp.where(i > 0, prev_ref[:, D:2 * D] * prev_ref[:, 2 * D:], 0.0)
        ext_s = jnp.concatenate([sp, cg * hv], axis=0)
        dyv = dy_ref[...]
        dcn = jnp.where(i < n_t - 1, dyn_ref[...] * next_ref[:, :D], 0.0)
        ds, dw = _conv_grads(ext_s, jnp.concatenate([dyv * bg, dcn], axis=0), w_ref[...])
        dpm_ref[:, :D] = dyv * _conv_apply(ext_s, w_ref[...])
        dpm_ref[:, D:2 * D] = ds * hv
        dpm_ref[:, 2 * D:] = ds * cg

        @pl.when(i == 0)
        def _():
            dw_ref[...] = jnp.zeros_like(dw_ref)
        dw_ref[...] += dw

    return pl.pallas_call(
        body, name=name, grid=(n_t,),
        in_specs=[_row_spec(D), _next_spec(D, T), _row_spec(3 * D), _prev_spec(3 * D, T), _next_spec(3 * D, T),
                  _full_spec((3, D))],
        out_specs=[_row_spec(3 * D), _full_spec((3, D))],
        out_shape=[SDS((T, 3 * D), F32), SDS((3, D), F32)],
        compiler_params=_params(1, VMEM_LIMIT))(dy, dy, pm, pm, pm, w)


def _dnconv_fwd(pm, w, name):
    T = pm.shape[0]

    def body(pm_ref, prev_ref, w_ref, c_ref):
        sp = jnp.where(pl.program_id(0) > 0, prev_ref[...], 0.0)
        c_ref[...] = _conv_apply(jnp.concatenate([sp, pm_ref[...]], axis=0), w_ref[...])

    return pl.pallas_call(
        body, name=name, grid=(T // TT,),
        in_specs=[_row_spec(3 * D), _prev_spec(3 * D, T), _full_spec((4, 3 * D))],
        out_specs=_row_spec(3 * D), out_shape=SDS((T, 3 * D), F32),
        compiler_params=_params(1, VMEM_LIMIT))(pm, pm, w)


def _dnconv_bwd(dcq, dck, dcv, dab, pm, w, name):
    T = pm.shape[0]
    n_t = T // TT

    def body(dq_ref, dk_ref, dv_ref, dqn_ref, dkn_ref, dvn_ref, dab_ref, pm_ref, prev_ref, w_ref, dpm_ref, dw_ref):
        i = pl.program_id(0)
        sp = jnp.where(i > 0, prev_ref[...], 0.0)
        ext_s = jnp.concatenate([sp, pm_ref[...]], axis=0)
        own = jnp.concatenate([dq_ref[...], dk_ref[...], dv_ref[...]], axis=1)
        nxt = jnp.where(i < n_t - 1, jnp.concatenate([dqn_ref[...], dkn_ref[...], dvn_ref[...]], axis=1), 0.0)
        ds, dw = _conv_grads(ext_s, jnp.concatenate([own, nxt], axis=0), w_ref[...])
        dpm_ref[:, :3 * D] = ds
        dpm_ref[:, 3 * D:] = dab_ref[...]

        @pl.when(i == 0)
        def _():
            dw_ref[...] = jnp.zeros_like(dw_ref)
        dw_ref[...] += dw

    return pl.pallas_call(
        body, name=name, grid=(n_t,),
        in_specs=[_row_spec(D), _row_spec(D), _row_spec(D), _next_spec(D, T), _next_spec(D, T), _next_spec(D, T),
                  _row_spec(AB_PAD), _row_spec(3 * D), _prev_spec(3 * D, T), _full_spec((4, 3 * D))],
        out_specs=[_row_spec(3 * D + AB_PAD), _full_spec((4, 3 * D))],
        out_shape=[SDS((T, 3 * D + AB_PAD), F32), SDS((4, 3 * D), F32)],
        compiler_params=_params(1, VMEM_LIMIT))(dcq, dck, dcv, dcq, dck, dcv, dab, pm, pm, w)


def _l2n(x):
    return x * lax.rsqrt(jnp.sum(x * x, axis=-1, keepdims=True) + EPS)


def _softplus(x):
    return jnp.maximum(x, 0.0) + jnp.log1p(jnp.exp(-jnp.abs(x)))


def _dn_gates(ab, alog, dtb, h):
    lane = lax.broadcasted_iota(jnp.int32, ab.shape, 1)
    g_all = -jnp.exp(alog) * _softplus(ab + dtb)
    g = jnp.sum(jnp.where(lane == h, g_all, 0.0), axis=1, keepdims=True)
    beta = jnp.sum(jnp.where(lane == 8 + h, jax.nn.sigmoid(ab), 0.0), axis=1, keepdims=True)
    ones = jnp.ones((1, HD), F32)
    return g * ones, beta * ones


def _dn_chunk(cq, ck, cv, gb, bb, S, onw):
    C = DN_C
    q = _l2n(jax.nn.silu(cq)) * (HD ** -0.5)
    k = _l2n(jax.nn.silu(ck))
    v = jax.nn.silu(cv)
    incl, strict = _tril(C), _tril(C, strict=True)
    gcum = _dot_hi(incl.astype(F32), gb)
    gi = gcum[:, :C]
    gj = gcum.T[:C, :]
    decay = jnp.where(incl, jnp.exp(jnp.where(incl, gi - gj, 0.0)), 0.0)
    kb = k * bb
    a_mat = jnp.where(strict, mm_nt(kb, k) * decay, 0.0)
    p = -a_mat
    eye = (lax.broadcasted_iota(jnp.int32, (C, C), 0) == lax.broadcasted_iota(jnp.int32, (C, C), 1)).astype(F32)
    t_mat = eye + p
    for _ in range(5):
        p = _dot_hi(p, p)
        t_mat = t_mat + _dot_hi(t_mat, p)
    eg = jnp.exp(gcum)
    u = mm(t_mat, v * bb)
    w = mm(t_mat, kb * eg)
    qk = mm_nt(q, k) * decay
    glast = gcum[C - 1:C, :]
    v_new = u - mm(w, S)
    o = mm(q * eg, S) + mm(qk, v_new)
    s_new = S * jnp.exp(glast) + mm_tn(k * jnp.exp(glast - gcum), v_new)
    return _rms(o, onw), s_new


def _dn_specs(T, rev):
    nb = T // DN_TB
    blk = (lambda n: nb - 1 - n) if rev else (lambda n: n)
    head = [pl.BlockSpec((DN_TB, HD), functools.partial(lambda n, h, off: (blk(n), off + h), off=8 * s)) for s in range(3)]
    ab = pl.BlockSpec((DN_TB, AB_PAD), lambda n, h: (blk(n), 3 * D // AB_PAD))
    st = pl.BlockSpec((DN_TB // DN_C, None, HD, HD), lambda n, h: (blk(n), h, 0, 0))
    out = pl.BlockSpec((DN_TB, HD), lambda n, h: (blk(n), h))
    row = pl.BlockSpec((1, HD), lambda n, h: (0, 0))
    return nb, head, ab, st, out, row


def _dn_fwd(cpre, pm, alog, dtb, onw, name):
    T = cpre.shape[0]
    nb, head, ab, st, out, row = _dn_specs(T, False)

    def body(cq_ref, ck_ref, cv_ref, ab_ref, alog_ref, dtb_ref, onw_ref, o_ref, st_ref, s_scr):
        n, h = pl.program_id(0), pl.program_id(1)

        @pl.when(n == 0)
        def _():
            s_scr[h] = jnp.zeros((HD, HD), F32)
        gb, bb = _dn_gates(ab_ref[...], alog_ref[...], dtb_ref[...], h)
        S = s_scr[h]
        for c in range(DN_TB // DN_C):
            rows = slice(c * DN_C, (c + 1) * DN_C)
            st_ref[c] = S
            o, S = _dn_chunk(cq_ref[rows, :], ck_ref[rows, :], cv_ref[rows, :], gb[rows], bb[rows], S, onw_ref[...])
            o_ref[rows, :] = o
        s_scr[h] = S

    return pl.pallas_call(
        body, name=name, grid=(nb, 8),
        in_specs=head + [ab, row, row, row], out_specs=[out, st],
        out_shape=[SDS((T, D), F32), SDS((T // DN_C, 8, HD, HD), F32)],
        scratch_shapes=[pltpu.VMEM((8, HD, HD), F32)],
        compiler_params=_params(2, VMEM_LIMIT))(cpre, cpre, cpre, pm, alog, dtb, onw)


def _dn_bwd(do, cpre, pm, st, alog, dtb, onw, name):
    T = cpre.shape[0]
    nb, head, ab, stspec, out, row = _dn_specs(T, True)

    def body(do_ref, cq_ref, ck_ref, cv_ref, ab_ref, st_ref, alog_ref, dtb_ref, onw_ref,
             dcq_ref, dck_ref, dcv_ref, dab_ref, dalog_ref, ddtb_ref, donw_ref, ds_scr):
        n, h = pl.program_id(0), pl.program_id(1)

        @pl.when(n == 0)
        def _():
            ds_scr[h] = jnp.zeros((HD, HD), F32)

        @pl.when((n == 0) & (h == 0))
        def _():
            dalog_ref[...] = jnp.zeros_like(dalog_ref)
            ddtb_ref[...] = jnp.zeros_like(ddtb_ref)
            donw_ref[...] = jnp.zeros_like(donw_ref)

        @pl.when(h == 0)
        def _():
            dab_ref[...] = jnp.zeros_like(dab_ref)

        (gb, bb), gates_vjp = jax.vjp(lambda a, b, c: _dn_gates(a, b, c, h), ab_ref[...], alog_ref[...], dtb_ref[...])
        dS = ds_scr[h]
        n_c = DN_TB // DN_C
        dgs, dbs = [None] * n_c, [None] * n_c
        donw = jnp.zeros((1, HD), F32)
        for c in reversed(range(n_c)):
            rows = slice(c * DN_C, (c + 1) * DN_C)
            _, vjp = jax.vjp(_dn_chunk, cq_ref[rows, :], ck_ref[rows, :], cv_ref[rows, :], gb[rows], bb[rows],
                             st_ref[c], onw_ref[...])
            dcq, dck, dcv, dgs[c], dbs[c], dS, dn = vjp((do_ref[rows, :], dS))
            dcq_ref[rows, :] = dcq
            dck_ref[rows, :] = dck
            dcv_ref[rows, :] = dcv
            donw += dn
        ds_scr[h] = dS
        dab, dalog, ddtb = gates_vjp((jnp.concatenate(dgs, axis=0), jnp.concatenate(dbs, axis=0)))
        dab_ref[...] += dab
        dalog_ref[...] += dalog
        ddtb_ref[...] += ddtb
        donw_ref[...] += donw

    dabspec = pl.BlockSpec((DN_TB, AB_PAD), lambda n, h: (nb - 1 - n, 0))
    return pl.pallas_call(
        body, name=name, grid=(nb, 8),
        in_specs=[out] + head + [ab, stspec, row, row, row],
        out_specs=[out, out, out, dabspec, row, row, row],
        out_shape=[SDS((T, D), F32)] * 3 + [SDS((T, AB_PAD), F32)] + [SDS((1, HD), F32)] * 3,
        scratch_shapes=[pltpu.VMEM((8, HD, HD), F32)],
        compiler_params=_params(2, VMEM_LIMIT))(do, cpre, cpre, cpre, pm, st, alog, dtb, onw)


def _adamw(w, g, m, v, name):
    R, C = w.shape
    tr = 256 if R % 256 == 0 and R > 256 else R
    c1 = 1.0 - ADAM_B1 ** ADAM_STEP
    c2 = 1.0 - ADAM_B2 ** ADAM_STEP

    def body(w_ref, g_ref, m_ref, v_ref, d_ref, nm_ref, nv_ref):
        gv = g_ref[...]
        nm = ADAM_B1 * m_ref[...] + (1.0 - ADAM_B1) * gv
        nv = ADAM_B2 * v_ref[...] + (1.0 - ADAM_B2) * (gv * gv)
        nm_ref[...] = nm
        nv_ref[...] = nv
        d_ref[...] = -ADAM_LR * ((nm / c1) / (jnp.sqrt(nv / c2) + ADAM_EPS) + ADAM_WD * w_ref[...])

    spec = pl.BlockSpec((tr, C), lambda i: (i, 0))
    return pl.pallas_call(
        body, name=name, grid=(R // tr,), in_specs=[spec] * 4, out_specs=[spec] * 3,
        out_shape=[SDS((R, C), F32)] * 3, compiler_params=_params(1, VMEM_LIMIT))(w, g, m, v)


def _local_step(x, mem, target, wts, sm):
    kinds = [i % 3 for i in range(DEPTH)]
    mnw = sm["mem_norm_w"].reshape(1, D)
    kv = _memkv_fwd(mem, mnw, wts["wkv"])
    saved = []
    for i, kind in enumerate(kinds):
        j = i // 3
        npre = sm["norm_pre"][i].reshape(1, D)
        npost = sm["norm_post"][i].reshape(1, D)
        pm, pg, h = _inproj_fwd(x, npre, wts["wm"][i], wts["wg"][i], f"inproj_fwd_{i}")
        extra = None
        if kind == 0:
            bs3 = jnp.broadcast_to(sm["a_b_s"][j][:, :, None], (8, HD, HD))
            ymix = _gmlp_fwd(pm, sm["a_ln_w"][j].reshape(1, D), sm["a_ln_b"][j].reshape(1, D), sm["a_w_s"][j], bs3,
                             f"gmlp_fwd_{i}")
            extra = bs3
        elif kind == 1:
            ymix = _sconv_fwd(pm, sm["b_conv_w"][j], f"sconv_fwd_{i}")
        else:
            cpre = _dnconv_fwd(pm, sm["c_conv_w"][j], f"dnconv_fwd_{i}")
            alog = jnp.pad(sm["c_a_log"][j], (0, HD - 8)).reshape(1, HD)
            dtb = jnp.pad(sm["c_dt_bias"][j], (0, HD - 8)).reshape(1, HD)
            onw = sm["c_o_norm_w"][j].reshape(1, HD)
            ymix, st = _dn_fwd(cpre, pm, alog, dtb, onw, f"dn_fwd_{i}")
            extra = (cpre, st, alog, dtb, onw)
        ycat = _ag_fwd(ymix, pg, kv, f"ag_fwd_{i}")
        o, xn = _outproj_fwd(ycat, wts["wo"][i], x, npost, f"outproj_fwd_{i}")
        saved.append((x, h, pm, pg, ymix, ycat, o, extra))
        x = xn

    loss, dx = _loss_head(x, target)

    g = {"wm": [None] * DEPTH, "wg": [None] * DEPTH, "wo": [None] * DEPTH, "norm_pre": [None] * DEPTH,
         "norm_post": [None] * DEPTH}
    dkv = jnp.zeros((N_MEM, 2 * D_XA), F32)
    for i in reversed(range(DEPTH)):
        kind, j = kinds[i], i // 3
        xi, h, pm, pg, ymix, ycat, o, extra = saved[i]
        npre = sm["norm_pre"][i].reshape(1, D)
        npost = sm["norm_post"][i].reshape(1, D)
        dycat, dobf, g["norm_post"][i] = _outproj_bwd(dx, o, npost, wts["wo"][i], f"outproj_bwd_{i}")
        g["wo"][i] = _matmul_tn(ycat, dobf, f"dwo_{i}")
        dymix, dpg, dkv = _ag_bwd(dycat, ymix, pg, kv, dkv, f"ag_bwd_{i}")
        if kind == 0:
            dpm, dlnw, dlnb, dws, dbs3 = _gmlp_bwd(dymix, pm, sm["a_ln_w"][j].reshape(1, D),
                                                   sm["a_ln_b"][j].reshape(1, D), sm["a_w_s"][j], extra,
                                                   f"gmlp_bwd_{i}")
            g.setdefault("a_ln_w", {})[j] = dlnw.reshape(D)
            g.setdefault("a_ln_b", {})[j] = dlnb.reshape(D)
            g.setdefault("a_w_s", {})[j] = dws
            g.setdefault("a_b_s", {})[j] = dbs3[:, :, 0]
        elif kind == 1:
            dpm, dcw = _sconv_bwd(dymix, pm, sm["b_conv_w"][j], f"sconv_bwd_{i}")
            g.setdefault("b_conv_w", {})[j] = dcw
        else:
            cpre, st, alog, dtb, onw = extra
            dcq, dck, dcv, dab, dalog, ddtb, donw = _dn_bwd(dymix, cpre, pm, st, alog, dtb, onw, f"dn_bwd_{i}")
            dpm, dcw = _dnconv_bwd(dcq, dck, dcv, dab, pm, sm["c_conv_w"][j], f"dnconv_bwd_{i}")
            g.setdefault("c_conv_w", {})[j] = dcw
            g.setdefault("c_a_log", {})[j] = dalog[0, :8]
            g.setdefault("c_dt_bias", {})[j] = ddtb[0, :8]
            g.setdefault("c_o_norm_w", {})[j] = donw[0]
        g["wm"][i] = _matmul_tn(h, dpm, f"dwm_{i}")
        g["wg"][i] = _matmul_tn(h, dpg, f"dwg_{i}")
        dx, g["norm_pre"][i] = _inproj_bwd(dpm, dpg, xi, npre, wts["wm"][i], wts["wg"][i], dx, f"inproj_bwd_{i}")
    g["mem_norm_w"], g["wkv"] = _memkv_bwd(mem, mnw, wts["wkv"], dkv)
    return loss[0, 0], dx, g


ANY = pl.BlockSpec(memory_space=pl.ANY)


def _place():
    return lax.axis_index("x"), lax.axis_index("y"), lax.axis_index("c")


def _gather_chips(shards):
    n = len(shards)

    def body(*refs):
        ins, outs = refs[:n], refs[n:2 * n]
        send_sems, recv_sems, loc_sems = refs[2 * n:]
        x, y, c = _place()
        peers = [(1 - x, y), (x, 1 - y), (1 - x, 1 - y)]

        def remote(k, j, slab):
            px, py = peers[j]
            return pltpu.make_async_remote_copy(
                src_ref=ins[k], dst_ref=outs[k].at[slab], send_sem=send_sems.at[k, j], recv_sem=recv_sems.at[k, j],
                device_id=(px, py, c), device_id_type=MESH)

        local = [pltpu.make_async_copy(ins[k], outs[k].at[2 * x + y], loc_sems.at[k]) for k in range(n)]
        sends = [remote(k, j, 2 * x + y) for k in range(n) for j in range(3)]
        for cp in local + sends:
            cp.start()
        for k in range(n):
            for j, (px, py) in enumerate(peers):
                remote(k, j, 2 * px + py).wait_recv()
        for cp in sends:
            cp.wait_send()
        for cp in local:
            cp.wait()

    return pl.pallas_call(
        body, name="gather_chips", in_specs=[ANY] * n, out_specs=[ANY] * n,
        out_shape=[SDS((4,) + a.shape, a.dtype) for a in shards],
        scratch_shapes=[pltpu.SemaphoreType.DMA((n, 3)), pltpu.SemaphoreType.DMA((n, 3)),
                        pltpu.SemaphoreType.DMA((n,))])(*shards)


def _exchange_all(send):
    def body(send_ref, land_ref, send_sems, recv_sems, loc_sem):
        x, y, c = _place()
        me = 4 * x + 2 * y + c

        def peer(r):
            px = 1 - x if (r >> 2) & 1 else x
            py = 1 - y if (r >> 1) & 1 else y
            pc = 1 - c if r & 1 else c
            return px, py, pc

        def remote(r, slab):
            px, py, pc = peer(r)
            return pltpu.make_async_remote_copy(
                src_ref=send_ref.at[4 * px + 2 * py + pc], dst_ref=land_ref.at[slab], send_sem=send_sems.at[r - 1],
                recv_sem=recv_sems.at[r - 1], device_id=(px, py, pc), device_id_type=MESH)

        local = pltpu.make_async_copy(send_ref.at[me], land_ref.at[me], loc_sem)
        local.start()
        sends = [remote(r, me) for r in range(1, 8)]
        for cp in sends:
            cp.start()
        for r in range(1, 8):
            px, py, pc = peer(r)
            remote(r, 4 * px + 2 * py + pc).wait_recv()
        for cp in sends:
            cp.wait_send()
        local.wait()

    return pl.pallas_call(
        body, name="exchange_all", in_specs=[ANY], out_specs=ANY, out_shape=SDS(send.shape, send.dtype),
        scratch_shapes=[pltpu.SemaphoreType.DMA((7,)), pltpu.SemaphoreType.DMA((7,)), pltpu.SemaphoreType.DMA(())])(send)


def _sum_slabs(land):
    n, P, C = land.shape
    tr = max(t for t in range(8, 257, 8) if P % t == 0)

    def body(l_ref, o_ref):
        acc = l_ref[0]
        for s in range(1, n):
            acc = acc + l_ref[s]
        o_ref[...] = acc

    return pl.pallas_call(
        body, name="sum_slabs", grid=(P // tr,),
        in_specs=[pl.BlockSpec((n, tr, C), lambda i: (0, i, 0))], out_specs=pl.BlockSpec((tr, C), lambda i: (i, 0)),
        out_shape=SDS((P, C), land.dtype), compiler_params=_params(1, VMEM_LIMIT))(land)


def _share_with_sibling(r, rows):
    def body(r_ref, o_ref, send_sem, recv_sem, loc_sem):
        x, y, c = _place()
        mine = r_ref.at[pl.ds(0, rows)]
        local = pltpu.make_async_copy(mine, o_ref.at[c], loc_sem)
        local.start()
        send = pltpu.make_async_remote_copy(src_ref=mine, dst_ref=o_ref.at[c], send_sem=send_sem, recv_sem=recv_sem,
                                            device_id=(x, y, 1 - c), device_id_type=MESH)
        send.start()
        pltpu.make_async_remote_copy(src_ref=mine, dst_ref=o_ref.at[1 - c], send_sem=send_sem, recv_sem=recv_sem,
                                     device_id=(x, y, 1 - c), device_id_type=MESH).wait_recv()
        send.wait_send()
        local.wait()

    return pl.pallas_call(
        body, name="share_with_sibling", in_specs=[ANY], out_specs=ANY, out_shape=SDS((2, rows, r.shape[1]), r.dtype),
        scratch_shapes=[pltpu.SemaphoreType.DMA(()), pltpu.SemaphoreType.DMA(()), pltpu.SemaphoreType.DMA(())])(r)


_SMALL = ["mem_norm_w", "norm_pre", "norm_post", "a_ln_w", "a_ln_b", "a_w_s", "a_b_s", "b_conv_w", "c_conv_w",
          "c_a_log", "c_dt_bias", "c_o_norm_w"]
_SMALL_SHAPES = {"mem_norm_w": (D,), "norm_pre": (4, D), "norm_post": (4, D), "a_ln_w": (2, D), "a_ln_b": (2, D),
                 "a_w_s": (2, 8, HD, HD), "a_b_s": (2, 8, HD), "b_conv_w": (1, 3, D), "c_conv_w": (1, 4, 3 * D),
                 "c_a_log": (1, 8), "c_dt_bias": (1, 8), "c_o_norm_w": (1, HD)}
_SHARDED_SMALL = {"a_ln_w": D // 4, "a_ln_b": D // 4, "b_conv_w": D // 4, "c_conv_w": 3 * D // 4}
_ROWS = [2048, 1280, 1284, 1536, 256]
_BIG_ROWS = sum(_ROWS)
_BIG_PAD = 6416
_SMALL_ROWS = 288


def _size(shape):
    n = 1
    for d in shape:
        n *= d
    return n


def kernel(x, mem, mem_norm_w, w_mem_kv, norm_pre, norm_post, w_out, a_w_in, a_ln_w, a_ln_b, a_w_s, a_b_s, b_w_in, b_conv_w, c_w_in, c_conv_w, c_a_log, c_dt_bias, c_o_norm_w, loss_target, m_mem_norm_w, m_w_mem_kv, m_norm_pre, m_norm_post, m_w_out, m_a_w_in, m_a_ln_w, m_a_ln_b, m_a_w_s, m_a_b_s, m_b_w_in, m_b_conv_w, m_c_w_in, m_c_conv_w, m_c_a_log, m_c_dt_bias, m_c_o_norm_w, v_mem_norm_w, v_w_mem_kv, v_norm_pre, v_norm_post, v_w_out, v_a_w_in, v_a_ln_w, v_a_ln_b, v_a_w_s, v_a_b_s, v_b_w_in, v_b_conv_w, v_c_w_in, v_c_conv_w, v_c_a_log, v_c_dt_bias, v_c_o_norm_w):
    names = ["mem_norm_w", "w_mem_kv", "norm_pre", "norm_post", "w_out", "a_w_in", "a_ln_w", "a_ln_b", "a_w_s", "a_b_s",
             "b_w_in", "b_conv_w", "c_w_in", "c_conv_w", "c_a_log", "c_dt_bias", "c_o_norm_w"]
    w = dict(zip(names, [mem_norm_w, w_mem_kv, norm_pre, norm_post, w_out, a_w_in, a_ln_w, a_ln_b, a_w_s, a_b_s, b_w_in,
                         b_conv_w, c_w_in, c_conv_w, c_a_log, c_dt_bias, c_o_norm_w]))
    m = dict(zip(names, [m_mem_norm_w, m_w_mem_kv, m_norm_pre, m_norm_post, m_w_out, m_a_w_in, m_a_ln_w, m_a_ln_b, m_a_w_s,
                         m_a_b_s, m_b_w_in, m_b_conv_w, m_c_w_in, m_c_conv_w, m_c_a_log, m_c_dt_bias, m_c_o_norm_w]))
    v = dict(zip(names, [v_mem_norm_w, v_w_mem_kv, v_norm_pre, v_norm_post, v_w_out, v_a_w_in, v_a_ln_w, v_a_ln_b, v_a_w_s,
                         v_a_b_s, v_b_w_in, v_b_conv_w, v_c_w_in, v_c_conv_w, v_c_a_log, v_c_dt_bias, v_c_o_norm_w]))
    chip = 2 * lax.axis_index("x") + lax.axis_index("y")

    big = jnp.concatenate([a_w_in.reshape(_ROWS[0], D), b_w_in.reshape(_ROWS[1], D), c_w_in.reshape(_ROWS[2], D),
                           w_out.reshape(_ROWS[3], D), w_mem_kv, jnp.zeros((_BIG_PAD - _BIG_ROWS, D), F32)],
                          axis=0).astype(BF16)
    vec = jnp.concatenate([a_ln_w.reshape(-1), a_ln_b.reshape(-1), b_conv_w.reshape(-1), c_conv_w.reshape(-1)])
    vec = jnp.pad(vec, (0, 8 * D - vec.shape[0])).reshape(8, D)
    gbig, gvec = _gather_chips([big, vec])
    r0, r1, r2, r3, r4 = [sum(_ROWS[:k]) for k in range(5)]
    fa = gbig[:, r0:r1].reshape(4, 2, D, D).transpose(1, 2, 0, 3).reshape(2, D, 4 * D)
    fb = gbig[:, r1:r2].reshape(4, D, 1280).transpose(1, 0, 2).reshape(D, 5120)
    fc = gbig[:, r2:r3].reshape(4, D, 1284).transpose(1, 0, 2).reshape(D, 5136)
    fo = gbig[:, r3:r4].reshape(4, 4, 384, D).transpose(1, 0, 2, 3).reshape(4, D_CAT, D)
    fkv = gbig[:, r4:_BIG_ROWS].reshape(D, 2 * D_XA)
    gv = gvec.reshape(4, 8 * D)
    sm = {"mem_norm_w": mem_norm_w, "norm_pre": norm_pre, "norm_post": norm_post, "a_w_s": a_w_s, "a_b_s": a_b_s,
          "c_a_log": c_a_log, "c_dt_bias": c_dt_bias, "c_o_norm_w": c_o_norm_w,
          "a_ln_w": gv[:, 0:512].reshape(4, 2, 256).transpose(1, 0, 2).reshape(2, D),
          "a_ln_b": gv[:, 512:1024].reshape(4, 2, 256).transpose(1, 0, 2).reshape(2, D),
          "b_conv_w": gv[:, 1024:1792].reshape(4, 1, 3, 256).transpose(1, 2, 0, 3).reshape(1, 3, D),
          "c_conv_w": gv[:, 1792:4864].reshape(4, 1, 4, 768).transpose(1, 2, 0, 3).reshape(1, 4, 3 * D)}
    c_mix = jnp.concatenate([fc[:, :3 * D + 16], jnp.zeros((D, AB_PAD - 16), BF16)], axis=1)
    wts = {"wkv": fkv, "wo": fo,
           "wm": [fa[0][:, :2 * D], fb[:, :3 * D], c_mix, fa[1][:, :2 * D]],
           "wg": [fa[0][:, 2 * D:], fb[:, 3 * D:], fc[:, 3 * D + 16:], fa[1][:, 2 * D:]]}

    loss, dx, g = _local_step(x[0], mem[0], loss_target[0], wts, sm)
    loss = lax.psum(loss, ("x", "y", "c"))

    ga = jnp.stack([jnp.concatenate([g["wm"][i], g["wg"][i]], axis=1) for i in (0, 3)])
    gb = jnp.concatenate([g["wm"][1], g["wg"][1]], axis=1)
    gc = jnp.concatenate([g["wm"][2][:, :3 * D + 16], g["wg"][2]], axis=1)
    go = jnp.stack(g["wo"])
    gbig_all = jnp.concatenate([
        ga.reshape(2, D, 4, D).transpose(2, 0, 1, 3).reshape(4, _ROWS[0], D),
        gb.reshape(D, 4, 1280).transpose(1, 0, 2).reshape(4, _ROWS[1], D),
        gc.reshape(D, 4, 1284).transpose(1, 0, 2).reshape(4, _ROWS[2], D),
        go.reshape(4, 4, 384, D).transpose(1, 0, 2, 3).reshape(4, _ROWS[3], D),
        g["wkv"].reshape(4, _ROWS[4], D),
        jnp.zeros((4, _BIG_PAD - _BIG_ROWS, D), F32)], axis=1).reshape(8, _BIG_PAD // 2, D)
    gs = {"mem_norm_w": g["mem_norm_w"], "norm_pre": jnp.concatenate(g["norm_pre"]),
          "norm_post": jnp.concatenate(g["norm_post"])}
    for n in _SMALL[3:]:
        gs[n] = jnp.stack([g[n][j] for j in sorted(g[n])])
    flat = jnp.concatenate([gs[n].reshape(-1) for n in _SMALL])
    flat = jnp.pad(flat, (0, _SMALL_ROWS * D - flat.shape[0])).reshape(1, _SMALL_ROWS, D)
    send = jnp.concatenate([gbig_all, jnp.broadcast_to(flat, (8, _SMALL_ROWS, D))], axis=1)
    red = _sum_slabs(_exchange_all(send))
    gshard = _share_with_sibling(red, _BIG_PAD // 2).reshape(_BIG_PAD, D)
    grads = {"a_w_in": gshard[r0:r1].reshape(a_w_in.shape), "b_w_in": gshard[r1:r2].reshape(b_w_in.shape),
             "c_w_in": gshard[r2:r3].reshape(c_w_in.shape), "w_out": gshard[r3:r4].reshape(w_out.shape),
             "w_mem_kv": gshard[r4:_BIG_ROWS]}
    flat = red[_BIG_PAD // 2:].reshape(-1)
    off = 0
    for n in _SMALL:
        shape = _SMALL_SHAPES[n]
        full = flat[off:off + _size(shape)].reshape(shape)
        off += _size(shape)
        if n in _SHARDED_SMALL:
            full = lax.dynamic_slice_in_dim(full, chip * _SHARDED_SMALL[n], _SHARDED_SMALL[n], axis=len(shape) - 1)
        grads[n] = full

    delta, new_m, new_v = {}, {}, {}
    for n in names:
        shape = w[n].shape
        view = (1, shape[0]) if len(shape) == 1 else (_size(shape[:-1]), shape[-1])
        d_, m_, v_ = _adamw(w[n].reshape(view), grads[n].reshape(view), m[n].reshape(view), v[n].reshape(view),
                            f"adamw_{n}")
        delta[n], new_m[n], new_v[n] = d_.reshape(shape), m_.reshape(shape), v_.reshape(shape)
    return (loss, dx[None], *[grads[n].reshape(w[n].shape) for n in names], *[delta[n] for n in names],
            *[new_m[n] for n in names], *[new_v[n] for n in names])
```

```python
import functools

import jax
import jax.numpy as jnp
from jax import lax
from jax.experimental import pallas as pl
from jax.experimental.pallas import tpu as pltpu

F32 = jnp.float32
BF16 = jnp.bfloat16
HI = lax.Precision.HIGHEST
MESH = pl.DeviceIdType.MESH
SDS = jax.ShapeDtypeStruct

D = 1024
D_XA = 512
D_CAT = 1536
N_MEM = 256
HD = 128
DEPTH = 4
EPS = 1e-6
TT = 256
DN_C = 64
DN_TB = 256
HALO = 8
AB_PAD = 128
VMEM_LIMIT = 56 * 1024 * 1024

ADAM_LR, ADAM_B1, ADAM_B2, ADAM_EPS, ADAM_WD, ADAM_STEP = 0.001, 0.9, 0.999, 1e-08, 0.01, 10


def _params(n_grid, vmem=None):
    return pltpu.CompilerParams(dimension_semantics=("arbitrary",) * n_grid, vmem_limit_bytes=vmem)


def _rms(x, w):
    return x * lax.rsqrt(jnp.mean(x * x, axis=-1, keepdims=True) + EPS) * w


def _dot_nn(a, b):
    return jnp.dot(a.astype(BF16), b.astype(BF16), preferred_element_type=F32)


def _dot_nt(a, b):
    return lax.dot_general(a.astype(BF16), b.astype(BF16), (((1,), (1,)), ((), ())), preferred_element_type=F32)


def _dot_tn(a, b):
    return lax.dot_general(a.astype(BF16), b.astype(BF16), (((0,), (0,)), ((), ())), preferred_element_type=F32)


@jax.custom_vjp
def mm(a, b):
    return _dot_nn(a, b)


mm.defvjp(lambda a, b: (_dot_nn(a, b), (a, b)), lambda r, g: (_dot_nt(g, r[1]), _dot_tn(r[0], g)))


@jax.custom_vjp
def mm_nt(a, b):
    return _dot_nt(a, b)


mm_nt.defvjp(lambda a, b: (_dot_nt(a, b), (a, b)), lambda r, g: (_dot_nn(g, r[1]), _dot_tn(g, r[0])))


@jax.custom_vjp
def mm_tn(a, b):
    return _dot_tn(a, b)


mm_tn.defvjp(lambda a, b: (_dot_tn(a, b), (a, b)), lambda r, g: (_dot_nt(r[1], g), _dot_nn(r[0], g)))


def _dot_hi(a, b):
    return jnp.dot(a, b, precision=HI, preferred_element_type=F32)


def _row_spec(width, tile=TT):
    return pl.BlockSpec((tile, width), lambda i: (i, 0))


def _full_spec(shape):
    return pl.BlockSpec(shape, lambda *_: (0,) * len(shape))


def _inproj_fwd(x, nw, wm, wg, name):
    T, M, G = x.shape[0], wm.shape[1], wg.shape[1]

    def body(x_ref, nw_ref, wm_ref, wg_ref, pm_ref, pg_ref, h_ref):
        h = _rms(x_ref[...], nw_ref[...]).astype(BF16)
        h_ref[...] = h
        pm_ref[...] = jnp.dot(h, wm_ref[...], preferred_element_type=F32)
        pg_ref[...] = jnp.dot(h, wg_ref[...], preferred_element_type=F32)

    return pl.pallas_call(
        body, name=name, grid=(T // TT,),
        in_specs=[_row_spec(D), _full_spec((1, D)), _full_spec((D, M)), _full_spec((D, G))],
        out_specs=[_row_spec(M), _row_spec(G), _row_spec(D)],
        out_shape=[SDS((T, M), F32), SDS((T, G), F32), SDS((T, D), BF16)],
        compiler_params=_params(1, VMEM_LIMIT))(x, nw, wm, wg)


def _inproj_bwd(dpm, dpg, x, nw, wm, wg, dxc, name):
    T, M, G = x.shape[0], wm.shape[1], wg.shape[1]

    def body(dpm_ref, dpg_ref, x_ref, nw_ref, wm_ref, wg_ref, dxc_ref, dx_ref, dnw_ref):
        dh = _dot_nt(dpm_ref[...], wm_ref[...]) + _dot_nt(dpg_ref[...], wg_ref[...])
        _, vjp = jax.vjp(_rms, x_ref[...], nw_ref[...])
        dxr, dnw = vjp(dh)
        dx_ref[...] = dxc_ref[...] + dxr

        @pl.when(pl.program_id(0) == 0)
        def _():
            dnw_ref[...] = jnp.zeros_like(dnw_ref)
        dnw_ref[...] += dnw

    return pl.pallas_call(
        body, name=name, grid=(T // TT,),
        in_specs=[_row_spec(M), _row_spec(G), _row_spec(D), _full_spec((1, D)), _full_spec((D, M)),
                  _full_spec((D, G)), _row_spec(D)],
        out_specs=[_row_spec(D), _full_spec((1, D))],
        out_shape=[SDS((T, D), F32), SDS((1, D), F32)],
        compiler_params=_params(1, VMEM_LIMIT))(dpm, dpg, x, nw, wm, wg, dxc)


def _matmul_tn(a, b, name):
    T, K = a.shape
    N = b.shape[1]
    tn = 512 if N % 512 == 0 else (640 if N % 640 == 0 else N)
    tt = min(512, T)

    def body(a_ref, b_ref, o_ref):
        @pl.when(pl.program_id(1) == 0)
        def _():
            o_ref[...] = jnp.zeros_like(o_ref)
        o_ref[...] += _dot_tn(a_ref[...], b_ref[...])

    return pl.pallas_call(
        body, name=name, grid=(N // tn, T // tt),
        in_specs=[pl.BlockSpec((tt, K), lambda j, t: (t, 0)), pl.BlockSpec((tt, tn), lambda j, t: (t, j))],
        out_specs=pl.BlockSpec((K, tn), lambda j, t: (0, j)),
        out_shape=SDS((K, N), F32),
        compiler_params=_params(2, VMEM_LIMIT))(a, b)


def _memkv_fn(mem, w, wkv):
    return mm(_rms(mem, w), wkv)


def _memkv_fwd(mem, w, wkv):
    def body(mem_ref, w_ref, wkv_ref, kv_ref):
        kv_ref[...] = _memkv_fn(mem_ref[...], w_ref[...], wkv_ref[...])

    return pl.pallas_call(body, name="memkv_fwd", out_shape=SDS((N_MEM, 2 * D_XA), F32),
                          compiler_params=_params(0, VMEM_LIMIT))(mem, w, wkv)


def _memkv_bwd(mem, w, wkv, dkv):
    def body(mem_ref, w_ref, wkv_ref, dkv_ref, dw_ref, dwkv_ref):
        _, vjp = jax.vjp(functools.partial(_memkv_fn, mem_ref[...]), w_ref[...], wkv_ref[...].astype(F32))
        dw, dwkv = vjp(dkv_ref[...])
        dw_ref[...] = dw
        dwkv_ref[...] = dwkv

    return pl.pallas_call(body, name="memkv_bwd", out_shape=[SDS((1, D), F32), SDS((D, 2 * D_XA), F32)],
                          compiler_params=_params(0, VMEM_LIMIT))(mem, w, wkv, dkv)


def _attn_gate(ymix, qx, z, *kvs):
    outs = []
    for j in range(4):
        s = mm_nt(qx[:, j * HD:(j + 1) * HD], kvs[j]) * (HD ** -0.5)
        e = jnp.exp(s - lax.stop_gradient(jnp.max(s, axis=-1, keepdims=True)))
        outs.append(mm(e / jnp.sum(e, axis=-1, keepdims=True), kvs[4 + j]))
    return jnp.concatenate([ymix] + outs, axis=1) * jax.nn.silu(z)


def _kv_blocks(kv_ref):
    return [kv_ref[:, j * HD:(j + 1) * HD] for j in range(8)]


def _ag_fwd(ymix, pg, kv, name):
    T = ymix.shape[0]

    def body(ymix_ref, pg_ref, kv_ref, ycat_ref):
        ycat_ref[...] = _attn_gate(ymix_ref[...], pg_ref[:, :D_XA], pg_ref[:, D_XA:], *_kv_blocks(kv_ref)).astype(BF16)

    return pl.pallas_call(
        body, name=name, grid=(T // TT,),
        in_specs=[_row_spec(D), _row_spec(D_XA + D_CAT), _full_spec((N_MEM, 2 * D_XA))],
        out_specs=_row_spec(D_CAT), out_shape=SDS((T, D_CAT), BF16),
        compiler_params=_params(1, VMEM_LIMIT))(ymix, pg, kv)


def _ag_bwd(dycat, ymix, pg, kv, dkv_in, name):
    T = ymix.shape[0]

    def body(dycat_ref, ymix_ref, pg_ref, kv_ref, dkvin_ref, dymix_ref, dpg_ref, dkv_ref):
        _, vjp = jax.vjp(_attn_gate, ymix_ref[...], pg_ref[:, :D_XA], pg_ref[:, D_XA:], *_kv_blocks(kv_ref))
        g = vjp(dycat_ref[...])
        dymix_ref[...] = g[0]
        dpg_ref[:, :D_XA] = g[1]
        dpg_ref[:, D_XA:] = g[2]

        @pl.when(pl.program_id(0) == 0)
        def _():
            dkv_ref[...] = dkvin_ref[...]
        for j in range(8):
            dkv_ref[:, j * HD:(j + 1) * HD] += g[3 + j]

    return pl.pallas_call(
        body, name=name, grid=(T // TT,),
        in_specs=[_row_spec(D_CAT), _row_spec(D), _row_spec(D_XA + D_CAT), _full_spec((N_MEM, 2 * D_XA)),
                  _full_spec((N_MEM, 2 * D_XA))],
        out_specs=[_row_spec(D), _row_spec(D_XA + D_CAT), _full_spec((N_MEM, 2 * D_XA))],
        out_shape=[SDS((T, D), F32), SDS((T, D_XA + D_CAT), F32), SDS((N_MEM, 2 * D_XA), F32)],
        compiler_params=_params(1, VMEM_LIMIT))(dycat, ymix, pg, kv, dkv_in)


def _outproj_fwd(ycat, wo, x, nw, name):
    T = x.shape[0]

    def body(ycat_ref, wo_ref, x_ref, nw_ref, o_ref, xn_ref):
        o = jnp.dot(ycat_ref[...], wo_ref[...], preferred_element_type=F32)
        o_ref[...] = o
        xn_ref[...] = x_ref[...] + _rms(o, nw_ref[...])

    return pl.pallas_call(
        body, name=name, grid=(T // TT,),
        in_specs=[_row_spec(D_CAT), _full_spec((D_CAT, D)), _row_spec(D), _full_spec((1, D))],
        out_specs=[_row_spec(D), _row_spec(D)], out_shape=[SDS((T, D), F32), SDS((T, D), F32)],
        compiler_params=_params(1, VMEM_LIMIT))(ycat, wo, x, nw)


def _outproj_bwd(dxo, o, nw, wo, name):
    T = dxo.shape[0]

    def body(dxo_ref, o_ref, nw_ref, wo_ref, dycat_ref, dobf_ref, dnw_ref):
        _, vjp = jax.vjp(_rms, o_ref[...], nw_ref[...])
        do, dnw = vjp(dxo_ref[...])
        dobf = do.astype(BF16)
        dobf_ref[...] = dobf
        dycat_ref[...] = _dot_nt(dobf, wo_ref[...])

        @pl.when(pl.program_id(0) == 0)
        def _():
            dnw_ref[...] = jnp.zeros_like(dnw_ref)
        dnw_ref[...] += dnw

    return pl.pallas_call(
        body, name=name, grid=(T // TT,),
        in_specs=[_row_spec(D), _row_spec(D), _full_spec((1, D)), _full_spec((D_CAT, D))],
        out_specs=[_row_spec(D_CAT), _row_spec(D), _full_spec((1, D))],
        out_shape=[SDS((T, D_CAT), F32), SDS((T, D), BF16), SDS((1, D), F32)],
        compiler_params=_params(1, VMEM_LIMIT))(dxo, o, nw, wo)


def _loss_head(xl, target):
    T = xl.shape[0]

    def body(x_ref, t_ref, loss_ref, dx_ref):
        err = x_ref[...] - t_ref[...]
        dx_ref[...] = err * (1.0 / D)

        @pl.when(pl.program_id(0) == 0)
        def _():
            loss_ref[...] = jnp.zeros_like(loss_ref)
        part = jnp.sum(jnp.sum(err * err, axis=1, keepdims=True), axis=0, keepdims=True) * (0.5 / D)
        loss_ref[...] += jnp.broadcast_to(part, loss_ref.shape)

    return pl.pallas_call(
        body, name="loss_head", grid=(T // TT,),
        in_specs=[_row_spec(D), _row_spec(D)],
        out_specs=[_full_spec((8, 128)), _row_spec(D)], out_shape=[SDS((8, 128), F32), SDS((T, D), F32)],
        compiler_params=_params(1))(xl, target)


def _gmlp_pre(u, v, lnw, lnb):
    vg = jax.nn.gelu(v)
    xc = vg - jnp.mean(vg, axis=-1, keepdims=True)
    vl = xc * lax.rsqrt(jnp.mean(xc * xc, axis=-1, keepdims=True) + EPS) * lnw + lnb
    return jax.nn.gelu(u), vl


def _tril(n, strict=False):
    r = lax.broadcasted_iota(jnp.int32, (n, n), 0)
    c = lax.broadcasted_iota(jnp.int32, (n, n), 1)
    return (r > c) if strict else (r >= c)


def _gmlp_fwd(pm, lnw, lnb, ws, bs3, name):
    T = pm.shape[0]

    def body(pm_ref, lnw_ref, lnb_ref, ws_ref, bs_ref, y_ref):
        ug, vl = _gmlp_pre(pm_ref[:, :D], pm_ref[:, D:], lnw_ref[...], lnb_ref[...])
        mask = _tril(HD)
        for g in range(8):
            w = jnp.where(mask, ws_ref[g], 0.0)
            for c in range(TT // HD):
                rows, cols = slice(c * HD, (c + 1) * HD), slice(g * HD, (g + 1) * HD)
                y_ref[rows, cols] = ug[rows, cols] * (_dot_nn(w, vl[rows, cols]) + bs_ref[g])

    return pl.pallas_call(
        body, name=name, grid=(T // TT,),
        in_specs=[_row_spec(2 * D), _full_spec((1, D)), _full_spec((1, D)), _full_spec((8, HD, HD)),
                  _full_spec((8, HD, HD))],
        out_specs=_row_spec(D), out_shape=SDS((T, D), F32),
        compiler_params=_params(1, VMEM_LIMIT))(pm, lnw, lnb, ws, bs3)


def _gmlp_bwd(dy, pm, lnw, lnb, ws, bs3, name):
    T = pm.shape[0]
    n_t = T // TT

    def body(dy_ref, pm_ref, lnw_ref, lnb_ref, ws_ref, bs_ref, dpm_ref, dlnw_ref, dlnb_ref, dws_ref, dbs_ref,
             dug_scr, dvl_scr, dbs_scr):
        i = pl.program_id(0)

        @pl.when(i == 0)
        def _():
            dlnw_ref[...] = jnp.zeros_like(dlnw_ref)
            dlnb_ref[...] = jnp.zeros_like(dlnb_ref)
            dws_ref[...] = jnp.zeros_like(dws_ref)
            dbs_scr[...] = jnp.zeros_like(dbs_scr)

        (ug, vl), vjp = jax.vjp(_gmlp_pre, pm_ref[:, :D], pm_ref[:, D:], lnw_ref[...], lnb_ref[...])
        mask = _tril(HD)
        for g in range(8):
            w = jnp.where(mask, ws_ref[g], 0.0)
            dw = jnp.zeros((HD, HD), F32)
            db = jnp.zeros((HD, HD), F32)
            for c in range(TT // HD):
                rows, cols = slice(c * HD, (c + 1) * HD), slice(g * HD, (g + 1) * HD)
                dyb, vlb = dy_ref[rows, cols], vl[rows, cols]
                sp = _dot_nn(w, vlb) + bs_ref[g]
                dsp = dyb * ug[rows, cols]
                dug_scr[rows, cols] = dyb * sp
                dvl_scr[rows, cols] = _dot_tn(w, dsp)
                dw += _dot_nt(dsp, vlb)
                db += dsp
            dws_ref[g] += jnp.where(mask, dw, 0.0)
            dbs_scr[g] += db
        du, dv, dlnw, dlnb = vjp((dug_scr[...], dvl_scr[...]))
        dpm_ref[:, :D] = du
        dpm_ref[:, D:] = dv
        dlnw_ref[...] += dlnw
        dlnb_ref[...] += dlnb

        @pl.when(i == n_t - 1)
        def _():
            for g in range(8):
                dbs_ref[g] = jnp.broadcast_to(jnp.sum(dbs_scr[g], axis=1, keepdims=True), (HD, HD))

    return pl.pallas_call(
        body, name=name, grid=(n_t,),
        in_specs=[_row_spec(D), _row_spec(2 * D), _full_spec((1, D)), _full_spec((1, D)), _full_spec((8, HD, HD)),
                  _full_spec((8, HD, HD))],
        out_specs=[_row_spec(2 * D), _full_spec((1, D)), _full_spec((1, D)), _full_spec((8, HD, HD)),
                   _full_spec((8, HD, HD))],
        out_shape=[SDS((T, 2 * D), F32), SDS((1, D), F32), SDS((1, D), F32), SDS((8, HD, HD), F32),
                   SDS((8, HD, HD), F32)],
        scratch_shapes=[pltpu.VMEM((TT, D), F32), pltpu.VMEM((TT, D), F32), pltpu.VMEM((8, HD, HD), F32)],
        compiler_params=_params(1, VMEM_LIMIT))(dy, pm, lnw, lnb, ws, bs3)


def _prev_spec(width, T):
    return pl.BlockSpec((HALO, width), lambda i: (jnp.maximum(i * (TT // HALO) - 1, 0), 0))


def _next_spec(width, T):
    return pl.BlockSpec((HALO, width), lambda i: (jnp.minimum((i + 1) * (TT // HALO), T // HALO - 1), 0))


def _rows_before(ext, j):
    return ext[HALO:] if j == 0 else pltpu.roll(ext, j, 0)[HALO:]


def _rows_after(ext, j):
    n = ext.shape[0]
    return ext[:n - HALO] if j == 0 else pltpu.roll(ext, n - j, 0)[:n - HALO]


def _conv_apply(ext_s, w):
    K = w.shape[0]
    y = _rows_before(ext_s, K - 1) * w[0:1]
    for k in range(1, K):
        y = y + _rows_before(ext_s, K - 1 - k) * w[k:k + 1]
    return y


def _conv_grads(ext_s, ext_dy, w):
    K = w.shape[0]
    dy = ext_dy[:ext_dy.shape[0] - HALO]
    ds = _rows_after(ext_dy, K - 1) * w[0:1]
    dws = [jnp.sum(dy * _rows_before(ext_s, K - 1), axis=0, keepdims=True)]
    for k in range(1, K):
        ds = ds + _rows_after(ext_dy, K - 1 - k) * w[k:k + 1]
        dws.append(jnp.sum(dy * _rows_before(ext_s, K - 1 - k), axis=0, keepdims=True))
    return ds, jnp.concatenate(dws, axis=0)


def _sconv_fwd(pm, w, name):
    T = pm.shape[0]

    def body(pm_ref, prev_ref, w_ref, y_ref):
        s = pm_ref[:, D:2 * D] * pm_ref[:, 2 * D:]
        sp = jnp.where(pl.program_id(0) > 0, prev_ref[:, D:2 * D] * prev_ref[:, 2 * D:], 0.0)
        y_ref[...] = pm_ref[:, :D] * _conv_apply(jnp.concatenate([sp, s], axis=0), w_ref[...])

    return pl.pallas_call(
        body, name=name, grid=(T // TT,),
        in_specs=[_row_spec(3 * D), _prev_spec(3 * D, T), _full_spec((3, D))],
        out_specs=_row_spec(D), out_shape=SDS((T, D), F32),
        compiler_params=_params(1, VMEM_LIMIT))(pm, pm, w)


def _sconv_bwd(dy, pm, w, name):
    T = pm.shape[0]
    n_t = T // TT

    def body(dy_ref, dyn_ref, pm_ref, prev_ref, next_ref, w_ref, dpm_ref, dw_ref):
        i = pl.program_id(0)
        bg, cg, hv = pm_ref[:, :D], pm_ref[:, D:2 * D], pm_ref[:, 2 * D:]
        sp = jnp.where(i > 0, prev_ref[:, D:2 * D] * prev_ref[:, 2 * D:], 0.0)
        ext_s = jnp.concatenate([sp, cg * hv], axis=0)
        dyv = dy_ref[...]
        dcn = jnp.where(i < n_t - 1, dyn_ref[...] * next_ref[:, :D], 0.0)
        ds, dw = _conv_grads(ext_s, jnp.concatenate([dyv * bg, dcn], axis=0), w_ref[...])
        dpm_ref[:, :D] = dyv * _conv_apply(ext_s, w_ref[...])
        dpm_ref[:, D:2 * D] = ds * hv
        dpm_ref[:, 2 * D:] = ds * cg

        @pl.when(i == 0)
        def _():
            dw_ref[...] = jnp.zeros_like(dw_ref)
        dw_ref[...] += dw

    return pl.pallas_call(
        body, name=name, grid=(n_t,),
        in_specs=[_row_spec(D), _next_spec(D, T), _row_spec(3 * D), _prev_spec(3 * D, T), _next_spec(3 * D, T),
                  _full_spec((3, D))],
        out_specs=[_row_spec(3 * D), _full_spec((3, D))],
        out_shape=[SDS((T, 3 * D), F32), SDS((3, D), F32)],
        compiler_params=_params(1, VMEM_LIMIT))(dy, dy, pm, pm, pm, w)


def _dnconv_fwd(pm, w, name):
    T = pm.shape[0]

    def body(pm_ref, prev_ref, w_ref, c_ref):
        sp = jnp.where(pl.program_id(0) > 0, prev_ref[...], 0.0)
        c_ref[...] = _conv_apply(jnp.concatenate([sp, pm_ref[...]], axis=0), w_ref[...])

    return pl.pallas_call(
        body, name=name, grid=(T // TT,),
        in_specs=[_row_spec(3 * D), _prev_spec(3 * D, T), _full_spec((4, 3 * D))],
        out_specs=_row_spec(3 * D), out_shape=SDS((T, 3 * D), F32),
        compiler_params=_params(1, VMEM_LIMIT))(pm, pm, w)


def _dnconv_bwd(dcq, dck, dcv, dab, pm, w, name):
    T = pm.shape[0]
    n_t = T // TT

    def body(dq_ref, dk_ref, dv_ref, dqn_ref, dkn_ref, dvn_ref, dab_ref, pm_ref, prev_ref, w_ref, dpm_ref, dw_ref):
        i = pl.program_id(0)
        sp = jnp.where(i > 0, prev_ref[...], 0.0)
        ext_s = jnp.concatenate([sp, pm_ref[...]], axis=0)
        own = jnp.concatenate([dq_ref[...], dk_ref[...], dv_ref[...]], axis=1)
        nxt = jnp.where(i < n_t - 1, jnp.concatenate([dqn_ref[...], dkn_ref[...], dvn_ref[...]], axis=1), 0.0)
        ds, dw = _conv_grads(ext_s, jnp.concatenate([own, nxt], axis=0), w_ref[...])
        dpm_ref[:, :3 * D] = ds
        dpm_ref[:, 3 * D:] = dab_ref[...]

        @pl.when(i == 0)
        def _():
            dw_ref[...] = jnp.zeros_like(dw_ref)
        dw_ref[...] += dw

    return pl.pallas_call(
        body, name=name, grid=(n_t,),
        in_specs=[_row_spec(D), _row_spec(D), _row_spec(D), _next_spec(D, T), _next_spec(D, T), _next_spec(D, T),
                  _row_spec(AB_PAD), _row_spec(3 * D), _prev_spec(3 * D, T), _full_spec((4, 3 * D))],
        out_specs=[_row_spec(3 * D + AB_PAD), _full_spec((4, 3 * D))],
        out_shape=[SDS((T, 3 * D + AB_PAD), F32), SDS((4, 3 * D), F32)],
        compiler_params=_params(1, VMEM_LIMIT))(dcq, dck, dcv, dcq, dck, dcv, dab, pm, pm, w)


def _l2n(x):
    return x * lax.rsqrt(jnp.sum(x * x, axis=-1, keepdims=True) + EPS)


def _softplus(x):
    return jnp.maximum(x, 0.0) + jnp.log1p(jnp.exp(-jnp.abs(x)))


def _dn_gates(ab, alog, dtb, h):
    lane = lax.broadcasted_iota(jnp.int32, ab.shape, 1)
    g_all = -jnp.exp(alog) * _softplus(ab + dtb)
    g = jnp.sum(jnp.where(lane == h, g_all, 0.0), axis=1, keepdims=True)
    beta = jnp.sum(jnp.where(lane == 8 + h, jax.nn.sigmoid(ab), 0.0), axis=1, keepdims=True)
    ones = jnp.ones((1, HD), F32)
    return g * ones, beta * ones


def _dn_chunk(cq, ck, cv, gb, bb, S, onw):
    C = DN_C
    q = _l2n(jax.nn.silu(cq)) * (HD ** -0.5)
    k = _l2n(jax.nn.silu(ck))
    v = jax.nn.silu(cv)
    incl, strict = _tril(C), _tril(C, strict=True)
    gcum = _dot_hi(incl.astype(F32), gb)
    gi = gcum[:, :C]
    gj = gcum.T[:C, :]
    decay = jnp.where(incl, jnp.exp(jnp.where(incl, gi - gj, 0.0)), 0.0)
    kb = k * bb
    a_mat = jnp.where(strict, mm_nt(kb, k) * decay, 0.0)
    p = -a_mat
    eye = (lax.broadcasted_iota(jnp.int32, (C, C), 0) == lax.broadcasted_iota(jnp.int32, (C, C), 1)).astype(F32)
    t_mat = eye + p
    for _ in range(5):
        p = _dot_hi(p, p)
        t_mat = t_mat + _dot_hi(t_mat, p)
    eg = jnp.exp(gcum)
    u = mm(t_mat, v * bb)
    w = mm(t_mat, kb * eg)
    qk = mm_nt(q, k) * decay
    glast = gcum[C - 1:C, :]
    v_new = u - mm(w, S)
    o = mm(q * eg, S) + mm(qk, v_new)
    s_new = S * jnp.exp(glast) + mm_tn(k * jnp.exp(glast - gcum), v_new)
    return _rms(o, onw), s_new


def _dn_specs(T, rev):
    nb = T // DN_TB
    blk = (lambda n: nb - 1 - n) if rev else (lambda n: n)
    head = [pl.BlockSpec((DN_TB, HD), functools.partial(lambda n, h, off: (blk(n), off + h), off=8 * s)) for s in range(3)]
    ab = pl.BlockSpec((DN_TB, AB_PAD), lambda n, h: (blk(n), 3 * D // AB_PAD))
    st = pl.BlockSpec((DN_TB // DN_C, None, HD, HD), lambda n, h: (blk(n), h, 0, 0))
    out = pl.BlockSpec((DN_TB, HD), lambda n, h: (blk(n), h))
    row = pl.BlockSpec((1, HD), lambda n, h: (0, 0))
    return nb, head, ab, st, out, row


def _dn_fwd(cpre, pm, alog, dtb, onw, name):
    T = cpre.shape[0]
    nb, head, ab, st, out, row = _dn_specs(T, False)

    def body(cq_ref, ck_ref, cv_ref, ab_ref, alog_ref, dtb_ref, onw_ref, o_ref, st_ref, s_scr):
        n, h = pl.program_id(0), pl.program_id(1)

        @pl.when(n == 0)
        def _():
            s_scr[h] = jnp.zeros((HD, HD), F32)
        gb, bb = _dn_gates(ab_ref[...], alog_ref[...], dtb_ref[...], h)
        S = s_scr[h]
        for c in range(DN_TB // DN_C):
            rows = slice(c * DN_C, (c + 1) * DN_C)
            st_ref[c] = S
            o, S = _dn_chunk(cq_ref[rows, :], ck_ref[rows, :], cv_ref[rows, :], gb[rows], bb[rows], S, onw_ref[...])
            o_ref[rows, :] = o
        s_scr[h] = S

    return pl.pallas_call(
        body, name=name, grid=(nb, 8),
        in_specs=head + [ab, row, row, row], out_specs=[out, st],
        out_shape=[SDS((T, D), F32), SDS((T // DN_C, 8, HD, HD), F32)],
        scratch_shapes=[pltpu.VMEM((8, HD, HD), F32)],
        compiler_params=_params(2, VMEM_LIMIT))(cpre, cpre, cpre, pm, alog, dtb, onw)


def _dn_bwd(do, cpre, pm, st, alog, dtb, onw, name):
    T = cpre.shape[0]
    nb, head, ab, stspec, out, row = _dn_specs(T, True)

    def body(do_ref, cq_ref, ck_ref, cv_ref, ab_ref, st_ref, alog_ref, dtb_ref, onw_ref,
             dcq_ref, dck_ref, dcv_ref, dab_ref, dalog_ref, ddtb_ref, donw_ref, ds_scr):
        n, h = pl.program_id(0), pl.program_id(1)

        @pl.when(n == 0)
        def _():
            ds_scr[h] = jnp.zeros((HD, HD), F32)

        @pl.when((n == 0) & (h == 0))
        def _():
            dalog_ref[...] = jnp.zeros_like(dalog_ref)
            ddtb_ref[...] = jnp.zeros_like(ddtb_ref)
            donw_ref[...] = jnp.zeros_like(donw_ref)

        @pl.when(h == 0)
        def _():
            dab_ref[...] = jnp.zeros_like(dab_ref)

        (gb, bb), gates_vjp = jax.vjp(lambda a, b, c: _dn_gates(a, b, c, h), ab_ref[...], alog_ref[...], dtb_ref[...])
        dS = ds_scr[h]
        n_c = DN_TB // DN_C
        dgs, dbs = [None] * n_c, [None] * n_c
        donw = jnp.zeros((1, HD), F32)
        for c in reversed(range(n_c)):
            rows = slice(c * DN_C, (c + 1) * DN_C)
            _, vjp = jax.vjp(_dn_chunk, cq_ref[rows, :], ck_ref[rows, :], cv_ref[rows, :], gb[rows], bb[rows],
                             st_ref[c], onw_ref[...])
            dcq, dck, dcv, dgs[c], dbs[c], dS, dn = vjp((do_ref[rows, :], dS))
            dcq_ref[rows, :] = dcq
            dck_ref[rows, :] = dck
            dcv_ref[rows, :] = dcv
            donw += dn
        ds_scr[h] = dS
        dab, dalog, ddtb = gates_vjp((jnp.concatenate(dgs, axis=0), jnp.concatenate(dbs, axis=0)))
        dab_ref[...] += dab
        dalog_ref[...] += dalog
        ddtb_ref[...] += ddtb
        donw_ref[...] += donw

    dabspec = pl.BlockSpec((DN_TB, AB_PAD), lambda n, h: (nb - 1 - n, 0))
    return pl.pallas_call(
        body, name=name, grid=(nb, 8),
        in_specs=[out] + head + [ab, stspec, row, row, row],
        out_specs=[out, out, out, dabspec, row, row, row],
        out_shape=[SDS((T, D), F32)] * 3 + [SDS((T, AB_PAD), F32)] + [SDS((1, HD), F32)] * 3,
        scratch_shapes=[pltpu.VMEM((8, HD, HD), F32)],
        compiler_params=_params(2, VMEM_LIMIT))(do, cpre, cpre, cpre, pm, st, alog, dtb, onw)


_BNN = (((2,), (1,)), ((0,), (0,)))
_BNT = (((2,), (2,)), ((0,), (0,)))
_BTN = (((1,), (1,)), ((0,), (0,)))


def _bdot(a, b, dims):
    return lax.dot_general(a.astype(BF16), b.astype(BF16), dims, preferred_element_type=F32)


def _bdot3(a, b, dims):
    ah, bh = a.astype(BF16), b.astype(BF16)
    al, bl = (a - ah.astype(F32)).astype(BF16), (b - bh.astype(F32)).astype(BF16)
    d = functools.partial(lax.dot_general, dimension_numbers=dims, preferred_element_type=F32)
    return d(ah, bh) + (d(ah, bl) + d(al, bh))


def _bdot_hi(a, b, dims):
    return lax.dot_general(a, b, dims, precision=HI, preferred_element_type=F32)


def _batched_matmuls(dot):
    @jax.custom_vjp
    def nn(a, b):
        return dot(a, b, _BNN)

    @jax.custom_vjp
    def nt(a, b):
        return dot(a, b, _BNT)

    @jax.custom_vjp
    def tn(a, b):
        return dot(a, b, _BTN)

    nn.defvjp(lambda a, b: (dot(a, b, _BNN), (a, b)), lambda r, g: (dot(g, r[1], _BNT), dot(r[0], g, _BTN)))
    nt.defvjp(lambda a, b: (dot(a, b, _BNT), (a, b)), lambda r, g: (dot(g, r[1], _BNN), dot(g, r[0], _BTN)))
    tn.defvjp(lambda a, b: (dot(a, b, _BTN), (a, b)), lambda r, g: (dot(r[1], g, _BNT), dot(r[0], g, _BNN)))
    return nn, nt, tn


bmm, bmm_nt, bmm_tn = _batched_matmuls(_bdot)
bmm3, _, _ = _batched_matmuls(_bdot3)
bmm_hi, bmm_hi_nt, _ = _batched_matmuls(_bdot_hi)

DN_NCH = DN_TB // DN_C
DN_NH = 4


def _dn_prep(cq, ck, cv, gb, bb):
    B, C = cq.shape[0], DN_C
    q = _l2n(jax.nn.silu(cq)) * (HD ** -0.5)
    k = _l2n(jax.nn.silu(ck))
    v = jax.nn.silu(cv)
    r = lax.broadcasted_iota(jnp.int32, (B, C, C), 1)
    c = lax.broadcasted_iota(jnp.int32, (B, C, C), 2)
    incl, strict = r >= c, r > c
    gcum = bmm_hi(incl.astype(F32), gb)
    lane0 = lax.broadcasted_iota(jnp.int32, (B, C, HD), 2) == 0
    gj = bmm_hi_nt(jnp.ones((B, C, HD), F32), jnp.where(lane0, gcum, 0.0))
    decay = jnp.where(incl, jnp.exp(jnp.where(incl, gcum[:, :, :C] - gj, 0.0)), 0.0)
    kb = k * bb
    p = -jnp.where(strict, bmm_nt(kb, k) * decay, 0.0)
    t_mat = (r == c).astype(F32) + p
    for _ in range(5):
        p = bmm3(p, p)
        t_mat = t_mat + bmm3(t_mat, p)
    eg = jnp.exp(gcum)
    glast = gcum[:, C - 1:C, :]
    return (bmm(t_mat, v * bb), bmm(t_mat, kb * eg), bmm_nt(q, k) * decay, q * eg, k * jnp.exp(glast - gcum),
            jnp.exp(glast))


def _dn_scan_step(u, w, qk, qd, kd, egl, S, onw):
    v_new = u - bmm(w, S)
    o = bmm(qd, S) + bmm(qk, v_new)
    return _rms(o, onw), S * egl + bmm_tn(kd, v_new)


def _head_gates(ab, alog, dtb, first_head, n_heads):
    gs, bs = [], []
    for i in range(n_heads):
        g, b = _dn_gates(ab, alog, dtb, first_head + i)
        gs.append(g.reshape(DN_NCH, DN_C, HD))
        bs.append(b.reshape(DN_NCH, DN_C, HD))
    return jnp.concatenate(gs, axis=0), jnp.concatenate(bs, axis=0)


def _to_batch(ref, n_heads):
    return jnp.concatenate([ref[:, i * HD:(i + 1) * HD].astype(F32).reshape(DN_NCH, DN_C, HD) for i in range(n_heads)],
                           axis=0)


def _from_batch(ref, val, n_heads):
    for i in range(n_heads):
        ref[:, i * HD:(i + 1) * HD] = val[i * DN_NCH:(i + 1) * DN_NCH].reshape(DN_TB, HD).astype(ref.dtype)


def _prep_specs(T, rev):
    nb = T // DN_TB
    blk = (lambda n: nb - 1 - n) if rev else (lambda n: n)
    ng = 8 // DN_NH
    head = [pl.BlockSpec((DN_TB, DN_NH * HD), functools.partial(lambda n, h, off: (blk(n), off + h), off=ng * s))
            for s in range(3)]
    ab = pl.BlockSpec((DN_TB, AB_PAD), lambda n, h: (blk(n), 3 * D // AB_PAD))
    row = pl.BlockSpec((1, HD), lambda n, h: (0, 0))
    wide = pl.BlockSpec((DN_TB, DN_NH * HD), lambda n, h: (blk(n), h))
    qk = pl.BlockSpec((DN_NCH, DN_NH, DN_C, DN_C), lambda n, h: (blk(n), h, 0, 0))
    eg = pl.BlockSpec((DN_NCH, DN_NH, 1, HD), lambda n, h: (blk(n), h, 0, 0))
    return nb, ng, head, ab, row, wide, qk, eg


def _dn_prep_fwd(cpre, pm, alog, dtb, name):
    T = cpre.shape[0]
    nb, ng, head, ab, row, wide, qks, egs = _prep_specs(T, False)

    def body(cq_ref, ck_ref, cv_ref, ab_ref, alog_ref, dtb_ref, u_ref, w_ref, qk_ref, qd_ref, kd_ref, e_ref):
        gb, bb = _head_gates(ab_ref[...], alog_ref[...], dtb_ref[...], pl.program_id(1) * DN_NH, DN_NH)
        u, w, qk, qd, kd, egl = _dn_prep(_to_batch(cq_ref, DN_NH), _to_batch(ck_ref, DN_NH), _to_batch(cv_ref, DN_NH),
                                         gb, bb)
        _from_batch(u_ref, u, DN_NH)
        _from_batch(w_ref, w, DN_NH)
        _from_batch(qd_ref, qd, DN_NH)
        _from_batch(kd_ref, kd, DN_NH)
        for i in range(DN_NH):
            qk_ref[:, i] = qk[i * DN_NCH:(i + 1) * DN_NCH].astype(BF16)
            e_ref[:, i] = egl[i * DN_NCH:(i + 1) * DN_NCH]

    return pl.pallas_call(
        body, name=name, grid=(nb, ng), in_specs=head + [ab, row, row],
        out_specs=[wide, wide, qks, wide, wide, egs],
        out_shape=[SDS((T, D), F32), SDS((T, D), BF16), SDS((T // DN_C, 8, DN_C, DN_C), BF16), SDS((T, D), BF16),
                   SDS((T, D), BF16), SDS((T // DN_C, 8, 1, HD), F32)],
        compiler_params=_params(2, VMEM_LIMIT))(cpre, cpre, cpre, pm, alog, dtb)


def _dn_prep_bwd(du, dw, dqk, dqd, dkd, degl, cpre, pm, alog, dtb, name):
    T = cpre.shape[0]
    nb, ng, head, ab, row, wide, qks, egs = _prep_specs(T, True)

    def body(du_ref, dw_ref, dqk_ref, dqd_ref, dkd_ref, de_ref, cq_ref, ck_ref, cv_ref, ab_ref, alog_ref, dtb_ref,
             dcq_ref, dck_ref, dcv_ref, dab_ref, dalog_ref, ddtb_ref):
        n, h = pl.program_id(0), pl.program_id(1)

        @pl.when((n == 0) & (h == 0))
        def _():
            dalog_ref[...] = jnp.zeros_like(dalog_ref)
            ddtb_ref[...] = jnp.zeros_like(ddtb_ref)

        @pl.when(h == 0)
        def _():
            dab_ref[...] = jnp.zeros_like(dab_ref)

        def fwd(cq, ck, cv, ab_v, alog_v, dtb_v):
            gb, bb = _head_gates(ab_v, alog_v, dtb_v, h * DN_NH, DN_NH)
            return _dn_prep(cq, ck, cv, gb, bb)

        _, vjp = jax.vjp(fwd, _to_batch(cq_ref, DN_NH), _to_batch(ck_ref, DN_NH), _to_batch(cv_ref, DN_NH), ab_ref[...],
                         alog_ref[...], dtb_ref[...])
        cot = (_to_batch(du_ref, DN_NH), _to_batch(dw_ref, DN_NH),
               jnp.concatenate([dqk_ref[:, i] for i in range(DN_NH)], axis=0), _to_batch(dqd_ref, DN_NH),
               _to_batch(dkd_ref, DN_NH), jnp.concatenate([de_ref[:, i] for i in range(DN_NH)], axis=0))
        dcq, dck, dcv, dab, dalog, ddtb = vjp(cot)
        _from_batch(dcq_ref, dcq, DN_NH)
        _from_batch(dck_ref, dck, DN_NH)
        _from_batch(dcv_ref, dcv, DN_NH)
        dab_ref[...] += dab
        dalog_ref[...] += dalog
        ddtb_ref[...] += ddtb

    dabspec = pl.BlockSpec((DN_TB, AB_PAD), lambda n, h: (nb - 1 - n, 0))
    return pl.pallas_call(
        body, name=name, grid=(nb, ng),
        in_specs=[wide, wide, qks, wide, wide, egs] + head + [ab, row, row],
        out_specs=[wide, wide, wide, dabspec, row, row],
        out_shape=[SDS((T, D), F32)] * 3 + [SDS((T, AB_PAD), F32)] + [SDS((1, HD), F32)] * 2,
        compiler_params=_params(2, VMEM_LIMIT))(du, dw, dqk, dqd, dkd, degl, cpre, cpre, cpre, pm, alog, dtb)


def _scan_specs(T, rev):
    nb = T // DN_TB
    blk = (lambda n: nb - 1 - n) if rev else (lambda n: n)
    wide = pl.BlockSpec((DN_TB, D), lambda n: (blk(n), 0))
    qk = pl.BlockSpec((DN_NCH, 8, DN_C, DN_C), lambda n: (blk(n), 0, 0, 0))
    eg = pl.BlockSpec((DN_NCH, 8, 1, HD), lambda n: (blk(n), 0, 0, 0))
    st = pl.BlockSpec((DN_NCH, 8, HD, HD), lambda n: (blk(n), 0, 0, 0))
    row = pl.BlockSpec((1, HD), lambda n: (0, 0))
    return nb, wide, qk, eg, st, row


def _heads_of(ref, rows):
    return jnp.concatenate([ref[rows, h * HD:(h + 1) * HD].astype(F32)[None] for h in range(8)], axis=0)


def _dn_scan_fwd(u, w, qk, qd, kd, egl, onw, name):
    T = u.shape[0]
    nb, wide, qks, egs, sts, row = _scan_specs(T, False)

    def body(u_ref, w_ref, qk_ref, qd_ref, kd_ref, e_ref, onw_ref, o_ref, st_ref, s_scr):
        @pl.when(pl.program_id(0) == 0)
        def _():
            s_scr[...] = jnp.zeros_like(s_scr)
        S = s_scr[...]
        for c in range(DN_NCH):
            rows = slice(c * DN_C, (c + 1) * DN_C)
            st_ref[c] = S
            o, S = _dn_scan_step(_heads_of(u_ref, rows), _heads_of(w_ref, rows), qk_ref[c].astype(F32),
                                 _heads_of(qd_ref, rows), _heads_of(kd_ref, rows), e_ref[c], S, onw_ref[...])
            for h in range(8):
                o_ref[rows, h * HD:(h + 1) * HD] = o[h]
        s_scr[...] = S

    return pl.pallas_call(
        body, name=name, grid=(nb,), in_specs=[wide, wide, qks, wide, wide, egs, row], out_specs=[wide, sts],
        out_shape=[SDS((T, D), F32), SDS((T // DN_C, 8, HD, HD), F32)],
        scratch_shapes=[pltpu.VMEM((8, HD, HD), F32)],
        compiler_params=_params(1, VMEM_LIMIT))(u, w, qk, qd, kd, egl, onw)


def _dn_scan_bwd(do, u, w, qk, qd, kd, egl, st, onw, name):
    T = u.shape[0]
    nb, wide, qks, egs, sts, row = _scan_specs(T, True)

    def body(do_ref, u_ref, w_ref, qk_ref, qd_ref, kd_ref, e_ref, st_ref, onw_ref,
             du_ref, dw_ref, dqk_ref, dqd_ref, dkd_ref, de_ref, donw_ref, ds_scr):
        @pl.when(pl.program_id(0) == 0)
        def _():
            ds_scr[...] = jnp.zeros_like(ds_scr)
            donw_ref[...] = jnp.zeros_like(donw_ref)
        dS = ds_scr[...]
        donw = jnp.zeros((1, HD), F32)
        for c in reversed(range(DN_NCH)):
            rows = slice(c * DN_C, (c + 1) * DN_C)
            _, vjp = jax.vjp(_dn_scan_step, _heads_of(u_ref, rows), _heads_of(w_ref, rows), qk_ref[c].astype(F32),
                             _heads_of(qd_ref, rows), _heads_of(kd_ref, rows), e_ref[c], st_ref[c], onw_ref[...])
            du, dw, dqk, dqd, dkd, de, dS, dn = vjp((_heads_of(do_ref, rows), dS))
            for h in range(8):
                cols = slice(h * HD, (h + 1) * HD)
                du_ref[rows, cols] = du[h]
                dw_ref[rows, cols] = dw[h]
                dqd_ref[rows, cols] = dqd[h]
                dkd_ref[rows, cols] = dkd[h]
            dqk_ref[c] = dqk
            de_ref[c] = de
            donw += dn
        ds_scr[...] = dS
        donw_ref[...] += donw

    return pl.pallas_call(
        body, name=name, grid=(nb,), in_specs=[wide, wide, wide, qks, wide, wide, egs, sts, row],
        out_specs=[wide, wide, qks, wide, wide, egs, row],
        out_shape=[SDS((T, D), F32), SDS((T, D), F32), SDS((T // DN_C, 8, DN_C, DN_C), F32), SDS((T, D), F32),
                   SDS((T, D), F32), SDS((T // DN_C, 8, 1, HD), F32), SDS((1, HD), F32)],
        scratch_shapes=[pltpu.VMEM((8, HD, HD), F32)],
        compiler_params=_params(1, VMEM_LIMIT))(do, u, w, qk, qd, kd, egl, st, onw)


def _adamw(w, g, m, v, name):
    R, C = w.shape
    tr = 256 if R % 256 == 0 and R > 256 else R
    c1 = 1.0 - ADAM_B1 ** ADAM_STEP
    c2 = 1.0 - ADAM_B2 ** ADAM_STEP

    def body(w_ref, g_ref, m_ref, v_ref, d_ref, nm_ref, nv_ref):
        gv = g_ref[...]
        nm = ADAM_B1 * m_ref[...] + (1.0 - ADAM_B1) * gv
        nv = ADAM_B2 * v_ref[...] + (1.0 - ADAM_B2) * (gv * gv)
        nm_ref[...] = nm
        nv_ref[...] = nv
        d_ref[...] = -ADAM_LR * ((nm / c1) / (jnp.sqrt(nv / c2) + ADAM_EPS) + ADAM_WD * w_ref[...])

    spec = pl.BlockSpec((tr, C), lambda i: (i, 0))
    return pl.pallas_call(
        body, name=name, grid=(R // tr,), in_specs=[spec] * 4, out_specs=[spec] * 3,
        out_shape=[SDS((R, C), F32)] * 3, compiler_params=_params(1, VMEM_LIMIT))(w, g, m, v)


def _local_step(x, mem, target, wts, sm):
    kinds = [i % 3 for i in range(DEPTH)]
    mnw = sm["mem_norm_w"].reshape(1, D)
    kv = _memkv_fwd(mem, mnw, wts["wkv"])
    saved = []
    for i, kind in enumerate(kinds):
        j = i // 3
        npre = sm["norm_pre"][i].reshape(1, D)
        npost = sm["norm_post"][i].reshape(1, D)
        pm, pg, h = _inproj_fwd(x, npre, wts["wm"][i], wts["wg"][i], f"inproj_fwd_{i}")
        extra = None
        if kind == 0:
            bs3 = jnp.broadcast_to(sm["a_b_s"][j][:, :, None], (8, HD, HD))
            ymix = _gmlp_fwd(pm, sm["a_ln_w"][j].reshape(1, D), sm["a_ln_b"][j].reshape(1, D), sm["a_w_s"][j], bs3,
                             f"gmlp_fwd_{i}")
            extra = bs3
        elif kind == 1:
            ymix = _sconv_fwd(pm, sm["b_conv_w"][j], f"sconv_fwd_{i}")
        else:
            cpre = _dnconv_fwd(pm, sm["c_conv_w"][j], f"dnconv_fwd_{i}")
            alog = jnp.pad(sm["c_a_log"][j], (0, HD - 8)).reshape(1, HD)
            dtb = jnp.pad(sm["c_dt_bias"][j], (0, HD - 8)).reshape(1, HD)
            onw = sm["c_o_norm_w"][j].reshape(1, HD)
            prep = _dn_prep_fwd(cpre, pm, alog, dtb, f"dn_prep_fwd_{i}")
            ymix, st = _dn_scan_fwd(*prep, onw, f"dn_scan_fwd_{i}")
            extra = (cpre, prep, st, alog, dtb, onw)
        ycat = _ag_fwd(ymix, pg, kv, f"ag_fwd_{i}")
        o, xn = _outproj_fwd(ycat, wts["wo"][i], x, npost, f"outproj_fwd_{i}")
        saved.append((x, h, pm, pg, ymix, ycat, o, extra))
        x = xn

    loss, dx = _loss_head(x, target)

    g = {"wm": [None] * DEPTH, "wg": [None] * DEPTH, "wo": [None] * DEPTH, "norm_pre": [None] * DEPTH,
         "norm_post": [None] * DEPTH}
    dkv = jnp.zeros((N_MEM, 2 * D_XA), F32)
    for i in reversed(range(DEPTH)):
        kind, j = kinds[i], i // 3
        xi, h, pm, pg, ymix, ycat, o, extra = saved[i]
        npre = sm["norm_pre"][i].reshape(1, D)
        npost = sm["norm_post"][i].reshape(1, D)
        dycat, dobf, g["norm_post"][i] = _outproj_bwd(dx, o, npost, wts["wo"][i], f"outproj_bwd_{i}")
        g["wo"][i] = _matmul_tn(ycat, dobf, f"dwo_{i}")
        dymix, dpg, dkv = _ag_bwd(dycat, ymix, pg, kv, dkv, f"ag_bwd_{i}")
        if kind == 0:
            dpm, dlnw, dlnb, dws, dbs3 = _gmlp_bwd(dymix, pm, sm["a_ln_w"][j].reshape(1, D),
                                                   sm["a_ln_b"][j].reshape(1, D), sm["a_w_s"][j], extra,
                                                   f"gmlp_bwd_{i}")
            g.setdefault("a_ln_w", {})[j] = dlnw.reshape(D)
            g.setdefault("a_ln_b", {})[j] = dlnb.reshape(D)
            g.setdefault("a_w_s", {})[j] = dws
            g.setdefault("a_b_s", {})[j] = dbs3[:, :, 0]
        elif kind == 1:
            dpm, dcw = _sconv_bwd(dymix, pm, sm["b_conv_w"][j], f"sconv_bwd_{i}")
            g.setdefault("b_conv_w", {})[j] = dcw
        else:
            cpre, prep, st, alog, dtb, onw = extra
            *dprep, donw = _dn_scan_bwd(dymix, *prep, st, onw, f"dn_scan_bwd_{i}")
            dcq, dck, dcv, dab, dalog, ddtb = _dn_prep_bwd(*dprep, cpre, pm, alog, dtb, f"dn_prep_bwd_{i}")
            dpm, dcw = _dnconv_bwd(dcq, dck, dcv, dab, pm, sm["c_conv_w"][j], f"dnconv_bwd_{i}")
            g.setdefault("c_conv_w", {})[j] = dcw
            g.setdefault("c_a_log", {})[j] = dalog[0, :8]
            g.setdefault("c_dt_bias", {})[j] = ddtb[0, :8]
            g.setdefault("c_o_norm_w", {})[j] = donw[0]
        g["wm"][i] = _matmul_tn(h, dpm, f"dwm_{i}")
        g["wg"][i] = _matmul_tn(h, dpg, f"dwg_{i}")
        dx, g["norm_pre"][i] = _inproj_bwd(dpm, dpg, xi, npre, wts["wm"][i], wts["wg"][i], dx, f"inproj_bwd_{i}")
    g["mem_norm_w"], g["wkv"] = _memkv_bwd(mem, mnw, wts["wkv"], dkv)
    return loss[0, 0], dx, g


ANY = pl.BlockSpec(memory_space=pl.ANY)


def _place():
    return lax.axis_index("x"), lax.axis_index("y"), lax.axis_index("c")


def _gather_chips(shards):
    n = len(shards)

    def body(*refs):
        ins, outs = refs[:n], refs[n:2 * n]
        send_sems, recv_sems, loc_sems = refs[2 * n:]
        x, y, c = _place()
        peers = [(1 - x, y), (x, 1 - y), (1 - x, 1 - y)]

        def remote(k, j, slab):
            px, py = peers[j]
            return pltpu.make_async_remote_copy(
                src_ref=ins[k], dst_ref=outs[k].at[slab], send_sem=send_sems.at[k, j], recv_sem=recv_sems.at[k, j],
                device_id=(px, py, c), device_id_type=MESH)

        local = [pltpu.make_async_copy(ins[k], outs[k].at[2 * x + y], loc_sems.at[k]) for k in range(n)]
        sends = [remote(k, j, 2 * x + y) for k in range(n) for j in range(3)]
        for cp in local + sends:
            cp.start()
        for k in range(n):
            for j, (px, py) in enumerate(peers):
                remote(k, j, 2 * px + py).wait_recv()
        for cp in sends:
            cp.wait_send()
        for cp in local:
            cp.wait()

    return pl.pallas_call(
        body, name="gather_chips", in_specs=[ANY] * n, out_specs=[ANY] * n,
        out_shape=[SDS((4,) + a.shape, a.dtype) for a in shards],
        scratch_shapes=[pltpu.SemaphoreType.DMA((n, 3)), pltpu.SemaphoreType.DMA((n, 3)),
                        pltpu.SemaphoreType.DMA((n,))])(*shards)


def _exchange_all(send):
    def body(send_ref, land_ref, send_sems, recv_sems, loc_sem):
        x, y, c = _place()
        me = 4 * x + 2 * y + c

        def peer(r):
            px = 1 - x if (r >> 2) & 1 else x
            py = 1 - y if (r >> 1) & 1 else y
            pc = 1 - c if r & 1 else c
            return px, py, pc

        def remote(r, slab):
            px, py, pc = peer(r)
            return pltpu.make_async_remote_copy(
                src_ref=send_ref.at[4 * px + 2 * py + pc], dst_ref=land_ref.at[slab], send_sem=send_sems.at[r - 1],
                recv_sem=recv_sems.at[r - 1], device_id=(px, py, pc), device_id_type=MESH)

        local = pltpu.make_async_copy(send_ref.at[me], land_ref.at[me], loc_sem)
        local.start()
        sends = [remote(r, me) for r in range(1, 8)]
        for cp in sends:
            cp.start()
        for r in range(1, 8):
            px, py, pc = peer(r)
            remote(r, 4 * px + 2 * py + pc).wait_recv()
        for cp in sends:
            cp.wait_send()
        local.wait()

    return pl.pallas_call(
        body, name="exchange_all", in_specs=[ANY], out_specs=ANY, out_shape=SDS(send.shape, send.dtype),
        scratch_shapes=[pltpu.SemaphoreType.DMA((7,)), pltpu.SemaphoreType.DMA((7,)), pltpu.SemaphoreType.DMA(())])(send)


def _sum_slabs(land):
    n, P, C = land.shape
    tr = max(t for t in range(8, 257, 8) if P % t == 0)

    def body(l_ref, o_ref):
        acc = l_ref[0]
        for s in range(1, n):
            acc = acc + l_ref[s]
        o_ref[...] = acc

    return pl.pallas_call(
        body, name="sum_slabs", grid=(P // tr,),
        in_specs=[pl.BlockSpec((n, tr, C), lambda i: (0, i, 0))], out_specs=pl.BlockSpec((tr, C), lambda i: (i, 0)),
        out_shape=SDS((P, C), land.dtype), compiler_params=_params(1, VMEM_LIMIT))(land)


def _share_with_sibling(r, rows):
    def body(r_ref, o_ref, send_sem, recv_sem, loc_sem):
        x, y, c = _place()
        mine = r_ref.at[pl.ds(0, rows)]
        local = pltpu.make_async_copy(mine, o_ref.at[c], loc_sem)
        local.start()
        send = pltpu.make_async_remote_copy(src_ref=mine, dst_ref=o_ref.at[c], send_sem=send_sem, recv_sem=recv_sem,
                                            device_id=(x, y, 1 - c), device_id_type=MESH)
        send.start()
        pltpu.make_async_remote_copy(src_ref=mine, dst_ref=o_ref.at[1 - c], send_sem=send_sem, recv_sem=recv_sem,
                                     device_id=(x, y, 1 - c), device_id_type=MESH).wait_recv()
        send.wait_send()
        local.wait()

    return pl.pallas_call(
        body, name="share_with_sibling", in_specs=[ANY], out_specs=ANY, out_shape=SDS((2, rows, r.shape[1]), r.dtype),
        scratch_shapes=[pltpu.SemaphoreType.DMA(()), pltpu.SemaphoreType.DMA(()), pltpu.SemaphoreType.DMA(())])(r)


_SMALL = ["mem_norm_w", "norm_pre", "norm_post", "a_ln_w", "a_ln_b", "a_w_s", "a_b_s", "b_conv_w", "c_conv_w",
          "c_a_log", "c_dt_bias", "c_o_norm_w"]
_SMALL_SHAPES = {"mem_norm_w": (D,), "norm_pre": (4, D), "norm_post": (4, D), "a_ln_w": (2, D), "a_ln_b": (2, D),
                 "a_w_s": (2, 8, HD, HD), "a_b_s": (2, 8, HD), "b_conv_w": (1, 3, D), "c_conv_w": (1, 4, 3 * D),
                 "c_a_log": (1, 8), "c_dt_bias": (1, 8), "c_o_norm_w": (1, HD)}
_SHARDED_SMALL = {"a_ln_w": D // 4, "a_ln_b": D // 4, "b_conv_w": D // 4, "c_conv_w": 3 * D // 4}
_ROWS = [2048, 1280, 1284, 1536, 256]
_BIG_ROWS = sum(_ROWS)
_BIG_PAD = 6416
_SMALL_ROWS = 288


def _size(shape):
    n = 1
    for d in shape:
        n *= d
    return n


def kernel(x, mem, mem_norm_w, w_mem_kv, norm_pre, norm_post, w_out, a_w_in, a_ln_w, a_ln_b, a_w_s, a_b_s, b_w_in, b_conv_w, c_w_in, c_conv_w, c_a_log, c_dt_bias, c_o_norm_w, loss_target, m_mem_norm_w, m_w_mem_kv, m_norm_pre, m_norm_post, m_w_out, m_a_w_in, m_a_ln_w, m_a_ln_b, m_a_w_s, m_a_b_s, m_b_w_in, m_b_conv_w, m_c_w_in, m_c_conv_w, m_c_a_log, m_c_dt_bias, m_c_o_norm_w, v_mem_norm_w, v_w_mem_kv, v_norm_pre, v_norm_post, v_w_out, v_a_w_in, v_a_ln_w, v_a_ln_b, v_a_w_s, v_a_b_s, v_b_w_in, v_b_conv_w, v_c_w_in, v_c_conv_w, v_c_a_log, v_c_dt_bias, v_c_o_norm_w):
    names = ["mem_norm_w", "w_mem_kv", "norm_pre", "norm_post", "w_out", "a_w_in", "a_ln_w", "a_ln_b", "a_w_s", "a_b_s",
             "b_w_in", "b_conv_w", "c_w_in", "c_conv_w", "c_a_log", "c_dt_bias", "c_o_norm_w"]
    w = dict(zip(names, [mem_norm_w, w_mem_kv, norm_pre, norm_post, w_out, a_w_in, a_ln_w, a_ln_b, a_w_s, a_b_s, b_w_in,
                         b_conv_w, c_w_in, c_conv_w, c_a_log, c_dt_bias, c_o_norm_w]))
    m = dict(zip(names, [m_mem_norm_w, m_w_mem_kv, m_norm_pre, m_norm_post, m_w_out, m_a_w_in, m_a_ln_w, m_a_ln_b, m_a_w_s,
                         m_a_b_s, m_b_w_in, m_b_conv_w, m_c_w_in, m_c_conv_w, m_c_a_log, m_c_dt_bias, m_c_o_norm_w]))
    v = dict(zip(names, [v_mem_norm_w, v_w_mem_kv, v_norm_pre, v_norm_post, v_w_out, v_a_w_in, v_a_ln_w, v_a_ln_b, v_a_w_s,
                         v_a_b_s, v_b_w_in, v_b_conv_w, v_c_w_in, v_c_conv_w, v_c_a_log, v_c_dt_bias, v_c_o_norm_w]))
    chip = 2 * lax.axis_index("x") + lax.axis_index("y")

    big = jnp.concatenate([a_w_in.reshape(_ROWS[0], D), b_w_in.reshape(_ROWS[1], D), c_w_in.reshape(_ROWS[2], D),
                           w_out.reshape(_ROWS[3], D), w_mem_kv, jnp.zeros((_BIG_PAD - _BIG_ROWS, D), F32)],
                          axis=0).astype(BF16)
    vec = jnp.concatenate([a_ln_w.reshape(-1), a_ln_b.reshape(-1), b_conv_w.reshape(-1), c_conv_w.reshape(-1)])
    vec = jnp.pad(vec, (0, 8 * D - vec.shape[0])).reshape(8, D)
    gbig, gvec = _gather_chips([big, vec])
    r0, r1, r2, r3, r4 = [sum(_ROWS[:k]) for k in range(5)]
    fa = gbig[:, r0:r1].reshape(4, 2, D, D).transpose(1, 2, 0, 3).reshape(2, D, 4 * D)
    fb = gbig[:, r1:r2].reshape(4, D, 1280).transpose(1, 0, 2).reshape(D, 5120)
    fc = gbig[:, r2:r3].reshape(4, D, 1284).transpose(1, 0, 2).reshape(D, 5136)
    fo = gbig[:, r3:r4].reshape(4, 4, 384, D).transpose(1, 0, 2, 3).reshape(4, D_CAT, D)
    fkv = gbig[:, r4:_BIG_ROWS].reshape(D, 2 * D_XA)
    gv = gvec.reshape(4, 8 * D)
    sm = {"mem_norm_w": mem_norm_w, "norm_pre": norm_pre, "norm_post": norm_post, "a_w_s": a_w_s, "a_b_s": a_b_s,
          "c_a_log": c_a_log, "c_dt_bias": c_dt_bias, "c_o_norm_w": c_o_norm_w,
          "a_ln_w": gv[:, 0:512].reshape(4, 2, 256).transpose(1, 0, 2).reshape(2, D),
          "a_ln_b": gv[:, 512:1024].reshape(4, 2, 256).transpose(1, 0, 2).reshape(2, D),
          "b_conv_w": gv[:, 1024:1792].reshape(4, 1, 3, 256).transpose(1, 2, 0, 3).reshape(1, 3, D),
          "c_conv_w": gv[:, 1792:4864].reshape(4, 1, 4, 768).transpose(1, 2, 0, 3).reshape(1, 4, 3 * D)}
    c_mix = jnp.concatenate([fc[:, :3 * D + 16], jnp.zeros((D, AB_PAD - 16), BF16)], axis=1)
    wts = {"wkv": fkv, "wo": fo,
           "wm": [fa[0][:, :2 * D], fb[:, :3 * D], c_mix, fa[1][:, :2 * D]],
           "wg": [fa[0][:, 2 * D:], fb[:, 3 * D:], fc[:, 3 * D + 16:], fa[1][:, 2 * D:]]}

    loss, dx, g = _local_step(x[0], mem[0], loss_target[0], wts, sm)
    loss = lax.psum(loss, ("x", "y", "c"))

    ga = jnp.stack([jnp.concatenate([g["wm"][i], g["wg"][i]], axis=1) for i in (0, 3)])
    gb = jnp.concatenate([g["wm"][1], g["wg"][1]], axis=1)
    gc = jnp.concatenate([g["wm"][2][:, :3 * D + 16], g["wg"][2]], axis=1)
    go = jnp.stack(g["wo"])
    gbig_all = jnp.concatenate([
        ga.reshape(2, D, 4, D).transpose(2, 0, 1, 3).reshape(4, _ROWS[0], D),
        gb.reshape(D, 4, 1280).transpose(1, 0, 2).reshape(4, _ROWS[1], D),
        gc.reshape(D, 4, 1284).transpose(1, 0, 2).reshape(4, _ROWS[2], D),
        go.reshape(4, 4, 384, D).transpose(1, 0, 2, 3).reshape(4, _ROWS[3], D),
        g["wkv"].reshape(4, _ROWS[4], D),
        jnp.zeros((4, _BIG_PAD - _BIG_ROWS, D), F32)], axis=1).reshape(8, _BIG_PAD // 2, D)
    gs = {"mem_norm_w": g["mem_norm_w"], "norm_pre": jnp.concatenate(g["norm_pre"]),
          "norm_post": jnp.concatenate(g["norm_post"])}
    for n in _SMALL[3:]:
        gs[n] = jnp.stack([g[n][j] for j in sorted(g[n])])
    flat = jnp.concatenate([gs[n].reshape(-1) for n in _SMALL])
    flat = jnp.pad(flat, (0, _SMALL_ROWS * D - flat.shape[0])).reshape(1, _SMALL_ROWS, D)
    send = jnp.concatenate([gbig_all, jnp.broadcast_to(flat, (8, _SMALL_ROWS, D))], axis=1)
    red = _sum_slabs(_exchange_all(send))
    gshard = _share_with_sibling(red, _BIG_PAD // 2).reshape(_BIG_PAD, D)
    grads = {"a_w_in": gshard[r0:r1].reshape(a_w_in.shape), "b_w_in": gshard[r1:r2].reshape(b_w_in.shape),
             "c_w_in": gshard[r2:r3].reshape(c_w_in.shape), "w_out": gshard[r3:r4].reshape(w_out.shape),
             "w_mem_kv": gshard[r4:_BIG_ROWS]}
    flat = red[_BIG_PAD // 2:].reshape(-1)
    off = 0
    for n in _SMALL:
        shape = _SMALL_SHAPES[n]
        full = flat[off:off + _size(shape)].reshape(shape)
        off += _size(shape)
        if n in _SHARDED_SMALL:
            full = lax.dynamic_slice_in_dim(full, chip * _SHARDED_SMALL[n], _SHARDED_SMALL[n], axis=len(shape) - 1)
        grads[n] = full

    delta, new_m, new_v = {}, {}, {}
    for n in names:
        shape = w[n].shape
        view = (1, shape[0]) if len(shape) == 1 else (_size(shape[:-1]), shape[-1])
        d_, m_, v_ = _adamw(w[n].reshape(view), grads[n].reshape(view), m[n].reshape(view), v[n].reshape(view),
                            f"adamw_{n}")
        delta[n], new_m[n], new_v[n] = d_.reshape(shape), m_.reshape(shape), v_.reshape(shape)
    return (loss, dx[None], *[grads[n].reshape(w[n].shape) for n in names], *[delta[n] for n in names],
            *[new_m[n] for n in names], *[new_v[n] for n in names])
```

```python
import functools

import jax
import jax.numpy as jnp
from jax import lax
from jax.experimental import pallas as pl
from jax.experimental.pallas import tpu as pltpu

F32 = jnp.float32
BF16 = jnp.bfloat16
HI = lax.Precision.HIGHEST
MESH = pl.DeviceIdType.MESH
SDS = jax.ShapeDtypeStruct

D = 1024
D_XA = 512
D_CAT = 1536
N_MEM = 256
HD = 128
DEPTH = 4
EPS = 1e-6
TT = 256
DN_C = 64
DN_TB = 256
HALO = 8
AB_PAD = 128
VMEM_LIMIT = 56 * 1024 * 1024

ADAM_LR, ADAM_B1, ADAM_B2, ADAM_EPS, ADAM_WD, ADAM_STEP = 0.001, 0.9, 0.999, 1e-08, 0.01, 10


def _params(n_grid, vmem=None):
    return pltpu.CompilerParams(dimension_semantics=("arbitrary",) * n_grid, vmem_limit_bytes=vmem)


def _rms(x, w):
    return x * lax.rsqrt(jnp.mean(x * x, axis=-1, keepdims=True) + EPS) * w


def _dot_nn(a, b):
    return jnp.dot(a.astype(BF16), b.astype(BF16), preferred_element_type=F32)


def _dot_nt(a, b):
    return lax.dot_general(a.astype(BF16), b.astype(BF16), (((1,), (1,)), ((), ())), preferred_element_type=F32)


def _dot_tn(a, b):
    return lax.dot_general(a.astype(BF16), b.astype(BF16), (((0,), (0,)), ((), ())), preferred_element_type=F32)


@jax.custom_vjp
def mm(a, b):
    return _dot_nn(a, b)


mm.defvjp(lambda a, b: (_dot_nn(a, b), (a, b)), lambda r, g: (_dot_nt(g, r[1]), _dot_tn(r[0], g)))


@jax.custom_vjp
def mm_nt(a, b):
    return _dot_nt(a, b)


mm_nt.defvjp(lambda a, b: (_dot_nt(a, b), (a, b)), lambda r, g: (_dot_nn(g, r[1]), _dot_tn(g, r[0])))


@jax.custom_vjp
def mm_tn(a, b):
    return _dot_tn(a, b)


mm_tn.defvjp(lambda a, b: (_dot_tn(a, b), (a, b)), lambda r, g: (_dot_nt(r[1], g), _dot_nn(r[0], g)))


def _dot_hi(a, b):
    return jnp.dot(a, b, precision=HI, preferred_element_type=F32)


def _row_spec(width, tile=TT):
    return pl.BlockSpec((tile, width), lambda i: (i, 0))


def _full_spec(shape):
    return pl.BlockSpec(shape, lambda *_: (0,) * len(shape))


def _inproj_fwd(x, nw, wm, wg, name):
    T, M, G = x.shape[0], wm.shape[1], wg.shape[1]

    def body(x_ref, nw_ref, wm_ref, wg_ref, pm_ref, pg_ref, h_ref):
        h = _rms(x_ref[...], nw_ref[...]).astype(BF16)
        h_ref[...] = h
        pm_ref[...] = jnp.dot(h, wm_ref[...], preferred_element_type=F32)
        pg_ref[...] = jnp.dot(h, wg_ref[...], preferred_element_type=F32)

    return pl.pallas_call(
        body, name=name, grid=(T // TT,),
        in_specs=[_row_spec(D), _full_spec((1, D)), _full_spec((D, M)), _full_spec((D, G))],
        out_specs=[_row_spec(M), _row_spec(G), _row_spec(D)],
        out_shape=[SDS((T, M), F32), SDS((T, G), F32), SDS((T, D), BF16)],
        compiler_params=_params(1, VMEM_LIMIT))(x, nw, wm, wg)


def _inproj_bwd(dpm, dpg, x, nw, wm, wg, dxc, name):
    T, M, G = x.shape[0], wm.shape[1], wg.shape[1]

    def body(dpm_ref, dpg_ref, x_ref, nw_ref, wm_ref, wg_ref, dxc_ref, dx_ref, dnw_ref):
        dh = _dot_nt(dpm_ref[...], wm_ref[...]) + _dot_nt(dpg_ref[...], wg_ref[...])
        _, vjp = jax.vjp(_rms, x_ref[...], nw_ref[...])
        dxr, dnw = vjp(dh)
        dx_ref[...] = dxc_ref[...] + dxr

        @pl.when(pl.program_id(0) == 0)
        def _():
            dnw_ref[...] = jnp.zeros_like(dnw_ref)
        dnw_ref[...] += dnw

    return pl.pallas_call(
        body, name=name, grid=(T // TT,),
        in_specs=[_row_spec(M), _row_spec(G), _row_spec(D), _full_spec((1, D)), _full_spec((D, M)),
                  _full_spec((D, G)), _row_spec(D)],
        out_specs=[_row_spec(D), _full_spec((1, D))],
        out_shape=[SDS((T, D), F32), SDS((1, D), F32)],
        compiler_params=_params(1, VMEM_LIMIT))(dpm, dpg, x, nw, wm, wg, dxc)


def _matmul_tn(a, b, name):
    T, K = a.shape
    N = b.shape[1]
    tn = 512 if N % 512 == 0 else (640 if N % 640 == 0 else N)
    tt = min(512, T)

    def body(a_ref, b_ref, o_ref):
        @pl.when(pl.program_id(1) == 0)
        def _():
            o_ref[...] = jnp.zeros_like(o_ref)
        o_ref[...] += _dot_tn(a_ref[...], b_ref[...])

    return pl.pallas_call(
        body, name=name, grid=(N // tn, T // tt),
        in_specs=[pl.BlockSpec((tt, K), lambda j, t: (t, 0)), pl.BlockSpec((tt, tn), lambda j, t: (t, j))],
        out_specs=pl.BlockSpec((K, tn), lambda j, t: (0, j)),
        out_shape=SDS((K, N), F32),
        compiler_params=_params(2, VMEM_LIMIT))(a, b)


def _memkv_fn(mem, w, wkv):
    return mm(_rms(mem, w), wkv)


def _memkv_fwd(mem, w, wkv):
    def body(mem_ref, w_ref, wkv_ref, kv_ref):
        kv_ref[...] = _memkv_fn(mem_ref[...], w_ref[...], wkv_ref[...])

    return pl.pallas_call(body, name="memkv_fwd", out_shape=SDS((N_MEM, 2 * D_XA), F32),
                          compiler_params=_params(0, VMEM_LIMIT))(mem, w, wkv)


def _memkv_bwd(mem, w, wkv, dkv):
    def body(mem_ref, w_ref, wkv_ref, dkv_ref, dw_ref, dwkv_ref):
        _, vjp = jax.vjp(functools.partial(_memkv_fn, mem_ref[...]), w_ref[...], wkv_ref[...].astype(F32))
        dw, dwkv = vjp(dkv_ref[...])
        dw_ref[...] = dw
        dwkv_ref[...] = dwkv

    return pl.pallas_call(body, name="memkv_bwd", out_shape=[SDS((1, D), F32), SDS((D, 2 * D_XA), F32)],
                          compiler_params=_params(0, VMEM_LIMIT))(mem, w, wkv, dkv)


def _attn_gate(ymix, qx, z, *kvs):
    outs = []
    for j in range(4):
        s = mm_nt(qx[:, j * HD:(j + 1) * HD], kvs[j]) * (HD ** -0.5)
        e = jnp.exp(s - lax.stop_gradient(jnp.max(s, axis=-1, keepdims=True)))
        outs.append(mm(e / jnp.sum(e, axis=-1, keepdims=True), kvs[4 + j]))
    return jnp.concatenate([ymix] + outs, axis=1) * jax.nn.silu(z)


def _kv_blocks(kv_ref):
    return [kv_ref[:, j * HD:(j + 1) * HD] for j in range(8)]


def _ag_fwd(ymix, pg, kv, name):
    T = ymix.shape[0]

    def body(ymix_ref, pg_ref, kv_ref, ycat_ref):
        ycat_ref[...] = _attn_gate(ymix_ref[...], pg_ref[:, :D_XA], pg_ref[:, D_XA:], *_kv_blocks(kv_ref)).astype(BF16)

    return pl.pallas_call(
        body, name=name, grid=(T // TT,),
        in_specs=[_row_spec(D), _row_spec(D_XA + D_CAT), _full_spec((N_MEM, 2 * D_XA))],
        out_specs=_row_spec(D_CAT), out_shape=SDS((T, D_CAT), BF16),
        compiler_params=_params(1, VMEM_LIMIT))(ymix, pg, kv)


def _ag_bwd(dycat, ymix, pg, kv, dkv_in, name):
    T = ymix.shape[0]

    def body(dycat_ref, ymix_ref, pg_ref, kv_ref, dkvin_ref, dymix_ref, dpg_ref, dkv_ref):
        _, vjp = jax.vjp(_attn_gate, ymix_ref[...], pg_ref[:, :D_XA], pg_ref[:, D_XA:], *_kv_blocks(kv_ref))
        g = vjp(dycat_ref[...])
        dymix_ref[...] = g[0]
        dpg_ref[:, :D_XA] = g[1]
        dpg_ref[:, D_XA:] = g[2]

        @pl.when(pl.program_id(0) == 0)
        def _():
            dkv_ref[...] = dkvin_ref[...]
        for j in range(8):
            dkv_ref[:, j * HD:(j + 1) * HD] += g[3 + j]

    return pl.pallas_call(
        body, name=name, grid=(T // TT,),
        in_specs=[_row_spec(D_CAT), _row_spec(D), _row_spec(D_XA + D_CAT), _full_spec((N_MEM, 2 * D_XA)),
                  _full_spec((N_MEM, 2 * D_XA))],
        out_specs=[_row_spec(D), _row_spec(D_XA + D_CAT), _full_spec((N_MEM, 2 * D_XA))],
        out_shape=[SDS((T, D), F32), SDS((T, D_XA + D_CAT), F32), SDS((N_MEM, 2 * D_XA), F32)],
        compiler_params=_params(1, VMEM_LIMIT))(dycat, ymix, pg, kv, dkv_in)


def _outproj_fwd(ycat, wo, x, nw, name):
    T = x.shape[0]

    def body(ycat_ref, wo_ref, x_ref, nw_ref, o_ref, xn_ref):
        o = jnp.dot(ycat_ref[...], wo_ref[...], preferred_element_type=F32)
        o_ref[...] = o
        xn_ref[...] = x_ref[...] + _rms(o, nw_ref[...])

    return pl.pallas_call(
        body, name=name, grid=(T // TT,),
        in_specs=[_row_spec(D_CAT), _full_spec((D_CAT, D)), _row_spec(D), _full_spec((1, D))],
        out_specs=[_row_spec(D), _row_spec(D)], out_shape=[SDS((T, D), F32), SDS((T, D), F32)],
        compiler_params=_params(1, VMEM_LIMIT))(ycat, wo, x, nw)


def _outproj_bwd(dxo, o, nw, wo, name):
    T = dxo.shape[0]

    def body(dxo_ref, o_ref, nw_ref, wo_ref, dycat_ref, dobf_ref, dnw_ref):
        _, vjp = jax.vjp(_rms, o_ref[...], nw_ref[...])
        do, dnw = vjp(dxo_ref[...])
        dobf = do.astype(BF16)
        dobf_ref[...] = dobf
        dycat_ref[...] = _dot_nt(dobf, wo_ref[...])

        @pl.when(pl.program_id(0) == 0)
        def _():
            dnw_ref[...] = jnp.zeros_like(dnw_ref)
        dnw_ref[...] += dnw

    return pl.pallas_call(
        body, name=name, grid=(T // TT,),
        in_specs=[_row_spec(D), _row_spec(D), _full_spec((1, D)), _full_spec((D_CAT, D))],
        out_specs=[_row_spec(D_CAT), _row_spec(D), _full_spec((1, D))],
        out_shape=[SDS((T, D_CAT), F32), SDS((T, D), BF16), SDS((1, D), F32)],
        compiler_params=_params(1, VMEM_LIMIT))(dxo, o, nw, wo)


def _loss_head(xl, target):
    T = xl.shape[0]

    def body(x_ref, t_ref, loss_ref, dx_ref):
        err = x_ref[...] - t_ref[...]
        dx_ref[...] = err * (1.0 / D)

        @pl.when(pl.program_id(0) == 0)
        def _():
            loss_ref[...] = jnp.zeros_like(loss_ref)
        part = jnp.sum(jnp.sum(err * err, axis=1, keepdims=True), axis=0, keepdims=True) * (0.5 / D)
        loss_ref[...] += jnp.broadcast_to(part, loss_ref.shape)

    return pl.pallas_call(
        body, name="loss_head", grid=(T // TT,),
        in_specs=[_row_spec(D), _row_spec(D)],
        out_specs=[_full_spec((8, 128)), _row_spec(D)], out_shape=[SDS((8, 128), F32), SDS((T, D), F32)],
        compiler_params=_params(1))(xl, target)


def _gmlp_pre(u, v, lnw, lnb):
    vg = jax.nn.gelu(v)
    xc = vg - jnp.mean(vg, axis=-1, keepdims=True)
    vl = xc * lax.rsqrt(jnp.mean(xc * xc, axis=-1, keepdims=True) + EPS) * lnw + lnb
    return jax.nn.gelu(u), vl


def _tril(n, strict=False):
    r = lax.broadcasted_iota(jnp.int32, (n, n), 0)
    c = lax.broadcasted_iota(jnp.int32, (n, n), 1)
    return (r > c) if strict else (r >= c)


def _gmlp_fwd(pm, lnw, lnb, ws, bs3, name):
    T = pm.shape[0]

    def body(pm_ref, lnw_ref, lnb_ref, ws_ref, bs_ref, y_ref):
        ug, vl = _gmlp_pre(pm_ref[:, :D], pm_ref[:, D:], lnw_ref[...], lnb_ref[...])
        mask = _tril(HD)
        for g in range(8):
            w = jnp.where(mask, ws_ref[g], 0.0)
            for c in range(TT // HD):
                rows, cols = slice(c * HD, (c + 1) * HD), slice(g * HD, (g + 1) * HD)
                y_ref[rows, cols] = ug[rows, cols] * (_dot_nn(w, vl[rows, cols]) + bs_ref[g])

    return pl.pallas_call(
        body, name=name, grid=(T // TT,),
        in_specs=[_row_spec(2 * D), _full_spec((1, D)), _full_spec((1, D)), _full_spec((8, HD, HD)),
                  _full_spec((8, HD, HD))],
        out_specs=_row_spec(D), out_shape=SDS((T, D), F32),
        compiler_params=_params(1, VMEM_LIMIT))(pm, lnw, lnb, ws, bs3)


def _gmlp_bwd(dy, pm, lnw, lnb, ws, bs3, name):
    T = pm.shape[0]
    n_t = T // TT

    def body(dy_ref, pm_ref, lnw_ref, lnb_ref, ws_ref, bs_ref, dpm_ref, dlnw_ref, dlnb_ref, dws_ref, dbs_ref,
             dug_scr, dvl_scr, dbs_scr):
        i = pl.program_id(0)

        @pl.when(i == 0)
        def _():
            dlnw_ref[...] = jnp.zeros_like(dlnw_ref)
            dlnb_ref[...] = jnp.zeros_like(dlnb_ref)
            dws_ref[...] = jnp.zeros_like(dws_ref)
            dbs_scr[...] = jnp.zeros_like(dbs_scr)

        (ug, vl), vjp = jax.vjp(_gmlp_pre, pm_ref[:, :D], pm_ref[:, D:], lnw_ref[...], lnb_ref[...])
        mask = _tril(HD)
        for g in range(8):
            w = jnp.where(mask, ws_ref[g], 0.0)
            dw = jnp.zeros((HD, HD), F32)
            db = jnp.zeros((HD, HD), F32)
            for c in range(TT // HD):
                rows, cols = slice(c * HD, (c + 1) * HD), slice(g * HD, (g + 1) * HD)
                dyb, vlb = dy_ref[rows, cols], vl[rows, cols]
                sp = _dot_nn(w, vlb) + bs_ref[g]
                dsp = dyb * ug[rows, cols]
                dug_scr[rows, cols] = dyb * sp
                dvl_scr[rows, cols] = _dot_tn(w, dsp)
                dw += _dot_nt(dsp, vlb)
                db += dsp
            dws_ref[g] += jnp.where(mask, dw, 0.0)
            dbs_scr[g] += db
        du, dv, dlnw, dlnb = vjp((dug_scr[...], dvl_scr[...]))
        dpm_ref[:, :D] = du
        dpm_ref[:, D:] = dv
        dlnw_ref[...] += dlnw
        dlnb_ref[...] += dlnb

        @pl.when(i == n_t - 1)
        def _():
            for g in range(8):
                dbs_ref[g] = jnp.broadcast_to(jnp.sum(dbs_scr[g], axis=1, keepdims=True), (HD, HD))

    return pl.pallas_call(
        body, name=name, grid=(n_t,),
        in_specs=[_row_spec(D), _row_spec(2 * D), _full_spec((1, D)), _full_spec((1, D)), _full_spec((8, HD, HD)),
                  _full_spec((8, HD, HD))],
        out_specs=[_row_spec(2 * D), _full_spec((1, D)), _full_spec((1, D)), _full_spec((8, HD, HD)),
                   _full_spec((8, HD, HD))],
        out_shape=[SDS((T, 2 * D), F32), SDS((1, D), F32), SDS((1, D), F32), SDS((8, HD, HD), F32),
                   SDS((8, HD, HD), F32)],
        scratch_shapes=[pltpu.VMEM((TT, D), F32), pltpu.VMEM((TT, D), F32), pltpu.VMEM((8, HD, HD), F32)],
        compiler_params=_params(1, VMEM_LIMIT))(dy, pm, lnw, lnb, ws, bs3)


def _prev_spec(width, T):
    return pl.BlockSpec((HALO, width), lambda i: (jnp.maximum(i * (TT // HALO) - 1, 0), 0))


def _next_spec(width, T):
    return pl.BlockSpec((HALO, width), lambda i: (jnp.minimum((i + 1) * (TT // HALO), T // HALO - 1), 0))


def _rows_before(ext, j):
    return ext[HALO:] if j == 0 else pltpu.roll(ext, j, 0)[HALO:]


def _rows_after(ext, j):
    n = ext.shape[0]
    return ext[:n - HALO] if j == 0 else pltpu.roll(ext, n - j, 0)[:n - HALO]


def _conv_apply(ext_s, w):
    K = w.shape[0]
    y = _rows_before(ext_s, K - 1) * w[0:1]
    for k in range(1, K):
        y = y + _rows_before(ext_s, K - 1 - k) * w[k:k + 1]
    return y


def _conv_grads(ext_s, ext_dy, w):
    K = w.shape[0]
    dy = ext_dy[:ext_dy.shape[0] - HALO]
    ds = _rows_after(ext_dy, K - 1) * w[0:1]
    dws = [jnp.sum(dy * _rows_before(ext_s, K - 1), axis=0, keepdims=True)]
    for k in range(1, K):
        ds = ds + _rows_after(ext_dy, K - 1 - k) * w[k:k + 1]
        dws.append(jnp.sum(dy * _rows_before(ext_s, K - 1 - k), axis=0, keepdims=True))
    return ds, jnp.concatenate(dws, axis=0)


def _sconv_fwd(pm, w, name):
    T = pm.shape[0]

    def body(pm_ref, prev_ref, w_ref, y_ref):
        s = pm_ref[:, D:2 * D] * pm_ref[:, 2 * D:]
        sp = jnp.where(pl.program_id(0) > 0, prev_ref[:, D:2 * D] * prev_ref[:, 2 * D:], 0.0)
        y_ref[...] = pm_ref[:, :D] * _conv_apply(jnp.concatenate([sp, s], axis=0), w_ref[...])

    return pl.pallas_call(
        body, name=name, grid=(T // TT,),
        in_specs=[_row_spec(3 * D), _prev_spec(3 * D, T), _full_spec((3, D))],
        out_specs=_row_spec(D), out_shape=SDS((T, D), F32),
        compiler_params=_params(1, VMEM_LIMIT))(pm, pm, w)


def _sconv_bwd(dy, pm, w, name):
    T = pm.shape[0]
    n_t = T // TT

    def body(dy_ref, dyn_ref, pm_ref, prev_ref, next_ref, w_ref, dpm_ref, dw_ref):
        i = pl.program_id(0)
        bg, cg, hv = pm_ref[:, :D], pm_ref[:, D:2 * D], pm_ref[:, 2 * D:]
        sp = jnp.where(i > 0, prev_ref[:, D:2 * D] * prev_ref[:, 2 * D:], 0.0)
        ext_s = jnp.concatenate([sp, cg * hv], axis=0)
        dyv = dy_ref[...]
        dcn = jnp.where(i < n_t - 1, dyn_ref[...] * next_ref[:, :D], 0.0)
        ds, dw = _conv_grads(ext_s, jnp.concatenate([dyv * bg, dcn], axis=0), w_ref[...])
        dpm_ref[:, :D] = dyv * _conv_apply(ext_s, w_ref[...])
        dpm_ref[:, D:2 * D] = ds * hv
        dpm_ref[:, 2 * D:] = ds * cg

        @pl.when(i == 0)
        def _():
            dw_ref[...] = jnp.zeros_like(dw_ref)
        dw_ref[...] += dw

    return pl.pallas_call(
        body, name=name, grid=(n_t,),
        in_specs=[_row_spec(D), _next_spec(D, T), _row_spec(3 * D), _prev_spec(3 * D, T), _next_spec(3 * D, T),
                  _full_spec((3, D))],
        out_specs=[_row_spec(3 * D), _full_spec((3, D))],
        out_shape=[SDS((T, 3 * D), F32), SDS((3, D), F32)],
        compiler_params=_params(1, VMEM_LIMIT))(dy, dy, pm, pm, pm, w)


def _dnconv_fwd(pm, w, name):
    T = pm.shape[0]

    def body(pm_ref, prev_ref, w_ref, c_ref):
        sp = jnp.where(pl.program_id(0) > 0, prev_ref[...], 0.0)
        c_ref[...] = _conv_apply(jnp.concatenate([sp, pm_ref[...]], axis=0), w_ref[...])

    return pl.pallas_call(
        body, name=name, grid=(T // TT,),
        in_specs=[_row_spec(3 * D), _prev_spec(3 * D, T), _full_spec((4, 3 * D))],
        out_specs=_row_spec(3 * D), out_shape=SDS((T, 3 * D), F32),
        compiler_params=_params(1, VMEM_LIMIT))(pm, pm, w)


def _dnconv_bwd(dcq, dck, dcv, dab, pm, w, name):
    T = pm.shape[0]
    n_t = T // TT

    def body(dq_ref, dk_ref, dv_ref, dqn_ref, dkn_ref, dvn_ref, dab_ref, pm_ref, prev_ref, w_ref, dpm_ref, dw_ref):
        i = pl.program_id(0)
        sp = jnp.where(i > 0, prev_ref[...], 0.0)
        ext_s = jnp.concatenate([sp, pm_ref[...]], axis=0)
        own = jnp.concatenate([dq_ref[...], dk_ref[...], dv_ref[...]], axis=1)
        nxt = jnp.where(i < n_t - 1, jnp.concatenate([dqn_ref[...], dkn_ref[...], dvn_ref[...]], axis=1), 0.0)
        ds, dw = _conv_grads(ext_s, jnp.concatenate([own, nxt], axis=0), w_ref[...])
        dpm_ref[:, :3 * D] = ds
        dpm_ref[:, 3 * D:] = dab_ref[...]

        @pl.when(i == 0)
        def _():
            dw_ref[...] = jnp.zeros_like(dw_ref)
        dw_ref[...] += dw

    return pl.pallas_call(
        body, name=name, grid=(n_t,),
        in_specs=[_row_spec(D), _row_spec(D), _row_spec(D), _next_spec(D, T), _next_spec(D, T), _next_spec(D, T),
                  _row_spec(AB_PAD), _row_spec(3 * D), _prev_spec(3 * D, T), _full_spec((4, 3 * D))],
        out_specs=[_row_spec(3 * D + AB_PAD), _full_spec((4, 3 * D))],
        out_shape=[SDS((T, 3 * D + AB_PAD), F32), SDS((4, 3 * D), F32)],
        compiler_params=_params(1, VMEM_LIMIT))(dcq, dck, dcv, dcq, dck, dcv, dab, pm, pm, w)


def _l2n(x):
    return x * lax.rsqrt(jnp.sum(x * x, axis=-1, keepdims=True) + EPS)


def _softplus(x):
    return jnp.maximum(x, 0.0) + jnp.log1p(jnp.exp(-jnp.abs(x)))


def _dn_gates(ab, alog, dtb, h):
    lane = lax.broadcasted_iota(jnp.int32, ab.shape, 1)
    g_all = -jnp.exp(alog) * _softplus(ab + dtb)
    g = jnp.sum(jnp.where(lane == h, g_all, 0.0), axis=1, keepdims=True)
    beta = jnp.sum(jnp.where(lane == 8 + h, jax.nn.sigmoid(ab), 0.0), axis=1, keepdims=True)
    ones = jnp.ones((1, HD), F32)
    return g * ones, beta * ones


def _dn_chunk(cq, ck, cv, gb, bb, S, onw):
    C = DN_C
    q = _l2n(jax.nn.silu(cq)) * (HD ** -0.5)
    k = _l2n(jax.nn.silu(ck))
    v = jax.nn.silu(cv)
    incl, strict = _tril(C), _tril(C, strict=True)
    gcum = _dot_hi(incl.astype(F32), gb)
    gi = gcum[:, :C]
    gj = gcum.T[:C, :]
    decay = jnp.where(incl, jnp.exp(jnp.where(incl, gi - gj, 0.0)), 0.0)
    kb = k * bb
    a_mat = jnp.where(strict, mm_nt(kb, k) * decay, 0.0)
    p = -a_mat
    eye = (lax.broadcasted_iota(jnp.int32, (C, C), 0) == lax.broadcasted_iota(jnp.int32, (C, C), 1)).astype(F32)
    t_mat = eye + p
    for _ in range(5):
        p = _dot_hi(p, p)
        t_mat = t_mat + _dot_hi(t_mat, p)
    eg = jnp.exp(gcum)
    u = mm(t_mat, v * bb)
    w = mm(t_mat, kb * eg)
    qk = mm_nt(q, k) * decay
    glast = gcum[C - 1:C, :]
    v_new = u - mm(w, S)
    o = mm(q * eg, S) + mm(qk, v_new)
    s_new = S * jnp.exp(glast) + mm_tn(k * jnp.exp(glast - gcum), v_new)
    return _rms(o, onw), s_new


def _dn_specs(T, rev):
    nb = T // DN_TB
    blk = (lambda n: nb - 1 - n) if rev else (lambda n: n)
    head = [pl.BlockSpec((DN_TB, HD), functools.partial(lambda n, h, off: (blk(n), off + h), off=8 * s)) for s in range(3)]
    ab = pl.BlockSpec((DN_TB, AB_PAD), lambda n, h: (blk(n), 3 * D // AB_PAD))
    st = pl.BlockSpec((DN_TB // DN_C, None, HD, HD), lambda n, h: (blk(n), h, 0, 0))
    out = pl.BlockSpec((DN_TB, HD), lambda n, h: (blk(n), h))
    row = pl.BlockSpec((1, HD), lambda n, h: (0, 0))
    return nb, head, ab, st, out, row


def _dn_fwd(cpre, pm, alog, dtb, onw, name):
    T = cpre.shape[0]
    nb, head, ab, st, out, row = _dn_specs(T, False)

    def body(cq_ref, ck_ref, cv_ref, ab_ref, alog_ref, dtb_ref, onw_ref, o_ref, st_ref, s_scr):
        n, h = pl.program_id(0), pl.program_id(1)

        @pl.when(n == 0)
        def _():
            s_scr[h] = jnp.zeros((HD, HD), F32)
        gb, bb = _dn_gates(ab_ref[...], alog_ref[...], dtb_ref[...], h)
        S = s_scr[h]
        for c in range(DN_TB // DN_C):
            rows = slice(c * DN_C, (c + 1) * DN_C)
            st_ref[c] = S
            o, S = _dn_chunk(cq_ref[rows, :], ck_ref[rows, :], cv_ref[rows, :], gb[rows], bb[rows], S, onw_ref[...])
            o_ref[rows, :] = o
        s_scr[h] = S

    return pl.pallas_call(
        body, name=name, grid=(nb, 8),
        in_specs=head + [ab, row, row, row], out_specs=[out, st],
        out_shape=[SDS((T, D), F32), SDS((T // DN_C, 8, HD, HD), F32)],
        scratch_shapes=[pltpu.VMEM((8, HD, HD), F32)],
        compiler_params=_params(2, VMEM_LIMIT))(cpre, cpre, cpre, pm, alog, dtb, onw)


def _dn_bwd(do, cpre, pm, st, alog, dtb, onw, name):
    T = cpre.shape[0]
    nb, head, ab, stspec, out, row = _dn_specs(T, True)

    def body(do_ref, cq_ref, ck_ref, cv_ref, ab_ref, st_ref, alog_ref, dtb_ref, onw_ref,
             dcq_ref, dck_ref, dcv_ref, dab_ref, dalog_ref, ddtb_ref, donw_ref, ds_scr):
        n, h = pl.program_id(0), pl.program_id(1)

        @pl.when(n == 0)
        def _():
            ds_scr[h] = jnp.zeros((HD, HD), F32)

        @pl.when((n == 0) & (h == 0))
        def _():
            dalog_ref[...] = jnp.zeros_like(dalog_ref)
            ddtb_ref[...] = jnp.zeros_like(ddtb_ref)
            donw_ref[...] = jnp.zeros_like(donw_ref)

        @pl.when(h == 0)
        def _():
            dab_ref[...] = jnp.zeros_like(dab_ref)

        (gb, bb), gates_vjp = jax.vjp(lambda a, b, c: _dn_gates(a, b, c, h), ab_ref[...], alog_ref[...], dtb_ref[...])
        dS = ds_scr[h]
        n_c = DN_TB // DN_C
        dgs, dbs = [None] * n_c, [None] * n_c
        donw = jnp.zeros((1, HD), F32)
        for c in reversed(range(n_c)):
            rows = slice(c * DN_C, (c + 1) * DN_C)
            _, vjp = jax.vjp(_dn_chunk, cq_ref[rows, :], ck_ref[rows, :], cv_ref[rows, :], gb[rows], bb[rows],
                             st_ref[c], onw_ref[...])
            dcq, dck, dcv, dgs[c], dbs[c], dS, dn = vjp((do_ref[rows, :], dS))
            dcq_ref[rows, :] = dcq
            dck_ref[rows, :] = dck
            dcv_ref[rows, :] = dcv
            donw += dn
        ds_scr[h] = dS
        dab, dalog, ddtb = gates_vjp((jnp.concatenate(dgs, axis=0), jnp.concatenate(dbs, axis=0)))
        dab_ref[...] += dab
        dalog_ref[...] += dalog
        ddtb_ref[...] += ddtb
        donw_ref[...] += donw

    dabspec = pl.BlockSpec((DN_TB, AB_PAD), lambda n, h: (nb - 1 - n, 0))
    return pl.pallas_call(
        body, name=name, grid=(nb, 8),
        in_specs=[out] + head + [ab, stspec, row, row, row],
        out_specs=[out, out, out, dabspec, row, row, row],
        out_shape=[SDS((T, D), F32)] * 3 + [SDS((T, AB_PAD), F32)] + [SDS((1, HD), F32)] * 3,
        scratch_shapes=[pltpu.VMEM((8, HD, HD), F32)],
        compiler_params=_params(2, VMEM_LIMIT))(do, cpre, cpre, cpre, pm, st, alog, dtb, onw)


_BNN = (((2,), (1,)), ((0,), (0,)))
_BNT = (((2,), (2,)), ((0,), (0,)))
_BTN = (((1,), (1,)), ((0,), (0,)))


def _bdot(a, b, dims):
    return lax.dot_general(a.astype(BF16), b.astype(BF16), dims, preferred_element_type=F32)


def _bdot3(a, b, dims):
    ah, bh = a.astype(BF16), b.astype(BF16)
    al, bl = (a - ah.astype(F32)).astype(BF16), (b - bh.astype(F32)).astype(BF16)
    d = functools.partial(lax.dot_general, dimension_numbers=dims, preferred_element_type=F32)
    return d(ah, bh) + (d(ah, bl) + d(al, bh))


def _bdot_hi(a, b, dims):
    return lax.dot_general(a, b, dims, precision=HI, preferred_element_type=F32)


def _batched_matmuls(dot):
    @jax.custom_vjp
    def nn(a, b):
        return dot(a, b, _BNN)

    @jax.custom_vjp
    def nt(a, b):
        return dot(a, b, _BNT)

    @jax.custom_vjp
    def tn(a, b):
        return dot(a, b, _BTN)

    nn.defvjp(lambda a, b: (dot(a, b, _BNN), (a, b)), lambda r, g: (dot(g, r[1], _BNT), dot(r[0], g, _BTN)))
    nt.defvjp(lambda a, b: (dot(a, b, _BNT), (a, b)), lambda r, g: (dot(g, r[1], _BNN), dot(g, r[0], _BTN)))
    tn.defvjp(lambda a, b: (dot(a, b, _BTN), (a, b)), lambda r, g: (dot(r[1], g, _BNT), dot(r[0], g, _BNN)))
    return nn, nt, tn


bmm, bmm_nt, bmm_tn = _batched_matmuls(_bdot)
bmm3, _, _ = _batched_matmuls(_bdot3)
bmm_hi, bmm_hi_nt, _ = _batched_matmuls(_bdot_hi)

DN_NCH = DN_TB // DN_C
DN_NH = 4


def _dn_prep(cq, ck, cv, gb, bb):
    B, C = cq.shape[0], DN_C
    q = _l2n(jax.nn.silu(cq)) * (HD ** -0.5)
    k = _l2n(jax.nn.silu(ck))
    v = jax.nn.silu(cv)
    r = lax.broadcasted_iota(jnp.int32, (B, C, C), 1)
    c = lax.broadcasted_iota(jnp.int32, (B, C, C), 2)
    incl, strict = r >= c, r > c
    gcum = bmm_hi(incl.astype(F32), gb)
    lane0 = lax.broadcasted_iota(jnp.int32, (B, C, HD), 2) == 0
    gj = bmm_hi_nt(jnp.ones((B, C, HD), F32), jnp.where(lane0, gcum, 0.0))
    decay = jnp.where(incl, jnp.exp(jnp.where(incl, gcum[:, :, :C] - gj, 0.0)), 0.0)
    kb = k * bb
    p = -jnp.where(strict, bmm_nt(kb, k) * decay, 0.0)
    t_mat = (r == c).astype(F32) + p
    for _ in range(5):
        p = bmm3(p, p)
        t_mat = t_mat + bmm3(t_mat, p)
    eg = jnp.exp(gcum)
    glast = gcum[:, C - 1:C, :]
    return (bmm(t_mat, v * bb), bmm(t_mat, kb * eg), bmm_nt(q, k) * decay, q * eg, k * jnp.exp(glast - gcum),
            jnp.exp(glast))


def _dn_scan_step(u, w, qk, qd, kd, egl, S, onw):
    v_new = u - bmm(w, S)
    o = bmm(qd, S) + bmm(qk, v_new)
    return _rms(o, onw), S * egl + bmm_tn(kd, v_new)


def _head_gates(ab, alog, dtb, first_head, n_heads):
    gs, bs = [], []
    for i in range(n_heads):
        g, b = _dn_gates(ab, alog, dtb, first_head + i)
        gs.append(g.reshape(DN_NCH, DN_C, HD))
        bs.append(b.reshape(DN_NCH, DN_C, HD))
    return jnp.concatenate(gs, axis=0), jnp.concatenate(bs, axis=0)


def _to_batch(ref, n_heads):
    return jnp.concatenate([ref[:, i * HD:(i + 1) * HD].astype(F32).reshape(DN_NCH, DN_C, HD) for i in range(n_heads)],
                           axis=0)


def _from_batch(ref, val, n_heads):
    for i in range(n_heads):
        ref[:, i * HD:(i + 1) * HD] = val[i * DN_NCH:(i + 1) * DN_NCH].reshape(DN_TB, HD).astype(ref.dtype)


def _prep_specs(T, rev):
    nb = T // DN_TB
    blk = (lambda n: nb - 1 - n) if rev else (lambda n: n)
    ng = 8 // DN_NH
    head = [pl.BlockSpec((DN_TB, DN_NH * HD), functools.partial(lambda n, h, off: (blk(n), off + h), off=ng * s))
            for s in range(3)]
    ab = pl.BlockSpec((DN_TB, AB_PAD), lambda n, h: (blk(n), 3 * D // AB_PAD))
    row = pl.BlockSpec((1, HD), lambda n, h: (0, 0))
    wide = pl.BlockSpec((DN_TB, DN_NH * HD), lambda n, h: (blk(n), h))
    qk = pl.BlockSpec((DN_NCH, DN_NH, DN_C, DN_C), lambda n, h: (blk(n), h, 0, 0))
    eg = pl.BlockSpec((DN_NCH, DN_NH, 1, HD), lambda n, h: (blk(n), h, 0, 0))
    return nb, ng, head, ab, row, wide, qk, eg


def _dn_prep_fwd(cpre, pm, alog, dtb, name):
    T = cpre.shape[0]
    nb, ng, head, ab, row, wide, qks, egs = _prep_specs(T, False)

    def body(cq_ref, ck_ref, cv_ref, ab_ref, alog_ref, dtb_ref, u_ref, w_ref, qk_ref, qd_ref, kd_ref, e_ref):
        gb, bb = _head_gates(ab_ref[...], alog_ref[...], dtb_ref[...], pl.program_id(1) * DN_NH, DN_NH)
        u, w, qk, qd, kd, egl = _dn_prep(_to_batch(cq_ref, DN_NH), _to_batch(ck_ref, DN_NH), _to_batch(cv_ref, DN_NH),
                                         gb, bb)
        _from_batch(u_ref, u, DN_NH)
        _from_batch(w_ref, w, DN_NH)
        _from_batch(qd_ref, qd, DN_NH)
        _from_batch(kd_ref, kd, DN_NH)
        for i in range(DN_NH):
            qk_ref[:, i] = qk[i * DN_NCH:(i + 1) * DN_NCH].astype(BF16)
            e_ref[:, i] = egl[i * DN_NCH:(i + 1) * DN_NCH]

    return pl.pallas_call(
        body, name=name, grid=(nb, ng), in_specs=head + [ab, row, row],
        out_specs=[wide, wide, qks, wide, wide, egs],
        out_shape=[SDS((T, D), F32), SDS((T, D), BF16), SDS((T // DN_C, 8, DN_C, DN_C), BF16), SDS((T, D), BF16),
                   SDS((T, D), BF16), SDS((T // DN_C, 8, 1, HD), F32)],
        compiler_params=_params(2, VMEM_LIMIT))(cpre, cpre, cpre, pm, alog, dtb)


def _dn_prep_bwd(du, dw, dqk, dqd, dkd, degl, cpre, pm, alog, dtb, name):
    T = cpre.shape[0]
    nb, ng, head, ab, row, wide, qks, egs = _prep_specs(T, True)

    def body(du_ref, dw_ref, dqk_ref, dqd_ref, dkd_ref, de_ref, cq_ref, ck_ref, cv_ref, ab_ref, alog_ref, dtb_ref,
             dcq_ref, dck_ref, dcv_ref, dab_ref, dalog_ref, ddtb_ref):
        n, h = pl.program_id(0), pl.program_id(1)

        @pl.when((n == 0) & (h == 0))
        def _():
            dalog_ref[...] = jnp.zeros_like(dalog_ref)
            ddtb_ref[...] = jnp.zeros_like(ddtb_ref)

        @pl.when(h == 0)
        def _():
            dab_ref[...] = jnp.zeros_like(dab_ref)

        def fwd(cq, ck, cv, ab_v, alog_v, dtb_v):
            gb, bb = _head_gates(ab_v, alog_v, dtb_v, h * DN_NH, DN_NH)
            return _dn_prep(cq, ck, cv, gb, bb)

        _, vjp = jax.vjp(fwd, _to_batch(cq_ref, DN_NH), _to_batch(ck_ref, DN_NH), _to_batch(cv_ref, DN_NH), ab_ref[...],
                         alog_ref[...], dtb_ref[...])
        cot = (_to_batch(du_ref, DN_NH), _to_batch(dw_ref, DN_NH),
               jnp.concatenate([dqk_ref[:, i] for i in range(DN_NH)], axis=0), _to_batch(dqd_ref, DN_NH),
               _to_batch(dkd_ref, DN_NH), jnp.concatenate([de_ref[:, i] for i in range(DN_NH)], axis=0))
        dcq, dck, dcv, dab, dalog, ddtb = vjp(cot)
        _from_batch(dcq_ref, dcq, DN_NH)
        _from_batch(dck_ref, dck, DN_NH)
        _from_batch(dcv_ref, dcv, DN_NH)
        dab_ref[...] += dab
        dalog_ref[...] += dalog
        ddtb_ref[...] += ddtb

    dabspec = pl.BlockSpec((DN_TB, AB_PAD), lambda n, h: (nb - 1 - n, 0))
    return pl.pallas_call(
        body, name=name, grid=(nb, ng),
        in_specs=[wide, wide, qks, wide, wide, egs] + head + [ab, row, row],
        out_specs=[wide, wide, wide, dabspec, row, row],
        out_shape=[SDS((T, D), F32)] * 3 + [SDS((T, AB_PAD), F32)] + [SDS((1, HD), F32)] * 2,
        compiler_params=_params(2, VMEM_LIMIT))(du, dw, dqk, dqd, dkd, degl, cpre, cpre, cpre, pm, alog, dtb)


def _scan_specs(T, rev):
    nb = T // DN_TB
    blk = (lambda n: nb - 1 - n) if rev else (lambda n: n)
    wide = pl.BlockSpec((DN_TB, D), lambda n: (blk(n), 0))
    qk = pl.BlockSpec((DN_NCH, 8, DN_C, DN_C), lambda n: (blk(n), 0, 0, 0))
    eg = pl.BlockSpec((DN_NCH, 8, 1, HD), lambda n: (blk(n), 0, 0, 0))
    st = pl.BlockSpec((DN_NCH, 8, HD, HD), lambda n: (blk(n), 0, 0, 0))
    row = pl.BlockSpec((1, HD), lambda n: (0, 0))
    return nb, wide, qk, eg, st, row


def _heads_of(ref, rows):
    return jnp.concatenate([ref[rows, h * HD:(h + 1) * HD].astype(F32)[None] for h in range(8)], axis=0)


def _dn_scan_fwd(u, w, qk, qd, kd, egl, onw, name):
    T = u.shape[0]
    nb, wide, qks, egs, sts, row = _scan_specs(T, False)

    def body(u_ref, w_ref, qk_ref, qd_ref, kd_ref, e_ref, onw_ref, o_ref, st_ref, s_scr):
        @pl.when(pl.program_id(0) == 0)
        def _():
            s_scr[...] = jnp.zeros_like(s_scr)
        S = s_scr[...]
        for c in range(DN_NCH):
            rows = slice(c * DN_C, (c + 1) * DN_C)
            st_ref[c] = S
            o, S = _dn_scan_step(_heads_of(u_ref, rows), _heads_of(w_ref, rows), qk_ref[c].astype(F32),
                                 _heads_of(qd_ref, rows), _heads_of(kd_ref, rows), e_ref[c], S, onw_ref[...])
            for h in range(8):
                o_ref[rows, h * HD:(h + 1) * HD] = o[h]
        s_scr[...] = S

    return pl.pallas_call(
        body, name=name, grid=(nb,), in_specs=[wide, wide, qks, wide, wide, egs, row], out_specs=[wide, sts],
        out_shape=[SDS((T, D), F32), SDS((T // DN_C, 8, HD, HD), F32)],
        scratch_shapes=[pltpu.VMEM((8, HD, HD), F32)],
        compiler_params=_params(1, VMEM_LIMIT))(u, w, qk, qd, kd, egl, onw)


def _dn_scan_bwd(do, u, w, qk, qd, kd, egl, st, onw, name):
    T = u.shape[0]
    nb, wide, qks, egs, sts, row = _scan_specs(T, True)

    def body(do_ref, u_ref, w_ref, qk_ref, qd_ref, kd_ref, e_ref, st_ref, onw_ref,
             du_ref, dw_ref, dqk_ref, dqd_ref, dkd_ref, de_ref, donw_ref, ds_scr):
        @pl.when(pl.program_id(0) == 0)
        def _():
            ds_scr[...] = jnp.zeros_like(ds_scr)
            donw_ref[...] = jnp.zeros_like(donw_ref)
        dS = ds_scr[...]
        donw = jnp.zeros((1, HD), F32)
        for c in reversed(range(DN_NCH)):
            rows = slice(c * DN_C, (c + 1) * DN_C)
            _, vjp = jax.vjp(_dn_scan_step, _heads_of(u_ref, rows), _heads_of(w_ref, rows), qk_ref[c].astype(F32),
                             _heads_of(qd_ref, rows), _heads_of(kd_ref, rows), e_ref[c], st_ref[c], onw_ref[...])
            du, dw, dqk, dqd, dkd, de, dS, dn = vjp((_heads_of(do_ref, rows), dS))
            for h in range(8):
                cols = slice(h * HD, (h + 1) * HD)
                du_ref[rows, cols] = du[h]
                dw_ref[rows, cols] = dw[h]
                dqd_ref[rows, cols] = dqd[h]
                dkd_ref[rows, cols] = dkd[h]
            dqk_ref[c] = dqk
            de_ref[c] = de
            donw += dn
        ds_scr[...] = dS
        donw_ref[...] += donw

    return pl.pallas_call(
        body, name=name, grid=(nb,), in_specs=[wide, wide, wide, qks, wide, wide, egs, sts, row],
        out_specs=[wide, wide, qks, wide, wide, egs, row],
        out_shape=[SDS((T, D), F32), SDS((T, D), F32), SDS((T // DN_C, 8, DN_C, DN_C), F32), SDS((T, D), F32),
                   SDS((T, D), F32), SDS((T // DN_C, 8, 1, HD), F32), SDS((1, HD), F32)],
        scratch_shapes=[pltpu.VMEM((8, HD, HD), F32)],
        compiler_params=_params(1, VMEM_LIMIT))(do, u, w, qk, qd, kd, egl, st, onw)


def _adamw(w, g, m, v, name):
    R, C = w.shape
    tr = 256 if R % 256 == 0 and R > 256 else R
    c1 = 1.0 - ADAM_B1 ** ADAM_STEP
    c2 = 1.0 - ADAM_B2 ** ADAM_STEP

    def body(w_ref, g_ref, m_ref, v_ref, d_ref, nm_ref, nv_ref):
        gv = g_ref[...]
        nm = ADAM_B1 * m_ref[...] + (1.0 - ADAM_B1) * gv
        nv = ADAM_B2 * v_ref[...] + (1.0 - ADAM_B2) * (gv * gv)
        nm_ref[...] = nm
        nv_ref[...] = nv
        d_ref[...] = -ADAM_LR * ((nm / c1) / (jnp.sqrt(nv / c2) + ADAM_EPS) + ADAM_WD * w_ref[...])

    spec = pl.BlockSpec((tr, C), lambda i: (i, 0))
    return pl.pallas_call(
        body, name=name, grid=(R // tr,), in_specs=[spec] * 4, out_specs=[spec] * 3,
        out_shape=[SDS((R, C), F32)] * 3, compiler_params=_params(1, VMEM_LIMIT))(w, g, m, v)


def _local_step(x, mem, target, wts, sm):
    kinds = [i % 3 for i in range(DEPTH)]
    mnw = sm["mem_norm_w"].reshape(1, D)
    kv = _memkv_fwd(mem, mnw, wts["wkv"])
    saved = []
    for i, kind in enumerate(kinds):
        j = i // 3
        npre = sm["norm_pre"][i].reshape(1, D)
        npost = sm["norm_post"][i].reshape(1, D)
        pm, pg, h = _inproj_fwd(x, npre, wts["wm"][i], wts["wg"][i], f"inproj_fwd_{i}")
        extra = None
        if kind == 0:
            bs3 = jnp.broadcast_to(sm["a_b_s"][j][:, :, None], (8, HD, HD))
            ymix = _gmlp_fwd(pm, sm["a_ln_w"][j].reshape(1, D), sm["a_ln_b"][j].reshape(1, D), sm["a_w_s"][j], bs3,
                             f"gmlp_fwd_{i}")
            extra = bs3
        elif kind == 1:
            ymix = _sconv_fwd(pm, sm["b_conv_w"][j], f"sconv_fwd_{i}")
        else:
            cpre = _dnconv_fwd(pm, sm["c_conv_w"][j], f"dnconv_fwd_{i}")
            alog = jnp.pad(sm["c_a_log"][j], (0, HD - 8)).reshape(1, HD)
            dtb = jnp.pad(sm["c_dt_bias"][j], (0, HD - 8)).reshape(1, HD)
            onw = sm["c_o_norm_w"][j].reshape(1, HD)
            prep = _dn_prep_fwd(cpre, pm, alog, dtb, f"dn_prep_fwd_{i}")
            ymix, st = _dn_scan_fwd(*prep, onw, f"dn_scan_fwd_{i}")
            extra = (cpre, prep, st, alog, dtb, onw)
        ycat = _ag_fwd(ymix, pg, kv, f"ag_fwd_{i}")
        o, xn = _outproj_fwd(ycat, wts["wo"][i], x, npost, f"outproj_fwd_{i}")
        saved.append((x, h, pm, pg, ymix, ycat, o, extra))
        x = xn

    loss, dx = _loss_head(x, target)

    g = {"wm": [None] * DEPTH, "wg": [None] * DEPTH, "wo": [None] * DEPTH, "norm_pre": [None] * DEPTH,
         "norm_post": [None] * DEPTH}
    dkv = jnp.zeros((N_MEM, 2 * D_XA), F32)
    for i in reversed(range(DEPTH)):
        kind, j = kinds[i], i // 3
        xi, h, pm, pg, ymix, ycat, o, extra = saved[i]
        npre = sm["norm_pre"][i].reshape(1, D)
        npost = sm["norm_post"][i].reshape(1, D)
        dycat, dobf, g["norm_post"][i] = _outproj_bwd(dx, o, npost, wts["wo"][i], f"outproj_bwd_{i}")
        g["wo"][i] = _matmul_tn(ycat, dobf, f"dwo_{i}")
        dymix, dpg, dkv = _ag_bwd(dycat, ymix, pg, kv, dkv, f"ag_bwd_{i}")
        if kind == 0:
            dpm, dlnw, dlnb, dws, dbs3 = _gmlp_bwd(dymix, pm, sm["a_ln_w"][j].reshape(1, D),
                                                   sm["a_ln_b"][j].reshape(1, D), sm["a_w_s"][j], extra,
                                                   f"gmlp_bwd_{i}")
            g.setdefault("a_ln_w", {})[j] = dlnw.reshape(D)
            g.setdefault("a_ln_b", {})[j] = dlnb.reshape(D)
            g.setdefault("a_w_s", {})[j] = dws
            g.setdefault("a_b_s", {})[j] = dbs3[:, :, 0]
        elif kind == 1:
            dpm, dcw = _sconv_bwd(dymix, pm, sm["b_conv_w"][j], f"sconv_bwd_{i}")
            g.setdefault("b_conv_w", {})[j] = dcw
        else:
            cpre, prep, st, alog, dtb, onw = extra
            *dprep, donw = _dn_scan_bwd(dymix, *prep, st, onw, f"dn_scan_bwd_{i}")
            dcq, dck, dcv, dab, dalog, ddtb = _dn_prep_bwd(*dprep, cpre, pm, alog, dtb, f"dn_prep_bwd_{i}")
            dpm, dcw = _dnconv_bwd(dcq, dck, dcv, dab, pm, sm["c_conv_w"][j], f"dnconv_bwd_{i}")
            g.setdefault("c_conv_w", {})[j] = dcw
            g.setdefault("c_a_log", {})[j] = dalog[0, :8]
            g.setdefault("c_dt_bias", {})[j] = ddtb[0, :8]
            g.setdefault("c_o_norm_w", {})[j] = donw[0]
        g["wm"][i] = _matmul_tn(h, dpm, f"dwm_{i}")
        g["wg"][i] = _matmul_tn(h, dpg, f"dwg_{i}")
        dx, g["norm_pre"][i] = _inproj_bwd(dpm, dpg, xi, npre, wts["wm"][i], wts["wg"][i], dx, f"inproj_bwd_{i}")
    g["mem_norm_w"], g["wkv"] = _memkv_bwd(mem, mnw, wts["wkv"], dkv)
    return loss[0, 0], dx, g


ANY = pl.BlockSpec(memory_space=pl.ANY)


def _place():
    return lax.axis_index("x"), lax.axis_index("y"), lax.axis_index("c")


def _gather_chips(shards):
    n = len(shards)

    def body(*refs):
        ins, outs = refs[:n], refs[n:2 * n]
        send_sems, recv_sems, loc_sems = refs[2 * n:]
        x, y, c = _place()
        peers = [(1 - x, y), (x, 1 - y), (1 - x, 1 - y)]

        def remote(k, j, slab):
            px, py = peers[j]
            return pltpu.make_async_remote_copy(
                src_ref=ins[k], dst_ref=outs[k].at[slab], send_sem=send_sems.at[k, j], recv_sem=recv_sems.at[k, j],
                device_id=(px, py, c), device_id_type=MESH)

        local = [pltpu.make_async_copy(ins[k], outs[k].at[2 * x + y], loc_sems.at[k]) for k in range(n)]
        sends = [remote(k, j, 2 * x + y) for k in range(n) for j in range(3)]
        for cp in local + sends:
            cp.start()
        for k in range(n):
            for j, (px, py) in enumerate(peers):
                remote(k, j, 2 * px + py).wait_recv()
        for cp in sends:
            cp.wait_send()
        for cp in local:
            cp.wait()

    return pl.pallas_call(
        body, name="gather_chips", in_specs=[ANY] * n, out_specs=[ANY] * n,
        out_shape=[SDS((4,) + a.shape, a.dtype) for a in shards],
        scratch_shapes=[pltpu.SemaphoreType.DMA((n, 3)), pltpu.SemaphoreType.DMA((n, 3)),
                        pltpu.SemaphoreType.DMA((n,))])(*shards)


def _exchange_all(send):
    def body(send_ref, land_ref, send_sems, recv_sems, loc_sem):
        x, y, c = _place()
        me = 4 * x + 2 * y + c

        def peer(r):
            px = 1 - x if (r >> 2) & 1 else x
            py = 1 - y if (r >> 1) & 1 else y
            pc = 1 - c if r & 1 else c
            return px, py, pc

        def remote(r, slab):
            px, py, pc = peer(r)
            return pltpu.make_async_remote_copy(
                src_ref=send_ref.at[4 * px + 2 * py + pc], dst_ref=land_ref.at[slab], send_sem=send_sems.at[r - 1],
                recv_sem=recv_sems.at[r - 1], device_id=(px, py, pc), device_id_type=MESH)

        local = pltpu.make_async_copy(send_ref.at[me], land_ref.at[me], loc_sem)
        local.start()
        sends = [remote(r, me) for r in range(1, 8)]
        for cp in sends:
            cp.start()
        for r in range(1, 8):
            px, py, pc = peer(r)
            remote(r, 4 * px + 2 * py + pc).wait_recv()
        for cp in sends:
            cp.wait_send()
        local.wait()

    return pl.pallas_call(
        body, name="exchange_all", in_specs=[ANY], out_specs=ANY, out_shape=SDS(send.shape, send.dtype),
        scratch_shapes=[pltpu.SemaphoreType.DMA((7,)), pltpu.SemaphoreType.DMA((7,)), pltpu.SemaphoreType.DMA(())])(send)


def _sum_slabs(land):
    n, P, C = land.shape
    tr = max(t for t in range(8, 257, 8) if P % t == 0)

    def body(l_ref, o_ref):
        acc = l_ref[0]
        for s in range(1, n):
            acc = acc + l_ref[s]
        o_ref[...] = acc

    return pl.pallas_call(
        body, name="sum_slabs", grid=(P // tr,),
        in_specs=[pl.BlockSpec((n, tr, C), lambda i: (0, i, 0))], out_specs=pl.BlockSpec((tr, C), lambda i: (i, 0)),
        out_shape=SDS((P, C), land.dtype), compiler_params=_params(1, VMEM_LIMIT))(land)


HALF_ROWS = 3328
CHUNK_ROWS = 256
N_CHUNKS = HALF_ROWS // CHUNK_ROWS


def _swap_halves(gbig, small):
    def body(g_ref, s_ref, ob_ref, os_ref, send_sems, recv_sems):
        x, y, c = _place()
        copies = []
        for s in range(4):
            for k in range(N_CHUNKS):
                rows = pl.ds(k * CHUNK_ROWS, CHUNK_ROWS)
                copies.append(pltpu.make_async_remote_copy(
                    src_ref=g_ref.at[s, 1 - c, rows], dst_ref=ob_ref.at[s, rows], send_sem=send_sems.at[len(copies)],
                    recv_sem=recv_sems.at[len(copies)], device_id=(x, y, 1 - c), device_id_type=MESH))
        copies.append(pltpu.make_async_remote_copy(
            src_ref=s_ref, dst_ref=os_ref, send_sem=send_sems.at[len(copies)], recv_sem=recv_sems.at[len(copies)],
            device_id=(x, y, 1 - c), device_id_type=MESH))
        for cp in copies:
            cp.start()
        for cp in copies:
            cp.wait_recv()
        for cp in copies:
            cp.wait_send()

    n = 4 * N_CHUNKS + 1
    return pl.pallas_call(
        body, name="swap_halves", in_specs=[ANY, ANY], out_specs=[ANY, ANY],
        out_shape=[SDS((4, HALF_ROWS, D), F32), SDS(small.shape, F32)],
        scratch_shapes=[pltpu.SemaphoreType.DMA((n,)), pltpu.SemaphoreType.DMA((n,))])(gbig, small)


def _pair_sum(gbig, other):
    def body(g_ref, o_ref, pb_ref, own_ref):
        x, y, c = _place()
        v = jnp.where(c == 0, g_ref[0], g_ref[1]) + o_ref[...]
        pb_ref[...] = v.astype(BF16)

        @pl.when(pl.program_id(1) == 2 * x + y)
        def _():
            own_ref[...] = v

    return pl.pallas_call(
        body, name="pair_sum", grid=(N_CHUNKS, 4),
        in_specs=[pl.BlockSpec((None, 2, CHUNK_ROWS, D), lambda i, s: (s, 0, i, 0)),
                  pl.BlockSpec((None, CHUNK_ROWS, D), lambda i, s: (s, i, 0))],
        out_specs=[pl.BlockSpec((None, CHUNK_ROWS, D), lambda i, s: (s, i, 0)),
                   pl.BlockSpec((CHUNK_ROWS, D), lambda i, s: (i, 0))],
        out_shape=[SDS((4, HALF_ROWS, D), BF16), SDS((HALF_ROWS, D), F32)],
        compiler_params=_params(2, VMEM_LIMIT))(gbig, other)


def _add(a, b, name):
    def body(a_ref, b_ref, o_ref):
        o_ref[...] = a_ref[...] + b_ref[...]

    return pl.pallas_call(body, name=name, out_shape=SDS(a.shape, a.dtype), compiler_params=_params(0, VMEM_LIMIT))(a, b)


def _chip_exchange(pb, ps):
    n_small = ps.shape[0]

    def body(pb_ref, ps_ref, lb_ref, ls_ref, send_sems, recv_sems, loc_sems):
        x, y, c = _place()
        chip = 2 * x + y
        peers = [(1 - x, y), (x, 1 - y), (1 - x, 1 - y)]
        local = [pltpu.make_async_copy(pb_ref.at[chip, pl.ds(k * CHUNK_ROWS, CHUNK_ROWS)],
                                       lb_ref.at[chip, pl.ds(k * CHUNK_ROWS, CHUNK_ROWS)], loc_sems.at[k])
                 for k in range(N_CHUNKS)]
        local.append(pltpu.make_async_copy(ps_ref, ls_ref.at[chip], loc_sems.at[N_CHUNKS]))

        def copies(slab_of):
            out = []
            for j, (px, py) in enumerate(peers):
                for k in range(N_CHUNKS):
                    rows = pl.ds(k * CHUNK_ROWS, CHUNK_ROWS)
                    out.append(pltpu.make_async_remote_copy(
                        src_ref=pb_ref.at[2 * px + py, rows], dst_ref=lb_ref.at[slab_of(j), rows],
                        send_sem=send_sems.at[len(out)], recv_sem=recv_sems.at[len(out)], device_id=(px, py, c),
                        device_id_type=MESH))
                out.append(pltpu.make_async_remote_copy(
                    src_ref=ps_ref, dst_ref=ls_ref.at[slab_of(j)], send_sem=send_sems.at[len(out)],
                    recv_sem=recv_sems.at[len(out)], device_id=(px, py, c), device_id_type=MESH))
            return out

        sends = copies(lambda j: chip)
        for cp in local + sends:
            cp.start()
        for cp in copies(lambda j: 2 * peers[j][0] + peers[j][1]):
            cp.wait_recv()
        for cp in sends:
            cp.wait_send()
        for cp in local:
            cp.wait()

    n = 3 * (N_CHUNKS + 1)
    return pl.pallas_call(
        body, name="chip_exchange", in_specs=[ANY, ANY], out_specs=[ANY, ANY],
        out_shape=[SDS((4, HALF_ROWS, D), BF16), SDS((4, n_small, D), F32)],
        scratch_shapes=[pltpu.SemaphoreType.DMA((n,)), pltpu.SemaphoreType.DMA((n,)),
                        pltpu.SemaphoreType.DMA((N_CHUNKS + 1,))])(pb, ps)


def _chip_sum(own, land):
    def body(own_ref, l_ref, o_ref):
        chip = 2 * lax.axis_index("x") + lax.axis_index("y")
        acc = jnp.where(chip == 0, own_ref[...], l_ref[0].astype(F32))
        for s in range(1, 4):
            acc = acc + jnp.where(chip == s, own_ref[...], l_ref[s].astype(F32))
        o_ref[...] = acc

    return pl.pallas_call(
        body, name="chip_sum", grid=(N_CHUNKS,),
        in_specs=[pl.BlockSpec((CHUNK_ROWS, D), lambda i: (i, 0)), pl.BlockSpec((4, CHUNK_ROWS, D), lambda i: (0, i, 0))],
        out_specs=pl.BlockSpec((CHUNK_ROWS, D), lambda i: (i, 0)), out_shape=SDS((HALF_ROWS, D), F32),
        compiler_params=_params(1, VMEM_LIMIT))(own, land)


def _sum4(land):
    def body(l_ref, o_ref):
        o_ref[...] = ((l_ref[0] + l_ref[1]) + l_ref[2]) + l_ref[3]

    return pl.pallas_call(body, name="sum_small", out_shape=SDS(land.shape[1:], land.dtype),
                          compiler_params=_params(0, VMEM_LIMIT))(land)


def _share_half(r):
    def body(r_ref, o_ref, send_sems, recv_sems, loc_sems):
        x, y, c = _place()

        def chunks(slab, remote):
            out = []
            for k in range(N_CHUNKS):
                rows = pl.ds(k * CHUNK_ROWS, CHUNK_ROWS)
                if remote:
                    out.append(pltpu.make_async_remote_copy(
                        src_ref=r_ref.at[rows], dst_ref=o_ref.at[slab, rows], send_sem=send_sems.at[k],
                        recv_sem=recv_sems.at[k], device_id=(x, y, 1 - c), device_id_type=MESH))
                else:
                    out.append(pltpu.make_async_copy(r_ref.at[rows], o_ref.at[slab, rows], loc_sems.at[k]))
            return out

        local, sends = chunks(c, False), chunks(c, True)
        for cp in local + sends:
            cp.start()
        for cp in chunks(1 - c, True):
            cp.wait_recv()
        for cp in sends:
            cp.wait_send()
        for cp in local:
            cp.wait()

    return pl.pallas_call(
        body, name="share_half", in_specs=[ANY], out_specs=ANY, out_shape=SDS((2, HALF_ROWS, D), F32),
        scratch_shapes=[pltpu.SemaphoreType.DMA((N_CHUNKS,)), pltpu.SemaphoreType.DMA((N_CHUNKS,)),
                        pltpu.SemaphoreType.DMA((N_CHUNKS,))])(r)


_SMALL = ["mem_norm_w", "norm_pre", "norm_post", "a_ln_w", "a_ln_b", "a_w_s", "a_b_s", "b_conv_w", "c_conv_w",
          "c_a_log", "c_dt_bias", "c_o_norm_w"]
_SMALL_SHAPES = {"mem_norm_w": (D,), "norm_pre": (4, D), "norm_post": (4, D), "a_ln_w": (2, D), "a_ln_b": (2, D),
                 "a_w_s": (2, 8, HD, HD), "a_b_s": (2, 8, HD), "b_conv_w": (1, 3, D), "c_conv_w": (1, 4, 3 * D),
                 "c_a_log": (1, 8), "c_dt_bias": (1, 8), "c_o_norm_w": (1, HD)}
_SHARDED_SMALL = {"a_ln_w": D // 4, "a_ln_b": D // 4, "b_conv_w": D // 4, "c_conv_w": 3 * D // 4}
_ROWS = [2048, 1280, 1284, 1536, 256]
_BIG_ROWS = sum(_ROWS)
_BIG_PAD = 6416
_SMALL_ROWS = 288


def _size(shape):
    n = 1
    for d in shape:
        n *= d
    return n


def kernel(x, mem, mem_norm_w, w_mem_kv, norm_pre, norm_post, w_out, a_w_in, a_ln_w, a_ln_b, a_w_s, a_b_s, b_w_in, b_conv_w, c_w_in, c_conv_w, c_a_log, c_dt_bias, c_o_norm_w, loss_target, m_mem_norm_w, m_w_mem_kv, m_norm_pre, m_norm_post, m_w_out, m_a_w_in, m_a_ln_w, m_a_ln_b, m_a_w_s, m_a_b_s, m_b_w_in, m_b_conv_w, m_c_w_in, m_c_conv_w, m_c_a_log, m_c_dt_bias, m_c_o_norm_w, v_mem_norm_w, v_w_mem_kv, v_norm_pre, v_norm_post, v_w_out, v_a_w_in, v_a_ln_w, v_a_ln_b, v_a_w_s, v_a_b_s, v_b_w_in, v_b_conv_w, v_c_w_in, v_c_conv_w, v_c_a_log, v_c_dt_bias, v_c_o_norm_w):
    names = ["mem_norm_w", "w_mem_kv", "norm_pre", "norm_post", "w_out", "a_w_in", "a_ln_w", "a_ln_b", "a_w_s", "a_b_s",
             "b_w_in", "b_conv_w", "c_w_in", "c_conv_w", "c_a_log", "c_dt_bias", "c_o_norm_w"]
    w = dict(zip(names, [mem_norm_w, w_mem_kv, norm_pre, norm_post, w_out, a_w_in, a_ln_w, a_ln_b, a_w_s, a_b_s, b_w_in,
                         b_conv_w, c_w_in, c_conv_w, c_a_log, c_dt_bias, c_o_norm_w]))
    m = dict(zip(names, [m_mem_norm_w, m_w_mem_kv, m_norm_pre, m_norm_post, m_w_out, m_a_w_in, m_a_ln_w, m_a_ln_b, m_a_w_s,
                         m_a_b_s, m_b_w_in, m_b_conv_w, m_c_w_in, m_c_conv_w, m_c_a_log, m_c_dt_bias, m_c_o_norm_w]))
    v = dict(zip(names, [v_mem_norm_w, v_w_mem_kv, v_norm_pre, v_norm_post, v_w_out, v_a_w_in, v_a_ln_w, v_a_ln_b, v_a_w_s,
                         v_a_b_s, v_b_w_in, v_b_conv_w, v_c_w_in, v_c_conv_w, v_c_a_log, v_c_dt_bias, v_c_o_norm_w]))
    chip = 2 * lax.axis_index("x") + lax.axis_index("y")

    big = jnp.concatenate([a_w_in.reshape(_ROWS[0], D), b_w_in.reshape(_ROWS[1], D), c_w_in.reshape(_ROWS[2], D),
                           w_out.reshape(_ROWS[3], D), w_mem_kv, jnp.zeros((_BIG_PAD - _BIG_ROWS, D), F32)],
                          axis=0).astype(BF16)
    vec = jnp.concatenate([a_ln_w.reshape(-1), a_ln_b.reshape(-1), b_conv_w.reshape(-1), c_conv_w.reshape(-1)])
    vec = jnp.pad(vec, (0, 8 * D - vec.shape[0])).reshape(8, D)
    gbig, gvec = _gather_chips([big, vec])
    r0, r1, r2, r3, r4 = [sum(_ROWS[:k]) for k in range(5)]
    fa = gbig[:, r0:r1].reshape(4, 2, D, D).transpose(1, 2, 0, 3).reshape(2, D, 4 * D)
    fb = gbig[:, r1:r2].reshape(4, D, 1280).transpose(1, 0, 2).reshape(D, 5120)
    fc = gbig[:, r2:r3].reshape(4, D, 1284).transpose(1, 0, 2).reshape(D, 5136)
    fo = gbig[:, r3:r4].reshape(4, 4, 384, D).transpose(1, 0, 2, 3).reshape(4, D_CAT, D)
    fkv = gbig[:, r4:_BIG_ROWS].reshape(D, 2 * D_XA)
    gv = gvec.reshape(4, 8 * D)
    sm = {"mem_norm_w": mem_norm_w, "norm_pre": norm_pre, "norm_post": norm_post, "a_w_s": a_w_s, "a_b_s": a_b_s,
          "c_a_log": c_a_log, "c_dt_bias": c_dt_bias, "c_o_norm_w": c_o_norm_w,
          "a_ln_w": gv[:, 0:512].reshape(4, 2, 256).transpose(1, 0, 2).reshape(2, D),
          "a_ln_b": gv[:, 512:1024].reshape(4, 2, 256).transpose(1, 0, 2).reshape(2, D),
          "b_conv_w": gv[:, 1024:1792].reshape(4, 1, 3, 256).transpose(1, 2, 0, 3).reshape(1, 3, D),
          "c_conv_w": gv[:, 1792:4864].reshape(4, 1, 4, 768).transpose(1, 2, 0, 3).reshape(1, 4, 3 * D)}
    c_mix = jnp.concatenate([fc[:, :3 * D + 16], jnp.zeros((D, AB_PAD - 16), BF16)], axis=1)
    wts = {"wkv": fkv, "wo": fo,
           "wm": [fa[0][:, :2 * D], fb[:, :3 * D], c_mix, fa[1][:, :2 * D]],
           "wg": [fa[0][:, 2 * D:], fb[:, 3 * D:], fc[:, 3 * D + 16:], fa[1][:, 2 * D:]]}

    loss, dx, g = _local_step(x[0], mem[0], loss_target[0], wts, sm)
    loss = lax.psum(loss, ("x", "y", "c"))

    ga = jnp.stack([jnp.concatenate([g["wm"][i], g["wg"][i]], axis=1) for i in (0, 3)])
    gb = jnp.concatenate([g["wm"][1], g["wg"][1]], axis=1)
    gc = jnp.concatenate([g["wm"][2][:, :3 * D + 16], g["wg"][2]], axis=1)
    go = jnp.stack(g["wo"])
    gbig_all = jnp.concatenate([
        ga.reshape(2, D, 4, D).transpose(2, 0, 1, 3).reshape(4, _ROWS[0], D),
        gb.reshape(D, 4, 1280).transpose(1, 0, 2).reshape(4, _ROWS[1], D),
        gc.reshape(D, 4, 1284).transpose(1, 0, 2).reshape(4, _ROWS[2], D),
        go.reshape(4, 4, 384, D).transpose(1, 0, 2, 3).reshape(4, _ROWS[3], D),
        g["wkv"].reshape(4, _ROWS[4], D),
        jnp.zeros((4, 2 * HALF_ROWS - _BIG_ROWS, D), F32)], axis=1).reshape(4, 2, HALF_ROWS, D)
    gs = {"mem_norm_w": g["mem_norm_w"], "norm_pre": jnp.concatenate(g["norm_pre"]),
          "norm_post": jnp.concatenate(g["norm_post"])}
    for n in _SMALL[3:]:
        gs[n] = jnp.stack([g[n][j] for j in sorted(g[n])])
    flat = jnp.concatenate([gs[n].reshape(-1) for n in _SMALL])
    small = jnp.pad(flat, (0, _SMALL_ROWS * D - flat.shape[0])).reshape(_SMALL_ROWS, D)
    other_big, other_small = _swap_halves(gbig_all, small)
    pair_bf16, pair_own = _pair_sum(gbig_all, other_big)
    land_big, land_small = _chip_exchange(pair_bf16, _add(small, other_small, "pair_sum_small"))
    gshard = _share_half(_chip_sum(pair_own, land_big)).reshape(2 * HALF_ROWS, D)
    grads = {"a_w_in": gshard[r0:r1].reshape(a_w_in.shape), "b_w_in": gshard[r1:r2].reshape(b_w_in.shape),
             "c_w_in": gshard[r2:r3].reshape(c_w_in.shape), "w_out": gshard[r3:r4].reshape(w_out.shape),
             "w_mem_kv": gshard[r4:_BIG_ROWS]}
    flat = _sum4(land_small).reshape(-1)
    off = 0
    for n in _SMALL:
        shape = _SMALL_SHAPES[n]
        full = flat[off:off + _size(shape)].reshape(shape)
        off += _size(shape)
        if n in _SHARDED_SMALL:
            full = lax.dynamic_slice_in_dim(full, chip * _SHARDED_SMALL[n], _SHARDED_SMALL[n], axis=len(shape) - 1)
        grads[n] = full

    delta, new_m, new_v = {}, {}, {}
    for n in names:
        shape = w[n].shape
        view = (1, shape[0]) if len(shape) == 1 else (_size(shape[:-1]), shape[-1])
        d_, m_, v_ = _adamw(w[n].reshape(view), grads[n].reshape(view), m[n].reshape(view), v[n].reshape(view),
                            f"adamw_{n}")
        delta[n], new_m[n], new_v[n] = d_.reshape(shape), m_.reshape(shape), v_.reshape(shape)
    return (loss, dx[None], *[grads[n].reshape(w[n].shape) for n in names], *[delta[n] for n in names],
            *[new_m[n] for n in names], *[new_v[n] for n in names])
```

```python
import functools

import jax
import jax.numpy as jnp
from jax import lax
from jax.experimental import pallas as pl
from jax.experimental.pallas import tpu as pltpu

F32 = jnp.float32
BF16 = jnp.bfloat16
HI = lax.Precision.HIGHEST
MESH = pl.DeviceIdType.MESH
SDS = jax.ShapeDtypeStruct

D = 1024
D_XA = 512
D_CAT = 1536
N_MEM = 256
HD = 128
DEPTH = 4
EPS = 1e-6
TT = 256
DN_C = 64
DN_TB = 256
HALO = 8
AB_PAD = 128
VMEM_LIMIT = 56 * 1024 * 1024

ADAM_LR, ADAM_B1, ADAM_B2, ADAM_EPS, ADAM_WD, ADAM_STEP = 0.001, 0.9, 0.999, 1e-08, 0.01, 10


def _params(n_grid, vmem=None):
    return pltpu.CompilerParams(dimension_semantics=("arbitrary",) * n_grid, vmem_limit_bytes=vmem)


def _rms(x, w):
    return x * lax.rsqrt(jnp.mean(x * x, axis=-1, keepdims=True) + EPS) * w


def _dot_nn(a, b):
    return jnp.dot(a.astype(BF16), b.astype(BF16), preferred_element_type=F32)


def _dot_nt(a, b):
    return lax.dot_general(a.astype(BF16), b.astype(BF16), (((1,), (1,)), ((), ())), preferred_element_type=F32)


def _dot_tn(a, b):
    return lax.dot_general(a.astype(BF16), b.astype(BF16), (((0,), (0,)), ((), ())), preferred_element_type=F32)


@jax.custom_vjp
def mm(a, b):
    return _dot_nn(a, b)


mm.defvjp(lambda a, b: (_dot_nn(a, b), (a, b)), lambda r, g: (_dot_nt(g, r[1]), _dot_tn(r[0], g)))


@jax.custom_vjp
def mm_nt(a, b):
    return _dot_nt(a, b)


mm_nt.defvjp(lambda a, b: (_dot_nt(a, b), (a, b)), lambda r, g: (_dot_nn(g, r[1]), _dot_tn(g, r[0])))


@jax.custom_vjp
def mm_tn(a, b):
    return _dot_tn(a, b)


mm_tn.defvjp(lambda a, b: (_dot_tn(a, b), (a, b)), lambda r, g: (_dot_nt(r[1], g), _dot_nn(r[0], g)))


def _dot_hi(a, b):
    return jnp.dot(a, b, precision=HI, preferred_element_type=F32)


def _row_spec(width, tile=TT):
    return pl.BlockSpec((tile, width), lambda i: (i, 0))


def _full_spec(shape):
    return pl.BlockSpec(shape, lambda *_: (0,) * len(shape))


def _inproj_fwd(x, nw, wm, wg, name):
    T, M, G = x.shape[0], wm.shape[1], wg.shape[1]

    def body(x_ref, nw_ref, wm_ref, wg_ref, pm_ref, pg_ref, h_ref):
        h = _rms(x_ref[...], nw_ref[...]).astype(BF16)
        h_ref[...] = h
        pm_ref[...] = jnp.dot(h, wm_ref[...], preferred_element_type=F32)
        pg_ref[...] = jnp.dot(h, wg_ref[...], preferred_element_type=F32)

    return pl.pallas_call(
        body, name=name, grid=(T // TT,),
        in_specs=[_row_spec(D), _full_spec((1, D)), _full_spec((D, M)), _full_spec((D, G))],
        out_specs=[_row_spec(M), _row_spec(G), _row_spec(D)],
        out_shape=[SDS((T, M), F32), SDS((T, G), F32), SDS((T, D), BF16)],
        compiler_params=_params(1, VMEM_LIMIT))(x, nw, wm, wg)


def _inproj_bwd(dpm, dpg, x, nw, wm, wg, dxc, name):
    T, M, G = x.shape[0], wm.shape[1], wg.shape[1]

    def body(dpm_ref, dpg_ref, x_ref, nw_ref, wm_ref, wg_ref, dxc_ref, dx_ref, dnw_ref):
        dh = _dot_nt(dpm_ref[...], wm_ref[...]) + _dot_nt(dpg_ref[...], wg_ref[...])
        _, vjp = jax.vjp(_rms, x_ref[...], nw_ref[...])
        dxr, dnw = vjp(dh)
        dx_ref[...] = dxc_ref[...] + dxr

        @pl.when(pl.program_id(0) == 0)
        def _():
            dnw_ref[...] = jnp.zeros_like(dnw_ref)
        dnw_ref[...] += dnw

    return pl.pallas_call(
        body, name=name, grid=(T // TT,),
        in_specs=[_row_spec(M), _row_spec(G), _row_spec(D), _full_spec((1, D)), _full_spec((D, M)),
                  _full_spec((D, G)), _row_spec(D)],
        out_specs=[_row_spec(D), _full_spec((1, D))],
        out_shape=[SDS((T, D), F32), SDS((1, D), F32)],
        compiler_params=_params(1, VMEM_LIMIT))(dpm, dpg, x, nw, wm, wg, dxc)


def _matmul_tn(a, b, name):
    T, K = a.shape
    N = b.shape[1]
    tn = 512 if N % 512 == 0 else (640 if N % 640 == 0 else N)
    tt = min(512, T)

    def body(a_ref, b_ref, o_ref):
        @pl.when(pl.program_id(1) == 0)
        def _():
            o_ref[...] = jnp.zeros_like(o_ref)
        o_ref[...] += _dot_tn(a_ref[...], b_ref[...])

    return pl.pallas_call(
        body, name=name, grid=(N // tn, T // tt),
        in_specs=[pl.BlockSpec((tt, K), lambda j, t: (t, 0)), pl.BlockSpec((tt, tn), lambda j, t: (t, j))],
        out_specs=pl.BlockSpec((K, tn), lambda j, t: (0, j)),
        out_shape=SDS((K, N), F32),
        compiler_params=_params(2, VMEM_LIMIT))(a, b)


def _memkv_fn(mem, w, wkv):
    return mm(_rms(mem, w), wkv)


def _memkv_fwd(mem, w, wkv):
    def body(mem_ref, w_ref, wkv_ref, kv_ref):
        kv_ref[...] = _memkv_fn(mem_ref[...], w_ref[...], wkv_ref[...])

    return pl.pallas_call(body, name="memkv_fwd", out_shape=SDS((N_MEM, 2 * D_XA), F32),
                          compiler_params=_params(0, VMEM_LIMIT))(mem, w, wkv)


def _memkv_bwd(mem, w, wkv, dkv):
    def body(mem_ref, w_ref, wkv_ref, dkv_ref, dw_ref, dwkv_ref):
        _, vjp = jax.vjp(functools.partial(_memkv_fn, mem_ref[...]), w_ref[...], wkv_ref[...].astype(F32))
        dw, dwkv = vjp(dkv_ref[...])
        dw_ref[...] = dw
        dwkv_ref[...] = dwkv

    return pl.pallas_call(body, name="memkv_bwd", out_shape=[SDS((1, D), F32), SDS((D, 2 * D_XA), F32)],
                          compiler_params=_params(0, VMEM_LIMIT))(mem, w, wkv, dkv)


def _attn_gate(ymix, qx, z, *kvs):
    outs = []
    for j in range(4):
        s = mm_nt(qx[:, j * HD:(j + 1) * HD], kvs[j]) * (HD ** -0.5)
        e = jnp.exp(s - lax.stop_gradient(jnp.max(s, axis=-1, keepdims=True)))
        outs.append(mm(e / jnp.sum(e, axis=-1, keepdims=True), kvs[4 + j]))
    return jnp.concatenate([ymix] + outs, axis=1) * jax.nn.silu(z)


def _kv_blocks(kv_ref):
    return [kv_ref[:, j * HD:(j + 1) * HD] for j in range(8)]


def _ag_fwd(ymix, pg, kv, name):
    T = ymix.shape[0]

    def body(ymix_ref, pg_ref, kv_ref, ycat_ref):
        ycat_ref[...] = _attn_gate(ymix_ref[...], pg_ref[:, :D_XA], pg_ref[:, D_XA:], *_kv_blocks(kv_ref)).astype(BF16)

    return pl.pallas_call(
        body, name=name, grid=(T // TT,),
        in_specs=[_row_spec(D), _row_spec(D_XA + D_CAT), _full_spec((N_MEM, 2 * D_XA))],
        out_specs=_row_spec(D_CAT), out_shape=SDS((T, D_CAT), BF16),
        compiler_params=_params(1, VMEM_LIMIT))(ymix, pg, kv)


def _ag_bwd(dycat, ymix, pg, kv, dkv_in, name):
    T = ymix.shape[0]

    def body(dycat_ref, ymix_ref, pg_ref, kv_ref, dkvin_ref, dymix_ref, dpg_ref, dkv_ref):
        _, vjp = jax.vjp(_attn_gate, ymix_ref[...], pg_ref[:, :D_XA], pg_ref[:, D_XA:], *_kv_blocks(kv_ref))
        g = vjp(dycat_ref[...])
        dymix_ref[...] = g[0]
        dpg_ref[:, :D_XA] = g[1]
        dpg_ref[:, D_XA:] = g[2]

        @pl.when(pl.program_id(0) == 0)
        def _():
            dkv_ref[...] = dkvin_ref[...]
        for j in range(8):
            dkv_ref[:, j * HD:(j + 1) * HD] += g[3 + j]

    return pl.pallas_call(
        body, name=name, grid=(T // TT,),
        in_specs=[_row_spec(D_CAT), _row_spec(D), _row_spec(D_XA + D_CAT), _full_spec((N_MEM, 2 * D_XA)),
                  _full_spec((N_MEM, 2 * D_XA))],
        out_specs=[_row_spec(D), _row_spec(D_XA + D_CAT), _full_spec((N_MEM, 2 * D_XA))],
        out_shape=[SDS((T, D), F32), SDS((T, D_XA + D_CAT), F32), SDS((N_MEM, 2 * D_XA), F32)],
        compiler_params=_params(1, VMEM_LIMIT))(dycat, ymix, pg, kv, dkv_in)


def _outproj_fwd(ycat, wo, x, nw, name):
    T = x.shape[0]

    def body(ycat_ref, wo_ref, x_ref, nw_ref, o_ref, xn_ref):
        o = jnp.dot(ycat_ref[...], wo_ref[...], preferred_element_type=F32)
        o_ref[...] = o
        xn_ref[...] = x_ref[...] + _rms(o, nw_ref[...])

    return pl.pallas_call(
        body, name=name, grid=(T // TT,),
        in_specs=[_row_spec(D_CAT), _full_spec((D_CAT, D)), _row_spec(D), _full_spec((1, D))],
        out_specs=[_row_spec(D), _row_spec(D)], out_shape=[SDS((T, D), F32), SDS((T, D), F32)],
        compiler_params=_params(1, VMEM_LIMIT))(ycat, wo, x, nw)


def _outproj_bwd(dxo, o, nw, wo, name):
    T = dxo.shape[0]

    def body(dxo_ref, o_ref, nw_ref, wo_ref, dycat_ref, dobf_ref, dnw_ref):
        _, vjp = jax.vjp(_rms, o_ref[...], nw_ref[...])
        do, dnw = vjp(dxo_ref[...])
        dobf = do.astype(BF16)
        dobf_ref[...] = dobf
        dycat_ref[...] = _dot_nt(dobf, wo_ref[...])

        @pl.when(pl.program_id(0) == 0)
        def _():
            dnw_ref[...] = jnp.zeros_like(dnw_ref)
        dnw_ref[...] += dnw

    return pl.pallas_call(
        body, name=name, grid=(T // TT,),
        in_specs=[_row_spec(D), _row_spec(D), _full_spec((1, D)), _full_spec((D_CAT, D))],
        out_specs=[_row_spec(D_CAT), _row_spec(D), _full_spec((1, D))],
        out_shape=[SDS((T, D_CAT), F32), SDS((T, D), BF16), SDS((1, D), F32)],
        compiler_params=_params(1, VMEM_LIMIT))(dxo, o, nw, wo)


def _loss_head(xl, target):
    T = xl.shape[0]

    def body(x_ref, t_ref, loss_ref, dx_ref):
        err = x_ref[...] - t_ref[...]
        dx_ref[...] = err * (1.0 / D)

        @pl.when(pl.program_id(0) == 0)
        def _():
            loss_ref[...] = jnp.zeros_like(loss_ref)
        part = jnp.sum(jnp.sum(err * err, axis=1, keepdims=True), axis=0, keepdims=True) * (0.5 / D)
        loss_ref[...] += jnp.broadcast_to(part, loss_ref.shape)

    return pl.pallas_call(
        body, name="loss_head", grid=(T // TT,),
        in_specs=[_row_spec(D), _row_spec(D)],
        out_specs=[_full_spec((8, 128)), _row_spec(D)], out_shape=[SDS((8, 128), F32), SDS((T, D), F32)],
        compiler_params=_params(1))(xl, target)


def _gmlp_pre(u, v, lnw, lnb):
    vg = jax.nn.gelu(v)
    xc = vg - jnp.mean(vg, axis=-1, keepdims=True)
    vl = xc * lax.rsqrt(jnp.mean(xc * xc, axis=-1, keepdims=True) + EPS) * lnw + lnb
    return jax.nn.gelu(u), vl


def _tril(n, strict=False):
    r = lax.broadcasted_iota(jnp.int32, (n, n), 0)
    c = lax.broadcasted_iota(jnp.int32, (n, n), 1)
    return (r > c) if strict else (r >= c)


def _gmlp_fwd(pm, lnw, lnb, ws, bs3, name):
    T = pm.shape[0]

    def body(pm_ref, lnw_ref, lnb_ref, ws_ref, bs_ref, y_ref):
        ug, vl = _gmlp_pre(pm_ref[:, :D], pm_ref[:, D:], lnw_ref[...], lnb_ref[...])
        mask = _tril(HD)
        for g in range(8):
            w = jnp.where(mask, ws_ref[g], 0.0)
            for c in range(TT // HD):
                rows, cols = slice(c * HD, (c + 1) * HD), slice(g * HD, (g + 1) * HD)
                y_ref[rows, cols] = ug[rows, cols] * (_dot_nn(w, vl[rows, cols]) + bs_ref[g])

    return pl.pallas_call(
        body, name=name, grid=(T // TT,),
        in_specs=[_row_spec(2 * D), _full_spec((1, D)), _full_spec((1, D)), _full_spec((8, HD, HD)),
                  _full_spec((8, HD, HD))],
        out_specs=_row_spec(D), out_shape=SDS((T, D), F32),
        compiler_params=_params(1, VMEM_LIMIT))(pm, lnw, lnb, ws, bs3)


def _gmlp_bwd(dy, pm, lnw, lnb, ws, bs3, name):
    T = pm.shape[0]
    n_t = T // TT

    def body(dy_ref, pm_ref, lnw_ref, lnb_ref, ws_ref, bs_ref, dpm_ref, dlnw_ref, dlnb_ref, dws_ref, dbs_ref,
             dug_scr, dvl_scr, dbs_scr):
        i = pl.program_id(0)

        @pl.when(i == 0)
        def _():
            dlnw_ref[...] = jnp.zeros_like(dlnw_ref)
            dlnb_ref[...] = jnp.zeros_like(dlnb_ref)
            dws_ref[...] = jnp.zeros_like(dws_ref)
            dbs_scr[...] = jnp.zeros_like(dbs_scr)

        (ug, vl), vjp = jax.vjp(_gmlp_pre, pm_ref[:, :D], pm_ref[:, D:], lnw_ref[...], lnb_ref[...])
        mask = _tril(HD)
        for g in range(8):
            w = jnp.where(mask, ws_ref[g], 0.0)
            dw = jnp.zeros((HD, HD), F32)
            db = jnp.zeros((HD, HD), F32)
            for c in range(TT // HD):
                rows, cols = slice(c * HD, (c + 1) * HD), slice(g * HD, (g + 1) * HD)
                dyb, vlb = dy_ref[rows, cols], vl[rows, cols]
                sp = _dot_nn(w, vlb) + bs_ref[g]
                dsp = dyb * ug[rows, cols]
                dug_scr[rows, cols] = dyb * sp
                dvl_scr[rows, cols] = _dot_tn(w, dsp)
                dw += _dot_nt(dsp, vlb)
                db += dsp
            dws_ref[g] += jnp.where(mask, dw, 0.0)
            dbs_scr[g] += db
        du, dv, dlnw, dlnb = vjp((dug_scr[...], dvl_scr[...]))
        dpm_ref[:, :D] = du
        dpm_ref[:, D:] = dv
        dlnw_ref[...] += dlnw
        dlnb_ref[...] += dlnb

        @pl.when(i == n_t - 1)
        def _():
            for g in range(8):
                dbs_ref[g] = jnp.broadcast_to(jnp.sum(dbs_scr[g], axis=1, keepdims=True), (HD, HD))

    return pl.pallas_call(
        body, name=name, grid=(n_t,),
        in_specs=[_row_spec(D), _row_spec(2 * D), _full_spec((1, D)), _full_spec((1, D)), _full_spec((8, HD, HD)),
                  _full_spec((8, HD, HD))],
        out_specs=[_row_spec(2 * D), _full_spec((1, D)), _full_spec((1, D)), _full_spec((8, HD, HD)),
                   _full_spec((8, HD, HD))],
        out_shape=[SDS((T, 2 * D), F32), SDS((1, D), F32), SDS((1, D), F32), SDS((8, HD, HD), F32),
                   SDS((8, HD, HD), F32)],
        scratch_shapes=[pltpu.VMEM((TT, D), F32), pltpu.VMEM((TT, D), F32), pltpu.VMEM((8, HD, HD), F32)],
        compiler_params=_params(1, VMEM_LIMIT))(dy, pm, lnw, lnb, ws, bs3)


def _prev_spec(width, T):
    return pl.BlockSpec((HALO, width), lambda i: (jnp.maximum(i * (TT // HALO) - 1, 0), 0))


def _next_spec(width, T):
    return pl.BlockSpec((HALO, width), lambda i: (jnp.minimum((i + 1) * (TT // HALO), T // HALO - 1), 0))


def _rows_before(ext, j):
    return ext[HALO:] if j == 0 else pltpu.roll(ext, j, 0)[HALO:]


def _rows_after(ext, j):
    n = ext.shape[0]
    return ext[:n - HALO] if j == 0 else pltpu.roll(ext, n - j, 0)[:n - HALO]


def _conv_apply(ext_s, w):
    K = w.shape[0]
    y = _rows_before(ext_s, K - 1) * w[0:1]
    for k in range(1, K):
        y = y + _rows_before(ext_s, K - 1 - k) * w[k:k + 1]
    return y


def _conv_grads(ext_s, ext_dy, w):
    K = w.shape[0]
    dy = ext_dy[:ext_dy.shape[0] - HALO]
    ds = _rows_after(ext_dy, K - 1) * w[0:1]
    dws = [jnp.sum(dy * _rows_before(ext_s, K - 1), axis=0, keepdims=True)]
    for k in range(1, K):
        ds = ds + _rows_after(ext_dy, K - 1 - k) * w[k:k + 1]
        dws.append(jnp.sum(dy * _rows_before(ext_s, K - 1 - k), axis=0, keepdims=True))
    return ds, jnp.concatenate(dws, axis=0)


def _sconv_fwd(pm, w, name):
    T = pm.shape[0]

    def body(pm_ref, prev_ref, w_ref, y_ref):
        s = pm_ref[:, D:2 * D] * pm_ref[:, 2 * D:]
        sp = jnp.where(pl.program_id(0) > 0, prev_ref[:, D:2 * D] * prev_ref[:, 2 * D:], 0.0)
        y_ref[...] = pm_ref[:, :D] * _conv_apply(jnp.concatenate([sp, s], axis=0), w_ref[...])

    return pl.pallas_call(
        body, name=name, grid=(T // TT,),
        in_specs=[_row_spec(3 * D), _prev_spec(3 * D, T), _full_spec((3, D))],
        out_specs=_row_spec(D), out_shape=SDS((T, D), F32),
        compiler_params=_params(1, VMEM_LIMIT))(pm, pm, w)


def _sconv_bwd(dy, pm, w, name):
    T = pm.shape[0]
    n_t = T // TT

    def body(dy_ref, dyn_ref, pm_ref, prev_ref, next_ref, w_ref, dpm_ref, dw_ref):
        i = pl.program_id(0)
        bg, cg, hv = pm_ref[:, :D], pm_ref[:, D:2 * D], pm_ref[:, 2 * D:]
        sp = jnp.where(i > 0, prev_ref[:, D:2 * D] * prev_ref[:, 2 * D:], 0.0)
        ext_s = jnp.concatenate([sp, cg * hv], axis=0)
        dyv = dy_ref[...]
        dcn = jnp.where(i < n_t - 1, dyn_ref[...] * next_ref[:, :D], 0.0)
        ds, dw = _conv_grads(ext_s, jnp.concatenate([dyv * bg, dcn], axis=0), w_ref[...])
        dpm_ref[:, :D] = dyv * _conv_apply(ext_s, w_ref[...])
        dpm_ref[:, D:2 * D] = ds * hv
        dpm_ref[:, 2 * D:] = ds * cg

        @pl.when(i == 0)
        def _():
            dw_ref[...] = jnp.zeros_like(dw_ref)
        dw_ref[...] += dw

    return pl.pallas_call(
        body, name=name, grid=(n_t,),
        in_specs=[_row_spec(D), _next_spec(D, T), _row_spec(3 * D), _prev_spec(3 * D, T), _next_spec(3 * D, T),
                  _full_spec((3, D))],
        out_specs=[_row_spec(3 * D), _full_spec((3, D))],
        out_shape=[SDS((T, 3 * D), F32), SDS((3, D), F32)],
        compiler_params=_params(1, VMEM_LIMIT))(dy, dy, pm, pm, pm, w)


def _dnconv_fwd(pm, w, name):
    T = pm.shape[0]

    def body(pm_ref, prev_ref, w_ref, c_ref):
        sp = jnp.where(pl.program_id(0) > 0, prev_ref[...], 0.0)
        c_ref[...] = _conv_apply(jnp.concatenate([sp, pm_ref[...]], axis=0), w_ref[...])

    return pl.pallas_call(
        body, name=name, grid=(T // TT,),
        in_specs=[_row_spec(3 * D), _prev_spec(3 * D, T), _full_spec((4, 3 * D))],
        out_specs=_row_spec(3 * D), out_shape=SDS((T, 3 * D), F32),
        compiler_params=_params(1, VMEM_LIMIT))(pm, pm, w)


def _dnconv_bwd(dcq, dck, dcv, dab, pm, w, name):
    T = pm.shape[0]
    n_t = T // TT

    def body(dq_ref, dk_ref, dv_ref, dqn_ref, dkn_ref, dvn_ref, dab_ref, pm_ref, prev_ref, w_ref, dpm_ref, dw_ref):
        i = pl.program_id(0)
        sp = jnp.where(i > 0, prev_ref[...], 0.0)
        ext_s = jnp.concatenate([sp, pm_ref[...]], axis=0)
        own = jnp.concatenate([dq_ref[...], dk_ref[...], dv_ref[...]], axis=1)
        nxt = jnp.where(i < n_t - 1, jnp.concatenate([dqn_ref[...], dkn_ref[...], dvn_ref[...]], axis=1), 0.0)
        ds, dw = _conv_grads(ext_s, jnp.concatenate([own, nxt], axis=0), w_ref[...])
        dpm_ref[:, :3 * D] = ds
        dpm_ref[:, 3 * D:] = dab_ref[...]

        @pl.when(i == 0)
        def _():
            dw_ref[...] = jnp.zeros_like(dw_ref)
        dw_ref[...] += dw

    return pl.pallas_call(
        body, name=name, grid=(n_t,),
        in_specs=[_row_spec(D), _row_spec(D), _row_spec(D), _next_spec(D, T), _next_spec(D, T), _next_spec(D, T),
                  _row_spec(AB_PAD), _row_spec(3 * D), _prev_spec(3 * D, T), _full_spec((4, 3 * D))],
        out_specs=[_row_spec(3 * D + AB_PAD), _full_spec((4, 3 * D))],
        out_shape=[SDS((T, 3 * D + AB_PAD), F32), SDS((4, 3 * D), F32)],
        compiler_params=_params(1, VMEM_LIMIT))(dcq, dck, dcv, dcq, dck, dcv, dab, pm, pm, w)


def _l2n(x):
    return x * lax.rsqrt(jnp.sum(x * x, axis=-1, keepdims=True) + EPS)


def _softplus(x):
    return jnp.maximum(x, 0.0) + jnp.log1p(jnp.exp(-jnp.abs(x)))


def _dn_gates(ab, alog, dtb, h):
    lane = lax.broadcasted_iota(jnp.int32, ab.shape, 1)
    g_all = -jnp.exp(alog) * _softplus(ab + dtb)
    g = jnp.sum(jnp.where(lane == h, g_all, 0.0), axis=1, keepdims=True)
    beta = jnp.sum(jnp.where(lane == 8 + h, jax.nn.sigmoid(ab), 0.0), axis=1, keepdims=True)
    ones = jnp.ones((1, HD), F32)
    return g * ones, beta * ones


def _dn_chunk(cq, ck, cv, gb, bb, S, onw):
    C = DN_C
    q = _l2n(jax.nn.silu(cq)) * (HD ** -0.5)
    k = _l2n(jax.nn.silu(ck))
    v = jax.nn.silu(cv)
    incl, strict = _tril(C), _tril(C, strict=True)
    gcum = _dot_hi(incl.astype(F32), gb)
    gi = gcum[:, :C]
    gj = gcum.T[:C, :]
    decay = jnp.where(incl, jnp.exp(jnp.where(incl, gi - gj, 0.0)), 0.0)
    kb = k * bb
    a_mat = jnp.where(strict, mm_nt(kb, k) * decay, 0.0)
    p = -a_mat
    eye = (lax.broadcasted_iota(jnp.int32, (C, C), 0) == lax.broadcasted_iota(jnp.int32, (C, C), 1)).astype(F32)
    t_mat = eye + p
    for _ in range(5):
        p = _dot_hi(p, p)
        t_mat = t_mat + _dot_hi(t_mat, p)
    eg = jnp.exp(gcum)
    u = mm(t_mat, v * bb)
    w = mm(t_mat, kb * eg)
    qk = mm_nt(q, k) * decay
    glast = gcum[C - 1:C, :]
    v_new = u - mm(w, S)
    o = mm(q * eg, S) + mm(qk, v_new)
    s_new = S * jnp.exp(glast) + mm_tn(k * jnp.exp(glast - gcum), v_new)
    return _rms(o, onw), s_new


def _dn_specs(T, rev):
    nb = T // DN_TB
    blk = (lambda n: nb - 1 - n) if rev else (lambda n: n)
    head = [pl.BlockSpec((DN_TB, HD), functools.partial(lambda n, h, off: (blk(n), off + h), off=8 * s)) for s in range(3)]
    ab = pl.BlockSpec((DN_TB, AB_PAD), lambda n, h: (blk(n), 3 * D // AB_PAD))
    st = pl.BlockSpec((DN_TB // DN_C, None, HD, HD), lambda n, h: (blk(n), h, 0, 0))
    out = pl.BlockSpec((DN_TB, HD), lambda n, h: (blk(n), h))
    row = pl.BlockSpec((1, HD), lambda n, h: (0, 0))
    return nb, head, ab, st, out, row


def _dn_fwd(cpre, pm, alog, dtb, onw, name):
    T = cpre.shape[0]
    nb, head, ab, st, out, row = _dn_specs(T, False)

    def body(cq_ref, ck_ref, cv_ref, ab_ref, alog_ref, dtb_ref, onw_ref, o_ref, st_ref, s_scr):
        n, h = pl.program_id(0), pl.program_id(1)

        @pl.when(n == 0)
        def _():
            s_scr[h] = jnp.zeros((HD, HD), F32)
        gb, bb = _dn_gates(ab_ref[...], alog_ref[...], dtb_ref[...], h)
        S = s_scr[h]
        for c in range(DN_TB // DN_C):
            rows = slice(c * DN_C, (c + 1) * DN_C)
            st_ref[c] = S
            o, S = _dn_chunk(cq_ref[rows, :], ck_ref[rows, :], cv_ref[rows, :], gb[rows], bb[rows], S, onw_ref[...])
            o_ref[rows, :] = o
        s_scr[h] = S

    return pl.pallas_call(
        body, name=name, grid=(nb, 8),
        in_specs=head + [ab, row, row, row], out_specs=[out, st],
        out_shape=[SDS((T, D), F32), SDS((T // DN_C, 8, HD, HD), F32)],
        scratch_shapes=[pltpu.VMEM((8, HD, HD), F32)],
        compiler_params=_params(2, VMEM_LIMIT))(cpre, cpre, cpre, pm, alog, dtb, onw)


def _dn_bwd(do, cpre, pm, st, alog, dtb, onw, name):
    T = cpre.shape[0]
    nb, head, ab, stspec, out, row = _dn_specs(T, True)

    def body(do_ref, cq_ref, ck_ref, cv_ref, ab_ref, st_ref, alog_ref, dtb_ref, onw_ref,
             dcq_ref, dck_ref, dcv_ref, dab_ref, dalog_ref, ddtb_ref, donw_ref, ds_scr):
        n, h = pl.program_id(0), pl.program_id(1)

        @pl.when(n == 0)
        def _():
            ds_scr[h] = jnp.zeros((HD, HD), F32)

        @pl.when((n == 0) & (h == 0))
        def _():
            dalog_ref[...] = jnp.zeros_like(dalog_ref)
            ddtb_ref[...] = jnp.zeros_like(ddtb_ref)
            donw_ref[...] = jnp.zeros_like(donw_ref)

        @pl.when(h == 0)
        def _():
            dab_ref[...] = jnp.zeros_like(dab_ref)

        (gb, bb), gates_vjp = jax.vjp(lambda a, b, c: _dn_gates(a, b, c, h), ab_ref[...], alog_ref[...], dtb_ref[...])
        dS = ds_scr[h]
        n_c = DN_TB // DN_C
        dgs, dbs = [None] * n_c, [None] * n_c
        donw = jnp.zeros((1, HD), F32)
        for c in reversed(range(n_c)):
            rows = slice(c * DN_C, (c + 1) * DN_C)
            _, vjp = jax.vjp(_dn_chunk, cq_ref[rows, :], ck_ref[rows, :], cv_ref[rows, :], gb[rows], bb[rows],
                             st_ref[c], onw_ref[...])
            dcq, dck, dcv, dgs[c], dbs[c], dS, dn = vjp((do_ref[rows, :], dS))
            dcq_ref[rows, :] = dcq
            dck_ref[rows, :] = dck
            dcv_ref[rows, :] = dcv
            donw += dn
        ds_scr[h] = dS
        dab, dalog, ddtb = gates_vjp((jnp.concatenate(dgs, axis=0), jnp.concatenate(dbs, axis=0)))
        dab_ref[...] += dab
        dalog_ref[...] += dalog
        ddtb_ref[...] += ddtb
        donw_ref[...] += donw

    dabspec = pl.BlockSpec((DN_TB, AB_PAD), lambda n, h: (nb - 1 - n, 0))
    return pl.pallas_call(
        body, name=name, grid=(nb, 8),
        in_specs=[out] + head + [ab, stspec, row, row, row],
        out_specs=[out, out, out, dabspec, row, row, row],
        out_shape=[SDS((T, D), F32)] * 3 + [SDS((T, AB_PAD), F32)] + [SDS((1, HD), F32)] * 3,
        scratch_shapes=[pltpu.VMEM((8, HD, HD), F32)],
        compiler_params=_params(2, VMEM_LIMIT))(do, cpre, cpre, cpre, pm, st, alog, dtb, onw)


_BNN = (((2,), (1,)), ((0,), (0,)))
_BNT = (((2,), (2,)), ((0,), (0,)))
_BTN = (((1,), (1,)), ((0,), (0,)))


def _bdot(a, b, dims):
    return lax.dot_general(a.astype(BF16), b.astype(BF16), dims, preferred_element_type=F32)


def _bdot3(a, b, dims):
    ah, bh = a.astype(BF16), b.astype(BF16)
    al, bl = (a - ah.astype(F32)).astype(BF16), (b - bh.astype(F32)).astype(BF16)
    d = functools.partial(lax.dot_general, dimension_numbers=dims, preferred_element_type=F32)
    return d(ah, bh) + (d(ah, bl) + d(al, bh))


def _bdot_hi(a, b, dims):
    return lax.dot_general(a, b, dims, precision=HI, preferred_element_type=F32)


def _batched_matmuls(dot):
    @jax.custom_vjp
    def nn(a, b):
        return dot(a, b, _BNN)

    @jax.custom_vjp
    def nt(a, b):
        return dot(a, b, _BNT)

    @jax.custom_vjp
    def tn(a, b):
        return dot(a, b, _BTN)

    nn.defvjp(lambda a, b: (dot(a, b, _BNN), (a, b)), lambda r, g: (dot(g, r[1], _BNT), dot(r[0], g, _BTN)))
    nt.defvjp(lambda a, b: (dot(a, b, _BNT), (a, b)), lambda r, g: (dot(g, r[1], _BNN), dot(g, r[0], _BTN)))
    tn.defvjp(lambda a, b: (dot(a, b, _BTN), (a, b)), lambda r, g: (dot(r[1], g, _BNT), dot(r[0], g, _BNN)))
    return nn, nt, tn


bmm, bmm_nt, bmm_tn = _batched_matmuls(_bdot)
bmm3, _, _ = _batched_matmuls(_bdot3)
bmm_hi, bmm_hi_nt, _ = _batched_matmuls(_bdot_hi)

@jax.custom_vjp
def _neumann_inverse(n):
    C = n.shape[1]
    eye = lax.broadcasted_iota(jnp.int32, n.shape, 1) == lax.broadcasted_iota(jnp.int32, n.shape, 2)
    t = eye.astype(F32) + n
    for _ in range(5):
        n = _bdot3(n, n, _BNN)
        t = t + _bdot3(t, n, _BNN)
    return t


def _neumann_inverse_fwd(n):
    t = _neumann_inverse(n)
    return t, t


def _neumann_inverse_bwd(t, g):
    return (_bdot3(_bdot3(t, g, _BTN), t, _BNT),)


_neumann_inverse.defvjp(_neumann_inverse_fwd, _neumann_inverse_bwd)

DN_NCH = DN_TB // DN_C
DN_NH = 4


def _dn_prep(cq, ck, cv, gb, bb):
    B, C = cq.shape[0], DN_C
    q = _l2n(jax.nn.silu(cq)) * (HD ** -0.5)
    k = _l2n(jax.nn.silu(ck))
    v = jax.nn.silu(cv)
    r = lax.broadcasted_iota(jnp.int32, (B, C, C), 1)
    c = lax.broadcasted_iota(jnp.int32, (B, C, C), 2)
    incl, strict = r >= c, r > c
    gcum = bmm_hi(incl.astype(F32), gb)
    lane0 = lax.broadcasted_iota(jnp.int32, (B, C, HD), 2) == 0
    gj = bmm_hi_nt(jnp.ones((B, C, HD), F32), jnp.where(lane0, gcum, 0.0))
    decay = jnp.where(incl, jnp.exp(jnp.where(incl, gcum[:, :, :C] - gj, 0.0)), 0.0)
    kb = k * bb
    t_mat = _neumann_inverse(-jnp.where(strict, bmm_nt(kb, k) * decay, 0.0))
    eg = jnp.exp(gcum)
    glast = gcum[:, C - 1:C, :]
    return (bmm(t_mat, v * bb), bmm(t_mat, kb * eg), bmm_nt(q, k) * decay, q * eg, k * jnp.exp(glast - gcum),
            jnp.exp(glast))


def _dn_scan_step(u, w, qk, qd, kd, egl, S, onw):
    v_new = u - bmm(w, S)
    o = bmm(qd, S) + bmm(qk, v_new)
    return _rms(o, onw), S * egl + bmm_tn(kd, v_new)


def _head_gates(ab, alog, dtb, first_head, n_heads):
    gs, bs = [], []
    for i in range(n_heads):
        g, b = _dn_gates(ab, alog, dtb, first_head + i)
        gs.append(g.reshape(DN_NCH, DN_C, HD))
        bs.append(b.reshape(DN_NCH, DN_C, HD))
    return jnp.concatenate(gs, axis=0), jnp.concatenate(bs, axis=0)


def _to_batch(ref, n_heads):
    return jnp.concatenate([ref[:, i * HD:(i + 1) * HD].astype(F32).reshape(DN_NCH, DN_C, HD) for i in range(n_heads)],
                           axis=0)


def _from_batch(ref, val, n_heads):
    for i in range(n_heads):
        ref[:, i * HD:(i + 1) * HD] = val[i * DN_NCH:(i + 1) * DN_NCH].reshape(DN_TB, HD).astype(ref.dtype)


def _prep_specs(T, rev):
    nb = T // DN_TB
    blk = (lambda n: nb - 1 - n) if rev else (lambda n: n)
    ng = 8 // DN_NH
    head = [pl.BlockSpec((DN_TB, DN_NH * HD), functools.partial(lambda n, h, off: (blk(n), off + h), off=ng * s))
            for s in range(3)]
    ab = pl.BlockSpec((DN_TB, AB_PAD), lambda n, h: (blk(n), 3 * D // AB_PAD))
    row = pl.BlockSpec((1, HD), lambda n, h: (0, 0))
    wide = pl.BlockSpec((DN_TB, DN_NH * HD), lambda n, h: (blk(n), h))
    qk = pl.BlockSpec((DN_NCH, DN_NH, DN_C, DN_C), lambda n, h: (blk(n), h, 0, 0))
    eg = pl.BlockSpec((DN_NCH, DN_NH, 1, HD), lambda n, h: (blk(n), h, 0, 0))
    return nb, ng, head, ab, row, wide, qk, eg


def _dn_prep_fwd(cpre, pm, alog, dtb, name):
    T = cpre.shape[0]
    nb, ng, head, ab, row, wide, qks, egs = _prep_specs(T, False)

    def body(cq_ref, ck_ref, cv_ref, ab_ref, alog_ref, dtb_ref, u_ref, w_ref, qk_ref, qd_ref, kd_ref, e_ref):
        gb, bb = _head_gates(ab_ref[...], alog_ref[...], dtb_ref[...], pl.program_id(1) * DN_NH, DN_NH)
        u, w, qk, qd, kd, egl = _dn_prep(_to_batch(cq_ref, DN_NH), _to_batch(ck_ref, DN_NH), _to_batch(cv_ref, DN_NH),
                                         gb, bb)
        _from_batch(u_ref, u, DN_NH)
        _from_batch(w_ref, w, DN_NH)
        _from_batch(qd_ref, qd, DN_NH)
        _from_batch(kd_ref, kd, DN_NH)
        for i in range(DN_NH):
            qk_ref[:, i] = qk[i * DN_NCH:(i + 1) * DN_NCH].astype(BF16)
            e_ref[:, i] = egl[i * DN_NCH:(i + 1) * DN_NCH]

    return pl.pallas_call(
        body, name=name, grid=(nb, ng), in_specs=head + [ab, row, row],
        out_specs=[wide, wide, qks, wide, wide, egs],
        out_shape=[SDS((T, D), F32), SDS((T, D), BF16), SDS((T // DN_C, 8, DN_C, DN_C), BF16), SDS((T, D), BF16),
                   SDS((T, D), BF16), SDS((T // DN_C, 8, 1, HD), F32)],
        compiler_params=_params(2, VMEM_LIMIT))(cpre, cpre, cpre, pm, alog, dtb)


def _dn_prep_bwd(du, dw, dqk, dqd, dkd, degl, cpre, pm, alog, dtb, name):
    T = cpre.shape[0]
    nb, ng, head, ab, row, wide, qks, egs = _prep_specs(T, True)

    def body(du_ref, dw_ref, dqk_ref, dqd_ref, dkd_ref, de_ref, cq_ref, ck_ref, cv_ref, ab_ref, alog_ref, dtb_ref,
             dcq_ref, dck_ref, dcv_ref, dab_ref, dalog_ref, ddtb_ref):
        n, h = pl.program_id(0), pl.program_id(1)

        @pl.when((n == 0) & (h == 0))
        def _():
            dalog_ref[...] = jnp.zeros_like(dalog_ref)
            ddtb_ref[...] = jnp.zeros_like(ddtb_ref)

        @pl.when(h == 0)
        def _():
            dab_ref[...] = jnp.zeros_like(dab_ref)

        def fwd(cq, ck, cv, ab_v, alog_v, dtb_v):
            gb, bb = _head_gates(ab_v, alog_v, dtb_v, h * DN_NH, DN_NH)
            return _dn_prep(cq, ck, cv, gb, bb)

        _, vjp = jax.vjp(fwd, _to_batch(cq_ref, DN_NH), _to_batch(ck_ref, DN_NH), _to_batch(cv_ref, DN_NH), ab_ref[...],
                         alog_ref[...], dtb_ref[...])
        cot = (_to_batch(du_ref, DN_NH), _to_batch(dw_ref, DN_NH),
               jnp.concatenate([dqk_ref[:, i] for i in range(DN_NH)], axis=0), _to_batch(dqd_ref, DN_NH),
               _to_batch(dkd_ref, DN_NH), jnp.concatenate([de_ref[:, i] for i in range(DN_NH)], axis=0))
        dcq, dck, dcv, dab, dalog, ddtb = vjp(cot)
        _from_batch(dcq_ref, dcq, DN_NH)
        _from_batch(dck_ref, dck, DN_NH)
        _from_batch(dcv_ref, dcv, DN_NH)
        dab_ref[...] += dab
        dalog_ref[...] += dalog
        ddtb_ref[...] += ddtb

    dabspec = pl.BlockSpec((DN_TB, AB_PAD), lambda n, h: (nb - 1 - n, 0))
    return pl.pallas_call(
        body, name=name, grid=(nb, ng),
        in_specs=[wide, wide, qks, wide, wide, egs] + head + [ab, row, row],
        out_specs=[wide, wide, wide, dabspec, row, row],
        out_shape=[SDS((T, D), F32)] * 3 + [SDS((T, AB_PAD), F32)] + [SDS((1, HD), F32)] * 2,
        compiler_params=_params(2, VMEM_LIMIT))(du, dw, dqk, dqd, dkd, degl, cpre, cpre, cpre, pm, alog, dtb)


def _scan_specs(T, rev):
    nb = T // DN_TB
    blk = (lambda n: nb - 1 - n) if rev else (lambda n: n)
    wide = pl.BlockSpec((DN_TB, D), lambda n: (blk(n), 0))
    qk = pl.BlockSpec((DN_NCH, 8, DN_C, DN_C), lambda n: (blk(n), 0, 0, 0))
    eg = pl.BlockSpec((DN_NCH, 8, 1, HD), lambda n: (blk(n), 0, 0, 0))
    st = pl.BlockSpec((DN_NCH, 8, HD, HD), lambda n: (blk(n), 0, 0, 0))
    row = pl.BlockSpec((1, HD), lambda n: (0, 0))
    return nb, wide, qk, eg, st, row


def _heads_of(ref, rows):
    return jnp.concatenate([ref[rows, h * HD:(h + 1) * HD].astype(F32)[None] for h in range(8)], axis=0)


def _dn_scan_fwd(u, w, qk, qd, kd, egl, onw, name):
    T = u.shape[0]
    nb, wide, qks, egs, sts, row = _scan_specs(T, False)

    def body(u_ref, w_ref, qk_ref, qd_ref, kd_ref, e_ref, onw_ref, o_ref, st_ref, s_scr):
        @pl.when(pl.program_id(0) == 0)
        def _():
            s_scr[...] = jnp.zeros_like(s_scr)
        S = s_scr[...]
        for c in range(DN_NCH):
            rows = slice(c * DN_C, (c + 1) * DN_C)
            st_ref[c] = S
            o, S = _dn_scan_step(_heads_of(u_ref, rows), _heads_of(w_ref, rows), qk_ref[c].astype(F32),
                                 _heads_of(qd_ref, rows), _heads_of(kd_ref, rows), e_ref[c], S, onw_ref[...])
            for h in range(8):
                o_ref[rows, h * HD:(h + 1) * HD] = o[h]
        s_scr[...] = S

    return pl.pallas_call(
        body, name=name, grid=(nb,), in_specs=[wide, wide, qks, wide, wide, egs, row], out_specs=[wide, sts],
        out_shape=[SDS((T, D), F32), SDS((T // DN_C, 8, HD, HD), F32)],
        scratch_shapes=[pltpu.VMEM((8, HD, HD), F32)],
        compiler_params=_params(1, VMEM_LIMIT))(u, w, qk, qd, kd, egl, onw)


def _dn_scan_bwd(do, u, w, qk, qd, kd, egl, st, onw, name):
    T = u.shape[0]
    nb, wide, qks, egs, sts, row = _scan_specs(T, True)

    def body(do_ref, u_ref, w_ref, qk_ref, qd_ref, kd_ref, e_ref, st_ref, onw_ref,
             du_ref, dw_ref, dqk_ref, dqd_ref, dkd_ref, de_ref, donw_ref, ds_scr):
        @pl.when(pl.program_id(0) == 0)
        def _():
            ds_scr[...] = jnp.zeros_like(ds_scr)
            donw_ref[...] = jnp.zeros_like(donw_ref)
        dS = ds_scr[...]
        donw = jnp.zeros((1, HD), F32)
        for c in reversed(range(DN_NCH)):
            rows = slice(c * DN_C, (c + 1) * DN_C)
            _, vjp = jax.vjp(_dn_scan_step, _heads_of(u_ref, rows), _heads_of(w_ref, rows), qk_ref[c].astype(F32),
                             _heads_of(qd_ref, rows), _heads_of(kd_ref, rows), e_ref[c], st_ref[c], onw_ref[...])
            du, dw, dqk, dqd, dkd, de, dS, dn = vjp((_heads_of(do_ref, rows), dS))
            for h in range(8):
                cols = slice(h * HD, (h + 1) * HD)
                du_ref[rows, cols] = du[h]
                dw_ref[rows, cols] = dw[h]
                dqd_ref[rows, cols] = dqd[h]
                dkd_ref[rows, cols] = dkd[h]
            dqk_ref[c] = dqk
            de_ref[c] = de
            donw += dn
        ds_scr[...] = dS
        donw_ref[...] += donw

    return pl.pallas_call(
        body, name=name, grid=(nb,), in_specs=[wide, wide, wide, qks, wide, wide, egs, sts, row],
        out_specs=[wide, wide, qks, wide, wide, egs, row],
        out_shape=[SDS((T, D), F32), SDS((T, D), F32), SDS((T // DN_C, 8, DN_C, DN_C), F32), SDS((T, D), F32),
                   SDS((T, D), F32), SDS((T // DN_C, 8, 1, HD), F32), SDS((1, HD), F32)],
        scratch_shapes=[pltpu.VMEM((8, HD, HD), F32)],
        compiler_params=_params(1, VMEM_LIMIT))(do, u, w, qk, qd, kd, egl, st, onw)


def _adamw(w, g, m, v, name):
    R, C = w.shape
    tr = 256 if R % 256 == 0 and R > 256 else R
    c1 = 1.0 - ADAM_B1 ** ADAM_STEP
    c2 = 1.0 - ADAM_B2 ** ADAM_STEP

    def body(w_ref, g_ref, m_ref, v_ref, d_ref, nm_ref, nv_ref):
        gv = g_ref[...]
        nm = ADAM_B1 * m_ref[...] + (1.0 - ADAM_B1) * gv
        nv = ADAM_B2 * v_ref[...] + (1.0 - ADAM_B2) * (gv * gv)
        nm_ref[...] = nm
        nv_ref[...] = nv
        d_ref[...] = -ADAM_LR * ((nm / c1) / (jnp.sqrt(nv / c2) + ADAM_EPS) + ADAM_WD * w_ref[...])

    spec = pl.BlockSpec((tr, C), lambda i: (i, 0))
    return pl.pallas_call(
        body, name=name, grid=(R // tr,), in_specs=[spec] * 4, out_specs=[spec] * 3,
        out_shape=[SDS((R, C), F32)] * 3, compiler_params=_params(1, VMEM_LIMIT))(w, g, m, v)


def _local_step(x, mem, target, wts, sm):
    kinds = [i % 3 for i in range(DEPTH)]
    mnw = sm["mem_norm_w"].reshape(1, D)
    kv = _memkv_fwd(mem, mnw, wts["wkv"])
    saved = []
    for i, kind in enumerate(kinds):
        j = i // 3
        npre = sm["norm_pre"][i].reshape(1, D)
        npost = sm["norm_post"][i].reshape(1, D)
        pm, pg, h = _inproj_fwd(x, npre, wts["wm"][i], wts["wg"][i], f"inproj_fwd_{i}")
        extra = None
        if kind == 0:
            bs3 = jnp.broadcast_to(sm["a_b_s"][j][:, :, None], (8, HD, HD))
            ymix = _gmlp_fwd(pm, sm["a_ln_w"][j].reshape(1, D), sm["a_ln_b"][j].reshape(1, D), sm["a_w_s"][j], bs3,
                             f"gmlp_fwd_{i}")
            extra = bs3
        elif kind == 1:
            ymix = _sconv_fwd(pm, sm["b_conv_w"][j], f"sconv_fwd_{i}")
        else:
            cpre = _dnconv_fwd(pm, sm["c_conv_w"][j], f"dnconv_fwd_{i}")
            alog = jnp.pad(sm["c_a_log"][j], (0, HD - 8)).reshape(1, HD)
            dtb = jnp.pad(sm["c_dt_bias"][j], (0, HD - 8)).reshape(1, HD)
            onw = sm["c_o_norm_w"][j].reshape(1, HD)
            prep = _dn_prep_fwd(cpre, pm, alog, dtb, f"dn_prep_fwd_{i}")
            ymix, st = _dn_scan_fwd(*prep, onw, f"dn_scan_fwd_{i}")
            extra = (cpre, prep, st, alog, dtb, onw)
        ycat = _ag_fwd(ymix, pg, kv, f"ag_fwd_{i}")
        o, xn = _outproj_fwd(ycat, wts["wo"][i], x, npost, f"outproj_fwd_{i}")
        saved.append((x, h, pm, pg, ymix, ycat, o, extra))
        x = xn

    loss, dx = _loss_head(x, target)

    g = {"wm": [None] * DEPTH, "wg": [None] * DEPTH, "wo": [None] * DEPTH, "norm_pre": [None] * DEPTH,
         "norm_post": [None] * DEPTH}
    dkv = jnp.zeros((N_MEM, 2 * D_XA), F32)
    for i in reversed(range(DEPTH)):
        kind, j = kinds[i], i // 3
        xi, h, pm, pg, ymix, ycat, o, extra = saved[i]
        npre = sm["norm_pre"][i].reshape(1, D)
        npost = sm["norm_post"][i].reshape(1, D)
        dycat, dobf, g["norm_post"][i] = _outproj_bwd(dx, o, npost, wts["wo"][i], f"outproj_bwd_{i}")
        g["wo"][i] = _matmul_tn(ycat, dobf, f"dwo_{i}")
        dymix, dpg, dkv = _ag_bwd(dycat, ymix, pg, kv, dkv, f"ag_bwd_{i}")
        if kind == 0:
            dpm, dlnw, dlnb, dws, dbs3 = _gmlp_bwd(dymix, pm, sm["a_ln_w"][j].reshape(1, D),
                                                   sm["a_ln_b"][j].reshape(1, D), sm["a_w_s"][j], extra,
                                                   f"gmlp_bwd_{i}")
            g.setdefault("a_ln_w", {})[j] = dlnw.reshape(D)
            g.setdefault("a_ln_b", {})[j] = dlnb.reshape(D)
            g.setdefault("a_w_s", {})[j] = dws
            g.setdefault("a_b_s", {})[j] = dbs3[:, :, 0]
        elif kind == 1:
            dpm, dcw = _sconv_bwd(dymix, pm, sm["b_conv_w"][j], f"sconv_bwd_{i}")
            g.setdefault("b_conv_w", {})[j] = dcw
        else:
            cpre, prep, st, alog, dtb, onw = extra
            *dprep, donw = _dn_scan_bwd(dymix, *prep, st, onw, f"dn_scan_bwd_{i}")
            dcq, dck, dcv, dab, dalog, ddtb = _dn_prep_bwd(*dprep, cpre, pm, alog, dtb, f"dn_prep_bwd_{i}")
            dpm, dcw = _dnconv_bwd(dcq, dck, dcv, dab, pm, sm["c_conv_w"][j], f"dnconv_bwd_{i}")
            g.setdefault("c_conv_w", {})[j] = dcw
            g.setdefault("c_a_log", {})[j] = dalog[0, :8]
            g.setdefault("c_dt_bias", {})[j] = ddtb[0, :8]
            g.setdefault("c_o_norm_w", {})[j] = donw[0]
        g["wm"][i] = _matmul_tn(h, dpm, f"dwm_{i}")
        g["wg"][i] = _matmul_tn(h, dpg, f"dwg_{i}")
        dx, g["norm_pre"][i] = _inproj_bwd(dpm, dpg, xi, npre, wts["wm"][i], wts["wg"][i], dx, f"inproj_bwd_{i}")
    g["mem_norm_w"], g["wkv"] = _memkv_bwd(mem, mnw, wts["wkv"], dkv)
    return loss[0, 0], dx, g


ANY = pl.BlockSpec(memory_space=pl.ANY)


def _place():
    return lax.axis_index("x"), lax.axis_index("y"), lax.axis_index("c")


def _gather_weights(big, vec):
    def body(big_ref, vec_ref, ob_ref, ov_ref, ici_send, ici_recv, d2d_send, d2d_recv, vec_send, vec_recv):
        x, y, c = _place()
        chip = 2 * x + y
        peers = [(1 - x, y), (x, 1 - y), (1 - x, 1 - y)]

        def rows(half, k):
            return pl.ds(half * HALF_ROWS + k * CHUNK_ROWS, CHUNK_ROWS)

        def over_ici(j, k, slab):
            px, py = peers[j]
            i = j * N_CHUNKS + k
            return pltpu.make_async_remote_copy(
                src_ref=big_ref.at[rows(c, k)], dst_ref=ob_ref.at[slab, rows(c, k)], send_sem=ici_send.at[i],
                recv_sem=ici_recv.at[i], device_id=(px, py, c), device_id_type=MESH)

        def over_d2d(j, k, half):
            px, py = peers[j]
            i = j * N_CHUNKS + k
            where = ob_ref.at[2 * px + py, rows(half, k)]
            return pltpu.make_async_remote_copy(
                src_ref=where, dst_ref=where, send_sem=d2d_send.at[i], recv_sem=d2d_recv.at[i],
                device_id=(x, y, 1 - c), device_id_type=MESH)

        def small(j, slab):
            px, py = peers[j]
            return pltpu.make_async_remote_copy(
                src_ref=vec_ref, dst_ref=ov_ref.at[slab], send_sem=vec_send.at[j], recv_sem=vec_recv.at[j],
                device_id=(px, py, c), device_id_type=MESH)

        sends = [small(j, chip) for j in range(3)] + [over_ici(j, k, chip) for k in range(N_CHUNKS) for j in range(3)]
        for cp in sends:
            cp.start()
        forwards = []
        for k in range(N_CHUNKS):
            for j, (px, py) in enumerate(peers):
                over_ici(j, k, 2 * px + py).wait_recv()
                forwards.append(over_d2d(j, k, c))
                forwards[-1].start()
        for k in range(N_CHUNKS):
            for j in range(3):
                over_d2d(j, k, 1 - c).wait_recv()
        for j, (px, py) in enumerate(peers):
            small(j, 2 * px + py).wait_recv()
        for cp in sends + forwards:
            cp.wait_send()

    n = 3 * N_CHUNKS
    dma = pltpu.SemaphoreType.DMA
    return pl.pallas_call(
        body, name="gather_weights", in_specs=[ANY, ANY], out_specs=[ANY, ANY],
        out_shape=[SDS((4,) + big.shape, big.dtype), SDS((4,) + vec.shape, vec.dtype)],
        scratch_shapes=[dma((n,)), dma((n,)), dma((n,)), dma((n,)), dma((3,)), dma((3,))])(big, vec)


HALF_ROWS = 3328
CHUNK_ROWS = 256
N_CHUNKS = HALF_ROWS // CHUNK_ROWS


def _swap_halves(gbig, small):
    def body(g_ref, s_ref, ob_ref, os_ref, send_sems, recv_sems):
        x, y, c = _place()
        copies = []
        for s in range(4):
            for k in range(N_CHUNKS):
                rows = pl.ds(k * CHUNK_ROWS, CHUNK_ROWS)
                copies.append(pltpu.make_async_remote_copy(
                    src_ref=g_ref.at[s, 1 - c, rows], dst_ref=ob_ref.at[s, rows], send_sem=send_sems.at[len(copies)],
                    recv_sem=recv_sems.at[len(copies)], device_id=(x, y, 1 - c), device_id_type=MESH))
        copies.append(pltpu.make_async_remote_copy(
            src_ref=s_ref, dst_ref=os_ref, send_sem=send_sems.at[len(copies)], recv_sem=recv_sems.at[len(copies)],
            device_id=(x, y, 1 - c), device_id_type=MESH))
        for cp in copies:
            cp.start()
        for cp in copies:
            cp.wait_recv()
        for cp in copies:
            cp.wait_send()

    n = 4 * N_CHUNKS + 1
    return pl.pallas_call(
        body, name="swap_halves", in_specs=[ANY, ANY], out_specs=[ANY, ANY],
        out_shape=[SDS((4, HALF_ROWS, D), F32), SDS(small.shape, F32)],
        scratch_shapes=[pltpu.SemaphoreType.DMA((n,)), pltpu.SemaphoreType.DMA((n,))])(gbig, small)


def _pair_sum(gbig, other):
    def body(g_ref, o_ref, pb_ref, own_ref):
        x, y, c = _place()
        v = jnp.where(c == 0, g_ref[0], g_ref[1]) + o_ref[...]
        pb_ref[...] = v.astype(BF16)

        @pl.when(pl.program_id(1) == 2 * x + y)
        def _():
            own_ref[...] = v

    return pl.pallas_call(
        body, name="pair_sum", grid=(N_CHUNKS, 4),
        in_specs=[pl.BlockSpec((None, 2, CHUNK_ROWS, D), lambda i, s: (s, 0, i, 0)),
                  pl.BlockSpec((None, CHUNK_ROWS, D), lambda i, s: (s, i, 0))],
        out_specs=[pl.BlockSpec((None, CHUNK_ROWS, D), lambda i, s: (s, i, 0)),
                   pl.BlockSpec((CHUNK_ROWS, D), lambda i, s: (i, 0))],
        out_shape=[SDS((4, HALF_ROWS, D), BF16), SDS((HALF_ROWS, D), F32)],
        compiler_params=_params(2, VMEM_LIMIT))(gbig, other)


def _add(a, b, name):
    def body(a_ref, b_ref, o_ref):
        o_ref[...] = a_ref[...] + b_ref[...]

    return pl.pallas_call(body, name=name, out_shape=SDS(a.shape, a.dtype), compiler_params=_params(0, VMEM_LIMIT))(a, b)


def _chip_exchange(pb, ps):
    n_small = ps.shape[0]

    def body(pb_ref, ps_ref, lb_ref, ls_ref, send_sems, recv_sems):
        x, y, c = _place()
        chip = 2 * x + y
        peers = [(1 - x, y), (x, 1 - y), (1 - x, 1 - y)]

        def copies(slab_of):
            out = []
            for j, (px, py) in enumerate(peers):
                for k in range(N_CHUNKS):
                    rows = pl.ds(k * CHUNK_ROWS, CHUNK_ROWS)
                    out.append(pltpu.make_async_remote_copy(
                        src_ref=pb_ref.at[2 * px + py, rows], dst_ref=lb_ref.at[slab_of(j), rows],
                        send_sem=send_sems.at[len(out)], recv_sem=recv_sems.at[len(out)], device_id=(px, py, c),
                        device_id_type=MESH))
                out.append(pltpu.make_async_remote_copy(
                    src_ref=ps_ref, dst_ref=ls_ref.at[slab_of(j)], send_sem=send_sems.at[len(out)],
                    recv_sem=recv_sems.at[len(out)], device_id=(px, py, c), device_id_type=MESH))
            return out

        sends = copies(lambda j: chip)
        for cp in sends:
            cp.start()
        for cp in copies(lambda j: 2 * peers[j][0] + peers[j][1]):
            cp.wait_recv()
        for cp in sends:
            cp.wait_send()

    n = 3 * (N_CHUNKS + 1)
    return pl.pallas_call(
        body, name="chip_exchange", in_specs=[ANY, ANY], out_specs=[ANY, ANY],
        out_shape=[SDS((4, HALF_ROWS, D), BF16), SDS((4, n_small, D), F32)],
        scratch_shapes=[pltpu.SemaphoreType.DMA((n,)), pltpu.SemaphoreType.DMA((n,))])(pb, ps)


def _chip_sum(own, land):
    def body(own_ref, l_ref, o_ref):
        chip = 2 * lax.axis_index("x") + lax.axis_index("y")
        acc = jnp.where(chip == 0, own_ref[...], l_ref[0].astype(F32))
        for s in range(1, 4):
            acc = acc + jnp.where(chip == s, own_ref[...], l_ref[s].astype(F32))
        o_ref[...] = acc

    return pl.pallas_call(
        body, name="chip_sum", grid=(N_CHUNKS,),
        in_specs=[pl.BlockSpec((CHUNK_ROWS, D), lambda i: (i, 0)), pl.BlockSpec((4, CHUNK_ROWS, D), lambda i: (0, i, 0))],
        out_specs=pl.BlockSpec((CHUNK_ROWS, D), lambda i: (i, 0)), out_shape=SDS((HALF_ROWS, D), F32),
        compiler_params=_params(1, VMEM_LIMIT))(own, land)


def _sum4(own, land):
    def body(own_ref, l_ref, o_ref):
        chip = 2 * lax.axis_index("x") + lax.axis_index("y")
        acc = jnp.where(chip == 0, own_ref[...], l_ref[0])
        for s in range(1, 4):
            acc = acc + jnp.where(chip == s, own_ref[...], l_ref[s])
        o_ref[...] = acc

    return pl.pallas_call(body, name="sum_small", out_shape=SDS(own.shape, own.dtype),
                          compiler_params=_params(0, VMEM_LIMIT))(own, land)


def _share_half(r):
    def body(r_ref, o_ref, send_sems, recv_sems):
        x, y, c = _place()
        copies = [pltpu.make_async_remote_copy(
            src_ref=r_ref.at[pl.ds(k * CHUNK_ROWS, CHUNK_ROWS)], dst_ref=o_ref.at[pl.ds(k * CHUNK_ROWS, CHUNK_ROWS)],
            send_sem=send_sems.at[k], recv_sem=recv_sems.at[k], device_id=(x, y, 1 - c), device_id_type=MESH)
            for k in range(N_CHUNKS)]
        for cp in copies:
            cp.start()
        for cp in copies:
            cp.wait_recv()
        for cp in copies:
            cp.wait_send()

    return pl.pallas_call(
        body, name="share_half", in_specs=[ANY], out_specs=ANY, out_shape=SDS((HALF_ROWS, D), F32),
        scratch_shapes=[pltpu.SemaphoreType.DMA((N_CHUNKS,)), pltpu.SemaphoreType.DMA((N_CHUNKS,))])(r)


_SMALL = ["mem_norm_w", "norm_pre", "norm_post", "a_ln_w", "a_ln_b", "a_w_s", "a_b_s", "b_conv_w", "c_conv_w",
          "c_a_log", "c_dt_bias", "c_o_norm_w"]
_SMALL_SHAPES = {"mem_norm_w": (D,), "norm_pre": (4, D), "norm_post": (4, D), "a_ln_w": (2, D), "a_ln_b": (2, D),
                 "a_w_s": (2, 8, HD, HD), "a_b_s": (2, 8, HD), "b_conv_w": (1, 3, D), "c_conv_w": (1, 4, 3 * D),
                 "c_a_log": (1, 8), "c_dt_bias": (1, 8), "c_o_norm_w": (1, HD)}
_SHARDED_SMALL = {"a_ln_w": D // 4, "a_ln_b": D // 4, "b_conv_w": D // 4, "c_conv_w": 3 * D // 4}
_ROWS = [2048, 1280, 1284, 1536, 256]
_BIG_ROWS = sum(_ROWS)
_BIG_PAD = 6416
_SMALL_ROWS = 288


def _size(shape):
    n = 1
    for d in shape:
        n *= d
    return n


def kernel(x, mem, mem_norm_w, w_mem_kv, norm_pre, norm_post, w_out, a_w_in, a_ln_w, a_ln_b, a_w_s, a_b_s, b_w_in, b_conv_w, c_w_in, c_conv_w, c_a_log, c_dt_bias, c_o_norm_w, loss_target, m_mem_norm_w, m_w_mem_kv, m_norm_pre, m_norm_post, m_w_out, m_a_w_in, m_a_ln_w, m_a_ln_b, m_a_w_s, m_a_b_s, m_b_w_in, m_b_conv_w, m_c_w_in, m_c_conv_w, m_c_a_log, m_c_dt_bias, m_c_o_norm_w, v_mem_norm_w, v_w_mem_kv, v_norm_pre, v_norm_post, v_w_out, v_a_w_in, v_a_ln_w, v_a_ln_b, v_a_w_s, v_a_b_s, v_b_w_in, v_b_conv_w, v_c_w_in, v_c_conv_w, v_c_a_log, v_c_dt_bias, v_c_o_norm_w):
    names = ["mem_norm_w", "w_mem_kv", "norm_pre", "norm_post", "w_out", "a_w_in", "a_ln_w", "a_ln_b", "a_w_s", "a_b_s",
             "b_w_in", "b_conv_w", "c_w_in", "c_conv_w", "c_a_log", "c_dt_bias", "c_o_norm_w"]
    w = dict(zip(names, [mem_norm_w, w_mem_kv, norm_pre, norm_post, w_out, a_w_in, a_ln_w, a_ln_b, a_w_s, a_b_s, b_w_in,
                         b_conv_w, c_w_in, c_conv_w, c_a_log, c_dt_bias, c_o_norm_w]))
    m = dict(zip(names, [m_mem_norm_w, m_w_mem_kv, m_norm_pre, m_norm_post, m_w_out, m_a_w_in, m_a_ln_w, m_a_ln_b, m_a_w_s,
                         m_a_b_s, m_b_w_in, m_b_conv_w, m_c_w_in, m_c_conv_w, m_c_a_log, m_c_dt_bias, m_c_o_norm_w]))
    v = dict(zip(names, [v_mem_norm_w, v_w_mem_kv, v_norm_pre, v_norm_post, v_w_out, v_a_w_in, v_a_ln_w, v_a_ln_b, v_a_w_s,
                         v_a_b_s, v_b_w_in, v_b_conv_w, v_c_w_in, v_c_conv_w, v_c_a_log, v_c_dt_bias, v_c_o_norm_w]))
    chip = 2 * lax.axis_index("x") + lax.axis_index("y")

    big = jnp.concatenate([a_w_in.reshape(_ROWS[0], D), b_w_in.reshape(_ROWS[1], D), c_w_in.reshape(_ROWS[2], D),
                           w_out.reshape(_ROWS[3], D), w_mem_kv, jnp.zeros((2 * HALF_ROWS - _BIG_ROWS, D), F32)],
                          axis=0).astype(BF16)
    vec = jnp.concatenate([a_ln_w.reshape(-1), a_ln_b.reshape(-1), b_conv_w.reshape(-1), c_conv_w.reshape(-1)])
    vec = jnp.pad(vec, (0, 8 * D - vec.shape[0])).reshape(8, D)
    gbig, gvec = _gather_weights(big, vec)
    gbig = lax.dynamic_update_slice(gbig, big[None], (chip, 0, 0))
    gvec = lax.dynamic_update_slice(gvec, vec[None], (chip, 0, 0))
    r0, r1, r2, r3, r4 = [sum(_ROWS[:k]) for k in range(5)]
    fa = gbig[:, r0:r1].reshape(4, 2, D, D).transpose(1, 2, 0, 3).reshape(2, D, 4 * D)
    fb = gbig[:, r1:r2].reshape(4, D, 1280).transpose(1, 0, 2).reshape(D, 5120)
    fc = gbig[:, r2:r3].reshape(4, D, 1284).transpose(1, 0, 2).reshape(D, 5136)
    fo = gbig[:, r3:r4].reshape(4, 4, 384, D).transpose(1, 0, 2, 3).reshape(4, D_CAT, D)
    fkv = gbig[:, r4:_BIG_ROWS].reshape(D, 2 * D_XA)
    gv = gvec.reshape(4, 8 * D)
    sm = {"mem_norm_w": mem_norm_w, "norm_pre": norm_pre, "norm_post": norm_post, "a_w_s": a_w_s, "a_b_s": a_b_s,
          "c_a_log": c_a_log, "c_dt_bias": c_dt_bias, "c_o_norm_w": c_o_norm_w,
          "a_ln_w": gv[:, 0:512].reshape(4, 2, 256).transpose(1, 0, 2).reshape(2, D),
          "a_ln_b": gv[:, 512:1024].reshape(4, 2, 256).transpose(1, 0, 2).reshape(2, D),
          "b_conv_w": gv[:, 1024:1792].reshape(4, 1, 3, 256).transpose(1, 2, 0, 3).reshape(1, 3, D),
          "c_conv_w": gv[:, 1792:4864].reshape(4, 1, 4, 768).transpose(1, 2, 0, 3).reshape(1, 4, 3 * D)}
    c_mix = jnp.concatenate([fc[:, :3 * D + 16], jnp.zeros((D, AB_PAD - 16), BF16)], axis=1)
    wts = {"wkv": fkv, "wo": fo,
           "wm": [fa[0][:, :2 * D], fb[:, :3 * D], c_mix, fa[1][:, :2 * D]],
           "wg": [fa[0][:, 2 * D:], fb[:, 3 * D:], fc[:, 3 * D + 16:], fa[1][:, 2 * D:]]}

    loss, dx, g = _local_step(x[0], mem[0], loss_target[0], wts, sm)
    loss = lax.psum(loss, ("x", "y", "c"))

    ga = jnp.stack([jnp.concatenate([g["wm"][i], g["wg"][i]], axis=1) for i in (0, 3)])
    gb = jnp.concatenate([g["wm"][1], g["wg"][1]], axis=1)
    gc = jnp.concatenate([g["wm"][2][:, :3 * D + 16], g["wg"][2]], axis=1)
    go = jnp.stack(g["wo"])
    gbig_all = jnp.concatenate([
        ga.reshape(2, D, 4, D).transpose(2, 0, 1, 3).reshape(4, _ROWS[0], D),
        gb.reshape(D, 4, 1280).transpose(1, 0, 2).reshape(4, _ROWS[1], D),
        gc.reshape(D, 4, 1284).transpose(1, 0, 2).reshape(4, _ROWS[2], D),
        go.reshape(4, 4, 384, D).transpose(1, 0, 2, 3).reshape(4, _ROWS[3], D),
        g["wkv"].reshape(4, _ROWS[4], D),
        jnp.zeros((4, 2 * HALF_ROWS - _BIG_ROWS, D), F32)], axis=1).reshape(4, 2, HALF_ROWS, D)
    gs = {"mem_norm_w": g["mem_norm_w"], "norm_pre": jnp.concatenate(g["norm_pre"]),
          "norm_post": jnp.concatenate(g["norm_post"])}
    for n in _SMALL[3:]:
        gs[n] = jnp.stack([g[n][j] for j in sorted(g[n])])
    flat = jnp.concatenate([gs[n].reshape(-1) for n in _SMALL])
    small = jnp.pad(flat, (0, _SMALL_ROWS * D - flat.shape[0])).reshape(_SMALL_ROWS, D)
    other_big, other_small = _swap_halves(gbig_all, small)
    pair_bf16, pair_own = _pair_sum(gbig_all, other_big)
    pair_small = _add(small, other_small, "pair_sum_small")
    land_big, land_small = _chip_exchange(pair_bf16, pair_small)
    mine = _chip_sum(pair_own, land_big)
    theirs = _share_half(mine)
    south = lax.axis_index("c") == 0
    gshard = jnp.concatenate([jnp.where(south, mine, theirs), jnp.where(south, theirs, mine)], axis=0)
    grads = {"a_w_in": gshard[r0:r1].reshape(a_w_in.shape), "b_w_in": gshard[r1:r2].reshape(b_w_in.shape),
             "c_w_in": gshard[r2:r3].reshape(c_w_in.shape), "w_out": gshard[r3:r4].reshape(w_out.shape),
             "w_mem_kv": gshard[r4:_BIG_ROWS]}
    flat = _sum4(pair_small, land_small).reshape(-1)
    off = 0
    for n in _SMALL:
        shape = _SMALL_SHAPES[n]
        full = flat[off:off + _size(shape)].reshape(shape)
        off += _size(shape)
        if n in _SHARDED_SMALL:
            full = lax.dynamic_slice_in_dim(full, chip * _SHARDED_SMALL[n], _SHARDED_SMALL[n], axis=len(shape) - 1)
        grads[n] = full

    delta, new_m, new_v = {}, {}, {}
    for n in names:
        shape = w[n].shape
        view = (1, shape[0]) if len(shape) == 1 else (_size(shape[:-1]), shape[-1])
        d_, m_, v_ = _adamw(w[n].reshape(view), grads[n].reshape(view), m[n].reshape(view), v[n].reshape(view),
                            f"adamw_{n}")
        delta[n], new_m[n], new_v[n] = d_.reshape(shape), m_.reshape(shape), v_.reshape(shape)
    return (loss, dx[None], *[grads[n].reshape(w[n].shape) for n in names], *[delta[n] for n in names],
            *[new_m[n] for n in names], *[new_v[n] for n in names])
```

```python
import functools

import jax
import jax.numpy as jnp
from jax import lax
from jax.experimental import pallas as pl
from jax.experimental.pallas import tpu as pltpu

F32 = jnp.float32
BF16 = jnp.bfloat16
HI = lax.Precision.HIGHEST
MESH = pl.DeviceIdType.MESH
SDS = jax.ShapeDtypeStruct

D = 1024
D_XA = 512
D_CAT = 1536
N_MEM = 256
HD = 128
DEPTH = 4
EPS = 1e-6
TT = 512
DN_C = 64
DN_TB = 256
HALO = 8
AB_PAD = 128
VMEM_LIMIT = 56 * 1024 * 1024

ADAM_LR, ADAM_B1, ADAM_B2, ADAM_EPS, ADAM_WD, ADAM_STEP = 0.001, 0.9, 0.999, 1e-08, 0.01, 10


def _params(n_grid, vmem=None):
    return pltpu.CompilerParams(dimension_semantics=("arbitrary",) * n_grid, vmem_limit_bytes=vmem)


def _rms(x, w):
    return x * lax.rsqrt(jnp.mean(x * x, axis=-1, keepdims=True) + EPS) * w


def _dot_nn(a, b):
    return jnp.dot(a.astype(BF16), b.astype(BF16), preferred_element_type=F32)


def _dot_nt(a, b):
    return lax.dot_general(a.astype(BF16), b.astype(BF16), (((1,), (1,)), ((), ())), preferred_element_type=F32)


def _dot_tn(a, b):
    return lax.dot_general(a.astype(BF16), b.astype(BF16), (((0,), (0,)), ((), ())), preferred_element_type=F32)


@jax.custom_vjp
def mm(a, b):
    return _dot_nn(a, b)


mm.defvjp(lambda a, b: (_dot_nn(a, b), (a, b)), lambda r, g: (_dot_nt(g, r[1]), _dot_tn(r[0], g)))


@jax.custom_vjp
def mm_nt(a, b):
    return _dot_nt(a, b)


mm_nt.defvjp(lambda a, b: (_dot_nt(a, b), (a, b)), lambda r, g: (_dot_nn(g, r[1]), _dot_tn(g, r[0])))


@jax.custom_vjp
def mm_tn(a, b):
    return _dot_tn(a, b)


mm_tn.defvjp(lambda a, b: (_dot_tn(a, b), (a, b)), lambda r, g: (_dot_nt(r[1], g), _dot_nn(r[0], g)))


def _dot_hi(a, b):
    return jnp.dot(a, b, precision=HI, preferred_element_type=F32)


def _row_spec(width, tile=TT):
    return pl.BlockSpec((tile, width), lambda i: (i, 0))


def _full_spec(shape):
    return pl.BlockSpec(shape, lambda *_: (0,) * len(shape))


def _inproj_fwd(x, nw, wm, wg, name):
    T, M, G = x.shape[0], wm.shape[1], wg.shape[1]

    def body(x_ref, nw_ref, wm_ref, wg_ref, pm_ref, pg_ref, h_ref):
        h = _rms(x_ref[...], nw_ref[...]).astype(BF16)
        h_ref[...] = h
        pm_ref[...] = jnp.dot(h, wm_ref[...], preferred_element_type=F32)
        pg_ref[...] = jnp.dot(h, wg_ref[...], preferred_element_type=F32)

    return pl.pallas_call(
        body, name=name, grid=(T // TT,),
        in_specs=[_row_spec(D), _full_spec((1, D)), _full_spec((D, M)), _full_spec((D, G))],
        out_specs=[_row_spec(M), _row_spec(G), _row_spec(D)],
        out_shape=[SDS((T, M), F32), SDS((T, G), F32), SDS((T, D), BF16)],
        compiler_params=_params(1, VMEM_LIMIT))(x, nw, wm, wg)


def _inproj_bwd(dpm, dpg, x, nw, wm, wg, dxc, name):
    T, M, G = x.shape[0], wm.shape[1], wg.shape[1]

    def body(dpm_ref, dpg_ref, x_ref, nw_ref, wm_ref, wg_ref, dxc_ref, dx_ref, dnw_ref):
        dh = _dot_nt(dpm_ref[...], wm_ref[...]) + _dot_nt(dpg_ref[...], wg_ref[...])
        _, vjp = jax.vjp(_rms, x_ref[...], nw_ref[...])
        dxr, dnw = vjp(dh)
        dx_ref[...] = dxc_ref[...] + dxr

        @pl.when(pl.program_id(0) == 0)
        def _():
            dnw_ref[...] = jnp.zeros_like(dnw_ref)
        dnw_ref[...] += dnw

    return pl.pallas_call(
        body, name=name, grid=(T // TT,),
        in_specs=[_row_spec(M), _row_spec(G), _row_spec(D), _full_spec((1, D)), _full_spec((D, M)),
                  _full_spec((D, G)), _row_spec(D)],
        out_specs=[_row_spec(D), _full_spec((1, D))],
        out_shape=[SDS((T, D), F32), SDS((1, D), F32)],
        compiler_params=_params(1, VMEM_LIMIT))(dpm, dpg, x, nw, wm, wg, dxc)


def _matmul_tn(a, b, name):
    T, K = a.shape
    N = b.shape[1]
    tn = 1024 if N % 1024 == 0 else (640 if N % 640 == 0 else N)
    tt = min(1024, T)

    def body(a_ref, b_ref, o_ref):
        @pl.when(pl.program_id(1) == 0)
        def _():
            o_ref[...] = jnp.zeros_like(o_ref)
        o_ref[...] += _dot_tn(a_ref[...], b_ref[...])

    return pl.pallas_call(
        body, name=name, grid=(N // tn, T // tt),
        in_specs=[pl.BlockSpec((tt, K), lambda j, t: (t, 0)), pl.BlockSpec((tt, tn), lambda j, t: (t, j))],
        out_specs=pl.BlockSpec((K, tn), lambda j, t: (0, j)),
        out_shape=SDS((K, N), F32),
        compiler_params=_params(2, VMEM_LIMIT))(a, b)


def _memkv_fn(mem, w, wkv):
    return mm(_rms(mem, w), wkv)


def _memkv_fwd(mem, w, wkv):
    def body(mem_ref, w_ref, wkv_ref, kv_ref):
        kv_ref[...] = _memkv_fn(mem_ref[...], w_ref[...], wkv_ref[...])

    return pl.pallas_call(body, name="memkv_fwd", out_shape=SDS((N_MEM, 2 * D_XA), F32),
                          compiler_params=_params(0, VMEM_LIMIT))(mem, w, wkv)


def _memkv_bwd(mem, w, wkv, dkv):
    def body(mem_ref, w_ref, wkv_ref, dkv_ref, dw_ref, dwkv_ref):
        _, vjp = jax.vjp(functools.partial(_memkv_fn, mem_ref[...]), w_ref[...], wkv_ref[...].astype(F32))
        dw, dwkv = vjp(dkv_ref[...])
        dw_ref[...] = dw
        dwkv_ref[...] = dwkv

    return pl.pallas_call(body, name="memkv_bwd", out_shape=[SDS((1, D), F32), SDS((D, 2 * D_XA), F32)],
                          compiler_params=_params(0, VMEM_LIMIT))(mem, w, wkv, dkv)


def _attn_gate(ymix, qx, z, *kvs):
    outs = []
    for j in range(4):
        s = mm_nt(qx[:, j * HD:(j + 1) * HD], kvs[j]) * (HD ** -0.5)
        e = jnp.exp(s - lax.stop_gradient(jnp.max(s, axis=-1, keepdims=True)))
        outs.append(mm(e / jnp.sum(e, axis=-1, keepdims=True), kvs[4 + j]))
    return jnp.concatenate([ymix] + outs, axis=1) * jax.nn.silu(z)


def _kv_blocks(kv_ref):
    return [kv_ref[:, j * HD:(j + 1) * HD] for j in range(8)]


def _ag_fwd(ymix, pg, kv, name):
    T = ymix.shape[0]

    def body(ymix_ref, pg_ref, kv_ref, ycat_ref):
        ycat_ref[...] = _attn_gate(ymix_ref[...], pg_ref[:, :D_XA], pg_ref[:, D_XA:], *_kv_blocks(kv_ref)).astype(BF16)

    return pl.pallas_call(
        body, name=name, grid=(T // TT,),
        in_specs=[_row_spec(D), _row_spec(D_XA + D_CAT), _full_spec((N_MEM, 2 * D_XA))],
        out_specs=_row_spec(D_CAT), out_shape=SDS((T, D_CAT), BF16),
        compiler_params=_params(1, VMEM_LIMIT))(ymix, pg, kv)


def _ag_bwd(dycat, ymix, pg, kv, dkv_in, name):
    T = ymix.shape[0]

    def body(dycat_ref, ymix_ref, pg_ref, kv_ref, dkvin_ref, dymix_ref, dpg_ref, dkv_ref):
        _, vjp = jax.vjp(_attn_gate, ymix_ref[...], pg_ref[:, :D_XA], pg_ref[:, D_XA:], *_kv_blocks(kv_ref))
        g = vjp(dycat_ref[...])
        dymix_ref[...] = g[0]
        dpg_ref[:, :D_XA] = g[1].astype(BF16)
        dpg_ref[:, D_XA:] = g[2].astype(BF16)

        @pl.when(pl.program_id(0) == 0)
        def _():
            dkv_ref[...] = dkvin_ref[...]
        for j in range(8):
            dkv_ref[:, j * HD:(j + 1) * HD] += g[3 + j]

    return pl.pallas_call(
        body, name=name, grid=(T // TT,),
        in_specs=[_row_spec(D_CAT), _row_spec(D), _row_spec(D_XA + D_CAT), _full_spec((N_MEM, 2 * D_XA)),
                  _full_spec((N_MEM, 2 * D_XA))],
        out_specs=[_row_spec(D), _row_spec(D_XA + D_CAT), _full_spec((N_MEM, 2 * D_XA))],
        out_shape=[SDS((T, D), F32), SDS((T, D_XA + D_CAT), BF16), SDS((N_MEM, 2 * D_XA), F32)],
        compiler_params=_params(1, VMEM_LIMIT))(dycat, ymix, pg, kv, dkv_in)


def _outproj_fwd(ycat, wo, x, nw, name):
    T = x.shape[0]

    def body(ycat_ref, wo_ref, x_ref, nw_ref, o_ref, xn_ref):
        o = jnp.dot(ycat_ref[...], wo_ref[...], preferred_element_type=F32)
        o_ref[...] = o
        xn_ref[...] = x_ref[...] + _rms(o, nw_ref[...])

    return pl.pallas_call(
        body, name=name, grid=(T // TT,),
        in_specs=[_row_spec(D_CAT), _full_spec((D_CAT, D)), _row_spec(D), _full_spec((1, D))],
        out_specs=[_row_spec(D), _row_spec(D)], out_shape=[SDS((T, D), F32), SDS((T, D), F32)],
        compiler_params=_params(1, VMEM_LIMIT))(ycat, wo, x, nw)


def _outproj_bwd(dxo, o, nw, wo, name):
    T = dxo.shape[0]

    def body(dxo_ref, o_ref, nw_ref, wo_ref, dycat_ref, dobf_ref, dnw_ref):
        _, vjp = jax.vjp(_rms, o_ref[...], nw_ref[...])
        do, dnw = vjp(dxo_ref[...])
        dobf = do.astype(BF16)
        dobf_ref[...] = dobf
        dycat_ref[...] = _dot_nt(dobf, wo_ref[...])

        @pl.when(pl.program_id(0) == 0)
        def _():
            dnw_ref[...] = jnp.zeros_like(dnw_ref)
        dnw_ref[...] += dnw

    return pl.pallas_call(
        body, name=name, grid=(T // TT,),
        in_specs=[_row_spec(D), _row_spec(D), _full_spec((1, D)), _full_spec((D_CAT, D))],
        out_specs=[_row_spec(D_CAT), _row_spec(D), _full_spec((1, D))],
        out_shape=[SDS((T, D_CAT), F32), SDS((T, D), BF16), SDS((1, D), F32)],
        compiler_params=_params(1, VMEM_LIMIT))(dxo, o, nw, wo)


def _loss_head(xl, target):
    T = xl.shape[0]

    def body(x_ref, t_ref, loss_ref, dx_ref):
        err = x_ref[...] - t_ref[...]
        dx_ref[...] = err * (1.0 / D)

        @pl.when(pl.program_id(0) == 0)
        def _():
            loss_ref[...] = jnp.zeros_like(loss_ref)
        part = jnp.sum(jnp.sum(err * err, axis=1, keepdims=True), axis=0, keepdims=True) * (0.5 / D)
        loss_ref[...] += jnp.broadcast_to(part, loss_ref.shape)

    return pl.pallas_call(
        body, name="loss_head", grid=(T // TT,),
        in_specs=[_row_spec(D), _row_spec(D)],
        out_specs=[_full_spec((8, 128)), _row_spec(D)], out_shape=[SDS((8, 128), F32), SDS((T, D), F32)],
        compiler_params=_params(1))(xl, target)


def _gmlp_pre(u, v, lnw, lnb):
    vg = jax.nn.gelu(v)
    xc = vg - jnp.mean(vg, axis=-1, keepdims=True)
    vl = xc * lax.rsqrt(jnp.mean(xc * xc, axis=-1, keepdims=True) + EPS) * lnw + lnb
    return jax.nn.gelu(u), vl


def _tril(n, strict=False):
    r = lax.broadcasted_iota(jnp.int32, (n, n), 0)
    c = lax.broadcasted_iota(jnp.int32, (n, n), 1)
    return (r > c) if strict else (r >= c)


def _gmlp_fwd(pm, lnw, lnb, ws, bs3, name):
    T = pm.shape[0]

    def body(pm_ref, lnw_ref, lnb_ref, ws_ref, bs_ref, y_ref):
        ug, vl = _gmlp_pre(pm_ref[:, :D], pm_ref[:, D:], lnw_ref[...], lnb_ref[...])
        mask = _tril(HD)
        for g in range(8):
            w = jnp.where(mask, ws_ref[g], 0.0)
            for c in range(TT // HD):
                rows, cols = slice(c * HD, (c + 1) * HD), slice(g * HD, (g + 1) * HD)
                y_ref[rows, cols] = ug[rows, cols] * (_dot_nn(w, vl[rows, cols]) + bs_ref[g])

    return pl.pallas_call(
        body, name=name, grid=(T // TT,),
        in_specs=[_row_spec(2 * D), _full_spec((1, D)), _full_spec((1, D)), _full_spec((8, HD, HD)),
                  _full_spec((8, HD, HD))],
        out_specs=_row_spec(D), out_shape=SDS((T, D), F32),
        compiler_params=_params(1, VMEM_LIMIT))(pm, lnw, lnb, ws, bs3)


def _gmlp_bwd(dy, pm, lnw, lnb, ws, bs3, name):
    T = pm.shape[0]
    n_t = T // TT

    def body(dy_ref, pm_ref, lnw_ref, lnb_ref, ws_ref, bs_ref, dpm_ref, dlnw_ref, dlnb_ref, dws_ref, dbs_ref,
             dug_scr, dvl_scr, dbs_scr):
        i = pl.program_id(0)

        @pl.when(i == 0)
        def _():
            dlnw_ref[...] = jnp.zeros_like(dlnw_ref)
            dlnb_ref[...] = jnp.zeros_like(dlnb_ref)
            dws_ref[...] = jnp.zeros_like(dws_ref)
            dbs_scr[...] = jnp.zeros_like(dbs_scr)

        (ug, vl), vjp = jax.vjp(_gmlp_pre, pm_ref[:, :D], pm_ref[:, D:], lnw_ref[...], lnb_ref[...])
        mask = _tril(HD)
        for g in range(8):
            w = jnp.where(mask, ws_ref[g], 0.0)
            dw = jnp.zeros((HD, HD), F32)
            db = jnp.zeros((HD, HD), F32)
            for c in range(TT // HD):
                rows, cols = slice(c * HD, (c + 1) * HD), slice(g * HD, (g + 1) * HD)
                dyb, vlb = dy_ref[rows, cols], vl[rows, cols]
                sp = _dot_nn(w, vlb) + bs_ref[g]
                dsp = dyb * ug[rows, cols]
                dug_scr[rows, cols] = dyb * sp
                dvl_scr[rows, cols] = _dot_tn(w, dsp)
                dw += _dot_nt(dsp, vlb)
                db += dsp
            dws_ref[g] += jnp.where(mask, dw, 0.0)
            dbs_scr[g] += db
        du, dv, dlnw, dlnb = vjp((dug_scr[...], dvl_scr[...]))
        dpm_ref[:, :D] = du.astype(BF16)
        dpm_ref[:, D:] = dv.astype(BF16)
        dlnw_ref[...] += dlnw
        dlnb_ref[...] += dlnb

        @pl.when(i == n_t - 1)
        def _():
            for g in range(8):
                dbs_ref[g] = jnp.broadcast_to(jnp.sum(dbs_scr[g], axis=1, keepdims=True), (HD, HD))

    return pl.pallas_call(
        body, name=name, grid=(n_t,),
        in_specs=[_row_spec(D), _row_spec(2 * D), _full_spec((1, D)), _full_spec((1, D)), _full_spec((8, HD, HD)),
                  _full_spec((8, HD, HD))],
        out_specs=[_row_spec(2 * D), _full_spec((1, D)), _full_spec((1, D)), _full_spec((8, HD, HD)),
                   _full_spec((8, HD, HD))],
        out_shape=[SDS((T, 2 * D), BF16), SDS((1, D), F32), SDS((1, D), F32), SDS((8, HD, HD), F32),
                   SDS((8, HD, HD), F32)],
        scratch_shapes=[pltpu.VMEM((TT, D), F32), pltpu.VMEM((TT, D), F32), pltpu.VMEM((8, HD, HD), F32)],
        compiler_params=_params(1, VMEM_LIMIT))(dy, pm, lnw, lnb, ws, bs3)


def _prev_spec(width, T):
    return pl.BlockSpec((HALO, width), lambda i: (jnp.maximum(i * (TT // HALO) - 1, 0), 0))


def _next_spec(width, T):
    return pl.BlockSpec((HALO, width), lambda i: (jnp.minimum((i + 1) * (TT // HALO), T // HALO - 1), 0))


def _rows_before(ext, j):
    return ext[HALO:] if j == 0 else pltpu.roll(ext, j, 0)[HALO:]


def _rows_after(ext, j):
    n = ext.shape[0]
    return ext[:n - HALO] if j == 0 else pltpu.roll(ext, n - j, 0)[:n - HALO]


def _conv_apply(ext_s, w):
    K = w.shape[0]
    y = _rows_before(ext_s, K - 1) * w[0:1]
    for k in range(1, K):
        y = y + _rows_before(ext_s, K - 1 - k) * w[k:k + 1]
    return y


def _conv_grads(ext_s, ext_dy, w):
    K = w.shape[0]
    dy = ext_dy[:ext_dy.shape[0] - HALO]
    ds = _rows_after(ext_dy, K - 1) * w[0:1]
    dws = [jnp.sum(dy * _rows_before(ext_s, K - 1), axis=0, keepdims=True)]
    for k in range(1, K):
        ds = ds + _rows_after(ext_dy, K - 1 - k) * w[k:k + 1]
        dws.append(jnp.sum(dy * _rows_before(ext_s, K - 1 - k), axis=0, keepdims=True))
    return ds, jnp.concatenate(dws, axis=0)


def _sconv_fwd(pm, w, name):
    T = pm.shape[0]

    def body(pm_ref, prev_ref, w_ref, y_ref):
        s = pm_ref[:, D:2 * D] * pm_ref[:, 2 * D:]
        sp = jnp.where(pl.program_id(0) > 0, prev_ref[:, D:2 * D] * prev_ref[:, 2 * D:], 0.0)
        y_ref[...] = pm_ref[:, :D] * _conv_apply(jnp.concatenate([sp, s], axis=0), w_ref[...])

    return pl.pallas_call(
        body, name=name, grid=(T // TT,),
        in_specs=[_row_spec(3 * D), _prev_spec(3 * D, T), _full_spec((3, D))],
        out_specs=_row_spec(D), out_shape=SDS((T, D), F32),
        compiler_params=_params(1, VMEM_LIMIT))(pm, pm, w)


def _sconv_bwd(dy, pm, w, name):
    T = pm.shape[0]
    n_t = T // TT

    def body(dy_ref, dyn_ref, pm_ref, prev_ref, next_ref, w_ref, dpm_ref, dw_ref):
        i = pl.program_id(0)
        bg, cg, hv = pm_ref[:, :D], pm_ref[:, D:2 * D], pm_ref[:, 2 * D:]
        sp = jnp.where(i > 0, prev_ref[:, D:2 * D] * prev_ref[:, 2 * D:], 0.0)
        ext_s = jnp.concatenate([sp, cg * hv], axis=0)
        dyv = dy_ref[...]
        dcn = jnp.where(i < n_t - 1, dyn_ref[...] * next_ref[:, :D], 0.0)
        ds, dw = _conv_grads(ext_s, jnp.concatenate([dyv * bg, dcn], axis=0), w_ref[...])
        dpm_ref[:, :D] = (dyv * _conv_apply(ext_s, w_ref[...])).astype(BF16)
        dpm_ref[:, D:2 * D] = (ds * hv).astype(BF16)
        dpm_ref[:, 2 * D:] = (ds * cg).astype(BF16)

        @pl.when(i == 0)
        def _():
            dw_ref[...] = jnp.zeros_like(dw_ref)
        dw_ref[...] += dw

    return pl.pallas_call(
        body, name=name, grid=(n_t,),
        in_specs=[_row_spec(D), _next_spec(D, T), _row_spec(3 * D), _prev_spec(3 * D, T), _next_spec(3 * D, T),
                  _full_spec((3, D))],
        out_specs=[_row_spec(3 * D), _full_spec((3, D))],
        out_shape=[SDS((T, 3 * D), BF16), SDS((3, D), F32)],
        compiler_params=_params(1, VMEM_LIMIT))(dy, dy, pm, pm, pm, w)


def _dnconv_fwd(pm, w, name):
    T = pm.shape[0]

    def body(pm_ref, prev_ref, w_ref, c_ref):
        sp = jnp.where(pl.program_id(0) > 0, prev_ref[...], 0.0)
        c_ref[...] = _conv_apply(jnp.concatenate([sp, pm_ref[...]], axis=0), w_ref[...])

    return pl.pallas_call(
        body, name=name, grid=(T // TT,),
        in_specs=[_row_spec(3 * D), _prev_spec(3 * D, T), _full_spec((4, 3 * D))],
        out_specs=_row_spec(3 * D), out_shape=SDS((T, 3 * D), F32),
        compiler_params=_params(1, VMEM_LIMIT))(pm, pm, w)


def _dnconv_bwd(dcq, dck, dcv, dab, pm, w, name):
    T = pm.shape[0]
    n_t = T // TT

    def body(dq_ref, dk_ref, dv_ref, dqn_ref, dkn_ref, dvn_ref, dab_ref, pm_ref, prev_ref, w_ref, dpm_ref, dw_ref):
        i = pl.program_id(0)
        sp = jnp.where(i > 0, prev_ref[...], 0.0)
        ext_s = jnp.concatenate([sp, pm_ref[...]], axis=0)
        own = jnp.concatenate([dq_ref[...], dk_ref[...], dv_ref[...]], axis=1)
        nxt = jnp.where(i < n_t - 1, jnp.concatenate([dqn_ref[...], dkn_ref[...], dvn_ref[...]], axis=1), 0.0)
        ds, dw = _conv_grads(ext_s, jnp.concatenate([own, nxt], axis=0), w_ref[...])
        dpm_ref[:, :3 * D] = ds.astype(BF16)
        dpm_ref[:, 3 * D:] = dab_ref[...].astype(BF16)

        @pl.when(i == 0)
        def _():
            dw_ref[...] = jnp.zeros_like(dw_ref)
        dw_ref[...] += dw

    return pl.pallas_call(
        body, name=name, grid=(n_t,),
        in_specs=[_row_spec(D), _row_spec(D), _row_spec(D), _next_spec(D, T), _next_spec(D, T), _next_spec(D, T),
                  _row_spec(AB_PAD), _row_spec(3 * D), _prev_spec(3 * D, T), _full_spec((4, 3 * D))],
        out_specs=[_row_spec(3 * D + AB_PAD), _full_spec((4, 3 * D))],
        out_shape=[SDS((T, 3 * D + AB_PAD), BF16), SDS((4, 3 * D), F32)],
        compiler_params=_params(1, VMEM_LIMIT))(dcq, dck, dcv, dcq, dck, dcv, dab, pm, pm, w)


def _l2n(x):
    return x * lax.rsqrt(jnp.sum(x * x, axis=-1, keepdims=True) + EPS)


def _softplus(x):
    return jnp.maximum(x, 0.0) + jnp.log1p(jnp.exp(-jnp.abs(x)))


def _dn_gates(ab, alog, dtb, h):
    lane = lax.broadcasted_iota(jnp.int32, ab.shape, 1)
    g_all = -jnp.exp(alog) * _softplus(ab + dtb)
    g = jnp.sum(jnp.where(lane == h, g_all, 0.0), axis=1, keepdims=True)
    beta = jnp.sum(jnp.where(lane == 8 + h, jax.nn.sigmoid(ab), 0.0), axis=1, keepdims=True)
    ones = jnp.ones((1, HD), F32)
    return g * ones, beta * ones


def _dn_chunk(cq, ck, cv, gb, bb, S, onw):
    C = DN_C
    q = _l2n(jax.nn.silu(cq)) * (HD ** -0.5)
    k = _l2n(jax.nn.silu(ck))
    v = jax.nn.silu(cv)
    incl, strict = _tril(C), _tril(C, strict=True)
    gcum = _dot_hi(incl.astype(F32), gb)
    gi = gcum[:, :C]
    gj = gcum.T[:C, :]
    decay = jnp.where(incl, jnp.exp(jnp.where(incl, gi - gj, 0.0)), 0.0)
    kb = k * bb
    a_mat = jnp.where(strict, mm_nt(kb, k) * decay, 0.0)
    p = -a_mat
    eye = (lax.broadcasted_iota(jnp.int32, (C, C), 0) == lax.broadcasted_iota(jnp.int32, (C, C), 1)).astype(F32)
    t_mat = eye + p
    for _ in range(5):
        p = _dot_hi(p, p)
        t_mat = t_mat + _dot_hi(t_mat, p)
    eg = jnp.exp(gcum)
    u = mm(t_mat, v * bb)
    w = mm(t_mat, kb * eg)
    qk = mm_nt(q, k) * decay
    glast = gcum[C - 1:C, :]
    v_new = u - mm(w, S)
    o = mm(q * eg, S) + mm(qk, v_new)
    s_new = S * jnp.exp(glast) + mm_tn(k * jnp.exp(glast - gcum), v_new)
    return _rms(o, onw), s_new


def _dn_specs(T, rev):
    nb = T // DN_TB
    blk = (lambda n: nb - 1 - n) if rev else (lambda n: n)
    head = [pl.BlockSpec((DN_TB, HD), functools.partial(lambda n, h, off: (blk(n), off + h), off=8 * s)) for s in range(3)]
    ab = pl.BlockSpec((DN_TB, AB_PAD), lambda n, h: (blk(n), 3 * D // AB_PAD))
    st = pl.BlockSpec((DN_TB // DN_C, None, HD, HD), lambda n, h: (blk(n), h, 0, 0))
    out = pl.BlockSpec((DN_TB, HD), lambda n, h: (blk(n), h))
    row = pl.BlockSpec((1, HD), lambda n, h: (0, 0))
    return nb, head, ab, st, out, row


def _dn_fwd(cpre, pm, alog, dtb, onw, name):
    T = cpre.shape[0]
    nb, head, ab, st, out, row = _dn_specs(T, False)

    def body(cq_ref, ck_ref, cv_ref, ab_ref, alog_ref, dtb_ref, onw_ref, o_ref, st_ref, s_scr):
        n, h = pl.program_id(0), pl.program_id(1)

        @pl.when(n == 0)
        def _():
            s_scr[h] = jnp.zeros((HD, HD), F32)
        gb, bb = _dn_gates(ab_ref[...], alog_ref[...], dtb_ref[...], h)
        S = s_scr[h]
        for c in range(DN_TB // DN_C):
            rows = slice(c * DN_C, (c + 1) * DN_C)
            st_ref[c] = S
            o, S = _dn_chunk(cq_ref[rows, :], ck_ref[rows, :], cv_ref[rows, :], gb[rows], bb[rows], S, onw_ref[...])
            o_ref[rows, :] = o
        s_scr[h] = S

    return pl.pallas_call(
        body, name=name, grid=(nb, 8),
        in_specs=head + [ab, row, row, row], out_specs=[out, st],
        out_shape=[SDS((T, D), F32), SDS((T // DN_C, 8, HD, HD), F32)],
        scratch_shapes=[pltpu.VMEM((8, HD, HD), F32)],
        compiler_params=_params(2, VMEM_LIMIT))(cpre, cpre, cpre, pm, alog, dtb, onw)


def _dn_bwd(do, cpre, pm, st, alog, dtb, onw, name):
    T = cpre.shape[0]
    nb, head, ab, stspec, out, row = _dn_specs(T, True)

    def body(do_ref, cq_ref, ck_ref, cv_ref, ab_ref, st_ref, alog_ref, dtb_ref, onw_ref,
             dcq_ref, dck_ref, dcv_ref, dab_ref, dalog_ref, ddtb_ref, donw_ref, ds_scr):
        n, h = pl.program_id(0), pl.program_id(1)

        @pl.when(n == 0)
        def _():
            ds_scr[h] = jnp.zeros((HD, HD), F32)

        @pl.when((n == 0) & (h == 0))
        def _():
            dalog_ref[...] = jnp.zeros_like(dalog_ref)
            ddtb_ref[...] = jnp.zeros_like(ddtb_ref)
            donw_ref[...] = jnp.zeros_like(donw_ref)

        @pl.when(h == 0)
        def _():
            dab_ref[...] = jnp.zeros_like(dab_ref)

        (gb, bb), gates_vjp = jax.vjp(lambda a, b, c: _dn_gates(a, b, c, h), ab_ref[...], alog_ref[...], dtb_ref[...])
        dS = ds_scr[h]
        n_c = DN_TB // DN_C
        dgs, dbs = [None] * n_c, [None] * n_c
        donw = jnp.zeros((1, HD), F32)
        for c in reversed(range(n_c)):
            rows = slice(c * DN_C, (c + 1) * DN_C)
            _, vjp = jax.vjp(_dn_chunk, cq_ref[rows, :], ck_ref[rows, :], cv_ref[rows, :], gb[rows], bb[rows],
                             st_ref[c], onw_ref[...])
            dcq, dck, dcv, dgs[c], dbs[c], dS, dn = vjp((do_ref[rows, :], dS))
            dcq_ref[rows, :] = dcq
            dck_ref[rows, :] = dck
            dcv_ref[rows, :] = dcv
            donw += dn
        ds_scr[h] = dS
        dab, dalog, ddtb = gates_vjp((jnp.concatenate(dgs, axis=0), jnp.concatenate(dbs, axis=0)))
        dab_ref[...] += dab
        dalog_ref[...] += dalog
        ddtb_ref[...] += ddtb
        donw_ref[...] += donw

    dabspec = pl.BlockSpec((DN_TB, AB_PAD), lambda n, h: (nb - 1 - n, 0))
    return pl.pallas_call(
        body, name=name, grid=(nb, 8),
        in_specs=[out] + head + [ab, stspec, row, row, row],
        out_specs=[out, out, out, dabspec, row, row, row],
        out_shape=[SDS((T, D), F32)] * 3 + [SDS((T, AB_PAD), F32)] + [SDS((1, HD), F32)] * 3,
        scratch_shapes=[pltpu.VMEM((8, HD, HD), F32)],
        compiler_params=_params(2, VMEM_LIMIT))(do, cpre, cpre, cpre, pm, st, alog, dtb, onw)


_BNN = (((2,), (1,)), ((0,), (0,)))
_BNT = (((2,), (2,)), ((0,), (0,)))
_BTN = (((1,), (1,)), ((0,), (0,)))


def _bdot(a, b, dims):
    return lax.dot_general(a.astype(BF16), b.astype(BF16), dims, preferred_element_type=F32)


def _bdot3(a, b, dims):
    ah, bh = a.astype(BF16), b.astype(BF16)
    al, bl = (a - ah.astype(F32)).astype(BF16), (b - bh.astype(F32)).astype(BF16)
    d = functools.partial(lax.dot_general, dimension_numbers=dims, preferred_element_type=F32)
    return d(ah, bh) + (d(ah, bl) + d(al, bh))


def _bdot_hi(a, b, dims):
    return lax.dot_general(a, b, dims, precision=HI, preferred_element_type=F32)


def _batched_matmuls(dot):
    @jax.custom_vjp
    def nn(a, b):
        return dot(a, b, _BNN)

    @jax.custom_vjp
    def nt(a, b):
        return dot(a, b, _BNT)

    @jax.custom_vjp
    def tn(a, b):
        return dot(a, b, _BTN)

    nn.defvjp(lambda a, b: (dot(a, b, _BNN), (a, b)), lambda r, g: (dot(g, r[1], _BNT), dot(r[0], g, _BTN)))
    nt.defvjp(lambda a, b: (dot(a, b, _BNT), (a, b)), lambda r, g: (dot(g, r[1], _BNN), dot(g, r[0], _BTN)))
    tn.defvjp(lambda a, b: (dot(a, b, _BTN), (a, b)), lambda r, g: (dot(r[1], g, _BNT), dot(r[0], g, _BNN)))
    return nn, nt, tn


bmm, bmm_nt, bmm_tn = _batched_matmuls(_bdot)
bmm3, _, _ = _batched_matmuls(_bdot3)
bmm_hi, bmm_hi_nt, _ = _batched_matmuls(_bdot_hi)

@jax.custom_vjp
def _neumann_inverse(n):
    C = n.shape[1]
    eye = lax.broadcasted_iota(jnp.int32, n.shape, 1) == lax.broadcasted_iota(jnp.int32, n.shape, 2)
    t = eye.astype(F32) + n
    for _ in range(5):
        n = _bdot3(n, n, _BNN)
        t = t + _bdot3(t, n, _BNN)
    return t


def _neumann_inverse_fwd(n):
    t = _neumann_inverse(n)
    return t, t


def _neumann_inverse_bwd(t, g):
    return (_bdot3(_bdot3(t, g, _BTN), t, _BNT),)


_neumann_inverse.defvjp(_neumann_inverse_fwd, _neumann_inverse_bwd)

DN_NCH = DN_TB // DN_C
DN_NH = 4


def _dn_prep(cq, ck, cv, gb, bb):
    B, C = cq.shape[0], DN_C
    q = _l2n(jax.nn.silu(cq)) * (HD ** -0.5)
    k = _l2n(jax.nn.silu(ck))
    v = jax.nn.silu(cv)
    r = lax.broadcasted_iota(jnp.int32, (B, C, C), 1)
    c = lax.broadcasted_iota(jnp.int32, (B, C, C), 2)
    incl, strict = r >= c, r > c
    gcum = bmm_hi(incl.astype(F32), gb)
    lane0 = lax.broadcasted_iota(jnp.int32, (B, C, HD), 2) == 0
    gj = bmm_hi_nt(jnp.ones((B, C, HD), F32), jnp.where(lane0, gcum, 0.0))
    decay = jnp.where(incl, jnp.exp(jnp.where(incl, gcum[:, :, :C] - gj, 0.0)), 0.0)
    kb = k * bb
    t_mat = _neumann_inverse(-jnp.where(strict, bmm_nt(kb, k) * decay, 0.0))
    eg = jnp.exp(gcum)
    glast = gcum[:, C - 1:C, :]
    return (bmm(t_mat, v * bb), bmm(t_mat, kb * eg), bmm_nt(q, k) * decay, q * eg, k * jnp.exp(glast - gcum),
            jnp.exp(glast))


def _dn_scan_step(u, w, qk, qd, kd, egl, S, onw):
    v_new = u - bmm(w, S)
    o = bmm(qd, S) + bmm(qk, v_new)
    return _rms(o, onw), S * egl + bmm_tn(kd, v_new)


def _head_gates(ab, alog, dtb, first_head, n_heads):
    gs, bs = [], []
    for i in range(n_heads):
        g, b = _dn_gates(ab, alog, dtb, first_head + i)
        gs.append(g.reshape(DN_NCH, DN_C, HD))
        bs.append(b.reshape(DN_NCH, DN_C, HD))
    return jnp.concatenate(gs, axis=0), jnp.concatenate(bs, axis=0)


def _to_batch(ref, n_heads):
    return jnp.concatenate([ref[:, i * HD:(i + 1) * HD].astype(F32).reshape(DN_NCH, DN_C, HD) for i in range(n_heads)],
                           axis=0)


def _from_batch(ref, val, n_heads):
    for i in range(n_heads):
        ref[:, i * HD:(i + 1) * HD] = val[i * DN_NCH:(i + 1) * DN_NCH].reshape(DN_TB, HD).astype(ref.dtype)


def _prep_specs(T, rev):
    nb = T // DN_TB
    blk = (lambda n: nb - 1 - n) if rev else (lambda n: n)
    ng = 8 // DN_NH
    head = [pl.BlockSpec((DN_TB, DN_NH * HD), functools.partial(lambda n, h, off: (blk(n), off + h), off=ng * s))
            for s in range(3)]
    ab = pl.BlockSpec((DN_TB, AB_PAD), lambda n, h: (blk(n), 3 * D // AB_PAD))
    row = pl.BlockSpec((1, HD), lambda n, h: (0, 0))
    wide = pl.BlockSpec((DN_TB, DN_NH * HD), lambda n, h: (blk(n), h))
    qk = pl.BlockSpec((DN_NCH, DN_NH, DN_C, DN_C), lambda n, h: (blk(n), h, 0, 0))
    eg = pl.BlockSpec((DN_NCH, DN_NH, 1, HD), lambda n, h: (blk(n), h, 0, 0))
    return nb, ng, head, ab, row, wide, qk, eg


def _dn_prep_fwd(cpre, pm, alog, dtb, name):
    T = cpre.shape[0]
    nb, ng, head, ab, row, wide, qks, egs = _prep_specs(T, False)

    def body(cq_ref, ck_ref, cv_ref, ab_ref, alog_ref, dtb_ref, u_ref, w_ref, qk_ref, qd_ref, kd_ref, e_ref):
        gb, bb = _head_gates(ab_ref[...], alog_ref[...], dtb_ref[...], pl.program_id(1) * DN_NH, DN_NH)
        u, w, qk, qd, kd, egl = _dn_prep(_to_batch(cq_ref, DN_NH), _to_batch(ck_ref, DN_NH), _to_batch(cv_ref, DN_NH),
                                         gb, bb)
        _from_batch(u_ref, u, DN_NH)
        _from_batch(w_ref, w, DN_NH)
        _from_batch(qd_ref, qd, DN_NH)
        _from_batch(kd_ref, kd, DN_NH)
        for i in range(DN_NH):
            qk_ref[:, i] = qk[i * DN_NCH:(i + 1) * DN_NCH].astype(BF16)
            e_ref[:, i] = egl[i * DN_NCH:(i + 1) * DN_NCH]

    return pl.pallas_call(
        body, name=name, grid=(nb, ng), in_specs=head + [ab, row, row],
        out_specs=[wide, wide, qks, wide, wide, egs],
        out_shape=[SDS((T, D), F32), SDS((T, D), BF16), SDS((T // DN_C, 8, DN_C, DN_C), BF16), SDS((T, D), BF16),
                   SDS((T, D), BF16), SDS((T // DN_C, 8, 1, HD), F32)],
        compiler_params=_params(2, VMEM_LIMIT))(cpre, cpre, cpre, pm, alog, dtb)


def _dn_prep_bwd(du, dw, dqk, dqd, dkd, degl, cpre, pm, alog, dtb, name):
    T = cpre.shape[0]
    nb, ng, head, ab, row, wide, qks, egs = _prep_specs(T, True)

    def body(du_ref, dw_ref, dqk_ref, dqd_ref, dkd_ref, de_ref, cq_ref, ck_ref, cv_ref, ab_ref, alog_ref, dtb_ref,
             dcq_ref, dck_ref, dcv_ref, dab_ref, dalog_ref, ddtb_ref):
        n, h = pl.program_id(0), pl.program_id(1)

        @pl.when((n == 0) & (h == 0))
        def _():
            dalog_ref[...] = jnp.zeros_like(dalog_ref)
            ddtb_ref[...] = jnp.zeros_like(ddtb_ref)

        @pl.when(h == 0)
        def _():
            dab_ref[...] = jnp.zeros_like(dab_ref)

        def fwd(cq, ck, cv, ab_v, alog_v, dtb_v):
            gb, bb = _head_gates(ab_v, alog_v, dtb_v, h * DN_NH, DN_NH)
            return _dn_prep(cq, ck, cv, gb, bb)

        _, vjp = jax.vjp(fwd, _to_batch(cq_ref, DN_NH), _to_batch(ck_ref, DN_NH), _to_batch(cv_ref, DN_NH), ab_ref[...],
                         alog_ref[...], dtb_ref[...])
        cot = (_to_batch(du_ref, DN_NH), _to_batch(dw_ref, DN_NH),
               jnp.concatenate([dqk_ref[:, i] for i in range(DN_NH)], axis=0), _to_batch(dqd_ref, DN_NH),
               _to_batch(dkd_ref, DN_NH), jnp.concatenate([de_ref[:, i] for i in range(DN_NH)], axis=0))
        dcq, dck, dcv, dab, dalog, ddtb = vjp(cot)
        _from_batch(dcq_ref, dcq, DN_NH)
        _from_batch(dck_ref, dck, DN_NH)
        _from_batch(dcv_ref, dcv, DN_NH)
        dab_ref[...] += dab
        dalog_ref[...] += dalog
        ddtb_ref[...] += ddtb

    dabspec = pl.BlockSpec((DN_TB, AB_PAD), lambda n, h: (nb - 1 - n, 0))
    return pl.pallas_call(
        body, name=name, grid=(nb, ng),
        in_specs=[wide, wide, qks, wide, wide, egs] + head + [ab, row, row],
        out_specs=[wide, wide, wide, dabspec, row, row],
        out_shape=[SDS((T, D), F32)] * 3 + [SDS((T, AB_PAD), F32)] + [SDS((1, HD), F32)] * 2,
        compiler_params=_params(2, VMEM_LIMIT))(du, dw, dqk, dqd, dkd, degl, cpre, cpre, cpre, pm, alog, dtb)


def _scan_specs(T, rev):
    nb = T // DN_TB
    blk = (lambda n: nb - 1 - n) if rev else (lambda n: n)
    wide = pl.BlockSpec((DN_TB, D), lambda n: (blk(n), 0))
    qk = pl.BlockSpec((DN_NCH, 8, DN_C, DN_C), lambda n: (blk(n), 0, 0, 0))
    eg = pl.BlockSpec((DN_NCH, 8, 1, HD), lambda n: (blk(n), 0, 0, 0))
    st = pl.BlockSpec((DN_NCH, 8, HD, HD), lambda n: (blk(n), 0, 0, 0))
    row = pl.BlockSpec((1, HD), lambda n: (0, 0))
    return nb, wide, qk, eg, st, row


def _heads_of(ref, rows):
    return jnp.concatenate([ref[rows, h * HD:(h + 1) * HD].astype(F32)[None] for h in range(8)], axis=0)


def _dn_scan_fwd(u, w, qk, qd, kd, egl, onw, name):
    T = u.shape[0]
    nb, wide, qks, egs, sts, row = _scan_specs(T, False)

    def body(u_ref, w_ref, qk_ref, qd_ref, kd_ref, e_ref, onw_ref, o_ref, st_ref, s_scr):
        @pl.when(pl.program_id(0) == 0)
        def _():
            s_scr[...] = jnp.zeros_like(s_scr)
        S = s_scr[...]
        for c in range(DN_NCH):
            rows = slice(c * DN_C, (c + 1) * DN_C)
            st_ref[c] = S
            o, S = _dn_scan_step(_heads_of(u_ref, rows), _heads_of(w_ref, rows), qk_ref[c].astype(F32),
                                 _heads_of(qd_ref, rows), _heads_of(kd_ref, rows), e_ref[c], S, onw_ref[...])
            for h in range(8):
                o_ref[rows, h * HD:(h + 1) * HD] = o[h]
        s_scr[...] = S

    return pl.pallas_call(
        body, name=name, grid=(nb,), in_specs=[wide, wide, qks, wide, wide, egs, row], out_specs=[wide, sts],
        out_shape=[SDS((T, D), F32), SDS((T // DN_C, 8, HD, HD), F32)],
        scratch_shapes=[pltpu.VMEM((8, HD, HD), F32)],
        compiler_params=_params(1, VMEM_LIMIT))(u, w, qk, qd, kd, egl, onw)


def _dn_scan_bwd(do, u, w, qk, qd, kd, egl, st, onw, name):
    T = u.shape[0]
    nb, wide, qks, egs, sts, row = _scan_specs(T, True)

    def body(do_ref, u_ref, w_ref, qk_ref, qd_ref, kd_ref, e_ref, st_ref, onw_ref,
             du_ref, dw_ref, dqk_ref, dqd_ref, dkd_ref, de_ref, donw_ref, ds_scr):
        @pl.when(pl.program_id(0) == 0)
        def _():
            ds_scr[...] = jnp.zeros_like(ds_scr)
            donw_ref[...] = jnp.zeros_like(donw_ref)
        dS = ds_scr[...]
        donw = jnp.zeros((1, HD), F32)
        for c in reversed(range(DN_NCH)):
            rows = slice(c * DN_C, (c + 1) * DN_C)
            _, vjp = jax.vjp(_dn_scan_step, _heads_of(u_ref, rows), _heads_of(w_ref, rows), qk_ref[c].astype(F32),
                             _heads_of(qd_ref, rows), _heads_of(kd_ref, rows), e_ref[c], st_ref[c], onw_ref[...])
            du, dw, dqk, dqd, dkd, de, dS, dn = vjp((_heads_of(do_ref, rows), dS))
            for h in range(8):
                cols = slice(h * HD, (h + 1) * HD)
                du_ref[rows, cols] = du[h]
                dw_ref[rows, cols] = dw[h]
                dqd_ref[rows, cols] = dqd[h]
                dkd_ref[rows, cols] = dkd[h]
            dqk_ref[c] = dqk
            de_ref[c] = de
            donw += dn
        ds_scr[...] = dS
        donw_ref[...] += donw

    return pl.pallas_call(
        body, name=name, grid=(nb,), in_specs=[wide, wide, wide, qks, wide, wide, egs, sts, row],
        out_specs=[wide, wide, qks, wide, wide, egs, row],
        out_shape=[SDS((T, D), F32), SDS((T, D), F32), SDS((T // DN_C, 8, DN_C, DN_C), F32), SDS((T, D), F32),
                   SDS((T, D), F32), SDS((T // DN_C, 8, 1, HD), F32), SDS((1, HD), F32)],
        scratch_shapes=[pltpu.VMEM((8, HD, HD), F32)],
        compiler_params=_params(1, VMEM_LIMIT))(do, u, w, qk, qd, kd, egl, st, onw)


def _adamw(w, g, m, v, name):
    R, C = w.shape
    tr = 256 if R % 256 == 0 and R > 256 else R
    c1 = 1.0 - ADAM_B1 ** ADAM_STEP
    c2 = 1.0 - ADAM_B2 ** ADAM_STEP

    def body(w_ref, g_ref, m_ref, v_ref, d_ref, nm_ref, nv_ref):
        gv = g_ref[...]
        nm = ADAM_B1 * m_ref[...] + (1.0 - ADAM_B1) * gv
        nv = ADAM_B2 * v_ref[...] + (1.0 - ADAM_B2) * (gv * gv)
        nm_ref[...] = nm
        nv_ref[...] = nv
        d_ref[...] = -ADAM_LR * ((nm / c1) / (jnp.sqrt(nv / c2) + ADAM_EPS) + ADAM_WD * w_ref[...])

    spec = pl.BlockSpec((tr, C), lambda i: (i, 0))
    return pl.pallas_call(
        body, name=name, grid=(R // tr,), in_specs=[spec] * 4, out_specs=[spec] * 3,
        out_shape=[SDS((R, C), F32)] * 3, compiler_params=_params(1, VMEM_LIMIT))(w, g, m, v)


def _local_step(x, mem, target, wts, sm):
    kinds = [i % 3 for i in range(DEPTH)]
    mnw = sm["mem_norm_w"].reshape(1, D)
    kv = _memkv_fwd(mem, mnw, wts["wkv"])
    saved = []
    for i, kind in enumerate(kinds):
        j = i // 3
        npre = sm["norm_pre"][i].reshape(1, D)
        npost = sm["norm_post"][i].reshape(1, D)
        pm, pg, h = _inproj_fwd(x, npre, wts["wm"][i], wts["wg"][i], f"inproj_fwd_{i}")
        extra = None
        if kind == 0:
            bs3 = jnp.broadcast_to(sm["a_b_s"][j][:, :, None], (8, HD, HD))
            ymix = _gmlp_fwd(pm, sm["a_ln_w"][j].reshape(1, D), sm["a_ln_b"][j].reshape(1, D), sm["a_w_s"][j], bs3,
                             f"gmlp_fwd_{i}")
            extra = bs3
        elif kind == 1:
            ymix = _sconv_fwd(pm, sm["b_conv_w"][j], f"sconv_fwd_{i}")
        else:
            cpre = _dnconv_fwd(pm, sm["c_conv_w"][j], f"dnconv_fwd_{i}")
            alog = jnp.pad(sm["c_a_log"][j], (0, HD - 8)).reshape(1, HD)
            dtb = jnp.pad(sm["c_dt_bias"][j], (0, HD - 8)).reshape(1, HD)
            onw = sm["c_o_norm_w"][j].reshape(1, HD)
            prep = _dn_prep_fwd(cpre, pm, alog, dtb, f"dn_prep_fwd_{i}")
            ymix, st = _dn_scan_fwd(*prep, onw, f"dn_scan_fwd_{i}")
            extra = (cpre, prep, st, alog, dtb, onw)
        ycat = _ag_fwd(ymix, pg, kv, f"ag_fwd_{i}")
        o, xn = _outproj_fwd(ycat, wts["wo"][i], x, npost, f"outproj_fwd_{i}")
        saved.append((x, h, pm, pg, ymix, ycat, o, extra))
        x = xn

    loss, dx = _loss_head(x, target)

    g = {"wm": [None] * DEPTH, "wg": [None] * DEPTH, "wo": [None] * DEPTH, "norm_pre": [None] * DEPTH,
         "norm_post": [None] * DEPTH}
    dkv = jnp.zeros((N_MEM, 2 * D_XA), F32)
    for i in reversed(range(DEPTH)):
        kind, j = kinds[i], i // 3
        xi, h, pm, pg, ymix, ycat, o, extra = saved[i]
        npre = sm["norm_pre"][i].reshape(1, D)
        npost = sm["norm_post"][i].reshape(1, D)
        dycat, dobf, g["norm_post"][i] = _outproj_bwd(dx, o, npost, wts["wo"][i], f"outproj_bwd_{i}")
        g["wo"][i] = _matmul_tn(ycat, dobf, f"dwo_{i}")
        dymix, dpg, dkv = _ag_bwd(dycat, ymix, pg, kv, dkv, f"ag_bwd_{i}")
        if kind == 0:
            dpm, dlnw, dlnb, dws, dbs3 = _gmlp_bwd(dymix, pm, sm["a_ln_w"][j].reshape(1, D),
                                                   sm["a_ln_b"][j].reshape(1, D), sm["a_w_s"][j], extra,
                                                   f"gmlp_bwd_{i}")
            g.setdefault("a_ln_w", {})[j] = dlnw.reshape(D)
            g.setdefault("a_ln_b", {})[j] = dlnb.reshape(D)
            g.setdefault("a_w_s", {})[j] = dws
            g.setdefault("a_b_s", {})[j] = dbs3[:, :, 0]
        elif kind == 1:
            dpm, dcw = _sconv_bwd(dymix, pm, sm["b_conv_w"][j], f"sconv_bwd_{i}")
            g.setdefault("b_conv_w", {})[j] = dcw
        else:
            cpre, prep, st, alog, dtb, onw = extra
            *dprep, donw = _dn_scan_bwd(dymix, *prep, st, onw, f"dn_scan_bwd_{i}")
            dcq, dck, dcv, dab, dalog, ddtb = _dn_prep_bwd(*dprep, cpre, pm, alog, dtb, f"dn_prep_bwd_{i}")
            dpm, dcw = _dnconv_bwd(dcq, dck, dcv, dab, pm, sm["c_conv_w"][j], f"dnconv_bwd_{i}")
            g.setdefault("c_conv_w", {})[j] = dcw
            g.setdefault("c_a_log", {})[j] = dalog[0, :8]
            g.setdefault("c_dt_bias", {})[j] = ddtb[0, :8]
            g.setdefault("c_o_norm_w", {})[j] = donw[0]
        g["wm"][i] = _matmul_tn(h, dpm, f"dwm_{i}")
        g["wg"][i] = _matmul_tn(h, dpg, f"dwg_{i}")
        dx, g["norm_pre"][i] = _inproj_bwd(dpm, dpg, xi, npre, wts["wm"][i], wts["wg"][i], dx, f"inproj_bwd_{i}")
    g["mem_norm_w"], g["wkv"] = _memkv_bwd(mem, mnw, wts["wkv"], dkv)
    return loss[0, 0], dx, g


ANY = pl.BlockSpec(memory_space=pl.ANY)


def _place():
    return lax.axis_index("x"), lax.axis_index("y"), lax.axis_index("c")


def _gather_weights(big, vec):
    def body(big_ref, vec_ref, ob_ref, ov_ref, ici_send, ici_recv, d2d_send, d2d_recv, vec_send, vec_recv):
        x, y, c = _place()
        chip = 2 * x + y
        peers = [(1 - x, y), (x, 1 - y), (1 - x, 1 - y)]

        def rows(half, k):
            return pl.ds(half * HALF_ROWS + k * CHUNK_ROWS, CHUNK_ROWS)

        def over_ici(j, k, slab):
            px, py = peers[j]
            i = j * N_CHUNKS + k
            return pltpu.make_async_remote_copy(
                src_ref=big_ref.at[rows(c, k)], dst_ref=ob_ref.at[slab, rows(c, k)], send_sem=ici_send.at[i],
                recv_sem=ici_recv.at[i], device_id=(px, py, c), device_id_type=MESH)

        def over_d2d(j, k, half):
            px, py = peers[j]
            i = j * N_CHUNKS + k
            where = ob_ref.at[2 * px + py, rows(half, k)]
            return pltpu.make_async_remote_copy(
                src_ref=where, dst_ref=where, send_sem=d2d_send.at[i], recv_sem=d2d_recv.at[i],
                device_id=(x, y, 1 - c), device_id_type=MESH)

        def small(j, slab):
            px, py = peers[j]
            return pltpu.make_async_remote_copy(
                src_ref=vec_ref, dst_ref=ov_ref.at[slab], send_sem=vec_send.at[j], recv_sem=vec_recv.at[j],
                device_id=(px, py, c), device_id_type=MESH)

        sends = [small(j, chip) for j in range(3)] + [over_ici(j, k, chip) for k in range(N_CHUNKS) for j in range(3)]
        for cp in sends:
            cp.start()
        forwards = []
        for k in range(N_CHUNKS):
            for j, (px, py) in enumerate(peers):
                over_ici(j, k, 2 * px + py).wait_recv()
                forwards.append(over_d2d(j, k, c))
                forwards[-1].start()
        for k in range(N_CHUNKS):
            for j in range(3):
                over_d2d(j, k, 1 - c).wait_recv()
        for j, (px, py) in enumerate(peers):
            small(j, 2 * px + py).wait_recv()
        for cp in sends + forwards:
            cp.wait_send()

    n = 3 * N_CHUNKS
    dma = pltpu.SemaphoreType.DMA
    return pl.pallas_call(
        body, name="gather_weights", in_specs=[ANY, ANY], out_specs=[ANY, ANY],
        out_shape=[SDS((4,) + big.shape, big.dtype), SDS((4,) + vec.shape, vec.dtype)],
        scratch_shapes=[dma((n,)), dma((n,)), dma((n,)), dma((n,)), dma((3,)), dma((3,))])(big, vec)


HALF_ROWS = 3328
CHUNK_ROWS = 256
N_CHUNKS = HALF_ROWS // CHUNK_ROWS


def _swap_halves(gbig, small):
    def body(g_ref, s_ref, ob_ref, os_ref, send_sems, recv_sems):
        x, y, c = _place()
        copies = []
        for s in range(4):
            for k in range(N_CHUNKS):
                rows = pl.ds(k * CHUNK_ROWS, CHUNK_ROWS)
                copies.append(pltpu.make_async_remote_copy(
                    src_ref=g_ref.at[s, 1 - c, rows], dst_ref=ob_ref.at[s, rows], send_sem=send_sems.at[len(copies)],
                    recv_sem=recv_sems.at[len(copies)], device_id=(x, y, 1 - c), device_id_type=MESH))
        copies.append(pltpu.make_async_remote_copy(
            src_ref=s_ref, dst_ref=os_ref, send_sem=send_sems.at[len(copies)], recv_sem=recv_sems.at[len(copies)],
            device_id=(x, y, 1 - c), device_id_type=MESH))
        for cp in copies:
            cp.start()
        for cp in copies:
            cp.wait_recv()
        for cp in copies:
            cp.wait_send()

    n = 4 * N_CHUNKS + 1
    return pl.pallas_call(
        body, name="swap_halves", in_specs=[ANY, ANY], out_specs=[ANY, ANY],
        out_shape=[SDS((4, HALF_ROWS, D), F32), SDS(small.shape, F32)],
        scratch_shapes=[pltpu.SemaphoreType.DMA((n,)), pltpu.SemaphoreType.DMA((n,))])(gbig, small)


def _pair_sum(gbig, other):
    def body(g_ref, o_ref, pb_ref, own_ref):
        x, y, c = _place()
        v = jnp.where(c == 0, g_ref[0], g_ref[1]) + o_ref[...]
        pb_ref[...] = v.astype(BF16)

        @pl.when(pl.program_id(1) == 2 * x + y)
        def _():
            own_ref[...] = v

    return pl.pallas_call(
        body, name="pair_sum", grid=(N_CHUNKS, 4),
        in_specs=[pl.BlockSpec((None, 2, CHUNK_ROWS, D), lambda i, s: (s, 0, i, 0)),
                  pl.BlockSpec((None, CHUNK_ROWS, D), lambda i, s: (s, i, 0))],
        out_specs=[pl.BlockSpec((None, CHUNK_ROWS, D), lambda i, s: (s, i, 0)),
                   pl.BlockSpec((CHUNK_ROWS, D), lambda i, s: (i, 0))],
        out_shape=[SDS((4, HALF_ROWS, D), BF16), SDS((HALF_ROWS, D), F32)],
        compiler_params=_params(2, VMEM_LIMIT))(gbig, other)


def _add(a, b, name):
    def body(a_ref, b_ref, o_ref):
        o_ref[...] = a_ref[...] + b_ref[...]

    return pl.pallas_call(body, name=name, out_shape=SDS(a.shape, a.dtype), compiler_params=_params(0, VMEM_LIMIT))(a, b)


def _chip_exchange(pb, ps):
    n_small = ps.shape[0]

    def body(pb_ref, ps_ref, lb_ref, ls_ref, send_sems, recv_sems):
        x, y, c = _place()
        chip = 2 * x + y
        peers = [(1 - x, y), (x, 1 - y), (1 - x, 1 - y)]

        def copies(slab_of):
            out = []
            for j, (px, py) in enumerate(peers):
                for k in range(N_CHUNKS):
                    rows = pl.ds(k * CHUNK_ROWS, CHUNK_ROWS)
                    out.append(pltpu.make_async_remote_copy(
                        src_ref=pb_ref.at[2 * px + py, rows], dst_ref=lb_ref.at[slab_of(j), rows],
                        send_sem=send_sems.at[len(out)], recv_sem=recv_sems.at[len(out)], device_id=(px, py, c),
                        device_id_type=MESH))
                out.append(pltpu.make_async_remote_copy(
                    src_ref=ps_ref, dst_ref=ls_ref.at[slab_of(j)], send_sem=send_sems.at[len(out)],
                    recv_sem=recv_sems.at[len(out)], device_id=(px, py, c), device_id_type=MESH))
            return out

        sends = copies(lambda j: chip)
        for cp in sends:
            cp.start()
        for cp in copies(lambda j: 2 * peers[j][0] + peers[j][1]):
            cp.wait_recv()
        for cp in sends:
            cp.wait_send()

    n = 3 * (N_CHUNKS + 1)
    return pl.pallas_call(
        body, name="chip_exchange", in_specs=[ANY, ANY], out_specs=[ANY, ANY],
        out_shape=[SDS((4, HALF_ROWS, D), BF16), SDS((4, n_small, D), F32)],
        scratch_shapes=[pltpu.SemaphoreType.DMA((n,)), pltpu.SemaphoreType.DMA((n,))])(pb, ps)


def _chip_sum(own, land):
    def body(own_ref, l_ref, o_ref):
        chip = 2 * lax.axis_index("x") + lax.axis_index("y")
        acc = jnp.where(chip == 0, own_ref[...], l_ref[0].astype(F32))
        for s in range(1, 4):
            acc = acc + jnp.where(chip == s, own_ref[...], l_ref[s].astype(F32))
        o_ref[...] = acc

    return pl.pallas_call(
        body, name="chip_sum", grid=(N_CHUNKS,),
        in_specs=[pl.BlockSpec((CHUNK_ROWS, D), lambda i: (i, 0)), pl.BlockSpec((4, CHUNK_ROWS, D), lambda i: (0, i, 0))],
        out_specs=pl.BlockSpec((CHUNK_ROWS, D), lambda i: (i, 0)), out_shape=SDS((HALF_ROWS, D), F32),
        compiler_params=_params(1, VMEM_LIMIT))(own, land)


def _sum4(own, land):
    def body(own_ref, l_ref, o_ref):
        chip = 2 * lax.axis_index("x") + lax.axis_index("y")
        acc = jnp.where(chip == 0, own_ref[...], l_ref[0])
        for s in range(1, 4):
            acc = acc + jnp.where(chip == s, own_ref[...], l_ref[s])
        o_ref[...] = acc

    return pl.pallas_call(body, name="sum_small", out_shape=SDS(own.shape, own.dtype),
                          compiler_params=_params(0, VMEM_LIMIT))(own, land)


def _share_half(r):
    def body(r_ref, o_ref, send_sems, recv_sems):
        x, y, c = _place()
        copies = [pltpu.make_async_remote_copy(
            src_ref=r_ref.at[pl.ds(k * CHUNK_ROWS, CHUNK_ROWS)], dst_ref=o_ref.at[pl.ds(k * CHUNK_ROWS, CHUNK_ROWS)],
            send_sem=send_sems.at[k], recv_sem=recv_sems.at[k], device_id=(x, y, 1 - c), device_id_type=MESH)
            for k in range(N_CHUNKS)]
        for cp in copies:
            cp.start()
        for cp in copies:
            cp.wait_recv()
        for cp in copies:
            cp.wait_send()

    return pl.pallas_call(
        body, name="share_half", in_specs=[ANY], out_specs=ANY, out_shape=SDS((HALF_ROWS, D), F32),
        scratch_shapes=[pltpu.SemaphoreType.DMA((N_CHUNKS,)), pltpu.SemaphoreType.DMA((N_CHUNKS,))])(r)


_SMALL = ["mem_norm_w", "norm_pre", "norm_post", "a_ln_w", "a_ln_b", "a_w_s", "a_b_s", "b_conv_w", "c_conv_w",
          "c_a_log", "c_dt_bias", "c_o_norm_w"]
_SMALL_SHAPES = {"mem_norm_w": (D,), "norm_pre": (4, D), "norm_post": (4, D), "a_ln_w": (2, D), "a_ln_b": (2, D),
                 "a_w_s": (2, 8, HD, HD), "a_b_s": (2, 8, HD), "b_conv_w": (1, 3, D), "c_conv_w": (1, 4, 3 * D),
                 "c_a_log": (1, 8), "c_dt_bias": (1, 8), "c_o_norm_w": (1, HD)}
_SHARDED_SMALL = {"a_ln_w": D // 4, "a_ln_b": D // 4, "b_conv_w": D // 4, "c_conv_w": 3 * D // 4}
_ROWS = [2048, 1280, 1284, 1536, 256]
_BIG_ROWS = sum(_ROWS)
_BIG_PAD = 6416
_SMALL_ROWS = 288


def _size(shape):
    n = 1
    for d in shape:
        n *= d
    return n


def kernel(x, mem, mem_norm_w, w_mem_kv, norm_pre, norm_post, w_out, a_w_in, a_ln_w, a_ln_b, a_w_s, a_b_s, b_w_in, b_conv_w, c_w_in, c_conv_w, c_a_log, c_dt_bias, c_o_norm_w, loss_target, m_mem_norm_w, m_w_mem_kv, m_norm_pre, m_norm_post, m_w_out, m_a_w_in, m_a_ln_w, m_a_ln_b, m_a_w_s, m_a_b_s, m_b_w_in, m_b_conv_w, m_c_w_in, m_c_conv_w, m_c_a_log, m_c_dt_bias, m_c_o_norm_w, v_mem_norm_w, v_w_mem_kv, v_norm_pre, v_norm_post, v_w_out, v_a_w_in, v_a_ln_w, v_a_ln_b, v_a_w_s, v_a_b_s, v_b_w_in, v_b_conv_w, v_c_w_in, v_c_conv_w, v_c_a_log, v_c_dt_bias, v_c_o_norm_w):
    names = ["mem_norm_w", "w_mem_kv", "norm_pre", "norm_post", "w_out", "a_w_in", "a_ln_w", "a_ln_b", "a_w_s", "a_b_s",
             "b_w_in", "b_conv_w", "c_w_in", "c_conv_w", "c_a_log", "c_dt_bias", "c_o_norm_w"]
    w = dict(zip(names, [mem_norm_w, w_mem_kv, norm_pre, norm_post, w_out, a_w_in, a_ln_w, a_ln_b, a_w_s, a_b_s, b_w_in,
                         b_conv_w, c_w_in, c_conv_w, c_a_log, c_dt_bias, c_o_norm_w]))
    m = dict(zip(names, [m_mem_norm_w, m_w_mem_kv, m_norm_pre, m_norm_post, m_w_out, m_a_w_in, m_a_ln_w, m_a_ln_b, m_a_w_s,
                         m_a_b_s, m_b_w_in, m_b_conv_w, m_c_w_in, m_c_conv_w, m_c_a_log, m_c_dt_bias, m_c_o_norm_w]))
    v = dict(zip(names, [v_mem_norm_w, v_w_mem_kv, v_norm_pre, v_norm_post, v_w_out, v_a_w_in, v_a_ln_w, v_a_ln_b, v_a_w_s,
                         v_a_b_s, v_b_w_in, v_b_conv_w, v_c_w_in, v_c_conv_w, v_c_a_log, v_c_dt_bias, v_c_o_norm_w]))
    chip = 2 * lax.axis_index("x") + lax.axis_index("y")

    big = jnp.concatenate([a_w_in.reshape(_ROWS[0], D), b_w_in.reshape(_ROWS[1], D), c_w_in.reshape(_ROWS[2], D),
                           w_out.reshape(_ROWS[3], D), w_mem_kv, jnp.zeros((2 * HALF_ROWS - _BIG_ROWS, D), F32)],
                          axis=0).astype(BF16)
    vec = jnp.concatenate([a_ln_w.reshape(-1), a_ln_b.reshape(-1), b_conv_w.reshape(-1), c_conv_w.reshape(-1)])
    vec = jnp.pad(vec, (0, 8 * D - vec.shape[0])).reshape(8, D)
    gbig, gvec = _gather_weights(big, vec)
    gbig = lax.dynamic_update_slice(gbig, big[None], (chip, 0, 0))
    gvec = lax.dynamic_update_slice(gvec, vec[None], (chip, 0, 0))
    r0, r1, r2, r3, r4 = [sum(_ROWS[:k]) for k in range(5)]
    fa = gbig[:, r0:r1].reshape(4, 2, D, D).transpose(1, 2, 0, 3).reshape(2, D, 4 * D)
    fb = gbig[:, r1:r2].reshape(4, D, 1280).transpose(1, 0, 2).reshape(D, 5120)
    fc = gbig[:, r2:r3].reshape(4, D, 1284).transpose(1, 0, 2).reshape(D, 5136)
    fo = gbig[:, r3:r4].reshape(4, 4, 384, D).transpose(1, 0, 2, 3).reshape(4, D_CAT, D)
    fkv = gbig[:, r4:_BIG_ROWS].reshape(D, 2 * D_XA)
    gv = gvec.reshape(4, 8 * D)
    sm = {"mem_norm_w": mem_norm_w, "norm_pre": norm_pre, "norm_post": norm_post, "a_w_s": a_w_s, "a_b_s": a_b_s,
          "c_a_log": c_a_log, "c_dt_bias": c_dt_bias, "c_o_norm_w": c_o_norm_w,
          "a_ln_w": gv[:, 0:512].reshape(4, 2, 256).transpose(1, 0, 2).reshape(2, D),
          "a_ln_b": gv[:, 512:1024].reshape(4, 2, 256).transpose(1, 0, 2).reshape(2, D),
          "b_conv_w": gv[:, 1024:1792].reshape(4, 1, 3, 256).transpose(1, 2, 0, 3).reshape(1, 3, D),
          "c_conv_w": gv[:, 1792:4864].reshape(4, 1, 4, 768).transpose(1, 2, 0, 3).reshape(1, 4, 3 * D)}
    c_mix = jnp.concatenate([fc[:, :3 * D + 16], jnp.zeros((D, AB_PAD - 16), BF16)], axis=1)
    wts = {"wkv": fkv, "wo": fo,
           "wm": [fa[0][:, :2 * D], fb[:, :3 * D], c_mix, fa[1][:, :2 * D]],
           "wg": [fa[0][:, 2 * D:], fb[:, 3 * D:], fc[:, 3 * D + 16:], fa[1][:, 2 * D:]]}

    loss, dx, g = _local_step(x[0], mem[0], loss_target[0], wts, sm)
    loss = lax.psum(loss, ("x", "y", "c"))

    ga = jnp.stack([jnp.concatenate([g["wm"][i], g["wg"][i]], axis=1) for i in (0, 3)])
    gb = jnp.concatenate([g["wm"][1], g["wg"][1]], axis=1)
    gc = jnp.concatenate([g["wm"][2][:, :3 * D + 16], g["wg"][2]], axis=1)
    go = jnp.stack(g["wo"])
    gbig_all = jnp.concatenate([
        ga.reshape(2, D, 4, D).transpose(2, 0, 1, 3).reshape(4, _ROWS[0], D),
        gb.reshape(D, 4, 1280).transpose(1, 0, 2).reshape(4, _ROWS[1], D),
        gc.reshape(D, 4, 1284).transpose(1, 0, 2).reshape(4, _ROWS[2], D),
        go.reshape(4, 4, 384, D).transpose(1, 0, 2, 3).reshape(4, _ROWS[3], D),
        g["wkv"].reshape(4, _ROWS[4], D),
        jnp.zeros((4, 2 * HALF_ROWS - _BIG_ROWS, D), F32)], axis=1).reshape(4, 2, HALF_ROWS, D)
    gs = {"mem_norm_w": g["mem_norm_w"], "norm_pre": jnp.concatenate(g["norm_pre"]),
          "norm_post": jnp.concatenate(g["norm_post"])}
    for n in _SMALL[3:]:
        gs[n] = jnp.stack([g[n][j] for j in sorted(g[n])])
    flat = jnp.concatenate([gs[n].reshape(-1) for n in _SMALL])
    small = jnp.pad(flat, (0, _SMALL_ROWS * D - flat.shape[0])).reshape(_SMALL_ROWS, D)
    other_big, other_small = _swap_halves(gbig_all, small)
    pair_bf16, pair_own = _pair_sum(gbig_all, other_big)
    pair_small = _add(small, other_small, "pair_sum_small")
    land_big, land_small = _chip_exchange(pair_bf16, pair_small)
    mine = _chip_sum(pair_own, land_big)
    theirs = _share_half(mine)
    south = lax.axis_index("c") == 0
    gshard = jnp.concatenate([jnp.where(south, mine, theirs), jnp.where(south, theirs, mine)], axis=0)
    grads = {"a_w_in": gshard[r0:r1].reshape(a_w_in.shape), "b_w_in": gshard[r1:r2].reshape(b_w_in.shape),
             "c_w_in": gshard[r2:r3].reshape(c_w_in.shape), "w_out": gshard[r3:r4].reshape(w_out.shape),
             "w_mem_kv": gshard[r4:_BIG_ROWS]}
    flat = _sum4(pair_small, land_small).reshape(-1)
    off = 0
    for n in _SMALL:
        shape = _SMALL_SHAPES[n]
        full = flat[off:off + _size(shape)].reshape(shape)
        off += _size(shape)
        if n in _SHARDED_SMALL:
            full = lax.dynamic_slice_in_dim(full, chip * _SHARDED_SMALL[n], _SHARDED_SMALL[n], axis=len(shape) - 1)
        grads[n] = full

    delta, new_m, new_v = {}, {}, {}
    for n in names:
        shape = w[n].shape
        view = (1, shape[0]) if len(shape) == 1 else (_size(shape[:-1]), shape[-1])
        d_, m_, v_ = _adamw(w[n].reshape(view), grads[n].reshape(view), m[n].reshape(view), v[n].reshape(view),
                            f"adamw_{n}")
        delta[n], new_m[n], new_v[n] = d_.reshape(shape), m_.reshape(shape), v_.reshape(shape)
    return (loss, dx[None], *[grads[n].reshape(w[n].shape) for n in names], *[delta[n] for n in names],
            *[new_m[n] for n in names], *[new_v[n] for n in names])
```

```python
import functools

import jax
import jax.numpy as jnp
from jax import lax
from jax.experimental import pallas as pl
from jax.experimental.pallas import tpu as pltpu

F32 = jnp.float32
BF16 = jnp.bfloat16
HI = lax.Precision.HIGHEST
MESH = pl.DeviceIdType.MESH
SDS = jax.ShapeDtypeStruct

D = 1024
D_XA = 512
D_CAT = 1536
N_MEM = 256
HD = 128
DEPTH = 4
EPS = 1e-6
TT = 512
DN_C = 64
DN_TB = 256
HALO = 8
AB_PAD = 128
VMEM_LIMIT = 56 * 1024 * 1024
GRAD_TILE = {0: 1024, 1: 256, 2: None}

ADAM_LR, ADAM_B1, ADAM_B2, ADAM_EPS, ADAM_WD, ADAM_STEP = 0.001, 0.9, 0.999, 1e-08, 0.01, 10


def _params(n_grid, vmem=None):
    return pltpu.CompilerParams(dimension_semantics=("arbitrary",) * n_grid, vmem_limit_bytes=vmem)


def _rms(x, w):
    return x * lax.rsqrt(jnp.mean(x * x, axis=-1, keepdims=True) + EPS) * w


def _dot_nn(a, b):
    return jnp.dot(a.astype(BF16), b.astype(BF16), preferred_element_type=F32)


def _dot_nt(a, b):
    return lax.dot_general(a.astype(BF16), b.astype(BF16), (((1,), (1,)), ((), ())), preferred_element_type=F32)


def _dot_tn(a, b):
    return lax.dot_general(a.astype(BF16), b.astype(BF16), (((0,), (0,)), ((), ())), preferred_element_type=F32)


@jax.custom_vjp
def mm(a, b):
    return _dot_nn(a, b)


mm.defvjp(lambda a, b: (_dot_nn(a, b), (a, b)), lambda r, g: (_dot_nt(g, r[1]), _dot_tn(r[0], g)))


@jax.custom_vjp
def mm_nt(a, b):
    return _dot_nt(a, b)


mm_nt.defvjp(lambda a, b: (_dot_nt(a, b), (a, b)), lambda r, g: (_dot_nn(g, r[1]), _dot_tn(g, r[0])))


@jax.custom_vjp
def mm_tn(a, b):
    return _dot_tn(a, b)


mm_tn.defvjp(lambda a, b: (_dot_tn(a, b), (a, b)), lambda r, g: (_dot_nt(r[1], g), _dot_nn(r[0], g)))


def _dot_hi(a, b):
    return jnp.dot(a, b, precision=HI, preferred_element_type=F32)


def _row_spec(width, tile=TT):
    return pl.BlockSpec((tile, width), lambda i: (i, 0))


def _full_spec(shape):
    return pl.BlockSpec(shape, lambda *_: (0,) * len(shape))


def _widths(blocks):
    return [b.shape[1] for b in blocks]


def _inproj_fwd(x, nw, mix, gate, name):
    T, nm = x.shape[0], len(mix)
    M, G = sum(_widths(mix)), sum(_widths(gate))

    def body(x_ref, nw_ref, *refs):
        blocks, (pm_ref, pg_ref, h_ref) = refs[:-3], refs[-3:]
        h = _rms(x_ref[...], nw_ref[...]).astype(BF16)
        h_ref[...] = h
        for p_ref, group in ((pm_ref, blocks[:nm]), (pg_ref, blocks[nm:])):
            off = 0
            for w_ref in group:
                p_ref[:, off:off + w_ref.shape[1]] = jnp.dot(h, w_ref[...], preferred_element_type=F32)
                off += w_ref.shape[1]

    return pl.pallas_call(
        body, name=name, grid=(T // TT,),
        in_specs=[_row_spec(D), _full_spec((1, D))] + [_full_spec((D, w)) for w in _widths(mix + gate)],
        out_specs=[_row_spec(M), _row_spec(G), _row_spec(D)],
        out_shape=[SDS((T, M), F32), SDS((T, G), F32), SDS((T, D), BF16)],
        compiler_params=_params(1, VMEM_LIMIT))(x, nw, *mix, *gate)


def _inproj_bwd(dpm, dpg, x, nw, mix, gate, dxc, name):
    T, nm = x.shape[0], len(mix)
    M, G = sum(_widths(mix)), sum(_widths(gate))

    def body(dpm_ref, dpg_ref, x_ref, nw_ref, *refs):
        blocks, (dxc_ref, dx_ref, dnw_ref) = refs[:-3], refs[-3:]
        dh = None
        for dp_ref, group in ((dpm_ref, blocks[:nm]), (dpg_ref, blocks[nm:])):
            off = 0
            for w_ref in group:
                part = _dot_nt(dp_ref[:, off:off + w_ref.shape[1]], w_ref[...])
                dh = part if dh is None else dh + part
                off += w_ref.shape[1]
        _, vjp = jax.vjp(_rms, x_ref[...], nw_ref[...])
        dxr, dnw = vjp(dh)
        dx_ref[...] = dxc_ref[...] + dxr

        @pl.when(pl.program_id(0) == 0)
        def _():
            dnw_ref[...] = jnp.zeros_like(dnw_ref)
        dnw_ref[...] += dnw

    return pl.pallas_call(
        body, name=name, grid=(T // TT,),
        in_specs=[_row_spec(M), _row_spec(G), _row_spec(D), _full_spec((1, D))]
        + [_full_spec((D, w)) for w in _widths(mix + gate)] + [_row_spec(D)],
        out_specs=[_row_spec(D), _full_spec((1, D))],
        out_shape=[SDS((T, D), F32), SDS((1, D), F32)],
        compiler_params=_params(1, VMEM_LIMIT))(dpm, dpg, x, nw, *mix, *gate, dxc)


def _matmul_tn(a, b, name, sub=None):
    T, K = a.shape
    N = b.shape[1]
    tn = 1024 if N % 1024 == 0 else (640 if N % 640 == 0 else N)
    tt = min(1024, T)
    n_sub = 1 if sub is None else tn // sub

    def body(a_ref, b_ref, o_ref):
        @pl.when(pl.program_id(1) == 0)
        def _():
            o_ref[...] = jnp.zeros_like(o_ref)
        res = _dot_tn(a_ref[...], b_ref[...])
        if sub is None:
            o_ref[...] += res
        else:
            for i in range(n_sub):
                o_ref[i] += res[:, i * sub:(i + 1) * sub]

    if sub is None:
        out_spec, out_shape = pl.BlockSpec((K, tn), lambda j, t: (0, j)), SDS((K, N), F32)
    else:
        out_spec, out_shape = pl.BlockSpec((n_sub, K, sub), lambda j, t: (j, 0, 0)), SDS((N // sub, K, sub), F32)
    return pl.pallas_call(
        body, name=name, grid=(N // tn, T // tt),
        in_specs=[pl.BlockSpec((tt, K), lambda j, t: (t, 0)), pl.BlockSpec((tt, tn), lambda j, t: (t, j))],
        out_specs=out_spec, out_shape=out_shape,
        compiler_params=_params(2, VMEM_LIMIT))(a, b)


def _memkv_fn(mem, w, wkv):
    return mm(_rms(mem, w), wkv)


def _memkv_fwd(mem, w, wkv):
    def body(mem_ref, w_ref, wkv_ref, kv_ref):
        kv_ref[...] = _memkv_fn(mem_ref[...], w_ref[...], wkv_ref[...])

    return pl.pallas_call(body, name="memkv_fwd", out_shape=SDS((N_MEM, 2 * D_XA), F32),
                          compiler_params=_params(0, VMEM_LIMIT))(mem, w, wkv)


def _memkv_bwd(mem, w, wkv, dkv):
    def body(mem_ref, w_ref, wkv_ref, dkv_ref, dw_ref, dwkv_ref):
        _, vjp = jax.vjp(functools.partial(_memkv_fn, mem_ref[...]), w_ref[...], wkv_ref[...].astype(F32))
        dw, dwkv = vjp(dkv_ref[...])
        dw_ref[...] = dw
        dwkv_ref[...] = dwkv

    return pl.pallas_call(body, name="memkv_bwd", out_shape=[SDS((1, D), F32), SDS((D, 2 * D_XA), F32)],
                          compiler_params=_params(0, VMEM_LIMIT))(mem, w, wkv, dkv)


def _attn_gate(ymix, qx, z, *kvs):
    outs = []
    for j in range(4):
        s = mm_nt(qx[:, j * HD:(j + 1) * HD], kvs[j]) * (HD ** -0.5)
        e = jnp.exp(s - lax.stop_gradient(jnp.max(s, axis=-1, keepdims=True)))
        outs.append(mm(e / jnp.sum(e, axis=-1, keepdims=True), kvs[4 + j]))
    return jnp.concatenate([ymix] + outs, axis=1) * jax.nn.silu(z)


def _kv_blocks(kv_ref):
    return [kv_ref[:, j * HD:(j + 1) * HD] for j in range(8)]


def _ag_fwd(ymix, pg, kv, name):
    T = ymix.shape[0]

    def body(ymix_ref, pg_ref, kv_ref, ycat_ref):
        ycat_ref[...] = _attn_gate(ymix_ref[...], pg_ref[:, :D_XA], pg_ref[:, D_XA:], *_kv_blocks(kv_ref)).astype(BF16)

    return pl.pallas_call(
        body, name=name, grid=(T // TT,),
        in_specs=[_row_spec(D), _row_spec(D_XA + D_CAT), _full_spec((N_MEM, 2 * D_XA))],
        out_specs=_row_spec(D_CAT), out_shape=SDS((T, D_CAT), BF16),
        compiler_params=_params(1, VMEM_LIMIT))(ymix, pg, kv)


def _ag_bwd(dycat, ymix, pg, kv, dkv_in, name):
    T = ymix.shape[0]

    def body(dycat_ref, ymix_ref, pg_ref, kv_ref, dkvin_ref, dymix_ref, dpg_ref, dkv_ref):
        _, vjp = jax.vjp(_attn_gate, ymix_ref[...], pg_ref[:, :D_XA], pg_ref[:, D_XA:], *_kv_blocks(kv_ref))
        g = vjp(dycat_ref[...])
        dymix_ref[...] = g[0]
        dpg_ref[:, :D_XA] = g[1].astype(BF16)
        dpg_ref[:, D_XA:] = g[2].astype(BF16)

        @pl.when(pl.program_id(0) == 0)
        def _():
            dkv_ref[...] = dkvin_ref[...]
        for j in range(8):
            dkv_ref[:, j * HD:(j + 1) * HD] += g[3 + j]

    return pl.pallas_call(
        body, name=name, grid=(T // TT,),
        in_specs=[_row_spec(D_CAT), _row_spec(D), _row_spec(D_XA + D_CAT), _full_spec((N_MEM, 2 * D_XA)),
                  _full_spec((N_MEM, 2 * D_XA))],
        out_specs=[_row_spec(D), _row_spec(D_XA + D_CAT), _full_spec((N_MEM, 2 * D_XA))],
        out_shape=[SDS((T, D), F32), SDS((T, D_XA + D_CAT), BF16), SDS((N_MEM, 2 * D_XA), F32)],
        compiler_params=_params(1, VMEM_LIMIT))(dycat, ymix, pg, kv, dkv_in)


def _outproj_fwd(ycat, wo, x, nw, name):
    T = x.shape[0]

    def body(ycat_ref, wo_ref, x_ref, nw_ref, o_ref, xn_ref):
        o = jnp.dot(ycat_ref[...], wo_ref[...], preferred_element_type=F32)
        o_ref[...] = o
        xn_ref[...] = x_ref[...] + _rms(o, nw_ref[...])

    return pl.pallas_call(
        body, name=name, grid=(T // TT,),
        in_specs=[_row_spec(D_CAT), _full_spec((D_CAT, D)), _row_spec(D), _full_spec((1, D))],
        out_specs=[_row_spec(D), _row_spec(D)], out_shape=[SDS((T, D), F32), SDS((T, D), F32)],
        compiler_params=_params(1, VMEM_LIMIT))(ycat, wo, x, nw)


def _outproj_bwd(dxo, o, nw, wo, name):
    T = dxo.shape[0]

    def body(dxo_ref, o_ref, nw_ref, wo_ref, dycat_ref, dobf_ref, dnw_ref):
        _, vjp = jax.vjp(_rms, o_ref[...], nw_ref[...])
        do, dnw = vjp(dxo_ref[...])
        dobf = do.astype(BF16)
        dobf_ref[...] = dobf
        dycat_ref[...] = _dot_nt(dobf, wo_ref[...])

        @pl.when(pl.program_id(0) == 0)
        def _():
            dnw_ref[...] = jnp.zeros_like(dnw_ref)
        dnw_ref[...] += dnw

    return pl.pallas_call(
        body, name=name, grid=(T // TT,),
        in_specs=[_row_spec(D), _row_spec(D), _full_spec((1, D)), _full_spec((D_CAT, D))],
        out_specs=[_row_spec(D_CAT), _row_spec(D), _full_spec((1, D))],
        out_shape=[SDS((T, D_CAT), F32), SDS((T, D), BF16), SDS((1, D), F32)],
        compiler_params=_params(1, VMEM_LIMIT))(dxo, o, nw, wo)


def _loss_head(xl, target):
    T = xl.shape[0]

    def body(x_ref, t_ref, loss_ref, dx_ref):
        err = x_ref[...] - t_ref[...]
        dx_ref[...] = err * (1.0 / D)

        @pl.when(pl.program_id(0) == 0)
        def _():
            loss_ref[...] = jnp.zeros_like(loss_ref)
        part = jnp.sum(jnp.sum(err * err, axis=1, keepdims=True), axis=0, keepdims=True) * (0.5 / D)
        loss_ref[...] += jnp.broadcast_to(part, loss_ref.shape)

    return pl.pallas_call(
        body, name="loss_head", grid=(T // TT,),
        in_specs=[_row_spec(D), _row_spec(D)],
        out_specs=[_full_spec((8, 128)), _row_spec(D)], out_shape=[SDS((8, 128), F32), SDS((T, D), F32)],
        compiler_params=_params(1))(xl, target)


def _gmlp_pre(u, v, lnw, lnb):
    vg = jax.nn.gelu(v)
    xc = vg - jnp.mean(vg, axis=-1, keepdims=True)
    vl = xc * lax.rsqrt(jnp.mean(xc * xc, axis=-1, keepdims=True) + EPS) * lnw + lnb
    return jax.nn.gelu(u), vl


def _tril(n, strict=False):
    r = lax.broadcasted_iota(jnp.int32, (n, n), 0)
    c = lax.broadcasted_iota(jnp.int32, (n, n), 1)
    return (r > c) if strict else (r >= c)


def _gmlp_fwd(pm, lnw, lnb, ws, bs3, name):
    T = pm.shape[0]

    def body(pm_ref, lnw_ref, lnb_ref, ws_ref, bs_ref, y_ref):
        ug, vl = _gmlp_pre(pm_ref[:, :D], pm_ref[:, D:], lnw_ref[...], lnb_ref[...])
        mask = _tril(HD)
        for g in range(8):
            w = jnp.where(mask, ws_ref[g], 0.0)
            for c in range(TT // HD):
                rows, cols = slice(c * HD, (c + 1) * HD), slice(g * HD, (g + 1) * HD)
                y_ref[rows, cols] = ug[rows, cols] * (_dot_nn(w, vl[rows, cols]) + bs_ref[g])

    return pl.pallas_call(
        body, name=name, grid=(T // TT,),
        in_specs=[_row_spec(2 * D), _full_spec((1, D)), _full_spec((1, D)), _full_spec((8, HD, HD)),
                  _full_spec((8, HD, HD))],
        out_specs=_row_spec(D), out_shape=SDS((T, D), F32),
        compiler_params=_params(1, VMEM_LIMIT))(pm, lnw, lnb, ws, bs3)


def _gmlp_bwd(dy, pm, lnw, lnb, ws, bs3, name):
    T = pm.shape[0]
    n_t = T // TT

    def body(dy_ref, pm_ref, lnw_ref, lnb_ref, ws_ref, bs_ref, dpm_ref, dlnw_ref, dlnb_ref, dws_ref, dbs_ref,
             dug_scr, dvl_scr, dbs_scr):
        i = pl.program_id(0)

        @pl.when(i == 0)
        def _():
            dlnw_ref[...] = jnp.zeros_like(dlnw_ref)
            dlnb_ref[...] = jnp.zeros_like(dlnb_ref)
            dws_ref[...] = jnp.zeros_like(dws_ref)
            dbs_scr[...] = jnp.zeros_like(dbs_scr)

        (ug, vl), vjp = jax.vjp(_gmlp_pre, pm_ref[:, :D], pm_ref[:, D:], lnw_ref[...], lnb_ref[...])
        mask = _tril(HD)
        for g in range(8):
            w = jnp.where(mask, ws_ref[g], 0.0)
            dw = jnp.zeros((HD, HD), F32)
            db = jnp.zeros((HD, HD), F32)
            for c in range(TT // HD):
                rows, cols = slice(c * HD, (c + 1) * HD), slice(g * HD, (g + 1) * HD)
                dyb, vlb = dy_ref[rows, cols], vl[rows, cols]
                sp = _dot_nn(w, vlb) + bs_ref[g]
                dsp = dyb * ug[rows, cols]
                dug_scr[rows, cols] = dyb * sp
                dvl_scr[rows, cols] = _dot_tn(w, dsp)
                dw += _dot_nt(dsp, vlb)
                db += dsp
            dws_ref[g] += jnp.where(mask, dw, 0.0)
            dbs_scr[g] += db
        du, dv, dlnw, dlnb = vjp((dug_scr[...], dvl_scr[...]))
        dpm_ref[:, :D] = du.astype(BF16)
        dpm_ref[:, D:] = dv.astype(BF16)
        dlnw_ref[...] += dlnw
        dlnb_ref[...] += dlnb

        @pl.when(i == n_t - 1)
        def _():
            for g in range(8):
                dbs_ref[g] = jnp.broadcast_to(jnp.sum(dbs_scr[g], axis=1, keepdims=True), (HD, HD))

    return pl.pallas_call(
        body, name=name, grid=(n_t,),
        in_specs=[_row_spec(D), _row_spec(2 * D), _full_spec((1, D)), _full_spec((1, D)), _full_spec((8, HD, HD)),
                  _full_spec((8, HD, HD))],
        out_specs=[_row_spec(2 * D), _full_spec((1, D)), _full_spec((1, D)), _full_spec((8, HD, HD)),
                   _full_spec((8, HD, HD))],
        out_shape=[SDS((T, 2 * D), BF16), SDS((1, D), F32), SDS((1, D), F32), SDS((8, HD, HD), F32),
                   SDS((8, HD, HD), F32)],
        scratch_shapes=[pltpu.VMEM((TT, D), F32), pltpu.VMEM((TT, D), F32), pltpu.VMEM((8, HD, HD), F32)],
        compiler_params=_params(1, VMEM_LIMIT))(dy, pm, lnw, lnb, ws, bs3)


def _prev_spec(width, T):
    return pl.BlockSpec((HALO, width), lambda i: (jnp.maximum(i * (TT // HALO) - 1, 0), 0))


def _next_spec(width, T):
    return pl.BlockSpec((HALO, width), lambda i: (jnp.minimum((i + 1) * (TT // HALO), T // HALO - 1), 0))


def _rows_before(ext, j):
    return ext[HALO:] if j == 0 else pltpu.roll(ext, j, 0)[HALO:]


def _rows_after(ext, j):
    n = ext.shape[0]
    return ext[:n - HALO] if j == 0 else pltpu.roll(ext, n - j, 0)[:n - HALO]


def _conv_apply(ext_s, w):
    K = w.shape[0]
    y = _rows_before(ext_s, K - 1) * w[0:1]
    for k in range(1, K):
        y = y + _rows_before(ext_s, K - 1 - k) * w[k:k + 1]
    return y


def _conv_grads(ext_s, ext_dy, w):
    K = w.shape[0]
    dy = ext_dy[:ext_dy.shape[0] - HALO]
    ds = _rows_after(ext_dy, K - 1) * w[0:1]
    dws = [jnp.sum(dy * _rows_before(ext_s, K - 1), axis=0, keepdims=True)]
    for k in range(1, K):
        ds = ds + _rows_after(ext_dy, K - 1 - k) * w[k:k + 1]
        dws.append(jnp.sum(dy * _rows_before(ext_s, K - 1 - k), axis=0, keepdims=True))
    return ds, jnp.concatenate(dws, axis=0)


def _sconv_fwd(pm, w, name):
    T = pm.shape[0]

    def body(pm_ref, prev_ref, w_ref, y_ref):
        s = pm_ref[:, D:2 * D] * pm_ref[:, 2 * D:]
        sp = jnp.where(pl.program_id(0) > 0, prev_ref[:, D:2 * D] * prev_ref[:, 2 * D:], 0.0)
        y_ref[...] = pm_ref[:, :D] * _conv_apply(jnp.concatenate([sp, s], axis=0), w_ref[...])

    return pl.pallas_call(
        body, name=name, grid=(T // TT,),
        in_specs=[_row_spec(3 * D), _prev_spec(3 * D, T), _full_spec((3, D))],
        out_specs=_row_spec(D), out_shape=SDS((T, D), F32),
        compiler_params=_params(1, VMEM_LIMIT))(pm, pm, w)


def _sconv_bwd(dy, pm, w, name):
    T = pm.shape[0]
    n_t = T // TT

    def body(dy_ref, dyn_ref, pm_ref, prev_ref, next_ref, w_ref, dpm_ref, dw_ref):
        i = pl.program_id(0)
        bg, cg, hv = pm_ref[:, :D], pm_ref[:, D:2 * D], pm_ref[:, 2 * D:]
        sp = jnp.where(i > 0, prev_ref[:, D:2 * D] * prev_ref[:, 2 * D:], 0.0)
        ext_s = jnp.concatenate([sp, cg * hv], axis=0)
        dyv = dy_ref[...]
        dcn = jnp.where(i < n_t - 1, dyn_ref[...] * next_ref[:, :D], 0.0)
        ds, dw = _conv_grads(ext_s, jnp.concatenate([dyv * bg, dcn], axis=0), w_ref[...])
        dpm_ref[:, :D] = (dyv * _conv_apply(ext_s, w_ref[...])).astype(BF16)
        dpm_ref[:, D:2 * D] = (ds * hv).astype(BF16)
        dpm_ref[:, 2 * D:] = (ds * cg).astype(BF16)

        @pl.when(i == 0)
        def _():
            dw_ref[...] = jnp.zeros_like(dw_ref)
        dw_ref[...] += dw

    return pl.pallas_call(
        body, name=name, grid=(n_t,),
        in_specs=[_row_spec(D), _next_spec(D, T), _row_spec(3 * D), _prev_spec(3 * D, T), _next_spec(3 * D, T),
                  _full_spec((3, D))],
        out_specs=[_row_spec(3 * D), _full_spec((3, D))],
        out_shape=[SDS((T, 3 * D), BF16), SDS((3, D), F32)],
        compiler_params=_params(1, VMEM_LIMIT))(dy, dy, pm, pm, pm, w)


def _dnconv_fwd(pm, w, name):
    T = pm.shape[0]

    def body(pm_ref, prev_ref, w_ref, c_ref):
        sp = jnp.where(pl.program_id(0) > 0, prev_ref[...], 0.0)
        c_ref[...] = _conv_apply(jnp.concatenate([sp, pm_ref[...]], axis=0), w_ref[...])

    return pl.pallas_call(
        body, name=name, grid=(T // TT,),
        in_specs=[_row_spec(3 * D), _prev_spec(3 * D, T), _full_spec((4, 3 * D))],
        out_specs=_row_spec(3 * D), out_shape=SDS((T, 3 * D), F32),
        compiler_params=_params(1, VMEM_LIMIT))(pm, pm, w)


def _dnconv_bwd(dcq, dck, dcv, dab, pm, w, name):
    T = pm.shape[0]
    n_t = T // TT

    def body(dq_ref, dk_ref, dv_ref, dqn_ref, dkn_ref, dvn_ref, dab_ref, pm_ref, prev_ref, w_ref, dpm_ref, dw_ref):
        i = pl.program_id(0)
        sp = jnp.where(i > 0, prev_ref[...], 0.0)
        ext_s = jnp.concatenate([sp, pm_ref[...]], axis=0)
        own = jnp.concatenate([dq_ref[...], dk_ref[...], dv_ref[...]], axis=1)
        nxt = jnp.where(i < n_t - 1, jnp.concatenate([dqn_ref[...], dkn_ref[...], dvn_ref[...]], axis=1), 0.0)
        ds, dw = _conv_grads(ext_s, jnp.concatenate([own, nxt], axis=0), w_ref[...])
        dpm_ref[:, :3 * D] = ds.astype(BF16)
        dpm_ref[:, 3 * D:] = dab_ref[...].astype(BF16)

        @pl.when(i == 0)
        def _():
            dw_ref[...] = jnp.zeros_like(dw_ref)
        dw_ref[...] += dw

    return pl.pallas_call(
        body, name=name, grid=(n_t,),
        in_specs=[_row_spec(D), _row_spec(D), _row_spec(D), _next_spec(D, T), _next_spec(D, T), _next_spec(D, T),
                  _row_spec(AB_PAD), _row_spec(3 * D), _prev_spec(3 * D, T), _full_spec((4, 3 * D))],
        out_specs=[_row_spec(3 * D + AB_PAD), _full_spec((4, 3 * D))],
        out_shape=[SDS((T, 3 * D + AB_PAD), BF16), SDS((4, 3 * D), F32)],
        compiler_params=_params(1, VMEM_LIMIT))(dcq, dck, dcv, dcq, dck, dcv, dab, pm, pm, w)


def _l2n(x):
    return x * lax.rsqrt(jnp.sum(x * x, axis=-1, keepdims=True) + EPS)


def _softplus(x):
    return jnp.maximum(x, 0.0) + jnp.log1p(jnp.exp(-jnp.abs(x)))


def _dn_gates(ab, alog, dtb, h):
    lane = lax.broadcasted_iota(jnp.int32, ab.shape, 1)
    g_all = -jnp.exp(alog) * _softplus(ab + dtb)
    g = jnp.sum(jnp.where(lane == h, g_all, 0.0), axis=1, keepdims=True)
    beta = jnp.sum(jnp.where(lane == 8 + h, jax.nn.sigmoid(ab), 0.0), axis=1, keepdims=True)
    ones = jnp.ones((1, HD), F32)
    return g * ones, beta * ones


def _dn_chunk(cq, ck, cv, gb, bb, S, onw):
    C = DN_C
    q = _l2n(jax.nn.silu(cq)) * (HD ** -0.5)
    k = _l2n(jax.nn.silu(ck))
    v = jax.nn.silu(cv)
    incl, strict = _tril(C), _tril(C, strict=True)
    gcum = _dot_hi(incl.astype(F32), gb)
    gi = gcum[:, :C]
    gj = gcum.T[:C, :]
    decay = jnp.where(incl, jnp.exp(jnp.where(incl, gi - gj, 0.0)), 0.0)
    kb = k * bb
    a_mat = jnp.where(strict, mm_nt(kb, k) * decay, 0.0)
    p = -a_mat
    eye = (lax.broadcasted_iota(jnp.int32, (C, C), 0) == lax.broadcasted_iota(jnp.int32, (C, C), 1)).astype(F32)
    t_mat = eye + p
    for _ in range(5):
        p = _dot_hi(p, p)
        t_mat = t_mat + _dot_hi(t_mat, p)
    eg = jnp.exp(gcum)
    u = mm(t_mat, v * bb)
    w = mm(t_mat, kb * eg)
    qk = mm_nt(q, k) * decay
    glast = gcum[C - 1:C, :]
    v_new = u - mm(w, S)
    o = mm(q * eg, S) + mm(qk, v_new)
    s_new = S * jnp.exp(glast) + mm_tn(k * jnp.exp(glast - gcum), v_new)
    return _rms(o, onw), s_new


def _dn_specs(T, rev):
    nb = T // DN_TB
    blk = (lambda n: nb - 1 - n) if rev else (lambda n: n)
    head = [pl.BlockSpec((DN_TB, HD), functools.partial(lambda n, h, off: (blk(n), off + h), off=8 * s)) for s in range(3)]
    ab = pl.BlockSpec((DN_TB, AB_PAD), lambda n, h: (blk(n), 3 * D // AB_PAD))
    st = pl.BlockSpec((DN_TB // DN_C, None, HD, HD), lambda n, h: (blk(n), h, 0, 0))
    out = pl.BlockSpec((DN_TB, HD), lambda n, h: (blk(n), h))
    row = pl.BlockSpec((1, HD), lambda n, h: (0, 0))
    return nb, head, ab, st, out, row


def _dn_fwd(cpre, pm, alog, dtb, onw, name):
    T = cpre.shape[0]
    nb, head, ab, st, out, row = _dn_specs(T, False)

    def body(cq_ref, ck_ref, cv_ref, ab_ref, alog_ref, dtb_ref, onw_ref, o_ref, st_ref, s_scr):
        n, h = pl.program_id(0), pl.program_id(1)

        @pl.when(n == 0)
        def _():
            s_scr[h] = jnp.zeros((HD, HD), F32)
        gb, bb = _dn_gates(ab_ref[...], alog_ref[...], dtb_ref[...], h)
        S = s_scr[h]
        for c in range(DN_TB // DN_C):
            rows = slice(c * DN_C, (c + 1) * DN_C)
            st_ref[c] = S
            o, S = _dn_chunk(cq_ref[rows, :], ck_ref[rows, :], cv_ref[rows, :], gb[rows], bb[rows], S, onw_ref[...])
            o_ref[rows, :] = o
        s_scr[h] = S

    return pl.pallas_call(
        body, name=name, grid=(nb, 8),
        in_specs=head + [ab, row, row, row], out_specs=[out, st],
        out_shape=[SDS((T, D), F32), SDS((T // DN_C, 8, HD, HD), F32)],
        scratch_shapes=[pltpu.VMEM((8, HD, HD), F32)],
        compiler_params=_params(2, VMEM_LIMIT))(cpre, cpre, cpre, pm, alog, dtb, onw)


def _dn_bwd(do, cpre, pm, st, alog, dtb, onw, name):
    T = cpre.shape[0]
    nb, head, ab, stspec, out, row = _dn_specs(T, True)

    def body(do_ref, cq_ref, ck_ref, cv_ref, ab_ref, st_ref, alog_ref, dtb_ref, onw_ref,
             dcq_ref, dck_ref, dcv_ref, dab_ref, dalog_ref, ddtb_ref, donw_ref, ds_scr):
        n, h = pl.program_id(0), pl.program_id(1)

        @pl.when(n == 0)
        def _():
            ds_scr[h] = jnp.zeros((HD, HD), F32)

        @pl.when((n == 0) & (h == 0))
        def _():
            dalog_ref[...] = jnp.zeros_like(dalog_ref)
            ddtb_ref[...] = jnp.zeros_like(ddtb_ref)
            donw_ref[...] = jnp.zeros_like(donw_ref)

        @pl.when(h == 0)
        def _():
            dab_ref[...] = jnp.zeros_like(dab_ref)

        (gb, bb), gates_vjp = jax.vjp(lambda a, b, c: _dn_gates(a, b, c, h), ab_ref[...], alog_ref[...], dtb_ref[...])
        dS = ds_scr[h]
        n_c = DN_TB // DN_C
        dgs, dbs = [None] * n_c, [None] * n_c
        donw = jnp.zeros((1, HD), F32)
        for c in reversed(range(n_c)):
            rows = slice(c * DN_C, (c + 1) * DN_C)
            _, vjp = jax.vjp(_dn_chunk, cq_ref[rows, :], ck_ref[rows, :], cv_ref[rows, :], gb[rows], bb[rows],
                             st_ref[c], onw_ref[...])
            dcq, dck, dcv, dgs[c], dbs[c], dS, dn = vjp((do_ref[rows, :], dS))
            dcq_ref[rows, :] = dcq
            dck_ref[rows, :] = dck
            dcv_ref[rows, :] = dcv
            donw += dn
        ds_scr[h] = dS
        dab, dalog, ddtb = gates_vjp((jnp.concatenate(dgs, axis=0), jnp.concatenate(dbs, axis=0)))
        dab_ref[...] += dab
        dalog_ref[...] += dalog
        ddtb_ref[...] += ddtb
        donw_ref[...] += donw

    dabspec = pl.BlockSpec((DN_TB, AB_PAD), lambda n, h: (nb - 1 - n, 0))
    return pl.pallas_call(
        body, name=name, grid=(nb, 8),
        in_specs=[out] + head + [ab, stspec, row, row, row],
        out_specs=[out, out, out, dabspec, row, row, row],
        out_shape=[SDS((T, D), F32)] * 3 + [SDS((T, AB_PAD), F32)] + [SDS((1, HD), F32)] * 3,
        scratch_shapes=[pltpu.VMEM((8, HD, HD), F32)],
        compiler_params=_params(2, VMEM_LIMIT))(do, cpre, cpre, cpre, pm, st, alog, dtb, onw)


_BNN = (((2,), (1,)), ((0,), (0,)))
_BNT = (((2,), (2,)), ((0,), (0,)))
_BTN = (((1,), (1,)), ((0,), (0,)))


def _bdot(a, b, dims):
    return lax.dot_general(a.astype(BF16), b.astype(BF16), dims, preferred_element_type=F32)


def _bdot3(a, b, dims):
    ah, bh = a.astype(BF16), b.astype(BF16)
    al, bl = (a - ah.astype(F32)).astype(BF16), (b - bh.astype(F32)).astype(BF16)
    d = functools.partial(lax.dot_general, dimension_numbers=dims, preferred_element_type=F32)
    return d(ah, bh) + (d(ah, bl) + d(al, bh))


def _bdot_hi(a, b, dims):
    return lax.dot_general(a, b, dims, precision=HI, preferred_element_type=F32)


def _batched_matmuls(dot):
    @jax.custom_vjp
    def nn(a, b):
        return dot(a, b, _BNN)

    @jax.custom_vjp
    def nt(a, b):
        return dot(a, b, _BNT)

    @jax.custom_vjp
    def tn(a, b):
        return dot(a, b, _BTN)

    nn.defvjp(lambda a, b: (dot(a, b, _BNN), (a, b)), lambda r, g: (dot(g, r[1], _BNT), dot(r[0], g, _BTN)))
    nt.defvjp(lambda a, b: (dot(a, b, _BNT), (a, b)), lambda r, g: (dot(g, r[1], _BNN), dot(g, r[0], _BTN)))
    tn.defvjp(lambda a, b: (dot(a, b, _BTN), (a, b)), lambda r, g: (dot(r[1], g, _BNT), dot(r[0], g, _BNN)))
    return nn, nt, tn


bmm, bmm_nt, bmm_tn = _batched_matmuls(_bdot)
bmm3, _, _ = _batched_matmuls(_bdot3)
bmm_hi, bmm_hi_nt, _ = _batched_matmuls(_bdot_hi)

@jax.custom_vjp
def _neumann_inverse(n):
    C = n.shape[1]
    eye = lax.broadcasted_iota(jnp.int32, n.shape, 1) == lax.broadcasted_iota(jnp.int32, n.shape, 2)
    t = eye.astype(F32) + n
    for _ in range(5):
        n = _bdot3(n, n, _BNN)
        t = t + _bdot3(t, n, _BNN)
    return t


def _neumann_inverse_fwd(n):
    t = _neumann_inverse(n)
    return t, t


def _neumann_inverse_bwd(t, g):
    return (_bdot3(_bdot3(t, g, _BTN), t, _BNT),)


_neumann_inverse.defvjp(_neumann_inverse_fwd, _neumann_inverse_bwd)

DN_NCH = DN_TB // DN_C
DN_NH = 4


def _dn_prep(cq, ck, cv, gb, bb):
    B, C = cq.shape[0], DN_C
    q = _l2n(jax.nn.silu(cq)) * (HD ** -0.5)
    k = _l2n(jax.nn.silu(ck))
    v = jax.nn.silu(cv)
    r = lax.broadcasted_iota(jnp.int32, (B, C, C), 1)
    c = lax.broadcasted_iota(jnp.int32, (B, C, C), 2)
    incl, strict = r >= c, r > c
    gcum = bmm_hi(incl.astype(F32), gb)
    lane0 = lax.broadcasted_iota(jnp.int32, (B, C, HD), 2) == 0
    gj = bmm_hi_nt(jnp.ones((B, C, HD), F32), jnp.where(lane0, gcum, 0.0))
    decay = jnp.where(incl, jnp.exp(jnp.where(incl, gcum[:, :, :C] - gj, 0.0)), 0.0)
    kb = k * bb
    t_mat = _neumann_inverse(-jnp.where(strict, bmm_nt(kb, k) * decay, 0.0))
    eg = jnp.exp(gcum)
    glast = gcum[:, C - 1:C, :]
    return (bmm(t_mat, v * bb), bmm(t_mat, kb * eg), bmm_nt(q, k) * decay, q * eg, k * jnp.exp(glast - gcum),
            jnp.exp(glast))


def _dn_scan_step(u, w, qk, qd, kd, egl, S, onw):
    v_new = u - bmm(w, S)
    o = bmm(qd, S) + bmm(qk, v_new)
    return _rms(o, onw), S * egl + bmm_tn(kd, v_new)


def _head_gates(ab, alog, dtb, first_head, n_heads):
    gs, bs = [], []
    for i in range(n_heads):
        g, b = _dn_gates(ab, alog, dtb, first_head + i)
        gs.append(g.reshape(DN_NCH, DN_C, HD))
        bs.append(b.reshape(DN_NCH, DN_C, HD))
    return jnp.concatenate(gs, axis=0), jnp.concatenate(bs, axis=0)


def _to_batch(ref, n_heads):
    return jnp.concatenate([ref[:, i * HD:(i + 1) * HD].astype(F32).reshape(DN_NCH, DN_C, HD) for i in range(n_heads)],
                           axis=0)


def _from_batch(ref, val, n_heads):
    for i in range(n_heads):
        ref[:, i * HD:(i + 1) * HD] = val[i * DN_NCH:(i + 1) * DN_NCH].reshape(DN_TB, HD).astype(ref.dtype)


def _prep_specs(T, rev):
    nb = T // DN_TB
    blk = (lambda n: nb - 1 - n) if rev else (lambda n: n)
    ng = 8 // DN_NH
    head = [pl.BlockSpec((DN_TB, DN_NH * HD), functools.partial(lambda n, h, off: (blk(n), off + h), off=ng * s))
            for s in range(3)]
    ab = pl.BlockSpec((DN_TB, AB_PAD), lambda n, h: (blk(n), 3 * D // AB_PAD))
    row = pl.BlockSpec((1, HD), lambda n, h: (0, 0))
    wide = pl.BlockSpec((DN_TB, DN_NH * HD), lambda n, h: (blk(n), h))
    qk = pl.BlockSpec((DN_NCH, DN_NH, DN_C, DN_C), lambda n, h: (blk(n), h, 0, 0))
    eg = pl.BlockSpec((DN_NCH, DN_NH, 1, HD), lambda n, h: (blk(n), h, 0, 0))
    return nb, ng, head, ab, row, wide, qk, eg


def _dn_prep_fwd(cpre, pm, alog, dtb, name):
    T = cpre.shape[0]
    nb, ng, head, ab, row, wide, qks, egs = _prep_specs(T, False)

    def body(cq_ref, ck_ref, cv_ref, ab_ref, alog_ref, dtb_ref, u_ref, w_ref, qk_ref, qd_ref, kd_ref, e_ref):
        gb, bb = _head_gates(ab_ref[...], alog_ref[...], dtb_ref[...], pl.program_id(1) * DN_NH, DN_NH)
        u, w, qk, qd, kd, egl = _dn_prep(_to_batch(cq_ref, DN_NH), _to_batch(ck_ref, DN_NH), _to_batch(cv_ref, DN_NH),
                                         gb, bb)
        _from_batch(u_ref, u, DN_NH)
        _from_batch(w_ref, w, DN_NH)
        _from_batch(qd_ref, qd, DN_NH)
        _from_batch(kd_ref, kd, DN_NH)
        for i in range(DN_NH):
            qk_ref[:, i] = qk[i * DN_NCH:(i + 1) * DN_NCH].astype(BF16)
            e_ref[:, i] = egl[i * DN_NCH:(i + 1) * DN_NCH]

    return pl.pallas_call(
        body, name=name, grid=(nb, ng), in_specs=head + [ab, row, row],
        out_specs=[wide, wide, qks, wide, wide, egs],
        out_shape=[SDS((T, D), F32), SDS((T, D), BF16), SDS((T // DN_C, 8, DN_C, DN_C), BF16), SDS((T, D), BF16),
                   SDS((T, D), BF16), SDS((T // DN_C, 8, 1, HD), F32)],
        compiler_params=_params(2, VMEM_LIMIT))(cpre, cpre, cpre, pm, alog, dtb)


def _dn_prep_bwd(du, dw, dqk, dqd, dkd, degl, cpre, pm, alog, dtb, name):
    T = cpre.shape[0]
    nb, ng, head, ab, row, wide, qks, egs = _prep_specs(T, True)

    def body(du_ref, dw_ref, dqk_ref, dqd_ref, dkd_ref, de_ref, cq_ref, ck_ref, cv_ref, ab_ref, alog_ref, dtb_ref,
             dcq_ref, dck_ref, dcv_ref, dab_ref, dalog_ref, ddtb_ref):
        n, h = pl.program_id(0), pl.program_id(1)

        @pl.when((n == 0) & (h == 0))
        def _():
            dalog_ref[...] = jnp.zeros_like(dalog_ref)
            ddtb_ref[...] = jnp.zeros_like(ddtb_ref)

        @pl.when(h == 0)
        def _():
            dab_ref[...] = jnp.zeros_like(dab_ref)

        def fwd(cq, ck, cv, ab_v, alog_v, dtb_v):
            gb, bb = _head_gates(ab_v, alog_v, dtb_v, h * DN_NH, DN_NH)
            return _dn_prep(cq, ck, cv, gb, bb)

        _, vjp = jax.vjp(fwd, _to_batch(cq_ref, DN_NH), _to_batch(ck_ref, DN_NH), _to_batch(cv_ref, DN_NH), ab_ref[...],
                         alog_ref[...], dtb_ref[...])
        cot = (_to_batch(du_ref, DN_NH), _to_batch(dw_ref, DN_NH),
               jnp.concatenate([dqk_ref[:, i] for i in range(DN_NH)], axis=0), _to_batch(dqd_ref, DN_NH),
               _to_batch(dkd_ref, DN_NH), jnp.concatenate([de_ref[:, i] for i in range(DN_NH)], axis=0))
        dcq, dck, dcv, dab, dalog, ddtb = vjp(cot)
        _from_batch(dcq_ref, dcq, DN_NH)
        _from_batch(dck_ref, dck, DN_NH)
        _from_batch(dcv_ref, dcv, DN_NH)
        dab_ref[...] += dab
        dalog_ref[...] += dalog
        ddtb_ref[...] += ddtb

    dabspec = pl.BlockSpec((DN_TB, AB_PAD), lambda n, h: (nb - 1 - n, 0))
    return pl.pallas_call(
        body, name=name, grid=(nb, ng),
        in_specs=[wide, wide, qks, wide, wide, egs] + head + [ab, row, row],
        out_specs=[wide, wide, wide, dabspec, row, row],
        out_shape=[SDS((T, D), F32)] * 3 + [SDS((T, AB_PAD), F32)] + [SDS((1, HD), F32)] * 2,
        compiler_params=_params(2, VMEM_LIMIT))(du, dw, dqk, dqd, dkd, degl, cpre, cpre, cpre, pm, alog, dtb)


def _scan_specs(T, rev):
    nb = T // DN_TB
    blk = (lambda n: nb - 1 - n) if rev else (lambda n: n)
    wide = pl.BlockSpec((DN_TB, D), lambda n: (blk(n), 0))
    qk = pl.BlockSpec((DN_NCH, 8, DN_C, DN_C), lambda n: (blk(n), 0, 0, 0))
    eg = pl.BlockSpec((DN_NCH, 8, 1, HD), lambda n: (blk(n), 0, 0, 0))
    st = pl.BlockSpec((DN_NCH, 8, HD, HD), lambda n: (blk(n), 0, 0, 0))
    row = pl.BlockSpec((1, HD), lambda n: (0, 0))
    return nb, wide, qk, eg, st, row


def _heads_of(ref, rows):
    return jnp.concatenate([ref[rows, h * HD:(h + 1) * HD].astype(F32)[None] for h in range(8)], axis=0)


def _dn_scan_fwd(u, w, qk, qd, kd, egl, onw, name):
    T = u.shape[0]
    nb, wide, qks, egs, sts, row = _scan_specs(T, False)

    def body(u_ref, w_ref, qk_ref, qd_ref, kd_ref, e_ref, onw_ref, o_ref, st_ref, s_scr):
        @pl.when(pl.program_id(0) == 0)
        def _():
            s_scr[...] = jnp.zeros_like(s_scr)
        S = s_scr[...]
        for c in range(DN_NCH):
            rows = slice(c * DN_C, (c + 1) * DN_C)
            st_ref[c] = S
            o, S = _dn_scan_step(_heads_of(u_ref, rows), _heads_of(w_ref, rows), qk_ref[c].astype(F32),
                                 _heads_of(qd_ref, rows), _heads_of(kd_ref, rows), e_ref[c], S, onw_ref[...])
            for h in range(8):
                o_ref[rows, h * HD:(h + 1) * HD] = o[h]
        s_scr[...] = S

    return pl.pallas_call(
        body, name=name, grid=(nb,), in_specs=[wide, wide, qks, wide, wide, egs, row], out_specs=[wide, sts],
        out_shape=[SDS((T, D), F32), SDS((T // DN_C, 8, HD, HD), F32)],
        scratch_shapes=[pltpu.VMEM((8, HD, HD), F32)],
        compiler_params=_params(1, VMEM_LIMIT))(u, w, qk, qd, kd, egl, onw)


def _dn_scan_bwd(do, u, w, qk, qd, kd, egl, st, onw, name):
    T = u.shape[0]
    nb, wide, qks, egs, sts, row = _scan_specs(T, True)

    def body(do_ref, u_ref, w_ref, qk_ref, qd_ref, kd_ref, e_ref, st_ref, onw_ref,
             du_ref, dw_ref, dqk_ref, dqd_ref, dkd_ref, de_ref, donw_ref, ds_scr):
        @pl.when(pl.program_id(0) == 0)
        def _():
            ds_scr[...] = jnp.zeros_like(ds_scr)
            donw_ref[...] = jnp.zeros_like(donw_ref)
        dS = ds_scr[...]
        donw = jnp.zeros((1, HD), F32)
        for c in reversed(range(DN_NCH)):
            rows = slice(c * DN_C, (c + 1) * DN_C)
            _, vjp = jax.vjp(_dn_scan_step, _heads_of(u_ref, rows), _heads_of(w_ref, rows), qk_ref[c].astype(F32),
                             _heads_of(qd_ref, rows), _heads_of(kd_ref, rows), e_ref[c], st_ref[c], onw_ref[...])
            du, dw, dqk, dqd, dkd, de, dS, dn = vjp((_heads_of(do_ref, rows), dS))
            for h in range(8):
                cols = slice(h * HD, (h + 1) * HD)
                du_ref[rows, cols] = du[h]
                dw_ref[rows, cols] = dw[h]
                dqd_ref[rows, cols] = dqd[h]
                dkd_ref[rows, cols] = dkd[h]
            dqk_ref[c] = dqk
            de_ref[c] = de
            donw += dn
        ds_scr[...] = dS
        donw_ref[...] += donw

    return pl.pallas_call(
        body, name=name, grid=(nb,), in_specs=[wide, wide, wide, qks, wide, wide, egs, sts, row],
        out_specs=[wide, wide, qks, wide, wide, egs, row],
        out_shape=[SDS((T, D), F32), SDS((T, D), F32), SDS((T // DN_C, 8, DN_C, DN_C), F32), SDS((T, D), F32),
                   SDS((T, D), F32), SDS((T // DN_C, 8, 1, HD), F32), SDS((1, HD), F32)],
        scratch_shapes=[pltpu.VMEM((8, HD, HD), F32)],
        compiler_params=_params(1, VMEM_LIMIT))(do, u, w, qk, qd, kd, egl, st, onw)


def _adamw(w, g, m, v, name):
    R, C = w.shape
    tr = 256 if R % 256 == 0 and R > 256 else R
    c1 = 1.0 - ADAM_B1 ** ADAM_STEP
    c2 = 1.0 - ADAM_B2 ** ADAM_STEP

    def body(w_ref, g_ref, m_ref, v_ref, d_ref, nm_ref, nv_ref):
        gv = g_ref[...]
        nm = ADAM_B1 * m_ref[...] + (1.0 - ADAM_B1) * gv
        nv = ADAM_B2 * v_ref[...] + (1.0 - ADAM_B2) * (gv * gv)
        nm_ref[...] = nm
        nv_ref[...] = nv
        d_ref[...] = -ADAM_LR * ((nm / c1) / (jnp.sqrt(nv / c2) + ADAM_EPS) + ADAM_WD * w_ref[...])

    spec = pl.BlockSpec((tr, C), lambda i: (i, 0))
    return pl.pallas_call(
        body, name=name, grid=(R // tr,), in_specs=[spec] * 4, out_specs=[spec] * 3,
        out_shape=[SDS((R, C), F32)] * 3, compiler_params=_params(1, VMEM_LIMIT))(w, g, m, v)


def _local_step(x, mem, target, wts, sm):
    kinds = [i % 3 for i in range(DEPTH)]
    mnw = sm["mem_norm_w"].reshape(1, D)
    kv = _memkv_fwd(mem, mnw, wts["wkv"])
    saved = []
    for i, kind in enumerate(kinds):
        j = i // 3
        npre = sm["norm_pre"][i].reshape(1, D)
        npost = sm["norm_post"][i].reshape(1, D)
        pm, pg, h = _inproj_fwd(x, npre, wts["mix"][i], wts["gate"][i], f"inproj_fwd_{i}")
        extra = None
        if kind == 0:
            bs3 = jnp.broadcast_to(sm["a_b_s"][j][:, :, None], (8, HD, HD))
            ymix = _gmlp_fwd(pm, sm["a_ln_w"][j].reshape(1, D), sm["a_ln_b"][j].reshape(1, D), sm["a_w_s"][j], bs3,
                             f"gmlp_fwd_{i}")
            extra = bs3
        elif kind == 1:
            ymix = _sconv_fwd(pm, sm["b_conv_w"][j], f"sconv_fwd_{i}")
        else:
            cpre = _dnconv_fwd(pm, sm["c_conv_w"][j], f"dnconv_fwd_{i}")
            alog = jnp.pad(sm["c_a_log"][j], (0, HD - 8)).reshape(1, HD)
            dtb = jnp.pad(sm["c_dt_bias"][j], (0, HD - 8)).reshape(1, HD)
            onw = sm["c_o_norm_w"][j].reshape(1, HD)
            prep = _dn_prep_fwd(cpre, pm, alog, dtb, f"dn_prep_fwd_{i}")
            ymix, st = _dn_scan_fwd(*prep, onw, f"dn_scan_fwd_{i}")
            extra = (cpre, prep, st, alog, dtb, onw)
        ycat = _ag_fwd(ymix, pg, kv, f"ag_fwd_{i}")
        o, xn = _outproj_fwd(ycat, wts["wo"][i], x, npost, f"outproj_fwd_{i}")
        saved.append((x, h, pm, pg, ymix, ycat, o, extra))
        x = xn

    loss, dx = _loss_head(x, target)

    g = {"wm": [None] * DEPTH, "wg": [None] * DEPTH, "wo": [None] * DEPTH, "norm_pre": [None] * DEPTH,
         "norm_post": [None] * DEPTH}
    dkv = jnp.zeros((N_MEM, 2 * D_XA), F32)
    for i in reversed(range(DEPTH)):
        kind, j = kinds[i], i // 3
        xi, h, pm, pg, ymix, ycat, o, extra = saved[i]
        npre = sm["norm_pre"][i].reshape(1, D)
        npost = sm["norm_post"][i].reshape(1, D)
        dycat, dobf, g["norm_post"][i] = _outproj_bwd(dx, o, npost, wts["wo"][i], f"outproj_bwd_{i}")
        g["wo"][i] = _matmul_tn(ycat, dobf, f"dwo_{i}")
        dymix, dpg, dkv = _ag_bwd(dycat, ymix, pg, kv, dkv, f"ag_bwd_{i}")
        if kind == 0:
            dpm, dlnw, dlnb, dws, dbs3 = _gmlp_bwd(dymix, pm, sm["a_ln_w"][j].reshape(1, D),
                                                   sm["a_ln_b"][j].reshape(1, D), sm["a_w_s"][j], extra,
                                                   f"gmlp_bwd_{i}")
            g.setdefault("a_ln_w", {})[j] = dlnw.reshape(D)
            g.setdefault("a_ln_b", {})[j] = dlnb.reshape(D)
            g.setdefault("a_w_s", {})[j] = dws
            g.setdefault("a_b_s", {})[j] = dbs3[:, :, 0]
        elif kind == 1:
            dpm, dcw = _sconv_bwd(dymix, pm, sm["b_conv_w"][j], f"sconv_bwd_{i}")
            g.setdefault("b_conv_w", {})[j] = dcw
        else:
            cpre, prep, st, alog, dtb, onw = extra
            *dprep, donw = _dn_scan_bwd(dymix, *prep, st, onw, f"dn_scan_bwd_{i}")
            dcq, dck, dcv, dab, dalog, ddtb = _dn_prep_bwd(*dprep, cpre, pm, alog, dtb, f"dn_prep_bwd_{i}")
            dpm, dcw = _dnconv_bwd(dcq, dck, dcv, dab, pm, sm["c_conv_w"][j], f"dnconv_bwd_{i}")
            g.setdefault("c_conv_w", {})[j] = dcw
            g.setdefault("c_a_log", {})[j] = dalog[0, :8]
            g.setdefault("c_dt_bias", {})[j] = ddtb[0, :8]
            g.setdefault("c_o_norm_w", {})[j] = donw[0]
        sub = GRAD_TILE[kind]
        g["wm"][i] = _matmul_tn(h, dpm, f"dwm_{i}", sub)
        g["wg"][i] = _matmul_tn(h, dpg, f"dwg_{i}", sub)
        dx, g["norm_pre"][i] = _inproj_bwd(dpm, dpg, xi, npre, wts["mix"][i], wts["gate"][i], dx, f"inproj_bwd_{i}")
    g["mem_norm_w"], g["wkv"] = _memkv_bwd(mem, mnw, wts["wkv"], dkv)
    return loss[0, 0], dx, g


ANY = pl.BlockSpec(memory_space=pl.ANY)


def _place():
    return lax.axis_index("x"), lax.axis_index("y"), lax.axis_index("c")


def _gather_weights(big, vec):
    def body(big_ref, vec_ref, ob_ref, ov_ref, ici_send, ici_recv, d2d_send, d2d_recv, vec_send, vec_recv):
        x, y, c = _place()
        chip = 2 * x + y
        peers = [(1 - x, y), (x, 1 - y), (1 - x, 1 - y)]

        def rows(half, k):
            return pl.ds(half * HALF_ROWS + k * CHUNK_ROWS, CHUNK_ROWS)

        def over_ici(j, k, slab):
            px, py = peers[j]
            i = j * N_CHUNKS + k
            return pltpu.make_async_remote_copy(
                src_ref=big_ref.at[rows(c, k)], dst_ref=ob_ref.at[slab, rows(c, k)], send_sem=ici_send.at[i],
                recv_sem=ici_recv.at[i], device_id=(px, py, c), device_id_type=MESH)

        def over_d2d(j, k, half):
            px, py = peers[j]
            i = j * N_CHUNKS + k
            where = ob_ref.at[2 * px + py, rows(half, k)]
            return pltpu.make_async_remote_copy(
                src_ref=where, dst_ref=where, send_sem=d2d_send.at[i], recv_sem=d2d_recv.at[i],
                device_id=(x, y, 1 - c), device_id_type=MESH)

        def small(j, slab):
            px, py = peers[j]
            return pltpu.make_async_remote_copy(
                src_ref=vec_ref, dst_ref=ov_ref.at[slab], send_sem=vec_send.at[j], recv_sem=vec_recv.at[j],
                device_id=(px, py, c), device_id_type=MESH)

        sends = [small(j, chip) for j in range(3)] + [over_ici(j, k, chip) for k in range(N_CHUNKS) for j in range(3)]
        for cp in sends:
            cp.start()
        forwards = []
        for k in range(N_CHUNKS):
            for j, (px, py) in enumerate(peers):
                over_ici(j, k, 2 * px + py).wait_recv()
                forwards.append(over_d2d(j, k, c))
                forwards[-1].start()
        for k in range(N_CHUNKS):
            for j in range(3):
                over_d2d(j, k, 1 - c).wait_recv()
        for j, (px, py) in enumerate(peers):
            small(j, 2 * px + py).wait_recv()
        for cp in sends + forwards:
            cp.wait_send()

    n = 3 * N_CHUNKS
    dma = pltpu.SemaphoreType.DMA
    return pl.pallas_call(
        body, name="gather_weights", in_specs=[ANY, ANY], out_specs=[ANY, ANY],
        out_shape=[SDS((4,) + big.shape, big.dtype), SDS((4,) + vec.shape, vec.dtype)],
        scratch_shapes=[dma((n,)), dma((n,)), dma((n,)), dma((n,)), dma((3,)), dma((3,))])(big, vec)


HALF_ROWS = 3328
CHUNK_ROWS = 256
N_CHUNKS = HALF_ROWS // CHUNK_ROWS


def _swap_halves(gbig, small):
    def body(g_ref, s_ref, ob_ref, os_ref, send_sems, recv_sems):
        x, y, c = _place()
        copies = []
        for s in range(4):
            for k in range(N_CHUNKS):
                rows = pl.ds(k * CHUNK_ROWS, CHUNK_ROWS)
                copies.append(pltpu.make_async_remote_copy(
                    src_ref=g_ref.at[s, 1 - c, rows], dst_ref=ob_ref.at[s, rows], send_sem=send_sems.at[len(copies)],
                    recv_sem=recv_sems.at[len(copies)], device_id=(x, y, 1 - c), device_id_type=MESH))
        copies.append(pltpu.make_async_remote_copy(
            src_ref=s_ref, dst_ref=os_ref, send_sem=send_sems.at[len(copies)], recv_sem=recv_sems.at[len(copies)],
            device_id=(x, y, 1 - c), device_id_type=MESH))
        for cp in copies:
            cp.start()
        for cp in copies:
            cp.wait_recv()
        for cp in copies:
            cp.wait_send()

    n = 4 * N_CHUNKS + 1
    return pl.pallas_call(
        body, name="swap_halves", in_specs=[ANY, ANY], out_specs=[ANY, ANY],
        out_shape=[SDS((4, HALF_ROWS, D), F32), SDS(small.shape, F32)],
        scratch_shapes=[pltpu.SemaphoreType.DMA((n,)), pltpu.SemaphoreType.DMA((n,))])(gbig, small)


def _pair_sum(gbig, other):
    def body(g_ref, o_ref, pb_ref, own_ref):
        x, y, c = _place()
        v = jnp.where(c == 0, g_ref[0], g_ref[1]) + o_ref[...]
        pb_ref[...] = v.astype(BF16)

        @pl.when(pl.program_id(1) == 2 * x + y)
        def _():
            own_ref[...] = v

    return pl.pallas_call(
        body, name="pair_sum", grid=(N_CHUNKS, 4),
        in_specs=[pl.BlockSpec((None, 2, CHUNK_ROWS, D), lambda i, s: (s, 0, i, 0)),
                  pl.BlockSpec((None, CHUNK_ROWS, D), lambda i, s: (s, i, 0))],
        out_specs=[pl.BlockSpec((None, CHUNK_ROWS, D), lambda i, s: (s, i, 0)),
                   pl.BlockSpec((CHUNK_ROWS, D), lambda i, s: (i, 0))],
        out_shape=[SDS((4, HALF_ROWS, D), BF16), SDS((HALF_ROWS, D), F32)],
        compiler_params=_params(2, VMEM_LIMIT))(gbig, other)


def _add(a, b, name):
    def body(a_ref, b_ref, o_ref):
        o_ref[...] = a_ref[...] + b_ref[...]

    return pl.pallas_call(body, name=name, out_shape=SDS(a.shape, a.dtype), compiler_params=_params(0, VMEM_LIMIT))(a, b)


def _chip_exchange(pb, ps):
    n_small = ps.shape[0]

    def body(pb_ref, ps_ref, lb_ref, ls_ref, send_sems, recv_sems):
        x, y, c = _place()
        chip = 2 * x + y
        peers = [(1 - x, y), (x, 1 - y), (1 - x, 1 - y)]

        def copies(slab_of):
            out = []
            for j, (px, py) in enumerate(peers):
                for k in range(N_CHUNKS):
                    rows = pl.ds(k * CHUNK_ROWS, CHUNK_ROWS)
                    out.append(pltpu.make_async_remote_copy(
                        src_ref=pb_ref.at[2 * px + py, rows], dst_ref=lb_ref.at[slab_of(j), rows],
                        send_sem=send_sems.at[len(out)], recv_sem=recv_sems.at[len(out)], device_id=(px, py, c),
                        device_id_type=MESH))
                out.append(pltpu.make_async_remote_copy(
                    src_ref=ps_ref, dst_ref=ls_ref.at[slab_of(j)], send_sem=send_sems.at[len(out)],
                    recv_sem=recv_sems.at[len(out)], device_id=(px, py, c), device_id_type=MESH))
            return out

        sends = copies(lambda j: chip)
        for cp in sends:
            cp.start()
        for cp in copies(lambda j: 2 * peers[j][0] + peers[j][1]):
            cp.wait_recv()
        for cp in sends:
            cp.wait_send()

    n = 3 * (N_CHUNKS + 1)
    return pl.pallas_call(
        body, name="chip_exchange", in_specs=[ANY, ANY], out_specs=[ANY, ANY],
        out_shape=[SDS((4, HALF_ROWS, D), BF16), SDS((4, n_small, D), F32)],
        scratch_shapes=[pltpu.SemaphoreType.DMA((n,)), pltpu.SemaphoreType.DMA((n,))])(pb, ps)


def _chip_sum(own, land):
    def body(own_ref, l_ref, o_ref):
        chip = 2 * lax.axis_index("x") + lax.axis_index("y")
        acc = jnp.where(chip == 0, own_ref[...], l_ref[0].astype(F32))
        for s in range(1, 4):
            acc = acc + jnp.where(chip == s, own_ref[...], l_ref[s].astype(F32))
        o_ref[...] = acc

    return pl.pallas_call(
        body, name="chip_sum", grid=(N_CHUNKS,),
        in_specs=[pl.BlockSpec((CHUNK_ROWS, D), lambda i: (i, 0)), pl.BlockSpec((4, CHUNK_ROWS, D), lambda i: (0, i, 0))],
        out_specs=pl.BlockSpec((CHUNK_ROWS, D), lambda i: (i, 0)), out_shape=SDS((HALF_ROWS, D), F32),
        compiler_params=_params(1, VMEM_LIMIT))(own, land)


def _sum4(own, land):
    def body(own_ref, l_ref, o_ref):
        chip = 2 * lax.axis_index("x") + lax.axis_index("y")
        acc = jnp.where(chip == 0, own_ref[...], l_ref[0])
        for s in range(1, 4):
            acc = acc + jnp.where(chip == s, own_ref[...], l_ref[s])
        o_ref[...] = acc

    return pl.pallas_call(body, name="sum_small", out_shape=SDS(own.shape, own.dtype),
                          compiler_params=_params(0, VMEM_LIMIT))(own, land)


def _share_half(r):
    def body(r_ref, o_ref, send_sems, recv_sems):
        x, y, c = _place()
        copies = [pltpu.make_async_remote_copy(
            src_ref=r_ref.at[pl.ds(k * CHUNK_ROWS, CHUNK_ROWS)], dst_ref=o_ref.at[pl.ds(k * CHUNK_ROWS, CHUNK_ROWS)],
            send_sem=send_sems.at[k], recv_sem=recv_sems.at[k], device_id=(x, y, 1 - c), device_id_type=MESH)
            for k in range(N_CHUNKS)]
        for cp in copies:
            cp.start()
        for cp in copies:
            cp.wait_recv()
        for cp in copies:
            cp.wait_send()

    return pl.pallas_call(
        body, name="share_half", in_specs=[ANY], out_specs=ANY, out_shape=SDS((HALF_ROWS, D), F32),
        scratch_shapes=[pltpu.SemaphoreType.DMA((N_CHUNKS,)), pltpu.SemaphoreType.DMA((N_CHUNKS,))])(r)


CLASSES = {"a": (1024, 256), "b": (2560, 640), "c": (512, 256), "o": (768, 256), "kv": (128, 128)}
W_CLASSES = {"a": (1024, 256), "b": (512, 256), "c": (512, 256), "o": (768, 256), "kv": (128, 128)}


def _chunk_list(specs):
    return [(k, r, chunk) for k, (half, chunk) in enumerate(specs) for r in range(0, half, chunk)]


def _gather_classes(arrs, specs, vec):
    n = len(arrs)
    chunks = _chunk_list(specs)
    nc = len(chunks)

    def body(*refs):
        ins, vec_ref, outs, ov_ref = refs[:n], refs[n], refs[n + 1:2 * n + 1], refs[2 * n + 1]
        ici_send, ici_recv, d2d_send, d2d_recv, vec_send, vec_recv = refs[2 * n + 2:]
        x, y, c = _place()
        chip = 2 * x + y
        peers = [(1 - x, y), (x, 1 - y), (1 - x, 1 - y)]

        def rows(ci, half):
            k, r, cnt = chunks[ci]
            return k, pl.ds(half * specs[k][0] + r, cnt)

        def over_ici(j, ci, slab):
            px, py = peers[j]
            k, rs = rows(ci, c)
            return pltpu.make_async_remote_copy(
                src_ref=ins[k].at[rs], dst_ref=outs[k].at[slab, rs], send_sem=ici_send.at[j * nc + ci],
                recv_sem=ici_recv.at[j * nc + ci], device_id=(px, py, c), device_id_type=MESH)

        def over_d2d(j, ci, half):
            px, py = peers[j]
            k, rs = rows(ci, half)
            where = outs[k].at[2 * px + py, rs]
            return pltpu.make_async_remote_copy(
                src_ref=where, dst_ref=where, send_sem=d2d_send.at[j * nc + ci], recv_sem=d2d_recv.at[j * nc + ci],
                device_id=(x, y, 1 - c), device_id_type=MESH)

        def small(j, slab):
            px, py = peers[j]
            return pltpu.make_async_remote_copy(
                src_ref=vec_ref, dst_ref=ov_ref.at[slab], send_sem=vec_send.at[j], recv_sem=vec_recv.at[j],
                device_id=(px, py, c), device_id_type=MESH)

        sends = [small(j, chip) for j in range(3)] + [over_ici(j, ci, chip) for ci in range(nc) for j in range(3)]
        for cp in sends:
            cp.start()
        forwards = []
        for ci in range(nc):
            for j, (px, py) in enumerate(peers):
                over_ici(j, ci, 2 * px + py).wait_recv()
                forwards.append(over_d2d(j, ci, c))
                forwards[-1].start()
        for ci in range(nc):
            for j in range(3):
                over_d2d(j, ci, 1 - c).wait_recv()
        for j, (px, py) in enumerate(peers):
            small(j, 2 * px + py).wait_recv()
        for cp in sends + forwards:
            cp.wait_send()

    dma = pltpu.SemaphoreType.DMA
    return pl.pallas_call(
        body, name="gather_weights", in_specs=[ANY] * (n + 1), out_specs=[ANY] * (n + 1),
        out_shape=[SDS((4,) + a.shape, a.dtype) for a in arrs] + [SDS((4,) + vec.shape, vec.dtype)],
        scratch_shapes=[dma((3 * nc,)), dma((3 * nc,)), dma((3 * nc,)), dma((3 * nc,)), dma((3,)), dma((3,))])(*arrs, vec)


def _swap_classes(grads, specs, small):
    n = len(grads)
    chunks = _chunk_list(specs)

    def body(*refs):
        ins, s_ref, outs, os_ref, send_sems, recv_sems = refs[:n], refs[n], refs[n + 1:2 * n + 1], *refs[2 * n + 1:]
        x, y, c = _place()
        copies = []
        for s in range(4):
            for k, r, cnt in chunks:
                copies.append(pltpu.make_async_remote_copy(
                    src_ref=ins[k].at[s, 1 - c, pl.ds(r, cnt)], dst_ref=outs[k].at[s, pl.ds(r, cnt)],
                    send_sem=send_sems.at[len(copies)], recv_sem=recv_sems.at[len(copies)],
                    device_id=(x, y, 1 - c), device_id_type=MESH))
        copies.append(pltpu.make_async_remote_copy(
            src_ref=s_ref, dst_ref=os_ref, send_sem=send_sems.at[len(copies)], recv_sem=recv_sems.at[len(copies)],
            device_id=(x, y, 1 - c), device_id_type=MESH))
        for cp in copies:
            cp.start()
        for cp in copies:
            cp.wait_recv()
        for cp in copies:
            cp.wait_send()

    m = 4 * len(chunks) + 1
    return pl.pallas_call(
        body, name="swap_halves", in_specs=[ANY] * (n + 1), out_specs=[ANY] * (n + 1),
        out_shape=[SDS((4, g.shape[2], g.shape[3]), F32) for g in grads] + [SDS(small.shape, F32)],
        scratch_shapes=[pltpu.SemaphoreType.DMA((m,)), pltpu.SemaphoreType.DMA((m,))])(*grads, small)


def _pair_sum_class(g, other, chunk, name):
    _, _, half, w = g.shape

    def body(g_ref, o_ref, pb_ref, own_ref):
        x, y, c = _place()
        v = jnp.where(c == 0, g_ref[0], g_ref[1]) + o_ref[...]
        pb_ref[...] = v.astype(BF16)

        @pl.when(pl.program_id(1) == 2 * x + y)
        def _():
            own_ref[...] = v

    return pl.pallas_call(
        body, name=name, grid=(half // chunk, 4),
        in_specs=[pl.BlockSpec((None, 2, chunk, w), lambda i, s: (s, 0, i, 0)),
                  pl.BlockSpec((None, chunk, w), lambda i, s: (s, i, 0))],
        out_specs=[pl.BlockSpec((None, chunk, w), lambda i, s: (s, i, 0)), pl.BlockSpec((chunk, w), lambda i, s: (i, 0))],
        out_shape=[SDS((4, half, w), BF16), SDS((half, w), F32)],
        compiler_params=_params(2, VMEM_LIMIT))(g, other)


def _exchange_classes(pbs, specs, ps):
    n = len(pbs)
    chunks = _chunk_list(specs)
    per_peer = len(chunks) + 1

    def body(*refs):
        ins, ps_ref, outs, ls_ref, send_sems, recv_sems = refs[:n], refs[n], refs[n + 1:2 * n + 1], *refs[2 * n + 1:]
        x, y, c = _place()
        chip = 2 * x + y
        peers = [(1 - x, y), (x, 1 - y), (1 - x, 1 - y)]

        def copies(slab_of):
            out = []
            for j, (px, py) in enumerate(peers):
                for k, r, cnt in chunks:
                    out.append(pltpu.make_async_remote_copy(
                        src_ref=ins[k].at[2 * px + py, pl.ds(r, cnt)], dst_ref=outs[k].at[slab_of(j), pl.ds(r, cnt)],
                        send_sem=send_sems.at[len(out)], recv_sem=recv_sems.at[len(out)], device_id=(px, py, c),
                        device_id_type=MESH))
                out.append(pltpu.make_async_remote_copy(
                    src_ref=ps_ref, dst_ref=ls_ref.at[slab_of(j)], send_sem=send_sems.at[len(out)],
                    recv_sem=recv_sems.at[len(out)], device_id=(px, py, c), device_id_type=MESH))
            return out

        sends = copies(lambda j: chip)
        for cp in sends:
            cp.start()
        for cp in copies(lambda j: 2 * peers[j][0] + peers[j][1]):
            cp.wait_recv()
        for cp in sends:
            cp.wait_send()

    m = 3 * per_peer
    return pl.pallas_call(
        body, name="chip_exchange", in_specs=[ANY] * (n + 1), out_specs=[ANY] * (n + 1),
        out_shape=[SDS(p.shape, BF16) for p in pbs] + [SDS((4,) + ps.shape, F32)],
        scratch_shapes=[pltpu.SemaphoreType.DMA((m,)), pltpu.SemaphoreType.DMA((m,))])(*pbs, ps)


def _chip_sum_class(own, land, chunk, name):
    half, w = own.shape

    def body(own_ref, l_ref, o_ref):
        chip = 2 * lax.axis_index("x") + lax.axis_index("y")
        acc = jnp.where(chip == 0, own_ref[...], l_ref[0].astype(F32))
        for s in range(1, 4):
            acc = acc + jnp.where(chip == s, own_ref[...], l_ref[s].astype(F32))
        o_ref[...] = acc

    return pl.pallas_call(
        body, name=name, grid=(half // chunk,),
        in_specs=[pl.BlockSpec((chunk, w), lambda i: (i, 0)), pl.BlockSpec((4, chunk, w), lambda i: (0, i, 0))],
        out_specs=pl.BlockSpec((chunk, w), lambda i: (i, 0)), out_shape=SDS((half, w), F32),
        compiler_params=_params(1, VMEM_LIMIT))(own, land)


def _share_classes(rs, specs):
    n = len(rs)
    chunks = _chunk_list(specs)

    def body(*refs):
        ins, outs, send_sems, recv_sems = refs[:n], refs[n:2 * n], *refs[2 * n:]
        x, y, c = _place()
        copies = [pltpu.make_async_remote_copy(
            src_ref=ins[k].at[pl.ds(r, cnt)], dst_ref=outs[k].at[pl.ds(r, cnt)], send_sem=send_sems.at[i],
            recv_sem=recv_sems.at[i], device_id=(x, y, 1 - c), device_id_type=MESH)
            for i, (k, r, cnt) in enumerate(chunks)]
        for cp in copies:
            cp.start()
        for cp in copies:
            cp.wait_recv()
        for cp in copies:
            cp.wait_send()

    m = len(chunks)
    return pl.pallas_call(
        body, name="share_half", in_specs=[ANY] * n, out_specs=[ANY] * n, out_shape=[SDS(r.shape, F32) for r in rs],
        scratch_shapes=[pltpu.SemaphoreType.DMA((m,)), pltpu.SemaphoreType.DMA((m,))])(*rs)


_SMALL = ["mem_norm_w", "norm_pre", "norm_post", "a_ln_w", "a_ln_b", "a_w_s", "a_b_s", "b_conv_w", "c_conv_w",
          "c_a_log", "c_dt_bias", "c_o_norm_w"]
_SMALL_SHAPES = {"mem_norm_w": (D,), "norm_pre": (4, D), "norm_post": (4, D), "a_ln_w": (2, D), "a_ln_b": (2, D),
                 "a_w_s": (2, 8, HD, HD), "a_b_s": (2, 8, HD), "b_conv_w": (1, 3, D), "c_conv_w": (1, 4, 3 * D),
                 "c_a_log": (1, 8), "c_dt_bias": (1, 8), "c_o_norm_w": (1, HD)}
_SHARDED_SMALL = {"a_ln_w": D // 4, "a_ln_b": D // 4, "b_conv_w": D // 4, "c_conv_w": 3 * D // 4}
_ROWS = [2048, 1280, 1284, 1536, 256]
_BIG_ROWS = sum(_ROWS)
_BIG_PAD = 6416
_SMALL_ROWS = 288


def _size(shape):
    n = 1
    for d in shape:
        n *= d
    return n


def kernel(x, mem, mem_norm_w, w_mem_kv, norm_pre, norm_post, w_out, a_w_in, a_ln_w, a_ln_b, a_w_s, a_b_s, b_w_in, b_conv_w, c_w_in, c_conv_w, c_a_log, c_dt_bias, c_o_norm_w, loss_target, m_mem_norm_w, m_w_mem_kv, m_norm_pre, m_norm_post, m_w_out, m_a_w_in, m_a_ln_w, m_a_ln_b, m_a_w_s, m_a_b_s, m_b_w_in, m_b_conv_w, m_c_w_in, m_c_conv_w, m_c_a_log, m_c_dt_bias, m_c_o_norm_w, v_mem_norm_w, v_w_mem_kv, v_norm_pre, v_norm_post, v_w_out, v_a_w_in, v_a_ln_w, v_a_ln_b, v_a_w_s, v_a_b_s, v_b_w_in, v_b_conv_w, v_c_w_in, v_c_conv_w, v_c_a_log, v_c_dt_bias, v_c_o_norm_w):
    names = ["mem_norm_w", "w_mem_kv", "norm_pre", "norm_post", "w_out", "a_w_in", "a_ln_w", "a_ln_b", "a_w_s", "a_b_s",
             "b_w_in", "b_conv_w", "c_w_in", "c_conv_w", "c_a_log", "c_dt_bias", "c_o_norm_w"]
    w = dict(zip(names, [mem_norm_w, w_mem_kv, norm_pre, norm_post, w_out, a_w_in, a_ln_w, a_ln_b, a_w_s, a_b_s, b_w_in,
                         b_conv_w, c_w_in, c_conv_w, c_a_log, c_dt_bias, c_o_norm_w]))
    m = dict(zip(names, [m_mem_norm_w, m_w_mem_kv, m_norm_pre, m_norm_post, m_w_out, m_a_w_in, m_a_ln_w, m_a_ln_b, m_a_w_s,
                         m_a_b_s, m_b_w_in, m_b_conv_w, m_c_w_in, m_c_conv_w, m_c_a_log, m_c_dt_bias, m_c_o_norm_w]))
    v = dict(zip(names, [v_mem_norm_w, v_w_mem_kv, v_norm_pre, v_norm_post, v_w_out, v_a_w_in, v_a_ln_w, v_a_ln_b, v_a_w_s,
                         v_a_b_s, v_b_w_in, v_b_conv_w, v_c_w_in, v_c_conv_w, v_c_a_log, v_c_dt_bias, v_c_o_norm_w]))
    chip = 2 * lax.axis_index("x") + lax.axis_index("y")

    mine_w = [a_w_in.reshape(2 * D, D), b_w_in.reshape(D, 1280), c_w_in.reshape(D, 1284), w_out.reshape(D_CAT, D),
              w_mem_kv]
    mine_w = [a.astype(BF16) for a in mine_w]
    vec = jnp.concatenate([a_ln_w.reshape(-1), a_ln_b.reshape(-1), b_conv_w.reshape(-1), c_conv_w.reshape(-1)])
    vec = jnp.pad(vec, (0, 8 * D - vec.shape[0])).reshape(8, D)
    *gathered, gvec = _gather_classes(mine_w, [W_CLASSES[k] for k in W_CLASSES], vec)
    ga, gb, gc, go, gkv = [lax.dynamic_update_slice(g_, a[None], (chip, 0, 0)) for g_, a in zip(gathered, mine_w)]
    gvec = lax.dynamic_update_slice(gvec, vec[None], (chip, 0, 0))
    ga = ga.reshape(4, 2, D, D)
    go = go.reshape(4, 4, 384, D)
    fc = gc.transpose(1, 0, 2).reshape(D, 5136)
    gv = gvec.reshape(4, 8 * D)
    sm = {"mem_norm_w": mem_norm_w, "norm_pre": norm_pre, "norm_post": norm_post, "a_w_s": a_w_s, "a_b_s": a_b_s,
          "c_a_log": c_a_log, "c_dt_bias": c_dt_bias, "c_o_norm_w": c_o_norm_w,
          "a_ln_w": gv[:, 0:512].reshape(4, 2, 256).transpose(1, 0, 2).reshape(2, D),
          "a_ln_b": gv[:, 512:1024].reshape(4, 2, 256).transpose(1, 0, 2).reshape(2, D),
          "b_conv_w": gv[:, 1024:1792].reshape(4, 1, 3, 256).transpose(1, 2, 0, 3).reshape(1, 3, D),
          "c_conv_w": gv[:, 1792:4864].reshape(4, 1, 4, 768).transpose(1, 2, 0, 3).reshape(1, 4, 3 * D)}
    c_mix = jnp.concatenate([fc[:, :3 * D + 16], jnp.zeros((D, AB_PAD - 16), BF16)], axis=1)
    wts = {"wkv": gkv.reshape(D, 2 * D_XA), "wo": [go[:, i].reshape(D_CAT, D) for i in range(DEPTH)],
           "mix": [[ga[0, 0], ga[1, 0]], [gb[0], gb[1], gb[2][:, :512]], [c_mix], [ga[0, 1], ga[1, 1]]],
           "gate": [[ga[2, 0], ga[3, 0]], [gb[2][:, 512:], gb[3]], [fc[:, 3 * D + 16:]], [ga[2, 1], ga[3, 1]]]}

    loss, dx, g = _local_step(x[0], mem[0], loss_target[0], wts, sm)
    loss = lax.psum(loss, ("x", "y", "c"))

    gc = jnp.concatenate([g["wm"][2][:, :3 * D + 16], g["wg"][2]], axis=1)
    by_chip = {
        "a": jnp.stack([jnp.concatenate([g["wm"][i], g["wg"][i]], axis=0) for i in (0, 3)], axis=1),
        "b": jnp.concatenate([g["wm"][1], g["wg"][1]], axis=0),
        "c": gc.reshape(D, 4, 1284).transpose(1, 0, 2),
        "o": jnp.stack(g["wo"]).reshape(4, 4, 384, D).transpose(1, 0, 2, 3),
        "kv": g["wkv"]}
    specs = [CLASSES[k] for k in CLASSES]
    halves = [by_chip[k].reshape(4, 2, CLASSES[k][0], -1) for k in CLASSES]
    gs = {"mem_norm_w": g["mem_norm_w"], "norm_pre": jnp.concatenate(g["norm_pre"]),
          "norm_post": jnp.concatenate(g["norm_post"])}
    for n in _SMALL[3:]:
        gs[n] = jnp.stack([g[n][j] for j in sorted(g[n])])
    flat = jnp.concatenate([gs[n].reshape(-1) for n in _SMALL])
    small = jnp.pad(flat, (0, _SMALL_ROWS * D - flat.shape[0])).reshape(_SMALL_ROWS, D)
    *others, other_small = _swap_classes(halves, specs, small)
    pairs = [_pair_sum_class(h, o, CLASSES[k][1], f"pair_sum_{k}") for k, h, o in zip(CLASSES, halves, others)]
    pair_small = _add(small, other_small, "pair_sum_small")
    *lands, land_small = _exchange_classes([p[0] for p in pairs], specs, pair_small)
    mine = [_chip_sum_class(p[1], land, CLASSES[k][1], f"chip_sum_{k}") for k, p, land in zip(CLASSES, pairs, lands)]
    theirs = _share_classes(mine, specs)
    south = lax.axis_index("c") == 0
    sh = {k: jnp.concatenate([jnp.where(south, a, b), jnp.where(south, b, a)], axis=0)
          for k, a, b in zip(CLASSES, mine, theirs)}
    grads = {"a_w_in": sh["a"].reshape(a_w_in.shape),
             "b_w_in": sh["b"].reshape(5, D, 256).transpose(1, 0, 2).reshape(b_w_in.shape),
             "c_w_in": sh["c"].reshape(c_w_in.shape), "w_out": sh["o"].reshape(w_out.shape), "w_mem_kv": sh["kv"]}
    flat = _sum4(pair_small, land_small).reshape(-1)
    off = 0
    for n in _SMALL:
        shape = _SMALL_SHAPES[n]
        full = flat[off:off + _size(shape)].reshape(shape)
        off += _size(shape)
        if n in _SHARDED_SMALL:
            full = lax.dynamic_slice_in_dim(full, chip * _SHARDED_SMALL[n], _SHARDED_SMALL[n], axis=len(shape) - 1)
        grads[n] = full

    delta, new_m, new_v = {}, {}, {}
    for n in names:
        shape = w[n].shape
        view = (1, shape[0]) if len(shape) == 1 else (_size(shape[:-1]), shape[-1])
        d_, m_, v_ = _adamw(w[n].reshape(view), grads[n].reshape(view), m[n].reshape(view), v[n].reshape(view),
                            f"adamw_{n}")
        delta[n], new_m[n], new_v[n] = d_.reshape(shape), m_.reshape(shape), v_.reshape(shape)
    return (loss, dx[None], *[grads[n].reshape(w[n].shape) for n in names], *[delta[n] for n in names],
            *[new_m[n] for n in names], *[new_v[n] for n in names])
```

```python
import functools

import jax
import jax.numpy as jnp
from jax import lax
from jax.experimental import pallas as pl
from jax.experimental.pallas import tpu as pltpu

F32 = jnp.float32
BF16 = jnp.bfloat16
HI = lax.Precision.HIGHEST
MESH = pl.DeviceIdType.MESH
SDS = jax.ShapeDtypeStruct

D = 1024
D_XA = 512
D_CAT = 1536
N_MEM = 256
HD = 128
DEPTH = 4
EPS = 1e-6
TT = 512
DN_C = 64
DN_TB = 256
HALO = 8
AB_PAD = 128
VMEM_LIMIT = 56 * 1024 * 1024
GRAD_TILE = {0: 1024, 1: 256, 2: None}

ADAM_LR, ADAM_B1, ADAM_B2, ADAM_EPS, ADAM_WD, ADAM_STEP = 0.001, 0.9, 0.999, 1e-08, 0.01, 10


def _params(n_grid, vmem=None):
    return pltpu.CompilerParams(dimension_semantics=("arbitrary",) * n_grid, vmem_limit_bytes=vmem)


def _rms(x, w):
    return x * lax.rsqrt(jnp.mean(x * x, axis=-1, keepdims=True) + EPS) * w


def _dot_nn(a, b):
    return jnp.dot(a.astype(BF16), b.astype(BF16), preferred_element_type=F32)


def _dot_nt(a, b):
    return lax.dot_general(a.astype(BF16), b.astype(BF16), (((1,), (1,)), ((), ())), preferred_element_type=F32)


def _dot_tn(a, b):
    return lax.dot_general(a.astype(BF16), b.astype(BF16), (((0,), (0,)), ((), ())), preferred_element_type=F32)


@jax.custom_vjp
def mm(a, b):
    return _dot_nn(a, b)


mm.defvjp(lambda a, b: (_dot_nn(a, b), (a, b)), lambda r, g: (_dot_nt(g, r[1]), _dot_tn(r[0], g)))


@jax.custom_vjp
def mm_nt(a, b):
    return _dot_nt(a, b)


mm_nt.defvjp(lambda a, b: (_dot_nt(a, b), (a, b)), lambda r, g: (_dot_nn(g, r[1]), _dot_tn(g, r[0])))


@jax.custom_vjp
def mm_tn(a, b):
    return _dot_tn(a, b)


mm_tn.defvjp(lambda a, b: (_dot_tn(a, b), (a, b)), lambda r, g: (_dot_nt(r[1], g), _dot_nn(r[0], g)))


def _dot_hi(a, b):
    return jnp.dot(a, b, precision=HI, preferred_element_type=F32)


def _row_spec(width, tile=TT):
    return pl.BlockSpec((tile, width), lambda i: (i, 0))


def _full_spec(shape):
    return pl.BlockSpec(shape, lambda *_: (0,) * len(shape))


def _widths(blocks):
    return [b.shape[1] for b in blocks]


def _inproj_fwd(x, nw, mix, gate, name):
    T, nm = x.shape[0], len(mix)
    M, G = sum(_widths(mix)), sum(_widths(gate))

    def body(x_ref, nw_ref, *refs):
        blocks, (pm_ref, pg_ref, h_ref) = refs[:-3], refs[-3:]
        h = _rms(x_ref[...], nw_ref[...]).astype(BF16)
        h_ref[...] = h
        for p_ref, group in ((pm_ref, blocks[:nm]), (pg_ref, blocks[nm:])):
            off = 0
            for w_ref in group:
                p_ref[:, off:off + w_ref.shape[1]] = jnp.dot(h, w_ref[...], preferred_element_type=F32)
                off += w_ref.shape[1]

    return pl.pallas_call(
        body, name=name, grid=(T // TT,),
        in_specs=[_row_spec(D), _full_spec((1, D))] + [_full_spec((D, w)) for w in _widths(mix + gate)],
        out_specs=[_row_spec(M), _row_spec(G), _row_spec(D)],
        out_shape=[SDS((T, M), F32), SDS((T, G), F32), SDS((T, D), BF16)],
        compiler_params=_params(1, VMEM_LIMIT))(x, nw, *mix, *gate)


def _inproj_bwd(dpm, dpg, x, nw, mix, gate, dxc, name):
    T, nm = x.shape[0], len(mix)
    M, G = sum(_widths(mix)), sum(_widths(gate))

    def body(dpm_ref, dpg_ref, x_ref, nw_ref, *refs):
        blocks, (dxc_ref, dx_ref, dnw_ref) = refs[:-3], refs[-3:]
        dh = None
        for dp_ref, group in ((dpm_ref, blocks[:nm]), (dpg_ref, blocks[nm:])):
            off = 0
            for w_ref in group:
                part = _dot_nt(dp_ref[:, off:off + w_ref.shape[1]], w_ref[...])
                dh = part if dh is None else dh + part
                off += w_ref.shape[1]
        _, vjp = jax.vjp(_rms, x_ref[...], nw_ref[...])
        dxr, dnw = vjp(dh)
        dx_ref[...] = dxc_ref[...] + dxr

        @pl.when(pl.program_id(0) == 0)
        def _():
            dnw_ref[...] = jnp.zeros_like(dnw_ref)
        dnw_ref[...] += dnw

    return pl.pallas_call(
        body, name=name, grid=(T // TT,),
        in_specs=[_row_spec(M), _row_spec(G), _row_spec(D), _full_spec((1, D))]
        + [_full_spec((D, w)) for w in _widths(mix + gate)] + [_row_spec(D)],
        out_specs=[_row_spec(D), _full_spec((1, D))],
        out_shape=[SDS((T, D), F32), SDS((1, D), F32)],
        compiler_params=_params(1, VMEM_LIMIT))(dpm, dpg, x, nw, *mix, *gate, dxc)


def _matmul_tn(a, b, name, sub=None):
    T, K = a.shape
    N = b.shape[1]
    tn = 1024 if N % 1024 == 0 else (640 if N % 640 == 0 else N)
    tt = min(1024, T)
    n_sub = 1 if sub is None else tn // sub

    def body(a_ref, b_ref, o_ref):
        @pl.when(pl.program_id(1) == 0)
        def _():
            o_ref[...] = jnp.zeros_like(o_ref)
        res = _dot_tn(a_ref[...], b_ref[...])
        if sub is None:
            o_ref[...] += res
        else:
            for i in range(n_sub):
                o_ref[i] += res[:, i * sub:(i + 1) * sub]

    if sub is None:
        out_spec, out_shape = pl.BlockSpec((K, tn), lambda j, t: (0, j)), SDS((K, N), F32)
    else:
        out_spec, out_shape = pl.BlockSpec((n_sub, K, sub), lambda j, t: (j, 0, 0)), SDS((N // sub, K, sub), F32)
    return pl.pallas_call(
        body, name=name, grid=(N // tn, T // tt),
        in_specs=[pl.BlockSpec((tt, K), lambda j, t: (t, 0)), pl.BlockSpec((tt, tn), lambda j, t: (t, j))],
        out_specs=out_spec, out_shape=out_shape,
        compiler_params=_params(2, VMEM_LIMIT))(a, b)


def _memkv_fn(mem, w, wkv):
    return mm(_rms(mem, w), wkv)


def _memkv_fwd(mem, w, wkv):
    def body(mem_ref, w_ref, wkv_ref, kv_ref):
        kv_ref[...] = _memkv_fn(mem_ref[...], w_ref[...], wkv_ref[...])

    return pl.pallas_call(body, name="memkv_fwd", out_shape=SDS((N_MEM, 2 * D_XA), F32),
                          compiler_params=_params(0, VMEM_LIMIT))(mem, w, wkv)


def _memkv_bwd(mem, w, wkv, dkv):
    def body(mem_ref, w_ref, wkv_ref, dkv_ref, dw_ref, dwkv_ref):
        _, vjp = jax.vjp(functools.partial(_memkv_fn, mem_ref[...]), w_ref[...], wkv_ref[...].astype(F32))
        dw, dwkv = vjp(dkv_ref[...])
        dw_ref[...] = dw
        dwkv_ref[...] = dwkv

    return pl.pallas_call(body, name="memkv_bwd", out_shape=[SDS((1, D), F32), SDS((D, 2 * D_XA), F32)],
                          compiler_params=_params(0, VMEM_LIMIT))(mem, w, wkv, dkv)


def _attn_gate(ymix, qx, z, *kvs):
    outs = []
    for j in range(4):
        s = mm_nt(qx[:, j * HD:(j + 1) * HD], kvs[j]) * (HD ** -0.5)
        e = jnp.exp(s - lax.stop_gradient(jnp.max(s, axis=-1, keepdims=True)))
        outs.append(mm(e / jnp.sum(e, axis=-1, keepdims=True), kvs[4 + j]))
    return jnp.concatenate([ymix] + outs, axis=1) * jax.nn.silu(z)


def _kv_blocks(kv_ref):
    return [kv_ref[:, j * HD:(j + 1) * HD] for j in range(8)]


def _ag_fwd(ymix, pg, kv, name):
    T = ymix.shape[0]

    def body(ymix_ref, pg_ref, kv_ref, ycat_ref):
        ycat_ref[...] = _attn_gate(ymix_ref[...], pg_ref[:, :D_XA], pg_ref[:, D_XA:], *_kv_blocks(kv_ref)).astype(BF16)

    return pl.pallas_call(
        body, name=name, grid=(T // TT,),
        in_specs=[_row_spec(D), _row_spec(D_XA + D_CAT), _full_spec((N_MEM, 2 * D_XA))],
        out_specs=_row_spec(D_CAT), out_shape=SDS((T, D_CAT), BF16),
        compiler_params=_params(1, VMEM_LIMIT))(ymix, pg, kv)


def _ag_bwd(dycat, ymix, pg, kv, dkv_in, name):
    T = ymix.shape[0]

    def body(dycat_ref, ymix_ref, pg_ref, kv_ref, dkvin_ref, dymix_ref, dpg_ref, dkv_ref):
        _, vjp = jax.vjp(_attn_gate, ymix_ref[...], pg_ref[:, :D_XA], pg_ref[:, D_XA:], *_kv_blocks(kv_ref))
        g = vjp(dycat_ref[...])
        dymix_ref[...] = g[0]
        dpg_ref[:, :D_XA] = g[1].astype(BF16)
        dpg_ref[:, D_XA:] = g[2].astype(BF16)

        @pl.when(pl.program_id(0) == 0)
        def _():
            dkv_ref[...] = dkvin_ref[...]
        for j in range(8):
            dkv_ref[:, j * HD:(j + 1) * HD] += g[3 + j]

    return pl.pallas_call(
        body, name=name, grid=(T // TT,),
        in_specs=[_row_spec(D_CAT), _row_spec(D), _row_spec(D_XA + D_CAT), _full_spec((N_MEM, 2 * D_XA)),
                  _full_spec((N_MEM, 2 * D_XA))],
        out_specs=[_row_spec(D), _row_spec(D_XA + D_CAT), _full_spec((N_MEM, 2 * D_XA))],
        out_shape=[SDS((T, D), F32), SDS((T, D_XA + D_CAT), BF16), SDS((N_MEM, 2 * D_XA), F32)],
        compiler_params=_params(1, VMEM_LIMIT))(dycat, ymix, pg, kv, dkv_in)


def _outproj_fwd(ycat, wo, x, nw, name):
    T = x.shape[0]

    def body(ycat_ref, wo_ref, x_ref, nw_ref, o_ref, xn_ref):
        o = jnp.dot(ycat_ref[...], wo_ref[...], preferred_element_type=F32)
        o_ref[...] = o
        xn_ref[...] = x_ref[...] + _rms(o, nw_ref[...])

    return pl.pallas_call(
        body, name=name, grid=(T // TT,),
        in_specs=[_row_spec(D_CAT), _full_spec((D_CAT, D)), _row_spec(D), _full_spec((1, D))],
        out_specs=[_row_spec(D), _row_spec(D)], out_shape=[SDS((T, D), F32), SDS((T, D), F32)],
        compiler_params=_params(1, VMEM_LIMIT))(ycat, wo, x, nw)


def _outproj_bwd(dxo, o, nw, wo, name):
    T = dxo.shape[0]

    def body(dxo_ref, o_ref, nw_ref, wo_ref, dycat_ref, dobf_ref, dnw_ref):
        _, vjp = jax.vjp(_rms, o_ref[...], nw_ref[...])
        do, dnw = vjp(dxo_ref[...])
        dobf = do.astype(BF16)
        dobf_ref[...] = dobf
        dycat_ref[...] = _dot_nt(dobf, wo_ref[...])

        @pl.when(pl.program_id(0) == 0)
        def _():
            dnw_ref[...] = jnp.zeros_like(dnw_ref)
        dnw_ref[...] += dnw

    return pl.pallas_call(
        body, name=name, grid=(T // TT,),
        in_specs=[_row_spec(D), _row_spec(D), _full_spec((1, D)), _full_spec((D_CAT, D))],
        out_specs=[_row_spec(D_CAT), _row_spec(D), _full_spec((1, D))],
        out_shape=[SDS((T, D_CAT), F32), SDS((T, D), BF16), SDS((1, D), F32)],
        compiler_params=_params(1, VMEM_LIMIT))(dxo, o, nw, wo)


def _loss_head(xl, target):
    T = xl.shape[0]

    def body(x_ref, t_ref, loss_ref, dx_ref):
        err = x_ref[...] - t_ref[...]
        dx_ref[...] = err * (1.0 / D)

        @pl.when(pl.program_id(0) == 0)
        def _():
            loss_ref[...] = jnp.zeros_like(loss_ref)
        part = jnp.sum(jnp.sum(err * err, axis=1, keepdims=True), axis=0, keepdims=True) * (0.5 / D)
        loss_ref[...] += jnp.broadcast_to(part, loss_ref.shape)

    return pl.pallas_call(
        body, name="loss_head", grid=(T // TT,),
        in_specs=[_row_spec(D), _row_spec(D)],
        out_specs=[_full_spec((8, 128)), _row_spec(D)], out_shape=[SDS((8, 128), F32), SDS((T, D), F32)],
        compiler_params=_params(1))(xl, target)


def _gmlp_pre(u, v, lnw, lnb):
    vg = jax.nn.gelu(v)
    xc = vg - jnp.mean(vg, axis=-1, keepdims=True)
    vl = xc * lax.rsqrt(jnp.mean(xc * xc, axis=-1, keepdims=True) + EPS) * lnw + lnb
    return jax.nn.gelu(u), vl


def _tril(n, strict=False):
    r = lax.broadcasted_iota(jnp.int32, (n, n), 0)
    c = lax.broadcasted_iota(jnp.int32, (n, n), 1)
    return (r > c) if strict else (r >= c)


def _gmlp_fwd(pm, lnw, lnb, ws, bs3, name):
    T = pm.shape[0]

    def body(pm_ref, lnw_ref, lnb_ref, ws_ref, bs_ref, y_ref):
        ug, vl = _gmlp_pre(pm_ref[:, :D], pm_ref[:, D:], lnw_ref[...], lnb_ref[...])
        mask = _tril(HD)
        for g in range(8):
            w = jnp.where(mask, ws_ref[g], 0.0)
            for c in range(TT // HD):
                rows, cols = slice(c * HD, (c + 1) * HD), slice(g * HD, (g + 1) * HD)
                y_ref[rows, cols] = ug[rows, cols] * (_dot_nn(w, vl[rows, cols]) + bs_ref[g])

    return pl.pallas_call(
        body, name=name, grid=(T // TT,),
        in_specs=[_row_spec(2 * D), _full_spec((1, D)), _full_spec((1, D)), _full_spec((8, HD, HD)),
                  _full_spec((8, HD, HD))],
        out_specs=_row_spec(D), out_shape=SDS((T, D), F32),
        compiler_params=_params(1, VMEM_LIMIT))(pm, lnw, lnb, ws, bs3)


def _gmlp_bwd(dy, pm, lnw, lnb, ws, bs3, name):
    T = pm.shape[0]
    n_t = T // TT

    def body(dy_ref, pm_ref, lnw_ref, lnb_ref, ws_ref, bs_ref, dpm_ref, dlnw_ref, dlnb_ref, dws_ref, dbs_ref,
             dug_scr, dvl_scr, dbs_scr):
        i = pl.program_id(0)

        @pl.when(i == 0)
        def _():
            dlnw_ref[...] = jnp.zeros_like(dlnw_ref)
            dlnb_ref[...] = jnp.zeros_like(dlnb_ref)
            dws_ref[...] = jnp.zeros_like(dws_ref)
            dbs_scr[...] = jnp.zeros_like(dbs_scr)

        (ug, vl), vjp = jax.vjp(_gmlp_pre, pm_ref[:, :D], pm_ref[:, D:], lnw_ref[...], lnb_ref[...])
        mask = _tril(HD)
        for g in range(8):
            w = jnp.where(mask, ws_ref[g], 0.0)
            dw = jnp.zeros((HD, HD), F32)
            db = jnp.zeros((HD, HD), F32)
            for c in range(TT // HD):
                rows, cols = slice(c * HD, (c + 1) * HD), slice(g * HD, (g + 1) * HD)
                dyb, vlb = dy_ref[rows, cols], vl[rows, cols]
                sp = _dot_nn(w, vlb) + bs_ref[g]
                dsp = dyb * ug[rows, cols]
                dug_scr[rows, cols] = dyb * sp
                dvl_scr[rows, cols] = _dot_tn(w, dsp)
                dw += _dot_nt(dsp, vlb)
                db += dsp
            dws_ref[g] += jnp.where(mask, dw, 0.0)
            dbs_scr[g] += db
        du, dv, dlnw, dlnb = vjp((dug_scr[...], dvl_scr[...]))
        dpm_ref[:, :D] = du.astype(BF16)
        dpm_ref[:, D:] = dv.astype(BF16)
        dlnw_ref[...] += dlnw
        dlnb_ref[...] += dlnb

        @pl.when(i == n_t - 1)
        def _():
            for g in range(8):
                dbs_ref[g] = jnp.broadcast_to(jnp.sum(dbs_scr[g], axis=1, keepdims=True), (HD, HD))

    return pl.pallas_call(
        body, name=name, grid=(n_t,),
        in_specs=[_row_spec(D), _row_spec(2 * D), _full_spec((1, D)), _full_spec((1, D)), _full_spec((8, HD, HD)),
                  _full_spec((8, HD, HD))],
        out_specs=[_row_spec(2 * D), _full_spec((1, D)), _full_spec((1, D)), _full_spec((8, HD, HD)),
                   _full_spec((8, HD, HD))],
        out_shape=[SDS((T, 2 * D), BF16), SDS((1, D), F32), SDS((1, D), F32), SDS((8, HD, HD), F32),
                   SDS((8, HD, HD), F32)],
        scratch_shapes=[pltpu.VMEM((TT, D), F32), pltpu.VMEM((TT, D), F32), pltpu.VMEM((8, HD, HD), F32)],
        compiler_params=_params(1, VMEM_LIMIT))(dy, pm, lnw, lnb, ws, bs3)


def _prev_spec(width, T):
    return pl.BlockSpec((HALO, width), lambda i: (jnp.maximum(i * (TT // HALO) - 1, 0), 0))


def _next_spec(width, T):
    return pl.BlockSpec((HALO, width), lambda i: (jnp.minimum((i + 1) * (TT // HALO), T // HALO - 1), 0))


def _rows_before(ext, j):
    return ext[HALO:] if j == 0 else pltpu.roll(ext, j, 0)[HALO:]


def _rows_after(ext, j):
    n = ext.shape[0]
    return ext[:n - HALO] if j == 0 else pltpu.roll(ext, n - j, 0)[:n - HALO]


def _conv_apply(ext_s, w):
    K = w.shape[0]
    y = _rows_before(ext_s, K - 1) * w[0:1]
    for k in range(1, K):
        y = y + _rows_before(ext_s, K - 1 - k) * w[k:k + 1]
    return y


def _conv_grads(ext_s, ext_dy, w):
    K = w.shape[0]
    dy = ext_dy[:ext_dy.shape[0] - HALO]
    ds = _rows_after(ext_dy, K - 1) * w[0:1]
    dws = [jnp.sum(dy * _rows_before(ext_s, K - 1), axis=0, keepdims=True)]
    for k in range(1, K):
        ds = ds + _rows_after(ext_dy, K - 1 - k) * w[k:k + 1]
        dws.append(jnp.sum(dy * _rows_before(ext_s, K - 1 - k), axis=0, keepdims=True))
    return ds, jnp.concatenate(dws, axis=0)


def _sconv_fwd(pm, w, name):
    T = pm.shape[0]

    def body(pm_ref, prev_ref, w_ref, y_ref):
        s = pm_ref[:, D:2 * D] * pm_ref[:, 2 * D:]
        sp = jnp.where(pl.program_id(0) > 0, prev_ref[:, D:2 * D] * prev_ref[:, 2 * D:], 0.0)
        y_ref[...] = pm_ref[:, :D] * _conv_apply(jnp.concatenate([sp, s], axis=0), w_ref[...])

    return pl.pallas_call(
        body, name=name, grid=(T // TT,),
        in_specs=[_row_spec(3 * D), _prev_spec(3 * D, T), _full_spec((3, D))],
        out_specs=_row_spec(D), out_shape=SDS((T, D), F32),
        compiler_params=_params(1, VMEM_LIMIT))(pm, pm, w)


def _sconv_bwd(dy, pm, w, name):
    T = pm.shape[0]
    n_t = T // TT

    def body(dy_ref, dyn_ref, pm_ref, prev_ref, next_ref, w_ref, dpm_ref, dw_ref):
        i = pl.program_id(0)
        bg, cg, hv = pm_ref[:, :D], pm_ref[:, D:2 * D], pm_ref[:, 2 * D:]
        sp = jnp.where(i > 0, prev_ref[:, D:2 * D] * prev_ref[:, 2 * D:], 0.0)
        ext_s = jnp.concatenate([sp, cg * hv], axis=0)
        dyv = dy_ref[...]
        dcn = jnp.where(i < n_t - 1, dyn_ref[...] * next_ref[:, :D], 0.0)
        ds, dw = _conv_grads(ext_s, jnp.concatenate([dyv * bg, dcn], axis=0), w_ref[...])
        dpm_ref[:, :D] = (dyv * _conv_apply(ext_s, w_ref[...])).astype(BF16)
        dpm_ref[:, D:2 * D] = (ds * hv).astype(BF16)
        dpm_ref[:, 2 * D:] = (ds * cg).astype(BF16)

        @pl.when(i == 0)
        def _():
            dw_ref[...] = jnp.zeros_like(dw_ref)
        dw_ref[...] += dw

    return pl.pallas_call(
        body, name=name, grid=(n_t,),
        in_specs=[_row_spec(D), _next_spec(D, T), _row_spec(3 * D), _prev_spec(3 * D, T), _next_spec(3 * D, T),
                  _full_spec((3, D))],
        out_specs=[_row_spec(3 * D), _full_spec((3, D))],
        out_shape=[SDS((T, 3 * D), BF16), SDS((3, D), F32)],
        compiler_params=_params(1, VMEM_LIMIT))(dy, dy, pm, pm, pm, w)


def _dnconv_fwd(pm, w, name):
    T = pm.shape[0]

    def body(pm_ref, prev_ref, w_ref, c_ref):
        sp = jnp.where(pl.program_id(0) > 0, prev_ref[...], 0.0)
        c_ref[...] = _conv_apply(jnp.concatenate([sp, pm_ref[...]], axis=0), w_ref[...])

    return pl.pallas_call(
        body, name=name, grid=(T // TT,),
        in_specs=[_row_spec(3 * D), _prev_spec(3 * D, T), _full_spec((4, 3 * D))],
        out_specs=_row_spec(3 * D), out_shape=SDS((T, 3 * D), F32),
        compiler_params=_params(1, VMEM_LIMIT))(pm, pm, w)


def _dnconv_bwd(dcq, dck, dcv, dab, pm, w, name):
    T = pm.shape[0]
    n_t = T // TT

    def body(dq_ref, dk_ref, dv_ref, dqn_ref, dkn_ref, dvn_ref, dab_ref, pm_ref, prev_ref, w_ref, dpm_ref, dw_ref):
        i = pl.program_id(0)
        sp = jnp.where(i > 0, prev_ref[...], 0.0)
        ext_s = jnp.concatenate([sp, pm_ref[...]], axis=0)
        own = jnp.concatenate([dq_ref[...], dk_ref[...], dv_ref[...]], axis=1)
        nxt = jnp.where(i < n_t - 1, jnp.concatenate([dqn_ref[...], dkn_ref[...], dvn_ref[...]], axis=1), 0.0)
        ds, dw = _conv_grads(ext_s, jnp.concatenate([own, nxt], axis=0), w_ref[...])
        dpm_ref[:, :3 * D] = ds.astype(BF16)
        dpm_ref[:, 3 * D:] = dab_ref[...].astype(BF16)

        @pl.when(i == 0)
        def _():
            dw_ref[...] = jnp.zeros_like(dw_ref)
        dw_ref[...] += dw

    return pl.pallas_call(
        body, name=name, grid=(n_t,),
        in_specs=[_row_spec(D), _row_spec(D), _row_spec(D), _next_spec(D, T), _next_spec(D, T), _next_spec(D, T),
                  _row_spec(AB_PAD), _row_spec(3 * D), _prev_spec(3 * D, T), _full_spec((4, 3 * D))],
        out_specs=[_row_spec(3 * D + AB_PAD), _full_spec((4, 3 * D))],
        out_shape=[SDS((T, 3 * D + AB_PAD), BF16), SDS((4, 3 * D), F32)],
        compiler_params=_params(1, VMEM_LIMIT))(dcq, dck, dcv, dcq, dck, dcv, dab, pm, pm, w)


def _l2n(x):
    return x * lax.rsqrt(jnp.sum(x * x, axis=-1, keepdims=True) + EPS)


def _softplus(x):
    return jnp.maximum(x, 0.0) + jnp.log1p(jnp.exp(-jnp.abs(x)))


def _dn_gates(ab, alog, dtb, h):
    lane = lax.broadcasted_iota(jnp.int32, ab.shape, 1)
    g_all = -jnp.exp(alog) * _softplus(ab + dtb)
    g = jnp.sum(jnp.where(lane == h, g_all, 0.0), axis=1, keepdims=True)
    beta = jnp.sum(jnp.where(lane == 8 + h, jax.nn.sigmoid(ab), 0.0), axis=1, keepdims=True)
    ones = jnp.ones((1, HD), F32)
    return g * ones, beta * ones


def _dn_chunk(cq, ck, cv, gb, bb, S, onw):
    C = DN_C
    q = _l2n(jax.nn.silu(cq)) * (HD ** -0.5)
    k = _l2n(jax.nn.silu(ck))
    v = jax.nn.silu(cv)
    incl, strict = _tril(C), _tril(C, strict=True)
    gcum = _dot_hi(incl.astype(F32), gb)
    gi = gcum[:, :C]
    gj = gcum.T[:C, :]
    decay = jnp.where(incl, jnp.exp(jnp.where(incl, gi - gj, 0.0)), 0.0)
    kb = k * bb
    a_mat = jnp.where(strict, mm_nt(kb, k) * decay, 0.0)
    p = -a_mat
    eye = (lax.broadcasted_iota(jnp.int32, (C, C), 0) == lax.broadcasted_iota(jnp.int32, (C, C), 1)).astype(F32)
    t_mat = eye + p
    for _ in range(5):
        p = _dot_hi(p, p)
        t_mat = t_mat + _dot_hi(t_mat, p)
    eg = jnp.exp(gcum)
    u = mm(t_mat, v * bb)
    w = mm(t_mat, kb * eg)
    qk = mm_nt(q, k) * decay
    glast = gcum[C - 1:C, :]
    v_new = u - mm(w, S)
    o = mm(q * eg, S) + mm(qk, v_new)
    s_new = S * jnp.exp(glast) + mm_tn(k * jnp.exp(glast - gcum), v_new)
    return _rms(o, onw), s_new


def _dn_specs(T, rev):
    nb = T // DN_TB
    blk = (lambda n: nb - 1 - n) if rev else (lambda n: n)
    head = [pl.BlockSpec((DN_TB, HD), functools.partial(lambda n, h, off: (blk(n), off + h), off=8 * s)) for s in range(3)]
    ab = pl.BlockSpec((DN_TB, AB_PAD), lambda n, h: (blk(n), 3 * D // AB_PAD))
    st = pl.BlockSpec((DN_TB // DN_C, None, HD, HD), lambda n, h: (blk(n), h, 0, 0))
    out = pl.BlockSpec((DN_TB, HD), lambda n, h: (blk(n), h))
    row = pl.BlockSpec((1, HD), lambda n, h: (0, 0))
    return nb, head, ab, st, out, row


def _dn_fwd(cpre, pm, alog, dtb, onw, name):
    T = cpre.shape[0]
    nb, head, ab, st, out, row = _dn_specs(T, False)

    def body(cq_ref, ck_ref, cv_ref, ab_ref, alog_ref, dtb_ref, onw_ref, o_ref, st_ref, s_scr):
        n, h = pl.program_id(0), pl.program_id(1)

        @pl.when(n == 0)
        def _():
            s_scr[h] = jnp.zeros((HD, HD), F32)
        gb, bb = _dn_gates(ab_ref[...], alog_ref[...], dtb_ref[...], h)
        S = s_scr[h]
        for c in range(DN_TB // DN_C):
            rows = slice(c * DN_C, (c + 1) * DN_C)
            st_ref[c] = S
            o, S = _dn_chunk(cq_ref[rows, :], ck_ref[rows, :], cv_ref[rows, :], gb[rows], bb[rows], S, onw_ref[...])
            o_ref[rows, :] = o
        s_scr[h] = S

    return pl.pallas_call(
        body, name=name, grid=(nb, 8),
        in_specs=head + [ab, row, row, row], out_specs=[out, st],
        out_shape=[SDS((T, D), F32), SDS((T // DN_C, 8, HD, HD), F32)],
        scratch_shapes=[pltpu.VMEM((8, HD, HD), F32)],
        compiler_params=_params(2, VMEM_LIMIT))(cpre, cpre, cpre, pm, alog, dtb, onw)


def _dn_bwd(do, cpre, pm, st, alog, dtb, onw, name):
    T = cpre.shape[0]
    nb, head, ab, stspec, out, row = _dn_specs(T, True)

    def body(do_ref, cq_ref, ck_ref, cv_ref, ab_ref, st_ref, alog_ref, dtb_ref, onw_ref,
             dcq_ref, dck_ref, dcv_ref, dab_ref, dalog_ref, ddtb_ref, donw_ref, ds_scr):
        n, h = pl.program_id(0), pl.program_id(1)

        @pl.when(n == 0)
        def _():
            ds_scr[h] = jnp.zeros((HD, HD), F32)

        @pl.when((n == 0) & (h == 0))
        def _():
            dalog_ref[...] = jnp.zeros_like(dalog_ref)
            ddtb_ref[...] = jnp.zeros_like(ddtb_ref)
            donw_ref[...] = jnp.zeros_like(donw_ref)

        @pl.when(h == 0)
        def _():
            dab_ref[...] = jnp.zeros_like(dab_ref)

        (gb, bb), gates_vjp = jax.vjp(lambda a, b, c: _dn_gates(a, b, c, h), ab_ref[...], alog_ref[...], dtb_ref[...])
        dS = ds_scr[h]
        n_c = DN_TB // DN_C
        dgs, dbs = [None] * n_c, [None] * n_c
        donw = jnp.zeros((1, HD), F32)
        for c in reversed(range(n_c)):
            rows = slice(c * DN_C, (c + 1) * DN_C)
            _, vjp = jax.vjp(_dn_chunk, cq_ref[rows, :], ck_ref[rows, :], cv_ref[rows, :], gb[rows], bb[rows],
                             st_ref[c], onw_ref[...])
            dcq, dck, dcv, dgs[c], dbs[c], dS, dn = vjp((do_ref[rows, :], dS))
            dcq_ref[rows, :] = dcq
            dck_ref[rows, :] = dck
            dcv_ref[rows, :] = dcv
            donw += dn
        ds_scr[h] = dS
        dab, dalog, ddtb = gates_vjp((jnp.concatenate(dgs, axis=0), jnp.concatenate(dbs, axis=0)))
        dab_ref[...] += dab
        dalog_ref[...] += dalog
        ddtb_ref[...] += ddtb
        donw_ref[...] += donw

    dabspec = pl.BlockSpec((DN_TB, AB_PAD), lambda n, h: (nb - 1 - n, 0))
    return pl.pallas_call(
        body, name=name, grid=(nb, 8),
        in_specs=[out] + head + [ab, stspec, row, row, row],
        out_specs=[out, out, out, dabspec, row, row, row],
        out_shape=[SDS((T, D), F32)] * 3 + [SDS((T, AB_PAD), F32)] + [SDS((1, HD), F32)] * 3,
        scratch_shapes=[pltpu.VMEM((8, HD, HD), F32)],
        compiler_params=_params(2, VMEM_LIMIT))(do, cpre, cpre, cpre, pm, st, alog, dtb, onw)


_BNN = (((2,), (1,)), ((0,), (0,)))
_BNT = (((2,), (2,)), ((0,), (0,)))
_BTN = (((1,), (1,)), ((0,), (0,)))


def _bdot(a, b, dims):
    return lax.dot_general(a.astype(BF16), b.astype(BF16), dims, preferred_element_type=F32)


def _bdot3(a, b, dims):
    ah, bh = a.astype(BF16), b.astype(BF16)
    al, bl = (a - ah.astype(F32)).astype(BF16), (b - bh.astype(F32)).astype(BF16)
    d = functools.partial(lax.dot_general, dimension_numbers=dims, preferred_element_type=F32)
    return d(ah, bh) + (d(ah, bl) + d(al, bh))


def _bdot_hi(a, b, dims):
    return lax.dot_general(a, b, dims, precision=HI, preferred_element_type=F32)


def _batched_matmuls(dot):
    @jax.custom_vjp
    def nn(a, b):
        return dot(a, b, _BNN)

    @jax.custom_vjp
    def nt(a, b):
        return dot(a, b, _BNT)

    @jax.custom_vjp
    def tn(a, b):
        return dot(a, b, _BTN)

    nn.defvjp(lambda a, b: (dot(a, b, _BNN), (a, b)), lambda r, g: (dot(g, r[1], _BNT), dot(r[0], g, _BTN)))
    nt.defvjp(lambda a, b: (dot(a, b, _BNT), (a, b)), lambda r, g: (dot(g, r[1], _BNN), dot(g, r[0], _BTN)))
    tn.defvjp(lambda a, b: (dot(a, b, _BTN), (a, b)), lambda r, g: (dot(r[1], g, _BNT), dot(r[0], g, _BNN)))
    return nn, nt, tn


bmm, bmm_nt, bmm_tn = _batched_matmuls(_bdot)
bmm3, _, _ = _batched_matmuls(_bdot3)
bmm_hi, bmm_hi_nt, _ = _batched_matmuls(_bdot_hi)

@jax.custom_vjp
def _neumann_inverse(n):
    C = n.shape[1]
    eye = lax.broadcasted_iota(jnp.int32, n.shape, 1) == lax.broadcasted_iota(jnp.int32, n.shape, 2)
    t = eye.astype(F32) + n
    for _ in range(5):
        n = _bdot3(n, n, _BNN)
        t = t + _bdot3(t, n, _BNN)
    return t


def _neumann_inverse_fwd(n):
    t = _neumann_inverse(n)
    return t, t


def _neumann_inverse_bwd(t, g):
    return (_bdot3(_bdot3(t, g, _BTN), t, _BNT),)


_neumann_inverse.defvjp(_neumann_inverse_fwd, _neumann_inverse_bwd)


@jax.custom_vjp
def _saved_inverse(n, t):
    return t


_saved_inverse.defvjp(lambda n, t: (t, t), lambda t, g: (_bdot3(_bdot3(t, g, _BTN), t, _BNT), jnp.zeros_like(t)))

DN_NCH = DN_TB // DN_C
DN_NH = 4


def _decay_terms(ab, alog, dtb, first_head, n_heads):
    C = DN_C
    lane = lax.broadcasted_iota(jnp.int32, ab.shape, 1)
    g_all = (-jnp.exp(alog) * _softplus(ab + dtb)).reshape(DN_NCH, C, HD)
    beta_all = jax.nn.sigmoid(ab)
    r = lax.broadcasted_iota(jnp.int32, (DN_NCH, C, C), 1)
    c = lax.broadcasted_iota(jnp.int32, (DN_NCH, C, C), 2)
    gc_all = bmm_hi((r >= c).astype(F32), g_all)
    gc_rows = [gc_all[i].T for i in range(DN_NCH)]
    lane3 = lax.broadcasted_iota(jnp.int32, (DN_NCH, C, HD), 2)
    row = lax.broadcasted_iota(jnp.int32, (HD, C), 0)
    ones = jnp.ones((1, HD), F32)
    gcs, gjs, betas = [], [], []
    for i in range(n_heads):
        h = first_head + i
        gcs.append(jnp.sum(jnp.where(lane3 == h, gc_all, 0.0), axis=2, keepdims=True) * ones)
        gjs.append(jnp.concatenate(
            [jnp.broadcast_to(jnp.sum(jnp.where(row == h, t, 0.0), axis=0, keepdims=True), (C, C))[None] for t in gc_rows],
            axis=0))
        beta = jnp.sum(jnp.where(lane == 8 + h, beta_all, 0.0), axis=1, keepdims=True) * ones
        betas.append(beta.reshape(DN_NCH, C, HD))
    return jnp.concatenate(gcs, axis=0), jnp.concatenate(gjs, axis=0), jnp.concatenate(betas, axis=0)


def _dn_prep(cq, ck, cv, gcum, gj, bb, t_saved=None):
    B, C = cq.shape[0], DN_C
    q = _l2n(jax.nn.silu(cq)) * (HD ** -0.5)
    k = _l2n(jax.nn.silu(ck))
    v = jax.nn.silu(cv)
    r = lax.broadcasted_iota(jnp.int32, (B, C, C), 1)
    c = lax.broadcasted_iota(jnp.int32, (B, C, C), 2)
    incl, strict = r >= c, r > c
    decay = jnp.where(incl, jnp.exp(jnp.where(incl, gcum[:, :, :C] - gj, 0.0)), 0.0)
    kb = k * bb
    n_mat = -jnp.where(strict, bmm_nt(kb, k) * decay, 0.0)
    t_mat = _neumann_inverse(n_mat) if t_saved is None else _saved_inverse(n_mat, t_saved)
    eg = jnp.exp(gcum)
    glast = gcum[:, C - 1:C, :]
    return (bmm(t_mat, v * bb), bmm(t_mat, kb * eg), bmm_nt(q, k) * decay, q * eg, k * jnp.exp(glast - gcum),
            jnp.exp(glast), t_mat)


def _dn_scan_step(u, w, qk, qd, kd, egl, S, onw):
    v_new = u - bmm(w, S)
    o = bmm(qd, S) + bmm(qk, v_new)
    return _rms(o, onw), S * egl + bmm_tn(kd, v_new)


def _head_gates(ab, alog, dtb, first_head, n_heads):
    gs, bs = [], []
    for i in range(n_heads):
        g, b = _dn_gates(ab, alog, dtb, first_head + i)
        gs.append(g.reshape(DN_NCH, DN_C, HD))
        bs.append(b.reshape(DN_NCH, DN_C, HD))
    return jnp.concatenate(gs, axis=0), jnp.concatenate(bs, axis=0)


def _to_batch(ref, n_heads):
    return jnp.concatenate([ref[:, i * HD:(i + 1) * HD].astype(F32).reshape(DN_NCH, DN_C, HD) for i in range(n_heads)],
                           axis=0)


def _from_batch(ref, val, n_heads):
    for i in range(n_heads):
        ref[:, i * HD:(i + 1) * HD] = val[i * DN_NCH:(i + 1) * DN_NCH].reshape(DN_TB, HD).astype(ref.dtype)


def _prep_specs(T, rev):
    nb = T // DN_TB
    blk = (lambda n: nb - 1 - n) if rev else (lambda n: n)
    ng = 8 // DN_NH
    head = [pl.BlockSpec((DN_TB, DN_NH * HD), functools.partial(lambda n, h, off: (blk(n), off + h), off=ng * s))
            for s in range(3)]
    ab = pl.BlockSpec((DN_TB, AB_PAD), lambda n, h: (blk(n), 3 * D // AB_PAD))
    row = pl.BlockSpec((1, HD), lambda n, h: (0, 0))
    wide = pl.BlockSpec((DN_TB, DN_NH * HD), lambda n, h: (blk(n), h))
    qk = pl.BlockSpec((DN_NCH, DN_NH, DN_C, DN_C), lambda n, h: (blk(n), h, 0, 0))
    eg = pl.BlockSpec((DN_NCH, DN_NH, 1, HD), lambda n, h: (blk(n), h, 0, 0))
    return nb, ng, head, ab, row, wide, qk, eg


def _dn_prep_fwd(cpre, pm, alog, dtb, name):
    T = cpre.shape[0]
    nb, ng, head, ab, row, wide, qks, egs = _prep_specs(T, False)

    def body(cq_ref, ck_ref, cv_ref, ab_ref, alog_ref, dtb_ref, u_ref, w_ref, qk_ref, qd_ref, kd_ref, e_ref, t_ref):
        gcum, gj, bb = _decay_terms(ab_ref[...], alog_ref[...], dtb_ref[...], pl.program_id(1) * DN_NH, DN_NH)
        u, w, qk, qd, kd, egl, t_mat = _dn_prep(_to_batch(cq_ref, DN_NH), _to_batch(ck_ref, DN_NH),
                                                _to_batch(cv_ref, DN_NH), gcum, gj, bb)
        _from_batch(u_ref, u, DN_NH)
        _from_batch(w_ref, w, DN_NH)
        _from_batch(qd_ref, qd, DN_NH)
        _from_batch(kd_ref, kd, DN_NH)
        for i in range(DN_NH):
            qk_ref[:, i] = qk[i * DN_NCH:(i + 1) * DN_NCH].astype(BF16)
            e_ref[:, i] = egl[i * DN_NCH:(i + 1) * DN_NCH]
            t_ref[:, i] = t_mat[i * DN_NCH:(i + 1) * DN_NCH]

    return pl.pallas_call(
        body, name=name, grid=(nb, ng), in_specs=head + [ab, row, row],
        out_specs=[wide, wide, qks, wide, wide, egs, qks],
        out_shape=[SDS((T, D), F32), SDS((T, D), BF16), SDS((T // DN_C, 8, DN_C, DN_C), BF16), SDS((T, D), BF16),
                   SDS((T, D), BF16), SDS((T // DN_C, 8, 1, HD), F32), SDS((T // DN_C, 8, DN_C, DN_C), F32)],
        compiler_params=_params(2, VMEM_LIMIT))(cpre, cpre, cpre, pm, alog, dtb)


def _dn_prep_bwd(du, dw, dqk, dqd, dkd, degl, t_mat, cpre, pm, alog, dtb, name):
    T = cpre.shape[0]
    nb, ng, head, ab, row, wide, qks, egs = _prep_specs(T, True)

    def body(du_ref, dw_ref, dqk_ref, dqd_ref, dkd_ref, de_ref, t_ref, cq_ref, ck_ref, cv_ref, ab_ref, alog_ref,
             dtb_ref, dcq_ref, dck_ref, dcv_ref, dab_ref, dalog_ref, ddtb_ref):
        n, h = pl.program_id(0), pl.program_id(1)

        @pl.when((n == 0) & (h == 0))
        def _():
            dalog_ref[...] = jnp.zeros_like(dalog_ref)
            ddtb_ref[...] = jnp.zeros_like(ddtb_ref)

        @pl.when(h == 0)
        def _():
            dab_ref[...] = jnp.zeros_like(dab_ref)

        t_saved = jnp.concatenate([t_ref[:, i] for i in range(DN_NH)], axis=0)

        def fwd(cq, ck, cv, ab_v, alog_v, dtb_v):
            gcum, gj, bb = _decay_terms(ab_v, alog_v, dtb_v, h * DN_NH, DN_NH)
            return _dn_prep(cq, ck, cv, gcum, gj, bb, t_saved)[:6]

        _, vjp = jax.vjp(fwd, _to_batch(cq_ref, DN_NH), _to_batch(ck_ref, DN_NH), _to_batch(cv_ref, DN_NH), ab_ref[...],
                         alog_ref[...], dtb_ref[...])
        cot = (_to_batch(du_ref, DN_NH), _to_batch(dw_ref, DN_NH),
               jnp.concatenate([dqk_ref[:, i] for i in range(DN_NH)], axis=0), _to_batch(dqd_ref, DN_NH),
               _to_batch(dkd_ref, DN_NH), jnp.concatenate([de_ref[:, i] for i in range(DN_NH)], axis=0))
        dcq, dck, dcv, dab, dalog, ddtb = vjp(cot)
        _from_batch(dcq_ref, dcq, DN_NH)
        _from_batch(dck_ref, dck, DN_NH)
        _from_batch(dcv_ref, dcv, DN_NH)
        dab_ref[...] += dab
        dalog_ref[...] += dalog
        ddtb_ref[...] += ddtb

    dabspec = pl.BlockSpec((DN_TB, AB_PAD), lambda n, h: (nb - 1 - n, 0))
    return pl.pallas_call(
        body, name=name, grid=(nb, ng),
        in_specs=[wide, wide, qks, wide, wide, egs, qks] + head + [ab, row, row],
        out_specs=[wide, wide, wide, dabspec, row, row],
        out_shape=[SDS((T, D), F32)] * 3 + [SDS((T, AB_PAD), F32)] + [SDS((1, HD), F32)] * 2,
        compiler_params=_params(2, VMEM_LIMIT))(du, dw, dqk, dqd, dkd, degl, t_mat, cpre, cpre, cpre, pm, alog, dtb)


def _scan_specs(T, rev):
    nb = T // DN_TB
    blk = (lambda n: nb - 1 - n) if rev else (lambda n: n)
    wide = pl.BlockSpec((DN_TB, D), lambda n: (blk(n), 0))
    qk = pl.BlockSpec((DN_NCH, 8, DN_C, DN_C), lambda n: (blk(n), 0, 0, 0))
    eg = pl.BlockSpec((DN_NCH, 8, 1, HD), lambda n: (blk(n), 0, 0, 0))
    st = pl.BlockSpec((DN_NCH, 8, HD, HD), lambda n: (blk(n), 0, 0, 0))
    row = pl.BlockSpec((1, HD), lambda n: (0, 0))
    return nb, wide, qk, eg, st, row


def _heads_of(ref, rows):
    return jnp.concatenate([ref[rows, h * HD:(h + 1) * HD].astype(F32)[None] for h in range(8)], axis=0)


def _dn_scan_fwd(u, w, qk, qd, kd, egl, onw, name):
    T = u.shape[0]
    nb, wide, qks, egs, sts, row = _scan_specs(T, False)

    def body(u_ref, w_ref, qk_ref, qd_ref, kd_ref, e_ref, onw_ref, o_ref, st_ref, s_scr):
        @pl.when(pl.program_id(0) == 0)
        def _():
            s_scr[...] = jnp.zeros_like(s_scr)
        S = s_scr[...]
        for c in range(DN_NCH):
            rows = slice(c * DN_C, (c + 1) * DN_C)
            st_ref[c] = S
            o, S = _dn_scan_step(_heads_of(u_ref, rows), _heads_of(w_ref, rows), qk_ref[c].astype(F32),
                                 _heads_of(qd_ref, rows), _heads_of(kd_ref, rows), e_ref[c], S, onw_ref[...])
            for h in range(8):
                o_ref[rows, h * HD:(h + 1) * HD] = o[h]
        s_scr[...] = S

    return pl.pallas_call(
        body, name=name, grid=(nb,), in_specs=[wide, wide, qks, wide, wide, egs, row], out_specs=[wide, sts],
        out_shape=[SDS((T, D), F32), SDS((T // DN_C, 8, HD, HD), F32)],
        scratch_shapes=[pltpu.VMEM((8, HD, HD), F32)],
        compiler_params=_params(1, VMEM_LIMIT))(u, w, qk, qd, kd, egl, onw)


def _dn_scan_bwd(do, u, w, qk, qd, kd, egl, st, onw, name):
    T = u.shape[0]
    nb, wide, qks, egs, sts, row = _scan_specs(T, True)

    def body(do_ref, u_ref, w_ref, qk_ref, qd_ref, kd_ref, e_ref, st_ref, onw_ref,
             du_ref, dw_ref, dqk_ref, dqd_ref, dkd_ref, de_ref, donw_ref, ds_scr):
        @pl.when(pl.program_id(0) == 0)
        def _():
            ds_scr[...] = jnp.zeros_like(ds_scr)
            donw_ref[...] = jnp.zeros_like(donw_ref)
        dS = ds_scr[...]
        donw = jnp.zeros((1, HD), F32)
        for c in reversed(range(DN_NCH)):
            rows = slice(c * DN_C, (c + 1) * DN_C)
            _, vjp = jax.vjp(_dn_scan_step, _heads_of(u_ref, rows), _heads_of(w_ref, rows), qk_ref[c].astype(F32),
                             _heads_of(qd_ref, rows), _heads_of(kd_ref, rows), e_ref[c], st_ref[c], onw_ref[...])
            du, dw, dqk, dqd, dkd, de, dS, dn = vjp((_heads_of(do_ref, rows), dS))
            for h in range(8):
                cols = slice(h * HD, (h + 1) * HD)
                du_ref[rows, cols] = du[h]
                dw_ref[rows, cols] = dw[h]
                dqd_ref[rows, cols] = dqd[h]
                dkd_ref[rows, cols] = dkd[h]
            dqk_ref[c] = dqk
            de_ref[c] = de
            donw += dn
        ds_scr[...] = dS
        donw_ref[...] += donw

    return pl.pallas_call(
        body, name=name, grid=(nb,), in_specs=[wide, wide, wide, qks, wide, wide, egs, sts, row],
        out_specs=[wide, wide, qks, wide, wide, egs, row],
        out_shape=[SDS((T, D), F32), SDS((T, D), F32), SDS((T // DN_C, 8, DN_C, DN_C), F32), SDS((T, D), F32),
                   SDS((T, D), F32), SDS((T // DN_C, 8, 1, HD), F32), SDS((1, HD), F32)],
        scratch_shapes=[pltpu.VMEM((8, HD, HD), F32)],
        compiler_params=_params(1, VMEM_LIMIT))(do, u, w, qk, qd, kd, egl, st, onw)


def _adamw(w, g, m, v, name):
    R, C = w.shape
    tr = 256 if R % 256 == 0 and R > 256 else R
    c1 = 1.0 - ADAM_B1 ** ADAM_STEP
    c2 = 1.0 - ADAM_B2 ** ADAM_STEP

    def body(w_ref, g_ref, m_ref, v_ref, d_ref, nm_ref, nv_ref):
        gv = g_ref[...]
        nm = ADAM_B1 * m_ref[...] + (1.0 - ADAM_B1) * gv
        nv = ADAM_B2 * v_ref[...] + (1.0 - ADAM_B2) * (gv * gv)
        nm_ref[...] = nm
        nv_ref[...] = nv
        d_ref[...] = -ADAM_LR * ((nm / c1) / (jnp.sqrt(nv / c2) + ADAM_EPS) + ADAM_WD * w_ref[...])

    spec = pl.BlockSpec((tr, C), lambda i: (i, 0))
    return pl.pallas_call(
        body, name=name, grid=(R // tr,), in_specs=[spec] * 4, out_specs=[spec] * 3,
        out_shape=[SDS((R, C), F32)] * 3, compiler_params=_params(1, VMEM_LIMIT))(w, g, m, v)


def _local_step(x, mem, target, wts, sm):
    kinds = [i % 3 for i in range(DEPTH)]
    mnw = sm["mem_norm_w"].reshape(1, D)
    kv = _memkv_fwd(mem, mnw, wts["wkv"])
    saved = []
    for i, kind in enumerate(kinds):
        j = i // 3
        npre = sm["norm_pre"][i].reshape(1, D)
        npost = sm["norm_post"][i].reshape(1, D)
        pm, pg, h = _inproj_fwd(x, npre, wts["mix"][i], wts["gate"][i], f"inproj_fwd_{i}")
        extra = None
        if kind == 0:
            bs3 = jnp.broadcast_to(sm["a_b_s"][j][:, :, None], (8, HD, HD))
            ymix = _gmlp_fwd(pm, sm["a_ln_w"][j].reshape(1, D), sm["a_ln_b"][j].reshape(1, D), sm["a_w_s"][j], bs3,
                             f"gmlp_fwd_{i}")
            extra = bs3
        elif kind == 1:
            ymix = _sconv_fwd(pm, sm["b_conv_w"][j], f"sconv_fwd_{i}")
        else:
            cpre = _dnconv_fwd(pm, sm["c_conv_w"][j], f"dnconv_fwd_{i}")
            alog = jnp.pad(sm["c_a_log"][j], (0, HD - 8)).reshape(1, HD)
            dtb = jnp.pad(sm["c_dt_bias"][j], (0, HD - 8)).reshape(1, HD)
            onw = sm["c_o_norm_w"][j].reshape(1, HD)
            *prep, t_mat = _dn_prep_fwd(cpre, pm, alog, dtb, f"dn_prep_fwd_{i}")
            ymix, st = _dn_scan_fwd(*prep, onw, f"dn_scan_fwd_{i}")
            extra = (cpre, prep, t_mat, st, alog, dtb, onw)
        ycat = _ag_fwd(ymix, pg, kv, f"ag_fwd_{i}")
        o, xn = _outproj_fwd(ycat, wts["wo"][i], x, npost, f"outproj_fwd_{i}")
        saved.append((x, h, pm, pg, ymix, ycat, o, extra))
        x = xn

    loss, dx = _loss_head(x, target)

    g = {"wm": [None] * DEPTH, "wg": [None] * DEPTH, "wo": [None] * DEPTH, "norm_pre": [None] * DEPTH,
         "norm_post": [None] * DEPTH}
    dkv = jnp.zeros((N_MEM, 2 * D_XA), F32)
    for i in reversed(range(DEPTH)):
        kind, j = kinds[i], i // 3
        xi, h, pm, pg, ymix, ycat, o, extra = saved[i]
        npre = sm["norm_pre"][i].reshape(1, D)
        npost = sm["norm_post"][i].reshape(1, D)
        dycat, dobf, g["norm_post"][i] = _outproj_bwd(dx, o, npost, wts["wo"][i], f"outproj_bwd_{i}")
        g["wo"][i] = _matmul_tn(ycat, dobf, f"dwo_{i}")
        dymix, dpg, dkv = _ag_bwd(dycat, ymix, pg, kv, dkv, f"ag_bwd_{i}")
        if kind == 0:
            dpm, dlnw, dlnb, dws, dbs3 = _gmlp_bwd(dymix, pm, sm["a_ln_w"][j].reshape(1, D),
                                                   sm["a_ln_b"][j].reshape(1, D), sm["a_w_s"][j], extra,
                                                   f"gmlp_bwd_{i}")
            g.setdefault("a_ln_w", {})[j] = dlnw.reshape(D)
            g.setdefault("a_ln_b", {})[j] = dlnb.reshape(D)
            g.setdefault("a_w_s", {})[j] = dws
            g.setdefault("a_b_s", {})[j] = dbs3[:, :, 0]
        elif kind == 1:
            dpm, dcw = _sconv_bwd(dymix, pm, sm["b_conv_w"][j], f"sconv_bwd_{i}")
            g.setdefault("b_conv_w", {})[j] = dcw
        else:
            cpre, prep, t_mat, st, alog, dtb, onw = extra
            *dprep, donw = _dn_scan_bwd(dymix, *prep, st, onw, f"dn_scan_bwd_{i}")
            dcq, dck, dcv, dab, dalog, ddtb = _dn_prep_bwd(*dprep, t_mat, cpre, pm, alog, dtb, f"dn_prep_bwd_{i}")
            dpm, dcw = _dnconv_bwd(dcq, dck, dcv, dab, pm, sm["c_conv_w"][j], f"dnconv_bwd_{i}")
            g.setdefault("c_conv_w", {})[j] = dcw
            g.setdefault("c_a_log", {})[j] = dalog[0, :8]
            g.setdefault("c_dt_bias", {})[j] = ddtb[0, :8]
            g.setdefault("c_o_norm_w", {})[j] = donw[0]
        sub = GRAD_TILE[kind]
        g["wm"][i] = _matmul_tn(h, dpm, f"dwm_{i}", sub)
        g["wg"][i] = _matmul_tn(h, dpg, f"dwg_{i}", sub)
        dx, g["norm_pre"][i] = _inproj_bwd(dpm, dpg, xi, npre, wts["mix"][i], wts["gate"][i], dx, f"inproj_bwd_{i}")
    g["mem_norm_w"], g["wkv"] = _memkv_bwd(mem, mnw, wts["wkv"], dkv)
    return loss[0, 0], dx, g


ANY = pl.BlockSpec(memory_space=pl.ANY)


def _place():
    return lax.axis_index("x"), lax.axis_index("y"), lax.axis_index("c")


def _gather_weights(big, vec):
    def body(big_ref, vec_ref, ob_ref, ov_ref, ici_send, ici_recv, d2d_send, d2d_recv, vec_send, vec_recv):
        x, y, c = _place()
        chip = 2 * x + y
        peers = [(1 - x, y), (x, 1 - y), (1 - x, 1 - y)]

        def rows(half, k):
            return pl.ds(half * HALF_ROWS + k * CHUNK_ROWS, CHUNK_ROWS)

        def over_ici(j, k, slab):
            px, py = peers[j]
            i = j * N_CHUNKS + k
            return pltpu.make_async_remote_copy(
                src_ref=big_ref.at[rows(c, k)], dst_ref=ob_ref.at[slab, rows(c, k)], send_sem=ici_send.at[i],
                recv_sem=ici_recv.at[i], device_id=(px, py, c), device_id_type=MESH)

        def over_d2d(j, k, half):
            px, py = peers[j]
            i = j * N_CHUNKS + k
            where = ob_ref.at[2 * px + py, rows(half, k)]
            return pltpu.make_async_remote_copy(
                src_ref=where, dst_ref=where, send_sem=d2d_send.at[i], recv_sem=d2d_recv.at[i],
                device_id=(x, y, 1 - c), device_id_type=MESH)

        def small(j, slab):
            px, py = peers[j]
            return pltpu.make_async_remote_copy(
                src_ref=vec_ref, dst_ref=ov_ref.at[slab], send_sem=vec_send.at[j], recv_sem=vec_recv.at[j],
                device_id=(px, py, c), device_id_type=MESH)

        sends = [small(j, chip) for j in range(3)] + [over_ici(j, k, chip) for k in range(N_CHUNKS) for j in range(3)]
        for cp in sends:
            cp.start()
        forwards = []
        for k in range(N_CHUNKS):
            for j, (px, py) in enumerate(peers):
                over_ici(j, k, 2 * px + py).wait_recv()
                forwards.append(over_d2d(j, k, c))
                forwards[-1].start()
        for k in range(N_CHUNKS):
            for j in range(3):
                over_d2d(j, k, 1 - c).wait_recv()
        for j, (px, py) in enumerate(peers):
            small(j, 2 * px + py).wait_recv()
        for cp in sends + forwards:
            cp.wait_send()

    n = 3 * N_CHUNKS
    dma = pltpu.SemaphoreType.DMA
    return pl.pallas_call(
        body, name="gather_weights", in_specs=[ANY, ANY], out_specs=[ANY, ANY],
        out_shape=[SDS((4,) + big.shape, big.dtype), SDS((4,) + vec.shape, vec.dtype)],
        scratch_shapes=[dma((n,)), dma((n,)), dma((n,)), dma((n,)), dma((3,)), dma((3,))])(big, vec)


HALF_ROWS = 3328
CHUNK_ROWS = 256
N_CHUNKS = HALF_ROWS // CHUNK_ROWS


def _swap_halves(gbig, small):
    def body(g_ref, s_ref, ob_ref, os_ref, send_sems, recv_sems):
        x, y, c = _place()
        copies = []
        for s in range(4):
            for k in range(N_CHUNKS):
                rows = pl.ds(k * CHUNK_ROWS, CHUNK_ROWS)
                copies.append(pltpu.make_async_remote_copy(
                    src_ref=g_ref.at[s, 1 - c, rows], dst_ref=ob_ref.at[s, rows], send_sem=send_sems.at[len(copies)],
                    recv_sem=recv_sems.at[len(copies)], device_id=(x, y, 1 - c), device_id_type=MESH))
        copies.append(pltpu.make_async_remote_copy(
            src_ref=s_ref, dst_ref=os_ref, send_sem=send_sems.at[len(copies)], recv_sem=recv_sems.at[len(copies)],
            device_id=(x, y, 1 - c), device_id_type=MESH))
        for cp in copies:
            cp.start()
        for cp in copies:
            cp.wait_recv()
        for cp in copies:
            cp.wait_send()

    n = 4 * N_CHUNKS + 1
    return pl.pallas_call(
        body, name="swap_halves", in_specs=[ANY, ANY], out_specs=[ANY, ANY],
        out_shape=[SDS((4, HALF_ROWS, D), F32), SDS(small.shape, F32)],
        scratch_shapes=[pltpu.SemaphoreType.DMA((n,)), pltpu.SemaphoreType.DMA((n,))])(gbig, small)


def _pair_sum(gbig, other):
    def body(g_ref, o_ref, pb_ref, own_ref):
        x, y, c = _place()
        v = jnp.where(c == 0, g_ref[0], g_ref[1]) + o_ref[...]
        pb_ref[...] = v.astype(BF16)

        @pl.when(pl.program_id(1) == 2 * x + y)
        def _():
            own_ref[...] = v

    return pl.pallas_call(
        body, name="pair_sum", grid=(N_CHUNKS, 4),
        in_specs=[pl.BlockSpec((None, 2, CHUNK_ROWS, D), lambda i, s: (s, 0, i, 0)),
                  pl.BlockSpec((None, CHUNK_ROWS, D), lambda i, s: (s, i, 0))],
        out_specs=[pl.BlockSpec((None, CHUNK_ROWS, D), lambda i, s: (s, i, 0)),
                   pl.BlockSpec((CHUNK_ROWS, D), lambda i, s: (i, 0))],
        out_shape=[SDS((4, HALF_ROWS, D), BF16), SDS((HALF_ROWS, D), F32)],
        compiler_params=_params(2, VMEM_LIMIT))(gbig, other)


def _add(a, b, name):
    def body(a_ref, b_ref, o_ref):
        o_ref[...] = a_ref[...] + b_ref[...]

    return pl.pallas_call(body, name=name, out_shape=SDS(a.shape, a.dtype), compiler_params=_params(0, VMEM_LIMIT))(a, b)


def _chip_exchange(pb, ps):
    n_small = ps.shape[0]

    def body(pb_ref, ps_ref, lb_ref, ls_ref, send_sems, recv_sems):
        x, y, c = _place()
        chip = 2 * x + y
        peers = [(1 - x, y), (x, 1 - y), (1 - x, 1 - y)]

        def copies(slab_of):
            out = []
            for j, (px, py) in enumerate(peers):
                for k in range(N_CHUNKS):
                    rows = pl.ds(k * CHUNK_ROWS, CHUNK_ROWS)
                    out.append(pltpu.make_async_remote_copy(
                        src_ref=pb_ref.at[2 * px + py, rows], dst_ref=lb_ref.at[slab_of(j), rows],
                        send_sem=send_sems.at[len(out)], recv_sem=recv_sems.at[len(out)], device_id=(px, py, c),
                        device_id_type=MESH))
                out.append(pltpu.make_async_remote_copy(
                    src_ref=ps_ref, dst_ref=ls_ref.at[slab_of(j)], send_sem=send_sems.at[len(out)],
                    recv_sem=recv_sems.at[len(out)], device_id=(px, py, c), device_id_type=MESH))
            return out

        sends = copies(lambda j: chip)
        for cp in sends:
            cp.start()
        for cp in copies(lambda j: 2 * peers[j][0] + peers[j][1]):
            cp.wait_recv()
        for cp in sends:
            cp.wait_send()

    n = 3 * (N_CHUNKS + 1)
    return pl.pallas_call(
        body, name="chip_exchange", in_specs=[ANY, ANY], out_specs=[ANY, ANY],
        out_shape=[SDS((4, HALF_ROWS, D), BF16), SDS((4, n_small, D), F32)],
        scratch_shapes=[pltpu.SemaphoreType.DMA((n,)), pltpu.SemaphoreType.DMA((n,))])(pb, ps)


def _chip_sum(own, land):
    def body(own_ref, l_ref, o_ref):
        chip = 2 * lax.axis_index("x") + lax.axis_index("y")
        acc = jnp.where(chip == 0, own_ref[...], l_ref[0].astype(F32))
        for s in range(1, 4):
            acc = acc + jnp.where(chip == s, own_ref[...], l_ref[s].astype(F32))
        o_ref[...] = acc

    return pl.pallas_call(
        body, name="chip_sum", grid=(N_CHUNKS,),
        in_specs=[pl.BlockSpec((CHUNK_ROWS, D), lambda i: (i, 0)), pl.BlockSpec((4, CHUNK_ROWS, D), lambda i: (0, i, 0))],
        out_specs=pl.BlockSpec((CHUNK_ROWS, D), lambda i: (i, 0)), out_shape=SDS((HALF_ROWS, D), F32),
        compiler_params=_params(1, VMEM_LIMIT))(own, land)


def _sum4(own, land):
    def body(own_ref, l_ref, o_ref):
        chip = 2 * lax.axis_index("x") + lax.axis_index("y")
        acc = jnp.where(chip == 0, own_ref[...], l_ref[0])
        for s in range(1, 4):
            acc = acc + jnp.where(chip == s, own_ref[...], l_ref[s])
        o_ref[...] = acc

    return pl.pallas_call(body, name="sum_small", out_shape=SDS(own.shape, own.dtype),
                          compiler_params=_params(0, VMEM_LIMIT))(own, land)


def _share_half(r):
    def body(r_ref, o_ref, send_sems, recv_sems):
        x, y, c = _place()
        copies = [pltpu.make_async_remote_copy(
            src_ref=r_ref.at[pl.ds(k * CHUNK_ROWS, CHUNK_ROWS)], dst_ref=o_ref.at[pl.ds(k * CHUNK_ROWS, CHUNK_ROWS)],
            send_sem=send_sems.at[k], recv_sem=recv_sems.at[k], device_id=(x, y, 1 - c), device_id_type=MESH)
            for k in range(N_CHUNKS)]
        for cp in copies:
            cp.start()
        for cp in copies:
            cp.wait_recv()
        for cp in copies:
            cp.wait_send()

    return pl.pallas_call(
        body, name="share_half", in_specs=[ANY], out_specs=ANY, out_shape=SDS((HALF_ROWS, D), F32),
        scratch_shapes=[pltpu.SemaphoreType.DMA((N_CHUNKS,)), pltpu.SemaphoreType.DMA((N_CHUNKS,))])(r)


CLASSES = {"a": (1024, 256), "b": (2560, 640), "c": (512, 256), "o": (768, 256), "kv": (128, 128)}
W_CLASSES = {"a": (1024, 256), "b": (512, 256), "c": (512, 256), "o": (768, 256), "kv": (128, 128)}


def _chunk_list(specs):
    return [(k, r, chunk) for k, (half, chunk) in enumerate(specs) for r in range(0, half, chunk)]


def _gather_classes(arrs, specs, vec):
    n = len(arrs)
    chunks = _chunk_list(specs)
    nc = len(chunks)

    def body(*refs):
        ins, vec_ref, outs, ov_ref = refs[:n], refs[n], refs[n + 1:2 * n + 1], refs[2 * n + 1]
        ici_send, ici_recv, d2d_send, d2d_recv, vec_send, vec_recv = refs[2 * n + 2:]
        x, y, c = _place()
        chip = 2 * x + y
        peers = [(1 - x, y), (x, 1 - y), (1 - x, 1 - y)]

        def rows(ci, half):
            k, r, cnt = chunks[ci]
            return k, pl.ds(half * specs[k][0] + r, cnt)

        def over_ici(j, ci, slab):
            px, py = peers[j]
            k, rs = rows(ci, c)
            return pltpu.make_async_remote_copy(
                src_ref=ins[k].at[rs], dst_ref=outs[k].at[slab, rs], send_sem=ici_send.at[j * nc + ci],
                recv_sem=ici_recv.at[j * nc + ci], device_id=(px, py, c), device_id_type=MESH)

        def over_d2d(j, ci, half):
            px, py = peers[j]
            k, rs = rows(ci, half)
            where = outs[k].at[2 * px + py, rs]
            return pltpu.make_async_remote_copy(
                src_ref=where, dst_ref=where, send_sem=d2d_send.at[j * nc + ci], recv_sem=d2d_recv.at[j * nc + ci],
                device_id=(x, y, 1 - c), device_id_type=MESH)

        def small(j, slab):
            px, py = peers[j]
            return pltpu.make_async_remote_copy(
                src_ref=vec_ref, dst_ref=ov_ref.at[slab], send_sem=vec_send.at[j], recv_sem=vec_recv.at[j],
                device_id=(px, py, c), device_id_type=MESH)

        sends = [small(j, chip) for j in range(3)] + [over_ici(j, ci, chip) for ci in range(nc) for j in range(3)]
        for cp in sends:
            cp.start()
        forwards = []
        for ci in range(nc):
            for j, (px, py) in enumerate(peers):
                over_ici(j, ci, 2 * px + py).wait_recv()
                forwards.append(over_d2d(j, ci, c))
                forwards[-1].start()
        for ci in range(nc):
            for j in range(3):
                over_d2d(j, ci, 1 - c).wait_recv()
        for j, (px, py) in enumerate(peers):
            small(j, 2 * px + py).wait_recv()
        for cp in sends + forwards:
            cp.wait_send()

    dma = pltpu.SemaphoreType.DMA
    return pl.pallas_call(
        body, name="gather_weights", in_specs=[ANY] * (n + 1), out_specs=[ANY] * (n + 1),
        out_shape=[SDS((4,) + a.shape, a.dtype) for a in arrs] + [SDS((4,) + vec.shape, vec.dtype)],
        scratch_shapes=[dma((3 * nc,)), dma((3 * nc,)), dma((3 * nc,)), dma((3 * nc,)), dma((3,)), dma((3,))])(*arrs, vec)


def _swap_classes(grads, specs, small):
    n = len(grads)
    chunks = _chunk_list(specs)

    def body(*refs):
        ins, s_ref, outs, os_ref, send_sems, recv_sems = refs[:n], refs[n], refs[n + 1:2 * n + 1], *refs[2 * n + 1:]
        x, y, c = _place()
        copies = []
        for s in range(4):
            for k, r, cnt in chunks:
                copies.append(pltpu.make_async_remote_copy(
                    src_ref=ins[k].at[s, 1 - c, pl.ds(r, cnt)], dst_ref=outs[k].at[s, pl.ds(r, cnt)],
                    send_sem=send_sems.at[len(copies)], recv_sem=recv_sems.at[len(copies)],
                    device_id=(x, y, 1 - c), device_id_type=MESH))
        copies.append(pltpu.make_async_remote_copy(
            src_ref=s_ref, dst_ref=os_ref, send_sem=send_sems.at[len(copies)], recv_sem=recv_sems.at[len(copies)],
            device_id=(x, y, 1 - c), device_id_type=MESH))
        for cp in copies:
            cp.start()
        for cp in copies:
            cp.wait_recv()
        for cp in copies:
            cp.wait_send()

    m = 4 * len(chunks) + 1
    return pl.pallas_call(
        body, name="swap_halves", in_specs=[ANY] * (n + 1), out_specs=[ANY] * (n + 1),
        out_shape=[SDS((4, g.shape[2], g.shape[3]), F32) for g in grads] + [SDS(small.shape, F32)],
        scratch_shapes=[pltpu.SemaphoreType.DMA((m,)), pltpu.SemaphoreType.DMA((m,))])(*grads, small)


def _pair_sum_class(g, other, chunk, name):
    _, _, half, w = g.shape

    def body(g_ref, o_ref, pb_ref, own_ref):
        x, y, c = _place()
        v = jnp.where(c == 0, g_ref[0], g_ref[1]) + o_ref[...]
        pb_ref[...] = v.astype(BF16)

        @pl.when(pl.program_id(1) == 2 * x + y)
        def _():
            own_ref[...] = v

    return pl.pallas_call(
        body, name=name, grid=(half // chunk, 4),
        in_specs=[pl.BlockSpec((None, 2, chunk, w), lambda i, s: (s, 0, i, 0)),
                  pl.BlockSpec((None, chunk, w), lambda i, s: (s, i, 0))],
        out_specs=[pl.BlockSpec((None, chunk, w), lambda i, s: (s, i, 0)), pl.BlockSpec((chunk, w), lambda i, s: (i, 0))],
        out_shape=[SDS((4, half, w), BF16), SDS((half, w), F32)],
        compiler_params=_params(2, VMEM_LIMIT))(g, other)


def _exchange_classes(pbs, specs, ps):
    n = len(pbs)
    chunks = _chunk_list(specs)
    per_peer = len(chunks) + 1

    def body(*refs):
        ins, ps_ref, outs, ls_ref, send_sems, recv_sems = refs[:n], refs[n], refs[n + 1:2 * n + 1], *refs[2 * n + 1:]
        x, y, c = _place()
        chip = 2 * x + y
        peers = [(1 - x, y), (x, 1 - y), (1 - x, 1 - y)]

        def copies(slab_of):
            out = []
            for j, (px, py) in enumerate(peers):
                for k, r, cnt in chunks:
                    out.append(pltpu.make_async_remote_copy(
                        src_ref=ins[k].at[2 * px + py, pl.ds(r, cnt)], dst_ref=outs[k].at[slab_of(j), pl.ds(r, cnt)],
                        send_sem=send_sems.at[len(out)], recv_sem=recv_sems.at[len(out)], device_id=(px, py, c),
                        device_id_type=MESH))
                out.append(pltpu.make_async_remote_copy(
                    src_ref=ps_ref, dst_ref=ls_ref.at[slab_of(j)], send_sem=send_sems.at[len(out)],
                    recv_sem=recv_sems.at[len(out)], device_id=(px, py, c), device_id_type=MESH))
            return out

        sends = copies(lambda j: chip)
        for cp in sends:
            cp.start()
        for cp in copies(lambda j: 2 * peers[j][0] + peers[j][1]):
            cp.wait_recv()
        for cp in sends:
            cp.wait_send()

    m = 3 * per_peer
    return pl.pallas_call(
        body, name="chip_exchange", in_specs=[ANY] * (n + 1), out_specs=[ANY] * (n + 1),
        out_shape=[SDS(p.shape, BF16) for p in pbs] + [SDS((4,) + ps.shape, F32)],
        scratch_shapes=[pltpu.SemaphoreType.DMA((m,)), pltpu.SemaphoreType.DMA((m,))])(*pbs, ps)


def _chip_sum_class(own, land, chunk, name):
    half, w = own.shape

    def body(own_ref, l_ref, o_ref):
        chip = 2 * lax.axis_index("x") + lax.axis_index("y")
        acc = jnp.where(chip == 0, own_ref[...], l_ref[0].astype(F32))
        for s in range(1, 4):
            acc = acc + jnp.where(chip == s, own_ref[...], l_ref[s].astype(F32))
        o_ref[...] = acc

    return pl.pallas_call(
        body, name=name, grid=(half // chunk,),
        in_specs=[pl.BlockSpec((chunk, w), lambda i: (i, 0)), pl.BlockSpec((4, chunk, w), lambda i: (0, i, 0))],
        out_specs=pl.BlockSpec((chunk, w), lambda i: (i, 0)), out_shape=SDS((half, w), F32),
        compiler_params=_params(1, VMEM_LIMIT))(own, land)


def _share_classes(rs, specs):
    n = len(rs)
    chunks = _chunk_list(specs)

    def body(*refs):
        ins, outs, send_sems, recv_sems = refs[:n], refs[n:2 * n], *refs[2 * n:]
        x, y, c = _place()
        copies = [pltpu.make_async_remote_copy(
            src_ref=ins[k].at[pl.ds(r, cnt)], dst_ref=outs[k].at[pl.ds(r, cnt)], send_sem=send_sems.at[i],
            recv_sem=recv_sems.at[i], device_id=(x, y, 1 - c), device_id_type=MESH)
            for i, (k, r, cnt) in enumerate(chunks)]
        for cp in copies:
            cp.start()
        for cp in copies:
            cp.wait_recv()
        for cp in copies:
            cp.wait_send()

    m = len(chunks)
    return pl.pallas_call(
        body, name="share_half", in_specs=[ANY] * n, out_specs=[ANY] * n, out_shape=[SDS(r.shape, F32) for r in rs],
        scratch_shapes=[pltpu.SemaphoreType.DMA((m,)), pltpu.SemaphoreType.DMA((m,))])(*rs)


_SMALL = ["mem_norm_w", "norm_pre", "norm_post", "a_ln_w", "a_ln_b", "a_w_s", "a_b_s", "b_conv_w", "c_conv_w",
          "c_a_log", "c_dt_bias", "c_o_norm_w"]
_SMALL_SHAPES = {"mem_norm_w": (D,), "norm_pre": (4, D), "norm_post": (4, D), "a_ln_w": (2, D), "a_ln_b": (2, D),
                 "a_w_s": (2, 8, HD, HD), "a_b_s": (2, 8, HD), "b_conv_w": (1, 3, D), "c_conv_w": (1, 4, 3 * D),
                 "c_a_log": (1, 8), "c_dt_bias": (1, 8), "c_o_norm_w": (1, HD)}
_SHARDED_SMALL = {"a_ln_w": D // 4, "a_ln_b": D // 4, "b_conv_w": D // 4, "c_conv_w": 3 * D // 4}
_ROWS = [2048, 1280, 1284, 1536, 256]
_BIG_ROWS = sum(_ROWS)
_BIG_PAD = 6416
_SMALL_ROWS = 288


def _size(shape):
    n = 1
    for d in shape:
        n *= d
    return n


def kernel(x, mem, mem_norm_w, w_mem_kv, norm_pre, norm_post, w_out, a_w_in, a_ln_w, a_ln_b, a_w_s, a_b_s, b_w_in, b_conv_w, c_w_in, c_conv_w, c_a_log, c_dt_bias, c_o_norm_w, loss_target, m_mem_norm_w, m_w_mem_kv, m_norm_pre, m_norm_post, m_w_out, m_a_w_in, m_a_ln_w, m_a_ln_b, m_a_w_s, m_a_b_s, m_b_w_in, m_b_conv_w, m_c_w_in, m_c_conv_w, m_c_a_log, m_c_dt_bias, m_c_o_norm_w, v_mem_norm_w, v_w_mem_kv, v_norm_pre, v_norm_post, v_w_out, v_a_w_in, v_a_ln_w, v_a_ln_b, v_a_w_s, v_a_b_s, v_b_w_in, v_b_conv_w, v_c_w_in, v_c_conv_w, v_c_a_log, v_c_dt_bias, v_c_o_norm_w):
    names = ["mem_norm_w", "w_mem_kv", "norm_pre", "norm_post", "w_out", "a_w_in", "a_ln_w", "a_ln_b", "a_w_s", "a_b_s",
             "b_w_in", "b_conv_w", "c_w_in", "c_conv_w", "c_a_log", "c_dt_bias", "c_o_norm_w"]
    w = dict(zip(names, [mem_norm_w, w_mem_kv, norm_pre, norm_post, w_out, a_w_in, a_ln_w, a_ln_b, a_w_s, a_b_s, b_w_in,
                         b_conv_w, c_w_in, c_conv_w, c_a_log, c_dt_bias, c_o_norm_w]))
    m = dict(zip(names, [m_mem_norm_w, m_w_mem_kv, m_norm_pre, m_norm_post, m_w_out, m_a_w_in, m_a_ln_w, m_a_ln_b, m_a_w_s,
                         m_a_b_s, m_b_w_in, m_b_conv_w, m_c_w_in, m_c_conv_w, m_c_a_log, m_c_dt_bias, m_c_o_norm_w]))
    v = dict(zip(names, [v_mem_norm_w, v_w_mem_kv, v_norm_pre, v_norm_post, v_w_out, v_a_w_in, v_a_ln_w, v_a_ln_b, v_a_w_s,
                         v_a_b_s, v_b_w_in, v_b_conv_w, v_c_w_in, v_c_conv_w, v_c_a_log, v_c_dt_bias, v_c_o_norm_w]))
    chip = 2 * lax.axis_index("x") + lax.axis_index("y")

    mine_w = [a_w_in.reshape(2 * D, D), b_w_in.reshape(D, 1280), c_w_in.reshape(D, 1284), w_out.reshape(D_CAT, D),
              w_mem_kv]
    mine_w = [a.astype(BF16) for a in mine_w]
    vec = jnp.concatenate([a_ln_w.reshape(-1), a_ln_b.reshape(-1), b_conv_w.reshape(-1), c_conv_w.reshape(-1)])
    vec = jnp.pad(vec, (0, 8 * D - vec.shape[0])).reshape(8, D)
    *gathered, gvec = _gather_classes(mine_w, [W_CLASSES[k] for k in W_CLASSES], vec)
    ga, gb, gc, go, gkv = [lax.dynamic_update_slice(g_, a[None], (chip, 0, 0)) for g_, a in zip(gathered, mine_w)]
    gvec = lax.dynamic_update_slice(gvec, vec[None], (chip, 0, 0))
    ga = ga.reshape(4, 2, D, D)
    go = go.reshape(4, 4, 384, D)
    fc = gc.transpose(1, 0, 2).reshape(D, 5136)
    gv = gvec.reshape(4, 8 * D)
    sm = {"mem_norm_w": mem_norm_w, "norm_pre": norm_pre, "norm_post": norm_post, "a_w_s": a_w_s, "a_b_s": a_b_s,
          "c_a_log": c_a_log, "c_dt_bias": c_dt_bias, "c_o_norm_w": c_o_norm_w,
          "a_ln_w": gv[:, 0:512].reshape(4, 2, 256).transpose(1, 0, 2).reshape(2, D),
          "a_ln_b": gv[:, 512:1024].reshape(4, 2, 256).transpose(1, 0, 2).reshape(2, D),
          "b_conv_w": gv[:, 1024:1792].reshape(4, 1, 3, 256).transpose(1, 2, 0, 3).reshape(1, 3, D),
          "c_conv_w": gv[:, 1792:4864].reshape(4, 1, 4, 768).transpose(1, 2, 0, 3).reshape(1, 4, 3 * D)}
    c_mix = jnp.concatenate([fc[:, :3 * D + 16], jnp.zeros((D, AB_PAD - 16), BF16)], axis=1)
    wts = {"wkv": gkv.reshape(D, 2 * D_XA), "wo": [go[:, i].reshape(D_CAT, D) for i in range(DEPTH)],
           "mix": [[ga[0, 0], ga[1, 0]], [gb[0], gb[1], gb[2][:, :512]], [c_mix], [ga[0, 1], ga[1, 1]]],
           "gate": [[ga[2, 0], ga[3, 0]], [gb[2][:, 512:], gb[3]], [fc[:, 3 * D + 16:]], [ga[2, 1], ga[3, 1]]]}

    loss, dx, g = _local_step(x[0], mem[0], loss_target[0], wts, sm)
    loss = lax.psum(loss, ("x", "y", "c"))

    gc = jnp.concatenate([g["wm"][2][:, :3 * D + 16], g["wg"][2]], axis=1)
    by_chip = {
        "a": jnp.stack([jnp.concatenate([g["wm"][i], g["wg"][i]], axis=0) for i in (0, 3)], axis=1),
        "b": jnp.concatenate([g["wm"][1], g["wg"][1]], axis=0),
        "c": gc.reshape(D, 4, 1284).transpose(1, 0, 2),
        "o": jnp.stack(g["wo"]).reshape(4, 4, 384, D).transpose(1, 0, 2, 3),
        "kv": g["wkv"]}
    specs = [CLASSES[k] for k in CLASSES]
    halves = [by_chip[k].reshape(4, 2, CLASSES[k][0], -1) for k in CLASSES]
    gs = {"mem_norm_w": g["mem_norm_w"], "norm_pre": jnp.concatenate(g["norm_pre"]),
          "norm_post": jnp.concatenate(g["norm_post"])}
    for n in _SMALL[3:]:
        gs[n] = jnp.stack([g[n][j] for j in sorted(g[n])])
    flat = jnp.concatenate([gs[n].reshape(-1) for n in _SMALL])
    small = jnp.pad(flat, (0, _SMALL_ROWS * D - flat.shape[0])).reshape(_SMALL_ROWS, D)
    *others, other_small = _swap_classes(halves, specs, small)
    pairs = [_pair_sum_class(h, o, CLASSES[k][1], f"pair_sum_{k}") for k, h, o in zip(CLASSES, halves, others)]
    pair_small = _add(small, other_small, "pair_sum_small")
    *lands, land_small = _exchange_classes([p[0] for p in pairs], specs, pair_small)
    mine = [_chip_sum_class(p[1], land, CLASSES[k][1], f"chip_sum_{k}") for k, p, land in zip(CLASSES, pairs, lands)]
    theirs = _share_classes(mine, specs)
    south = lax.axis_index("c") == 0
    sh = {k: jnp.concatenate([jnp.where(south, a, b), jnp.where(south, b, a)], axis=0)
          for k, a, b in zip(CLASSES, mine, theirs)}
    grads = {"a_w_in": sh["a"].reshape(a_w_in.shape),
             "b_w_in": sh["b"].reshape(5, D, 256).transpose(1, 0, 2).reshape(b_w_in.shape),
             "c_w_in": sh["c"].reshape(c_w_in.shape), "w_out": sh["o"].reshape(w_out.shape), "w_mem_kv": sh["kv"]}
    flat = _sum4(pair_small, land_small).reshape(-1)
    off = 0
    for n in _SMALL:
        shape = _SMALL_SHAPES[n]
        full = flat[off:off + _size(shape)].reshape(shape)
        off += _size(shape)
        if n in _SHARDED_SMALL:
            full = lax.dynamic_slice_in_dim(full, chip * _SHARDED_SMALL[n], _SHARDED_SMALL[n], axis=len(shape) - 1)
        grads[n] = full

    delta, new_m, new_v = {}, {}, {}
    for n in names:
        shape = w[n].shape
        view = (1, shape[0]) if len(shape) == 1 else (_size(shape[:-1]), shape[-1])
        d_, m_, v_ = _adamw(w[n].reshape(view), grads[n].reshape(view), m[n].reshape(view), v[n].reshape(view),
                            f"adamw_{n}")
        delta[n], new_m[n], new_v[n] = d_.reshape(shape), m_.reshape(shape), v_.reshape(shape)
    return (loss, dx[None], *[grads[n].reshape(w[n].shape) for n in names], *[delta[n] for n in names],
            *[new_m[n] for n in names], *[new_v[n] for n in names])
```

```python
import functools

import jax
import jax.numpy as jnp
from jax import lax
from jax.experimental import pallas as pl
from jax.experimental.pallas import tpu as pltpu

F32 = jnp.float32
BF16 = jnp.bfloat16
HI = lax.Precision.HIGHEST
MESH = pl.DeviceIdType.MESH
SDS = jax.ShapeDtypeStruct

D = 1024
D_XA = 512
D_CAT = 1536
N_MEM = 256
HD = 128
DEPTH = 4
EPS = 1e-6
TT = 512
DN_C = 64
DN_TB = 256
HALO = 8
AB_PAD = 128
VMEM_LIMIT = 56 * 1024 * 1024
GRAD_TILE = {0: 1024, 1: 256}

ADAM_LR, ADAM_B1, ADAM_B2, ADAM_EPS, ADAM_WD, ADAM_STEP = 0.001, 0.9, 0.999, 1e-08, 0.01, 10


def _params(n_grid, vmem=None):
    return pltpu.CompilerParams(dimension_semantics=("arbitrary",) * n_grid, vmem_limit_bytes=vmem)


def _rms(x, w):
    return x * lax.rsqrt(jnp.mean(x * x, axis=-1, keepdims=True) + EPS) * w


def _dot_nn(a, b):
    return jnp.dot(a.astype(BF16), b.astype(BF16), preferred_element_type=F32)


def _dot_nt(a, b):
    return lax.dot_general(a.astype(BF16), b.astype(BF16), (((1,), (1,)), ((), ())), preferred_element_type=F32)


def _dot_tn(a, b):
    return lax.dot_general(a.astype(BF16), b.astype(BF16), (((0,), (0,)), ((), ())), preferred_element_type=F32)


@jax.custom_vjp
def mm(a, b):
    return _dot_nn(a, b)


mm.defvjp(lambda a, b: (_dot_nn(a, b), (a, b)), lambda r, g: (_dot_nt(g, r[1]), _dot_tn(r[0], g)))


@jax.custom_vjp
def mm_nt(a, b):
    return _dot_nt(a, b)


mm_nt.defvjp(lambda a, b: (_dot_nt(a, b), (a, b)), lambda r, g: (_dot_nn(g, r[1]), _dot_tn(g, r[0])))


@jax.custom_vjp
def mm_tn(a, b):
    return _dot_tn(a, b)


mm_tn.defvjp(lambda a, b: (_dot_tn(a, b), (a, b)), lambda r, g: (_dot_nt(r[1], g), _dot_nn(r[0], g)))


def _dot_hi(a, b):
    return jnp.dot(a, b, precision=HI, preferred_element_type=F32)


def _row_spec(width, tile=TT):
    return pl.BlockSpec((tile, width), lambda i: (i, 0))


def _full_spec(shape):
    return pl.BlockSpec(shape, lambda *_: (0,) * len(shape))


def _widths(blocks, transposed):
    return [b.shape[0 if transposed else 1] for b in blocks]


def _inproj_fwd(x, nw, mix, gate, transposed, name):
    T, nm = x.shape[0], len(mix)
    M, G = sum(_widths(mix, transposed)), sum(_widths(gate, transposed))

    def body(x_ref, nw_ref, *refs):
        blocks, (pm_ref, pg_ref, h_ref) = refs[:-3], refs[-3:]
        h = _rms(x_ref[...], nw_ref[...]).astype(BF16)
        h_ref[...] = h
        for p_ref, group in ((pm_ref, blocks[:nm]), (pg_ref, blocks[nm:])):
            off = 0
            for w_ref in group:
                w = w_ref.shape[0 if transposed else 1]
                p_ref[:, off:off + w] = _dot_nt(h, w_ref[...]) if transposed else _dot_nn(h, w_ref[...])
                off += w

    return pl.pallas_call(
        body, name=name, grid=(T // TT,),
        in_specs=[_row_spec(D), _full_spec((1, D))] + [_full_spec(b.shape) for b in mix + gate],
        out_specs=[_row_spec(M), _row_spec(G), _row_spec(D)],
        out_shape=[SDS((T, M), F32), SDS((T, G), F32), SDS((T, D), BF16)],
        compiler_params=_params(1, VMEM_LIMIT))(x, nw, *mix, *gate)


def _inproj_bwd(dpm, dpg, x, nw, mix, gate, transposed, dxc, name):
    T, nm = x.shape[0], len(mix)
    M, G = sum(_widths(mix, transposed)), sum(_widths(gate, transposed))

    def body(dpm_ref, dpg_ref, x_ref, nw_ref, *refs):
        blocks, (dxc_ref, dx_ref, dnw_ref) = refs[:-3], refs[-3:]
        dh = None
        for dp_ref, group in ((dpm_ref, blocks[:nm]), (dpg_ref, blocks[nm:])):
            off = 0
            for w_ref in group:
                w = w_ref.shape[0 if transposed else 1]
                dp = dp_ref[:, off:off + w]
                part = _dot_nn(dp, w_ref[...]) if transposed else _dot_nt(dp, w_ref[...])
                dh = part if dh is None else dh + part
                off += w
        _, vjp = jax.vjp(_rms, x_ref[...], nw_ref[...])
        dxr, dnw = vjp(dh)
        dx_ref[...] = dxc_ref[...] + dxr

        @pl.when(pl.program_id(0) == 0)
        def _():
            dnw_ref[...] = jnp.zeros_like(dnw_ref)
        dnw_ref[...] += dnw

    return pl.pallas_call(
        body, name=name, grid=(T // TT,),
        in_specs=[_row_spec(M), _row_spec(G), _row_spec(D), _full_spec((1, D))]
        + [_full_spec(b.shape) for b in mix + gate] + [_row_spec(D)],
        out_specs=[_row_spec(D), _full_spec((1, D))],
        out_shape=[SDS((T, D), F32), SDS((1, D), F32)],
        compiler_params=_params(1, VMEM_LIMIT))(dpm, dpg, x, nw, *mix, *gate, dxc)


def _matmul_tn(a, b, name, sub=None):
    T, K = a.shape
    N = b.shape[1]
    tn = 1024 if N % 1024 == 0 else (640 if N % 640 == 0 else N)
    tt = min(1024, T)
    n_sub = 1 if sub is None else tn // sub

    def body(a_ref, b_ref, o_ref):
        @pl.when(pl.program_id(1) == 0)
        def _():
            o_ref[...] = jnp.zeros_like(o_ref)
        res = _dot_tn(a_ref[...], b_ref[...])
        if sub is None:
            o_ref[...] += res
        else:
            for i in range(n_sub):
                o_ref[i] += res[:, i * sub:(i + 1) * sub]

    if sub is None:
        out_spec, out_shape = pl.BlockSpec((K, tn), lambda j, t: (0, j)), SDS((K, N), F32)
    else:
        out_spec, out_shape = pl.BlockSpec((n_sub, K, sub), lambda j, t: (j, 0, 0)), SDS((N // sub, K, sub), F32)
    return pl.pallas_call(
        body, name=name, grid=(N // tn, T // tt),
        in_specs=[pl.BlockSpec((tt, K), lambda j, t: (t, 0)), pl.BlockSpec((tt, tn), lambda j, t: (t, j))],
        out_specs=out_spec, out_shape=out_shape,
        compiler_params=_params(2, VMEM_LIMIT))(a, b)


def _memkv_fn(mem, w, wkv):
    return mm(_rms(mem, w), wkv)


def _memkv_fwd(mem, w, wkv):
    def body(mem_ref, w_ref, wkv_ref, kv_ref):
        kv_ref[...] = _memkv_fn(mem_ref[...], w_ref[...], wkv_ref[...])

    return pl.pallas_call(body, name="memkv_fwd", out_shape=SDS((N_MEM, 2 * D_XA), F32),
                          compiler_params=_params(0, VMEM_LIMIT))(mem, w, wkv)


def _memkv_bwd(mem, w, wkv, dkv):
    def body(mem_ref, w_ref, wkv_ref, dkv_ref, dw_ref, dwkv_ref):
        _, vjp = jax.vjp(functools.partial(_memkv_fn, mem_ref[...]), w_ref[...], wkv_ref[...].astype(F32))
        dw, dwkv = vjp(dkv_ref[...])
        dw_ref[...] = dw
        dwkv_ref[...] = dwkv

    return pl.pallas_call(body, name="memkv_bwd", out_shape=[SDS((1, D), F32), SDS((D, 2 * D_XA), F32)],
                          compiler_params=_params(0, VMEM_LIMIT))(mem, w, wkv, dkv)


def _attn_gate(ymix, qx, z, *kvs):
    outs = []
    for j in range(4):
        s = mm_nt(qx[:, j * HD:(j + 1) * HD], kvs[j]) * (HD ** -0.5)
        e = jnp.exp(s - lax.stop_gradient(jnp.max(s, axis=-1, keepdims=True)))
        outs.append(mm(e / jnp.sum(e, axis=-1, keepdims=True), kvs[4 + j]))
    return jnp.concatenate([ymix] + outs, axis=1) * jax.nn.silu(z)


def _kv_blocks(kv_ref):
    return [kv_ref[:, j * HD:(j + 1) * HD] for j in range(8)]


def _ag_fwd(ymix, pg, kv, name):
    T = ymix.shape[0]

    def body(ymix_ref, pg_ref, kv_ref, ycat_ref):
        ycat_ref[...] = _attn_gate(ymix_ref[...], pg_ref[:, :D_XA], pg_ref[:, D_XA:], *_kv_blocks(kv_ref)).astype(BF16)

    return pl.pallas_call(
        body, name=name, grid=(T // TT,),
        in_specs=[_row_spec(D), _row_spec(D_XA + D_CAT), _full_spec((N_MEM, 2 * D_XA))],
        out_specs=_row_spec(D_CAT), out_shape=SDS((T, D_CAT), BF16),
        compiler_params=_params(1, VMEM_LIMIT))(ymix, pg, kv)


def _ag_bwd(dycat, ymix, pg, kv, dkv_in, name):
    T = ymix.shape[0]

    def body(dycat_ref, ymix_ref, pg_ref, kv_ref, dkvin_ref, dymix_ref, dpg_ref, dkv_ref):
        _, vjp = jax.vjp(_attn_gate, ymix_ref[...], pg_ref[:, :D_XA], pg_ref[:, D_XA:], *_kv_blocks(kv_ref))
        g = vjp(dycat_ref[...])
        dymix_ref[...] = g[0]
        dpg_ref[:, :D_XA] = g[1].astype(BF16)
        dpg_ref[:, D_XA:] = g[2].astype(BF16)

        @pl.when(pl.program_id(0) == 0)
        def _():
            dkv_ref[...] = dkvin_ref[...]
        for j in range(8):
            dkv_ref[:, j * HD:(j + 1) * HD] += g[3 + j]

    return pl.pallas_call(
        body, name=name, grid=(T // TT,),
        in_specs=[_row_spec(D_CAT), _row_spec(D), _row_spec(D_XA + D_CAT), _full_spec((N_MEM, 2 * D_XA)),
                  _full_spec((N_MEM, 2 * D_XA))],
        out_specs=[_row_spec(D), _row_spec(D_XA + D_CAT), _full_spec((N_MEM, 2 * D_XA))],
        out_shape=[SDS((T, D), F32), SDS((T, D_XA + D_CAT), BF16), SDS((N_MEM, 2 * D_XA), F32)],
        compiler_params=_params(1, VMEM_LIMIT))(dycat, ymix, pg, kv, dkv_in)


def _outproj_fwd(ycat, wo, x, nw, name):
    T = x.shape[0]

    def body(ycat_ref, wo_ref, x_ref, nw_ref, o_ref, xn_ref):
        o = jnp.dot(ycat_ref[...], wo_ref[...], preferred_element_type=F32)
        o_ref[...] = o
        xn_ref[...] = x_ref[...] + _rms(o, nw_ref[...])

    return pl.pallas_call(
        body, name=name, grid=(T // TT,),
        in_specs=[_row_spec(D_CAT), _full_spec((D_CAT, D)), _row_spec(D), _full_spec((1, D))],
        out_specs=[_row_spec(D), _row_spec(D)], out_shape=[SDS((T, D), F32), SDS((T, D), F32)],
        compiler_params=_params(1, VMEM_LIMIT))(ycat, wo, x, nw)


def _outproj_bwd(dxo, o, nw, wo, name):
    T = dxo.shape[0]

    def body(dxo_ref, o_ref, nw_ref, wo_ref, dycat_ref, dobf_ref, dnw_ref):
        _, vjp = jax.vjp(_rms, o_ref[...], nw_ref[...])
        do, dnw = vjp(dxo_ref[...])
        dobf = do.astype(BF16)
        dobf_ref[...] = dobf
        dycat_ref[...] = _dot_nt(dobf, wo_ref[...])

        @pl.when(pl.program_id(0) == 0)
        def _():
            dnw_ref[...] = jnp.zeros_like(dnw_ref)
        dnw_ref[...] += dnw

    return pl.pallas_call(
        body, name=name, grid=(T // TT,),
        in_specs=[_row_spec(D), _row_spec(D), _full_spec((1, D)), _full_spec((D_CAT, D))],
        out_specs=[_row_spec(D_CAT), _row_spec(D), _full_spec((1, D))],
        out_shape=[SDS((T, D_CAT), F32), SDS((T, D), BF16), SDS((1, D), F32)],
        compiler_params=_params(1, VMEM_LIMIT))(dxo, o, nw, wo)


def _loss_head(xl, target):
    T = xl.shape[0]

    def body(x_ref, t_ref, loss_ref, dx_ref):
        err = x_ref[...] - t_ref[...]
        dx_ref[...] = err * (1.0 / D)

        @pl.when(pl.program_id(0) == 0)
        def _():
            loss_ref[...] = jnp.zeros_like(loss_ref)
        part = jnp.sum(jnp.sum(err * err, axis=1, keepdims=True), axis=0, keepdims=True) * (0.5 / D)
        loss_ref[...] += jnp.broadcast_to(part, loss_ref.shape)

    return pl.pallas_call(
        body, name="loss_head", grid=(T // TT,),
        in_specs=[_row_spec(D), _row_spec(D)],
        out_specs=[_full_spec((8, 128)), _row_spec(D)], out_shape=[SDS((8, 128), F32), SDS((T, D), F32)],
        compiler_params=_params(1))(xl, target)


def _gmlp_pre(u, v, lnw, lnb):
    vg = jax.nn.gelu(v)
    xc = vg - jnp.mean(vg, axis=-1, keepdims=True)
    vl = xc * lax.rsqrt(jnp.mean(xc * xc, axis=-1, keepdims=True) + EPS) * lnw + lnb
    return jax.nn.gelu(u), vl


def _tril(n, strict=False):
    r = lax.broadcasted_iota(jnp.int32, (n, n), 0)
    c = lax.broadcasted_iota(jnp.int32, (n, n), 1)
    return (r > c) if strict else (r >= c)


def _gmlp_fwd(pm, lnw, lnb, ws, bs3, name):
    T = pm.shape[0]

    def body(pm_ref, lnw_ref, lnb_ref, ws_ref, bs_ref, y_ref):
        ug, vl = _gmlp_pre(pm_ref[:, :D], pm_ref[:, D:], lnw_ref[...], lnb_ref[...])
        mask = _tril(HD)
        for g in range(8):
            w = jnp.where(mask, ws_ref[g], 0.0)
            for c in range(TT // HD):
                rows, cols = slice(c * HD, (c + 1) * HD), slice(g * HD, (g + 1) * HD)
                y_ref[rows, cols] = ug[rows, cols] * (_dot_nn(w, vl[rows, cols]) + bs_ref[g])

    return pl.pallas_call(
        body, name=name, grid=(T // TT,),
        in_specs=[_row_spec(2 * D), _full_spec((1, D)), _full_spec((1, D)), _full_spec((8, HD, HD)),
                  _full_spec((8, HD, HD))],
        out_specs=_row_spec(D), out_shape=SDS((T, D), F32),
        compiler_params=_params(1, VMEM_LIMIT))(pm, lnw, lnb, ws, bs3)


def _gmlp_bwd(dy, pm, lnw, lnb, ws, bs3, name):
    T = pm.shape[0]
    n_t = T // TT

    def body(dy_ref, pm_ref, lnw_ref, lnb_ref, ws_ref, bs_ref, dpm_ref, dlnw_ref, dlnb_ref, dws_ref, dbs_ref,
             dug_scr, dvl_scr, dbs_scr):
        i = pl.program_id(0)

        @pl.when(i == 0)
        def _():
            dlnw_ref[...] = jnp.zeros_like(dlnw_ref)
            dlnb_ref[...] = jnp.zeros_like(dlnb_ref)
            dws_ref[...] = jnp.zeros_like(dws_ref)
            dbs_scr[...] = jnp.zeros_like(dbs_scr)

        (ug, vl), vjp = jax.vjp(_gmlp_pre, pm_ref[:, :D], pm_ref[:, D:], lnw_ref[...], lnb_ref[...])
        mask = _tril(HD)
        for g in range(8):
            w = jnp.where(mask, ws_ref[g], 0.0)
            dw = jnp.zeros((HD, HD), F32)
            db = jnp.zeros((HD, HD), F32)
            for c in range(TT // HD):
                rows, cols = slice(c * HD, (c + 1) * HD), slice(g * HD, (g + 1) * HD)
                dyb, vlb = dy_ref[rows, cols], vl[rows, cols]
                sp = _dot_nn(w, vlb) + bs_ref[g]
                dsp = dyb * ug[rows, cols]
                dug_scr[rows, cols] = dyb * sp
                dvl_scr[rows, cols] = _dot_tn(w, dsp)
                dw += _dot_nt(dsp, vlb)
                db += dsp
            dws_ref[g] += jnp.where(mask, dw, 0.0)
            dbs_scr[g] += db
        du, dv, dlnw, dlnb = vjp((dug_scr[...], dvl_scr[...]))
        dpm_ref[:, :D] = du.astype(BF16)
        dpm_ref[:, D:] = dv.astype(BF16)
        dlnw_ref[...] += dlnw
        dlnb_ref[...] += dlnb

        @pl.when(i == n_t - 1)
        def _():
            for g in range(8):
                dbs_ref[g] = jnp.broadcast_to(jnp.sum(dbs_scr[g], axis=1, keepdims=True), (HD, HD))

    return pl.pallas_call(
        body, name=name, grid=(n_t,),
        in_specs=[_row_spec(D), _row_spec(2 * D), _full_spec((1, D)), _full_spec((1, D)), _full_spec((8, HD, HD)),
                  _full_spec((8, HD, HD))],
        out_specs=[_row_spec(2 * D), _full_spec((1, D)), _full_spec((1, D)), _full_spec((8, HD, HD)),
                   _full_spec((8, HD, HD))],
        out_shape=[SDS((T, 2 * D), BF16), SDS((1, D), F32), SDS((1, D), F32), SDS((8, HD, HD), F32),
                   SDS((8, HD, HD), F32)],
        scratch_shapes=[pltpu.VMEM((TT, D), F32), pltpu.VMEM((TT, D), F32), pltpu.VMEM((8, HD, HD), F32)],
        compiler_params=_params(1, VMEM_LIMIT))(dy, pm, lnw, lnb, ws, bs3)


def _prev_spec(width, T):
    return pl.BlockSpec((HALO, width), lambda i: (jnp.maximum(i * (TT // HALO) - 1, 0), 0))


def _next_spec(width, T):
    return pl.BlockSpec((HALO, width), lambda i: (jnp.minimum((i + 1) * (TT // HALO), T // HALO - 1), 0))


def _rows_before(ext, j):
    return ext[HALO:] if j == 0 else pltpu.roll(ext, j, 0)[HALO:]


def _rows_after(ext, j):
    n = ext.shape[0]
    return ext[:n - HALO] if j == 0 else pltpu.roll(ext, n - j, 0)[:n - HALO]


def _conv_apply(ext_s, w):
    K = w.shape[0]
    y = _rows_before(ext_s, K - 1) * w[0:1]
    for k in range(1, K):
        y = y + _rows_before(ext_s, K - 1 - k) * w[k:k + 1]
    return y


def _conv_grads(ext_s, ext_dy, w):
    K = w.shape[0]
    dy = ext_dy[:ext_dy.shape[0] - HALO]
    ds = _rows_after(ext_dy, K - 1) * w[0:1]
    dws = [jnp.sum(dy * _rows_before(ext_s, K - 1), axis=0, keepdims=True)]
    for k in range(1, K):
        ds = ds + _rows_after(ext_dy, K - 1 - k) * w[k:k + 1]
        dws.append(jnp.sum(dy * _rows_before(ext_s, K - 1 - k), axis=0, keepdims=True))
    return ds, jnp.concatenate(dws, axis=0)


def _sconv_fwd(pm, w, name):
    T = pm.shape[0]

    def body(pm_ref, prev_ref, w_ref, y_ref):
        s = pm_ref[:, D:2 * D] * pm_ref[:, 2 * D:]
        sp = jnp.where(pl.program_id(0) > 0, prev_ref[:, D:2 * D] * prev_ref[:, 2 * D:], 0.0)
        y_ref[...] = pm_ref[:, :D] * _conv_apply(jnp.concatenate([sp, s], axis=0), w_ref[...])

    return pl.pallas_call(
        body, name=name, grid=(T // TT,),
        in_specs=[_row_spec(3 * D), _prev_spec(3 * D, T), _full_spec((3, D))],
        out_specs=_row_spec(D), out_shape=SDS((T, D), F32),
        compiler_params=_params(1, VMEM_LIMIT))(pm, pm, w)


def _sconv_bwd(dy, pm, w, name):
    T = pm.shape[0]
    n_t = T // TT

    def body(dy_ref, dyn_ref, pm_ref, prev_ref, next_ref, w_ref, dpm_ref, dw_ref):
        i = pl.program_id(0)
        bg, cg, hv = pm_ref[:, :D], pm_ref[:, D:2 * D], pm_ref[:, 2 * D:]
        sp = jnp.where(i > 0, prev_ref[:, D:2 * D] * prev_ref[:, 2 * D:], 0.0)
        ext_s = jnp.concatenate([sp, cg * hv], axis=0)
        dyv = dy_ref[...]
        dcn = jnp.where(i < n_t - 1, dyn_ref[...] * next_ref[:, :D], 0.0)
        ds, dw = _conv_grads(ext_s, jnp.concatenate([dyv * bg, dcn], axis=0), w_ref[...])
        dpm_ref[:, :D] = (dyv * _conv_apply(ext_s, w_ref[...])).astype(BF16)
        dpm_ref[:, D:2 * D] = (ds * hv).astype(BF16)
        dpm_ref[:, 2 * D:] = (ds * cg).astype(BF16)

        @pl.when(i == 0)
        def _():
            dw_ref[...] = jnp.zeros_like(dw_ref)
        dw_ref[...] += dw

    return pl.pallas_call(
        body, name=name, grid=(n_t,),
        in_specs=[_row_spec(D), _next_spec(D, T), _row_spec(3 * D), _prev_spec(3 * D, T), _next_spec(3 * D, T),
                  _full_spec((3, D))],
        out_specs=[_row_spec(3 * D), _full_spec((3, D))],
        out_shape=[SDS((T, 3 * D), BF16), SDS((3, D), F32)],
        compiler_params=_params(1, VMEM_LIMIT))(dy, dy, pm, pm, pm, w)


def _dnconv_fwd(pm, w, name):
    T = pm.shape[0]

    def body(pm_ref, prev_ref, w_ref, c_ref):
        sp = jnp.where(pl.program_id(0) > 0, prev_ref[...], 0.0)
        c_ref[...] = _conv_apply(jnp.concatenate([sp, pm_ref[...]], axis=0), w_ref[...])

    return pl.pallas_call(
        body, name=name, grid=(T // TT,),
        in_specs=[_row_spec(3 * D), _prev_spec(3 * D, T), _full_spec((4, 3 * D))],
        out_specs=_row_spec(3 * D), out_shape=SDS((T, 3 * D), F32),
        compiler_params=_params(1, VMEM_LIMIT))(pm, pm, w)


def _dnconv_bwd(dcq, dck, dcv, dab, pm, w, name):
    T = pm.shape[0]
    n_t = T // TT

    def body(dq_ref, dk_ref, dv_ref, dqn_ref, dkn_ref, dvn_ref, dab_ref, pm_ref, prev_ref, w_ref, dpm_ref, dw_ref):
        i = pl.program_id(0)
        sp = jnp.where(i > 0, prev_ref[...], 0.0)
        ext_s = jnp.concatenate([sp, pm_ref[...]], axis=0)
        own = jnp.concatenate([dq_ref[...], dk_ref[...], dv_ref[...]], axis=1)
        nxt = jnp.where(i < n_t - 1, jnp.concatenate([dqn_ref[...], dkn_ref[...], dvn_ref[...]], axis=1), 0.0)
        ds, dw = _conv_grads(ext_s, jnp.concatenate([own, nxt], axis=0), w_ref[...])
        dpm_ref[:, :3 * D] = ds.astype(BF16)
        dpm_ref[:, 3 * D:] = dab_ref[...].astype(BF16)

        @pl.when(i == 0)
        def _():
            dw_ref[...] = jnp.zeros_like(dw_ref)
        dw_ref[...] += dw

    return pl.pallas_call(
        body, name=name, grid=(n_t,),
        in_specs=[_row_spec(D), _row_spec(D), _row_spec(D), _next_spec(D, T), _next_spec(D, T), _next_spec(D, T),
                  _row_spec(AB_PAD), _row_spec(3 * D), _prev_spec(3 * D, T), _full_spec((4, 3 * D))],
        out_specs=[_row_spec(3 * D + AB_PAD), _full_spec((4, 3 * D))],
        out_shape=[SDS((T, 3 * D + AB_PAD), BF16), SDS((4, 3 * D), F32)],
        compiler_params=_params(1, VMEM_LIMIT))(dcq, dck, dcv, dcq, dck, dcv, dab, pm, pm, w)


def _l2n(x):
    return x * lax.rsqrt(jnp.sum(x * x, axis=-1, keepdims=True) + EPS)


def _softplus(x):
    return jnp.maximum(x, 0.0) + jnp.log1p(jnp.exp(-jnp.abs(x)))


def _dn_gates(ab, alog, dtb, h):
    lane = lax.broadcasted_iota(jnp.int32, ab.shape, 1)
    g_all = -jnp.exp(alog) * _softplus(ab + dtb)
    g = jnp.sum(jnp.where(lane == h, g_all, 0.0), axis=1, keepdims=True)
    beta = jnp.sum(jnp.where(lane == 8 + h, jax.nn.sigmoid(ab), 0.0), axis=1, keepdims=True)
    ones = jnp.ones((1, HD), F32)
    return g * ones, beta * ones


def _dn_chunk(cq, ck, cv, gb, bb, S, onw):
    C = DN_C
    q = _l2n(jax.nn.silu(cq)) * (HD ** -0.5)
    k = _l2n(jax.nn.silu(ck))
    v = jax.nn.silu(cv)
    incl, strict = _tril(C), _tril(C, strict=True)
    gcum = _dot_hi(incl.astype(F32), gb)
    gi = gcum[:, :C]
    gj = gcum.T[:C, :]
    decay = jnp.where(incl, jnp.exp(jnp.where(incl, gi - gj, 0.0)), 0.0)
    kb = k * bb
    a_mat = jnp.where(strict, mm_nt(kb, k) * decay, 0.0)
    p = -a_mat
    eye = (lax.broadcasted_iota(jnp.int32, (C, C), 0) == lax.broadcasted_iota(jnp.int32, (C, C), 1)).astype(F32)
    t_mat = eye + p
    for _ in range(5):
        p = _dot_hi(p, p)
        t_mat = t_mat + _dot_hi(t_mat, p)
    eg = jnp.exp(gcum)
    u = mm(t_mat, v * bb)
    w = mm(t_mat, kb * eg)
    qk = mm_nt(q, k) * decay
    glast = gcum[C - 1:C, :]
    v_new = u - mm(w, S)
    o = mm(q * eg, S) + mm(qk, v_new)
    s_new = S * jnp.exp(glast) + mm_tn(k * jnp.exp(glast - gcum), v_new)
    return _rms(o, onw), s_new


def _dn_specs(T, rev):
    nb = T // DN_TB
    blk = (lambda n: nb - 1 - n) if rev else (lambda n: n)
    head = [pl.BlockSpec((DN_TB, HD), functools.partial(lambda n, h, off: (blk(n), off + h), off=8 * s)) for s in range(3)]
    ab = pl.BlockSpec((DN_TB, AB_PAD), lambda n, h: (blk(n), 3 * D // AB_PAD))
    st = pl.BlockSpec((DN_TB // DN_C, None, HD, HD), lambda n, h: (blk(n), h, 0, 0))
    out = pl.BlockSpec((DN_TB, HD), lambda n, h: (blk(n), h))
    row = pl.BlockSpec((1, HD), lambda n, h: (0, 0))
    return nb, head, ab, st, out, row


def _dn_fwd(cpre, pm, alog, dtb, onw, name):
    T = cpre.shape[0]
    nb, head, ab, st, out, row = _dn_specs(T, False)

    def body(cq_ref, ck_ref, cv_ref, ab_ref, alog_ref, dtb_ref, onw_ref, o_ref, st_ref, s_scr):
        n, h = pl.program_id(0), pl.program_id(1)

        @pl.when(n == 0)
        def _():
            s_scr[h] = jnp.zeros((HD, HD), F32)
        gb, bb = _dn_gates(ab_ref[...], alog_ref[...], dtb_ref[...], h)
        S = s_scr[h]
        for c in range(DN_TB // DN_C):
            rows = slice(c * DN_C, (c + 1) * DN_C)
            st_ref[c] = S
            o, S = _dn_chunk(cq_ref[rows, :], ck_ref[rows, :], cv_ref[rows, :], gb[rows], bb[rows], S, onw_ref[...])
            o_ref[rows, :] = o
        s_scr[h] = S

    return pl.pallas_call(
        body, name=name, grid=(nb, 8),
        in_specs=head + [ab, row, row, row], out_specs=[out, st],
        out_shape=[SDS((T, D), F32), SDS((T // DN_C, 8, HD, HD), F32)],
        scratch_shapes=[pltpu.VMEM((8, HD, HD), F32)],
        compiler_params=_params(2, VMEM_LIMIT))(cpre, cpre, cpre, pm, alog, dtb, onw)


def _dn_bwd(do, cpre, pm, st, alog, dtb, onw, name):
    T = cpre.shape[0]
    nb, head, ab, stspec, out, row = _dn_specs(T, True)

    def body(do_ref, cq_ref, ck_ref, cv_ref, ab_ref, st_ref, alog_ref, dtb_ref, onw_ref,
             dcq_ref, dck_ref, dcv_ref, dab_ref, dalog_ref, ddtb_ref, donw_ref, ds_scr):
        n, h = pl.program_id(0), pl.program_id(1)

        @pl.when(n == 0)
        def _():
            ds_scr[h] = jnp.zeros((HD, HD), F32)

        @pl.when((n == 0) & (h == 0))
        def _():
            dalog_ref[...] = jnp.zeros_like(dalog_ref)
            ddtb_ref[...] = jnp.zeros_like(ddtb_ref)
            donw_ref[...] = jnp.zeros_like(donw_ref)

        @pl.when(h == 0)
        def _():
            dab_ref[...] = jnp.zeros_like(dab_ref)

        (gb, bb), gates_vjp = jax.vjp(lambda a, b, c: _dn_gates(a, b, c, h), ab_ref[...], alog_ref[...], dtb_ref[...])
        dS = ds_scr[h]
        n_c = DN_TB // DN_C
        dgs, dbs = [None] * n_c, [None] * n_c
        donw = jnp.zeros((1, HD), F32)
        for c in reversed(range(n_c)):
            rows = slice(c * DN_C, (c + 1) * DN_C)
            _, vjp = jax.vjp(_dn_chunk, cq_ref[rows, :], ck_ref[rows, :], cv_ref[rows, :], gb[rows], bb[rows],
                             st_ref[c], onw_ref[...])
            dcq, dck, dcv, dgs[c], dbs[c], dS, dn = vjp((do_ref[rows, :], dS))
            dcq_ref[rows, :] = dcq
            dck_ref[rows, :] = dck
            dcv_ref[rows, :] = dcv
            donw += dn
        ds_scr[h] = dS
        dab, dalog, ddtb = gates_vjp((jnp.concatenate(dgs, axis=0), jnp.concatenate(dbs, axis=0)))
        dab_ref[...] += dab
        dalog_ref[...] += dalog
        ddtb_ref[...] += ddtb
        donw_ref[...] += donw

    dabspec = pl.BlockSpec((DN_TB, AB_PAD), lambda n, h: (nb - 1 - n, 0))
    return pl.pallas_call(
        body, name=name, grid=(nb, 8),
        in_specs=[out] + head + [ab, stspec, row, row, row],
        out_specs=[out, out, out, dabspec, row, row, row],
        out_shape=[SDS((T, D), F32)] * 3 + [SDS((T, AB_PAD), F32)] + [SDS((1, HD), F32)] * 3,
        scratch_shapes=[pltpu.VMEM((8, HD, HD), F32)],
        compiler_params=_params(2, VMEM_LIMIT))(do, cpre, cpre, cpre, pm, st, alog, dtb, onw)


_BNN = (((2,), (1,)), ((0,), (0,)))
_BNT = (((2,), (2,)), ((0,), (0,)))
_BTN = (((1,), (1,)), ((0,), (0,)))


def _bdot(a, b, dims):
    return lax.dot_general(a.astype(BF16), b.astype(BF16), dims, preferred_element_type=F32)


def _bdot3(a, b, dims):
    ah, bh = a.astype(BF16), b.astype(BF16)
    al, bl = (a - ah.astype(F32)).astype(BF16), (b - bh.astype(F32)).astype(BF16)
    d = functools.partial(lax.dot_general, dimension_numbers=dims, preferred_element_type=F32)
    return d(ah, bh) + (d(ah, bl) + d(al, bh))


def _bdot_hi(a, b, dims):
    return lax.dot_general(a, b, dims, precision=HI, preferred_element_type=F32)


def _batched_matmuls(dot):
    @jax.custom_vjp
    def nn(a, b):
        return dot(a, b, _BNN)

    @jax.custom_vjp
    def nt(a, b):
        return dot(a, b, _BNT)

    @jax.custom_vjp
    def tn(a, b):
        return dot(a, b, _BTN)

    nn.defvjp(lambda a, b: (dot(a, b, _BNN), (a, b)), lambda r, g: (dot(g, r[1], _BNT), dot(r[0], g, _BTN)))
    nt.defvjp(lambda a, b: (dot(a, b, _BNT), (a, b)), lambda r, g: (dot(g, r[1], _BNN), dot(g, r[0], _BTN)))
    tn.defvjp(lambda a, b: (dot(a, b, _BTN), (a, b)), lambda r, g: (dot(r[1], g, _BNT), dot(r[0], g, _BNN)))
    return nn, nt, tn


bmm, bmm_nt, bmm_tn = _batched_matmuls(_bdot)
bmm3, _, _ = _batched_matmuls(_bdot3)
bmm_hi, bmm_hi_nt, _ = _batched_matmuls(_bdot_hi)

@jax.custom_vjp
def _neumann_inverse(n):
    C = n.shape[1]
    eye = lax.broadcasted_iota(jnp.int32, n.shape, 1) == lax.broadcasted_iota(jnp.int32, n.shape, 2)
    t = eye.astype(F32) + n
    for _ in range(5):
        n = _bdot3(n, n, _BNN)
        t = t + _bdot3(t, n, _BNN)
    return t


def _neumann_inverse_fwd(n):
    t = _neumann_inverse(n)
    return t, t


def _neumann_inverse_bwd(t, g):
    return (_bdot3(_bdot3(t, g, _BTN), t, _BNT),)


_neumann_inverse.defvjp(_neumann_inverse_fwd, _neumann_inverse_bwd)


@jax.custom_vjp
def _saved_inverse(n, t):
    return t


_saved_inverse.defvjp(lambda n, t: (t, t), lambda t, g: (_bdot3(_bdot3(t, g, _BTN), t, _BNT), jnp.zeros_like(t)))

DN_NCH = DN_TB // DN_C
DN_NH = 4


def _decay_terms(ab, alog, dtb, first_head, n_heads):
    C = DN_C
    lane = lax.broadcasted_iota(jnp.int32, ab.shape, 1)
    g_all = (-jnp.exp(alog) * _softplus(ab + dtb)).reshape(DN_NCH, C, HD)
    beta_all = jax.nn.sigmoid(ab)
    r = lax.broadcasted_iota(jnp.int32, (DN_NCH, C, C), 1)
    c = lax.broadcasted_iota(jnp.int32, (DN_NCH, C, C), 2)
    gc_all = bmm_hi((r >= c).astype(F32), g_all)
    gc_rows = [gc_all[i].T for i in range(DN_NCH)]
    lane3 = lax.broadcasted_iota(jnp.int32, (DN_NCH, C, HD), 2)
    row = lax.broadcasted_iota(jnp.int32, (HD, C), 0)
    ones = jnp.ones((1, HD), F32)
    gcs, gjs, betas = [], [], []
    for i in range(n_heads):
        h = first_head + i
        gcs.append(jnp.sum(jnp.where(lane3 == h, gc_all, 0.0), axis=2, keepdims=True) * ones)
        gjs.append(jnp.concatenate(
            [jnp.broadcast_to(jnp.sum(jnp.where(row == h, t, 0.0), axis=0, keepdims=True), (C, C))[None] for t in gc_rows],
            axis=0))
        beta = jnp.sum(jnp.where(lane == 8 + h, beta_all, 0.0), axis=1, keepdims=True) * ones
        betas.append(beta.reshape(DN_NCH, C, HD))
    return jnp.concatenate(gcs, axis=0), jnp.concatenate(gjs, axis=0), jnp.concatenate(betas, axis=0)


def _dn_prep(cq, ck, cv, gcum, gj, bb, t_saved=None):
    B, C = cq.shape[0], DN_C
    q = _l2n(jax.nn.silu(cq)) * (HD ** -0.5)
    k = _l2n(jax.nn.silu(ck))
    v = jax.nn.silu(cv)
    r = lax.broadcasted_iota(jnp.int32, (B, C, C), 1)
    c = lax.broadcasted_iota(jnp.int32, (B, C, C), 2)
    incl, strict = r >= c, r > c
    decay = jnp.where(incl, jnp.exp(jnp.where(incl, gcum[:, :, :C] - gj, 0.0)), 0.0)
    kb = k * bb
    n_mat = -jnp.where(strict, bmm_nt(kb, k) * decay, 0.0)
    t_mat = _neumann_inverse(n_mat) if t_saved is None else _saved_inverse(n_mat, t_saved)
    eg = jnp.exp(gcum)
    glast = gcum[:, C - 1:C, :]
    return (bmm(t_mat, v * bb), bmm(t_mat, kb * eg), bmm_nt(q, k) * decay, q * eg, k * jnp.exp(glast - gcum),
            jnp.exp(glast), t_mat)


def _dn_scan_step(u, w, qk, qd, kd, egl, S, onw):
    v_new = u - bmm(w, S)
    o = bmm(qd, S) + bmm(qk, v_new)
    return _rms(o, onw), S * egl + bmm_tn(kd, v_new)


def _head_gates(ab, alog, dtb, first_head, n_heads):
    gs, bs = [], []
    for i in range(n_heads):
        g, b = _dn_gates(ab, alog, dtb, first_head + i)
        gs.append(g.reshape(DN_NCH, DN_C, HD))
        bs.append(b.reshape(DN_NCH, DN_C, HD))
    return jnp.concatenate(gs, axis=0), jnp.concatenate(bs, axis=0)


def _to_batch(ref, n_heads):
    return jnp.concatenate([ref[:, i * HD:(i + 1) * HD].astype(F32).reshape(DN_NCH, DN_C, HD) for i in range(n_heads)],
                           axis=0)


def _from_batch(ref, val, n_heads):
    for i in range(n_heads):
        ref[:, i * HD:(i + 1) * HD] = val[i * DN_NCH:(i + 1) * DN_NCH].reshape(DN_TB, HD).astype(ref.dtype)


def _prep_specs(T, rev):
    nb = T // DN_TB
    blk = (lambda n: nb - 1 - n) if rev else (lambda n: n)
    ng = 8 // DN_NH
    head = [pl.BlockSpec((DN_TB, DN_NH * HD), functools.partial(lambda n, h, off: (blk(n), off + h), off=ng * s))
            for s in range(3)]
    ab = pl.BlockSpec((DN_TB, AB_PAD), lambda n, h: (blk(n), 3 * D // AB_PAD))
    row = pl.BlockSpec((1, HD), lambda n, h: (0, 0))
    wide = pl.BlockSpec((DN_TB, DN_NH * HD), lambda n, h: (blk(n), h))
    qk = pl.BlockSpec((DN_NCH, DN_NH, DN_C, DN_C), lambda n, h: (blk(n), h, 0, 0))
    eg = pl.BlockSpec((DN_NCH, DN_NH, 1, HD), lambda n, h: (blk(n), h, 0, 0))
    return nb, ng, head, ab, row, wide, qk, eg


def _dn_prep_fwd(cpre, pm, alog, dtb, name):
    T = cpre.shape[0]
    nb, ng, head, ab, row, wide, qks, egs = _prep_specs(T, False)

    def body(cq_ref, ck_ref, cv_ref, ab_ref, alog_ref, dtb_ref, u_ref, w_ref, qk_ref, qd_ref, kd_ref, e_ref, t_ref):
        gcum, gj, bb = _decay_terms(ab_ref[...], alog_ref[...], dtb_ref[...], pl.program_id(1) * DN_NH, DN_NH)
        u, w, qk, qd, kd, egl, t_mat = _dn_prep(_to_batch(cq_ref, DN_NH), _to_batch(ck_ref, DN_NH),
                                                _to_batch(cv_ref, DN_NH), gcum, gj, bb)
        _from_batch(u_ref, u, DN_NH)
        _from_batch(w_ref, w, DN_NH)
        _from_batch(qd_ref, qd, DN_NH)
        _from_batch(kd_ref, kd, DN_NH)
        for i in range(DN_NH):
            qk_ref[:, i] = qk[i * DN_NCH:(i + 1) * DN_NCH].astype(BF16)
            e_ref[:, i] = egl[i * DN_NCH:(i + 1) * DN_NCH]
            t_ref[:, i] = t_mat[i * DN_NCH:(i + 1) * DN_NCH]

    return pl.pallas_call(
        body, name=name, grid=(nb, ng), in_specs=head + [ab, row, row],
        out_specs=[wide, wide, qks, wide, wide, egs, qks],
        out_shape=[SDS((T, D), F32), SDS((T, D), BF16), SDS((T // DN_C, 8, DN_C, DN_C), BF16), SDS((T, D), BF16),
                   SDS((T, D), BF16), SDS((T // DN_C, 8, 1, HD), F32), SDS((T // DN_C, 8, DN_C, DN_C), F32)],
        compiler_params=_params(2, VMEM_LIMIT))(cpre, cpre, cpre, pm, alog, dtb)


def _dn_prep_bwd(du, dw, dqk, dqd, dkd, degl, t_mat, cpre, pm, alog, dtb, name):
    T = cpre.shape[0]
    nb, ng, head, ab, row, wide, qks, egs = _prep_specs(T, True)

    def body(du_ref, dw_ref, dqk_ref, dqd_ref, dkd_ref, de_ref, t_ref, cq_ref, ck_ref, cv_ref, ab_ref, alog_ref,
             dtb_ref, dcq_ref, dck_ref, dcv_ref, dab_ref, dalog_ref, ddtb_ref):
        n, h = pl.program_id(0), pl.program_id(1)

        @pl.when((n == 0) & (h == 0))
        def _():
            dalog_ref[...] = jnp.zeros_like(dalog_ref)
            ddtb_ref[...] = jnp.zeros_like(ddtb_ref)

        @pl.when(h == 0)
        def _():
            dab_ref[...] = jnp.zeros_like(dab_ref)

        t_saved = jnp.concatenate([t_ref[:, i] for i in range(DN_NH)], axis=0)

        def fwd(cq, ck, cv, ab_v, alog_v, dtb_v):
            gcum, gj, bb = _decay_terms(ab_v, alog_v, dtb_v, h * DN_NH, DN_NH)
            return _dn_prep(cq, ck, cv, gcum, gj, bb, t_saved)[:6]

        _, vjp = jax.vjp(fwd, _to_batch(cq_ref, DN_NH), _to_batch(ck_ref, DN_NH), _to_batch(cv_ref, DN_NH), ab_ref[...],
                         alog_ref[...], dtb_ref[...])
        cot = (_to_batch(du_ref, DN_NH), _to_batch(dw_ref, DN_NH),
               jnp.concatenate([dqk_ref[:, i] for i in range(DN_NH)], axis=0), _to_batch(dqd_ref, DN_NH),
               _to_batch(dkd_ref, DN_NH), jnp.concatenate([de_ref[:, i] for i in range(DN_NH)], axis=0))
        dcq, dck, dcv, dab, dalog, ddtb = vjp(cot)
        _from_batch(dcq_ref, dcq, DN_NH)
        _from_batch(dck_ref, dck, DN_NH)
        _from_batch(dcv_ref, dcv, DN_NH)
        dab_ref[...] += dab
        dalog_ref[...] += dalog
        ddtb_ref[...] += ddtb

    dabspec = pl.BlockSpec((DN_TB, AB_PAD), lambda n, h: (nb - 1 - n, 0))
    return pl.pallas_call(
        body, name=name, grid=(nb, ng),
        in_specs=[wide, wide, qks, wide, wide, egs, qks] + head + [ab, row, row],
        out_specs=[wide, wide, wide, dabspec, row, row],
        out_shape=[SDS((T, D), F32)] * 3 + [SDS((T, AB_PAD), F32)] + [SDS((1, HD), F32)] * 2,
        compiler_params=_params(2, VMEM_LIMIT))(du, dw, dqk, dqd, dkd, degl, t_mat, cpre, cpre, cpre, pm, alog, dtb)


def _scan_specs(T, rev):
    nb = T // DN_TB
    blk = (lambda n: nb - 1 - n) if rev else (lambda n: n)
    wide = pl.BlockSpec((DN_TB, D), lambda n: (blk(n), 0))
    qk = pl.BlockSpec((DN_NCH, 8, DN_C, DN_C), lambda n: (blk(n), 0, 0, 0))
    eg = pl.BlockSpec((DN_NCH, 8, 1, HD), lambda n: (blk(n), 0, 0, 0))
    st = pl.BlockSpec((DN_NCH, 8, HD, HD), lambda n: (blk(n), 0, 0, 0))
    row = pl.BlockSpec((1, HD), lambda n: (0, 0))
    return nb, wide, qk, eg, st, row


def _heads_of(ref, rows):
    return jnp.concatenate([ref[rows, h * HD:(h + 1) * HD].astype(F32)[None] for h in range(8)], axis=0)


def _dn_scan_fwd(u, w, qk, qd, kd, egl, onw, name):
    T = u.shape[0]
    nb, wide, qks, egs, sts, row = _scan_specs(T, False)

    def body(u_ref, w_ref, qk_ref, qd_ref, kd_ref, e_ref, onw_ref, o_ref, st_ref, s_scr):
        @pl.when(pl.program_id(0) == 0)
        def _():
            s_scr[...] = jnp.zeros_like(s_scr)
        S = s_scr[...]
        for c in range(DN_NCH):
            rows = slice(c * DN_C, (c + 1) * DN_C)
            st_ref[c] = S
            o, S = _dn_scan_step(_heads_of(u_ref, rows), _heads_of(w_ref, rows), qk_ref[c].astype(F32),
                                 _heads_of(qd_ref, rows), _heads_of(kd_ref, rows), e_ref[c], S, onw_ref[...])
            for h in range(8):
                o_ref[rows, h * HD:(h + 1) * HD] = o[h]
        s_scr[...] = S

    return pl.pallas_call(
        body, name=name, grid=(nb,), in_specs=[wide, wide, qks, wide, wide, egs, row], out_specs=[wide, sts],
        out_shape=[SDS((T, D), F32), SDS((T // DN_C, 8, HD, HD), F32)],
        scratch_shapes=[pltpu.VMEM((8, HD, HD), F32)],
        compiler_params=_params(1, VMEM_LIMIT))(u, w, qk, qd, kd, egl, onw)


def _dn_scan_bwd(do, u, w, qk, qd, kd, egl, st, onw, name):
    T = u.shape[0]
    nb, wide, qks, egs, sts, row = _scan_specs(T, True)

    def body(do_ref, u_ref, w_ref, qk_ref, qd_ref, kd_ref, e_ref, st_ref, onw_ref,
             du_ref, dw_ref, dqk_ref, dqd_ref, dkd_ref, de_ref, donw_ref, ds_scr):
        @pl.when(pl.program_id(0) == 0)
        def _():
            ds_scr[...] = jnp.zeros_like(ds_scr)
            donw_ref[...] = jnp.zeros_like(donw_ref)
        dS = ds_scr[...]
        donw = jnp.zeros((1, HD), F32)
        for c in reversed(range(DN_NCH)):
            rows = slice(c * DN_C, (c + 1) * DN_C)
            _, vjp = jax.vjp(_dn_scan_step, _heads_of(u_ref, rows), _heads_of(w_ref, rows), qk_ref[c].astype(F32),
                             _heads_of(qd_ref, rows), _heads_of(kd_ref, rows), e_ref[c], st_ref[c], onw_ref[...])
            du, dw, dqk, dqd, dkd, de, dS, dn = vjp((_heads_of(do_ref, rows), dS))
            for h in range(8):
                cols = slice(h * HD, (h + 1) * HD)
                du_ref[rows, cols] = du[h]
                dw_ref[rows, cols] = dw[h]
                dqd_ref[rows, cols] = dqd[h]
                dkd_ref[rows, cols] = dkd[h]
            dqk_ref[c] = dqk
            de_ref[c] = de
            donw += dn
        ds_scr[...] = dS
        donw_ref[...] += donw

    return pl.pallas_call(
        body, name=name, grid=(nb,), in_specs=[wide, wide, wide, qks, wide, wide, egs, sts, row],
        out_specs=[wide, wide, qks, wide, wide, egs, row],
        out_shape=[SDS((T, D), F32), SDS((T, D), F32), SDS((T // DN_C, 8, DN_C, DN_C), F32), SDS((T, D), F32),
                   SDS((T, D), F32), SDS((T // DN_C, 8, 1, HD), F32), SDS((1, HD), F32)],
        scratch_shapes=[pltpu.VMEM((8, HD, HD), F32)],
        compiler_params=_params(1, VMEM_LIMIT))(do, u, w, qk, qd, kd, egl, st, onw)


def _adamw(w, g, m, v, name):
    R, C = w.shape
    tr = 256 if R % 256 == 0 and R > 256 else R
    tc = 256 if tr == R and R > 256 and C % 256 == 0 else C
    c1 = 1.0 - ADAM_B1 ** ADAM_STEP
    c2 = 1.0 - ADAM_B2 ** ADAM_STEP

    def body(w_ref, g_ref, m_ref, v_ref, d_ref, nm_ref, nv_ref):
        gv = g_ref[...]
        nm = ADAM_B1 * m_ref[...] + (1.0 - ADAM_B1) * gv
        nv = ADAM_B2 * v_ref[...] + (1.0 - ADAM_B2) * (gv * gv)
        nm_ref[...] = nm
        nv_ref[...] = nv
        d_ref[...] = -ADAM_LR * ((nm / c1) / (jnp.sqrt(nv / c2) + ADAM_EPS) + ADAM_WD * w_ref[...])

    spec = pl.BlockSpec((tr, tc), lambda i, j: (i, j))
    return pl.pallas_call(
        body, name=name, grid=(R // tr, C // tc), in_specs=[spec] * 4, out_specs=[spec] * 3,
        out_shape=[SDS((R, C), F32)] * 3, compiler_params=_params(2, VMEM_LIMIT))(w, g, m, v)


def _local_step(x, mem, target, wts, sm):
    kinds = [i % 3 for i in range(DEPTH)]
    mnw = sm["mem_norm_w"].reshape(1, D)
    kv = _memkv_fwd(mem, mnw, wts["wkv"])
    saved = []
    for i, kind in enumerate(kinds):
        j = i // 3
        npre = sm["norm_pre"][i].reshape(1, D)
        npost = sm["norm_post"][i].reshape(1, D)
        pm, pg, h = _inproj_fwd(x, npre, wts["mix"][i], wts["gate"][i], kind == 2, f"inproj_fwd_{i}")
        extra = None
        if kind == 0:
            bs3 = jnp.broadcast_to(sm["a_b_s"][j][:, :, None], (8, HD, HD))
            ymix = _gmlp_fwd(pm, sm["a_ln_w"][j].reshape(1, D), sm["a_ln_b"][j].reshape(1, D), sm["a_w_s"][j], bs3,
                             f"gmlp_fwd_{i}")
            extra = bs3
        elif kind == 1:
            ymix = _sconv_fwd(pm, sm["b_conv_w"][j], f"sconv_fwd_{i}")
        else:
            cpre = _dnconv_fwd(pm, sm["c_conv_w"][j], f"dnconv_fwd_{i}")
            alog = jnp.pad(sm["c_a_log"][j], (0, HD - 8)).reshape(1, HD)
            dtb = jnp.pad(sm["c_dt_bias"][j], (0, HD - 8)).reshape(1, HD)
            onw = sm["c_o_norm_w"][j].reshape(1, HD)
            *prep, t_mat = _dn_prep_fwd(cpre, pm, alog, dtb, f"dn_prep_fwd_{i}")
            ymix, st = _dn_scan_fwd(*prep, onw, f"dn_scan_fwd_{i}")
            extra = (cpre, prep, t_mat, st, alog, dtb, onw)
        ycat = _ag_fwd(ymix, pg, kv, f"ag_fwd_{i}")
        o, xn = _outproj_fwd(ycat, wts["wo"][i], x, npost, f"outproj_fwd_{i}")
        saved.append((x, h, pm, pg, ymix, ycat, o, extra))
        x = xn

    loss, dx = _loss_head(x, target)

    g = {"wm": [None] * DEPTH, "wg": [None] * DEPTH, "wo": [None] * DEPTH, "norm_pre": [None] * DEPTH,
         "norm_post": [None] * DEPTH}
    dkv = jnp.zeros((N_MEM, 2 * D_XA), F32)
    for i in reversed(range(DEPTH)):
        kind, j = kinds[i], i // 3
        xi, h, pm, pg, ymix, ycat, o, extra = saved[i]
        npre = sm["norm_pre"][i].reshape(1, D)
        npost = sm["norm_post"][i].reshape(1, D)
        dycat, dobf, g["norm_post"][i] = _outproj_bwd(dx, o, npost, wts["wo"][i], f"outproj_bwd_{i}")
        g["wo"][i] = _matmul_tn(ycat, dobf, f"dwo_{i}")
        dymix, dpg, dkv = _ag_bwd(dycat, ymix, pg, kv, dkv, f"ag_bwd_{i}")
        if kind == 0:
            dpm, dlnw, dlnb, dws, dbs3 = _gmlp_bwd(dymix, pm, sm["a_ln_w"][j].reshape(1, D),
                                                   sm["a_ln_b"][j].reshape(1, D), sm["a_w_s"][j], extra,
                                                   f"gmlp_bwd_{i}")
            g.setdefault("a_ln_w", {})[j] = dlnw.reshape(D)
            g.setdefault("a_ln_b", {})[j] = dlnb.reshape(D)
            g.setdefault("a_w_s", {})[j] = dws
            g.setdefault("a_b_s", {})[j] = dbs3[:, :, 0]
        elif kind == 1:
            dpm, dcw = _sconv_bwd(dymix, pm, sm["b_conv_w"][j], f"sconv_bwd_{i}")
            g.setdefault("b_conv_w", {})[j] = dcw
        else:
            cpre, prep, t_mat, st, alog, dtb, onw = extra
            *dprep, donw = _dn_scan_bwd(dymix, *prep, st, onw, f"dn_scan_bwd_{i}")
            dcq, dck, dcv, dab, dalog, ddtb = _dn_prep_bwd(*dprep, t_mat, cpre, pm, alog, dtb, f"dn_prep_bwd_{i}")
            dpm, dcw = _dnconv_bwd(dcq, dck, dcv, dab, pm, sm["c_conv_w"][j], f"dnconv_bwd_{i}")
            g.setdefault("c_conv_w", {})[j] = dcw
            g.setdefault("c_a_log", {})[j] = dalog[0, :8]
            g.setdefault("c_dt_bias", {})[j] = ddtb[0, :8]
            g.setdefault("c_o_norm_w", {})[j] = donw[0]
        if kind == 2:
            g["wm"][i] = _matmul_tn(dpm, h, f"dwm_{i}")
            g["wg"][i] = _matmul_tn(dpg, h, f"dwg_{i}")
        else:
            g["wm"][i] = _matmul_tn(h, dpm, f"dwm_{i}", GRAD_TILE[kind])
            g["wg"][i] = _matmul_tn(h, dpg, f"dwg_{i}", GRAD_TILE[kind])
        dx, g["norm_pre"][i] = _inproj_bwd(dpm, dpg, xi, npre, wts["mix"][i], wts["gate"][i], kind == 2, dx,
                                           f"inproj_bwd_{i}")
    g["mem_norm_w"], g["wkv"] = _memkv_bwd(mem, mnw, wts["wkv"], dkv)
    return loss[0, 0], dx, g


ANY = pl.BlockSpec(memory_space=pl.ANY)


def _place():
    return lax.axis_index("x"), lax.axis_index("y"), lax.axis_index("c")


def _gather_weights(big, vec):
    def body(big_ref, vec_ref, ob_ref, ov_ref, ici_send, ici_recv, d2d_send, d2d_recv, vec_send, vec_recv):
        x, y, c = _place()
        chip = 2 * x + y
        peers = [(1 - x, y), (x, 1 - y), (1 - x, 1 - y)]

        def rows(half, k):
            return pl.ds(half * HALF_ROWS + k * CHUNK_ROWS, CHUNK_ROWS)

        def over_ici(j, k, slab):
            px, py = peers[j]
            i = j * N_CHUNKS + k
            return pltpu.make_async_remote_copy(
                src_ref=big_ref.at[rows(c, k)], dst_ref=ob_ref.at[slab, rows(c, k)], send_sem=ici_send.at[i],
                recv_sem=ici_recv.at[i], device_id=(px, py, c), device_id_type=MESH)

        def over_d2d(j, k, half):
            px, py = peers[j]
            i = j * N_CHUNKS + k
            where = ob_ref.at[2 * px + py, rows(half, k)]
            return pltpu.make_async_remote_copy(
                src_ref=where, dst_ref=where, send_sem=d2d_send.at[i], recv_sem=d2d_recv.at[i],
                device_id=(x, y, 1 - c), device_id_type=MESH)

        def small(j, slab):
            px, py = peers[j]
            return pltpu.make_async_remote_copy(
                src_ref=vec_ref, dst_ref=ov_ref.at[slab], send_sem=vec_send.at[j], recv_sem=vec_recv.at[j],
                device_id=(px, py, c), device_id_type=MESH)

        sends = [small(j, chip) for j in range(3)] + [over_ici(j, k, chip) for k in range(N_CHUNKS) for j in range(3)]
        for cp in sends:
            cp.start()
        forwards = []
        for k in range(N_CHUNKS):
            for j, (px, py) in enumerate(peers):
                over_ici(j, k, 2 * px + py).wait_recv()
                forwards.append(over_d2d(j, k, c))
                forwards[-1].start()
        for k in range(N_CHUNKS):
            for j in range(3):
                over_d2d(j, k, 1 - c).wait_recv()
        for j, (px, py) in enumerate(peers):
            small(j, 2 * px + py).wait_recv()
        for cp in sends + forwards:
            cp.wait_send()

    n = 3 * N_CHUNKS
    dma = pltpu.SemaphoreType.DMA
    return pl.pallas_call(
        body, name="gather_weights", in_specs=[ANY, ANY], out_specs=[ANY, ANY],
        out_shape=[SDS((4,) + big.shape, big.dtype), SDS((4,) + vec.shape, vec.dtype)],
        scratch_shapes=[dma((n,)), dma((n,)), dma((n,)), dma((n,)), dma((3,)), dma((3,))])(big, vec)


HALF_ROWS = 3328
CHUNK_ROWS = 256
N_CHUNKS = HALF_ROWS // CHUNK_ROWS


def _swap_halves(gbig, small):
    def body(g_ref, s_ref, ob_ref, os_ref, send_sems, recv_sems):
        x, y, c = _place()
        copies = []
        for s in range(4):
            for k in range(N_CHUNKS):
                rows = pl.ds(k * CHUNK_ROWS, CHUNK_ROWS)
                copies.append(pltpu.make_async_remote_copy(
                    src_ref=g_ref.at[s, 1 - c, rows], dst_ref=ob_ref.at[s, rows], send_sem=send_sems.at[len(copies)],
                    recv_sem=recv_sems.at[len(copies)], device_id=(x, y, 1 - c), device_id_type=MESH))
        copies.append(pltpu.make_async_remote_copy(
            src_ref=s_ref, dst_ref=os_ref, send_sem=send_sems.at[len(copies)], recv_sem=recv_sems.at[len(copies)],
            device_id=(x, y, 1 - c), device_id_type=MESH))
        for cp in copies:
            cp.start()
        for cp in copies:
            cp.wait_recv()
        for cp in copies:
            cp.wait_send()

    n = 4 * N_CHUNKS + 1
    return pl.pallas_call(
        body, name="swap_halves", in_specs=[ANY, ANY], out_specs=[ANY, ANY],
        out_shape=[SDS((4, HALF_ROWS, D), F32), SDS(small.shape, F32)],
        scratch_shapes=[pltpu.SemaphoreType.DMA((n,)), pltpu.SemaphoreType.DMA((n,))])(gbig, small)


def _pair_sum(gbig, other):
    def body(g_ref, o_ref, pb_ref, own_ref):
        x, y, c = _place()
        v = jnp.where(c == 0, g_ref[0], g_ref[1]) + o_ref[...]
        pb_ref[...] = v.astype(BF16)

        @pl.when(pl.program_id(1) == 2 * x + y)
        def _():
            own_ref[...] = v

    return pl.pallas_call(
        body, name="pair_sum", grid=(N_CHUNKS, 4),
        in_specs=[pl.BlockSpec((None, 2, CHUNK_ROWS, D), lambda i, s: (s, 0, i, 0)),
                  pl.BlockSpec((None, CHUNK_ROWS, D), lambda i, s: (s, i, 0))],
        out_specs=[pl.BlockSpec((None, CHUNK_ROWS, D), lambda i, s: (s, i, 0)),
                   pl.BlockSpec((CHUNK_ROWS, D), lambda i, s: (i, 0))],
        out_shape=[SDS((4, HALF_ROWS, D), BF16), SDS((HALF_ROWS, D), F32)],
        compiler_params=_params(2, VMEM_LIMIT))(gbig, other)


def _add(a, b, name):
    def body(a_ref, b_ref, o_ref):
        o_ref[...] = a_ref[...] + b_ref[...]

    return pl.pallas_call(body, name=name, out_shape=SDS(a.shape, a.dtype), compiler_params=_params(0, VMEM_LIMIT))(a, b)


def _chip_exchange(pb, ps):
    n_small = ps.shape[0]

    def body(pb_ref, ps_ref, lb_ref, ls_ref, send_sems, recv_sems):
        x, y, c = _place()
        chip = 2 * x + y
        peers = [(1 - x, y), (x, 1 - y), (1 - x, 1 - y)]

        def copies(slab_of):
            out = []
            for j, (px, py) in enumerate(peers):
                for k in range(N_CHUNKS):
                    rows = pl.ds(k * CHUNK_ROWS, CHUNK_ROWS)
                    out.append(pltpu.make_async_remote_copy(
                        src_ref=pb_ref.at[2 * px + py, rows], dst_ref=lb_ref.at[slab_of(j), rows],
                        send_sem=send_sems.at[len(out)], recv_sem=recv_sems.at[len(out)], device_id=(px, py, c),
                        device_id_type=MESH))
                out.append(pltpu.make_async_remote_copy(
                    src_ref=ps_ref, dst_ref=ls_ref.at[slab_of(j)], send_sem=send_sems.at[len(out)],
                    recv_sem=recv_sems.at[len(out)], device_id=(px, py, c), device_id_type=MESH))
            return out

        sends = copies(lambda j: chip)
        for cp in sends:
            cp.start()
        for cp in copies(lambda j: 2 * peers[j][0] + peers[j][1]):
            cp.wait_recv()
        for cp in sends:
            cp.wait_send()

    n = 3 * (N_CHUNKS + 1)
    return pl.pallas_call(
        body, name="chip_exchange", in_specs=[ANY, ANY], out_specs=[ANY, ANY],
        out_shape=[SDS((4, HALF_ROWS, D), BF16), SDS((4, n_small, D), F32)],
        scratch_shapes=[pltpu.SemaphoreType.DMA((n,)), pltpu.SemaphoreType.DMA((n,))])(pb, ps)


def _chip_sum(own, land):
    def body(own_ref, l_ref, o_ref):
        chip = 2 * lax.axis_index("x") + lax.axis_index("y")
        acc = jnp.where(chip == 0, own_ref[...], l_ref[0].astype(F32))
        for s in range(1, 4):
            acc = acc + jnp.where(chip == s, own_ref[...], l_ref[s].astype(F32))
        o_ref[...] = acc

    return pl.pallas_call(
        body, name="chip_sum", grid=(N_CHUNKS,),
        in_specs=[pl.BlockSpec((CHUNK_ROWS, D), lambda i: (i, 0)), pl.BlockSpec((4, CHUNK_ROWS, D), lambda i: (0, i, 0))],
        out_specs=pl.BlockSpec((CHUNK_ROWS, D), lambda i: (i, 0)), out_shape=SDS((HALF_ROWS, D), F32),
        compiler_params=_params(1, VMEM_LIMIT))(own, land)


def _sum4(own, land):
    def body(own_ref, l_ref, o_ref):
        chip = 2 * lax.axis_index("x") + lax.axis_index("y")
        acc = jnp.where(chip == 0, own_ref[...], l_ref[0])
        for s in range(1, 4):
            acc = acc + jnp.where(chip == s, own_ref[...], l_ref[s])
        o_ref[...] = acc

    return pl.pallas_call(body, name="sum_small", out_shape=SDS(own.shape, own.dtype),
                          compiler_params=_params(0, VMEM_LIMIT))(own, land)


def _share_half(r):
    def body(r_ref, o_ref, send_sems, recv_sems):
        x, y, c = _place()
        copies = [pltpu.make_async_remote_copy(
            src_ref=r_ref.at[pl.ds(k * CHUNK_ROWS, CHUNK_ROWS)], dst_ref=o_ref.at[pl.ds(k * CHUNK_ROWS, CHUNK_ROWS)],
            send_sem=send_sems.at[k], recv_sem=recv_sems.at[k], device_id=(x, y, 1 - c), device_id_type=MESH)
            for k in range(N_CHUNKS)]
        for cp in copies:
            cp.start()
        for cp in copies:
            cp.wait_recv()
        for cp in copies:
            cp.wait_send()

    return pl.pallas_call(
        body, name="share_half", in_specs=[ANY], out_specs=ANY, out_shape=SDS((HALF_ROWS, D), F32),
        scratch_shapes=[pltpu.SemaphoreType.DMA((N_CHUNKS,)), pltpu.SemaphoreType.DMA((N_CHUNKS,))])(r)


C_ROWS = 1312
CLASSES = {"a": (1024, 256), "b": (2560, 640), "c": (C_ROWS // 2, C_ROWS // 2), "o": (768, 256), "kv": (128, 128)}
W_CLASSES = {"a": (1024, 256), "b": (512, 256), "c": (C_ROWS // 2, C_ROWS // 2), "o": (768, 256), "kv": (128, 128)}


def _chunk_list(specs):
    return [(k, r, chunk) for k, (half, chunk) in enumerate(specs) for r in range(0, half, chunk)]


def _gather_classes(arrs, specs, vec):
    n = len(arrs)
    chunks = _chunk_list(specs)
    nc = len(chunks)

    def body(*refs):
        ins, vec_ref, outs, ov_ref = refs[:n], refs[n], refs[n + 1:2 * n + 1], refs[2 * n + 1]
        ici_send, ici_recv, d2d_send, d2d_recv, vec_send, vec_recv = refs[2 * n + 2:]
        x, y, c = _place()
        chip = 2 * x + y
        peers = [(1 - x, y), (x, 1 - y), (1 - x, 1 - y)]

        def rows(ci, half):
            k, r, cnt = chunks[ci]
            return k, pl.ds(half * specs[k][0] + r, cnt)

        def over_ici(j, ci, slab):
            px, py = peers[j]
            k, rs = rows(ci, c)
            return pltpu.make_async_remote_copy(
                src_ref=ins[k].at[rs], dst_ref=outs[k].at[slab, rs], send_sem=ici_send.at[j * nc + ci],
                recv_sem=ici_recv.at[j * nc + ci], device_id=(px, py, c), device_id_type=MESH)

        def over_d2d(j, ci, half):
            px, py = peers[j]
            k, rs = rows(ci, half)
            where = outs[k].at[2 * px + py, rs]
            return pltpu.make_async_remote_copy(
                src_ref=where, dst_ref=where, send_sem=d2d_send.at[j * nc + ci], recv_sem=d2d_recv.at[j * nc + ci],
                device_id=(x, y, 1 - c), device_id_type=MESH)

        def small(j, slab):
            px, py = peers[j]
            return pltpu.make_async_remote_copy(
                src_ref=vec_ref, dst_ref=ov_ref.at[slab], send_sem=vec_send.at[j], recv_sem=vec_recv.at[j],
                device_id=(px, py, c), device_id_type=MESH)

        sends = [small(j, chip) for j in range(3)] + [over_ici(j, ci, chip) for ci in range(nc) for j in range(3)]
        for cp in sends:
            cp.start()
        forwards = []
        for ci in range(nc):
            for j, (px, py) in enumerate(peers):
                over_ici(j, ci, 2 * px + py).wait_recv()
                forwards.append(over_d2d(j, ci, c))
                forwards[-1].start()
        for ci in range(nc):
            for j in range(3):
                over_d2d(j, ci, 1 - c).wait_recv()
        for j, (px, py) in enumerate(peers):
            small(j, 2 * px + py).wait_recv()
        for cp in sends + forwards:
            cp.wait_send()

    dma = pltpu.SemaphoreType.DMA
    return pl.pallas_call(
        body, name="gather_weights", in_specs=[ANY] * (n + 1), out_specs=[ANY] * (n + 1),
        out_shape=[SDS((4,) + a.shape, a.dtype) for a in arrs] + [SDS((4,) + vec.shape, vec.dtype)],
        scratch_shapes=[dma((3 * nc,)), dma((3 * nc,)), dma((3 * nc,)), dma((3 * nc,)), dma((3,)), dma((3,))])(*arrs, vec)


def _swap_classes(grads, specs, small):
    n = len(grads)
    chunks = _chunk_list(specs)

    def body(*refs):
        ins, s_ref, outs, os_ref, send_sems, recv_sems = refs[:n], refs[n], refs[n + 1:2 * n + 1], *refs[2 * n + 1:]
        x, y, c = _place()
        copies = []
        for s in range(4):
            for k, r, cnt in chunks:
                copies.append(pltpu.make_async_remote_copy(
                    src_ref=ins[k].at[s, 1 - c, pl.ds(r, cnt)], dst_ref=outs[k].at[s, pl.ds(r, cnt)],
                    send_sem=send_sems.at[len(copies)], recv_sem=recv_sems.at[len(copies)],
                    device_id=(x, y, 1 - c), device_id_type=MESH))
        copies.append(pltpu.make_async_remote_copy(
            src_ref=s_ref, dst_ref=os_ref, send_sem=send_sems.at[len(copies)], recv_sem=recv_sems.at[len(copies)],
            device_id=(x, y, 1 - c), device_id_type=MESH))
        for cp in copies:
            cp.start()
        for cp in copies:
            cp.wait_recv()
        for cp in copies:
            cp.wait_send()

    m = 4 * len(chunks) + 1
    return pl.pallas_call(
        body, name="swap_halves", in_specs=[ANY] * (n + 1), out_specs=[ANY] * (n + 1),
        out_shape=[SDS((4, g.shape[2], g.shape[3]), F32) for g in grads] + [SDS(small.shape, F32)],
        scratch_shapes=[pltpu.SemaphoreType.DMA((m,)), pltpu.SemaphoreType.DMA((m,))])(*grads, small)


def _pair_sum_class(g, other, chunk, name):
    _, _, half, w = g.shape

    def body(g_ref, o_ref, pb_ref, own_ref):
        x, y, c = _place()
        v = jnp.where(c == 0, g_ref[0], g_ref[1]) + o_ref[...]
        pb_ref[...] = v.astype(BF16)

        @pl.when(pl.program_id(1) == 2 * x + y)
        def _():
            own_ref[...] = v

    return pl.pallas_call(
        body, name=name, grid=(half // chunk, 4),
        in_specs=[pl.BlockSpec((None, 2, chunk, w), lambda i, s: (s, 0, i, 0)),
                  pl.BlockSpec((None, chunk, w), lambda i, s: (s, i, 0))],
        out_specs=[pl.BlockSpec((None, chunk, w), lambda i, s: (s, i, 0)), pl.BlockSpec((chunk, w), lambda i, s: (i, 0))],
        out_shape=[SDS((4, half, w), BF16), SDS((half, w), F32)],
        compiler_params=_params(2, VMEM_LIMIT))(g, other)


def _exchange_classes(pbs, specs, ps):
    n = len(pbs)
    chunks = _chunk_list(specs)
    per_peer = len(chunks) + 1

    def body(*refs):
        ins, ps_ref, outs, ls_ref, send_sems, recv_sems = refs[:n], refs[n], refs[n + 1:2 * n + 1], *refs[2 * n + 1:]
        x, y, c = _place()
        chip = 2 * x + y
        peers = [(1 - x, y), (x, 1 - y), (1 - x, 1 - y)]

        def copies(slab_of):
            out = []
            for j, (px, py) in enumerate(peers):
                for k, r, cnt in chunks:
                    out.append(pltpu.make_async_remote_copy(
                        src_ref=ins[k].at[2 * px + py, pl.ds(r, cnt)], dst_ref=outs[k].at[slab_of(j), pl.ds(r, cnt)],
                        send_sem=send_sems.at[len(out)], recv_sem=recv_sems.at[len(out)], device_id=(px, py, c),
                        device_id_type=MESH))
                out.append(pltpu.make_async_remote_copy(
                    src_ref=ps_ref, dst_ref=ls_ref.at[slab_of(j)], send_sem=send_sems.at[len(out)],
                    recv_sem=recv_sems.at[len(out)], device_id=(px, py, c), device_id_type=MESH))
            return out

        sends = copies(lambda j: chip)
        for cp in sends:
            cp.start()
        for cp in copies(lambda j: 2 * peers[j][0] + peers[j][1]):
            cp.wait_recv()
        for cp in sends:
            cp.wait_send()

    m = 3 * per_peer
    return pl.pallas_call(
        body, name="chip_exchange", in_specs=[ANY] * (n + 1), out_specs=[ANY] * (n + 1),
        out_shape=[SDS(p.shape, BF16) for p in pbs] + [SDS((4,) + ps.shape, F32)],
        scratch_shapes=[pltpu.SemaphoreType.DMA((m,)), pltpu.SemaphoreType.DMA((m,))])(*pbs, ps)


def _chip_sum_class(own, land, chunk, name):
    half, w = own.shape

    def body(own_ref, l_ref, o_ref):
        chip = 2 * lax.axis_index("x") + lax.axis_index("y")
        acc = jnp.where(chip == 0, own_ref[...], l_ref[0].astype(F32))
        for s in range(1, 4):
            acc = acc + jnp.where(chip == s, own_ref[...], l_ref[s].astype(F32))
        o_ref[...] = acc

    return pl.pallas_call(
        body, name=name, grid=(half // chunk,),
        in_specs=[pl.BlockSpec((chunk, w), lambda i: (i, 0)), pl.BlockSpec((4, chunk, w), lambda i: (0, i, 0))],
        out_specs=pl.BlockSpec((chunk, w), lambda i: (i, 0)), out_shape=SDS((half, w), F32),
        compiler_params=_params(1, VMEM_LIMIT))(own, land)


def _share_classes(rs, specs):
    n = len(rs)
    chunks = _chunk_list(specs)

    def body(*refs):
        ins, outs, send_sems, recv_sems = refs[:n], refs[n:2 * n], *refs[2 * n:]
        x, y, c = _place()
        copies = [pltpu.make_async_remote_copy(
            src_ref=ins[k].at[pl.ds(r, cnt)], dst_ref=outs[k].at[pl.ds(r, cnt)], send_sem=send_sems.at[i],
            recv_sem=recv_sems.at[i], device_id=(x, y, 1 - c), device_id_type=MESH)
            for i, (k, r, cnt) in enumerate(chunks)]
        for cp in copies:
            cp.start()
        for cp in copies:
            cp.wait_recv()
        for cp in copies:
            cp.wait_send()

    m = len(chunks)
    return pl.pallas_call(
        body, name="share_half", in_specs=[ANY] * n, out_specs=[ANY] * n, out_shape=[SDS(r.shape, F32) for r in rs],
        scratch_shapes=[pltpu.SemaphoreType.DMA((m,)), pltpu.SemaphoreType.DMA((m,))])(*rs)


_SMALL = ["mem_norm_w", "norm_pre", "norm_post", "a_ln_w", "a_ln_b", "a_w_s", "a_b_s", "b_conv_w", "c_conv_w",
          "c_a_log", "c_dt_bias", "c_o_norm_w"]
_SMALL_SHAPES = {"mem_norm_w": (D,), "norm_pre": (4, D), "norm_post": (4, D), "a_ln_w": (2, D), "a_ln_b": (2, D),
                 "a_w_s": (2, 8, HD, HD), "a_b_s": (2, 8, HD), "b_conv_w": (1, 3, D), "c_conv_w": (1, 4, 3 * D),
                 "c_a_log": (1, 8), "c_dt_bias": (1, 8), "c_o_norm_w": (1, HD)}
_SHARDED_SMALL = {"a_ln_w": D // 4, "a_ln_b": D // 4, "b_conv_w": D // 4, "c_conv_w": 3 * D // 4}
_ROWS = [2048, 1280, 1284, 1536, 256]
_BIG_ROWS = sum(_ROWS)
_BIG_PAD = 6416
_SMALL_ROWS = 288


def _size(shape):
    n = 1
    for d in shape:
        n *= d
    return n


def kernel(x, mem, mem_norm_w, w_mem_kv, norm_pre, norm_post, w_out, a_w_in, a_ln_w, a_ln_b, a_w_s, a_b_s, b_w_in, b_conv_w, c_w_in, c_conv_w, c_a_log, c_dt_bias, c_o_norm_w, loss_target, m_mem_norm_w, m_w_mem_kv, m_norm_pre, m_norm_post, m_w_out, m_a_w_in, m_a_ln_w, m_a_ln_b, m_a_w_s, m_a_b_s, m_b_w_in, m_b_conv_w, m_c_w_in, m_c_conv_w, m_c_a_log, m_c_dt_bias, m_c_o_norm_w, v_mem_norm_w, v_w_mem_kv, v_norm_pre, v_norm_post, v_w_out, v_a_w_in, v_a_ln_w, v_a_ln_b, v_a_w_s, v_a_b_s, v_b_w_in, v_b_conv_w, v_c_w_in, v_c_conv_w, v_c_a_log, v_c_dt_bias, v_c_o_norm_w):
    names = ["mem_norm_w", "w_mem_kv", "norm_pre", "norm_post", "w_out", "a_w_in", "a_ln_w", "a_ln_b", "a_w_s", "a_b_s",
             "b_w_in", "b_conv_w", "c_w_in", "c_conv_w", "c_a_log", "c_dt_bias", "c_o_norm_w"]
    w = dict(zip(names, [mem_norm_w, w_mem_kv, norm_pre, norm_post, w_out, a_w_in, a_ln_w, a_ln_b, a_w_s, a_b_s, b_w_in,
                         b_conv_w, c_w_in, c_conv_w, c_a_log, c_dt_bias, c_o_norm_w]))
    m = dict(zip(names, [m_mem_norm_w, m_w_mem_kv, m_norm_pre, m_norm_post, m_w_out, m_a_w_in, m_a_ln_w, m_a_ln_b, m_a_w_s,
                         m_a_b_s, m_b_w_in, m_b_conv_w, m_c_w_in, m_c_conv_w, m_c_a_log, m_c_dt_bias, m_c_o_norm_w]))
    v = dict(zip(names, [v_mem_norm_w, v_w_mem_kv, v_norm_pre, v_norm_post, v_w_out, v_a_w_in, v_a_ln_w, v_a_ln_b, v_a_w_s,
                         v_a_b_s, v_b_w_in, v_b_conv_w, v_c_w_in, v_c_conv_w, v_c_a_log, v_c_dt_bias, v_c_o_norm_w]))
    chip = 2 * lax.axis_index("x") + lax.axis_index("y")

    def rows_of_ct(a):
        return a[0].T

    c_rows = jnp.pad(rows_of_ct(c_w_in), ((0, C_ROWS - 1284), (0, 0)))
    mine_w = [a_w_in.reshape(2 * D, D), b_w_in.reshape(D, 1280), c_rows, w_out.reshape(D_CAT, D), w_mem_kv]
    mine_w = [a.astype(BF16) for a in mine_w]
    vec = jnp.concatenate([a_ln_w.reshape(-1), a_ln_b.reshape(-1), b_conv_w.reshape(-1), c_conv_w.reshape(-1)])
    vec = jnp.pad(vec, (0, 8 * D - vec.shape[0])).reshape(8, D)
    *gathered, gvec = _gather_classes(mine_w, [W_CLASSES[k] for k in W_CLASSES], vec)
    ga, gb, gc, go, gkv = [lax.dynamic_update_slice(g_, a[None], (chip, 0, 0)) for g_, a in zip(gathered, mine_w)]
    gvec = lax.dynamic_update_slice(gvec, vec[None], (chip, 0, 0))
    ga = ga.reshape(4, 2, D, D)
    go = go.reshape(4, 4, 384, D)
    fct = gc[:, :1284].reshape(5136, D)
    gv = gvec.reshape(4, 8 * D)
    sm = {"mem_norm_w": mem_norm_w, "norm_pre": norm_pre, "norm_post": norm_post, "a_w_s": a_w_s, "a_b_s": a_b_s,
          "c_a_log": c_a_log, "c_dt_bias": c_dt_bias, "c_o_norm_w": c_o_norm_w,
          "a_ln_w": gv[:, 0:512].reshape(4, 2, 256).transpose(1, 0, 2).reshape(2, D),
          "a_ln_b": gv[:, 512:1024].reshape(4, 2, 256).transpose(1, 0, 2).reshape(2, D),
          "b_conv_w": gv[:, 1024:1792].reshape(4, 1, 3, 256).transpose(1, 2, 0, 3).reshape(1, 3, D),
          "c_conv_w": gv[:, 1792:4864].reshape(4, 1, 4, 768).transpose(1, 2, 0, 3).reshape(1, 4, 3 * D)}
    c_ab = jnp.concatenate([fct[3 * D:3 * D + 16], jnp.zeros((AB_PAD - 16, D), BF16)], axis=0)
    wts = {"wkv": gkv.reshape(D, 2 * D_XA), "wo": [go[:, i].reshape(D_CAT, D) for i in range(DEPTH)],
           "mix": [[ga[0, 0], ga[1, 0]], [gb[0], gb[1], gb[2][:, :512]], [fct[:3 * D], c_ab], [ga[0, 1], ga[1, 1]]],
           "gate": [[ga[2, 0], ga[3, 0]], [gb[2][:, 512:], gb[3]], [fct[3 * D + 16:]], [ga[2, 1], ga[3, 1]]]}

    loss, dx, g = _local_step(x[0], mem[0], loss_target[0], wts, sm)
    loss = lax.psum(loss, ("x", "y", "c"))

    gct = jnp.concatenate([g["wm"][2][:3 * D + 16], g["wg"][2]], axis=0).reshape(4, 1284, D)
    by_chip = {
        "a": jnp.stack([jnp.concatenate([g["wm"][i], g["wg"][i]], axis=0) for i in (0, 3)], axis=1),
        "b": jnp.concatenate([g["wm"][1], g["wg"][1]], axis=0),
        "c": jnp.pad(gct, ((0, 0), (0, C_ROWS - 1284), (0, 0))),
        "o": jnp.stack(g["wo"]).reshape(4, 4, 384, D).transpose(1, 0, 2, 3),
        "kv": g["wkv"]}
    specs = [CLASSES[k] for k in CLASSES]
    halves = [by_chip[k].reshape(4, 2, CLASSES[k][0], -1) for k in CLASSES]
    gs = {"mem_norm_w": g["mem_norm_w"], "norm_pre": jnp.concatenate(g["norm_pre"]),
          "norm_post": jnp.concatenate(g["norm_post"])}
    for n in _SMALL[3:]:
        gs[n] = jnp.stack([g[n][j] for j in sorted(g[n])])
    flat = jnp.concatenate([gs[n].reshape(-1) for n in _SMALL])
    small = jnp.pad(flat, (0, _SMALL_ROWS * D - flat.shape[0])).reshape(_SMALL_ROWS, D)
    *others, other_small = _swap_classes(halves, specs, small)
    pairs = [_pair_sum_class(h, o, CLASSES[k][1], f"pair_sum_{k}") for k, h, o in zip(CLASSES, halves, others)]
    pair_small = _add(small, other_small, "pair_sum_small")
    *lands, land_small = _exchange_classes([p[0] for p in pairs], specs, pair_small)
    mine = [_chip_sum_class(p[1], land, CLASSES[k][1], f"chip_sum_{k}") for k, p, land in zip(CLASSES, pairs, lands)]
    theirs = _share_classes(mine, specs)
    south = lax.axis_index("c") == 0
    sh = {k: jnp.concatenate([jnp.where(south, a, b), jnp.where(south, b, a)], axis=0)
          for k, a, b in zip(CLASSES, mine, theirs)}
    grads = {"a_w_in": sh["a"].reshape(a_w_in.shape),
             "b_w_in": sh["b"].reshape(5, D, 256).transpose(1, 0, 2).reshape(b_w_in.shape),
             "c_w_in": sh["c"][:1284], "w_out": sh["o"].reshape(w_out.shape), "w_mem_kv": sh["kv"]}
    flat = _sum4(pair_small, land_small).reshape(-1)
    off = 0
    for n in _SMALL:
        shape = _SMALL_SHAPES[n]
        full = flat[off:off + _size(shape)].reshape(shape)
        off += _size(shape)
        if n in _SHARDED_SMALL:
            full = lax.dynamic_slice_in_dim(full, chip * _SHARDED_SMALL[n], _SHARDED_SMALL[n], axis=len(shape) - 1)
        grads[n] = full

    delta, new_m, new_v = {}, {}, {}
    for n in names:
        shape = w[n].shape
        if n == "c_w_in":
            d_, m_, v_ = _adamw(rows_of_ct(w[n]), grads[n], rows_of_ct(m[n]), rows_of_ct(v[n]), f"adamw_{n}")
            delta[n], new_m[n], new_v[n], grads[n] = d_.T[None], m_.T[None], v_.T[None], grads[n].T[None]
            continue
        view = (1, shape[0]) if len(shape) == 1 else (_size(shape[:-1]), shape[-1])
        d_, m_, v_ = _adamw(w[n].reshape(view), grads[n].reshape(view), m[n].reshape(view), v[n].reshape(view),
                            f"adamw_{n}")
        delta[n], new_m[n], new_v[n] = d_.reshape(shape), m_.reshape(shape), v_.reshape(shape)
    return (loss, dx[None], *[grads[n].reshape(w[n].shape) for n in names], *[delta[n] for n in names],
            *[new_m[n] for n in names], *[new_v[n] for n in names])
```

```python
import functools

import jax
import jax.numpy as jnp
from jax import lax
from jax.experimental import pallas as pl
from jax.experimental.pallas import tpu as pltpu

F32 = jnp.float32
BF16 = jnp.bfloat16
HI = lax.Precision.HIGHEST
MESH = pl.DeviceIdType.MESH
SDS = jax.ShapeDtypeStruct

D = 1024
D_XA = 512
D_CAT = 1536
N_MEM = 256
HD = 128
DEPTH = 4
EPS = 1e-6
TT = 512
DN_C = 64
DN_TB = 256
HALO = 8
AB_PAD = 128
VMEM_LIMIT = 56 * 1024 * 1024
GRAD_TILE = {0: 1024, 1: 256}

ADAM_LR, ADAM_B1, ADAM_B2, ADAM_EPS, ADAM_WD, ADAM_STEP = 0.001, 0.9, 0.999, 1e-08, 0.01, 10


def _params(n_grid, vmem=None):
    return pltpu.CompilerParams(dimension_semantics=("arbitrary",) * n_grid, vmem_limit_bytes=vmem)


def _rms(x, w):
    return x * lax.rsqrt(jnp.mean(x * x, axis=-1, keepdims=True) + EPS) * w


def _dot_nn(a, b):
    return jnp.dot(a.astype(BF16), b.astype(BF16), preferred_element_type=F32)


def _dot_nt(a, b):
    return lax.dot_general(a.astype(BF16), b.astype(BF16), (((1,), (1,)), ((), ())), preferred_element_type=F32)


def _dot_tn(a, b):
    return lax.dot_general(a.astype(BF16), b.astype(BF16), (((0,), (0,)), ((), ())), preferred_element_type=F32)


@jax.custom_vjp
def mm(a, b):
    return _dot_nn(a, b)


mm.defvjp(lambda a, b: (_dot_nn(a, b), (a, b)), lambda r, g: (_dot_nt(g, r[1]), _dot_tn(r[0], g)))


@jax.custom_vjp
def mm_nt(a, b):
    return _dot_nt(a, b)


mm_nt.defvjp(lambda a, b: (_dot_nt(a, b), (a, b)), lambda r, g: (_dot_nn(g, r[1]), _dot_tn(g, r[0])))


@jax.custom_vjp
def mm_tn(a, b):
    return _dot_tn(a, b)


mm_tn.defvjp(lambda a, b: (_dot_tn(a, b), (a, b)), lambda r, g: (_dot_nt(r[1], g), _dot_nn(r[0], g)))


def _dot_hi(a, b):
    return jnp.dot(a, b, precision=HI, preferred_element_type=F32)


def _row_spec(width, tile=TT):
    return pl.BlockSpec((tile, width), lambda i: (i, 0))


def _full_spec(shape):
    return pl.BlockSpec(shape, lambda *_: (0,) * len(shape))


def _widths(blocks, transposed):
    return [b.shape[0 if transposed else 1] for b in blocks]


def _inproj_fwd(x, nw, mix, gate, transposed, name):
    T, nm = x.shape[0], len(mix)
    M, G = sum(_widths(mix, transposed)), sum(_widths(gate, transposed))

    def body(x_ref, nw_ref, *refs):
        blocks, (pm_ref, pg_ref, h_ref) = refs[:-3], refs[-3:]
        h = _rms(x_ref[...], nw_ref[...]).astype(BF16)
        h_ref[...] = h
        for p_ref, group in ((pm_ref, blocks[:nm]), (pg_ref, blocks[nm:])):
            off = 0
            for w_ref in group:
                w = w_ref.shape[0 if transposed else 1]
                p_ref[:, off:off + w] = _dot_nt(h, w_ref[...]) if transposed else _dot_nn(h, w_ref[...])
                off += w

    return pl.pallas_call(
        body, name=name, grid=(T // TT,),
        in_specs=[_row_spec(D), _full_spec((1, D))] + [_full_spec(b.shape) for b in mix + gate],
        out_specs=[_row_spec(M), _row_spec(G), _row_spec(D)],
        out_shape=[SDS((T, M), F32), SDS((T, G), F32), SDS((T, D), BF16)],
        compiler_params=_params(1, VMEM_LIMIT))(x, nw, *mix, *gate)


def _inproj_bwd(dpm, dpg, x, nw, mix, gate, transposed, dxc, name):
    T, nm = x.shape[0], len(mix)
    M, G = sum(_widths(mix, transposed)), sum(_widths(gate, transposed))

    def body(dpm_ref, dpg_ref, x_ref, nw_ref, *refs):
        blocks, (dxc_ref, dx_ref, dnw_ref) = refs[:-3], refs[-3:]
        dh = None
        for dp_ref, group in ((dpm_ref, blocks[:nm]), (dpg_ref, blocks[nm:])):
            off = 0
            for w_ref in group:
                w = w_ref.shape[0 if transposed else 1]
                dp = dp_ref[:, off:off + w]
                part = _dot_nn(dp, w_ref[...]) if transposed else _dot_nt(dp, w_ref[...])
                dh = part if dh is None else dh + part
                off += w
        _, vjp = jax.vjp(_rms, x_ref[...], nw_ref[...])
        dxr, dnw = vjp(dh)
        dx_ref[...] = dxc_ref[...] + dxr

        @pl.when(pl.program_id(0) == 0)
        def _():
            dnw_ref[...] = jnp.zeros_like(dnw_ref)
        dnw_ref[...] += dnw

    return pl.pallas_call(
        body, name=name, grid=(T // TT,),
        in_specs=[_row_spec(M), _row_spec(G), _row_spec(D), _full_spec((1, D))]
        + [_full_spec(b.shape) for b in mix + gate] + [_row_spec(D)],
        out_specs=[_row_spec(D), _full_spec((1, D))],
        out_shape=[SDS((T, D), F32), SDS((1, D), F32)],
        compiler_params=_params(1, VMEM_LIMIT))(dpm, dpg, x, nw, *mix, *gate, dxc)


def _matmul_tn(a, b, name, sub=None):
    T, K = a.shape
    N = b.shape[1]
    tn = 1024 if N % 1024 == 0 else (640 if N % 640 == 0 else N)
    tt = min(1024, T)
    n_sub = 1 if sub is None else tn // sub

    def body(a_ref, b_ref, o_ref):
        @pl.when(pl.program_id(1) == 0)
        def _():
            o_ref[...] = jnp.zeros_like(o_ref)
        res = _dot_tn(a_ref[...], b_ref[...])
        if sub is None:
            o_ref[...] += res
        else:
            for i in range(n_sub):
                o_ref[i] += res[:, i * sub:(i + 1) * sub]

    if sub is None:
        out_spec, out_shape = pl.BlockSpec((K, tn), lambda j, t: (0, j)), SDS((K, N), F32)
    else:
        out_spec, out_shape = pl.BlockSpec((n_sub, K, sub), lambda j, t: (j, 0, 0)), SDS((N // sub, K, sub), F32)
    return pl.pallas_call(
        body, name=name, grid=(N // tn, T // tt),
        in_specs=[pl.BlockSpec((tt, K), lambda j, t: (t, 0)), pl.BlockSpec((tt, tn), lambda j, t: (t, j))],
        out_specs=out_spec, out_shape=out_shape,
        compiler_params=_params(2, VMEM_LIMIT))(a, b)


def _memkv_fn(mem, w, wkv):
    return mm(_rms(mem, w), wkv)


def _memkv_fwd(mem, w, wkv):
    def body(mem_ref, w_ref, wkv_ref, kv_ref):
        kv_ref[...] = _memkv_fn(mem_ref[...], w_ref[...], wkv_ref[...])

    return pl.pallas_call(body, name="memkv_fwd", out_shape=SDS((N_MEM, 2 * D_XA), F32),
                          compiler_params=_params(0, VMEM_LIMIT))(mem, w, wkv)


def _memkv_bwd(mem, w, wkv, dkv):
    def body(mem_ref, w_ref, wkv_ref, dkv_ref, dw_ref, dwkv_ref):
        _, vjp = jax.vjp(functools.partial(_memkv_fn, mem_ref[...]), w_ref[...], wkv_ref[...].astype(F32))
        dw, dwkv = vjp(dkv_ref[...])
        dw_ref[...] = dw
        dwkv_ref[...] = dwkv

    return pl.pallas_call(body, name="memkv_bwd", out_shape=[SDS((1, D), F32), SDS((D, 2 * D_XA), F32)],
                          compiler_params=_params(0, VMEM_LIMIT))(mem, w, wkv, dkv)


def _attn_gate(ymix, qx, z, *kvs):
    outs = []
    for j in range(4):
        s = mm_nt(qx[:, j * HD:(j + 1) * HD], kvs[j]) * (HD ** -0.5)
        e = jnp.exp(s - lax.stop_gradient(jnp.max(s, axis=-1, keepdims=True)))
        outs.append(mm(e / jnp.sum(e, axis=-1, keepdims=True), kvs[4 + j]))
    return jnp.concatenate([ymix] + outs, axis=1) * jax.nn.silu(z)


def _kv_blocks(kv_ref):
    return [kv_ref[:, j * HD:(j + 1) * HD] for j in range(8)]


def _ag_fwd(ymix, pg, kv, name):
    T = ymix.shape[0]

    def body(ymix_ref, pg_ref, kv_ref, ycat_ref):
        ycat_ref[...] = _attn_gate(ymix_ref[...], pg_ref[:, :D_XA], pg_ref[:, D_XA:], *_kv_blocks(kv_ref)).astype(BF16)

    return pl.pallas_call(
        body, name=name, grid=(T // TT,),
        in_specs=[_row_spec(D), _row_spec(D_XA + D_CAT), _full_spec((N_MEM, 2 * D_XA))],
        out_specs=_row_spec(D_CAT), out_shape=SDS((T, D_CAT), BF16),
        compiler_params=_params(1, VMEM_LIMIT))(ymix, pg, kv)


def _ag_bwd(dycat, ymix, pg, kv, dkv_in, name):
    T = ymix.shape[0]

    def body(dycat_ref, ymix_ref, pg_ref, kv_ref, dkvin_ref, dymix_ref, dpg_ref, dkv_ref):
        _, vjp = jax.vjp(_attn_gate, ymix_ref[...], pg_ref[:, :D_XA], pg_ref[:, D_XA:], *_kv_blocks(kv_ref))
        g = vjp(dycat_ref[...])
        dymix_ref[...] = g[0]
        dpg_ref[:, :D_XA] = g[1].astype(BF16)
        dpg_ref[:, D_XA:] = g[2].astype(BF16)

        @pl.when(pl.program_id(0) == 0)
        def _():
            dkv_ref[...] = dkvin_ref[...]
        for j in range(8):
            dkv_ref[:, j * HD:(j + 1) * HD] += g[3 + j]

    return pl.pallas_call(
        body, name=name, grid=(T // TT,),
        in_specs=[_row_spec(D_CAT), _row_spec(D), _row_spec(D_XA + D_CAT), _full_spec((N_MEM, 2 * D_XA)),
                  _full_spec((N_MEM, 2 * D_XA))],
        out_specs=[_row_spec(D), _row_spec(D_XA + D_CAT), _full_spec((N_MEM, 2 * D_XA))],
        out_shape=[SDS((T, D), F32), SDS((T, D_XA + D_CAT), BF16), SDS((N_MEM, 2 * D_XA), F32)],
        compiler_params=_params(1, VMEM_LIMIT))(dycat, ymix, pg, kv, dkv_in)


def _outproj_fwd(ycat, wo, x, nw, name):
    T = x.shape[0]

    def body(ycat_ref, wo_ref, x_ref, nw_ref, o_ref, xn_ref):
        o = jnp.dot(ycat_ref[...], wo_ref[...], preferred_element_type=F32)
        o_ref[...] = o
        xn_ref[...] = x_ref[...] + _rms(o, nw_ref[...])

    return pl.pallas_call(
        body, name=name, grid=(T // TT,),
        in_specs=[_row_spec(D_CAT), _full_spec((D_CAT, D)), _row_spec(D), _full_spec((1, D))],
        out_specs=[_row_spec(D), _row_spec(D)], out_shape=[SDS((T, D), F32), SDS((T, D), F32)],
        compiler_params=_params(1, VMEM_LIMIT))(ycat, wo, x, nw)


def _outproj_bwd(dxo, o, nw, wo, name):
    T = dxo.shape[0]

    def body(dxo_ref, o_ref, nw_ref, wo_ref, dycat_ref, dobf_ref, dnw_ref):
        _, vjp = jax.vjp(_rms, o_ref[...], nw_ref[...])
        do, dnw = vjp(dxo_ref[...])
        dobf = do.astype(BF16)
        dobf_ref[...] = dobf
        dycat_ref[...] = _dot_nt(dobf, wo_ref[...])

        @pl.when(pl.program_id(0) == 0)
        def _():
            dnw_ref[...] = jnp.zeros_like(dnw_ref)
        dnw_ref[...] += dnw

    return pl.pallas_call(
        body, name=name, grid=(T // TT,),
        in_specs=[_row_spec(D), _row_spec(D), _full_spec((1, D)), _full_spec((D_CAT, D))],
        out_specs=[_row_spec(D_CAT), _row_spec(D), _full_spec((1, D))],
        out_shape=[SDS((T, D_CAT), F32), SDS((T, D), BF16), SDS((1, D), F32)],
        compiler_params=_params(1, VMEM_LIMIT))(dxo, o, nw, wo)


def _loss_head(xl, target):
    T = xl.shape[0]

    def body(x_ref, t_ref, loss_ref, dx_ref):
        err = x_ref[...] - t_ref[...]
        dx_ref[...] = err * (1.0 / D)

        @pl.when(pl.program_id(0) == 0)
        def _():
            loss_ref[...] = jnp.zeros_like(loss_ref)
        part = jnp.sum(jnp.sum(err * err, axis=1, keepdims=True), axis=0, keepdims=True) * (0.5 / D)
        loss_ref[...] += jnp.broadcast_to(part, loss_ref.shape)

    return pl.pallas_call(
        body, name="loss_head", grid=(T // TT,),
        in_specs=[_row_spec(D), _row_spec(D)],
        out_specs=[_full_spec((8, 128)), _row_spec(D)], out_shape=[SDS((8, 128), F32), SDS((T, D), F32)],
        compiler_params=_params(1))(xl, target)


def _gmlp_pre(u, v, lnw, lnb):
    vg = jax.nn.gelu(v)
    xc = vg - jnp.mean(vg, axis=-1, keepdims=True)
    vl = xc * lax.rsqrt(jnp.mean(xc * xc, axis=-1, keepdims=True) + EPS) * lnw + lnb
    return jax.nn.gelu(u), vl


def _tril(n, strict=False):
    r = lax.broadcasted_iota(jnp.int32, (n, n), 0)
    c = lax.broadcasted_iota(jnp.int32, (n, n), 1)
    return (r > c) if strict else (r >= c)


def _gmlp_fwd(pm, lnw, lnb, ws, bs3, name):
    T = pm.shape[0]

    def body(pm_ref, lnw_ref, lnb_ref, ws_ref, bs_ref, y_ref):
        ug, vl = _gmlp_pre(pm_ref[:, :D], pm_ref[:, D:], lnw_ref[...], lnb_ref[...])
        mask = _tril(HD)
        for g in range(8):
            w = jnp.where(mask, ws_ref[g], 0.0)
            for c in range(TT // HD):
                rows, cols = slice(c * HD, (c + 1) * HD), slice(g * HD, (g + 1) * HD)
                y_ref[rows, cols] = ug[rows, cols] * (_dot_nn(w, vl[rows, cols]) + bs_ref[g])

    return pl.pallas_call(
        body, name=name, grid=(T // TT,),
        in_specs=[_row_spec(2 * D), _full_spec((1, D)), _full_spec((1, D)), _full_spec((8, HD, HD)),
                  _full_spec((8, HD, HD))],
        out_specs=_row_spec(D), out_shape=SDS((T, D), F32),
        compiler_params=_params(1, VMEM_LIMIT))(pm, lnw, lnb, ws, bs3)


def _gmlp_bwd(dy, pm, lnw, lnb, ws, bs3, name):
    T = pm.shape[0]
    n_t = T // TT

    def body(dy_ref, pm_ref, lnw_ref, lnb_ref, ws_ref, bs_ref, dpm_ref, dlnw_ref, dlnb_ref, dws_ref, dbs_ref,
             dug_scr, dvl_scr, dbs_scr):
        i = pl.program_id(0)

        @pl.when(i == 0)
        def _():
            dlnw_ref[...] = jnp.zeros_like(dlnw_ref)
            dlnb_ref[...] = jnp.zeros_like(dlnb_ref)
            dws_ref[...] = jnp.zeros_like(dws_ref)
            dbs_scr[...] = jnp.zeros_like(dbs_scr)

        (ug, vl), vjp = jax.vjp(_gmlp_pre, pm_ref[:, :D], pm_ref[:, D:], lnw_ref[...], lnb_ref[...])
        mask = _tril(HD)
        for g in range(8):
            w = jnp.where(mask, ws_ref[g], 0.0)
            dw = jnp.zeros((HD, HD), F32)
            db = jnp.zeros((HD, HD), F32)
            for c in range(TT // HD):
                rows, cols = slice(c * HD, (c + 1) * HD), slice(g * HD, (g + 1) * HD)
                dyb, vlb = dy_ref[rows, cols], vl[rows, cols]
                sp = _dot_nn(w, vlb) + bs_ref[g]
                dsp = dyb * ug[rows, cols]
                dug_scr[rows, cols] = dyb * sp
                dvl_scr[rows, cols] = _dot_tn(w, dsp)
                dw += _dot_nt(dsp, vlb)
                db += dsp
            dws_ref[g] += jnp.where(mask, dw, 0.0)
            dbs_scr[g] += db
        du, dv, dlnw, dlnb = vjp((dug_scr[...], dvl_scr[...]))
        dpm_ref[:, :D] = du.astype(BF16)
        dpm_ref[:, D:] = dv.astype(BF16)
        dlnw_ref[...] += dlnw
        dlnb_ref[...] += dlnb

        @pl.when(i == n_t - 1)
        def _():
            for g in range(8):
                dbs_ref[g] = jnp.broadcast_to(jnp.sum(dbs_scr[g], axis=1, keepdims=True), (HD, HD))

    return pl.pallas_call(
        body, name=name, grid=(n_t,),
        in_specs=[_row_spec(D), _row_spec(2 * D), _full_spec((1, D)), _full_spec((1, D)), _full_spec((8, HD, HD)),
                  _full_spec((8, HD, HD))],
        out_specs=[_row_spec(2 * D), _full_spec((1, D)), _full_spec((1, D)), _full_spec((8, HD, HD)),
                   _full_spec((8, HD, HD))],
        out_shape=[SDS((T, 2 * D), BF16), SDS((1, D), F32), SDS((1, D), F32), SDS((8, HD, HD), F32),
                   SDS((8, HD, HD), F32)],
        scratch_shapes=[pltpu.VMEM((TT, D), F32), pltpu.VMEM((TT, D), F32), pltpu.VMEM((8, HD, HD), F32)],
        compiler_params=_params(1, VMEM_LIMIT))(dy, pm, lnw, lnb, ws, bs3)


def _prev_spec(width, T):
    return pl.BlockSpec((HALO, width), lambda i: (jnp.maximum(i * (TT // HALO) - 1, 0), 0))


def _next_spec(width, T):
    return pl.BlockSpec((HALO, width), lambda i: (jnp.minimum((i + 1) * (TT // HALO), T // HALO - 1), 0))


def _rows_before(ext, j):
    return ext[HALO:] if j == 0 else pltpu.roll(ext, j, 0)[HALO:]


def _rows_after(ext, j):
    n = ext.shape[0]
    return ext[:n - HALO] if j == 0 else pltpu.roll(ext, n - j, 0)[:n - HALO]


def _conv_apply(ext_s, w):
    K = w.shape[0]
    y = _rows_before(ext_s, K - 1) * w[0:1]
    for k in range(1, K):
        y = y + _rows_before(ext_s, K - 1 - k) * w[k:k + 1]
    return y


def _conv_grads(ext_s, ext_dy, w):
    K = w.shape[0]
    dy = ext_dy[:ext_dy.shape[0] - HALO]
    ds = _rows_after(ext_dy, K - 1) * w[0:1]
    dws = [jnp.sum(dy * _rows_before(ext_s, K - 1), axis=0, keepdims=True)]
    for k in range(1, K):
        ds = ds + _rows_after(ext_dy, K - 1 - k) * w[k:k + 1]
        dws.append(jnp.sum(dy * _rows_before(ext_s, K - 1 - k), axis=0, keepdims=True))
    return ds, jnp.concatenate(dws, axis=0)


def _sconv_fwd(pm, w, name):
    T = pm.shape[0]

    def body(pm_ref, prev_ref, w_ref, y_ref):
        s = pm_ref[:, D:2 * D] * pm_ref[:, 2 * D:]
        sp = jnp.where(pl.program_id(0) > 0, prev_ref[:, D:2 * D] * prev_ref[:, 2 * D:], 0.0)
        y_ref[...] = pm_ref[:, :D] * _conv_apply(jnp.concatenate([sp, s], axis=0), w_ref[...])

    return pl.pallas_call(
        body, name=name, grid=(T // TT,),
        in_specs=[_row_spec(3 * D), _prev_spec(3 * D, T), _full_spec((3, D))],
        out_specs=_row_spec(D), out_shape=SDS((T, D), F32),
        compiler_params=_params(1, VMEM_LIMIT))(pm, pm, w)


def _sconv_bwd(dy, pm, w, name):
    T = pm.shape[0]
    n_t = T // TT

    def body(dy_ref, dyn_ref, pm_ref, prev_ref, next_ref, w_ref, dpm_ref, dw_ref):
        i = pl.program_id(0)
        bg, cg, hv = pm_ref[:, :D], pm_ref[:, D:2 * D], pm_ref[:, 2 * D:]
        sp = jnp.where(i > 0, prev_ref[:, D:2 * D] * prev_ref[:, 2 * D:], 0.0)
        ext_s = jnp.concatenate([sp, cg * hv], axis=0)
        dyv = dy_ref[...]
        dcn = jnp.where(i < n_t - 1, dyn_ref[...] * next_ref[:, :D], 0.0)
        ds, dw = _conv_grads(ext_s, jnp.concatenate([dyv * bg, dcn], axis=0), w_ref[...])
        dpm_ref[:, :D] = (dyv * _conv_apply(ext_s, w_ref[...])).astype(BF16)
        dpm_ref[:, D:2 * D] = (ds * hv).astype(BF16)
        dpm_ref[:, 2 * D:] = (ds * cg).astype(BF16)

        @pl.when(i == 0)
        def _():
            dw_ref[...] = jnp.zeros_like(dw_ref)
        dw_ref[...] += dw

    return pl.pallas_call(
        body, name=name, grid=(n_t,),
        in_specs=[_row_spec(D), _next_spec(D, T), _row_spec(3 * D), _prev_spec(3 * D, T), _next_spec(3 * D, T),
                  _full_spec((3, D))],
        out_specs=[_row_spec(3 * D), _full_spec((3, D))],
        out_shape=[SDS((T, 3 * D), BF16), SDS((3, D), F32)],
        compiler_params=_params(1, VMEM_LIMIT))(dy, dy, pm, pm, pm, w)


def _dnconv_fwd(pm, w, name):
    T = pm.shape[0]

    def body(pm_ref, prev_ref, w_ref, c_ref):
        sp = jnp.where(pl.program_id(0) > 0, prev_ref[...], 0.0)
        c_ref[...] = _conv_apply(jnp.concatenate([sp, pm_ref[...]], axis=0), w_ref[...])

    return pl.pallas_call(
        body, name=name, grid=(T // TT,),
        in_specs=[_row_spec(3 * D), _prev_spec(3 * D, T), _full_spec((4, 3 * D))],
        out_specs=_row_spec(3 * D), out_shape=SDS((T, 3 * D), F32),
        compiler_params=_params(1, VMEM_LIMIT))(pm, pm, w)


def _dnconv_bwd(dcq, dck, dcv, dab, pm, w, name):
    T = pm.shape[0]
    n_t = T // TT

    def body(dq_ref, dk_ref, dv_ref, dqn_ref, dkn_ref, dvn_ref, dab_ref, pm_ref, prev_ref, w_ref, dpm_ref, dw_ref):
        i = pl.program_id(0)
        sp = jnp.where(i > 0, prev_ref[...], 0.0)
        ext_s = jnp.concatenate([sp, pm_ref[...]], axis=0)
        own = jnp.concatenate([dq_ref[...], dk_ref[...], dv_ref[...]], axis=1)
        nxt = jnp.where(i < n_t - 1, jnp.concatenate([dqn_ref[...], dkn_ref[...], dvn_ref[...]], axis=1), 0.0)
        ds, dw = _conv_grads(ext_s, jnp.concatenate([own, nxt], axis=0), w_ref[...])
        dpm_ref[:, :3 * D] = ds.astype(BF16)
        dpm_ref[:, 3 * D:] = dab_ref[...].astype(BF16)

        @pl.when(i == 0)
        def _():
            dw_ref[...] = jnp.zeros_like(dw_ref)
        dw_ref[...] += dw

    return pl.pallas_call(
        body, name=name, grid=(n_t,),
        in_specs=[_row_spec(D), _row_spec(D), _row_spec(D), _next_spec(D, T), _next_spec(D, T), _next_spec(D, T),
                  _row_spec(AB_PAD), _row_spec(3 * D), _prev_spec(3 * D, T), _full_spec((4, 3 * D))],
        out_specs=[_row_spec(3 * D + AB_PAD), _full_spec((4, 3 * D))],
        out_shape=[SDS((T, 3 * D + AB_PAD), BF16), SDS((4, 3 * D), F32)],
        compiler_params=_params(1, VMEM_LIMIT))(dcq, dck, dcv, dcq, dck, dcv, dab, pm, pm, w)


def _l2n(x):
    return x * lax.rsqrt(jnp.sum(x * x, axis=-1, keepdims=True) + EPS)


def _softplus(x):
    return jnp.maximum(x, 0.0) + jnp.log1p(jnp.exp(-jnp.abs(x)))


def _dn_gates(ab, alog, dtb, h):
    lane = lax.broadcasted_iota(jnp.int32, ab.shape, 1)
    g_all = -jnp.exp(alog) * _softplus(ab + dtb)
    g = jnp.sum(jnp.where(lane == h, g_all, 0.0), axis=1, keepdims=True)
    beta = jnp.sum(jnp.where(lane == 8 + h, jax.nn.sigmoid(ab), 0.0), axis=1, keepdims=True)
    ones = jnp.ones((1, HD), F32)
    return g * ones, beta * ones


def _dn_chunk(cq, ck, cv, gb, bb, S, onw):
    C = DN_C
    q = _l2n(jax.nn.silu(cq)) * (HD ** -0.5)
    k = _l2n(jax.nn.silu(ck))
    v = jax.nn.silu(cv)
    incl, strict = _tril(C), _tril(C, strict=True)
    gcum = _dot_hi(incl.astype(F32), gb)
    gi = gcum[:, :C]
    gj = gcum.T[:C, :]
    decay = jnp.where(incl, jnp.exp(jnp.where(incl, gi - gj, 0.0)), 0.0)
    kb = k * bb
    a_mat = jnp.where(strict, mm_nt(kb, k) * decay, 0.0)
    p = -a_mat
    eye = (lax.broadcasted_iota(jnp.int32, (C, C), 0) == lax.broadcasted_iota(jnp.int32, (C, C), 1)).astype(F32)
    t_mat = eye + p
    for _ in range(5):
        p = _dot_hi(p, p)
        t_mat = t_mat + _dot_hi(t_mat, p)
    eg = jnp.exp(gcum)
    u = mm(t_mat, v * bb)
    w = mm(t_mat, kb * eg)
    qk = mm_nt(q, k) * decay
    glast = gcum[C - 1:C, :]
    v_new = u - mm(w, S)
    o = mm(q * eg, S) + mm(qk, v_new)
    s_new = S * jnp.exp(glast) + mm_tn(k * jnp.exp(glast - gcum), v_new)
    return _rms(o, onw), s_new


def _dn_specs(T, rev):
    nb = T // DN_TB
    blk = (lambda n: nb - 1 - n) if rev else (lambda n: n)
    head = [pl.BlockSpec((DN_TB, HD), functools.partial(lambda n, h, off: (blk(n), off + h), off=8 * s)) for s in range(3)]
    ab = pl.BlockSpec((DN_TB, AB_PAD), lambda n, h: (blk(n), 3 * D // AB_PAD))
    st = pl.BlockSpec((DN_TB // DN_C, None, HD, HD), lambda n, h: (blk(n), h, 0, 0))
    out = pl.BlockSpec((DN_TB, HD), lambda n, h: (blk(n), h))
    row = pl.BlockSpec((1, HD), lambda n, h: (0, 0))
    return nb, head, ab, st, out, row


def _dn_fwd(cpre, pm, alog, dtb, onw, name):
    T = cpre.shape[0]
    nb, head, ab, st, out, row = _dn_specs(T, False)

    def body(cq_ref, ck_ref, cv_ref, ab_ref, alog_ref, dtb_ref, onw_ref, o_ref, st_ref, s_scr):
        n, h = pl.program_id(0), pl.program_id(1)

        @pl.when(n == 0)
        def _():
            s_scr[h] = jnp.zeros((HD, HD), F32)
        gb, bb = _dn_gates(ab_ref[...], alog_ref[...], dtb_ref[...], h)
        S = s_scr[h]
        for c in range(DN_TB // DN_C):
            rows = slice(c * DN_C, (c + 1) * DN_C)
            st_ref[c] = S
            o, S = _dn_chunk(cq_ref[rows, :], ck_ref[rows, :], cv_ref[rows, :], gb[rows], bb[rows], S, onw_ref[...])
            o_ref[rows, :] = o
        s_scr[h] = S

    return pl.pallas_call(
        body, name=name, grid=(nb, 8),
        in_specs=head + [ab, row, row, row], out_specs=[out, st],
        out_shape=[SDS((T, D), F32), SDS((T // DN_C, 8, HD, HD), F32)],
        scratch_shapes=[pltpu.VMEM((8, HD, HD), F32)],
        compiler_params=_params(2, VMEM_LIMIT))(cpre, cpre, cpre, pm, alog, dtb, onw)


def _dn_bwd(do, cpre, pm, st, alog, dtb, onw, name):
    T = cpre.shape[0]
    nb, head, ab, stspec, out, row = _dn_specs(T, True)

    def body(do_ref, cq_ref, ck_ref, cv_ref, ab_ref, st_ref, alog_ref, dtb_ref, onw_ref,
             dcq_ref, dck_ref, dcv_ref, dab_ref, dalog_ref, ddtb_ref, donw_ref, ds_scr):
        n, h = pl.program_id(0), pl.program_id(1)

        @pl.when(n == 0)
        def _():
            ds_scr[h] = jnp.zeros((HD, HD), F32)

        @pl.when((n == 0) & (h == 0))
        def _():
            dalog_ref[...] = jnp.zeros_like(dalog_ref)
            ddtb_ref[...] = jnp.zeros_like(ddtb_ref)
            donw_ref[...] = jnp.zeros_like(donw_ref)

        @pl.when(h == 0)
        def _():
            dab_ref[...] = jnp.zeros_like(dab_ref)

        (gb, bb), gates_vjp = jax.vjp(lambda a, b, c: _dn_gates(a, b, c, h), ab_ref[...], alog_ref[...], dtb_ref[...])
        dS = ds_scr[h]
        n_c = DN_TB // DN_C
        dgs, dbs = [None] * n_c, [None] * n_c
        donw = jnp.zeros((1, HD), F32)
        for c in reversed(range(n_c)):
            rows = slice(c * DN_C, (c + 1) * DN_C)
            _, vjp = jax.vjp(_dn_chunk, cq_ref[rows, :], ck_ref[rows, :], cv_ref[rows, :], gb[rows], bb[rows],
                             st_ref[c], onw_ref[...])
            dcq, dck, dcv, dgs[c], dbs[c], dS, dn = vjp((do_ref[rows, :], dS))
            dcq_ref[rows, :] = dcq
            dck_ref[rows, :] = dck
            dcv_ref[rows, :] = dcv
            donw += dn
        ds_scr[h] = dS
        dab, dalog, ddtb = gates_vjp((jnp.concatenate(dgs, axis=0), jnp.concatenate(dbs, axis=0)))
        dab_ref[...] += dab
        dalog_ref[...] += dalog
        ddtb_ref[...] += ddtb
        donw_ref[...] += donw

    dabspec = pl.BlockSpec((DN_TB, AB_PAD), lambda n, h: (nb - 1 - n, 0))
    return pl.pallas_call(
        body, name=name, grid=(nb, 8),
        in_specs=[out] + head + [ab, stspec, row, row, row],
        out_specs=[out, out, out, dabspec, row, row, row],
        out_shape=[SDS((T, D), F32)] * 3 + [SDS((T, AB_PAD), F32)] + [SDS((1, HD), F32)] * 3,
        scratch_shapes=[pltpu.VMEM((8, HD, HD), F32)],
        compiler_params=_params(2, VMEM_LIMIT))(do, cpre, cpre, cpre, pm, st, alog, dtb, onw)


_BNN = (((2,), (1,)), ((0,), (0,)))
_BNT = (((2,), (2,)), ((0,), (0,)))
_BTN = (((1,), (1,)), ((0,), (0,)))


def _bdot(a, b, dims):
    return lax.dot_general(a.astype(BF16), b.astype(BF16), dims, preferred_element_type=F32)


def _bdot3(a, b, dims):
    ah, bh = a.astype(BF16), b.astype(BF16)
    al, bl = (a - ah.astype(F32)).astype(BF16), (b - bh.astype(F32)).astype(BF16)
    d = functools.partial(lax.dot_general, dimension_numbers=dims, preferred_element_type=F32)
    return d(ah, bh) + (d(ah, bl) + d(al, bh))


def _bdot_hi(a, b, dims):
    return lax.dot_general(a, b, dims, precision=HI, preferred_element_type=F32)


def _batched_matmuls(dot):
    @jax.custom_vjp
    def nn(a, b):
        return dot(a, b, _BNN)

    @jax.custom_vjp
    def nt(a, b):
        return dot(a, b, _BNT)

    @jax.custom_vjp
    def tn(a, b):
        return dot(a, b, _BTN)

    nn.defvjp(lambda a, b: (dot(a, b, _BNN), (a, b)), lambda r, g: (dot(g, r[1], _BNT), dot(r[0], g, _BTN)))
    nt.defvjp(lambda a, b: (dot(a, b, _BNT), (a, b)), lambda r, g: (dot(g, r[1], _BNN), dot(g, r[0], _BTN)))
    tn.defvjp(lambda a, b: (dot(a, b, _BTN), (a, b)), lambda r, g: (dot(r[1], g, _BNT), dot(r[0], g, _BNN)))
    return nn, nt, tn


bmm, bmm_nt, bmm_tn = _batched_matmuls(_bdot)
bmm3, _, _ = _batched_matmuls(_bdot3)
bmm_hi, bmm_hi_nt, _ = _batched_matmuls(_bdot_hi)

@jax.custom_vjp
def _neumann_inverse(n):
    C = n.shape[1]
    eye = lax.broadcasted_iota(jnp.int32, n.shape, 1) == lax.broadcasted_iota(jnp.int32, n.shape, 2)
    t = eye.astype(F32) + n
    for _ in range(5):
        n = _bdot3(n, n, _BNN)
        t = t + _bdot3(t, n, _BNN)
    return t


def _neumann_inverse_fwd(n):
    t = _neumann_inverse(n)
    return t, t


def _neumann_inverse_bwd(t, g):
    return (_bdot3(_bdot3(t, g, _BTN), t, _BNT),)


_neumann_inverse.defvjp(_neumann_inverse_fwd, _neumann_inverse_bwd)


@jax.custom_vjp
def _saved_inverse(n, t):
    return t


_saved_inverse.defvjp(lambda n, t: (t, t), lambda t, g: (_bdot3(_bdot3(t, g, _BTN), t, _BNT), jnp.zeros_like(t)))

DN_NCH = DN_TB // DN_C
DN_NH = 4


def _decay_terms(ab, alog, dtb, first_head, n_heads):
    C = DN_C
    lane = lax.broadcasted_iota(jnp.int32, ab.shape, 1)
    g_all = (-jnp.exp(alog) * _softplus(ab + dtb)).reshape(DN_NCH, C, HD)
    beta_all = jax.nn.sigmoid(ab)
    r = lax.broadcasted_iota(jnp.int32, (DN_NCH, C, C), 1)
    c = lax.broadcasted_iota(jnp.int32, (DN_NCH, C, C), 2)
    gc_all = bmm_hi((r >= c).astype(F32), g_all)
    gc_rows = [gc_all[i].T for i in range(DN_NCH)]
    lane3 = lax.broadcasted_iota(jnp.int32, (DN_NCH, C, HD), 2)
    row = lax.broadcasted_iota(jnp.int32, (HD, C), 0)
    ones = jnp.ones((1, HD), F32)
    gcs, gjs, betas = [], [], []
    for i in range(n_heads):
        h = first_head + i
        gcs.append(jnp.sum(jnp.where(lane3 == h, gc_all, 0.0), axis=2, keepdims=True) * ones)
        gjs.append(jnp.concatenate(
            [jnp.broadcast_to(jnp.sum(jnp.where(row == h, t, 0.0), axis=0, keepdims=True), (C, C))[None] for t in gc_rows],
            axis=0))
        beta = jnp.sum(jnp.where(lane == 8 + h, beta_all, 0.0), axis=1, keepdims=True) * ones
        betas.append(beta.reshape(DN_NCH, C, HD))
    return jnp.concatenate(gcs, axis=0), jnp.concatenate(gjs, axis=0), jnp.concatenate(betas, axis=0)


def _dn_prep(cq, ck, cv, gcum, gj, bb, t_saved=None):
    B, C = cq.shape[0], DN_C
    q = _l2n(jax.nn.silu(cq)) * (HD ** -0.5)
    k = _l2n(jax.nn.silu(ck))
    v = jax.nn.silu(cv)
    r = lax.broadcasted_iota(jnp.int32, (B, C, C), 1)
    c = lax.broadcasted_iota(jnp.int32, (B, C, C), 2)
    incl, strict = r >= c, r > c
    decay = jnp.where(incl, jnp.exp(jnp.where(incl, gcum[:, :, :C] - gj, 0.0)), 0.0)
    kb = k * bb
    n_mat = -jnp.where(strict, bmm_nt(kb, k) * decay, 0.0)
    t_mat = _neumann_inverse(n_mat) if t_saved is None else _saved_inverse(n_mat, t_saved)
    eg = jnp.exp(gcum)
    glast = gcum[:, C - 1:C, :]
    return (bmm(t_mat, v * bb), bmm(t_mat, kb * eg), bmm_nt(q, k) * decay, q * eg, k * jnp.exp(glast - gcum),
            jnp.exp(glast), t_mat)


def _dn_scan_step(u, w, qk, qd, kd, egl, S, onw):
    v_new = u - bmm(w, S)
    o = bmm(qd, S) + bmm(qk, v_new)
    return _rms(o, onw), S * egl + bmm_tn(kd, v_new)


def _head_gates(ab, alog, dtb, first_head, n_heads):
    gs, bs = [], []
    for i in range(n_heads):
        g, b = _dn_gates(ab, alog, dtb, first_head + i)
        gs.append(g.reshape(DN_NCH, DN_C, HD))
        bs.append(b.reshape(DN_NCH, DN_C, HD))
    return jnp.concatenate(gs, axis=0), jnp.concatenate(bs, axis=0)


def _to_batch(ref, n_heads):
    return jnp.concatenate([ref[:, i * HD:(i + 1) * HD].astype(F32).reshape(DN_NCH, DN_C, HD) for i in range(n_heads)],
                           axis=0)


def _from_batch(ref, val, n_heads):
    for i in range(n_heads):
        ref[:, i * HD:(i + 1) * HD] = val[i * DN_NCH:(i + 1) * DN_NCH].reshape(DN_TB, HD).astype(ref.dtype)


def _prep_specs(T, rev):
    nb = T // DN_TB
    blk = (lambda n: nb - 1 - n) if rev else (lambda n: n)
    ng = 8 // DN_NH
    head = [pl.BlockSpec((DN_TB, DN_NH * HD), functools.partial(lambda n, h, off: (blk(n), off + h), off=ng * s))
            for s in range(3)]
    ab = pl.BlockSpec((DN_TB, AB_PAD), lambda n, h: (blk(n), 3 * D // AB_PAD))
    row = pl.BlockSpec((1, HD), lambda n, h: (0, 0))
    wide = pl.BlockSpec((DN_TB, DN_NH * HD), lambda n, h: (blk(n), h))
    qk = pl.BlockSpec((DN_NCH, DN_NH, DN_C, DN_C), lambda n, h: (blk(n), h, 0, 0))
    eg = pl.BlockSpec((DN_NCH, DN_NH, 1, HD), lambda n, h: (blk(n), h, 0, 0))
    return nb, ng, head, ab, row, wide, qk, eg


def _dn_prep_fwd(cpre, pm, alog, dtb, name):
    T = cpre.shape[0]
    nb, ng, head, ab, row, wide, qks, egs = _prep_specs(T, False)

    def body(cq_ref, ck_ref, cv_ref, ab_ref, alog_ref, dtb_ref, u_ref, w_ref, qk_ref, qd_ref, kd_ref, e_ref, t_ref):
        gcum, gj, bb = _decay_terms(ab_ref[...], alog_ref[...], dtb_ref[...], pl.program_id(1) * DN_NH, DN_NH)
        u, w, qk, qd, kd, egl, t_mat = _dn_prep(_to_batch(cq_ref, DN_NH), _to_batch(ck_ref, DN_NH),
                                                _to_batch(cv_ref, DN_NH), gcum, gj, bb)
        _from_batch(u_ref, u, DN_NH)
        _from_batch(w_ref, w, DN_NH)
        _from_batch(qd_ref, qd, DN_NH)
        _from_batch(kd_ref, kd, DN_NH)
        for i in range(DN_NH):
            qk_ref[:, i] = qk[i * DN_NCH:(i + 1) * DN_NCH].astype(BF16)
            e_ref[:, i] = egl[i * DN_NCH:(i + 1) * DN_NCH]
            t_ref[:, i] = t_mat[i * DN_NCH:(i + 1) * DN_NCH]

    return pl.pallas_call(
        body, name=name, grid=(nb, ng), in_specs=head + [ab, row, row],
        out_specs=[wide, wide, qks, wide, wide, egs, qks],
        out_shape=[SDS((T, D), F32), SDS((T, D), BF16), SDS((T // DN_C, 8, DN_C, DN_C), BF16), SDS((T, D), BF16),
                   SDS((T, D), BF16), SDS((T // DN_C, 8, 1, HD), F32), SDS((T // DN_C, 8, DN_C, DN_C), F32)],
        compiler_params=_params(2, VMEM_LIMIT))(cpre, cpre, cpre, pm, alog, dtb)


def _dn_prep_bwd(du, dw, dqk, dqd, dkd, degl, t_mat, cpre, pm, alog, dtb, name):
    T = cpre.shape[0]
    nb, ng, head, ab, row, wide, qks, egs = _prep_specs(T, True)

    def body(du_ref, dw_ref, dqk_ref, dqd_ref, dkd_ref, de_ref, t_ref, cq_ref, ck_ref, cv_ref, ab_ref, alog_ref,
             dtb_ref, dcq_ref, dck_ref, dcv_ref, dab_ref, dalog_ref, ddtb_ref):
        n, h = pl.program_id(0), pl.program_id(1)

        @pl.when((n == 0) & (h == 0))
        def _():
            dalog_ref[...] = jnp.zeros_like(dalog_ref)
            ddtb_ref[...] = jnp.zeros_like(ddtb_ref)

        @pl.when(h == 0)
        def _():
            dab_ref[...] = jnp.zeros_like(dab_ref)

        t_saved = jnp.concatenate([t_ref[:, i] for i in range(DN_NH)], axis=0)

        def fwd(cq, ck, cv, ab_v, alog_v, dtb_v):
            gcum, gj, bb = _decay_terms(ab_v, alog_v, dtb_v, h * DN_NH, DN_NH)
            return _dn_prep(cq, ck, cv, gcum, gj, bb, t_saved)[:6]

        _, vjp = jax.vjp(fwd, _to_batch(cq_ref, DN_NH), _to_batch(ck_ref, DN_NH), _to_batch(cv_ref, DN_NH), ab_ref[...],
                         alog_ref[...], dtb_ref[...])
        cot = (_to_batch(du_ref, DN_NH), _to_batch(dw_ref, DN_NH),
               jnp.concatenate([dqk_ref[:, i] for i in range(DN_NH)], axis=0), _to_batch(dqd_ref, DN_NH),
               _to_batch(dkd_ref, DN_NH), jnp.concatenate([de_ref[:, i] for i in range(DN_NH)], axis=0))
        dcq, dck, dcv, dab, dalog, ddtb = vjp(cot)
        _from_batch(dcq_ref, dcq, DN_NH)
        _from_batch(dck_ref, dck, DN_NH)
        _from_batch(dcv_ref, dcv, DN_NH)
        dab_ref[...] += dab
        dalog_ref[...] += dalog
        ddtb_ref[...] += ddtb

    dabspec = pl.BlockSpec((DN_TB, AB_PAD), lambda n, h: (nb - 1 - n, 0))
    return pl.pallas_call(
        body, name=name, grid=(nb, ng),
        in_specs=[wide, wide, qks, wide, wide, egs, qks] + head + [ab, row, row],
        out_specs=[wide, wide, wide, dabspec, row, row],
        out_shape=[SDS((T, D), F32)] * 3 + [SDS((T, AB_PAD), F32)] + [SDS((1, HD), F32)] * 2,
        compiler_params=_params(2, VMEM_LIMIT))(du, dw, dqk, dqd, dkd, degl, t_mat, cpre, cpre, cpre, pm, alog, dtb)


def _scan_specs(T, rev):
    nb = T // DN_TB
    blk = (lambda n: nb - 1 - n) if rev else (lambda n: n)
    wide = pl.BlockSpec((DN_TB, D), lambda n: (blk(n), 0))
    qk = pl.BlockSpec((DN_NCH, 8, DN_C, DN_C), lambda n: (blk(n), 0, 0, 0))
    eg = pl.BlockSpec((DN_NCH, 8, 1, HD), lambda n: (blk(n), 0, 0, 0))
    st = pl.BlockSpec((DN_NCH, 8, HD, HD), lambda n: (blk(n), 0, 0, 0))
    row = pl.BlockSpec((1, HD), lambda n: (0, 0))
    return nb, wide, qk, eg, st, row


def _heads_of(ref, rows):
    return jnp.concatenate([ref[rows, h * HD:(h + 1) * HD].astype(F32)[None] for h in range(8)], axis=0)


def _dn_scan_fwd(u, w, qk, qd, kd, egl, onw, name):
    T = u.shape[0]
    nb, wide, qks, egs, sts, row = _scan_specs(T, False)

    def body(u_ref, w_ref, qk_ref, qd_ref, kd_ref, e_ref, onw_ref, o_ref, st_ref, s_scr):
        @pl.when(pl.program_id(0) == 0)
        def _():
            s_scr[...] = jnp.zeros_like(s_scr)
        S = s_scr[...]
        for c in range(DN_NCH):
            rows = slice(c * DN_C, (c + 1) * DN_C)
            st_ref[c] = S
            o, S = _dn_scan_step(_heads_of(u_ref, rows), _heads_of(w_ref, rows), qk_ref[c].astype(F32),
                                 _heads_of(qd_ref, rows), _heads_of(kd_ref, rows), e_ref[c], S, onw_ref[...])
            for h in range(8):
                o_ref[rows, h * HD:(h + 1) * HD] = o[h]
        s_scr[...] = S

    return pl.pallas_call(
        body, name=name, grid=(nb,), in_specs=[wide, wide, qks, wide, wide, egs, row], out_specs=[wide, sts],
        out_shape=[SDS((T, D), F32), SDS((T // DN_C, 8, HD, HD), F32)],
        scratch_shapes=[pltpu.VMEM((8, HD, HD), F32)],
        compiler_params=_params(1, VMEM_LIMIT))(u, w, qk, qd, kd, egl, onw)


def _dn_scan_bwd(do, u, w, qk, qd, kd, egl, st, onw, name):
    T = u.shape[0]
    nb, wide, qks, egs, sts, row = _scan_specs(T, True)

    def body(do_ref, u_ref, w_ref, qk_ref, qd_ref, kd_ref, e_ref, st_ref, onw_ref,
             du_ref, dw_ref, dqk_ref, dqd_ref, dkd_ref, de_ref, donw_ref, ds_scr):
        @pl.when(pl.program_id(0) == 0)
        def _():
            ds_scr[...] = jnp.zeros_like(ds_scr)
            donw_ref[...] = jnp.zeros_like(donw_ref)
        dS = ds_scr[...]
        donw = jnp.zeros((1, HD), F32)
        for c in reversed(range(DN_NCH)):
            rows = slice(c * DN_C, (c + 1) * DN_C)
            _, vjp = jax.vjp(_dn_scan_step, _heads_of(u_ref, rows), _heads_of(w_ref, rows), qk_ref[c].astype(F32),
                             _heads_of(qd_ref, rows), _heads_of(kd_ref, rows), e_ref[c], st_ref[c], onw_ref[...])
            du, dw, dqk, dqd, dkd, de, dS, dn = vjp((_heads_of(do_ref, rows), dS))
            for h in range(8):
                cols = slice(h * HD, (h + 1) * HD)
                du_ref[rows, cols] = du[h]
                dw_ref[rows, cols] = dw[h]
                dqd_ref[rows, cols] = dqd[h]
                dkd_ref[rows, cols] = dkd[h]
            dqk_ref[c] = dqk
            de_ref[c] = de
            donw += dn
        ds_scr[...] = dS
        donw_ref[...] += donw

    return pl.pallas_call(
        body, name=name, grid=(nb,), in_specs=[wide, wide, wide, qks, wide, wide, egs, sts, row],
        out_specs=[wide, wide, qks, wide, wide, egs, row],
        out_shape=[SDS((T, D), F32), SDS((T, D), F32), SDS((T // DN_C, 8, DN_C, DN_C), F32), SDS((T, D), F32),
                   SDS((T, D), F32), SDS((T // DN_C, 8, 1, HD), F32), SDS((1, HD), F32)],
        scratch_shapes=[pltpu.VMEM((8, HD, HD), F32)],
        compiler_params=_params(1, VMEM_LIMIT))(do, u, w, qk, qd, kd, egl, st, onw)


def _adamw(w, g, m, v, name):
    R, C = w.shape
    tr = 256 if R % 256 == 0 and R > 256 else R
    tc = 256 if tr == R and R > 256 and C % 256 == 0 else C
    c1 = 1.0 - ADAM_B1 ** ADAM_STEP
    c2 = 1.0 - ADAM_B2 ** ADAM_STEP

    def body(w_ref, g_ref, m_ref, v_ref, d_ref, nm_ref, nv_ref):
        gv = g_ref[...]
        nm = ADAM_B1 * m_ref[...] + (1.0 - ADAM_B1) * gv
        nv = ADAM_B2 * v_ref[...] + (1.0 - ADAM_B2) * (gv * gv)
        nm_ref[...] = nm
        nv_ref[...] = nv
        d_ref[...] = -ADAM_LR * ((nm / c1) / (jnp.sqrt(nv / c2) + ADAM_EPS) + ADAM_WD * w_ref[...])

    spec = pl.BlockSpec((tr, tc), lambda i, j: (i, j))
    return pl.pallas_call(
        body, name=name, grid=(R // tr, C // tc), in_specs=[spec] * 4, out_specs=[spec] * 3,
        out_shape=[SDS((R, C), F32)] * 3, compiler_params=_params(2, VMEM_LIMIT))(w, g, m, v)


def _local_step(x, mem, target, wts, sm):
    kinds = [i % 3 for i in range(DEPTH)]
    mnw = sm["mem_norm_w"].reshape(1, D)
    kv = _memkv_fwd(mem, mnw, wts["wkv"])
    saved, blocks = [], []
    for i, kind in enumerate(kinds):
        j = i // 3
        npre = sm["norm_pre"][i].reshape(1, D)
        npost = sm["norm_post"][i].reshape(1, D)
        mix, gate = wts["blocks"](i, x)
        blocks.append((mix, gate))
        pm, pg, h = _inproj_fwd(x, npre, mix, gate, kind == 2, f"inproj_fwd_{i}")
        extra = None
        if kind == 0:
            bs3 = jnp.broadcast_to(sm["a_b_s"][j][:, :, None], (8, HD, HD))
            ymix = _gmlp_fwd(pm, sm["a_ln_w"][j].reshape(1, D), sm["a_ln_b"][j].reshape(1, D), sm["a_w_s"][j], bs3,
                             f"gmlp_fwd_{i}")
            extra = bs3
        elif kind == 1:
            ymix = _sconv_fwd(pm, sm["b_conv_w"][j], f"sconv_fwd_{i}")
        else:
            cpre = _dnconv_fwd(pm, sm["c_conv_w"][j], f"dnconv_fwd_{i}")
            alog = jnp.pad(sm["c_a_log"][j], (0, HD - 8)).reshape(1, HD)
            dtb = jnp.pad(sm["c_dt_bias"][j], (0, HD - 8)).reshape(1, HD)
            onw = sm["c_o_norm_w"][j].reshape(1, HD)
            *prep, t_mat = _dn_prep_fwd(cpre, pm, alog, dtb, f"dn_prep_fwd_{i}")
            ymix, st = _dn_scan_fwd(*prep, onw, f"dn_scan_fwd_{i}")
            extra = (cpre, prep, t_mat, st, alog, dtb, onw)
        ycat = _ag_fwd(ymix, pg, kv, f"ag_fwd_{i}")
        o, xn = _outproj_fwd(ycat, wts["wo"][i], x, npost, f"outproj_fwd_{i}")
        saved.append((x, h, pm, pg, ymix, ycat, o, extra))
        x = xn

    loss, dx = _loss_head(x, target)

    g = {"wm": [None] * DEPTH, "wg": [None] * DEPTH, "wo": [None] * DEPTH, "norm_pre": [None] * DEPTH,
         "norm_post": [None] * DEPTH}
    dkv = jnp.zeros((N_MEM, 2 * D_XA), F32)
    for i in reversed(range(DEPTH)):
        kind, j = kinds[i], i // 3
        xi, h, pm, pg, ymix, ycat, o, extra = saved[i]
        npre = sm["norm_pre"][i].reshape(1, D)
        npost = sm["norm_post"][i].reshape(1, D)
        dycat, dobf, g["norm_post"][i] = _outproj_bwd(dx, o, npost, wts["wo"][i], f"outproj_bwd_{i}")
        g["wo"][i] = _matmul_tn(ycat, dobf, f"dwo_{i}")
        dymix, dpg, dkv = _ag_bwd(dycat, ymix, pg, kv, dkv, f"ag_bwd_{i}")
        if kind == 0:
            dpm, dlnw, dlnb, dws, dbs3 = _gmlp_bwd(dymix, pm, sm["a_ln_w"][j].reshape(1, D),
                                                   sm["a_ln_b"][j].reshape(1, D), sm["a_w_s"][j], extra,
                                                   f"gmlp_bwd_{i}")
            g.setdefault("a_ln_w", {})[j] = dlnw.reshape(D)
            g.setdefault("a_ln_b", {})[j] = dlnb.reshape(D)
            g.setdefault("a_w_s", {})[j] = dws
            g.setdefault("a_b_s", {})[j] = dbs3[:, :, 0]
        elif kind == 1:
            dpm, dcw = _sconv_bwd(dymix, pm, sm["b_conv_w"][j], f"sconv_bwd_{i}")
            g.setdefault("b_conv_w", {})[j] = dcw
        else:
            cpre, prep, t_mat, st, alog, dtb, onw = extra
            *dprep, donw = _dn_scan_bwd(dymix, *prep, st, onw, f"dn_scan_bwd_{i}")
            dcq, dck, dcv, dab, dalog, ddtb = _dn_prep_bwd(*dprep, t_mat, cpre, pm, alog, dtb, f"dn_prep_bwd_{i}")
            dpm, dcw = _dnconv_bwd(dcq, dck, dcv, dab, pm, sm["c_conv_w"][j], f"dnconv_bwd_{i}")
            g.setdefault("c_conv_w", {})[j] = dcw
            g.setdefault("c_a_log", {})[j] = dalog[0, :8]
            g.setdefault("c_dt_bias", {})[j] = ddtb[0, :8]
            g.setdefault("c_o_norm_w", {})[j] = donw[0]
        if kind == 2:
            g["wm"][i] = _matmul_tn(dpm, h, f"dwm_{i}")
            g["wg"][i] = _matmul_tn(dpg, h, f"dwg_{i}")
        else:
            g["wm"][i] = _matmul_tn(h, dpm, f"dwm_{i}", GRAD_TILE[kind])
            g["wg"][i] = _matmul_tn(h, dpg, f"dwg_{i}", GRAD_TILE[kind])
        dx, g["norm_pre"][i] = _inproj_bwd(dpm, dpg, xi, npre, blocks[i][0], blocks[i][1], kind == 2, dx,
                                           f"inproj_bwd_{i}")
    g["mem_norm_w"], g["wkv"] = _memkv_bwd(mem, mnw, wts["wkv"], dkv)
    return loss[0, 0], dx, g


ANY = pl.BlockSpec(memory_space=pl.ANY)


def _place():
    return lax.axis_index("x"), lax.axis_index("y"), lax.axis_index("c")


def _gather_weights(big, vec):
    def body(big_ref, vec_ref, ob_ref, ov_ref, ici_send, ici_recv, d2d_send, d2d_recv, vec_send, vec_recv):
        x, y, c = _place()
        chip = 2 * x + y
        peers = [(1 - x, y), (x, 1 - y), (1 - x, 1 - y)]

        def rows(half, k):
            return pl.ds(half * HALF_ROWS + k * CHUNK_ROWS, CHUNK_ROWS)

        def over_ici(j, k, slab):
            px, py = peers[j]
            i = j * N_CHUNKS + k
            return pltpu.make_async_remote_copy(
                src_ref=big_ref.at[rows(c, k)], dst_ref=ob_ref.at[slab, rows(c, k)], send_sem=ici_send.at[i],
                recv_sem=ici_recv.at[i], device_id=(px, py, c), device_id_type=MESH)

        def over_d2d(j, k, half):
            px, py = peers[j]
            i = j * N_CHUNKS + k
            where = ob_ref.at[2 * px + py, rows(half, k)]
            return pltpu.make_async_remote_copy(
                src_ref=where, dst_ref=where, send_sem=d2d_send.at[i], recv_sem=d2d_recv.at[i],
                device_id=(x, y, 1 - c), device_id_type=MESH)

        def small(j, slab):
            px, py = peers[j]
            return pltpu.make_async_remote_copy(
                src_ref=vec_ref, dst_ref=ov_ref.at[slab], send_sem=vec_send.at[j], recv_sem=vec_recv.at[j],
                device_id=(px, py, c), device_id_type=MESH)

        sends = [small(j, chip) for j in range(3)] + [over_ici(j, k, chip) for k in range(N_CHUNKS) for j in range(3)]
        for cp in sends:
            cp.start()
        forwards = []
        for k in range(N_CHUNKS):
            for j, (px, py) in enumerate(peers):
                over_ici(j, k, 2 * px + py).wait_recv()
                forwards.append(over_d2d(j, k, c))
                forwards[-1].start()
        for k in range(N_CHUNKS):
            for j in range(3):
                over_d2d(j, k, 1 - c).wait_recv()
        for j, (px, py) in enumerate(peers):
            small(j, 2 * px + py).wait_recv()
        for cp in sends + forwards:
            cp.wait_send()

    n = 3 * N_CHUNKS
    dma = pltpu.SemaphoreType.DMA
    return pl.pallas_call(
        body, name="gather_weights", in_specs=[ANY, ANY], out_specs=[ANY, ANY],
        out_shape=[SDS((4,) + big.shape, big.dtype), SDS((4,) + vec.shape, vec.dtype)],
        scratch_shapes=[dma((n,)), dma((n,)), dma((n,)), dma((n,)), dma((3,)), dma((3,))])(big, vec)


HALF_ROWS = 3328
CHUNK_ROWS = 256
N_CHUNKS = HALF_ROWS // CHUNK_ROWS


def _swap_halves(gbig, small):
    def body(g_ref, s_ref, ob_ref, os_ref, send_sems, recv_sems):
        x, y, c = _place()
        copies = []
        for s in range(4):
            for k in range(N_CHUNKS):
                rows = pl.ds(k * CHUNK_ROWS, CHUNK_ROWS)
                copies.append(pltpu.make_async_remote_copy(
                    src_ref=g_ref.at[s, 1 - c, rows], dst_ref=ob_ref.at[s, rows], send_sem=send_sems.at[len(copies)],
                    recv_sem=recv_sems.at[len(copies)], device_id=(x, y, 1 - c), device_id_type=MESH))
        copies.append(pltpu.make_async_remote_copy(
            src_ref=s_ref, dst_ref=os_ref, send_sem=send_sems.at[len(copies)], recv_sem=recv_sems.at[len(copies)],
            device_id=(x, y, 1 - c), device_id_type=MESH))
        for cp in copies:
            cp.start()
        for cp in copies:
            cp.wait_recv()
        for cp in copies:
            cp.wait_send()

    n = 4 * N_CHUNKS + 1
    return pl.pallas_call(
        body, name="swap_halves", in_specs=[ANY, ANY], out_specs=[ANY, ANY],
        out_shape=[SDS((4, HALF_ROWS, D), F32), SDS(small.shape, F32)],
        scratch_shapes=[pltpu.SemaphoreType.DMA((n,)), pltpu.SemaphoreType.DMA((n,))])(gbig, small)


def _pair_sum(gbig, other):
    def body(g_ref, o_ref, pb_ref, own_ref):
        x, y, c = _place()
        v = jnp.where(c == 0, g_ref[0], g_ref[1]) + o_ref[...]
        pb_ref[...] = v.astype(BF16)

        @pl.when(pl.program_id(1) == 2 * x + y)
        def _():
            own_ref[...] = v

    return pl.pallas_call(
        body, name="pair_sum", grid=(N_CHUNKS, 4),
        in_specs=[pl.BlockSpec((None, 2, CHUNK_ROWS, D), lambda i, s: (s, 0, i, 0)),
                  pl.BlockSpec((None, CHUNK_ROWS, D), lambda i, s: (s, i, 0))],
        out_specs=[pl.BlockSpec((None, CHUNK_ROWS, D), lambda i, s: (s, i, 0)),
                   pl.BlockSpec((CHUNK_ROWS, D), lambda i, s: (i, 0))],
        out_shape=[SDS((4, HALF_ROWS, D), BF16), SDS((HALF_ROWS, D), F32)],
        compiler_params=_params(2, VMEM_LIMIT))(gbig, other)


def _add(a, b, name):
    def body(a_ref, b_ref, o_ref):
        o_ref[...] = a_ref[...] + b_ref[...]

    return pl.pallas_call(body, name=name, out_shape=SDS(a.shape, a.dtype), compiler_params=_params(0, VMEM_LIMIT))(a, b)


def _chip_exchange(pb, ps):
    n_small = ps.shape[0]

    def body(pb_ref, ps_ref, lb_ref, ls_ref, send_sems, recv_sems):
        x, y, c = _place()
        chip = 2 * x + y
        peers = [(1 - x, y), (x, 1 - y), (1 - x, 1 - y)]

        def copies(slab_of):
            out = []
            for j, (px, py) in enumerate(peers):
                for k in range(N_CHUNKS):
                    rows = pl.ds(k * CHUNK_ROWS, CHUNK_ROWS)
                    out.append(pltpu.make_async_remote_copy(
                        src_ref=pb_ref.at[2 * px + py, rows], dst_ref=lb_ref.at[slab_of(j), rows],
                        send_sem=send_sems.at[len(out)], recv_sem=recv_sems.at[len(out)], device_id=(px, py, c),
                        device_id_type=MESH))
                out.append(pltpu.make_async_remote_copy(
                    src_ref=ps_ref, dst_ref=ls_ref.at[slab_of(j)], send_sem=send_sems.at[len(out)],
                    recv_sem=recv_sems.at[len(out)], device_id=(px, py, c), device_id_type=MESH))
            return out

        sends = copies(lambda j: chip)
        for cp in sends:
            cp.start()
        for cp in copies(lambda j: 2 * peers[j][0] + peers[j][1]):
            cp.wait_recv()
        for cp in sends:
            cp.wait_send()

    n = 3 * (N_CHUNKS + 1)
    return pl.pallas_call(
        body, name="chip_exchange", in_specs=[ANY, ANY], out_specs=[ANY, ANY],
        out_shape=[SDS((4, HALF_ROWS, D), BF16), SDS((4, n_small, D), F32)],
        scratch_shapes=[pltpu.SemaphoreType.DMA((n,)), pltpu.SemaphoreType.DMA((n,))])(pb, ps)


def _chip_sum(own, land):
    def body(own_ref, l_ref, o_ref):
        chip = 2 * lax.axis_index("x") + lax.axis_index("y")
        acc = jnp.where(chip == 0, own_ref[...], l_ref[0].astype(F32))
        for s in range(1, 4):
            acc = acc + jnp.where(chip == s, own_ref[...], l_ref[s].astype(F32))
        o_ref[...] = acc

    return pl.pallas_call(
        body, name="chip_sum", grid=(N_CHUNKS,),
        in_specs=[pl.BlockSpec((CHUNK_ROWS, D), lambda i: (i, 0)), pl.BlockSpec((4, CHUNK_ROWS, D), lambda i: (0, i, 0))],
        out_specs=pl.BlockSpec((CHUNK_ROWS, D), lambda i: (i, 0)), out_shape=SDS((HALF_ROWS, D), F32),
        compiler_params=_params(1, VMEM_LIMIT))(own, land)


def _sum4(own, land):
    def body(own_ref, l_ref, o_ref):
        chip = 2 * lax.axis_index("x") + lax.axis_index("y")
        acc = jnp.where(chip == 0, own_ref[...], l_ref[0])
        for s in range(1, 4):
            acc = acc + jnp.where(chip == s, own_ref[...], l_ref[s])
        o_ref[...] = acc

    return pl.pallas_call(body, name="sum_small", out_shape=SDS(own.shape, own.dtype),
                          compiler_params=_params(0, VMEM_LIMIT))(own, land)


def _share_half(r):
    def body(r_ref, o_ref, send_sems, recv_sems):
        x, y, c = _place()
        copies = [pltpu.make_async_remote_copy(
            src_ref=r_ref.at[pl.ds(k * CHUNK_ROWS, CHUNK_ROWS)], dst_ref=o_ref.at[pl.ds(k * CHUNK_ROWS, CHUNK_ROWS)],
            send_sem=send_sems.at[k], recv_sem=recv_sems.at[k], device_id=(x, y, 1 - c), device_id_type=MESH)
            for k in range(N_CHUNKS)]
        for cp in copies:
            cp.start()
        for cp in copies:
            cp.wait_recv()
        for cp in copies:
            cp.wait_send()

    return pl.pallas_call(
        body, name="share_half", in_specs=[ANY], out_specs=ANY, out_shape=SDS((HALF_ROWS, D), F32),
        scratch_shapes=[pltpu.SemaphoreType.DMA((N_CHUNKS,)), pltpu.SemaphoreType.DMA((N_CHUNKS,))])(r)


C_ROWS = 1312
CLASSES = {"a": (1024, 256), "b": (2560, 640), "c": (C_ROWS // 2, C_ROWS // 2), "o": (768, 256), "kv": (128, 128)}
W_CLASSES = {"a0": (512, 256), "o": (768, 256), "kv": (128, 128)}


def _chunk_list(specs):
    return [(k, r, chunk) for k, (half, chunk) in enumerate(specs) for r in range(0, half, chunk)]


def _gather_classes(arrs, specs, vec):
    n = len(arrs)
    chunks = _chunk_list(specs)
    nc = len(chunks)

    def body(*refs):
        ins, vec_ref, outs, ov_ref = refs[:n], refs[n], refs[n + 1:2 * n + 1], refs[2 * n + 1]
        ici_send, ici_recv, d2d_send, d2d_recv, vec_send, vec_recv = refs[2 * n + 2:]
        x, y, c = _place()
        chip = 2 * x + y
        peers = [(1 - x, y), (x, 1 - y), (1 - x, 1 - y)]

        def rows(ci, half):
            k, r, cnt = chunks[ci]
            return k, pl.ds(half * specs[k][0] + r, cnt)

        def over_ici(j, ci, slab):
            px, py = peers[j]
            k, rs = rows(ci, c)
            return pltpu.make_async_remote_copy(
                src_ref=ins[k].at[rs], dst_ref=outs[k].at[slab, rs], send_sem=ici_send.at[j * nc + ci],
                recv_sem=ici_recv.at[j * nc + ci], device_id=(px, py, c), device_id_type=MESH)

        def over_d2d(j, ci, half):
            px, py = peers[j]
            k, rs = rows(ci, half)
            where = outs[k].at[2 * px + py, rs]
            return pltpu.make_async_remote_copy(
                src_ref=where, dst_ref=where, send_sem=d2d_send.at[j * nc + ci], recv_sem=d2d_recv.at[j * nc + ci],
                device_id=(x, y, 1 - c), device_id_type=MESH)

        def small(j, slab):
            px, py = peers[j]
            return pltpu.make_async_remote_copy(
                src_ref=vec_ref, dst_ref=ov_ref.at[slab], send_sem=vec_send.at[j], recv_sem=vec_recv.at[j],
                device_id=(px, py, c), device_id_type=MESH)

        sends = [small(j, chip) for j in range(3)] + [over_ici(j, ci, chip) for ci in range(nc) for j in range(3)]
        for cp in sends:
            cp.start()
        forwards = []
        for ci in range(nc):
            for j, (px, py) in enumerate(peers):
                over_ici(j, ci, 2 * px + py).wait_recv()
                forwards.append(over_d2d(j, ci, c))
                forwards[-1].start()
        for ci in range(nc):
            for j in range(3):
                over_d2d(j, ci, 1 - c).wait_recv()
        for j, (px, py) in enumerate(peers):
            small(j, 2 * px + py).wait_recv()
        for cp in sends + forwards:
            cp.wait_send()

    dma = pltpu.SemaphoreType.DMA
    return pl.pallas_call(
        body, name="gather_weights", in_specs=[ANY] * (n + 1), out_specs=[ANY] * (n + 1),
        out_shape=[SDS((4,) + a.shape, a.dtype) for a in arrs] + [SDS((4,) + vec.shape, vec.dtype)],
        scratch_shapes=[dma((3 * nc,)), dma((3 * nc,)), dma((3 * nc,)), dma((3 * nc,)), dma((3,)), dma((3,))])(*arrs, vec)


def _swap_classes(grads, specs, small):
    n = len(grads)
    chunks = _chunk_list(specs)

    def body(*refs):
        ins, s_ref, outs, os_ref, send_sems, recv_sems = refs[:n], refs[n], refs[n + 1:2 * n + 1], *refs[2 * n + 1:]
        x, y, c = _place()
        copies = []
        for s in range(4):
            for k, r, cnt in chunks:
                copies.append(pltpu.make_async_remote_copy(
                    src_ref=ins[k].at[s, 1 - c, pl.ds(r, cnt)], dst_ref=outs[k].at[s, pl.ds(r, cnt)],
                    send_sem=send_sems.at[len(copies)], recv_sem=recv_sems.at[len(copies)],
                    device_id=(x, y, 1 - c), device_id_type=MESH))
        copies.append(pltpu.make_async_remote_copy(
            src_ref=s_ref, dst_ref=os_ref, send_sem=send_sems.at[len(copies)], recv_sem=recv_sems.at[len(copies)],
            device_id=(x, y, 1 - c), device_id_type=MESH))
        for cp in copies:
            cp.start()
        for cp in copies:
            cp.wait_recv()
        for cp in copies:
            cp.wait_send()

    m = 4 * len(chunks) + 1
    return pl.pallas_call(
        body, name="swap_halves", in_specs=[ANY] * (n + 1), out_specs=[ANY] * (n + 1),
        out_shape=[SDS((4, g.shape[2], g.shape[3]), F32) for g in grads] + [SDS(small.shape, F32)],
        scratch_shapes=[pltpu.SemaphoreType.DMA((m,)), pltpu.SemaphoreType.DMA((m,))])(*grads, small)


def _pair_sum_class(g, other, chunk, name):
    _, _, half, w = g.shape

    def body(g_ref, o_ref, pb_ref, own_ref):
        x, y, c = _place()
        v = jnp.where(c == 0, g_ref[0], g_ref[1]) + o_ref[...]
        pb_ref[...] = v.astype(BF16)

        @pl.when(pl.program_id(1) == 2 * x + y)
        def _():
            own_ref[...] = v

    return pl.pallas_call(
        body, name=name, grid=(half // chunk, 4),
        in_specs=[pl.BlockSpec((None, 2, chunk, w), lambda i, s: (s, 0, i, 0)),
                  pl.BlockSpec((None, chunk, w), lambda i, s: (s, i, 0))],
        out_specs=[pl.BlockSpec((None, chunk, w), lambda i, s: (s, i, 0)), pl.BlockSpec((chunk, w), lambda i, s: (i, 0))],
        out_shape=[SDS((4, half, w), BF16), SDS((half, w), F32)],
        compiler_params=_params(2, VMEM_LIMIT))(g, other)


def _exchange_classes(pbs, specs, ps):
    n = len(pbs)
    chunks = _chunk_list(specs)
    per_peer = len(chunks) + 1

    def body(*refs):
        ins, ps_ref, outs, ls_ref, send_sems, recv_sems = refs[:n], refs[n], refs[n + 1:2 * n + 1], *refs[2 * n + 1:]
        x, y, c = _place()
        chip = 2 * x + y
        peers = [(1 - x, y), (x, 1 - y), (1 - x, 1 - y)]

        def copies(slab_of):
            out = []
            for j, (px, py) in enumerate(peers):
                for k, r, cnt in chunks:
                    out.append(pltpu.make_async_remote_copy(
                        src_ref=ins[k].at[2 * px + py, pl.ds(r, cnt)], dst_ref=outs[k].at[slab_of(j), pl.ds(r, cnt)],
                        send_sem=send_sems.at[len(out)], recv_sem=recv_sems.at[len(out)], device_id=(px, py, c),
                        device_id_type=MESH))
                out.append(pltpu.make_async_remote_copy(
                    src_ref=ps_ref, dst_ref=ls_ref.at[slab_of(j)], send_sem=send_sems.at[len(out)],
                    recv_sem=recv_sems.at[len(out)], device_id=(px, py, c), device_id_type=MESH))
            return out

        sends = copies(lambda j: chip)
        for cp in sends:
            cp.start()
        for cp in copies(lambda j: 2 * peers[j][0] + peers[j][1]):
            cp.wait_recv()
        for cp in sends:
            cp.wait_send()

    m = 3 * per_peer
    return pl.pallas_call(
        body, name="chip_exchange", in_specs=[ANY] * (n + 1), out_specs=[ANY] * (n + 1),
        out_shape=[SDS(p.shape, BF16) for p in pbs] + [SDS((4,) + ps.shape, F32)],
        scratch_shapes=[pltpu.SemaphoreType.DMA((m,)), pltpu.SemaphoreType.DMA((m,))])(*pbs, ps)


def _chip_sum_class(own, land, chunk, name):
    half, w = own.shape

    def body(own_ref, l_ref, o_ref):
        chip = 2 * lax.axis_index("x") + lax.axis_index("y")
        acc = jnp.where(chip == 0, own_ref[...], l_ref[0].astype(F32))
        for s in range(1, 4):
            acc = acc + jnp.where(chip == s, own_ref[...], l_ref[s].astype(F32))
        o_ref[...] = acc

    return pl.pallas_call(
        body, name=name, grid=(half // chunk,),
        in_specs=[pl.BlockSpec((chunk, w), lambda i: (i, 0)), pl.BlockSpec((4, chunk, w), lambda i: (0, i, 0))],
        out_specs=pl.BlockSpec((chunk, w), lambda i: (i, 0)), out_shape=SDS((half, w), F32),
        compiler_params=_params(1, VMEM_LIMIT))(own, land)


def _share_classes(rs, specs):
    n = len(rs)
    chunks = _chunk_list(specs)

    def body(*refs):
        ins, outs, send_sems, recv_sems = refs[:n], refs[n:2 * n], *refs[2 * n:]
        x, y, c = _place()
        copies = [pltpu.make_async_remote_copy(
            src_ref=ins[k].at[pl.ds(r, cnt)], dst_ref=outs[k].at[pl.ds(r, cnt)], send_sem=send_sems.at[i],
            recv_sem=recv_sems.at[i], device_id=(x, y, 1 - c), device_id_type=MESH)
            for i, (k, r, cnt) in enumerate(chunks)]
        for cp in copies:
            cp.start()
        for cp in copies:
            cp.wait_recv()
        for cp in copies:
            cp.wait_send()

    m = len(chunks)
    return pl.pallas_call(
        body, name="share_half", in_specs=[ANY] * n, out_specs=[ANY] * n, out_shape=[SDS(r.shape, F32) for r in rs],
        scratch_shapes=[pltpu.SemaphoreType.DMA((m,)), pltpu.SemaphoreType.DMA((m,))])(*rs)


_HBM = pl.BlockSpec(memory_space=pltpu.HBM)
_SEM = pl.BlockSpec(memory_space=pltpu.SEMAPHORE)
_EFFECT = pltpu.SideEffectType.DATAFLOW_SIDE_EFFECTING


def _chip_peers():
    x, y, c = _place()
    return [(1 - x, y, c), (x, 1 - y, c), (1 - x, 1 - y, c)]


def _send_shard_start(v, name):
    def body(v_ref, land_ref, send_sems, recv_sems, v_thru, land_thru, token):
        x, y, c = _place()
        for j, peer in enumerate(_chip_peers()):
            pltpu.make_async_remote_copy(src_ref=v_ref, dst_ref=land_ref.at[2 * x + y], send_sem=send_sems.at[j],
                                         recv_sem=recv_sems.at[j], device_id=peer, device_id_type=MESH).start()
        token[...] = jnp.zeros_like(token)

    land_shape = (4,) + v.shape
    return pl.pallas_call(
        body, name=name,
        out_shape=(pltpu.SemaphoreType.DMA((3,)), pltpu.SemaphoreType.DMA((3,)), pltpu.HBM(v.shape, v.dtype),
                   pltpu.HBM(land_shape, v.dtype), SDS((8, 128), F32)),
        in_specs=(_HBM, _HBM), out_specs=(_SEM, _SEM, _HBM, _HBM, pl.BlockSpec(memory_space=pltpu.VMEM)),
        input_output_aliases={0: 2, 1: 3}, compiler_params=pltpu.CompilerParams(has_side_effects=_EFFECT),
    )(pltpu.with_memory_space_constraint(v, pltpu.HBM),
      pltpu.with_memory_space_constraint(lax.empty(land_shape, v.dtype), pltpu.HBM))


def _send_shard_wait(send_sems, recv_sems, v_thru, land_thru, after, name):
    def body(v_ref, land_ref, send_sems, recv_sems, after_ref, v_dead, got_ref):
        for j, (px, py, pc) in enumerate(_chip_peers()):
            copy = pltpu.make_async_remote_copy(src_ref=v_ref, dst_ref=land_ref.at[2 * px + py], send_sem=send_sems.at[j],
                                                recv_sem=recv_sems.at[j], device_id=(px, py, pc), device_id_type=MESH)
            copy.wait_send()
            copy.wait_recv()

    return pl.pallas_call(
        body, name=name,
        out_shape=(pltpu.HBM(v_thru.shape, v_thru.dtype), pltpu.HBM(land_thru.shape, land_thru.dtype)),
        in_specs=(_HBM, _HBM, _SEM, _SEM, pl.BlockSpec(memory_space=pl.ANY)), out_specs=(_HBM, _HBM),
        input_output_aliases={0: 0, 1: 1}, compiler_params=pltpu.CompilerParams(has_side_effects=_EFFECT),
    )(v_thru, land_thru, send_sems, recv_sems, after)[1]


_SMALL = ["mem_norm_w", "norm_pre", "norm_post", "a_ln_w", "a_ln_b", "a_w_s", "a_b_s", "b_conv_w", "c_conv_w",
          "c_a_log", "c_dt_bias", "c_o_norm_w"]
_SMALL_SHAPES = {"mem_norm_w": (D,), "norm_pre": (4, D), "norm_post": (4, D), "a_ln_w": (2, D), "a_ln_b": (2, D),
                 "a_w_s": (2, 8, HD, HD), "a_b_s": (2, 8, HD), "b_conv_w": (1, 3, D), "c_conv_w": (1, 4, 3 * D),
                 "c_a_log": (1, 8), "c_dt_bias": (1, 8), "c_o_norm_w": (1, HD)}
_SHARDED_SMALL = {"a_ln_w": D // 4, "a_ln_b": D // 4, "b_conv_w": D // 4, "c_conv_w": 3 * D // 4}
_ROWS = [2048, 1280, 1284, 1536, 256]
_BIG_ROWS = sum(_ROWS)
_BIG_PAD = 6416
_SMALL_ROWS = 288


def _size(shape):
    n = 1
    for d in shape:
        n *= d
    return n


def kernel(x, mem, mem_norm_w, w_mem_kv, norm_pre, norm_post, w_out, a_w_in, a_ln_w, a_ln_b, a_w_s, a_b_s, b_w_in, b_conv_w, c_w_in, c_conv_w, c_a_log, c_dt_bias, c_o_norm_w, loss_target, m_mem_norm_w, m_w_mem_kv, m_norm_pre, m_norm_post, m_w_out, m_a_w_in, m_a_ln_w, m_a_ln_b, m_a_w_s, m_a_b_s, m_b_w_in, m_b_conv_w, m_c_w_in, m_c_conv_w, m_c_a_log, m_c_dt_bias, m_c_o_norm_w, v_mem_norm_w, v_w_mem_kv, v_norm_pre, v_norm_post, v_w_out, v_a_w_in, v_a_ln_w, v_a_ln_b, v_a_w_s, v_a_b_s, v_b_w_in, v_b_conv_w, v_c_w_in, v_c_conv_w, v_c_a_log, v_c_dt_bias, v_c_o_norm_w):
    names = ["mem_norm_w", "w_mem_kv", "norm_pre", "norm_post", "w_out", "a_w_in", "a_ln_w", "a_ln_b", "a_w_s", "a_b_s",
             "b_w_in", "b_conv_w", "c_w_in", "c_conv_w", "c_a_log", "c_dt_bias", "c_o_norm_w"]
    w = dict(zip(names, [mem_norm_w, w_mem_kv, norm_pre, norm_post, w_out, a_w_in, a_ln_w, a_ln_b, a_w_s, a_b_s, b_w_in,
                         b_conv_w, c_w_in, c_conv_w, c_a_log, c_dt_bias, c_o_norm_w]))
    m = dict(zip(names, [m_mem_norm_w, m_w_mem_kv, m_norm_pre, m_norm_post, m_w_out, m_a_w_in, m_a_ln_w, m_a_ln_b, m_a_w_s,
                         m_a_b_s, m_b_w_in, m_b_conv_w, m_c_w_in, m_c_conv_w, m_c_a_log, m_c_dt_bias, m_c_o_norm_w]))
    v = dict(zip(names, [v_mem_norm_w, v_w_mem_kv, v_norm_pre, v_norm_post, v_w_out, v_a_w_in, v_a_ln_w, v_a_ln_b, v_a_w_s,
                         v_a_b_s, v_b_w_in, v_b_conv_w, v_c_w_in, v_c_conv_w, v_c_a_log, v_c_dt_bias, v_c_o_norm_w]))
    chip = 2 * lax.axis_index("x") + lax.axis_index("y")

    def rows_of_ct(a):
        return a[0].T

    def with_mine(gathered, own):
        return lax.dynamic_update_slice(gathered, own[None], (chip,) + (0,) * own.ndim)

    first = [a_w_in[0].astype(BF16), w_out.reshape(D_CAT, D).astype(BF16), w_mem_kv.astype(BF16)]
    vec = jnp.concatenate([a_ln_w.reshape(-1), a_ln_b.reshape(-1), b_conv_w.reshape(-1), c_conv_w.reshape(-1)])
    vec = jnp.pad(vec, (0, 8 * D - vec.shape[0])).reshape(8, D)
    *gathered, gvec = _gather_classes(first, [W_CLASSES[k] for k in W_CLASSES], vec)
    ga0, go, gkv = [with_mine(g_, a) for g_, a in zip(gathered, first)]
    gvec = with_mine(gvec, vec)
    go = go.reshape(4, 4, 384, D)
    gv = gvec.reshape(4, 8 * D)
    later = {1: b_w_in[0], 2: jnp.pad(rows_of_ct(c_w_in), ((0, C_ROWS - 1284), (0, 0))), 3: a_w_in[1]}
    later = {i: (a + 0.0 * gkv[0, 0, 0].astype(F32)).astype(BF16) for i, a in later.items()}
    sent = {i: _send_shard_start(a, f"send_w_in_{i}") for i, a in later.items()}
    started = sum(s[4][0, 0] for s in sent.values())

    def blocks(i, after):
        if i == 0:
            return [ga0[0], ga0[1]], [ga0[2], ga0[3]]
        got = with_mine(_send_shard_wait(*sent[i][:4], after, f"wait_w_in_{i}"), later[i])
        if i == 1:
            return [got[0], got[1], got[2][:, :512]], [got[2][:, 512:], got[3]]
        if i == 3:
            return [got[0], got[1]], [got[2], got[3]]
        fct = got[:, :1284].reshape(5136, D)
        c_ab = jnp.concatenate([fct[3 * D:3 * D + 16], jnp.zeros((AB_PAD - 16, D), BF16)], axis=0)
        return [fct[:3 * D], c_ab], [fct[3 * D + 16:]]
    sm = {"mem_norm_w": mem_norm_w, "norm_pre": norm_pre + started, "norm_post": norm_post, "a_w_s": a_w_s, "a_b_s": a_b_s,
          "c_a_log": c_a_log, "c_dt_bias": c_dt_bias, "c_o_norm_w": c_o_norm_w,
          "a_ln_w": gv[:, 0:512].reshape(4, 2, 256).transpose(1, 0, 2).reshape(2, D),
          "a_ln_b": gv[:, 512:1024].reshape(4, 2, 256).transpose(1, 0, 2).reshape(2, D),
          "b_conv_w": gv[:, 1024:1792].reshape(4, 1, 3, 256).transpose(1, 2, 0, 3).reshape(1, 3, D),
          "c_conv_w": gv[:, 1792:4864].reshape(4, 1, 4, 768).transpose(1, 2, 0, 3).reshape(1, 4, 3 * D)}
    wts = {"wkv": gkv.reshape(D, 2 * D_XA), "wo": [go[:, i].reshape(D_CAT, D) for i in range(DEPTH)], "blocks": blocks}

    loss, dx, g = _local_step(x[0], mem[0], loss_target[0], wts, sm)
    loss = lax.psum(loss, ("x", "y", "c"))

    gct = jnp.concatenate([g["wm"][2][:3 * D + 16], g["wg"][2]], axis=0).reshape(4, 1284, D)
    by_chip = {
        "a": jnp.stack([jnp.concatenate([g["wm"][i], g["wg"][i]], axis=0) for i in (0, 3)], axis=1),
        "b": jnp.concatenate([g["wm"][1], g["wg"][1]], axis=0),
        "c": jnp.pad(gct, ((0, 0), (0, C_ROWS - 1284), (0, 0))),
        "o": jnp.stack(g["wo"]).reshape(4, 4, 384, D).transpose(1, 0, 2, 3),
        "kv": g["wkv"]}
    specs = [CLASSES[k] for k in CLASSES]
    halves = [by_chip[k].reshape(4, 2, CLASSES[k][0], -1) for k in CLASSES]
    gs = {"mem_norm_w": g["mem_norm_w"], "norm_pre": jnp.concatenate(g["norm_pre"]),
          "norm_post": jnp.concatenate(g["norm_post"])}
    for n in _SMALL[3:]:
        gs[n] = jnp.stack([g[n][j] for j in sorted(g[n])])
    flat = jnp.concatenate([gs[n].reshape(-1) for n in _SMALL])
    small = jnp.pad(flat, (0, _SMALL_ROWS * D - flat.shape[0])).reshape(_SMALL_ROWS, D)
    *others, other_small = _swap_classes(halves, specs, small)
    pairs = [_pair_sum_class(h, o, CLASSES[k][1], f"pair_sum_{k}") for k, h, o in zip(CLASSES, halves, others)]
    pair_small = _add(small, other_small, "pair_sum_small")
    *lands, land_small = _exchange_classes([p[0] for p in pairs], specs, pair_small)
    mine = [_chip_sum_class(p[1], land, CLASSES[k][1], f"chip_sum_{k}") for k, p, land in zip(CLASSES, pairs, lands)]
    theirs = _share_classes(mine, specs)
    south = lax.axis_index("c") == 0
    sh = {k: jnp.concatenate([jnp.where(south, a, b), jnp.where(south, b, a)], axis=0)
          for k, a, b in zip(CLASSES, mine, theirs)}
    grads = {"a_w_in": sh["a"].reshape(a_w_in.shape),
             "b_w_in": sh["b"].reshape(5, D, 256).transpose(1, 0, 2).reshape(b_w_in.shape),
             "c_w_in": sh["c"][:1284], "w_out": sh["o"].reshape(w_out.shape), "w_mem_kv": sh["kv"]}
    flat = _sum4(pair_small, land_small).reshape(-1)
    off = 0
    for n in _SMALL:
        shape = _SMALL_SHAPES[n]
        full = flat[off:off + _size(shape)].reshape(shape)
        off += _size(shape)
        if n in _SHARDED_SMALL:
            full = lax.dynamic_slice_in_dim(full, chip * _SHARDED_SMALL[n], _SHARDED_SMALL[n], axis=len(shape) - 1)
        grads[n] = full

    delta, new_m, new_v = {}, {}, {}
    for n in names:
        shape = w[n].shape
        if n == "c_w_in":
            d_, m_, v_ = _adamw(rows_of_ct(w[n]), grads[n], rows_of_ct(m[n]), rows_of_ct(v[n]), f"adamw_{n}")
            delta[n], new_m[n], new_v[n], grads[n] = d_.T[None], m_.T[None], v_.T[None], grads[n].T[None]
            continue
        view = (1, shape[0]) if len(shape) == 1 else (_size(shape[:-1]), shape[-1])
        d_, m_, v_ = _adamw(w[n].reshape(view), grads[n].reshape(view), m[n].reshape(view), v[n].reshape(view),
                            f"adamw_{n}")
        delta[n], new_m[n], new_v[n] = d_.reshape(shape), m_.reshape(shape), v_.reshape(shape)
    return (loss, dx[None], *[grads[n].reshape(w[n].shape) for n in names], *[delta[n] for n in names],
            *[new_m[n] for n in names], *[new_v[n] for n in names])
```

```python
import functools

import jax
import jax.numpy as jnp
from jax import lax
from jax.experimental import pallas as pl
from jax.experimental.pallas import tpu as pltpu

F32 = jnp.float32
BF16 = jnp.bfloat16
HI = lax.Precision.HIGHEST
MESH = pl.DeviceIdType.MESH
SDS = jax.ShapeDtypeStruct

D = 1024
D_XA = 512
D_CAT = 1536
N_MEM = 256
HD = 128
DEPTH = 4
EPS = 1e-6
TT = 512
DN_C = 64
DN_TB = 256
HALO = 8
AB_PAD = 128
VMEM_LIMIT = 56 * 1024 * 1024
GRAD_TILE = {0: 1024, 1: 256}

ADAM_LR, ADAM_B1, ADAM_B2, ADAM_EPS, ADAM_WD, ADAM_STEP = 0.001, 0.9, 0.999, 1e-08, 0.01, 10


def _params(n_grid, vmem=None):
    return pltpu.CompilerParams(dimension_semantics=("arbitrary",) * n_grid, vmem_limit_bytes=vmem)


def _rms(x, w):
    return x * lax.rsqrt(jnp.mean(x * x, axis=-1, keepdims=True) + EPS) * w


def _dot_nn(a, b):
    return jnp.dot(a.astype(BF16), b.astype(BF16), preferred_element_type=F32)


def _dot_nt(a, b):
    return lax.dot_general(a.astype(BF16), b.astype(BF16), (((1,), (1,)), ((), ())), preferred_element_type=F32)


def _dot_tn(a, b):
    return lax.dot_general(a.astype(BF16), b.astype(BF16), (((0,), (0,)), ((), ())), preferred_element_type=F32)


@jax.custom_vjp
def mm(a, b):
    return _dot_nn(a, b)


mm.defvjp(lambda a, b: (_dot_nn(a, b), (a, b)), lambda r, g: (_dot_nt(g, r[1]), _dot_tn(r[0], g)))


@jax.custom_vjp
def mm_nt(a, b):
    return _dot_nt(a, b)


mm_nt.defvjp(lambda a, b: (_dot_nt(a, b), (a, b)), lambda r, g: (_dot_nn(g, r[1]), _dot_tn(g, r[0])))


@jax.custom_vjp
def mm_tn(a, b):
    return _dot_tn(a, b)


mm_tn.defvjp(lambda a, b: (_dot_tn(a, b), (a, b)), lambda r, g: (_dot_nt(r[1], g), _dot_nn(r[0], g)))


def _dot_hi(a, b):
    return jnp.dot(a, b, precision=HI, preferred_element_type=F32)


def _row_spec(width, tile=TT):
    return pl.BlockSpec((tile, width), lambda i: (i, 0))


def _full_spec(shape):
    return pl.BlockSpec(shape, lambda *_: (0,) * len(shape))


def _widths(blocks, transposed):
    return [b.shape[0 if transposed else 1] for b in blocks]


def _inproj_fwd(x, nw, mix, gate, transposed, name):
    T, nm = x.shape[0], len(mix)
    M, G = sum(_widths(mix, transposed)), sum(_widths(gate, transposed))

    def body(x_ref, nw_ref, *refs):
        blocks, (pm_ref, pg_ref, h_ref) = refs[:-3], refs[-3:]
        h = _rms(x_ref[...], nw_ref[...]).astype(BF16)
        h_ref[...] = h
        for p_ref, group in ((pm_ref, blocks[:nm]), (pg_ref, blocks[nm:])):
            off = 0
            for w_ref in group:
                w = w_ref.shape[0 if transposed else 1]
                p_ref[:, off:off + w] = _dot_nt(h, w_ref[...]) if transposed else _dot_nn(h, w_ref[...])
                off += w

    return pl.pallas_call(
        body, name=name, grid=(T // TT,),
        in_specs=[_row_spec(D), _full_spec((1, D))] + [_full_spec(b.shape) for b in mix + gate],
        out_specs=[_row_spec(M), _row_spec(G), _row_spec(D)],
        out_shape=[SDS((T, M), F32), SDS((T, G), F32), SDS((T, D), BF16)],
        compiler_params=_params(1, VMEM_LIMIT))(x, nw, *mix, *gate)


def _inproj_bwd(dpm, dpg, x, nw, mix, gate, transposed, dxc, name):
    T, nm = x.shape[0], len(mix)
    M, G = sum(_widths(mix, transposed)), sum(_widths(gate, transposed))

    def body(dpm_ref, dpg_ref, x_ref, nw_ref, *refs):
        blocks, (dxc_ref, dx_ref, dnw_ref) = refs[:-3], refs[-3:]
        dh = None
        for dp_ref, group in ((dpm_ref, blocks[:nm]), (dpg_ref, blocks[nm:])):
            off = 0
            for w_ref in group:
                w = w_ref.shape[0 if transposed else 1]
                dp = dp_ref[:, off:off + w]
                part = _dot_nn(dp, w_ref[...]) if transposed else _dot_nt(dp, w_ref[...])
                dh = part if dh is None else dh + part
                off += w
        _, vjp = jax.vjp(_rms, x_ref[...], nw_ref[...])
        dxr, dnw = vjp(dh)
        dx_ref[...] = dxc_ref[...] + dxr

        @pl.when(pl.program_id(0) == 0)
        def _():
            dnw_ref[...] = jnp.zeros_like(dnw_ref)
        dnw_ref[...] += dnw

    return pl.pallas_call(
        body, name=name, grid=(T // TT,),
        in_specs=[_row_spec(M), _row_spec(G), _row_spec(D), _full_spec((1, D))]
        + [_full_spec(b.shape) for b in mix + gate] + [_row_spec(D)],
        out_specs=[_row_spec(D), _full_spec((1, D))],
        out_shape=[SDS((T, D), F32), SDS((1, D), F32)],
        compiler_params=_params(1, VMEM_LIMIT))(dpm, dpg, x, nw, *mix, *gate, dxc)


def _matmul_tn(a, b, name, sub=None):
    T, K = a.shape
    N = b.shape[1]
    tn = 1024 if N % 1024 == 0 else (640 if N % 640 == 0 else N)
    tt = min(1024, T)
    n_sub = 1 if sub is None else tn // sub

    def body(a_ref, b_ref, o_ref):
        @pl.when(pl.program_id(1) == 0)
        def _():
            o_ref[...] = jnp.zeros_like(o_ref)
        res = _dot_tn(a_ref[...], b_ref[...])
        if sub is None:
            o_ref[...] += res
        else:
            for i in range(n_sub):
                o_ref[i] += res[:, i * sub:(i + 1) * sub]

    if sub is None:
        out_spec, out_shape = pl.BlockSpec((K, tn), lambda j, t: (0, j)), SDS((K, N), F32)
    else:
        out_spec, out_shape = pl.BlockSpec((n_sub, K, sub), lambda j, t: (j, 0, 0)), SDS((N // sub, K, sub), F32)
    return pl.pallas_call(
        body, name=name, grid=(N // tn, T // tt),
        in_specs=[pl.BlockSpec((tt, K), lambda j, t: (t, 0)), pl.BlockSpec((tt, tn), lambda j, t: (t, j))],
        out_specs=out_spec, out_shape=out_shape,
        compiler_params=_params(2, VMEM_LIMIT))(a, b)


def _memkv_fn(mem, w, wkv):
    return mm(_rms(mem, w), wkv)


def _memkv_fwd(mem, w, wkv):
    def body(mem_ref, w_ref, wkv_ref, kv_ref):
        kv_ref[...] = _memkv_fn(mem_ref[...], w_ref[...], wkv_ref[...])

    return pl.pallas_call(body, name="memkv_fwd", out_shape=SDS((N_MEM, 2 * D_XA), F32),
                          compiler_params=_params(0, VMEM_LIMIT))(mem, w, wkv)


def _memkv_bwd(mem, w, wkv, dkv):
    def body(mem_ref, w_ref, wkv_ref, dkv_ref, dw_ref, dwkv_ref):
        _, vjp = jax.vjp(functools.partial(_memkv_fn, mem_ref[...]), w_ref[...], wkv_ref[...].astype(F32))
        dw, dwkv = vjp(dkv_ref[...])
        dw_ref[...] = dw
        dwkv_ref[...] = dwkv

    return pl.pallas_call(body, name="memkv_bwd", out_shape=[SDS((1, D), F32), SDS((D, 2 * D_XA), F32)],
                          compiler_params=_params(0, VMEM_LIMIT))(mem, w, wkv, dkv)


def _attn_gate(ymix, qx, z, *kvs):
    outs = []
    for j in range(4):
        s = mm_nt(qx[:, j * HD:(j + 1) * HD], kvs[j]) * (HD ** -0.5)
        e = jnp.exp(s - lax.stop_gradient(jnp.max(s, axis=-1, keepdims=True)))
        outs.append(mm(e / jnp.sum(e, axis=-1, keepdims=True), kvs[4 + j]))
    return jnp.concatenate([ymix] + outs, axis=1) * jax.nn.silu(z)


def _kv_blocks(kv_ref):
    return [kv_ref[:, j * HD:(j + 1) * HD] for j in range(8)]


def _ag_fwd(ymix, pg, kv, name):
    T = ymix.shape[0]

    def body(ymix_ref, pg_ref, kv_ref, ycat_ref):
        ycat_ref[...] = _attn_gate(ymix_ref[...], pg_ref[:, :D_XA], pg_ref[:, D_XA:], *_kv_blocks(kv_ref)).astype(BF16)

    return pl.pallas_call(
        body, name=name, grid=(T // TT,),
        in_specs=[_row_spec(D), _row_spec(D_XA + D_CAT), _full_spec((N_MEM, 2 * D_XA))],
        out_specs=_row_spec(D_CAT), out_shape=SDS((T, D_CAT), BF16),
        compiler_params=_params(1, VMEM_LIMIT))(ymix, pg, kv)


def _ag_bwd(dycat, ymix, pg, kv, dkv_in, name):
    T = ymix.shape[0]

    def body(dycat_ref, ymix_ref, pg_ref, kv_ref, dkvin_ref, dymix_ref, dpg_ref, dkv_ref):
        _, vjp = jax.vjp(_attn_gate, ymix_ref[...], pg_ref[:, :D_XA], pg_ref[:, D_XA:], *_kv_blocks(kv_ref))
        g = vjp(dycat_ref[...])
        dymix_ref[...] = g[0]
        dpg_ref[:, :D_XA] = g[1].astype(BF16)
        dpg_ref[:, D_XA:] = g[2].astype(BF16)

        @pl.when(pl.program_id(0) == 0)
        def _():
            dkv_ref[...] = dkvin_ref[...]
        for j in range(8):
            dkv_ref[:, j * HD:(j + 1) * HD] += g[3 + j]

    return pl.pallas_call(
        body, name=name, grid=(T // TT,),
        in_specs=[_row_spec(D_CAT), _row_spec(D), _row_spec(D_XA + D_CAT), _full_spec((N_MEM, 2 * D_XA)),
                  _full_spec((N_MEM, 2 * D_XA))],
        out_specs=[_row_spec(D), _row_spec(D_XA + D_CAT), _full_spec((N_MEM, 2 * D_XA))],
        out_shape=[SDS((T, D), F32), SDS((T, D_XA + D_CAT), BF16), SDS((N_MEM, 2 * D_XA), F32)],
        compiler_params=_params(1, VMEM_LIMIT))(dycat, ymix, pg, kv, dkv_in)


def _outproj_fwd(ycat, wo, x, nw, name):
    T = x.shape[0]

    def body(ycat_ref, wo_ref, x_ref, nw_ref, o_ref, xn_ref):
        o = jnp.dot(ycat_ref[...], wo_ref[...], preferred_element_type=F32)
        o_ref[...] = o
        xn_ref[...] = x_ref[...] + _rms(o, nw_ref[...])

    return pl.pallas_call(
        body, name=name, grid=(T // TT,),
        in_specs=[_row_spec(D_CAT), _full_spec((D_CAT, D)), _row_spec(D), _full_spec((1, D))],
        out_specs=[_row_spec(D), _row_spec(D)], out_shape=[SDS((T, D), F32), SDS((T, D), F32)],
        compiler_params=_params(1, VMEM_LIMIT))(ycat, wo, x, nw)


def _outproj_bwd(dxo, o, nw, wo, name):
    T = dxo.shape[0]

    def body(dxo_ref, o_ref, nw_ref, wo_ref, dycat_ref, dobf_ref, dnw_ref):
        _, vjp = jax.vjp(_rms, o_ref[...], nw_ref[...])
        do, dnw = vjp(dxo_ref[...])
        dobf = do.astype(BF16)
        dobf_ref[...] = dobf
        dycat_ref[...] = _dot_nt(dobf, wo_ref[...])

        @pl.when(pl.program_id(0) == 0)
        def _():
            dnw_ref[...] = jnp.zeros_like(dnw_ref)
        dnw_ref[...] += dnw

    return pl.pallas_call(
        body, name=name, grid=(T // TT,),
        in_specs=[_row_spec(D), _row_spec(D), _full_spec((1, D)), _full_spec((D_CAT, D))],
        out_specs=[_row_spec(D_CAT), _row_spec(D), _full_spec((1, D))],
        out_shape=[SDS((T, D_CAT), F32), SDS((T, D), BF16), SDS((1, D), F32)],
        compiler_params=_params(1, VMEM_LIMIT))(dxo, o, nw, wo)


def _loss_head(xl, target):
    T = xl.shape[0]

    def body(x_ref, t_ref, loss_ref, dx_ref):
        err = x_ref[...] - t_ref[...]
        dx_ref[...] = err * (1.0 / D)

        @pl.when(pl.program_id(0) == 0)
        def _():
            loss_ref[...] = jnp.zeros_like(loss_ref)
        part = jnp.sum(jnp.sum(err * err, axis=1, keepdims=True), axis=0, keepdims=True) * (0.5 / D)
        loss_ref[...] += jnp.broadcast_to(part, loss_ref.shape)

    return pl.pallas_call(
        body, name="loss_head", grid=(T // TT,),
        in_specs=[_row_spec(D), _row_spec(D)],
        out_specs=[_full_spec((8, 128)), _row_spec(D)], out_shape=[SDS((8, 128), F32), SDS((T, D), F32)],
        compiler_params=_params(1))(xl, target)


def _gmlp_pre(u, v, lnw, lnb):
    vg = jax.nn.gelu(v)
    xc = vg - jnp.mean(vg, axis=-1, keepdims=True)
    vl = xc * lax.rsqrt(jnp.mean(xc * xc, axis=-1, keepdims=True) + EPS) * lnw + lnb
    return jax.nn.gelu(u), vl


def _tril(n, strict=False):
    r = lax.broadcasted_iota(jnp.int32, (n, n), 0)
    c = lax.broadcasted_iota(jnp.int32, (n, n), 1)
    return (r > c) if strict else (r >= c)


def _gmlp_fwd(pm, lnw, lnb, ws, bs3, name):
    T = pm.shape[0]

    def body(pm_ref, lnw_ref, lnb_ref, ws_ref, bs_ref, y_ref):
        ug, vl = _gmlp_pre(pm_ref[:, :D], pm_ref[:, D:], lnw_ref[...], lnb_ref[...])
        mask = _tril(HD)
        for g in range(8):
            w = jnp.where(mask, ws_ref[g], 0.0)
            for c in range(TT // HD):
                rows, cols = slice(c * HD, (c + 1) * HD), slice(g * HD, (g + 1) * HD)
                y_ref[rows, cols] = ug[rows, cols] * (_dot_nn(w, vl[rows, cols]) + bs_ref[g])

    return pl.pallas_call(
        body, name=name, grid=(T // TT,),
        in_specs=[_row_spec(2 * D), _full_spec((1, D)), _full_spec((1, D)), _full_spec((8, HD, HD)),
                  _full_spec((8, HD, HD))],
        out_specs=_row_spec(D), out_shape=SDS((T, D), F32),
        compiler_params=_params(1, VMEM_LIMIT))(pm, lnw, lnb, ws, bs3)


def _gmlp_bwd(dy, pm, lnw, lnb, ws, bs3, name):
    T = pm.shape[0]
    n_t = T // TT

    def body(dy_ref, pm_ref, lnw_ref, lnb_ref, ws_ref, bs_ref, dpm_ref, dlnw_ref, dlnb_ref, dws_ref, dbs_ref,
             dug_scr, dvl_scr, dbs_scr):
        i = pl.program_id(0)

        @pl.when(i == 0)
        def _():
            dlnw_ref[...] = jnp.zeros_like(dlnw_ref)
            dlnb_ref[...] = jnp.zeros_like(dlnb_ref)
            dws_ref[...] = jnp.zeros_like(dws_ref)
            dbs_scr[...] = jnp.zeros_like(dbs_scr)

        (ug, vl), vjp = jax.vjp(_gmlp_pre, pm_ref[:, :D], pm_ref[:, D:], lnw_ref[...], lnb_ref[...])
        mask = _tril(HD)
        for g in range(8):
            w = jnp.where(mask, ws_ref[g], 0.0)
            dw = jnp.zeros((HD, HD), F32)
            db = jnp.zeros((HD, HD), F32)
            for c in range(TT // HD):
                rows, cols = slice(c * HD, (c + 1) * HD), slice(g * HD, (g + 1) * HD)
                dyb, vlb = dy_ref[rows, cols], vl[rows, cols]
                sp = _dot_nn(w, vlb) + bs_ref[g]
                dsp = dyb * ug[rows, cols]
                dug_scr[rows, cols] = dyb * sp
                dvl_scr[rows, cols] = _dot_tn(w, dsp)
                dw += _dot_nt(dsp, vlb)
                db += dsp
            dws_ref[g] += jnp.where(mask, dw, 0.0)
            dbs_scr[g] += db
        du, dv, dlnw, dlnb = vjp((dug_scr[...], dvl_scr[...]))
        dpm_ref[:, :D] = du.astype(BF16)
        dpm_ref[:, D:] = dv.astype(BF16)
        dlnw_ref[...] += dlnw
        dlnb_ref[...] += dlnb

        @pl.when(i == n_t - 1)
        def _():
            for g in range(8):
                dbs_ref[g] = jnp.broadcast_to(jnp.sum(dbs_scr[g], axis=1, keepdims=True), (HD, HD))

    return pl.pallas_call(
        body, name=name, grid=(n_t,),
        in_specs=[_row_spec(D), _row_spec(2 * D), _full_spec((1, D)), _full_spec((1, D)), _full_spec((8, HD, HD)),
                  _full_spec((8, HD, HD))],
        out_specs=[_row_spec(2 * D), _full_spec((1, D)), _full_spec((1, D)), _full_spec((8, HD, HD)),
                   _full_spec((8, HD, HD))],
        out_shape=[SDS((T, 2 * D), BF16), SDS((1, D), F32), SDS((1, D), F32), SDS((8, HD, HD), F32),
                   SDS((8, HD, HD), F32)],
        scratch_shapes=[pltpu.VMEM((TT, D), F32), pltpu.VMEM((TT, D), F32), pltpu.VMEM((8, HD, HD), F32)],
        compiler_params=_params(1, VMEM_LIMIT))(dy, pm, lnw, lnb, ws, bs3)


def _prev_spec(width, T):
    return pl.BlockSpec((HALO, width), lambda i: (jnp.maximum(i * (TT // HALO) - 1, 0), 0))


def _next_spec(width, T):
    return pl.BlockSpec((HALO, width), lambda i: (jnp.minimum((i + 1) * (TT // HALO), T // HALO - 1), 0))


def _rows_before(ext, j):
    return ext[HALO:] if j == 0 else pltpu.roll(ext, j, 0)[HALO:]


def _rows_after(ext, j):
    n = ext.shape[0]
    return ext[:n - HALO] if j == 0 else pltpu.roll(ext, n - j, 0)[:n - HALO]


def _conv_apply(ext_s, w):
    K = w.shape[0]
    y = _rows_before(ext_s, K - 1) * w[0:1]
    for k in range(1, K):
        y = y + _rows_before(ext_s, K - 1 - k) * w[k:k + 1]
    return y


def _conv_grads(ext_s, ext_dy, w):
    K = w.shape[0]
    dy = ext_dy[:ext_dy.shape[0] - HALO]
    ds = _rows_after(ext_dy, K - 1) * w[0:1]
    dws = [jnp.sum(dy * _rows_before(ext_s, K - 1), axis=0, keepdims=True)]
    for k in range(1, K):
        ds = ds + _rows_after(ext_dy, K - 1 - k) * w[k:k + 1]
        dws.append(jnp.sum(dy * _rows_before(ext_s, K - 1 - k), axis=0, keepdims=True))
    return ds, jnp.concatenate(dws, axis=0)


def _sconv_fwd(pm, w, name):
    T = pm.shape[0]

    def body(pm_ref, prev_ref, w_ref, y_ref):
        s = pm_ref[:, D:2 * D] * pm_ref[:, 2 * D:]
        sp = jnp.where(pl.program_id(0) > 0, prev_ref[:, D:2 * D] * prev_ref[:, 2 * D:], 0.0)
        y_ref[...] = pm_ref[:, :D] * _conv_apply(jnp.concatenate([sp, s], axis=0), w_ref[...])

    return pl.pallas_call(
        body, name=name, grid=(T // TT,),
        in_specs=[_row_spec(3 * D), _prev_spec(3 * D, T), _full_spec((3, D))],
        out_specs=_row_spec(D), out_shape=SDS((T, D), F32),
        compiler_params=_params(1, VMEM_LIMIT))(pm, pm, w)


def _sconv_bwd(dy, pm, w, name):
    T = pm.shape[0]
    n_t = T // TT

    def body(dy_ref, dyn_ref, pm_ref, prev_ref, next_ref, w_ref, dpm_ref, dw_ref):
        i = pl.program_id(0)
        bg, cg, hv = pm_ref[:, :D], pm_ref[:, D:2 * D], pm_ref[:, 2 * D:]
        sp = jnp.where(i > 0, prev_ref[:, D:2 * D] * prev_ref[:, 2 * D:], 0.0)
        ext_s = jnp.concatenate([sp, cg * hv], axis=0)
        dyv = dy_ref[...]
        dcn = jnp.where(i < n_t - 1, dyn_ref[...] * next_ref[:, :D], 0.0)
        ds, dw = _conv_grads(ext_s, jnp.concatenate([dyv * bg, dcn], axis=0), w_ref[...])
        dpm_ref[:, :D] = (dyv * _conv_apply(ext_s, w_ref[...])).astype(BF16)
        dpm_ref[:, D:2 * D] = (ds * hv).astype(BF16)
        dpm_ref[:, 2 * D:] = (ds * cg).astype(BF16)

        @pl.when(i == 0)
        def _():
            dw_ref[...] = jnp.zeros_like(dw_ref)
        dw_ref[...] += dw

    return pl.pallas_call(
        body, name=name, grid=(n_t,),
        in_specs=[_row_spec(D), _next_spec(D, T), _row_spec(3 * D), _prev_spec(3 * D, T), _next_spec(3 * D, T),
                  _full_spec((3, D))],
        out_specs=[_row_spec(3 * D), _full_spec((3, D))],
        out_shape=[SDS((T, 3 * D), BF16), SDS((3, D), F32)],
        compiler_params=_params(1, VMEM_LIMIT))(dy, dy, pm, pm, pm, w)


def _dnconv_fwd(pm, w, name):
    T = pm.shape[0]

    def body(pm_ref, prev_ref, w_ref, c_ref):
        sp = jnp.where(pl.program_id(0) > 0, prev_ref[...], 0.0)
        c_ref[...] = _conv_apply(jnp.concatenate([sp, pm_ref[...]], axis=0), w_ref[...])

    return pl.pallas_call(
        body, name=name, grid=(T // TT,),
        in_specs=[_row_spec(3 * D), _prev_spec(3 * D, T), _full_spec((4, 3 * D))],
        out_specs=_row_spec(3 * D), out_shape=SDS((T, 3 * D), F32),
        compiler_params=_params(1, VMEM_LIMIT))(pm, pm, w)


def _dnconv_bwd(dcq, dck, dcv, dab, pm, w, name):
    T = pm.shape[0]
    n_t = T // TT

    def body(dq_ref, dk_ref, dv_ref, dqn_ref, dkn_ref, dvn_ref, dab_ref, pm_ref, prev_ref, w_ref, dpm_ref, dw_ref):
        i = pl.program_id(0)
        sp = jnp.where(i > 0, prev_ref[...], 0.0)
        ext_s = jnp.concatenate([sp, pm_ref[...]], axis=0)
        own = jnp.concatenate([dq_ref[...], dk_ref[...], dv_ref[...]], axis=1)
        nxt = jnp.where(i < n_t - 1, jnp.concatenate([dqn_ref[...], dkn_ref[...], dvn_ref[...]], axis=1), 0.0)
        ds, dw = _conv_grads(ext_s, jnp.concatenate([own, nxt], axis=0), w_ref[...])
        dpm_ref[:, :3 * D] = ds.astype(BF16)
        dpm_ref[:, 3 * D:] = dab_ref[...].astype(BF16)

        @pl.when(i == 0)
        def _():
            dw_ref[...] = jnp.zeros_like(dw_ref)
        dw_ref[...] += dw

    return pl.pallas_call(
        body, name=name, grid=(n_t,),
        in_specs=[_row_spec(D), _row_spec(D), _row_spec(D), _next_spec(D, T), _next_spec(D, T), _next_spec(D, T),
                  _row_spec(AB_PAD), _row_spec(3 * D), _prev_spec(3 * D, T), _full_spec((4, 3 * D))],
        out_specs=[_row_spec(3 * D + AB_PAD), _full_spec((4, 3 * D))],
        out_shape=[SDS((T, 3 * D + AB_PAD), BF16), SDS((4, 3 * D), F32)],
        compiler_params=_params(1, VMEM_LIMIT))(dcq, dck, dcv, dcq, dck, dcv, dab, pm, pm, w)


def _l2n(x):
    return x * lax.rsqrt(jnp.sum(x * x, axis=-1, keepdims=True) + EPS)


def _softplus(x):
    return jnp.maximum(x, 0.0) + jnp.log1p(jnp.exp(-jnp.abs(x)))


def _dn_gates(ab, alog, dtb, h):
    lane = lax.broadcasted_iota(jnp.int32, ab.shape, 1)
    g_all = -jnp.exp(alog) * _softplus(ab + dtb)
    g = jnp.sum(jnp.where(lane == h, g_all, 0.0), axis=1, keepdims=True)
    beta = jnp.sum(jnp.where(lane == 8 + h, jax.nn.sigmoid(ab), 0.0), axis=1, keepdims=True)
    ones = jnp.ones((1, HD), F32)
    return g * ones, beta * ones


def _dn_chunk(cq, ck, cv, gb, bb, S, onw):
    C = DN_C
    q = _l2n(jax.nn.silu(cq)) * (HD ** -0.5)
    k = _l2n(jax.nn.silu(ck))
    v = jax.nn.silu(cv)
    incl, strict = _tril(C), _tril(C, strict=True)
    gcum = _dot_hi(incl.astype(F32), gb)
    gi = gcum[:, :C]
    gj = gcum.T[:C, :]
    decay = jnp.where(incl, jnp.exp(jnp.where(incl, gi - gj, 0.0)), 0.0)
    kb = k * bb
    a_mat = jnp.where(strict, mm_nt(kb, k) * decay, 0.0)
    p = -a_mat
    eye = (lax.broadcasted_iota(jnp.int32, (C, C), 0) == lax.broadcasted_iota(jnp.int32, (C, C), 1)).astype(F32)
    t_mat = eye + p
    for _ in range(5):
        p = _dot_hi(p, p)
        t_mat = t_mat + _dot_hi(t_mat, p)
    eg = jnp.exp(gcum)
    u = mm(t_mat, v * bb)
    w = mm(t_mat, kb * eg)
    qk = mm_nt(q, k) * decay
    glast = gcum[C - 1:C, :]
    v_new = u - mm(w, S)
    o = mm(q * eg, S) + mm(qk, v_new)
    s_new = S * jnp.exp(glast) + mm_tn(k * jnp.exp(glast - gcum), v_new)
    return _rms(o, onw), s_new


def _dn_specs(T, rev):
    nb = T // DN_TB
    blk = (lambda n: nb - 1 - n) if rev else (lambda n: n)
    head = [pl.BlockSpec((DN_TB, HD), functools.partial(lambda n, h, off: (blk(n), off + h), off=8 * s)) for s in range(3)]
    ab = pl.BlockSpec((DN_TB, AB_PAD), lambda n, h: (blk(n), 3 * D // AB_PAD))
    st = pl.BlockSpec((DN_TB // DN_C, None, HD, HD), lambda n, h: (blk(n), h, 0, 0))
    out = pl.BlockSpec((DN_TB, HD), lambda n, h: (blk(n), h))
    row = pl.BlockSpec((1, HD), lambda n, h: (0, 0))
    return nb, head, ab, st, out, row


def _dn_fwd(cpre, pm, alog, dtb, onw, name):
    T = cpre.shape[0]
    nb, head, ab, st, out, row = _dn_specs(T, False)

    def body(cq_ref, ck_ref, cv_ref, ab_ref, alog_ref, dtb_ref, onw_ref, o_ref, st_ref, s_scr):
        n, h = pl.program_id(0), pl.program_id(1)

        @pl.when(n == 0)
        def _():
            s_scr[h] = jnp.zeros((HD, HD), F32)
        gb, bb = _dn_gates(ab_ref[...], alog_ref[...], dtb_ref[...], h)
        S = s_scr[h]
        for c in range(DN_TB // DN_C):
            rows = slice(c * DN_C, (c + 1) * DN_C)
            st_ref[c] = S
            o, S = _dn_chunk(cq_ref[rows, :], ck_ref[rows, :], cv_ref[rows, :], gb[rows], bb[rows], S, onw_ref[...])
            o_ref[rows, :] = o
        s_scr[h] = S

    return pl.pallas_call(
        body, name=name, grid=(nb, 8),
        in_specs=head + [ab, row, row, row], out_specs=[out, st],
        out_shape=[SDS((T, D), F32), SDS((T // DN_C, 8, HD, HD), F32)],
        scratch_shapes=[pltpu.VMEM((8, HD, HD), F32)],
        compiler_params=_params(2, VMEM_LIMIT))(cpre, cpre, cpre, pm, alog, dtb, onw)


def _dn_bwd(do, cpre, pm, st, alog, dtb, onw, name):
    T = cpre.shape[0]
    nb, head, ab, stspec, out, row = _dn_specs(T, True)

    def body(do_ref, cq_ref, ck_ref, cv_ref, ab_ref, st_ref, alog_ref, dtb_ref, onw_ref,
             dcq_ref, dck_ref, dcv_ref, dab_ref, dalog_ref, ddtb_ref, donw_ref, ds_scr):
        n, h = pl.program_id(0), pl.program_id(1)

        @pl.when(n == 0)
        def _():
            ds_scr[h] = jnp.zeros((HD, HD), F32)

        @pl.when((n == 0) & (h == 0))
        def _():
            dalog_ref[...] = jnp.zeros_like(dalog_ref)
            ddtb_ref[...] = jnp.zeros_like(ddtb_ref)
            donw_ref[...] = jnp.zeros_like(donw_ref)

        @pl.when(h == 0)
        def _():
            dab_ref[...] = jnp.zeros_like(dab_ref)

        (gb, bb), gates_vjp = jax.vjp(lambda a, b, c: _dn_gates(a, b, c, h), ab_ref[...], alog_ref[...], dtb_ref[...])
        dS = ds_scr[h]
        n_c = DN_TB // DN_C
        dgs, dbs = [None] * n_c, [None] * n_c
        donw = jnp.zeros((1, HD), F32)
        for c in reversed(range(n_c)):
            rows = slice(c * DN_C, (c + 1) * DN_C)
            _, vjp = jax.vjp(_dn_chunk, cq_ref[rows, :], ck_ref[rows, :], cv_ref[rows, :], gb[rows], bb[rows],
                             st_ref[c], onw_ref[...])
            dcq, dck, dcv, dgs[c], dbs[c], dS, dn = vjp((do_ref[rows, :], dS))
            dcq_ref[rows, :] = dcq
            dck_ref[rows, :] = dck
            dcv_ref[rows, :] = dcv
            donw += dn
        ds_scr[h] = dS
        dab, dalog, ddtb = gates_vjp((jnp.concatenate(dgs, axis=0), jnp.concatenate(dbs, axis=0)))
        dab_ref[...] += dab
        dalog_ref[...] += dalog
        ddtb_ref[...] += ddtb
        donw_ref[...] += donw

    dabspec = pl.BlockSpec((DN_TB, AB_PAD), lambda n, h: (nb - 1 - n, 0))
    return pl.pallas_call(
        body, name=name, grid=(nb, 8),
        in_specs=[out] + head + [ab, stspec, row, row, row],
        out_specs=[out, out, out, dabspec, row, row, row],
        out_shape=[SDS((T, D), F32)] * 3 + [SDS((T, AB_PAD), F32)] + [SDS((1, HD), F32)] * 3,
        scratch_shapes=[pltpu.VMEM((8, HD, HD), F32)],
        compiler_params=_params(2, VMEM_LIMIT))(do, cpre, cpre, cpre, pm, st, alog, dtb, onw)


_BNN = (((2,), (1,)), ((0,), (0,)))
_BNT = (((2,), (2,)), ((0,), (0,)))
_BTN = (((1,), (1,)), ((0,), (0,)))


def _bdot(a, b, dims):
    return lax.dot_general(a.astype(BF16), b.astype(BF16), dims, preferred_element_type=F32)


def _bdot3(a, b, dims):
    ah, bh = a.astype(BF16), b.astype(BF16)
    al, bl = (a - ah.astype(F32)).astype(BF16), (b - bh.astype(F32)).astype(BF16)
    d = functools.partial(lax.dot_general, dimension_numbers=dims, preferred_element_type=F32)
    return d(ah, bh) + (d(ah, bl) + d(al, bh))


def _bdot_hi(a, b, dims):
    return lax.dot_general(a, b, dims, precision=HI, preferred_element_type=F32)


def _batched_matmuls(dot):
    @jax.custom_vjp
    def nn(a, b):
        return dot(a, b, _BNN)

    @jax.custom_vjp
    def nt(a, b):
        return dot(a, b, _BNT)

    @jax.custom_vjp
    def tn(a, b):
        return dot(a, b, _BTN)

    nn.defvjp(lambda a, b: (dot(a, b, _BNN), (a, b)), lambda r, g: (dot(g, r[1], _BNT), dot(r[0], g, _BTN)))
    nt.defvjp(lambda a, b: (dot(a, b, _BNT), (a, b)), lambda r, g: (dot(g, r[1], _BNN), dot(g, r[0], _BTN)))
    tn.defvjp(lambda a, b: (dot(a, b, _BTN), (a, b)), lambda r, g: (dot(r[1], g, _BNT), dot(r[0], g, _BNN)))
    return nn, nt, tn


bmm, bmm_nt, bmm_tn = _batched_matmuls(_bdot)
bmm3, _, _ = _batched_matmuls(_bdot3)
bmm_hi, bmm_hi_nt, _ = _batched_matmuls(_bdot_hi)

@jax.custom_vjp
def _neumann_inverse(n):
    C = n.shape[1]
    eye = lax.broadcasted_iota(jnp.int32, n.shape, 1) == lax.broadcasted_iota(jnp.int32, n.shape, 2)
    t = eye.astype(F32) + n
    for _ in range(5):
        n = _bdot3(n, n, _BNN)
        t = t + _bdot3(t, n, _BNN)
    return t


def _neumann_inverse_fwd(n):
    t = _neumann_inverse(n)
    return t, t


def _neumann_inverse_bwd(t, g):
    return (_bdot3(_bdot3(t, g, _BTN), t, _BNT),)


_neumann_inverse.defvjp(_neumann_inverse_fwd, _neumann_inverse_bwd)


@jax.custom_vjp
def _saved_inverse(n, t):
    return t


_saved_inverse.defvjp(lambda n, t: (t, t), lambda t, g: (_bdot3(_bdot3(t, g, _BTN), t, _BNT), jnp.zeros_like(t)))

DN_NCH = DN_TB // DN_C
DN_NH = 4


def _decay_terms(ab, alog, dtb, first_head, n_heads):
    C = DN_C
    lane = lax.broadcasted_iota(jnp.int32, ab.shape, 1)
    g_all = (-jnp.exp(alog) * _softplus(ab + dtb)).reshape(DN_NCH, C, HD)
    beta_all = jax.nn.sigmoid(ab)
    r = lax.broadcasted_iota(jnp.int32, (DN_NCH, C, C), 1)
    c = lax.broadcasted_iota(jnp.int32, (DN_NCH, C, C), 2)
    gc_all = bmm_hi((r >= c).astype(F32), g_all)
    gc_rows = [gc_all[i].T for i in range(DN_NCH)]
    lane3 = lax.broadcasted_iota(jnp.int32, (DN_NCH, C, HD), 2)
    row = lax.broadcasted_iota(jnp.int32, (HD, C), 0)
    ones = jnp.ones((1, HD), F32)
    gcs, gjs, betas = [], [], []
    for i in range(n_heads):
        h = first_head + i
        gcs.append(jnp.sum(jnp.where(lane3 == h, gc_all, 0.0), axis=2, keepdims=True) * ones)
        gjs.append(jnp.concatenate(
            [jnp.broadcast_to(jnp.sum(jnp.where(row == h, t, 0.0), axis=0, keepdims=True), (C, C))[None] for t in gc_rows],
            axis=0))
        beta = jnp.sum(jnp.where(lane == 8 + h, beta_all, 0.0), axis=1, keepdims=True) * ones
        betas.append(beta.reshape(DN_NCH, C, HD))
    return jnp.concatenate(gcs, axis=0), jnp.concatenate(gjs, axis=0), jnp.concatenate(betas, axis=0)


def _dn_prep(cq, ck, cv, gcum, gj, bb, t_saved=None):
    B, C = cq.shape[0], DN_C
    q = _l2n(jax.nn.silu(cq)) * (HD ** -0.5)
    k = _l2n(jax.nn.silu(ck))
    v = jax.nn.silu(cv)
    r = lax.broadcasted_iota(jnp.int32, (B, C, C), 1)
    c = lax.broadcasted_iota(jnp.int32, (B, C, C), 2)
    incl, strict = r >= c, r > c
    decay = jnp.where(incl, jnp.exp(jnp.where(incl, gcum[:, :, :C] - gj, 0.0)), 0.0)
    kb = k * bb
    n_mat = -jnp.where(strict, bmm_nt(kb, k) * decay, 0.0)
    t_mat = _neumann_inverse(n_mat) if t_saved is None else _saved_inverse(n_mat, t_saved)
    eg = jnp.exp(gcum)
    glast = gcum[:, C - 1:C, :]
    return (bmm(t_mat, v * bb), bmm(t_mat, kb * eg), bmm_nt(q, k) * decay, q * eg, k * jnp.exp(glast - gcum),
            jnp.exp(glast), t_mat)


def _dn_scan_step(u, w, qk, qd, kd, egl, S, onw):
    v_new = u - bmm(w, S)
    o = bmm(qd, S) + bmm(qk, v_new)
    return _rms(o, onw), S * egl + bmm_tn(kd, v_new)


def _head_gates(ab, alog, dtb, first_head, n_heads):
    gs, bs = [], []
    for i in range(n_heads):
        g, b = _dn_gates(ab, alog, dtb, first_head + i)
        gs.append(g.reshape(DN_NCH, DN_C, HD))
        bs.append(b.reshape(DN_NCH, DN_C, HD))
    return jnp.concatenate(gs, axis=0), jnp.concatenate(bs, axis=0)


def _to_batch(ref, n_heads):
    return jnp.concatenate([ref[:, i * HD:(i + 1) * HD].astype(F32).reshape(DN_NCH, DN_C, HD) for i in range(n_heads)],
                           axis=0)


def _from_batch(ref, val, n_heads):
    for i in range(n_heads):
        ref[:, i * HD:(i + 1) * HD] = val[i * DN_NCH:(i + 1) * DN_NCH].reshape(DN_TB, HD).astype(ref.dtype)


def _prep_specs(T, rev):
    nb = T // DN_TB
    blk = (lambda n: nb - 1 - n) if rev else (lambda n: n)
    ng = 8 // DN_NH
    head = [pl.BlockSpec((DN_TB, DN_NH * HD), functools.partial(lambda n, h, off: (blk(n), off + h), off=ng * s))
            for s in range(3)]
    ab = pl.BlockSpec((DN_TB, AB_PAD), lambda n, h: (blk(n), 3 * D // AB_PAD))
    row = pl.BlockSpec((1, HD), lambda n, h: (0, 0))
    wide = pl.BlockSpec((DN_TB, DN_NH * HD), lambda n, h: (blk(n), h))
    qk = pl.BlockSpec((DN_NCH, DN_NH, DN_C, DN_C), lambda n, h: (blk(n), h, 0, 0))
    eg = pl.BlockSpec((DN_NCH, DN_NH, 1, HD), lambda n, h: (blk(n), h, 0, 0))
    return nb, ng, head, ab, row, wide, qk, eg


def _dn_prep_fwd(cpre, pm, alog, dtb, name):
    T = cpre.shape[0]
    nb, ng, head, ab, row, wide, qks, egs = _prep_specs(T, False)

    def body(cq_ref, ck_ref, cv_ref, ab_ref, alog_ref, dtb_ref, u_ref, w_ref, qk_ref, qd_ref, kd_ref, e_ref, t_ref):
        gcum, gj, bb = _decay_terms(ab_ref[...], alog_ref[...], dtb_ref[...], pl.program_id(1) * DN_NH, DN_NH)
        u, w, qk, qd, kd, egl, t_mat = _dn_prep(_to_batch(cq_ref, DN_NH), _to_batch(ck_ref, DN_NH),
                                                _to_batch(cv_ref, DN_NH), gcum, gj, bb)
        _from_batch(u_ref, u, DN_NH)
        _from_batch(w_ref, w, DN_NH)
        _from_batch(qd_ref, qd, DN_NH)
        _from_batch(kd_ref, kd, DN_NH)
        for i in range(DN_NH):
            qk_ref[:, i] = qk[i * DN_NCH:(i + 1) * DN_NCH].astype(BF16)
            e_ref[:, i] = egl[i * DN_NCH:(i + 1) * DN_NCH]
            t_ref[:, i] = t_mat[i * DN_NCH:(i + 1) * DN_NCH]

    return pl.pallas_call(
        body, name=name, grid=(nb, ng), in_specs=head + [ab, row, row],
        out_specs=[wide, wide, qks, wide, wide, egs, qks],
        out_shape=[SDS((T, D), F32), SDS((T, D), BF16), SDS((T // DN_C, 8, DN_C, DN_C), BF16), SDS((T, D), BF16),
                   SDS((T, D), BF16), SDS((T // DN_C, 8, 1, HD), F32), SDS((T // DN_C, 8, DN_C, DN_C), F32)],
        compiler_params=_params(2, VMEM_LIMIT))(cpre, cpre, cpre, pm, alog, dtb)


def _dn_prep_bwd(du, dw, dqk, dqd, dkd, degl, t_mat, cpre, pm, alog, dtb, name):
    T = cpre.shape[0]
    nb, ng, head, ab, row, wide, qks, egs = _prep_specs(T, True)

    def body(du_ref, dw_ref, dqk_ref, dqd_ref, dkd_ref, de_ref, t_ref, cq_ref, ck_ref, cv_ref, ab_ref, alog_ref,
             dtb_ref, dcq_ref, dck_ref, dcv_ref, dab_ref, dalog_ref, ddtb_ref):
        n, h = pl.program_id(0), pl.program_id(1)

        @pl.when((n == 0) & (h == 0))
        def _():
            dalog_ref[...] = jnp.zeros_like(dalog_ref)
            ddtb_ref[...] = jnp.zeros_like(ddtb_ref)

        @pl.when(h == 0)
        def _():
            dab_ref[...] = jnp.zeros_like(dab_ref)

        t_saved = jnp.concatenate([t_ref[:, i] for i in range(DN_NH)], axis=0)

        def fwd(cq, ck, cv, ab_v, alog_v, dtb_v):
            gcum, gj, bb = _decay_terms(ab_v, alog_v, dtb_v, h * DN_NH, DN_NH)
            return _dn_prep(cq, ck, cv, gcum, gj, bb, t_saved)[:6]

        _, vjp = jax.vjp(fwd, _to_batch(cq_ref, DN_NH), _to_batch(ck_ref, DN_NH), _to_batch(cv_ref, DN_NH), ab_ref[...],
                         alog_ref[...], dtb_ref[...])
        cot = (_to_batch(du_ref, DN_NH), _to_batch(dw_ref, DN_NH),
               jnp.concatenate([dqk_ref[:, i] for i in range(DN_NH)], axis=0), _to_batch(dqd_ref, DN_NH),
               _to_batch(dkd_ref, DN_NH), jnp.concatenate([de_ref[:, i] for i in range(DN_NH)], axis=0))
        dcq, dck, dcv, dab, dalog, ddtb = vjp(cot)
        _from_batch(dcq_ref, dcq, DN_NH)
        _from_batch(dck_ref, dck, DN_NH)
        _from_batch(dcv_ref, dcv, DN_NH)
        dab_ref[...] += dab
        dalog_ref[...] += dalog
        ddtb_ref[...] += ddtb

    dabspec = pl.BlockSpec((DN_TB, AB_PAD), lambda n, h: (nb - 1 - n, 0))
    return pl.pallas_call(
        body, name=name, grid=(nb, ng),
        in_specs=[wide, wide, qks, wide, wide, egs, qks] + head + [ab, row, row],
        out_specs=[wide, wide, wide, dabspec, row, row],
        out_shape=[SDS((T, D), F32)] * 3 + [SDS((T, AB_PAD), F32)] + [SDS((1, HD), F32)] * 2,
        compiler_params=_params(2, VMEM_LIMIT))(du, dw, dqk, dqd, dkd, degl, t_mat, cpre, cpre, cpre, pm, alog, dtb)


def _scan_specs(T, rev):
    nb = T // DN_TB
    blk = (lambda n: nb - 1 - n) if rev else (lambda n: n)
    wide = pl.BlockSpec((DN_TB, D), lambda n: (blk(n), 0))
    qk = pl.BlockSpec((DN_NCH, 8, DN_C, DN_C), lambda n: (blk(n), 0, 0, 0))
    eg = pl.BlockSpec((DN_NCH, 8, 1, HD), lambda n: (blk(n), 0, 0, 0))
    st = pl.BlockSpec((DN_NCH, 8, HD, HD), lambda n: (blk(n), 0, 0, 0))
    row = pl.BlockSpec((1, HD), lambda n: (0, 0))
    return nb, wide, qk, eg, st, row


def _heads_of(ref, rows):
    return jnp.concatenate([ref[rows, h * HD:(h + 1) * HD].astype(F32)[None] for h in range(8)], axis=0)


def _dn_scan_fwd(u, w, qk, qd, kd, egl, onw, name):
    T = u.shape[0]
    nb, wide, qks, egs, sts, row = _scan_specs(T, False)

    def body(u_ref, w_ref, qk_ref, qd_ref, kd_ref, e_ref, onw_ref, o_ref, st_ref, s_scr):
        @pl.when(pl.program_id(0) == 0)
        def _():
            s_scr[...] = jnp.zeros_like(s_scr)
        S = s_scr[...]
        for c in range(DN_NCH):
            rows = slice(c * DN_C, (c + 1) * DN_C)
            st_ref[c] = S
            o, S = _dn_scan_step(_heads_of(u_ref, rows), _heads_of(w_ref, rows), qk_ref[c].astype(F32),
                                 _heads_of(qd_ref, rows), _heads_of(kd_ref, rows), e_ref[c], S, onw_ref[...])
            for h in range(8):
                o_ref[rows, h * HD:(h + 1) * HD] = o[h]
        s_scr[...] = S

    return pl.pallas_call(
        body, name=name, grid=(nb,), in_specs=[wide, wide, qks, wide, wide, egs, row], out_specs=[wide, sts],
        out_shape=[SDS((T, D), F32), SDS((T // DN_C, 8, HD, HD), F32)],
        scratch_shapes=[pltpu.VMEM((8, HD, HD), F32)],
        compiler_params=_params(1, VMEM_LIMIT))(u, w, qk, qd, kd, egl, onw)


def _dn_scan_bwd(do, u, w, qk, qd, kd, egl, st, onw, name):
    T = u.shape[0]
    nb, wide, qks, egs, sts, row = _scan_specs(T, True)

    def body(do_ref, u_ref, w_ref, qk_ref, qd_ref, kd_ref, e_ref, st_ref, onw_ref,
             du_ref, dw_ref, dqk_ref, dqd_ref, dkd_ref, de_ref, donw_ref, ds_scr):
        @pl.when(pl.program_id(0) == 0)
        def _():
            ds_scr[...] = jnp.zeros_like(ds_scr)
            donw_ref[...] = jnp.zeros_like(donw_ref)
        dS = ds_scr[...]
        donw = jnp.zeros((1, HD), F32)
        for c in reversed(range(DN_NCH)):
            rows = slice(c * DN_C, (c + 1) * DN_C)
            _, vjp = jax.vjp(_dn_scan_step, _heads_of(u_ref, rows), _heads_of(w_ref, rows), qk_ref[c].astype(F32),
                             _heads_of(qd_ref, rows), _heads_of(kd_ref, rows), e_ref[c], st_ref[c], onw_ref[...])
            du, dw, dqk, dqd, dkd, de, dS, dn = vjp((_heads_of(do_ref, rows), dS))
            for h in range(8):
                cols = slice(h * HD, (h + 1) * HD)
                du_ref[rows, cols] = du[h]
                dw_ref[rows, cols] = dw[h]
                dqd_ref[rows, cols] = dqd[h]
                dkd_ref[rows, cols] = dkd[h]
            dqk_ref[c] = dqk
            de_ref[c] = de
            donw += dn
        ds_scr[...] = dS
        donw_ref[...] += donw

    return pl.pallas_call(
        body, name=name, grid=(nb,), in_specs=[wide, wide, wide, qks, wide, wide, egs, sts, row],
        out_specs=[wide, wide, qks, wide, wide, egs, row],
        out_shape=[SDS((T, D), F32), SDS((T, D), F32), SDS((T // DN_C, 8, DN_C, DN_C), F32), SDS((T, D), F32),
                   SDS((T, D), F32), SDS((T // DN_C, 8, 1, HD), F32), SDS((1, HD), F32)],
        scratch_shapes=[pltpu.VMEM((8, HD, HD), F32)],
        compiler_params=_params(1, VMEM_LIMIT))(do, u, w, qk, qd, kd, egl, st, onw)


def _adamw(w, g, m, v, name):
    R, C = w.shape
    tr = 256 if R % 256 == 0 and R > 256 else R
    tc = 256 if tr == R and R > 256 and C % 256 == 0 else C
    c1 = 1.0 - ADAM_B1 ** ADAM_STEP
    c2 = 1.0 - ADAM_B2 ** ADAM_STEP

    def body(w_ref, g_ref, m_ref, v_ref, d_ref, nm_ref, nv_ref):
        gv = g_ref[...]
        nm = ADAM_B1 * m_ref[...] + (1.0 - ADAM_B1) * gv
        nv = ADAM_B2 * v_ref[...] + (1.0 - ADAM_B2) * (gv * gv)
        nm_ref[...] = nm
        nv_ref[...] = nv
        d_ref[...] = -ADAM_LR * ((nm / c1) / (jnp.sqrt(nv / c2) + ADAM_EPS) + ADAM_WD * w_ref[...])

    spec = pl.BlockSpec((tr, tc), lambda i, j: (i, j))
    return pl.pallas_call(
        body, name=name, grid=(R // tr, C // tc), in_specs=[spec] * 4, out_specs=[spec] * 3,
        out_shape=[SDS((R, C), F32)] * 3, compiler_params=_params(2, VMEM_LIMIT))(w, g, m, v)


def _local_step(x, mem, target, wts, sm):
    kinds = [i % 3 for i in range(DEPTH)]
    mnw = sm["mem_norm_w"].reshape(1, D)
    kv = _memkv_fwd(mem, mnw, wts["wkv"])
    saved, blocks = [], []
    for i, kind in enumerate(kinds):
        j = i // 3
        npre = sm["norm_pre"][i].reshape(1, D)
        npost = sm["norm_post"][i].reshape(1, D)
        mix, gate = wts["blocks"](i, x)
        blocks.append((mix, gate))
        pm, pg, h = _inproj_fwd(x, npre, mix, gate, kind == 2, f"inproj_fwd_{i}")
        extra = None
        if kind == 0:
            bs3 = jnp.broadcast_to(sm["a_b_s"][j][:, :, None], (8, HD, HD))
            ymix = _gmlp_fwd(pm, sm["a_ln_w"][j].reshape(1, D), sm["a_ln_b"][j].reshape(1, D), sm["a_w_s"][j], bs3,
                             f"gmlp_fwd_{i}")
            extra = bs3
        elif kind == 1:
            ymix = _sconv_fwd(pm, sm["b_conv_w"][j], f"sconv_fwd_{i}")
        else:
            cpre = _dnconv_fwd(pm, sm["c_conv_w"][j], f"dnconv_fwd_{i}")
            alog = jnp.pad(sm["c_a_log"][j], (0, HD - 8)).reshape(1, HD)
            dtb = jnp.pad(sm["c_dt_bias"][j], (0, HD - 8)).reshape(1, HD)
            onw = sm["c_o_norm_w"][j].reshape(1, HD)
            *prep, t_mat = _dn_prep_fwd(cpre, pm, alog, dtb, f"dn_prep_fwd_{i}")
            ymix, st = _dn_scan_fwd(*prep, onw, f"dn_scan_fwd_{i}")
            extra = (cpre, prep, t_mat, st, alog, dtb, onw)
        ycat = _ag_fwd(ymix, pg, kv, f"ag_fwd_{i}")
        o, xn = _outproj_fwd(ycat, wts["wo"][i], x, npost, f"outproj_fwd_{i}")
        saved.append((x, h, pm, pg, ymix, ycat, o, extra))
        x = xn

    loss, dx = _loss_head(x, target)

    g = {"wm": [None] * DEPTH, "wg": [None] * DEPTH, "wo": [None] * DEPTH, "norm_pre": [None] * DEPTH,
         "norm_post": [None] * DEPTH}
    dkv = jnp.zeros((N_MEM, 2 * D_XA), F32)
    sent = 0.0
    for i in reversed(range(DEPTH)):
        kind, j = kinds[i], i // 3
        xi, h, pm, pg, ymix, ycat, o, extra = saved[i]
        npre = sm["norm_pre"][i].reshape(1, D)
        npost = sm["norm_post"][i].reshape(1, D) + sent
        dycat, dobf, g["norm_post"][i] = _outproj_bwd(dx, o, npost, wts["wo"][i], f"outproj_bwd_{i}")
        g["wo"][i] = _matmul_tn(ycat, dobf, f"dwo_{i}")
        dymix, dpg, dkv = _ag_bwd(dycat, ymix, pg, kv, dkv, f"ag_bwd_{i}")
        if kind == 0:
            dpm, dlnw, dlnb, dws, dbs3 = _gmlp_bwd(dymix, pm, sm["a_ln_w"][j].reshape(1, D),
                                                   sm["a_ln_b"][j].reshape(1, D), sm["a_w_s"][j], extra,
                                                   f"gmlp_bwd_{i}")
            g.setdefault("a_ln_w", {})[j] = dlnw.reshape(D)
            g.setdefault("a_ln_b", {})[j] = dlnb.reshape(D)
            g.setdefault("a_w_s", {})[j] = dws
            g.setdefault("a_b_s", {})[j] = dbs3[:, :, 0]
        elif kind == 1:
            dpm, dcw = _sconv_bwd(dymix, pm, sm["b_conv_w"][j], f"sconv_bwd_{i}")
            g.setdefault("b_conv_w", {})[j] = dcw
        else:
            cpre, prep, t_mat, st, alog, dtb, onw = extra
            *dprep, donw = _dn_scan_bwd(dymix, *prep, st, onw, f"dn_scan_bwd_{i}")
            dcq, dck, dcv, dab, dalog, ddtb = _dn_prep_bwd(*dprep, t_mat, cpre, pm, alog, dtb, f"dn_prep_bwd_{i}")
            dpm, dcw = _dnconv_bwd(dcq, dck, dcv, dab, pm, sm["c_conv_w"][j], f"dnconv_bwd_{i}")
            g.setdefault("c_conv_w", {})[j] = dcw
            g.setdefault("c_a_log", {})[j] = dalog[0, :8]
            g.setdefault("c_dt_bias", {})[j] = ddtb[0, :8]
            g.setdefault("c_o_norm_w", {})[j] = donw[0]
        if kind == 2:
            g["wm"][i] = _matmul_tn(dpm, h, f"dwm_{i}")
            g["wg"][i] = _matmul_tn(dpg, h, f"dwg_{i}")
        else:
            g["wm"][i] = _matmul_tn(h, dpm, f"dwm_{i}", GRAD_TILE[kind])
            g["wg"][i] = _matmul_tn(h, dpg, f"dwg_{i}", GRAD_TILE[kind])
        dx, g["norm_pre"][i] = _inproj_bwd(dpm, dpg, xi, npre, blocks[i][0], blocks[i][1], kind == 2, dx,
                                           f"inproj_bwd_{i}")
        sent = wts["layer_done"](i, g)
    g["mem_norm_w"], g["wkv"] = _memkv_bwd(mem, mnw, wts["wkv"], dkv)
    return loss[0, 0], dx, g


ANY = pl.BlockSpec(memory_space=pl.ANY)


def _place():
    return lax.axis_index("x"), lax.axis_index("y"), lax.axis_index("c")


def _gather_weights(big, vec):
    def body(big_ref, vec_ref, ob_ref, ov_ref, ici_send, ici_recv, d2d_send, d2d_recv, vec_send, vec_recv):
        x, y, c = _place()
        chip = 2 * x + y
        peers = [(1 - x, y), (x, 1 - y), (1 - x, 1 - y)]

        def rows(half, k):
            return pl.ds(half * HALF_ROWS + k * CHUNK_ROWS, CHUNK_ROWS)

        def over_ici(j, k, slab):
            px, py = peers[j]
            i = j * N_CHUNKS + k
            return pltpu.make_async_remote_copy(
                src_ref=big_ref.at[rows(c, k)], dst_ref=ob_ref.at[slab, rows(c, k)], send_sem=ici_send.at[i],
                recv_sem=ici_recv.at[i], device_id=(px, py, c), device_id_type=MESH)

        def over_d2d(j, k, half):
            px, py = peers[j]
            i = j * N_CHUNKS + k
            where = ob_ref.at[2 * px + py, rows(half, k)]
            return pltpu.make_async_remote_copy(
                src_ref=where, dst_ref=where, send_sem=d2d_send.at[i], recv_sem=d2d_recv.at[i],
                device_id=(x, y, 1 - c), device_id_type=MESH)

        def small(j, slab):
            px, py = peers[j]
            return pltpu.make_async_remote_copy(
                src_ref=vec_ref, dst_ref=ov_ref.at[slab], send_sem=vec_send.at[j], recv_sem=vec_recv.at[j],
                device_id=(px, py, c), device_id_type=MESH)

        sends = [small(j, chip) for j in range(3)] + [over_ici(j, k, chip) for k in range(N_CHUNKS) for j in range(3)]
        for cp in sends:
            cp.start()
        forwards = []
        for k in range(N_CHUNKS):
            for j, (px, py) in enumerate(peers):
                over_ici(j, k, 2 * px + py).wait_recv()
                forwards.append(over_d2d(j, k, c))
                forwards[-1].start()
        for k in range(N_CHUNKS):
            for j in range(3):
                over_d2d(j, k, 1 - c).wait_recv()
        for j, (px, py) in enumerate(peers):
            small(j, 2 * px + py).wait_recv()
        for cp in sends + forwards:
            cp.wait_send()

    n = 3 * N_CHUNKS
    dma = pltpu.SemaphoreType.DMA
    return pl.pallas_call(
        body, name="gather_weights", in_specs=[ANY, ANY], out_specs=[ANY, ANY],
        out_shape=[SDS((4,) + big.shape, big.dtype), SDS((4,) + vec.shape, vec.dtype)],
        scratch_shapes=[dma((n,)), dma((n,)), dma((n,)), dma((n,)), dma((3,)), dma((3,))])(big, vec)


HALF_ROWS = 3328
CHUNK_ROWS = 256
N_CHUNKS = HALF_ROWS // CHUNK_ROWS


def _swap_halves(gbig, small):
    def body(g_ref, s_ref, ob_ref, os_ref, send_sems, recv_sems):
        x, y, c = _place()
        copies = []
        for s in range(4):
            for k in range(N_CHUNKS):
                rows = pl.ds(k * CHUNK_ROWS, CHUNK_ROWS)
                copies.append(pltpu.make_async_remote_copy(
                    src_ref=g_ref.at[s, 1 - c, rows], dst_ref=ob_ref.at[s, rows], send_sem=send_sems.at[len(copies)],
                    recv_sem=recv_sems.at[len(copies)], device_id=(x, y, 1 - c), device_id_type=MESH))
        copies.append(pltpu.make_async_remote_copy(
            src_ref=s_ref, dst_ref=os_ref, send_sem=send_sems.at[len(copies)], recv_sem=recv_sems.at[len(copies)],
            device_id=(x, y, 1 - c), device_id_type=MESH))
        for cp in copies:
            cp.start()
        for cp in copies:
            cp.wait_recv()
        for cp in copies:
            cp.wait_send()

    n = 4 * N_CHUNKS + 1
    return pl.pallas_call(
        body, name="swap_halves", in_specs=[ANY, ANY], out_specs=[ANY, ANY],
        out_shape=[SDS((4, HALF_ROWS, D), F32), SDS(small.shape, F32)],
        scratch_shapes=[pltpu.SemaphoreType.DMA((n,)), pltpu.SemaphoreType.DMA((n,))])(gbig, small)


def _pair_sum(gbig, other):
    def body(g_ref, o_ref, pb_ref, own_ref):
        x, y, c = _place()
        v = jnp.where(c == 0, g_ref[0], g_ref[1]) + o_ref[...]
        pb_ref[...] = v.astype(BF16)

        @pl.when(pl.program_id(1) == 2 * x + y)
        def _():
            own_ref[...] = v

    return pl.pallas_call(
        body, name="pair_sum", grid=(N_CHUNKS, 4),
        in_specs=[pl.BlockSpec((None, 2, CHUNK_ROWS, D), lambda i, s: (s, 0, i, 0)),
                  pl.BlockSpec((None, CHUNK_ROWS, D), lambda i, s: (s, i, 0))],
        out_specs=[pl.BlockSpec((None, CHUNK_ROWS, D), lambda i, s: (s, i, 0)),
                   pl.BlockSpec((CHUNK_ROWS, D), lambda i, s: (i, 0))],
        out_shape=[SDS((4, HALF_ROWS, D), BF16), SDS((HALF_ROWS, D), F32)],
        compiler_params=_params(2, VMEM_LIMIT))(gbig, other)


def _add(a, b, name):
    def body(a_ref, b_ref, o_ref):
        o_ref[...] = a_ref[...] + b_ref[...]

    return pl.pallas_call(body, name=name, out_shape=SDS(a.shape, a.dtype), compiler_params=_params(0, VMEM_LIMIT))(a, b)


def _chip_exchange(pb, ps):
    n_small = ps.shape[0]

    def body(pb_ref, ps_ref, lb_ref, ls_ref, send_sems, recv_sems):
        x, y, c = _place()
        chip = 2 * x + y
        peers = [(1 - x, y), (x, 1 - y), (1 - x, 1 - y)]

        def copies(slab_of):
            out = []
            for j, (px, py) in enumerate(peers):
                for k in range(N_CHUNKS):
                    rows = pl.ds(k * CHUNK_ROWS, CHUNK_ROWS)
                    out.append(pltpu.make_async_remote_copy(
                        src_ref=pb_ref.at[2 * px + py, rows], dst_ref=lb_ref.at[slab_of(j), rows],
                        send_sem=send_sems.at[len(out)], recv_sem=recv_sems.at[len(out)], device_id=(px, py, c),
                        device_id_type=MESH))
                out.append(pltpu.make_async_remote_copy(
                    src_ref=ps_ref, dst_ref=ls_ref.at[slab_of(j)], send_sem=send_sems.at[len(out)],
                    recv_sem=recv_sems.at[len(out)], device_id=(px, py, c), device_id_type=MESH))
            return out

        sends = copies(lambda j: chip)
        for cp in sends:
            cp.start()
        for cp in copies(lambda j: 2 * peers[j][0] + peers[j][1]):
            cp.wait_recv()
        for cp in sends:
            cp.wait_send()

    n = 3 * (N_CHUNKS + 1)
    return pl.pallas_call(
        body, name="chip_exchange", in_specs=[ANY, ANY], out_specs=[ANY, ANY],
        out_shape=[SDS((4, HALF_ROWS, D), BF16), SDS((4, n_small, D), F32)],
        scratch_shapes=[pltpu.SemaphoreType.DMA((n,)), pltpu.SemaphoreType.DMA((n,))])(pb, ps)


def _chip_sum(own, land):
    def body(own_ref, l_ref, o_ref):
        chip = 2 * lax.axis_index("x") + lax.axis_index("y")
        acc = jnp.where(chip == 0, own_ref[...], l_ref[0].astype(F32))
        for s in range(1, 4):
            acc = acc + jnp.where(chip == s, own_ref[...], l_ref[s].astype(F32))
        o_ref[...] = acc

    return pl.pallas_call(
        body, name="chip_sum", grid=(N_CHUNKS,),
        in_specs=[pl.BlockSpec((CHUNK_ROWS, D), lambda i: (i, 0)), pl.BlockSpec((4, CHUNK_ROWS, D), lambda i: (0, i, 0))],
        out_specs=pl.BlockSpec((CHUNK_ROWS, D), lambda i: (i, 0)), out_shape=SDS((HALF_ROWS, D), F32),
        compiler_params=_params(1, VMEM_LIMIT))(own, land)


def _sum4(own, land):
    def body(own_ref, l_ref, o_ref):
        chip = 2 * lax.axis_index("x") + lax.axis_index("y")
        acc = jnp.where(chip == 0, own_ref[...], l_ref[0])
        for s in range(1, 4):
            acc = acc + jnp.where(chip == s, own_ref[...], l_ref[s])
        o_ref[...] = acc

    return pl.pallas_call(body, name="sum_small", out_shape=SDS(own.shape, own.dtype),
                          compiler_params=_params(0, VMEM_LIMIT))(own, land)


def _share_half(r):
    def body(r_ref, o_ref, send_sems, recv_sems):
        x, y, c = _place()
        copies = [pltpu.make_async_remote_copy(
            src_ref=r_ref.at[pl.ds(k * CHUNK_ROWS, CHUNK_ROWS)], dst_ref=o_ref.at[pl.ds(k * CHUNK_ROWS, CHUNK_ROWS)],
            send_sem=send_sems.at[k], recv_sem=recv_sems.at[k], device_id=(x, y, 1 - c), device_id_type=MESH)
            for k in range(N_CHUNKS)]
        for cp in copies:
            cp.start()
        for cp in copies:
            cp.wait_recv()
        for cp in copies:
            cp.wait_send()

    return pl.pallas_call(
        body, name="share_half", in_specs=[ANY], out_specs=ANY, out_shape=SDS((HALF_ROWS, D), F32),
        scratch_shapes=[pltpu.SemaphoreType.DMA((N_CHUNKS,)), pltpu.SemaphoreType.DMA((N_CHUNKS,))])(r)


C_ROWS = 1312
_REDUCE_CHUNK = {512: 256, 2560: 640}
W_CLASSES = {"a0": (512, 256), "o": (768, 256), "kv": (128, 128)}


def _chunk_list(specs):
    return [(k, r, chunk) for k, (half, chunk) in enumerate(specs) for r in range(0, half, chunk)]


def _gather_classes(arrs, specs, vec):
    n = len(arrs)
    chunks = _chunk_list(specs)
    nc = len(chunks)

    def body(*refs):
        ins, vec_ref, outs, ov_ref = refs[:n], refs[n], refs[n + 1:2 * n + 1], refs[2 * n + 1]
        ici_send, ici_recv, d2d_send, d2d_recv, vec_send, vec_recv = refs[2 * n + 2:]
        x, y, c = _place()
        chip = 2 * x + y
        peers = [(1 - x, y), (x, 1 - y), (1 - x, 1 - y)]

        def rows(ci, half):
            k, r, cnt = chunks[ci]
            return k, pl.ds(half * specs[k][0] + r, cnt)

        def over_ici(j, ci, slab):
            px, py = peers[j]
            k, rs = rows(ci, c)
            return pltpu.make_async_remote_copy(
                src_ref=ins[k].at[rs], dst_ref=outs[k].at[slab, rs], send_sem=ici_send.at[j * nc + ci],
                recv_sem=ici_recv.at[j * nc + ci], device_id=(px, py, c), device_id_type=MESH)

        def over_d2d(j, ci, half):
            px, py = peers[j]
            k, rs = rows(ci, half)
            where = outs[k].at[2 * px + py, rs]
            return pltpu.make_async_remote_copy(
                src_ref=where, dst_ref=where, send_sem=d2d_send.at[j * nc + ci], recv_sem=d2d_recv.at[j * nc + ci],
                device_id=(x, y, 1 - c), device_id_type=MESH)

        def small(j, slab):
            px, py = peers[j]
            return pltpu.make_async_remote_copy(
                src_ref=vec_ref, dst_ref=ov_ref.at[slab], send_sem=vec_send.at[j], recv_sem=vec_recv.at[j],
                device_id=(px, py, c), device_id_type=MESH)

        sends = [small(j, chip) for j in range(3)] + [over_ici(j, ci, chip) for ci in range(nc) for j in range(3)]
        for cp in sends:
            cp.start()
        forwards = []
        for ci in range(nc):
            for j, (px, py) in enumerate(peers):
                over_ici(j, ci, 2 * px + py).wait_recv()
                forwards.append(over_d2d(j, ci, c))
                forwards[-1].start()
        for ci in range(nc):
            for j in range(3):
                over_d2d(j, ci, 1 - c).wait_recv()
        for j, (px, py) in enumerate(peers):
            small(j, 2 * px + py).wait_recv()
        for cp in sends + forwards:
            cp.wait_send()

    dma = pltpu.SemaphoreType.DMA
    return pl.pallas_call(
        body, name="gather_weights", in_specs=[ANY] * (n + 1), out_specs=[ANY] * (n + 1),
        out_shape=[SDS((4,) + a.shape, a.dtype) for a in arrs] + [SDS((4,) + vec.shape, vec.dtype)],
        scratch_shapes=[dma((3 * nc,)), dma((3 * nc,)), dma((3 * nc,)), dma((3 * nc,)), dma((3,)), dma((3,))])(*arrs, vec)


def _swap_classes(grads, specs, small):
    n = len(grads)
    chunks = _chunk_list(specs)

    def body(*refs):
        ins, s_ref, outs, os_ref, send_sems, recv_sems = refs[:n], refs[n], refs[n + 1:2 * n + 1], *refs[2 * n + 1:]
        x, y, c = _place()
        copies = []
        for s in range(4):
            for k, r, cnt in chunks:
                copies.append(pltpu.make_async_remote_copy(
                    src_ref=ins[k].at[s, 1 - c, pl.ds(r, cnt)], dst_ref=outs[k].at[s, pl.ds(r, cnt)],
                    send_sem=send_sems.at[len(copies)], recv_sem=recv_sems.at[len(copies)],
                    device_id=(x, y, 1 - c), device_id_type=MESH))
        copies.append(pltpu.make_async_remote_copy(
            src_ref=s_ref, dst_ref=os_ref, send_sem=send_sems.at[len(copies)], recv_sem=recv_sems.at[len(copies)],
            device_id=(x, y, 1 - c), device_id_type=MESH))
        for cp in copies:
            cp.start()
        for cp in copies:
            cp.wait_recv()
        for cp in copies:
            cp.wait_send()

    m = 4 * len(chunks) + 1
    return pl.pallas_call(
        body, name="swap_halves", in_specs=[ANY] * (n + 1), out_specs=[ANY] * (n + 1),
        out_shape=[SDS((4, g.shape[2], g.shape[3]), F32) for g in grads] + [SDS(small.shape, F32)],
        scratch_shapes=[pltpu.SemaphoreType.DMA((m,)), pltpu.SemaphoreType.DMA((m,))])(*grads, small)


def _pair_sum_class(g, other, chunk, name):
    _, _, half, w = g.shape

    def body(g_ref, o_ref, pb_ref, own_ref):
        x, y, c = _place()
        v = jnp.where(c == 0, g_ref[0], g_ref[1]) + o_ref[...]
        pb_ref[...] = v.astype(BF16)

        @pl.when(pl.program_id(1) == 2 * x + y)
        def _():
            own_ref[...] = v

    return pl.pallas_call(
        body, name=name, grid=(half // chunk, 4),
        in_specs=[pl.BlockSpec((None, 2, chunk, w), lambda i, s: (s, 0, i, 0)),
                  pl.BlockSpec((None, chunk, w), lambda i, s: (s, i, 0))],
        out_specs=[pl.BlockSpec((None, chunk, w), lambda i, s: (s, i, 0)), pl.BlockSpec((chunk, w), lambda i, s: (i, 0))],
        out_shape=[SDS((4, half, w), BF16), SDS((half, w), F32)],
        compiler_params=_params(2, VMEM_LIMIT))(g, other)


def _exchange_classes(pbs, specs, ps):
    n = len(pbs)
    chunks = _chunk_list(specs)
    per_peer = len(chunks) + 1

    def body(*refs):
        ins, ps_ref, outs, ls_ref, send_sems, recv_sems = refs[:n], refs[n], refs[n + 1:2 * n + 1], *refs[2 * n + 1:]
        x, y, c = _place()
        chip = 2 * x + y
        peers = [(1 - x, y), (x, 1 - y), (1 - x, 1 - y)]

        def copies(slab_of):
            out = []
            for j, (px, py) in enumerate(peers):
                for k, r, cnt in chunks:
                    out.append(pltpu.make_async_remote_copy(
                        src_ref=ins[k].at[2 * px + py, pl.ds(r, cnt)], dst_ref=outs[k].at[slab_of(j), pl.ds(r, cnt)],
                        send_sem=send_sems.at[len(out)], recv_sem=recv_sems.at[len(out)], device_id=(px, py, c),
                        device_id_type=MESH))
                out.append(pltpu.make_async_remote_copy(
                    src_ref=ps_ref, dst_ref=ls_ref.at[slab_of(j)], send_sem=send_sems.at[len(out)],
                    recv_sem=recv_sems.at[len(out)], device_id=(px, py, c), device_id_type=MESH))
            return out

        sends = copies(lambda j: chip)
        for cp in sends:
            cp.start()
        for cp in copies(lambda j: 2 * peers[j][0] + peers[j][1]):
            cp.wait_recv()
        for cp in sends:
            cp.wait_send()

    m = 3 * per_peer
    return pl.pallas_call(
        body, name="chip_exchange", in_specs=[ANY] * (n + 1), out_specs=[ANY] * (n + 1),
        out_shape=[SDS(p.shape, BF16) for p in pbs] + [SDS((4,) + ps.shape, F32)],
        scratch_shapes=[pltpu.SemaphoreType.DMA((m,)), pltpu.SemaphoreType.DMA((m,))])(*pbs, ps)


def _chip_sum_class(own, land, chunk, name):
    half, w = own.shape

    def body(own_ref, l_ref, o_ref):
        chip = 2 * lax.axis_index("x") + lax.axis_index("y")
        acc = jnp.where(chip == 0, own_ref[...], l_ref[0].astype(F32))
        for s in range(1, 4):
            acc = acc + jnp.where(chip == s, own_ref[...], l_ref[s].astype(F32))
        o_ref[...] = acc

    return pl.pallas_call(
        body, name=name, grid=(half // chunk,),
        in_specs=[pl.BlockSpec((chunk, w), lambda i: (i, 0)), pl.BlockSpec((4, chunk, w), lambda i: (0, i, 0))],
        out_specs=pl.BlockSpec((chunk, w), lambda i: (i, 0)), out_shape=SDS((half, w), F32),
        compiler_params=_params(1, VMEM_LIMIT))(own, land)


def _share_classes(rs, specs):
    n = len(rs)
    chunks = _chunk_list(specs)

    def body(*refs):
        ins, outs, send_sems, recv_sems = refs[:n], refs[n:2 * n], *refs[2 * n:]
        x, y, c = _place()
        copies = [pltpu.make_async_remote_copy(
            src_ref=ins[k].at[pl.ds(r, cnt)], dst_ref=outs[k].at[pl.ds(r, cnt)], send_sem=send_sems.at[i],
            recv_sem=recv_sems.at[i], device_id=(x, y, 1 - c), device_id_type=MESH)
            for i, (k, r, cnt) in enumerate(chunks)]
        for cp in copies:
            cp.start()
        for cp in copies:
            cp.wait_recv()
        for cp in copies:
            cp.wait_send()

    m = len(chunks)
    return pl.pallas_call(
        body, name="share_half", in_specs=[ANY] * n, out_specs=[ANY] * n, out_shape=[SDS(r.shape, F32) for r in rs],
        scratch_shapes=[pltpu.SemaphoreType.DMA((m,)), pltpu.SemaphoreType.DMA((m,))])(*rs)


_HBM = pl.BlockSpec(memory_space=pltpu.HBM)
_SEM = pl.BlockSpec(memory_space=pltpu.SEMAPHORE)
_EFFECT = pltpu.SideEffectType.DATAFLOW_SIDE_EFFECTING


def _chip_peers():
    x, y, c = _place()
    return [(1 - x, y, c), (x, 1 - y, c), (1 - x, 1 - y, c)]


def _send_shard_start(v, name):
    def body(v_ref, land_ref, send_sems, recv_sems, v_thru, land_thru, token):
        x, y, c = _place()
        for j, peer in enumerate(_chip_peers()):
            pltpu.make_async_remote_copy(src_ref=v_ref, dst_ref=land_ref.at[2 * x + y], send_sem=send_sems.at[j],
                                         recv_sem=recv_sems.at[j], device_id=peer, device_id_type=MESH).start()
        token[...] = jnp.zeros_like(token)

    land_shape = (4,) + v.shape
    return pl.pallas_call(
        body, name=name,
        out_shape=(pltpu.SemaphoreType.DMA((3,)), pltpu.SemaphoreType.DMA((3,)), pltpu.HBM(v.shape, v.dtype),
                   pltpu.HBM(land_shape, v.dtype), SDS((8, 128), F32)),
        in_specs=(_HBM, _HBM), out_specs=(_SEM, _SEM, _HBM, _HBM, pl.BlockSpec(memory_space=pltpu.VMEM)),
        input_output_aliases={0: 2, 1: 3}, compiler_params=pltpu.CompilerParams(has_side_effects=_EFFECT),
    )(pltpu.with_memory_space_constraint(v, pltpu.HBM),
      pltpu.with_memory_space_constraint(lax.empty(land_shape, v.dtype), pltpu.HBM))


def _xor_peer(r):
    x, y, c = _place()
    return (1 - x if (r >> 2) & 1 else x, 1 - y if (r >> 1) & 1 else y, 1 - c if r & 1 else c)


def _send_pieces_start(parts, name):
    n = len(parts)

    def body(*refs):
        ins, lands = refs[:n], refs[n:2 * n]
        send_sems, recv_sems = refs[2 * n:2 * n + 2]
        token = refs[-1]
        x, y, c = _place()
        for r in range(1, 8):
            px, py, pc = _xor_peer(r)
            for k in range(n):
                pltpu.make_async_remote_copy(
                    src_ref=ins[k].at[2 * px + py, pc], dst_ref=lands[k].at[4 * x + 2 * y + c],
                    send_sem=send_sems.at[(r - 1) * n + k], recv_sem=recv_sems.at[(r - 1) * n + k],
                    device_id=(px, py, pc), device_id_type=MESH).start()
        token[...] = jnp.zeros_like(token)

    land_shapes = [(8,) + p.shape[2:] for p in parts]
    hbm = [pltpu.HBM(p.shape, p.dtype) for p in parts] + [pltpu.HBM(s, p.dtype) for s, p in zip(land_shapes, parts)]
    operands = [pltpu.with_memory_space_constraint(p, pltpu.HBM) for p in parts]
    operands += [pltpu.with_memory_space_constraint(lax.empty(s, p.dtype), pltpu.HBM) for s, p in zip(land_shapes, parts)]
    return pl.pallas_call(
        body, name=name,
        out_shape=(pltpu.SemaphoreType.DMA((7 * n,)), pltpu.SemaphoreType.DMA((7 * n,)), *hbm, SDS((8, 128), F32)),
        in_specs=(_HBM,) * (2 * n), out_specs=(_SEM, _SEM) + (_HBM,) * (2 * n) + (pl.BlockSpec(memory_space=pltpu.VMEM),),
        input_output_aliases={i: 2 + i for i in range(2 * n)},
        compiler_params=pltpu.CompilerParams(has_side_effects=_EFFECT))(*operands)


def _send_pieces_wait(started, after, name):
    send_sems, recv_sems, *thru, _ = started
    n = len(thru) // 2

    def body(*refs):
        ins, lands = refs[:n], refs[n:2 * n]
        send_sems, recv_sems = refs[2 * n:2 * n + 2]
        for r in range(1, 8):
            px, py, pc = _xor_peer(r)
            for k in range(n):
                copy = pltpu.make_async_remote_copy(
                    src_ref=ins[k].at[2 * px + py, pc], dst_ref=lands[k].at[4 * px + 2 * py + pc],
                    send_sem=send_sems.at[(r - 1) * n + k], recv_sem=recv_sems.at[(r - 1) * n + k],
                    device_id=(px, py, pc), device_id_type=MESH)
                copy.wait_send()
                copy.wait_recv()

    return pl.pallas_call(
        body, name=name, out_shape=tuple(pltpu.HBM(t.shape, t.dtype) for t in thru),
        in_specs=(_HBM,) * (2 * n) + (_SEM, _SEM, pl.BlockSpec(memory_space=pl.ANY)), out_specs=(_HBM,) * (2 * n),
        input_output_aliases={i: i for i in range(2 * n)},
        compiler_params=pltpu.CompilerParams(has_side_effects=_EFFECT))(*thru, send_sems, recv_sems, after)[n:]


def _sum8_class(own, land, chunk, name):
    rows, w = own.shape

    def body(own_ref, l_ref, o_ref):
        x, y, c = _place()
        me = 4 * x + 2 * y + c
        acc = jnp.where(me == 0, own_ref[...], l_ref[0].astype(F32))
        for d in range(1, 8):
            acc = acc + jnp.where(me == d, own_ref[...], l_ref[d].astype(F32))
        o_ref[...] = acc

    return pl.pallas_call(
        body, name=name, grid=(rows // chunk,),
        in_specs=[pl.BlockSpec((chunk, w), lambda i: (i, 0)), pl.BlockSpec((8, chunk, w), lambda i: (0, i, 0))],
        out_specs=pl.BlockSpec((chunk, w), lambda i: (i, 0)), out_shape=SDS((rows, w), F32),
        compiler_params=_params(1, VMEM_LIMIT))(own, land)


def _send_shard_wait(send_sems, recv_sems, v_thru, land_thru, after, name):
    def body(v_ref, land_ref, send_sems, recv_sems, after_ref, v_dead, got_ref):
        for j, (px, py, pc) in enumerate(_chip_peers()):
            copy = pltpu.make_async_remote_copy(src_ref=v_ref, dst_ref=land_ref.at[2 * px + py], send_sem=send_sems.at[j],
                                                recv_sem=recv_sems.at[j], device_id=(px, py, pc), device_id_type=MESH)
            copy.wait_send()
            copy.wait_recv()

    return pl.pallas_call(
        body, name=name,
        out_shape=(pltpu.HBM(v_thru.shape, v_thru.dtype), pltpu.HBM(land_thru.shape, land_thru.dtype)),
        in_specs=(_HBM, _HBM, _SEM, _SEM, pl.BlockSpec(memory_space=pl.ANY)), out_specs=(_HBM, _HBM),
        input_output_aliases={0: 0, 1: 1}, compiler_params=pltpu.CompilerParams(has_side_effects=_EFFECT),
    )(v_thru, land_thru, send_sems, recv_sems, after)[1]


_SMALL = ["mem_norm_w", "norm_pre", "norm_post", "a_ln_w", "a_ln_b", "a_w_s", "a_b_s", "b_conv_w", "c_conv_w",
          "c_a_log", "c_dt_bias", "c_o_norm_w"]
_SMALL_SHAPES = {"mem_norm_w": (D,), "norm_pre": (4, D), "norm_post": (4, D), "a_ln_w": (2, D), "a_ln_b": (2, D),
                 "a_w_s": (2, 8, HD, HD), "a_b_s": (2, 8, HD), "b_conv_w": (1, 3, D), "c_conv_w": (1, 4, 3 * D),
                 "c_a_log": (1, 8), "c_dt_bias": (1, 8), "c_o_norm_w": (1, HD)}
_SHARDED_SMALL = {"a_ln_w": D // 4, "a_ln_b": D // 4, "b_conv_w": D // 4, "c_conv_w": 3 * D // 4}
_ROWS = [2048, 1280, 1284, 1536, 256]
_BIG_ROWS = sum(_ROWS)
_BIG_PAD = 6416
_SMALL_ROWS = 288


def _size(shape):
    n = 1
    for d in shape:
        n *= d
    return n


def kernel(x, mem, mem_norm_w, w_mem_kv, norm_pre, norm_post, w_out, a_w_in, a_ln_w, a_ln_b, a_w_s, a_b_s, b_w_in, b_conv_w, c_w_in, c_conv_w, c_a_log, c_dt_bias, c_o_norm_w, loss_target, m_mem_norm_w, m_w_mem_kv, m_norm_pre, m_norm_post, m_w_out, m_a_w_in, m_a_ln_w, m_a_ln_b, m_a_w_s, m_a_b_s, m_b_w_in, m_b_conv_w, m_c_w_in, m_c_conv_w, m_c_a_log, m_c_dt_bias, m_c_o_norm_w, v_mem_norm_w, v_w_mem_kv, v_norm_pre, v_norm_post, v_w_out, v_a_w_in, v_a_ln_w, v_a_ln_b, v_a_w_s, v_a_b_s, v_b_w_in, v_b_conv_w, v_c_w_in, v_c_conv_w, v_c_a_log, v_c_dt_bias, v_c_o_norm_w):
    names = ["mem_norm_w", "w_mem_kv", "norm_pre", "norm_post", "w_out", "a_w_in", "a_ln_w", "a_ln_b", "a_w_s", "a_b_s",
             "b_w_in", "b_conv_w", "c_w_in", "c_conv_w", "c_a_log", "c_dt_bias", "c_o_norm_w"]
    w = dict(zip(names, [mem_norm_w, w_mem_kv, norm_pre, norm_post, w_out, a_w_in, a_ln_w, a_ln_b, a_w_s, a_b_s, b_w_in,
                         b_conv_w, c_w_in, c_conv_w, c_a_log, c_dt_bias, c_o_norm_w]))
    m = dict(zip(names, [m_mem_norm_w, m_w_mem_kv, m_norm_pre, m_norm_post, m_w_out, m_a_w_in, m_a_ln_w, m_a_ln_b, m_a_w_s,
                         m_a_b_s, m_b_w_in, m_b_conv_w, m_c_w_in, m_c_conv_w, m_c_a_log, m_c_dt_bias, m_c_o_norm_w]))
    v = dict(zip(names, [v_mem_norm_w, v_w_mem_kv, v_norm_pre, v_norm_post, v_w_out, v_a_w_in, v_a_ln_w, v_a_ln_b, v_a_w_s,
                         v_a_b_s, v_b_w_in, v_b_conv_w, v_c_w_in, v_c_conv_w, v_c_a_log, v_c_dt_bias, v_c_o_norm_w]))
    chip = 2 * lax.axis_index("x") + lax.axis_index("y")

    def rows_of_ct(a):
        return a[0].T

    def with_mine(gathered, own):
        return lax.dynamic_update_slice(gathered, own[None], (chip,) + (0,) * own.ndim)

    first = [a_w_in[0].astype(BF16), w_out.reshape(D_CAT, D).astype(BF16), w_mem_kv.astype(BF16)]
    vec = jnp.concatenate([a_ln_w.reshape(-1), a_ln_b.reshape(-1), b_conv_w.reshape(-1), c_conv_w.reshape(-1)])
    vec = jnp.pad(vec, (0, 8 * D - vec.shape[0])).reshape(8, D)
    *gathered, gvec = _gather_classes(first, [W_CLASSES[k] for k in W_CLASSES], vec)
    ga0, go, gkv = [with_mine(g_, a) for g_, a in zip(gathered, first)]
    gvec = with_mine(gvec, vec)
    go = go.reshape(4, 4, 384, D)
    gv = gvec.reshape(4, 8 * D)
    later = {1: b_w_in[0], 2: jnp.pad(rows_of_ct(c_w_in), ((0, C_ROWS - 1284), (0, 0))), 3: a_w_in[1]}
    later = {i: (a + 0.0 * gkv[0, 0, 0].astype(F32)).astype(BF16) for i, a in later.items()}
    sent = {i: _send_shard_start(a, f"send_w_in_{i}") for i, a in later.items()}
    started = sum(s[4][0, 0] for s in sent.values())

    def blocks(i, after):
        if i == 0:
            return [ga0[0], ga0[1]], [ga0[2], ga0[3]]
        got = with_mine(_send_shard_wait(*sent[i][:4], after, f"wait_w_in_{i}"), later[i])
        if i == 1:
            return [got[0], got[1], got[2][:, :512]], [got[2][:, 512:], got[3]]
        if i == 3:
            return [got[0], got[1]], [got[2], got[3]]
        fct = got[:, :1284].reshape(5136, D)
        c_ab = jnp.concatenate([fct[3 * D:3 * D + 16], jnp.zeros((AB_PAD - 16, D), BF16)], axis=0)
        return [fct[:3 * D], c_ab], [fct[3 * D + 16:]]
    sm = {"mem_norm_w": mem_norm_w, "norm_pre": norm_pre + started, "norm_post": norm_post, "a_w_s": a_w_s, "a_b_s": a_b_s,
          "c_a_log": c_a_log, "c_dt_bias": c_dt_bias, "c_o_norm_w": c_o_norm_w,
          "a_ln_w": gv[:, 0:512].reshape(4, 2, 256).transpose(1, 0, 2).reshape(2, D),
          "a_ln_b": gv[:, 512:1024].reshape(4, 2, 256).transpose(1, 0, 2).reshape(2, D),
          "b_conv_w": gv[:, 1024:1792].reshape(4, 1, 3, 256).transpose(1, 2, 0, 3).reshape(1, 3, D),
          "c_conv_w": gv[:, 1792:4864].reshape(4, 1, 4, 768).transpose(1, 2, 0, 3).reshape(1, 4, 3 * D)}
    wts = {"wkv": gkv.reshape(D, 2 * D_XA), "wo": [go[:, i].reshape(D_CAT, D) for i in range(DEPTH)], "blocks": blocks}

    def layer_grads(i, g):
        if i % 3 == 2:
            gct = jnp.concatenate([g["wm"][i][:3 * D + 16], g["wg"][i]], axis=0).reshape(4, 1284, D)
            w_in = jnp.pad(gct, ((0, 0), (0, C_ROWS - 1284), (0, 0)))
        else:
            w_in = jnp.concatenate([g["wm"][i], g["wg"][i]], axis=0).reshape(4, -1, GRAD_TILE[i % 3])
        out = {f"in{i}": w_in, f"out{i}": g["wo"][i].reshape(4, 384, D)}
        return {k: a.reshape(4, 2, a.shape[1] // 2, a.shape[2]) for k, a in out.items()}

    pending = {}

    def layer_done(i, g):
        if i == 0:
            return 0.0
        halves = layer_grads(i, g)
        started = _send_pieces_start([h.astype(BF16) for h in halves.values()], f"send_grads_{i}")
        pending[i] = (started, halves)
        return started[-1][0, 0]

    wts["layer_done"] = layer_done

    loss, dx, g = _local_step(x[0], mem[0], loss_target[0], wts, sm)
    loss = lax.psum(loss, ("x", "y", "c"))

    core = lax.axis_index("c")
    mine, specs = {}, {}
    for i in (3, 2, 1):
        started, halves = pending[i]
        lands = _send_pieces_wait(started, dx, f"wait_grads_{i}")
        for (k, h), land in zip(halves.items(), lands):
            own = lax.dynamic_index_in_dim(lax.dynamic_index_in_dim(h, chip, 0, False), core, 0, False)
            specs[k] = (own.shape[0], _REDUCE_CHUNK.get(own.shape[0], own.shape[0]))
            mine[k] = _sum8_class(own, land, specs[k][1], f"sum8_{k}")

    first = layer_grads(0, g)
    first["kv"] = g["wkv"].reshape(4, 2, 128, D)
    first_specs = [(h.shape[2], _REDUCE_CHUNK.get(h.shape[2], h.shape[2])) for h in first.values()]
    halves = list(first.values())
    gs = {"mem_norm_w": g["mem_norm_w"], "norm_pre": jnp.concatenate(g["norm_pre"]),
          "norm_post": jnp.concatenate(g["norm_post"])}
    for n in _SMALL[3:]:
        gs[n] = jnp.stack([g[n][j] for j in sorted(g[n])])
    flat = jnp.concatenate([gs[n].reshape(-1) for n in _SMALL])
    small = jnp.pad(flat, (0, _SMALL_ROWS * D - flat.shape[0])).reshape(_SMALL_ROWS, D)
    *others, other_small = _swap_classes(halves, first_specs, small)
    pairs = [_pair_sum_class(h, o, s[1], f"pair_sum_{k}") for k, h, o, s in zip(first, halves, others, first_specs)]
    pair_small = _add(small, other_small, "pair_sum_small")
    *lands, land_small = _exchange_classes([p[0] for p in pairs], first_specs, pair_small)
    for k, p, land, s in zip(first, pairs, lands, first_specs):
        mine[k], specs[k] = _chip_sum_class(p[1], land, s[1], f"chip_sum_{k}"), s
    theirs = _share_classes(list(mine.values()), [specs[k] for k in mine])
    south = core == 0
    sh = {k: jnp.concatenate([jnp.where(south, a, b), jnp.where(south, b, a)], axis=0)
          for (k, a), b in zip(mine.items(), theirs)}
    grads = {"a_w_in": jnp.stack([sh["in0"], sh["in3"]]),
             "b_w_in": sh["in1"].reshape(5, D, 256).transpose(1, 0, 2).reshape(b_w_in.shape),
             "c_w_in": sh["in2"][:1284], "w_out": jnp.stack([sh[f"out{i}"] for i in range(DEPTH)]),
             "w_mem_kv": sh["kv"]}
    flat = _sum4(pair_small, land_small).reshape(-1)
    off = 0
    for n in _SMALL:
        shape = _SMALL_SHAPES[n]
        full = flat[off:off + _size(shape)].reshape(shape)
        off += _size(shape)
        if n in _SHARDED_SMALL:
            full = lax.dynamic_slice_in_dim(full, chip * _SHARDED_SMALL[n], _SHARDED_SMALL[n], axis=len(shape) - 1)
        grads[n] = full

    delta, new_m, new_v = {}, {}, {}
    for n in names:
        shape = w[n].shape
        if n == "c_w_in":
            d_, m_, v_ = _adamw(rows_of_ct(w[n]), grads[n], rows_of_ct(m[n]), rows_of_ct(v[n]), f"adamw_{n}")
            delta[n], new_m[n], new_v[n], grads[n] = d_.T[None], m_.T[None], v_.T[None], grads[n].T[None]
            continue
        view = (1, shape[0]) if len(shape) == 1 else (_size(shape[:-1]), shape[-1])
        d_, m_, v_ = _adamw(w[n].reshape(view), grads[n].reshape(view), m[n].reshape(view), v[n].reshape(view),
                            f"adamw_{n}")
        delta[n], new_m[n], new_v[n] = d_.reshape(shape), m_.reshape(shape), v_.reshape(shape)
    return (loss, dx[None], *[grads[n].reshape(w[n].shape) for n in names], *[delta[n] for n in names],
            *[new_m[n] for n in names], *[new_v[n] for n in names])
```

```python
import functools

import jax
import jax.numpy as jnp
from jax import lax
from jax.experimental import pallas as pl
from jax.experimental.pallas import tpu as pltpu

F32 = jnp.float32
BF16 = jnp.bfloat16
HI = lax.Precision.HIGHEST
MESH = pl.DeviceIdType.MESH
SDS = jax.ShapeDtypeStruct

D = 1024
D_XA = 512
D_CAT = 1536
N_MEM = 256
HD = 128
DEPTH = 4
EPS = 1e-6
TT = 512
DN_C = 64
DN_TB = 256
HALO = 8
AB_PAD = 128
VMEM_LIMIT = 56 * 1024 * 1024
GRAD_TILE = {0: 1024, 1: 256}

ADAM_LR, ADAM_B1, ADAM_B2, ADAM_EPS, ADAM_WD, ADAM_STEP = 0.001, 0.9, 0.999, 1e-08, 0.01, 10


def _params(n_grid, vmem=None):
    return pltpu.CompilerParams(dimension_semantics=("arbitrary",) * n_grid, vmem_limit_bytes=vmem)


def _rms(x, w):
    return x * lax.rsqrt(jnp.mean(x * x, axis=-1, keepdims=True) + EPS) * w


def _dot_nn(a, b):
    return jnp.dot(a.astype(BF16), b.astype(BF16), preferred_element_type=F32)


def _dot_nt(a, b):
    return lax.dot_general(a.astype(BF16), b.astype(BF16), (((1,), (1,)), ((), ())), preferred_element_type=F32)


def _dot_tn(a, b):
    return lax.dot_general(a.astype(BF16), b.astype(BF16), (((0,), (0,)), ((), ())), preferred_element_type=F32)


@jax.custom_vjp
def mm(a, b):
    return _dot_nn(a, b)


mm.defvjp(lambda a, b: (_dot_nn(a, b), (a, b)), lambda r, g: (_dot_nt(g, r[1]), _dot_tn(r[0], g)))


@jax.custom_vjp
def mm_nt(a, b):
    return _dot_nt(a, b)


mm_nt.defvjp(lambda a, b: (_dot_nt(a, b), (a, b)), lambda r, g: (_dot_nn(g, r[1]), _dot_tn(g, r[0])))


@jax.custom_vjp
def mm_tn(a, b):
    return _dot_tn(a, b)


mm_tn.defvjp(lambda a, b: (_dot_tn(a, b), (a, b)), lambda r, g: (_dot_nt(r[1], g), _dot_nn(r[0], g)))


def _dot_hi(a, b):
    return jnp.dot(a, b, precision=HI, preferred_element_type=F32)


def _row_spec(width, tile=TT):
    return pl.BlockSpec((tile, width), lambda i: (i, 0))


def _full_spec(shape):
    return pl.BlockSpec(shape, lambda *_: (0,) * len(shape))


def _widths(blocks, transposed):
    return [b.shape[0 if transposed else 1] for b in blocks]


def _inproj_fwd(x, nw, mix, gate, transposed, name):
    T, nm = x.shape[0], len(mix)
    M, G = sum(_widths(mix, transposed)), sum(_widths(gate, transposed))

    def body(x_ref, nw_ref, *refs):
        blocks, (pm_ref, pg_ref, h_ref) = refs[:-3], refs[-3:]
        h = _rms(x_ref[...], nw_ref[...]).astype(BF16)
        h_ref[...] = h
        for p_ref, group in ((pm_ref, blocks[:nm]), (pg_ref, blocks[nm:])):
            off = 0
            for w_ref in group:
                w = w_ref.shape[0 if transposed else 1]
                p_ref[:, off:off + w] = _dot_nt(h, w_ref[...]) if transposed else _dot_nn(h, w_ref[...])
                off += w

    return pl.pallas_call(
        body, name=name, grid=(T // TT,),
        in_specs=[_row_spec(D), _full_spec((1, D))] + [_full_spec(b.shape) for b in mix + gate],
        out_specs=[_row_spec(M), _row_spec(G), _row_spec(D)],
        out_shape=[SDS((T, M), F32), SDS((T, G), F32), SDS((T, D), BF16)],
        compiler_params=_params(1, VMEM_LIMIT))(x, nw, *mix, *gate)


def _inproj_bwd(dpm, dpg, x, nw, mix, gate, transposed, dxc, name):
    T, nm = x.shape[0], len(mix)
    M, G = sum(_widths(mix, transposed)), sum(_widths(gate, transposed))

    def body(dpm_ref, dpg_ref, x_ref, nw_ref, *refs):
        blocks, (dxc_ref, dx_ref, dnw_ref) = refs[:-3], refs[-3:]
        dh = None
        for dp_ref, group in ((dpm_ref, blocks[:nm]), (dpg_ref, blocks[nm:])):
            off = 0
            for w_ref in group:
                w = w_ref.shape[0 if transposed else 1]
                dp = dp_ref[:, off:off + w]
                part = _dot_nn(dp, w_ref[...]) if transposed else _dot_nt(dp, w_ref[...])
                dh = part if dh is None else dh + part
                off += w
        _, vjp = jax.vjp(_rms, x_ref[...], nw_ref[...])
        dxr, dnw = vjp(dh)
        dx_ref[...] = dxc_ref[...] + dxr

        @pl.when(pl.program_id(0) == 0)
        def _():
            dnw_ref[...] = jnp.zeros_like(dnw_ref)
        dnw_ref[...] += dnw

    return pl.pallas_call(
        body, name=name, grid=(T // TT,),
        in_specs=[_row_spec(M), _row_spec(G), _row_spec(D), _full_spec((1, D))]
        + [_full_spec(b.shape) for b in mix + gate] + [_row_spec(D)],
        out_specs=[_row_spec(D), _full_spec((1, D))],
        out_shape=[SDS((T, D), F32), SDS((1, D), F32)],
        compiler_params=_params(1, VMEM_LIMIT))(dpm, dpg, x, nw, *mix, *gate, dxc)


def _matmul_tn(a, b, name, sub=None):
    T, K = a.shape
    N = b.shape[1]
    tn = 1024 if N % 1024 == 0 else (640 if N % 640 == 0 else N)
    tt = min(1024, T)
    n_sub = 1 if sub is None else tn // sub

    def body(a_ref, b_ref, o_ref):
        @pl.when(pl.program_id(1) == 0)
        def _():
            o_ref[...] = jnp.zeros_like(o_ref)
        res = _dot_tn(a_ref[...], b_ref[...])
        if sub is None:
            o_ref[...] += res
        else:
            for i in range(n_sub):
                o_ref[i] += res[:, i * sub:(i + 1) * sub]

    if sub is None:
        out_spec, out_shape = pl.BlockSpec((K, tn), lambda j, t: (0, j)), SDS((K, N), F32)
    else:
        out_spec, out_shape = pl.BlockSpec((n_sub, K, sub), lambda j, t: (j, 0, 0)), SDS((N // sub, K, sub), F32)
    return pl.pallas_call(
        body, name=name, grid=(N // tn, T // tt),
        in_specs=[pl.BlockSpec((tt, K), lambda j, t: (t, 0)), pl.BlockSpec((tt, tn), lambda j, t: (t, j))],
        out_specs=out_spec, out_shape=out_shape,
        compiler_params=_params(2, VMEM_LIMIT))(a, b)


def _memkv_fn(mem, w, wkv):
    return mm(_rms(mem, w), wkv)


def _memkv_fwd(mem, w, wkv):
    def body(mem_ref, w_ref, wkv_ref, kv_ref):
        kv_ref[...] = _memkv_fn(mem_ref[...], w_ref[...], wkv_ref[...])

    return pl.pallas_call(body, name="memkv_fwd", out_shape=SDS((N_MEM, 2 * D_XA), F32),
                          compiler_params=_params(0, VMEM_LIMIT))(mem, w, wkv)


def _memkv_bwd(mem, w, wkv, dkv):
    def body(mem_ref, w_ref, wkv_ref, dkv_ref, dw_ref, dwkv_ref):
        _, vjp = jax.vjp(functools.partial(_memkv_fn, mem_ref[...]), w_ref[...], wkv_ref[...].astype(F32))
        dw, dwkv = vjp(dkv_ref[...])
        dw_ref[...] = dw
        dwkv_ref[...] = dwkv

    return pl.pallas_call(body, name="memkv_bwd", out_shape=[SDS((1, D), F32), SDS((D, 2 * D_XA), F32)],
                          compiler_params=_params(0, VMEM_LIMIT))(mem, w, wkv, dkv)


def _attn_gate(ymix, qx, z, *kvs):
    outs = []
    for j in range(4):
        s = mm_nt(qx[:, j * HD:(j + 1) * HD], kvs[j]) * (HD ** -0.5)
        e = jnp.exp(s - lax.stop_gradient(jnp.max(s, axis=-1, keepdims=True)))
        outs.append(mm(e / jnp.sum(e, axis=-1, keepdims=True), kvs[4 + j]))
    return jnp.concatenate([ymix] + outs, axis=1) * jax.nn.silu(z)


def _kv_blocks(kv_ref):
    return [kv_ref[:, j * HD:(j + 1) * HD] for j in range(8)]


def _ag_fwd(ymix, pg, kv, name):
    T = ymix.shape[0]

    def body(ymix_ref, pg_ref, kv_ref, ycat_ref):
        ycat_ref[...] = _attn_gate(ymix_ref[...], pg_ref[:, :D_XA], pg_ref[:, D_XA:], *_kv_blocks(kv_ref)).astype(BF16)

    return pl.pallas_call(
        body, name=name, grid=(T // TT,),
        in_specs=[_row_spec(D), _row_spec(D_XA + D_CAT), _full_spec((N_MEM, 2 * D_XA))],
        out_specs=_row_spec(D_CAT), out_shape=SDS((T, D_CAT), BF16),
        compiler_params=_params(1, VMEM_LIMIT))(ymix, pg, kv)


def _ag_bwd(dycat, ymix, pg, kv, dkv_in, name):
    T = ymix.shape[0]

    def body(dycat_ref, ymix_ref, pg_ref, kv_ref, dkvin_ref, dymix_ref, dpg_ref, dkv_ref):
        _, vjp = jax.vjp(_attn_gate, ymix_ref[...], pg_ref[:, :D_XA], pg_ref[:, D_XA:], *_kv_blocks(kv_ref))
        g = vjp(dycat_ref[...])
        dymix_ref[...] = g[0]
        dpg_ref[:, :D_XA] = g[1].astype(BF16)
        dpg_ref[:, D_XA:] = g[2].astype(BF16)

        @pl.when(pl.program_id(0) == 0)
        def _():
            dkv_ref[...] = dkvin_ref[...]
        for j in range(8):
            dkv_ref[:, j * HD:(j + 1) * HD] += g[3 + j]

    return pl.pallas_call(
        body, name=name, grid=(T // TT,),
        in_specs=[_row_spec(D_CAT), _row_spec(D), _row_spec(D_XA + D_CAT), _full_spec((N_MEM, 2 * D_XA)),
                  _full_spec((N_MEM, 2 * D_XA))],
        out_specs=[_row_spec(D), _row_spec(D_XA + D_CAT), _full_spec((N_MEM, 2 * D_XA))],
        out_shape=[SDS((T, D), F32), SDS((T, D_XA + D_CAT), BF16), SDS((N_MEM, 2 * D_XA), F32)],
        compiler_params=_params(1, VMEM_LIMIT))(dycat, ymix, pg, kv, dkv_in)


def _outproj_fwd(ycat, wo, x, nw, name):
    T = x.shape[0]

    def body(ycat_ref, wo_ref, x_ref, nw_ref, o_ref, xn_ref):
        o = jnp.dot(ycat_ref[...], wo_ref[...], preferred_element_type=F32)
        o_ref[...] = o
        xn_ref[...] = x_ref[...] + _rms(o, nw_ref[...])

    return pl.pallas_call(
        body, name=name, grid=(T // TT,),
        in_specs=[_row_spec(D_CAT), _full_spec((D_CAT, D)), _row_spec(D), _full_spec((1, D))],
        out_specs=[_row_spec(D), _row_spec(D)], out_shape=[SDS((T, D), F32), SDS((T, D), F32)],
        compiler_params=_params(1, VMEM_LIMIT))(ycat, wo, x, nw)


def _outproj_bwd(dxo, o, nw, wo, name):
    T = dxo.shape[0]

    def body(dxo_ref, o_ref, nw_ref, wo_ref, dycat_ref, dobf_ref, dnw_ref):
        _, vjp = jax.vjp(_rms, o_ref[...], nw_ref[...])
        do, dnw = vjp(dxo_ref[...])
        dobf = do.astype(BF16)
        dobf_ref[...] = dobf
        dycat_ref[...] = _dot_nt(dobf, wo_ref[...])

        @pl.when(pl.program_id(0) == 0)
        def _():
            dnw_ref[...] = jnp.zeros_like(dnw_ref)
        dnw_ref[...] += dnw

    return pl.pallas_call(
        body, name=name, grid=(T // TT,),
        in_specs=[_row_spec(D), _row_spec(D), _full_spec((1, D)), _full_spec((D_CAT, D))],
        out_specs=[_row_spec(D_CAT), _row_spec(D), _full_spec((1, D))],
        out_shape=[SDS((T, D_CAT), F32), SDS((T, D), BF16), SDS((1, D), F32)],
        compiler_params=_params(1, VMEM_LIMIT))(dxo, o, nw, wo)


def _loss_head(xl, target):
    T = xl.shape[0]

    def body(x_ref, t_ref, loss_ref, dx_ref):
        err = x_ref[...] - t_ref[...]
        dx_ref[...] = err * (1.0 / D)

        @pl.when(pl.program_id(0) == 0)
        def _():
            loss_ref[...] = jnp.zeros_like(loss_ref)
        part = jnp.sum(jnp.sum(err * err, axis=1, keepdims=True), axis=0, keepdims=True) * (0.5 / D)
        loss_ref[...] += jnp.broadcast_to(part, loss_ref.shape)

    return pl.pallas_call(
        body, name="loss_head", grid=(T // TT,),
        in_specs=[_row_spec(D), _row_spec(D)],
        out_specs=[_full_spec((8, 128)), _row_spec(D)], out_shape=[SDS((8, 128), F32), SDS((T, D), F32)],
        compiler_params=_params(1))(xl, target)


def _gmlp_pre(u, v, lnw, lnb):
    vg = jax.nn.gelu(v)
    xc = vg - jnp.mean(vg, axis=-1, keepdims=True)
    vl = xc * lax.rsqrt(jnp.mean(xc * xc, axis=-1, keepdims=True) + EPS) * lnw + lnb
    return jax.nn.gelu(u), vl


def _tril(n, strict=False):
    r = lax.broadcasted_iota(jnp.int32, (n, n), 0)
    c = lax.broadcasted_iota(jnp.int32, (n, n), 1)
    return (r > c) if strict else (r >= c)


def _gmlp_fwd(pm, lnw, lnb, ws, bs3, name):
    T = pm.shape[0]

    def body(pm_ref, lnw_ref, lnb_ref, ws_ref, bs_ref, y_ref):
        ug, vl = _gmlp_pre(pm_ref[:, :D], pm_ref[:, D:], lnw_ref[...], lnb_ref[...])
        mask = _tril(HD)
        for g in range(8):
            w = jnp.where(mask, ws_ref[g], 0.0)
            for c in range(TT // HD):
                rows, cols = slice(c * HD, (c + 1) * HD), slice(g * HD, (g + 1) * HD)
                y_ref[rows, cols] = ug[rows, cols] * (_dot_nn(w, vl[rows, cols]) + bs_ref[g])

    return pl.pallas_call(
        body, name=name, grid=(T // TT,),
        in_specs=[_row_spec(2 * D), _full_spec((1, D)), _full_spec((1, D)), _full_spec((8, HD, HD)),
                  _full_spec((8, HD, HD))],
        out_specs=_row_spec(D), out_shape=SDS((T, D), F32),
        compiler_params=_params(1, VMEM_LIMIT))(pm, lnw, lnb, ws, bs3)


def _gmlp_bwd(dy, pm, lnw, lnb, ws, bs3, name):
    T = pm.shape[0]
    n_t = T // TT

    def body(dy_ref, pm_ref, lnw_ref, lnb_ref, ws_ref, bs_ref, dpm_ref, dlnw_ref, dlnb_ref, dws_ref, dbs_ref,
             dug_scr, dvl_scr, dbs_scr):
        i = pl.program_id(0)

        @pl.when(i == 0)
        def _():
            dlnw_ref[...] = jnp.zeros_like(dlnw_ref)
            dlnb_ref[...] = jnp.zeros_like(dlnb_ref)
            dws_ref[...] = jnp.zeros_like(dws_ref)
            dbs_scr[...] = jnp.zeros_like(dbs_scr)

        (ug, vl), vjp = jax.vjp(_gmlp_pre, pm_ref[:, :D], pm_ref[:, D:], lnw_ref[...], lnb_ref[...])
        mask = _tril(HD)
        for g in range(8):
            w = jnp.where(mask, ws_ref[g], 0.0)
            dw = jnp.zeros((HD, HD), F32)
            db = jnp.zeros((HD, HD), F32)
            for c in range(TT // HD):
                rows, cols = slice(c * HD, (c + 1) * HD), slice(g * HD, (g + 1) * HD)
                dyb, vlb = dy_ref[rows, cols], vl[rows, cols]
                sp = _dot_nn(w, vlb) + bs_ref[g]
                dsp = dyb * ug[rows, cols]
                dug_scr[rows, cols] = dyb * sp
                dvl_scr[rows, cols] = _dot_tn(w, dsp)
                dw += _dot_nt(dsp, vlb)
                db += dsp
            dws_ref[g] += jnp.where(mask, dw, 0.0)
            dbs_scr[g] += db
        du, dv, dlnw, dlnb = vjp((dug_scr[...], dvl_scr[...]))
        dpm_ref[:, :D] = du.astype(BF16)
        dpm_ref[:, D:] = dv.astype(BF16)
        dlnw_ref[...] += dlnw
        dlnb_ref[...] += dlnb

        @pl.when(i == n_t - 1)
        def _():
            for g in range(8):
                dbs_ref[g] = jnp.broadcast_to(jnp.sum(dbs_scr[g], axis=1, keepdims=True), (HD, HD))

    return pl.pallas_call(
        body, name=name, grid=(n_t,),
        in_specs=[_row_spec(D), _row_spec(2 * D), _full_spec((1, D)), _full_spec((1, D)), _full_spec((8, HD, HD)),
                  _full_spec((8, HD, HD))],
        out_specs=[_row_spec(2 * D), _full_spec((1, D)), _full_spec((1, D)), _full_spec((8, HD, HD)),
                   _full_spec((8, HD, HD))],
        out_shape=[SDS((T, 2 * D), BF16), SDS((1, D), F32), SDS((1, D), F32), SDS((8, HD, HD), F32),
                   SDS((8, HD, HD), F32)],
        scratch_shapes=[pltpu.VMEM((TT, D), F32), pltpu.VMEM((TT, D), F32), pltpu.VMEM((8, HD, HD), F32)],
        compiler_params=_params(1, VMEM_LIMIT))(dy, pm, lnw, lnb, ws, bs3)


def _prev_spec(width, T):
    return pl.BlockSpec((HALO, width), lambda i: (jnp.maximum(i * (TT // HALO) - 1, 0), 0))


def _next_spec(width, T):
    return pl.BlockSpec((HALO, width), lambda i: (jnp.minimum((i + 1) * (TT // HALO), T // HALO - 1), 0))


def _rows_before(ext, j):
    return ext[HALO:] if j == 0 else pltpu.roll(ext, j, 0)[HALO:]


def _rows_after(ext, j):
    n = ext.shape[0]
    return ext[:n - HALO] if j == 0 else pltpu.roll(ext, n - j, 0)[:n - HALO]


def _conv_apply(ext_s, w):
    K = w.shape[0]
    y = _rows_before(ext_s, K - 1) * w[0:1]
    for k in range(1, K):
        y = y + _rows_before(ext_s, K - 1 - k) * w[k:k + 1]
    return y


def _conv_grads(ext_s, ext_dy, w):
    K = w.shape[0]
    dy = ext_dy[:ext_dy.shape[0] - HALO]
    ds = _rows_after(ext_dy, K - 1) * w[0:1]
    dws = [jnp.sum(dy * _rows_before(ext_s, K - 1), axis=0, keepdims=True)]
    for k in range(1, K):
        ds = ds + _rows_after(ext_dy, K - 1 - k) * w[k:k + 1]
        dws.append(jnp.sum(dy * _rows_before(ext_s, K - 1 - k), axis=0, keepdims=True))
    return ds, jnp.concatenate(dws, axis=0)


def _sconv_fwd(pm, w, name):
    T = pm.shape[0]

    def body(pm_ref, prev_ref, w_ref, y_ref):
        s = pm_ref[:, D:2 * D] * pm_ref[:, 2 * D:]
        sp = jnp.where(pl.program_id(0) > 0, prev_ref[:, D:2 * D] * prev_ref[:, 2 * D:], 0.0)
        y_ref[...] = pm_ref[:, :D] * _conv_apply(jnp.concatenate([sp, s], axis=0), w_ref[...])

    return pl.pallas_call(
        body, name=name, grid=(T // TT,),
        in_specs=[_row_spec(3 * D), _prev_spec(3 * D, T), _full_spec((3, D))],
        out_specs=_row_spec(D), out_shape=SDS((T, D), F32),
        compiler_params=_params(1, VMEM_LIMIT))(pm, pm, w)


def _sconv_bwd(dy, pm, w, name):
    T = pm.shape[0]
    n_t = T // TT

    def body(dy_ref, dyn_ref, pm_ref, prev_ref, next_ref, w_ref, dpm_ref, dw_ref):
        i = pl.program_id(0)
        bg, cg, hv = pm_ref[:, :D], pm_ref[:, D:2 * D], pm_ref[:, 2 * D:]
        sp = jnp.where(i > 0, prev_ref[:, D:2 * D] * prev_ref[:, 2 * D:], 0.0)
        ext_s = jnp.concatenate([sp, cg * hv], axis=0)
        dyv = dy_ref[...]
        dcn = jnp.where(i < n_t - 1, dyn_ref[...] * next_ref[:, :D], 0.0)
        ds, dw = _conv_grads(ext_s, jnp.concatenate([dyv * bg, dcn], axis=0), w_ref[...])
        dpm_ref[:, :D] = (dyv * _conv_apply(ext_s, w_ref[...])).astype(BF16)
        dpm_ref[:, D:2 * D] = (ds * hv).astype(BF16)
        dpm_ref[:, 2 * D:] = (ds * cg).astype(BF16)

        @pl.when(i == 0)
        def _():
            dw_ref[...] = jnp.zeros_like(dw_ref)
        dw_ref[...] += dw

    return pl.pallas_call(
        body, name=name, grid=(n_t,),
        in_specs=[_row_spec(D), _next_spec(D, T), _row_spec(3 * D), _prev_spec(3 * D, T), _next_spec(3 * D, T),
                  _full_spec((3, D))],
        out_specs=[_row_spec(3 * D), _full_spec((3, D))],
        out_shape=[SDS((T, 3 * D), BF16), SDS((3, D), F32)],
        compiler_params=_params(1, VMEM_LIMIT))(dy, dy, pm, pm, pm, w)


def _dnconv_fwd(pm, w, name):
    T = pm.shape[0]

    def body(pm_ref, prev_ref, w_ref, c_ref):
        sp = jnp.where(pl.program_id(0) > 0, prev_ref[...], 0.0)
        c_ref[...] = _conv_apply(jnp.concatenate([sp, pm_ref[...]], axis=0), w_ref[...])

    return pl.pallas_call(
        body, name=name, grid=(T // TT,),
        in_specs=[_row_spec(3 * D), _prev_spec(3 * D, T), _full_spec((4, 3 * D))],
        out_specs=_row_spec(3 * D), out_shape=SDS((T, 3 * D), F32),
        compiler_params=_params(1, VMEM_LIMIT))(pm, pm, w)


def _dnconv_bwd(dcq, dck, dcv, dab, pm, w, name):
    T = pm.shape[0]
    n_t = T // TT

    def body(dq_ref, dk_ref, dv_ref, dqn_ref, dkn_ref, dvn_ref, dab_ref, pm_ref, prev_ref, w_ref, dpm_ref, dw_ref):
        i = pl.program_id(0)
        sp = jnp.where(i > 0, prev_ref[...], 0.0)
        ext_s = jnp.concatenate([sp, pm_ref[...]], axis=0)
        own = jnp.concatenate([dq_ref[...], dk_ref[...], dv_ref[...]], axis=1)
        nxt = jnp.where(i < n_t - 1, jnp.concatenate([dqn_ref[...], dkn_ref[...], dvn_ref[...]], axis=1), 0.0)
        ds, dw = _conv_grads(ext_s, jnp.concatenate([own, nxt], axis=0), w_ref[...])
        dpm_ref[:, :3 * D] = ds.astype(BF16)
        dpm_ref[:, 3 * D:] = dab_ref[...].astype(BF16)

        @pl.when(i == 0)
        def _():
            dw_ref[...] = jnp.zeros_like(dw_ref)
        dw_ref[...] += dw

    return pl.pallas_call(
        body, name=name, grid=(n_t,),
        in_specs=[_row_spec(D), _row_spec(D), _row_spec(D), _next_spec(D, T), _next_spec(D, T), _next_spec(D, T),
                  _row_spec(AB_PAD), _row_spec(3 * D), _prev_spec(3 * D, T), _full_spec((4, 3 * D))],
        out_specs=[_row_spec(3 * D + AB_PAD), _full_spec((4, 3 * D))],
        out_shape=[SDS((T, 3 * D + AB_PAD), BF16), SDS((4, 3 * D), F32)],
        compiler_params=_params(1, VMEM_LIMIT))(dcq, dck, dcv, dcq, dck, dcv, dab, pm, pm, w)


def _l2n(x):
    return x * lax.rsqrt(jnp.sum(x * x, axis=-1, keepdims=True) + EPS)


def _softplus(x):
    return jnp.maximum(x, 0.0) + jnp.log1p(jnp.exp(-jnp.abs(x)))


def _dn_gates(ab, alog, dtb, h):
    lane = lax.broadcasted_iota(jnp.int32, ab.shape, 1)
    g_all = -jnp.exp(alog) * _softplus(ab + dtb)
    g = jnp.sum(jnp.where(lane == h, g_all, 0.0), axis=1, keepdims=True)
    beta = jnp.sum(jnp.where(lane == 8 + h, jax.nn.sigmoid(ab), 0.0), axis=1, keepdims=True)
    ones = jnp.ones((1, HD), F32)
    return g * ones, beta * ones


def _dn_chunk(cq, ck, cv, gb, bb, S, onw):
    C = DN_C
    q = _l2n(jax.nn.silu(cq)) * (HD ** -0.5)
    k = _l2n(jax.nn.silu(ck))
    v = jax.nn.silu(cv)
    incl, strict = _tril(C), _tril(C, strict=True)
    gcum = _dot_hi(incl.astype(F32), gb)
    gi = gcum[:, :C]
    gj = gcum.T[:C, :]
    decay = jnp.where(incl, jnp.exp(jnp.where(incl, gi - gj, 0.0)), 0.0)
    kb = k * bb
    a_mat = jnp.where(strict, mm_nt(kb, k) * decay, 0.0)
    p = -a_mat
    eye = (lax.broadcasted_iota(jnp.int32, (C, C), 0) == lax.broadcasted_iota(jnp.int32, (C, C), 1)).astype(F32)
    t_mat = eye + p
    for _ in range(5):
        p = _dot_hi(p, p)
        t_mat = t_mat + _dot_hi(t_mat, p)
    eg = jnp.exp(gcum)
    u = mm(t_mat, v * bb)
    w = mm(t_mat, kb * eg)
    qk = mm_nt(q, k) * decay
    glast = gcum[C - 1:C, :]
    v_new = u - mm(w, S)
    o = mm(q * eg, S) + mm(qk, v_new)
    s_new = S * jnp.exp(glast) + mm_tn(k * jnp.exp(glast - gcum), v_new)
    return _rms(o, onw), s_new


def _dn_specs(T, rev):
    nb = T // DN_TB
    blk = (lambda n: nb - 1 - n) if rev else (lambda n: n)
    head = [pl.BlockSpec((DN_TB, HD), functools.partial(lambda n, h, off: (blk(n), off + h), off=8 * s)) for s in range(3)]
    ab = pl.BlockSpec((DN_TB, AB_PAD), lambda n, h: (blk(n), 3 * D // AB_PAD))
    st = pl.BlockSpec((DN_TB // DN_C, None, HD, HD), lambda n, h: (blk(n), h, 0, 0))
    out = pl.BlockSpec((DN_TB, HD), lambda n, h: (blk(n), h))
    row = pl.BlockSpec((1, HD), lambda n, h: (0, 0))
    return nb, head, ab, st, out, row


def _dn_fwd(cpre, pm, alog, dtb, onw, name):
    T = cpre.shape[0]
    nb, head, ab, st, out, row = _dn_specs(T, False)

    def body(cq_ref, ck_ref, cv_ref, ab_ref, alog_ref, dtb_ref, onw_ref, o_ref, st_ref, s_scr):
        n, h = pl.program_id(0), pl.program_id(1)

        @pl.when(n == 0)
        def _():
            s_scr[h] = jnp.zeros((HD, HD), F32)
        gb, bb = _dn_gates(ab_ref[...], alog_ref[...], dtb_ref[...], h)
        S = s_scr[h]
        for c in range(DN_TB // DN_C):
            rows = slice(c * DN_C, (c + 1) * DN_C)
            st_ref[c] = S
            o, S = _dn_chunk(cq_ref[rows, :], ck_ref[rows, :], cv_ref[rows, :], gb[rows], bb[rows], S, onw_ref[...])
            o_ref[rows, :] = o
        s_scr[h] = S

    return pl.pallas_call(
        body, name=name, grid=(nb, 8),
        in_specs=head + [ab, row, row, row], out_specs=[out, st],
        out_shape=[SDS((T, D), F32), SDS((T // DN_C, 8, HD, HD), F32)],
        scratch_shapes=[pltpu.VMEM((8, HD, HD), F32)],
        compiler_params=_params(2, VMEM_LIMIT))(cpre, cpre, cpre, pm, alog, dtb, onw)


def _dn_bwd(do, cpre, pm, st, alog, dtb, onw, name):
    T = cpre.shape[0]
    nb, head, ab, stspec, out, row = _dn_specs(T, True)

    def body(do_ref, cq_ref, ck_ref, cv_ref, ab_ref, st_ref, alog_ref, dtb_ref, onw_ref,
             dcq_ref, dck_ref, dcv_ref, dab_ref, dalog_ref, ddtb_ref, donw_ref, ds_scr):
        n, h = pl.program_id(0), pl.program_id(1)

        @pl.when(n == 0)
        def _():
            ds_scr[h] = jnp.zeros((HD, HD), F32)

        @pl.when((n == 0) & (h == 0))
        def _():
            dalog_ref[...] = jnp.zeros_like(dalog_ref)
            ddtb_ref[...] = jnp.zeros_like(ddtb_ref)
            donw_ref[...] = jnp.zeros_like(donw_ref)

        @pl.when(h == 0)
        def _():
            dab_ref[...] = jnp.zeros_like(dab_ref)

        (gb, bb), gates_vjp = jax.vjp(lambda a, b, c: _dn_gates(a, b, c, h), ab_ref[...], alog_ref[...], dtb_ref[...])
        dS = ds_scr[h]
        n_c = DN_TB // DN_C
        dgs, dbs = [None] * n_c, [None] * n_c
        donw = jnp.zeros((1, HD), F32)
        for c in reversed(range(n_c)):
            rows = slice(c * DN_C, (c + 1) * DN_C)
            _, vjp = jax.vjp(_dn_chunk, cq_ref[rows, :], ck_ref[rows, :], cv_ref[rows, :], gb[rows], bb[rows],
                             st_ref[c], onw_ref[...])
            dcq, dck, dcv, dgs[c], dbs[c], dS, dn = vjp((do_ref[rows, :], dS))
            dcq_ref[rows, :] = dcq
            dck_ref[rows, :] = dck
            dcv_ref[rows, :] = dcv
            donw += dn
        ds_scr[h] = dS
        dab, dalog, ddtb = gates_vjp((jnp.concatenate(dgs, axis=0), jnp.concatenate(dbs, axis=0)))
        dab_ref[...] += dab
        dalog_ref[...] += dalog
        ddtb_ref[...] += ddtb
        donw_ref[...] += donw

    dabspec = pl.BlockSpec((DN_TB, AB_PAD), lambda n, h: (nb - 1 - n, 0))
    return pl.pallas_call(
        body, name=name, grid=(nb, 8),
        in_specs=[out] + head + [ab, stspec, row, row, row],
        out_specs=[out, out, out, dabspec, row, row, row],
        out_shape=[SDS((T, D), F32)] * 3 + [SDS((T, AB_PAD), F32)] + [SDS((1, HD), F32)] * 3,
        scratch_shapes=[pltpu.VMEM((8, HD, HD), F32)],
        compiler_params=_params(2, VMEM_LIMIT))(do, cpre, cpre, cpre, pm, st, alog, dtb, onw)


_BNN = (((2,), (1,)), ((0,), (0,)))
_BNT = (((2,), (2,)), ((0,), (0,)))
_BTN = (((1,), (1,)), ((0,), (0,)))


def _bdot(a, b, dims):
    return lax.dot_general(a.astype(BF16), b.astype(BF16), dims, preferred_element_type=F32)


def _bdot3(a, b, dims):
    ah, bh = a.astype(BF16), b.astype(BF16)
    al, bl = (a - ah.astype(F32)).astype(BF16), (b - bh.astype(F32)).astype(BF16)
    d = functools.partial(lax.dot_general, dimension_numbers=dims, preferred_element_type=F32)
    return d(ah, bh) + (d(ah, bl) + d(al, bh))


def _bdot_hi(a, b, dims):
    return lax.dot_general(a, b, dims, precision=HI, preferred_element_type=F32)


def _batched_matmuls(dot):
    @jax.custom_vjp
    def nn(a, b):
        return dot(a, b, _BNN)

    @jax.custom_vjp
    def nt(a, b):
        return dot(a, b, _BNT)

    @jax.custom_vjp
    def tn(a, b):
        return dot(a, b, _BTN)

    nn.defvjp(lambda a, b: (dot(a, b, _BNN), (a, b)), lambda r, g: (dot(g, r[1], _BNT), dot(r[0], g, _BTN)))
    nt.defvjp(lambda a, b: (dot(a, b, _BNT), (a, b)), lambda r, g: (dot(g, r[1], _BNN), dot(g, r[0], _BTN)))
    tn.defvjp(lambda a, b: (dot(a, b, _BTN), (a, b)), lambda r, g: (dot(r[1], g, _BNT), dot(r[0], g, _BNN)))
    return nn, nt, tn


bmm, bmm_nt, bmm_tn = _batched_matmuls(_bdot)
bmm3, _, _ = _batched_matmuls(_bdot3)
bmm_hi, bmm_hi_nt, _ = _batched_matmuls(_bdot_hi)

@jax.custom_vjp
def _neumann_inverse(n):
    C = n.shape[1]
    eye = lax.broadcasted_iota(jnp.int32, n.shape, 1) == lax.broadcasted_iota(jnp.int32, n.shape, 2)
    t = eye.astype(F32) + n
    for _ in range(5):
        n = _bdot3(n, n, _BNN)
        t = t + _bdot3(t, n, _BNN)
    return t


def _neumann_inverse_fwd(n):
    t = _neumann_inverse(n)
    return t, t


def _neumann_inverse_bwd(t, g):
    return (_bdot3(_bdot3(t, g, _BTN), t, _BNT),)


_neumann_inverse.defvjp(_neumann_inverse_fwd, _neumann_inverse_bwd)


@jax.custom_vjp
def _saved_inverse(n, t):
    return t


_saved_inverse.defvjp(lambda n, t: (t, t), lambda t, g: (_bdot3(_bdot3(t, g, _BTN), t, _BNT), jnp.zeros_like(t)))

DN_NCH = DN_TB // DN_C
DN_NH = 4


def _decay_terms(ab, alog, dtb, first_head, n_heads):
    C = DN_C
    lane = lax.broadcasted_iota(jnp.int32, ab.shape, 1)
    g_all = (-jnp.exp(alog) * _softplus(ab + dtb)).reshape(DN_NCH, C, HD)
    beta_all = jax.nn.sigmoid(ab)
    r = lax.broadcasted_iota(jnp.int32, (DN_NCH, C, C), 1)
    c = lax.broadcasted_iota(jnp.int32, (DN_NCH, C, C), 2)
    gc_all = bmm_hi((r >= c).astype(F32), g_all)
    gc_rows = [gc_all[i].T for i in range(DN_NCH)]
    lane3 = lax.broadcasted_iota(jnp.int32, (DN_NCH, C, HD), 2)
    row = lax.broadcasted_iota(jnp.int32, (HD, C), 0)
    ones = jnp.ones((1, HD), F32)
    gcs, gjs, betas = [], [], []
    for i in range(n_heads):
        h = first_head + i
        gcs.append(jnp.sum(jnp.where(lane3 == h, gc_all, 0.0), axis=2, keepdims=True) * ones)
        gjs.append(jnp.concatenate(
            [jnp.broadcast_to(jnp.sum(jnp.where(row == h, t, 0.0), axis=0, keepdims=True), (C, C))[None] for t in gc_rows],
            axis=0))
        beta = jnp.sum(jnp.where(lane == 8 + h, beta_all, 0.0), axis=1, keepdims=True) * ones
        betas.append(beta.reshape(DN_NCH, C, HD))
    return jnp.concatenate(gcs, axis=0), jnp.concatenate(gjs, axis=0), jnp.concatenate(betas, axis=0)


def _dn_prep(cq, ck, cv, gcum, gj, bb, t_saved=None):
    B, C = cq.shape[0], DN_C
    q = _l2n(jax.nn.silu(cq)) * (HD ** -0.5)
    k = _l2n(jax.nn.silu(ck))
    v = jax.nn.silu(cv)
    r = lax.broadcasted_iota(jnp.int32, (B, C, C), 1)
    c = lax.broadcasted_iota(jnp.int32, (B, C, C), 2)
    incl, strict = r >= c, r > c
    decay = jnp.where(incl, jnp.exp(jnp.where(incl, gcum[:, :, :C] - gj, 0.0)), 0.0)
    kb = k * bb
    n_mat = -jnp.where(strict, bmm_nt(kb, k) * decay, 0.0)
    t_mat = _neumann_inverse(n_mat) if t_saved is None else _saved_inverse(n_mat, t_saved)
    eg = jnp.exp(gcum)
    glast = gcum[:, C - 1:C, :]
    return (bmm(t_mat, v * bb), bmm(t_mat, kb * eg), bmm_nt(q, k) * decay, q * eg, k * jnp.exp(glast - gcum),
            jnp.exp(glast), t_mat)


def _dn_scan_step(u, w, qk, qd, kd, egl, S, onw):
    v_new = u - bmm(w, S)
    o = bmm(qd, S) + bmm(qk, v_new)
    return _rms(o, onw), S * egl + bmm_tn(kd, v_new)


def _head_gates(ab, alog, dtb, first_head, n_heads):
    gs, bs = [], []
    for i in range(n_heads):
        g, b = _dn_gates(ab, alog, dtb, first_head + i)
        gs.append(g.reshape(DN_NCH, DN_C, HD))
        bs.append(b.reshape(DN_NCH, DN_C, HD))
    return jnp.concatenate(gs, axis=0), jnp.concatenate(bs, axis=0)


def _to_batch(ref, n_heads):
    return jnp.concatenate([ref[:, i * HD:(i + 1) * HD].astype(F32).reshape(DN_NCH, DN_C, HD) for i in range(n_heads)],
                           axis=0)


def _from_batch(ref, val, n_heads):
    for i in range(n_heads):
        ref[:, i * HD:(i + 1) * HD] = val[i * DN_NCH:(i + 1) * DN_NCH].reshape(DN_TB, HD).astype(ref.dtype)


def _prep_specs(T, rev):
    nb = T // DN_TB
    blk = (lambda n: nb - 1 - n) if rev else (lambda n: n)
    ng = 8 // DN_NH
    head = [pl.BlockSpec((DN_TB, DN_NH * HD), functools.partial(lambda n, h, off: (blk(n), off + h), off=ng * s))
            for s in range(3)]
    ab = pl.BlockSpec((DN_TB, AB_PAD), lambda n, h: (blk(n), 3 * D // AB_PAD))
    row = pl.BlockSpec((1, HD), lambda n, h: (0, 0))
    wide = pl.BlockSpec((DN_TB, DN_NH * HD), lambda n, h: (blk(n), h))
    qk = pl.BlockSpec((DN_NCH, DN_NH, DN_C, DN_C), lambda n, h: (blk(n), h, 0, 0))
    eg = pl.BlockSpec((DN_NCH, DN_NH, 1, HD), lambda n, h: (blk(n), h, 0, 0))
    return nb, ng, head, ab, row, wide, qk, eg


def _dn_prep_fwd(cpre, pm, alog, dtb, name):
    T = cpre.shape[0]
    nb, ng, head, ab, row, wide, qks, egs = _prep_specs(T, False)

    def body(cq_ref, ck_ref, cv_ref, ab_ref, alog_ref, dtb_ref, u_ref, w_ref, qk_ref, qd_ref, kd_ref, e_ref, t_ref):
        gcum, gj, bb = _decay_terms(ab_ref[...], alog_ref[...], dtb_ref[...], pl.program_id(1) * DN_NH, DN_NH)
        u, w, qk, qd, kd, egl, t_mat = _dn_prep(_to_batch(cq_ref, DN_NH), _to_batch(ck_ref, DN_NH),
                                                _to_batch(cv_ref, DN_NH), gcum, gj, bb)
        _from_batch(u_ref, u, DN_NH)
        _from_batch(w_ref, w, DN_NH)
        _from_batch(qd_ref, qd, DN_NH)
        _from_batch(kd_ref, kd, DN_NH)
        for i in range(DN_NH):
            qk_ref[:, i] = qk[i * DN_NCH:(i + 1) * DN_NCH].astype(BF16)
            e_ref[:, i] = egl[i * DN_NCH:(i + 1) * DN_NCH]
            t_ref[:, i] = t_mat[i * DN_NCH:(i + 1) * DN_NCH]

    return pl.pallas_call(
        body, name=name, grid=(nb, ng), in_specs=head + [ab, row, row],
        out_specs=[wide, wide, qks, wide, wide, egs, qks],
        out_shape=[SDS((T, D), F32), SDS((T, D), BF16), SDS((T // DN_C, 8, DN_C, DN_C), BF16), SDS((T, D), BF16),
                   SDS((T, D), BF16), SDS((T // DN_C, 8, 1, HD), F32), SDS((T // DN_C, 8, DN_C, DN_C), F32)],
        compiler_params=_params(2, VMEM_LIMIT))(cpre, cpre, cpre, pm, alog, dtb)


def _dn_prep_bwd(du, dw, dqk, dqd, dkd, degl, t_mat, cpre, pm, alog, dtb, name):
    T = cpre.shape[0]
    nb, ng, head, ab, row, wide, qks, egs = _prep_specs(T, True)

    def body(du_ref, dw_ref, dqk_ref, dqd_ref, dkd_ref, de_ref, t_ref, cq_ref, ck_ref, cv_ref, ab_ref, alog_ref,
             dtb_ref, dcq_ref, dck_ref, dcv_ref, dab_ref, dalog_ref, ddtb_ref):
        n, h = pl.program_id(0), pl.program_id(1)

        @pl.when((n == 0) & (h == 0))
        def _():
            dalog_ref[...] = jnp.zeros_like(dalog_ref)
            ddtb_ref[...] = jnp.zeros_like(ddtb_ref)

        @pl.when(h == 0)
        def _():
            dab_ref[...] = jnp.zeros_like(dab_ref)

        t_saved = jnp.concatenate([t_ref[:, i] for i in range(DN_NH)], axis=0)

        def fwd(cq, ck, cv, ab_v, alog_v, dtb_v):
            gcum, gj, bb = _decay_terms(ab_v, alog_v, dtb_v, h * DN_NH, DN_NH)
            return _dn_prep(cq, ck, cv, gcum, gj, bb, t_saved)[:6]

        _, vjp = jax.vjp(fwd, _to_batch(cq_ref, DN_NH), _to_batch(ck_ref, DN_NH), _to_batch(cv_ref, DN_NH), ab_ref[...],
                         alog_ref[...], dtb_ref[...])
        cot = (_to_batch(du_ref, DN_NH), _to_batch(dw_ref, DN_NH),
               jnp.concatenate([dqk_ref[:, i] for i in range(DN_NH)], axis=0), _to_batch(dqd_ref, DN_NH),
               _to_batch(dkd_ref, DN_NH), jnp.concatenate([de_ref[:, i] for i in range(DN_NH)], axis=0))
        dcq, dck, dcv, dab, dalog, ddtb = vjp(cot)
        _from_batch(dcq_ref, dcq, DN_NH)
        _from_batch(dck_ref, dck, DN_NH)
        _from_batch(dcv_ref, dcv, DN_NH)
        dab_ref[...] += dab
        dalog_ref[...] += dalog
        ddtb_ref[...] += ddtb

    dabspec = pl.BlockSpec((DN_TB, AB_PAD), lambda n, h: (nb - 1 - n, 0))
    return pl.pallas_call(
        body, name=name, grid=(nb, ng),
        in_specs=[wide, wide, qks, wide, wide, egs, qks] + head + [ab, row, row],
        out_specs=[wide, wide, wide, dabspec, row, row],
        out_shape=[SDS((T, D), F32)] * 3 + [SDS((T, AB_PAD), F32)] + [SDS((1, HD), F32)] * 2,
        compiler_params=_params(2, VMEM_LIMIT))(du, dw, dqk, dqd, dkd, degl, t_mat, cpre, cpre, cpre, pm, alog, dtb)


def _scan_specs(T, rev):
    nb = T // DN_TB
    blk = (lambda n: nb - 1 - n) if rev else (lambda n: n)
    wide = pl.BlockSpec((DN_TB, D), lambda n: (blk(n), 0))
    qk = pl.BlockSpec((DN_NCH, 8, DN_C, DN_C), lambda n: (blk(n), 0, 0, 0))
    eg = pl.BlockSpec((DN_NCH, 8, 1, HD), lambda n: (blk(n), 0, 0, 0))
    st = pl.BlockSpec((DN_NCH, 8, HD, HD), lambda n: (blk(n), 0, 0, 0))
    row = pl.BlockSpec((1, HD), lambda n: (0, 0))
    return nb, wide, qk, eg, st, row


def _heads_of(ref, rows):
    return jnp.concatenate([ref[rows, h * HD:(h + 1) * HD].astype(F32)[None] for h in range(8)], axis=0)


def _dn_scan_fwd(u, w, qk, qd, kd, egl, onw, name):
    T = u.shape[0]
    nb, wide, qks, egs, sts, row = _scan_specs(T, False)

    def body(u_ref, w_ref, qk_ref, qd_ref, kd_ref, e_ref, onw_ref, o_ref, st_ref, s_scr):
        @pl.when(pl.program_id(0) == 0)
        def _():
            s_scr[...] = jnp.zeros_like(s_scr)
        S = s_scr[...]
        for c in range(DN_NCH):
            rows = slice(c * DN_C, (c + 1) * DN_C)
            st_ref[c] = S
            o, S = _dn_scan_step(_heads_of(u_ref, rows), _heads_of(w_ref, rows), qk_ref[c].astype(F32),
                                 _heads_of(qd_ref, rows), _heads_of(kd_ref, rows), e_ref[c], S, onw_ref[...])
            for h in range(8):
                o_ref[rows, h * HD:(h + 1) * HD] = o[h]
        s_scr[...] = S

    return pl.pallas_call(
        body, name=name, grid=(nb,), in_specs=[wide, wide, qks, wide, wide, egs, row], out_specs=[wide, sts],
        out_shape=[SDS((T, D), F32), SDS((T // DN_C, 8, HD, HD), F32)],
        scratch_shapes=[pltpu.VMEM((8, HD, HD), F32)],
        compiler_params=_params(1, VMEM_LIMIT))(u, w, qk, qd, kd, egl, onw)


def _dn_scan_bwd(do, u, w, qk, qd, kd, egl, st, onw, name):
    T = u.shape[0]
    nb, wide, qks, egs, sts, row = _scan_specs(T, True)

    def body(do_ref, u_ref, w_ref, qk_ref, qd_ref, kd_ref, e_ref, st_ref, onw_ref,
             du_ref, dw_ref, dqk_ref, dqd_ref, dkd_ref, de_ref, donw_ref, ds_scr):
        @pl.when(pl.program_id(0) == 0)
        def _():
            ds_scr[...] = jnp.zeros_like(ds_scr)
            donw_ref[...] = jnp.zeros_like(donw_ref)
        dS = ds_scr[...]
        donw = jnp.zeros((1, HD), F32)
        for c in reversed(range(DN_NCH)):
            rows = slice(c * DN_C, (c + 1) * DN_C)
            _, vjp = jax.vjp(_dn_scan_step, _heads_of(u_ref, rows), _heads_of(w_ref, rows), qk_ref[c].astype(F32),
                             _heads_of(qd_ref, rows), _heads_of(kd_ref, rows), e_ref[c], st_ref[c], onw_ref[...])
            du, dw, dqk, dqd, dkd, de, dS, dn = vjp((_heads_of(do_ref, rows), dS))
            for h in range(8):
                cols = slice(h * HD, (h + 1) * HD)
                du_ref[rows, cols] = du[h]
                dw_ref[rows, cols] = dw[h]
                dqd_ref[rows, cols] = dqd[h]
                dkd_ref[rows, cols] = dkd[h]
            dqk_ref[c] = dqk
            de_ref[c] = de
            donw += dn
        ds_scr[...] = dS
        donw_ref[...] += donw

    return pl.pallas_call(
        body, name=name, grid=(nb,), in_specs=[wide, wide, wide, qks, wide, wide, egs, sts, row],
        out_specs=[wide, wide, qks, wide, wide, egs, row],
        out_shape=[SDS((T, D), F32), SDS((T, D), F32), SDS((T // DN_C, 8, DN_C, DN_C), F32), SDS((T, D), F32),
                   SDS((T, D), F32), SDS((T // DN_C, 8, 1, HD), F32), SDS((1, HD), F32)],
        scratch_shapes=[pltpu.VMEM((8, HD, HD), F32)],
        compiler_params=_params(1, VMEM_LIMIT))(do, u, w, qk, qd, kd, egl, st, onw)


def _adamw(w, g, m, v, name):
    R, C = w.shape
    tr = 256 if R % 256 == 0 and R > 256 else R
    tc = 256 if tr == R and R > 256 and C % 256 == 0 else C
    c1 = 1.0 - ADAM_B1 ** ADAM_STEP
    c2 = 1.0 - ADAM_B2 ** ADAM_STEP

    def body(w_ref, g_ref, m_ref, v_ref, d_ref, nm_ref, nv_ref):
        gv = g_ref[...]
        nm = ADAM_B1 * m_ref[...] + (1.0 - ADAM_B1) * gv
        nv = ADAM_B2 * v_ref[...] + (1.0 - ADAM_B2) * (gv * gv)
        nm_ref[...] = nm
        nv_ref[...] = nv
        d_ref[...] = -ADAM_LR * ((nm / c1) / (jnp.sqrt(nv / c2) + ADAM_EPS) + ADAM_WD * w_ref[...])

    spec = pl.BlockSpec((tr, tc), lambda i, j: (i, j))
    return pl.pallas_call(
        body, name=name, grid=(R // tr, C // tc), in_specs=[spec] * 4, out_specs=[spec] * 3,
        out_shape=[SDS((R, C), F32)] * 3, compiler_params=_params(2, VMEM_LIMIT))(w, g, m, v)


def _local_step(x, mem, target, wts, sm):
    kinds = [i % 3 for i in range(DEPTH)]
    mnw = sm["mem_norm_w"].reshape(1, D)
    kv = _memkv_fwd(mem, mnw, wts["wkv"])
    saved, blocks = [], []
    for i, kind in enumerate(kinds):
        j = i // 3
        npre = sm["norm_pre"][i].reshape(1, D)
        npost = sm["norm_post"][i].reshape(1, D)
        mix, gate, wo = wts["blocks"](i, x)
        blocks.append((mix, gate, wo))
        pm, pg, h = _inproj_fwd(x, npre, mix, gate, kind == 2, f"inproj_fwd_{i}")
        extra = None
        if kind == 0:
            bs3 = jnp.broadcast_to(sm["a_b_s"][j][:, :, None], (8, HD, HD))
            ymix = _gmlp_fwd(pm, sm["a_ln_w"][j].reshape(1, D), sm["a_ln_b"][j].reshape(1, D), sm["a_w_s"][j], bs3,
                             f"gmlp_fwd_{i}")
            extra = bs3
        elif kind == 1:
            ymix = _sconv_fwd(pm, sm["b_conv_w"][j], f"sconv_fwd_{i}")
        else:
            cpre = _dnconv_fwd(pm, sm["c_conv_w"][j], f"dnconv_fwd_{i}")
            alog = jnp.pad(sm["c_a_log"][j], (0, HD - 8)).reshape(1, HD)
            dtb = jnp.pad(sm["c_dt_bias"][j], (0, HD - 8)).reshape(1, HD)
            onw = sm["c_o_norm_w"][j].reshape(1, HD)
            *prep, t_mat = _dn_prep_fwd(cpre, pm, alog, dtb, f"dn_prep_fwd_{i}")
            ymix, st = _dn_scan_fwd(*prep, onw, f"dn_scan_fwd_{i}")
            extra = (cpre, prep, t_mat, st, alog, dtb, onw)
        ycat = _ag_fwd(ymix, pg, kv, f"ag_fwd_{i}")
        o, xn = _outproj_fwd(ycat, wo, x, npost, f"outproj_fwd_{i}")
        saved.append((x, h, pm, pg, ymix, ycat, o, extra))
        x = xn

    loss, dx = _loss_head(x, target)

    g = {"wm": [None] * DEPTH, "wg": [None] * DEPTH, "wo": [None] * DEPTH, "norm_pre": [None] * DEPTH,
         "norm_post": [None] * DEPTH}
    dkv = jnp.zeros((N_MEM, 2 * D_XA), F32)
    sent = 0.0
    for i in reversed(range(DEPTH)):
        kind, j = kinds[i], i // 3
        xi, h, pm, pg, ymix, ycat, o, extra = saved[i]
        npre = sm["norm_pre"][i].reshape(1, D)
        npost = sm["norm_post"][i].reshape(1, D) + sent
        dycat, dobf, g["norm_post"][i] = _outproj_bwd(dx, o, npost, blocks[i][2], f"outproj_bwd_{i}")
        g["wo"][i] = _matmul_tn(ycat, dobf, f"dwo_{i}")
        dymix, dpg, dkv = _ag_bwd(dycat, ymix, pg, kv, dkv, f"ag_bwd_{i}")
        if kind == 0:
            dpm, dlnw, dlnb, dws, dbs3 = _gmlp_bwd(dymix, pm, sm["a_ln_w"][j].reshape(1, D),
                                                   sm["a_ln_b"][j].reshape(1, D), sm["a_w_s"][j], extra,
                                                   f"gmlp_bwd_{i}")
            g.setdefault("a_ln_w", {})[j] = dlnw.reshape(D)
            g.setdefault("a_ln_b", {})[j] = dlnb.reshape(D)
            g.setdefault("a_w_s", {})[j] = dws
            g.setdefault("a_b_s", {})[j] = dbs3[:, :, 0]
        elif kind == 1:
            dpm, dcw = _sconv_bwd(dymix, pm, sm["b_conv_w"][j], f"sconv_bwd_{i}")
            g.setdefault("b_conv_w", {})[j] = dcw
        else:
            cpre, prep, t_mat, st, alog, dtb, onw = extra
            *dprep, donw = _dn_scan_bwd(dymix, *prep, st, onw, f"dn_scan_bwd_{i}")
            dcq, dck, dcv, dab, dalog, ddtb = _dn_prep_bwd(*dprep, t_mat, cpre, pm, alog, dtb, f"dn_prep_bwd_{i}")
            dpm, dcw = _dnconv_bwd(dcq, dck, dcv, dab, pm, sm["c_conv_w"][j], f"dnconv_bwd_{i}")
            g.setdefault("c_conv_w", {})[j] = dcw
            g.setdefault("c_a_log", {})[j] = dalog[0, :8]
            g.setdefault("c_dt_bias", {})[j] = ddtb[0, :8]
            g.setdefault("c_o_norm_w", {})[j] = donw[0]
        if kind == 2:
            g["wm"][i] = _matmul_tn(dpm, h, f"dwm_{i}")
            g["wg"][i] = _matmul_tn(dpg, h, f"dwg_{i}")
        else:
            g["wm"][i] = _matmul_tn(h, dpm, f"dwm_{i}", GRAD_TILE[kind])
            g["wg"][i] = _matmul_tn(h, dpg, f"dwg_{i}", GRAD_TILE[kind])
        dx, g["norm_pre"][i] = _inproj_bwd(dpm, dpg, xi, npre, blocks[i][0], blocks[i][1], kind == 2, dx,
                                           f"inproj_bwd_{i}")
        sent = wts["layer_done"](i, g)
    g["mem_norm_w"], g["wkv"] = _memkv_bwd(mem, mnw, wts["wkv"], dkv)
    return loss[0, 0], dx, g


ANY = pl.BlockSpec(memory_space=pl.ANY)


def _place():
    return lax.axis_index("x"), lax.axis_index("y"), lax.axis_index("c")


def _gather_weights(big, vec):
    def body(big_ref, vec_ref, ob_ref, ov_ref, ici_send, ici_recv, d2d_send, d2d_recv, vec_send, vec_recv):
        x, y, c = _place()
        chip = 2 * x + y
        peers = [(1 - x, y), (x, 1 - y), (1 - x, 1 - y)]

        def rows(half, k):
            return pl.ds(half * HALF_ROWS + k * CHUNK_ROWS, CHUNK_ROWS)

        def over_ici(j, k, slab):
            px, py = peers[j]
            i = j * N_CHUNKS + k
            return pltpu.make_async_remote_copy(
                src_ref=big_ref.at[rows(c, k)], dst_ref=ob_ref.at[slab, rows(c, k)], send_sem=ici_send.at[i],
                recv_sem=ici_recv.at[i], device_id=(px, py, c), device_id_type=MESH)

        def over_d2d(j, k, half):
            px, py = peers[j]
            i = j * N_CHUNKS + k
            where = ob_ref.at[2 * px + py, rows(half, k)]
            return pltpu.make_async_remote_copy(
                src_ref=where, dst_ref=where, send_sem=d2d_send.at[i], recv_sem=d2d_recv.at[i],
                device_id=(x, y, 1 - c), device_id_type=MESH)

        def small(j, slab):
            px, py = peers[j]
            return pltpu.make_async_remote_copy(
                src_ref=vec_ref, dst_ref=ov_ref.at[slab], send_sem=vec_send.at[j], recv_sem=vec_recv.at[j],
                device_id=(px, py, c), device_id_type=MESH)

        sends = [small(j, chip) for j in range(3)] + [over_ici(j, k, chip) for k in range(N_CHUNKS) for j in range(3)]
        for cp in sends:
            cp.start()
        forwards = []
        for k in range(N_CHUNKS):
            for j, (px, py) in enumerate(peers):
                over_ici(j, k, 2 * px + py).wait_recv()
                forwards.append(over_d2d(j, k, c))
                forwards[-1].start()
        for k in range(N_CHUNKS):
            for j in range(3):
                over_d2d(j, k, 1 - c).wait_recv()
        for j, (px, py) in enumerate(peers):
            small(j, 2 * px + py).wait_recv()
        for cp in sends + forwards:
            cp.wait_send()

    n = 3 * N_CHUNKS
    dma = pltpu.SemaphoreType.DMA
    return pl.pallas_call(
        body, name="gather_weights", in_specs=[ANY, ANY], out_specs=[ANY, ANY],
        out_shape=[SDS((4,) + big.shape, big.dtype), SDS((4,) + vec.shape, vec.dtype)],
        scratch_shapes=[dma((n,)), dma((n,)), dma((n,)), dma((n,)), dma((3,)), dma((3,))])(big, vec)


HALF_ROWS = 3328
CHUNK_ROWS = 256
N_CHUNKS = HALF_ROWS // CHUNK_ROWS


def _swap_halves(gbig, small):
    def body(g_ref, s_ref, ob_ref, os_ref, send_sems, recv_sems):
        x, y, c = _place()
        copies = []
        for s in range(4):
            for k in range(N_CHUNKS):
                rows = pl.ds(k * CHUNK_ROWS, CHUNK_ROWS)
                copies.append(pltpu.make_async_remote_copy(
                    src_ref=g_ref.at[s, 1 - c, rows], dst_ref=ob_ref.at[s, rows], send_sem=send_sems.at[len(copies)],
                    recv_sem=recv_sems.at[len(copies)], device_id=(x, y, 1 - c), device_id_type=MESH))
        copies.append(pltpu.make_async_remote_copy(
            src_ref=s_ref, dst_ref=os_ref, send_sem=send_sems.at[len(copies)], recv_sem=recv_sems.at[len(copies)],
            device_id=(x, y, 1 - c), device_id_type=MESH))
        for cp in copies:
            cp.start()
        for cp in copies:
            cp.wait_recv()
        for cp in copies:
            cp.wait_send()

    n = 4 * N_CHUNKS + 1
    return pl.pallas_call(
        body, name="swap_halves", in_specs=[ANY, ANY], out_specs=[ANY, ANY],
        out_shape=[SDS((4, HALF_ROWS, D), F32), SDS(small.shape, F32)],
        scratch_shapes=[pltpu.SemaphoreType.DMA((n,)), pltpu.SemaphoreType.DMA((n,))])(gbig, small)


def _pair_sum(gbig, other):
    def body(g_ref, o_ref, pb_ref, own_ref):
        x, y, c = _place()
        v = jnp.where(c == 0, g_ref[0], g_ref[1]) + o_ref[...]
        pb_ref[...] = v.astype(BF16)

        @pl.when(pl.program_id(1) == 2 * x + y)
        def _():
            own_ref[...] = v

    return pl.pallas_call(
        body, name="pair_sum", grid=(N_CHUNKS, 4),
        in_specs=[pl.BlockSpec((None, 2, CHUNK_ROWS, D), lambda i, s: (s, 0, i, 0)),
                  pl.BlockSpec((None, CHUNK_ROWS, D), lambda i, s: (s, i, 0))],
        out_specs=[pl.BlockSpec((None, CHUNK_ROWS, D), lambda i, s: (s, i, 0)),
                   pl.BlockSpec((CHUNK_ROWS, D), lambda i, s: (i, 0))],
        out_shape=[SDS((4, HALF_ROWS, D), BF16), SDS((HALF_ROWS, D), F32)],
        compiler_params=_params(2, VMEM_LIMIT))(gbig, other)


def _add(a, b, name):
    def body(a_ref, b_ref, o_ref):
        o_ref[...] = a_ref[...] + b_ref[...]

    return pl.pallas_call(body, name=name, out_shape=SDS(a.shape, a.dtype), compiler_params=_params(0, VMEM_LIMIT))(a, b)


def _chip_exchange(pb, ps):
    n_small = ps.shape[0]

    def body(pb_ref, ps_ref, lb_ref, ls_ref, send_sems, recv_sems):
        x, y, c = _place()
        chip = 2 * x + y
        peers = [(1 - x, y), (x, 1 - y), (1 - x, 1 - y)]

        def copies(slab_of):
            out = []
            for j, (px, py) in enumerate(peers):
                for k in range(N_CHUNKS):
                    rows = pl.ds(k * CHUNK_ROWS, CHUNK_ROWS)
                    out.append(pltpu.make_async_remote_copy(
                        src_ref=pb_ref.at[2 * px + py, rows], dst_ref=lb_ref.at[slab_of(j), rows],
                        send_sem=send_sems.at[len(out)], recv_sem=recv_sems.at[len(out)], device_id=(px, py, c),
                        device_id_type=MESH))
                out.append(pltpu.make_async_remote_copy(
                    src_ref=ps_ref, dst_ref=ls_ref.at[slab_of(j)], send_sem=send_sems.at[len(out)],
                    recv_sem=recv_sems.at[len(out)], device_id=(px, py, c), device_id_type=MESH))
            return out

        sends = copies(lambda j: chip)
        for cp in sends:
            cp.start()
        for cp in copies(lambda j: 2 * peers[j][0] + peers[j][1]):
            cp.wait_recv()
        for cp in sends:
            cp.wait_send()

    n = 3 * (N_CHUNKS + 1)
    return pl.pallas_call(
        body, name="chip_exchange", in_specs=[ANY, ANY], out_specs=[ANY, ANY],
        out_shape=[SDS((4, HALF_ROWS, D), BF16), SDS((4, n_small, D), F32)],
        scratch_shapes=[pltpu.SemaphoreType.DMA((n,)), pltpu.SemaphoreType.DMA((n,))])(pb, ps)


def _chip_sum(own, land):
    def body(own_ref, l_ref, o_ref):
        chip = 2 * lax.axis_index("x") + lax.axis_index("y")
        acc = jnp.where(chip == 0, own_ref[...], l_ref[0].astype(F32))
        for s in range(1, 4):
            acc = acc + jnp.where(chip == s, own_ref[...], l_ref[s].astype(F32))
        o_ref[...] = acc

    return pl.pallas_call(
        body, name="chip_sum", grid=(N_CHUNKS,),
        in_specs=[pl.BlockSpec((CHUNK_ROWS, D), lambda i: (i, 0)), pl.BlockSpec((4, CHUNK_ROWS, D), lambda i: (0, i, 0))],
        out_specs=pl.BlockSpec((CHUNK_ROWS, D), lambda i: (i, 0)), out_shape=SDS((HALF_ROWS, D), F32),
        compiler_params=_params(1, VMEM_LIMIT))(own, land)


def _sum4(own, land):
    def body(own_ref, l_ref, o_ref):
        chip = 2 * lax.axis_index("x") + lax.axis_index("y")
        acc = jnp.where(chip == 0, own_ref[...], l_ref[0])
        for s in range(1, 4):
            acc = acc + jnp.where(chip == s, own_ref[...], l_ref[s])
        o_ref[...] = acc

    return pl.pallas_call(body, name="sum_small", out_shape=SDS(own.shape, own.dtype),
                          compiler_params=_params(0, VMEM_LIMIT))(own, land)


def _share_half(r):
    def body(r_ref, o_ref, send_sems, recv_sems):
        x, y, c = _place()
        copies = [pltpu.make_async_remote_copy(
            src_ref=r_ref.at[pl.ds(k * CHUNK_ROWS, CHUNK_ROWS)], dst_ref=o_ref.at[pl.ds(k * CHUNK_ROWS, CHUNK_ROWS)],
            send_sem=send_sems.at[k], recv_sem=recv_sems.at[k], device_id=(x, y, 1 - c), device_id_type=MESH)
            for k in range(N_CHUNKS)]
        for cp in copies:
            cp.start()
        for cp in copies:
            cp.wait_recv()
        for cp in copies:
            cp.wait_send()

    return pl.pallas_call(
        body, name="share_half", in_specs=[ANY], out_specs=ANY, out_shape=SDS((HALF_ROWS, D), F32),
        scratch_shapes=[pltpu.SemaphoreType.DMA((N_CHUNKS,)), pltpu.SemaphoreType.DMA((N_CHUNKS,))])(r)


C_ROWS = 1312
_REDUCE_CHUNK = {512: 256, 2560: 640}
W_CLASSES = {"in0": (512, 256), "out0": (192, 192), "kv": (128, 128)}


def _chunk_list(specs):
    return [(k, r, chunk) for k, (half, chunk) in enumerate(specs) for r in range(0, half, chunk)]


def _gather_classes(arrs, specs, vec):
    n = len(arrs)
    chunks = _chunk_list(specs)
    nc = len(chunks)

    def body(*refs):
        ins, vec_ref, outs, ov_ref = refs[:n], refs[n], refs[n + 1:2 * n + 1], refs[2 * n + 1]
        ici_send, ici_recv, d2d_send, d2d_recv, vec_send, vec_recv = refs[2 * n + 2:]
        x, y, c = _place()
        chip = 2 * x + y
        peers = [(1 - x, y), (x, 1 - y), (1 - x, 1 - y)]

        def rows(ci, half):
            k, r, cnt = chunks[ci]
            return k, pl.ds(half * specs[k][0] + r, cnt)

        def over_ici(j, ci, slab):
            px, py = peers[j]
            k, rs = rows(ci, c)
            return pltpu.make_async_remote_copy(
                src_ref=ins[k].at[rs], dst_ref=outs[k].at[slab, rs], send_sem=ici_send.at[j * nc + ci],
                recv_sem=ici_recv.at[j * nc + ci], device_id=(px, py, c), device_id_type=MESH)

        def over_d2d(j, ci, half):
            px, py = peers[j]
            k, rs = rows(ci, half)
            where = outs[k].at[2 * px + py, rs]
            return pltpu.make_async_remote_copy(
                src_ref=where, dst_ref=where, send_sem=d2d_send.at[j * nc + ci], recv_sem=d2d_recv.at[j * nc + ci],
                device_id=(x, y, 1 - c), device_id_type=MESH)

        def small(j, slab):
            px, py = peers[j]
            return pltpu.make_async_remote_copy(
                src_ref=vec_ref, dst_ref=ov_ref.at[slab], send_sem=vec_send.at[j], recv_sem=vec_recv.at[j],
                device_id=(px, py, c), device_id_type=MESH)

        sends = [small(j, chip) for j in range(3)] + [over_ici(j, ci, chip) for ci in range(nc) for j in range(3)]
        for cp in sends:
            cp.start()
        forwards = []
        for ci in range(nc):
            for j, (px, py) in enumerate(peers):
                over_ici(j, ci, 2 * px + py).wait_recv()
                forwards.append(over_d2d(j, ci, c))
                forwards[-1].start()
        for ci in range(nc):
            for j in range(3):
                over_d2d(j, ci, 1 - c).wait_recv()
        for j, (px, py) in enumerate(peers):
            small(j, 2 * px + py).wait_recv()
        for cp in sends + forwards:
            cp.wait_send()

    dma = pltpu.SemaphoreType.DMA
    return pl.pallas_call(
        body, name="gather_weights", in_specs=[ANY] * (n + 1), out_specs=[ANY] * (n + 1),
        out_shape=[SDS((4,) + a.shape, a.dtype) for a in arrs] + [SDS((4,) + vec.shape, vec.dtype)],
        scratch_shapes=[dma((3 * nc,)), dma((3 * nc,)), dma((3 * nc,)), dma((3 * nc,)), dma((3,)), dma((3,))])(*arrs, vec)


def _swap_classes(grads, specs, small):
    n = len(grads)
    chunks = _chunk_list(specs)

    def body(*refs):
        ins, s_ref, outs, os_ref, send_sems, recv_sems = refs[:n], refs[n], refs[n + 1:2 * n + 1], *refs[2 * n + 1:]
        x, y, c = _place()
        copies = []
        for s in range(4):
            for k, r, cnt in chunks:
                copies.append(pltpu.make_async_remote_copy(
                    src_ref=ins[k].at[s, 1 - c, pl.ds(r, cnt)], dst_ref=outs[k].at[s, pl.ds(r, cnt)],
                    send_sem=send_sems.at[len(copies)], recv_sem=recv_sems.at[len(copies)],
                    device_id=(x, y, 1 - c), device_id_type=MESH))
        copies.append(pltpu.make_async_remote_copy(
            src_ref=s_ref, dst_ref=os_ref, send_sem=send_sems.at[len(copies)], recv_sem=recv_sems.at[len(copies)],
            device_id=(x, y, 1 - c), device_id_type=MESH))
        for cp in copies:
            cp.start()
        for cp in copies:
            cp.wait_recv()
        for cp in copies:
            cp.wait_send()

    m = 4 * len(chunks) + 1
    return pl.pallas_call(
        body, name="swap_halves", in_specs=[ANY] * (n + 1), out_specs=[ANY] * (n + 1),
        out_shape=[SDS((4, g.shape[2], g.shape[3]), F32) for g in grads] + [SDS(small.shape, F32)],
        scratch_shapes=[pltpu.SemaphoreType.DMA((m,)), pltpu.SemaphoreType.DMA((m,))])(*grads, small)


def _pair_sum_class(g, other, chunk, name):
    _, _, half, w = g.shape

    def body(g_ref, o_ref, pb_ref, own_ref):
        x, y, c = _place()
        v = jnp.where(c == 0, g_ref[0], g_ref[1]) + o_ref[...]
        pb_ref[...] = v.astype(BF16)

        @pl.when(pl.program_id(1) == 2 * x + y)
        def _():
            own_ref[...] = v

    return pl.pallas_call(
        body, name=name, grid=(half // chunk, 4),
        in_specs=[pl.BlockSpec((None, 2, chunk, w), lambda i, s: (s, 0, i, 0)),
                  pl.BlockSpec((None, chunk, w), lambda i, s: (s, i, 0))],
        out_specs=[pl.BlockSpec((None, chunk, w), lambda i, s: (s, i, 0)), pl.BlockSpec((chunk, w), lambda i, s: (i, 0))],
        out_shape=[SDS((4, half, w), BF16), SDS((half, w), F32)],
        compiler_params=_params(2, VMEM_LIMIT))(g, other)


def _exchange_classes(pbs, specs, ps):
    n = len(pbs)
    chunks = _chunk_list(specs)
    per_peer = len(chunks) + 1

    def body(*refs):
        ins, ps_ref, outs, ls_ref, send_sems, recv_sems = refs[:n], refs[n], refs[n + 1:2 * n + 1], *refs[2 * n + 1:]
        x, y, c = _place()
        chip = 2 * x + y
        peers = [(1 - x, y), (x, 1 - y), (1 - x, 1 - y)]

        def copies(slab_of):
            out = []
            for j, (px, py) in enumerate(peers):
                for k, r, cnt in chunks:
                    out.append(pltpu.make_async_remote_copy(
                        src_ref=ins[k].at[2 * px + py, pl.ds(r, cnt)], dst_ref=outs[k].at[slab_of(j), pl.ds(r, cnt)],
                        send_sem=send_sems.at[len(out)], recv_sem=recv_sems.at[len(out)], device_id=(px, py, c),
                        device_id_type=MESH))
                out.append(pltpu.make_async_remote_copy(
                    src_ref=ps_ref, dst_ref=ls_ref.at[slab_of(j)], send_sem=send_sems.at[len(out)],
                    recv_sem=recv_sems.at[len(out)], device_id=(px, py, c), device_id_type=MESH))
            return out

        sends = copies(lambda j: chip)
        for cp in sends:
            cp.start()
        for cp in copies(lambda j: 2 * peers[j][0] + peers[j][1]):
            cp.wait_recv()
        for cp in sends:
            cp.wait_send()

    m = 3 * per_peer
    return pl.pallas_call(
        body, name="chip_exchange", in_specs=[ANY] * (n + 1), out_specs=[ANY] * (n + 1),
        out_shape=[SDS(p.shape, BF16) for p in pbs] + [SDS((4,) + ps.shape, F32)],
        scratch_shapes=[pltpu.SemaphoreType.DMA((m,)), pltpu.SemaphoreType.DMA((m,))])(*pbs, ps)


def _chip_sum_class(own, land, chunk, name):
    half, w = own.shape

    def body(own_ref, l_ref, o_ref):
        chip = 2 * lax.axis_index("x") + lax.axis_index("y")
        acc = jnp.where(chip == 0, own_ref[...], l_ref[0].astype(F32))
        for s in range(1, 4):
            acc = acc + jnp.where(chip == s, own_ref[...], l_ref[s].astype(F32))
        o_ref[...] = acc

    return pl.pallas_call(
        body, name=name, grid=(half // chunk,),
        in_specs=[pl.BlockSpec((chunk, w), lambda i: (i, 0)), pl.BlockSpec((4, chunk, w), lambda i: (0, i, 0))],
        out_specs=pl.BlockSpec((chunk, w), lambda i: (i, 0)), out_shape=SDS((half, w), F32),
        compiler_params=_params(1, VMEM_LIMIT))(own, land)


def _share_classes(rs, specs):
    n = len(rs)
    chunks = _chunk_list(specs)

    def body(*refs):
        ins, outs, send_sems, recv_sems = refs[:n], refs[n:2 * n], *refs[2 * n:]
        x, y, c = _place()
        copies = [pltpu.make_async_remote_copy(
            src_ref=ins[k].at[pl.ds(r, cnt)], dst_ref=outs[k].at[pl.ds(r, cnt)], send_sem=send_sems.at[i],
            recv_sem=recv_sems.at[i], device_id=(x, y, 1 - c), device_id_type=MESH)
            for i, (k, r, cnt) in enumerate(chunks)]
        for cp in copies:
            cp.start()
        for cp in copies:
            cp.wait_recv()
        for cp in copies:
            cp.wait_send()

    m = len(chunks)
    return pl.pallas_call(
        body, name="share_half", in_specs=[ANY] * n, out_specs=[ANY] * n, out_shape=[SDS(r.shape, F32) for r in rs],
        scratch_shapes=[pltpu.SemaphoreType.DMA((m,)), pltpu.SemaphoreType.DMA((m,))])(*rs)


_HBM = pl.BlockSpec(memory_space=pltpu.HBM)
_SEM = pl.BlockSpec(memory_space=pltpu.SEMAPHORE)
_EFFECT = pltpu.SideEffectType.DATAFLOW_SIDE_EFFECTING


def _chip_peers():
    x, y, c = _place()
    return [(1 - x, y, c), (x, 1 - y, c), (1 - x, 1 - y, c)]


def _send_shard_start(v, name):
    def body(v_ref, land_ref, send_sems, recv_sems, v_thru, land_thru, token):
        x, y, c = _place()
        for j, peer in enumerate(_chip_peers()):
            pltpu.make_async_remote_copy(src_ref=v_ref, dst_ref=land_ref.at[2 * x + y], send_sem=send_sems.at[j],
                                         recv_sem=recv_sems.at[j], device_id=peer, device_id_type=MESH).start()
        token[...] = jnp.zeros_like(token)

    land_shape = (4,) + v.shape
    return pl.pallas_call(
        body, name=name,
        out_shape=(pltpu.SemaphoreType.DMA((3,)), pltpu.SemaphoreType.DMA((3,)), pltpu.HBM(v.shape, v.dtype),
                   pltpu.HBM(land_shape, v.dtype), SDS((8, 128), F32)),
        in_specs=(_HBM, _HBM), out_specs=(_SEM, _SEM, _HBM, _HBM, pl.BlockSpec(memory_space=pltpu.VMEM)),
        input_output_aliases={0: 2, 1: 3}, compiler_params=pltpu.CompilerParams(has_side_effects=_EFFECT),
    )(pltpu.with_memory_space_constraint(v, pltpu.HBM),
      pltpu.with_memory_space_constraint(lax.empty(land_shape, v.dtype), pltpu.HBM))


def _xor_peer(r):
    x, y, c = _place()
    return (1 - x if (r >> 2) & 1 else x, 1 - y if (r >> 1) & 1 else y, 1 - c if r & 1 else c)


def _send_pieces_start(parts, name):
    n = len(parts)

    def body(*refs):
        ins, lands = refs[:n], refs[n:2 * n]
        send_sems, recv_sems = refs[2 * n:2 * n + 2]
        token = refs[-1]
        x, y, c = _place()
        for r in range(1, 8):
            px, py, pc = _xor_peer(r)
            for k in range(n):
                pltpu.make_async_remote_copy(
                    src_ref=ins[k].at[2 * px + py, pc], dst_ref=lands[k].at[4 * x + 2 * y + c],
                    send_sem=send_sems.at[(r - 1) * n + k], recv_sem=recv_sems.at[(r - 1) * n + k],
                    device_id=(px, py, pc), device_id_type=MESH).start()
        token[...] = jnp.zeros_like(token)

    land_shapes = [(8,) + p.shape[2:] for p in parts]
    hbm = [pltpu.HBM(p.shape, p.dtype) for p in parts] + [pltpu.HBM(s, p.dtype) for s, p in zip(land_shapes, parts)]
    operands = [pltpu.with_memory_space_constraint(p, pltpu.HBM) for p in parts]
    operands += [pltpu.with_memory_space_constraint(lax.empty(s, p.dtype), pltpu.HBM) for s, p in zip(land_shapes, parts)]
    return pl.pallas_call(
        body, name=name,
        out_shape=(pltpu.SemaphoreType.DMA((7 * n,)), pltpu.SemaphoreType.DMA((7 * n,)), *hbm, SDS((8, 128), F32)),
        in_specs=(_HBM,) * (2 * n), out_specs=(_SEM, _SEM) + (_HBM,) * (2 * n) + (pl.BlockSpec(memory_space=pltpu.VMEM),),
        input_output_aliases={i: 2 + i for i in range(2 * n)},
        compiler_params=pltpu.CompilerParams(has_side_effects=_EFFECT))(*operands)


def _send_pieces_wait(started, after, name):
    send_sems, recv_sems, *thru, _ = started
    n = len(thru) // 2

    def body(*refs):
        ins, lands = refs[:n], refs[n:2 * n]
        send_sems, recv_sems = refs[2 * n:2 * n + 2]
        for r in range(1, 8):
            px, py, pc = _xor_peer(r)
            for k in range(n):
                copy = pltpu.make_async_remote_copy(
                    src_ref=ins[k].at[2 * px + py, pc], dst_ref=lands[k].at[4 * px + 2 * py + pc],
                    send_sem=send_sems.at[(r - 1) * n + k], recv_sem=recv_sems.at[(r - 1) * n + k],
                    device_id=(px, py, pc), device_id_type=MESH)
                copy.wait_send()
                copy.wait_recv()

    return pl.pallas_call(
        body, name=name, out_shape=tuple(pltpu.HBM(t.shape, t.dtype) for t in thru),
        in_specs=(_HBM,) * (2 * n) + (_SEM, _SEM, pl.BlockSpec(memory_space=pl.ANY)), out_specs=(_HBM,) * (2 * n),
        input_output_aliases={i: i for i in range(2 * n)},
        compiler_params=pltpu.CompilerParams(has_side_effects=_EFFECT))(*thru, send_sems, recv_sems, after)[n:]


def _sum8_class(own, land, chunk, name):
    rows, w = own.shape

    def body(own_ref, l_ref, o_ref):
        x, y, c = _place()
        me = 4 * x + 2 * y + c
        acc = jnp.where(me == 0, own_ref[...], l_ref[0].astype(F32))
        for d in range(1, 8):
            acc = acc + jnp.where(me == d, own_ref[...], l_ref[d].astype(F32))
        o_ref[...] = acc

    return pl.pallas_call(
        body, name=name, grid=(rows // chunk,),
        in_specs=[pl.BlockSpec((chunk, w), lambda i: (i, 0)), pl.BlockSpec((8, chunk, w), lambda i: (0, i, 0))],
        out_specs=pl.BlockSpec((chunk, w), lambda i: (i, 0)), out_shape=SDS((rows, w), F32),
        compiler_params=_params(1, VMEM_LIMIT))(own, land)


def _send_shard_wait(send_sems, recv_sems, v_thru, land_thru, after, name):
    def body(v_ref, land_ref, send_sems, recv_sems, after_ref, v_dead, got_ref):
        for j, (px, py, pc) in enumerate(_chip_peers()):
            copy = pltpu.make_async_remote_copy(src_ref=v_ref, dst_ref=land_ref.at[2 * px + py], send_sem=send_sems.at[j],
                                                recv_sem=recv_sems.at[j], device_id=(px, py, pc), device_id_type=MESH)
            copy.wait_send()
            copy.wait_recv()

    return pl.pallas_call(
        body, name=name,
        out_shape=(pltpu.HBM(v_thru.shape, v_thru.dtype), pltpu.HBM(land_thru.shape, land_thru.dtype)),
        in_specs=(_HBM, _HBM, _SEM, _SEM, pl.BlockSpec(memory_space=pl.ANY)), out_specs=(_HBM, _HBM),
        input_output_aliases={0: 0, 1: 1}, compiler_params=pltpu.CompilerParams(has_side_effects=_EFFECT),
    )(v_thru, land_thru, send_sems, recv_sems, after)[1]


_SMALL = ["mem_norm_w", "norm_pre", "norm_post", "a_ln_w", "a_ln_b", "a_w_s", "a_b_s", "b_conv_w", "c_conv_w",
          "c_a_log", "c_dt_bias", "c_o_norm_w"]
_SMALL_SHAPES = {"mem_norm_w": (D,), "norm_pre": (4, D), "norm_post": (4, D), "a_ln_w": (2, D), "a_ln_b": (2, D),
                 "a_w_s": (2, 8, HD, HD), "a_b_s": (2, 8, HD), "b_conv_w": (1, 3, D), "c_conv_w": (1, 4, 3 * D),
                 "c_a_log": (1, 8), "c_dt_bias": (1, 8), "c_o_norm_w": (1, HD)}
_SHARDED_SMALL = {"a_ln_w": D // 4, "a_ln_b": D // 4, "b_conv_w": D // 4, "c_conv_w": 3 * D // 4}
_ROWS = [2048, 1280, 1284, 1536, 256]
_BIG_ROWS = sum(_ROWS)
_BIG_PAD = 6416
_SMALL_ROWS = 288


def _size(shape):
    n = 1
    for d in shape:
        n *= d
    return n


def kernel(x, mem, mem_norm_w, w_mem_kv, norm_pre, norm_post, w_out, a_w_in, a_ln_w, a_ln_b, a_w_s, a_b_s, b_w_in, b_conv_w, c_w_in, c_conv_w, c_a_log, c_dt_bias, c_o_norm_w, loss_target, m_mem_norm_w, m_w_mem_kv, m_norm_pre, m_norm_post, m_w_out, m_a_w_in, m_a_ln_w, m_a_ln_b, m_a_w_s, m_a_b_s, m_b_w_in, m_b_conv_w, m_c_w_in, m_c_conv_w, m_c_a_log, m_c_dt_bias, m_c_o_norm_w, v_mem_norm_w, v_w_mem_kv, v_norm_pre, v_norm_post, v_w_out, v_a_w_in, v_a_ln_w, v_a_ln_b, v_a_w_s, v_a_b_s, v_b_w_in, v_b_conv_w, v_c_w_in, v_c_conv_w, v_c_a_log, v_c_dt_bias, v_c_o_norm_w):
    names = ["mem_norm_w", "w_mem_kv", "norm_pre", "norm_post", "w_out", "a_w_in", "a_ln_w", "a_ln_b", "a_w_s", "a_b_s",
             "b_w_in", "b_conv_w", "c_w_in", "c_conv_w", "c_a_log", "c_dt_bias", "c_o_norm_w"]
    w = dict(zip(names, [mem_norm_w, w_mem_kv, norm_pre, norm_post, w_out, a_w_in, a_ln_w, a_ln_b, a_w_s, a_b_s, b_w_in,
                         b_conv_w, c_w_in, c_conv_w, c_a_log, c_dt_bias, c_o_norm_w]))
    m = dict(zip(names, [m_mem_norm_w, m_w_mem_kv, m_norm_pre, m_norm_post, m_w_out, m_a_w_in, m_a_ln_w, m_a_ln_b, m_a_w_s,
                         m_a_b_s, m_b_w_in, m_b_conv_w, m_c_w_in, m_c_conv_w, m_c_a_log, m_c_dt_bias, m_c_o_norm_w]))
    v = dict(zip(names, [v_mem_norm_w, v_w_mem_kv, v_norm_pre, v_norm_post, v_w_out, v_a_w_in, v_a_ln_w, v_a_ln_b, v_a_w_s,
                         v_a_b_s, v_b_w_in, v_b_conv_w, v_c_w_in, v_c_conv_w, v_c_a_log, v_c_dt_bias, v_c_o_norm_w]))
    chip = 2 * lax.axis_index("x") + lax.axis_index("y")

    def rows_of_ct(a):
        return a[0].T

    def with_mine(gathered, own):
        return lax.dynamic_update_slice(gathered, own[None], (chip,) + (0,) * own.ndim)

    first = [a_w_in[0].astype(BF16), w_out[0].astype(BF16), w_mem_kv.astype(BF16)]
    vec = jnp.concatenate([a_ln_w.reshape(-1), a_ln_b.reshape(-1), b_conv_w.reshape(-1), c_conv_w.reshape(-1)])
    vec = jnp.pad(vec, (0, 8 * D - vec.shape[0])).reshape(8, D)
    *gathered, gvec = _gather_classes(first, [W_CLASSES[k] for k in W_CLASSES], vec)
    ga0, go0, gkv = [with_mine(g_, a) for g_, a in zip(gathered, first)]
    gvec = with_mine(gvec, vec)
    gv = gvec.reshape(4, 8 * D)
    later = {("in", 1): b_w_in[0], ("in", 2): jnp.pad(rows_of_ct(c_w_in), ((0, C_ROWS - 1284), (0, 0))),
             ("in", 3): a_w_in[1], ("out", 1): w_out[1], ("out", 2): w_out[2], ("out", 3): w_out[3]}
    later = {k: (a + 0.0 * gkv[0, 0, 0].astype(F32)).astype(BF16) for k, a in later.items()}
    sent = {k: _send_shard_start(a, f"send_w_{k[0]}_{k[1]}") for k, a in later.items()}
    started = sum(s[4][0, 0] for s in sent.values())

    def arrived(k, after):
        return with_mine(_send_shard_wait(*sent[k][:4], after, f"wait_w_{k[0]}_{k[1]}"), later[k])

    def blocks(i, after):
        if i == 0:
            return [ga0[0], ga0[1]], [ga0[2], ga0[3]], go0.reshape(D_CAT, D)
        got, wo = arrived(("in", i), after), arrived(("out", i), after).reshape(D_CAT, D)
        if i == 1:
            return [got[0], got[1], got[2][:, :512]], [got[2][:, 512:], got[3]], wo
        if i == 3:
            return [got[0], got[1]], [got[2], got[3]], wo
        fct = got[:, :1284].reshape(5136, D)
        c_ab = jnp.concatenate([fct[3 * D:3 * D + 16], jnp.zeros((AB_PAD - 16, D), BF16)], axis=0)
        return [fct[:3 * D], c_ab], [fct[3 * D + 16:]], wo
    sm = {"mem_norm_w": mem_norm_w, "norm_pre": norm_pre + started, "norm_post": norm_post, "a_w_s": a_w_s, "a_b_s": a_b_s,
          "c_a_log": c_a_log, "c_dt_bias": c_dt_bias, "c_o_norm_w": c_o_norm_w,
          "a_ln_w": gv[:, 0:512].reshape(4, 2, 256).transpose(1, 0, 2).reshape(2, D),
          "a_ln_b": gv[:, 512:1024].reshape(4, 2, 256).transpose(1, 0, 2).reshape(2, D),
          "b_conv_w": gv[:, 1024:1792].reshape(4, 1, 3, 256).transpose(1, 2, 0, 3).reshape(1, 3, D),
          "c_conv_w": gv[:, 1792:4864].reshape(4, 1, 4, 768).transpose(1, 2, 0, 3).reshape(1, 4, 3 * D)}
    wts = {"wkv": gkv.reshape(D, 2 * D_XA), "blocks": blocks}

    def layer_grads(i, g):
        if i % 3 == 2:
            gct = jnp.concatenate([g["wm"][i][:3 * D + 16], g["wg"][i]], axis=0).reshape(4, 1284, D)
            w_in = jnp.pad(gct, ((0, 0), (0, C_ROWS - 1284), (0, 0)))
        else:
            w_in = jnp.concatenate([g["wm"][i], g["wg"][i]], axis=0).reshape(4, -1, GRAD_TILE[i % 3])
        out = {f"in{i}": w_in, f"out{i}": g["wo"][i].reshape(4, 384, D)}
        return {k: a.reshape(4, 2, a.shape[1] // 2, a.shape[2]) for k, a in out.items()}

    pending = {}

    def layer_done(i, g):
        if i == 0:
            return 0.0
        halves = layer_grads(i, g)
        started = _send_pieces_start([h.astype(BF16) for h in halves.values()], f"send_grads_{i}")
        pending[i] = (started, halves)
        return started[-1][0, 0]

    wts["layer_done"] = layer_done

    loss, dx, g = _local_step(x[0], mem[0], loss_target[0], wts, sm)
    loss = lax.psum(loss, ("x", "y", "c"))

    core = lax.axis_index("c")
    mine, specs = {}, {}
    for i in (3, 2, 1):
        started, halves = pending[i]
        lands = _send_pieces_wait(started, dx, f"wait_grads_{i}")
        for (k, h), land in zip(halves.items(), lands):
            own = lax.dynamic_index_in_dim(lax.dynamic_index_in_dim(h, chip, 0, False), core, 0, False)
            specs[k] = (own.shape[0], _REDUCE_CHUNK.get(own.shape[0], own.shape[0]))
            mine[k] = _sum8_class(own, land, specs[k][1], f"sum8_{k}")

    first = layer_grads(0, g)
    first["kv"] = g["wkv"].reshape(4, 2, 128, D)
    first_specs = [(h.shape[2], _REDUCE_CHUNK.get(h.shape[2], h.shape[2])) for h in first.values()]
    halves = list(first.values())
    gs = {"mem_norm_w": g["mem_norm_w"], "norm_pre": jnp.concatenate(g["norm_pre"]),
          "norm_post": jnp.concatenate(g["norm_post"])}
    for n in _SMALL[3:]:
        gs[n] = jnp.stack([g[n][j] for j in sorted(g[n])])
    flat = jnp.concatenate([gs[n].reshape(-1) for n in _SMALL])
    small = jnp.pad(flat, (0, _SMALL_ROWS * D - flat.shape[0])).reshape(_SMALL_ROWS, D)
    *others, other_small = _swap_classes(halves, first_specs, small)
    pairs = [_pair_sum_class(h, o, s[1], f"pair_sum_{k}") for k, h, o, s in zip(first, halves, others, first_specs)]
    pair_small = _add(small, other_small, "pair_sum_small")
    *lands, land_small = _exchange_classes([p[0] for p in pairs], first_specs, pair_small)
    for k, p, land, s in zip(first, pairs, lands, first_specs):
        mine[k], specs[k] = _chip_sum_class(p[1], land, s[1], f"chip_sum_{k}"), s
    theirs = _share_classes(list(mine.values()), [specs[k] for k in mine])
    south = core == 0
    sh = {k: jnp.concatenate([jnp.where(south, a, b), jnp.where(south, b, a)], axis=0)
          for (k, a), b in zip(mine.items(), theirs)}
    grads = {"a_w_in": jnp.stack([sh["in0"], sh["in3"]]),
             "b_w_in": sh["in1"].reshape(5, D, 256).transpose(1, 0, 2).reshape(b_w_in.shape),
             "c_w_in": sh["in2"][:1284], "w_out": jnp.stack([sh[f"out{i}"] for i in range(DEPTH)]),
             "w_mem_kv": sh["kv"]}
    flat = _sum4(pair_small, land_small).reshape(-1)
    off = 0
    for n in _SMALL:
        shape = _SMALL_SHAPES[n]
        full = flat[off:off + _size(shape)].reshape(shape)
        off += _size(shape)
        if n in _SHARDED_SMALL:
            full = lax.dynamic_slice_in_dim(full, chip * _SHARDED_SMALL[n], _SHARDED_SMALL[n], axis=len(shape) - 1)
        grads[n] = full

    delta, new_m, new_v = {}, {}, {}
    for n in names:
        shape = w[n].shape
        if n == "c_w_in":
            d_, m_, v_ = _adamw(rows_of_ct(w[n]), grads[n], rows_of_ct(m[n]), rows_of_ct(v[n]), f"adamw_{n}")
            delta[n], new_m[n], new_v[n], grads[n] = d_.T[None], m_.T[None], v_.T[None], grads[n].T[None]
            continue
        view = (1, shape[0]) if len(shape) == 1 else (_size(shape[:-1]), shape[-1])
        d_, m_, v_ = _adamw(w[n].reshape(view), grads[n].reshape(view), m[n].reshape(view), v[n].reshape(view),
                            f"adamw_{n}")
        delta[n], new_m[n], new_v[n] = d_.reshape(shape), m_.reshape(shape), v_.reshape(shape)
    return (loss, dx[None], *[grads[n].reshape(w[n].shape) for n in names], *[delta[n] for n in names],
            *[new_m[n] for n in names], *[new_v[n] for n in names])
```

```python
import functools

import jax
import jax.numpy as jnp
from jax import lax
from jax.experimental import pallas as pl
from jax.experimental.pallas import tpu as pltpu

F32 = jnp.float32
BF16 = jnp.bfloat16
HI = lax.Precision.HIGHEST
MESH = pl.DeviceIdType.MESH
SDS = jax.ShapeDtypeStruct

D = 1024
D_XA = 512
D_CAT = 1536
N_MEM = 256
HD = 128
DEPTH = 4
EPS = 1e-6
TT = 512
DN_C = 64
DN_TB = 256
HALO = 8
AB_PAD = 128
VMEM_LIMIT = 56 * 1024 * 1024
GRAD_TILE = {0: 1024, 1: 256}

ADAM_LR, ADAM_B1, ADAM_B2, ADAM_EPS, ADAM_WD, ADAM_STEP = 0.001, 0.9, 0.999, 1e-08, 0.01, 10


def _params(n_grid, vmem=None):
    return pltpu.CompilerParams(dimension_semantics=("arbitrary",) * n_grid, vmem_limit_bytes=vmem)


def _rms(x, w):
    return x * lax.rsqrt(jnp.mean(x * x, axis=-1, keepdims=True) + EPS) * w


def _dot_nn(a, b):
    return jnp.dot(a.astype(BF16), b.astype(BF16), preferred_element_type=F32)


def _dot_nt(a, b):
    return lax.dot_general(a.astype(BF16), b.astype(BF16), (((1,), (1,)), ((), ())), preferred_element_type=F32)


def _dot_tn(a, b):
    return lax.dot_general(a.astype(BF16), b.astype(BF16), (((0,), (0,)), ((), ())), preferred_element_type=F32)


@jax.custom_vjp
def mm(a, b):
    return _dot_nn(a, b)


mm.defvjp(lambda a, b: (_dot_nn(a, b), (a, b)), lambda r, g: (_dot_nt(g, r[1]), _dot_tn(r[0], g)))


@jax.custom_vjp
def mm_nt(a, b):
    return _dot_nt(a, b)


mm_nt.defvjp(lambda a, b: (_dot_nt(a, b), (a, b)), lambda r, g: (_dot_nn(g, r[1]), _dot_tn(g, r[0])))


def _row_spec(width, tile=TT):
    return pl.BlockSpec((tile, width), lambda i: (i, 0))


def _full_spec(shape):
    return pl.BlockSpec(shape, lambda *_: (0,) * len(shape))


def _widths(blocks, transposed):
    return [b.shape[0 if transposed else 1] for b in blocks]


def _inproj_fwd(x, nw, mix, gate, transposed, name):
    T, nm = x.shape[0], len(mix)
    M, G = sum(_widths(mix, transposed)), sum(_widths(gate, transposed))

    def body(x_ref, nw_ref, *refs):
        blocks, (pm_ref, pg_ref, h_ref) = refs[:-3], refs[-3:]
        h = _rms(x_ref[...], nw_ref[...]).astype(BF16)
        h_ref[...] = h
        for p_ref, group in ((pm_ref, blocks[:nm]), (pg_ref, blocks[nm:])):
            off = 0
            for w_ref in group:
                w = w_ref.shape[0 if transposed else 1]
                p_ref[:, off:off + w] = _dot_nt(h, w_ref[...]) if transposed else _dot_nn(h, w_ref[...])
                off += w

    return pl.pallas_call(
        body, name=name, grid=(T // TT,),
        in_specs=[_row_spec(D), _full_spec((1, D))] + [_full_spec(b.shape) for b in mix + gate],
        out_specs=[_row_spec(M), _row_spec(G), _row_spec(D)],
        out_shape=[SDS((T, M), F32), SDS((T, G), F32), SDS((T, D), BF16)],
        compiler_params=_params(1, VMEM_LIMIT))(x, nw, *mix, *gate)


def _inproj_bwd(dpm, dpg, x, nw, mix, gate, transposed, dxc, name):
    T, nm = x.shape[0], len(mix)
    M, G = sum(_widths(mix, transposed)), sum(_widths(gate, transposed))

    def body(dpm_ref, dpg_ref, x_ref, nw_ref, *refs):
        blocks, (dxc_ref, dx_ref, dnw_ref) = refs[:-3], refs[-3:]
        dh = None
        for dp_ref, group in ((dpm_ref, blocks[:nm]), (dpg_ref, blocks[nm:])):
            off = 0
            for w_ref in group:
                w = w_ref.shape[0 if transposed else 1]
                dp = dp_ref[:, off:off + w]
                part = _dot_nn(dp, w_ref[...]) if transposed else _dot_nt(dp, w_ref[...])
                dh = part if dh is None else dh + part
                off += w
        _, vjp = jax.vjp(_rms, x_ref[...], nw_ref[...])
        dxr, dnw = vjp(dh)
        dx_ref[...] = dxc_ref[...] + dxr

        @pl.when(pl.program_id(0) == 0)
        def _():
            dnw_ref[...] = jnp.zeros_like(dnw_ref)
        dnw_ref[...] += dnw

    return pl.pallas_call(
        body, name=name, grid=(T // TT,),
        in_specs=[_row_spec(M), _row_spec(G), _row_spec(D), _full_spec((1, D))]
        + [_full_spec(b.shape) for b in mix + gate] + [_row_spec(D)],
        out_specs=[_row_spec(D), _full_spec((1, D))],
        out_shape=[SDS((T, D), F32), SDS((1, D), F32)],
        compiler_params=_params(1, VMEM_LIMIT))(dpm, dpg, x, nw, *mix, *gate, dxc)


def _matmul_tn(a, b, name, sub=None):
    T, K = a.shape
    N = b.shape[1]
    tn = 1024 if N % 1024 == 0 else (640 if N % 640 == 0 else N)
    tt = min(1024, T)
    n_sub = 1 if sub is None else tn // sub

    def body(a_ref, b_ref, o_ref):
        @pl.when(pl.program_id(1) == 0)
        def _():
            o_ref[...] = jnp.zeros_like(o_ref)
        res = _dot_tn(a_ref[...], b_ref[...])
        if sub is None:
            o_ref[...] += res
        else:
            for i in range(n_sub):
                o_ref[i] += res[:, i * sub:(i + 1) * sub]

    if sub is None:
        out_spec, out_shape = pl.BlockSpec((K, tn), lambda j, t: (0, j)), SDS((K, N), F32)
    else:
        out_spec, out_shape = pl.BlockSpec((n_sub, K, sub), lambda j, t: (j, 0, 0)), SDS((N // sub, K, sub), F32)
    return pl.pallas_call(
        body, name=name, grid=(N // tn, T // tt),
        in_specs=[pl.BlockSpec((tt, K), lambda j, t: (t, 0)), pl.BlockSpec((tt, tn), lambda j, t: (t, j))],
        out_specs=out_spec, out_shape=out_shape,
        compiler_params=_params(2, VMEM_LIMIT))(a, b)


def _memkv_fn(mem, w, wkv):
    return mm(_rms(mem, w), wkv)


def _memkv_fwd(mem, w, wkv):
    def body(mem_ref, w_ref, wkv_ref, kv_ref):
        kv_ref[...] = _memkv_fn(mem_ref[...], w_ref[...], wkv_ref[...])

    return pl.pallas_call(body, name="memkv_fwd", out_shape=SDS((N_MEM, 2 * D_XA), F32),
                          compiler_params=_params(0, VMEM_LIMIT))(mem, w, wkv)


def _memkv_bwd(mem, w, wkv, dkv):
    def body(mem_ref, w_ref, wkv_ref, dkv_ref, dw_ref, dwkv_ref):
        _, vjp = jax.vjp(functools.partial(_memkv_fn, mem_ref[...]), w_ref[...], wkv_ref[...].astype(F32))
        dw, dwkv = vjp(dkv_ref[...])
        dw_ref[...] = dw
        dwkv_ref[...] = dwkv

    return pl.pallas_call(body, name="memkv_bwd", out_shape=[SDS((1, D), F32), SDS((D, 2 * D_XA), F32)],
                          compiler_params=_params(0, VMEM_LIMIT))(mem, w, wkv, dkv)


def _attn_gate(ymix, qx, z, *kvs):
    outs = []
    for j in range(4):
        s = mm_nt(qx[:, j * HD:(j + 1) * HD], kvs[j]) * (HD ** -0.5)
        e = jnp.exp(s - lax.stop_gradient(jnp.max(s, axis=-1, keepdims=True)))
        outs.append(mm(e / jnp.sum(e, axis=-1, keepdims=True), kvs[4 + j]))
    return jnp.concatenate([ymix] + outs, axis=1) * jax.nn.silu(z)


def _kv_blocks(kv_ref):
    return [kv_ref[:, j * HD:(j + 1) * HD] for j in range(8)]


def _ag_fwd(ymix, pg, kv, name):
    T = ymix.shape[0]

    def body(ymix_ref, pg_ref, kv_ref, ycat_ref):
        ycat_ref[...] = _attn_gate(ymix_ref[...], pg_ref[:, :D_XA], pg_ref[:, D_XA:], *_kv_blocks(kv_ref)).astype(BF16)

    return pl.pallas_call(
        body, name=name, grid=(T // TT,),
        in_specs=[_row_spec(D), _row_spec(D_XA + D_CAT), _full_spec((N_MEM, 2 * D_XA))],
        out_specs=_row_spec(D_CAT), out_shape=SDS((T, D_CAT), BF16),
        compiler_params=_params(1, VMEM_LIMIT))(ymix, pg, kv)


def _ag_bwd(dycat, ymix, pg, kv, dkv_in, name):
    T = ymix.shape[0]

    def body(dycat_ref, ymix_ref, pg_ref, kv_ref, dkvin_ref, dymix_ref, dpg_ref, dkv_ref):
        _, vjp = jax.vjp(_attn_gate, ymix_ref[...], pg_ref[:, :D_XA], pg_ref[:, D_XA:], *_kv_blocks(kv_ref))
        g = vjp(dycat_ref[...])
        dymix_ref[...] = g[0]
        dpg_ref[:, :D_XA] = g[1].astype(BF16)
        dpg_ref[:, D_XA:] = g[2].astype(BF16)

        @pl.when(pl.program_id(0) == 0)
        def _():
            dkv_ref[...] = dkvin_ref[...]
        for j in range(8):
            dkv_ref[:, j * HD:(j + 1) * HD] += g[3 + j]

    return pl.pallas_call(
        body, name=name, grid=(T // TT,),
        in_specs=[_row_spec(D_CAT), _row_spec(D), _row_spec(D_XA + D_CAT), _full_spec((N_MEM, 2 * D_XA)),
                  _full_spec((N_MEM, 2 * D_XA))],
        out_specs=[_row_spec(D), _row_spec(D_XA + D_CAT), _full_spec((N_MEM, 2 * D_XA))],
        out_shape=[SDS((T, D), F32), SDS((T, D_XA + D_CAT), BF16), SDS((N_MEM, 2 * D_XA), F32)],
        compiler_params=_params(1, VMEM_LIMIT))(dycat, ymix, pg, kv, dkv_in)


def _outproj_fwd(ycat, wo, x, nw, name):
    T = x.shape[0]

    def body(ycat_ref, wo_ref, x_ref, nw_ref, o_ref, xn_ref):
        o = jnp.dot(ycat_ref[...], wo_ref[...], preferred_element_type=F32)
        o_ref[...] = o
        xn_ref[...] = x_ref[...] + _rms(o, nw_ref[...])

    return pl.pallas_call(
        body, name=name, grid=(T // TT,),
        in_specs=[_row_spec(D_CAT), _full_spec((D_CAT, D)), _row_spec(D), _full_spec((1, D))],
        out_specs=[_row_spec(D), _row_spec(D)], out_shape=[SDS((T, D), F32), SDS((T, D), F32)],
        compiler_params=_params(1, VMEM_LIMIT))(ycat, wo, x, nw)


def _outproj_bwd(dxo, o, nw, wo, name):
    T = dxo.shape[0]

    def body(dxo_ref, o_ref, nw_ref, wo_ref, dycat_ref, dobf_ref, dnw_ref):
        _, vjp = jax.vjp(_rms, o_ref[...], nw_ref[...])
        do, dnw = vjp(dxo_ref[...])
        dobf = do.astype(BF16)
        dobf_ref[...] = dobf
        dycat_ref[...] = _dot_nt(dobf, wo_ref[...])

        @pl.when(pl.program_id(0) == 0)
        def _():
            dnw_ref[...] = jnp.zeros_like(dnw_ref)
        dnw_ref[...] += dnw

    return pl.pallas_call(
        body, name=name, grid=(T // TT,),
        in_specs=[_row_spec(D), _row_spec(D), _full_spec((1, D)), _full_spec((D_CAT, D))],
        out_specs=[_row_spec(D_CAT), _row_spec(D), _full_spec((1, D))],
        out_shape=[SDS((T, D_CAT), F32), SDS((T, D), BF16), SDS((1, D), F32)],
        compiler_params=_params(1, VMEM_LIMIT))(dxo, o, nw, wo)


def _loss_head(xl, target):
    T = xl.shape[0]

    def body(x_ref, t_ref, loss_ref, dx_ref):
        err = x_ref[...] - t_ref[...]
        dx_ref[...] = err * (1.0 / D)

        @pl.when(pl.program_id(0) == 0)
        def _():
            loss_ref[...] = jnp.zeros_like(loss_ref)
        part = jnp.sum(jnp.sum(err * err, axis=1, keepdims=True), axis=0, keepdims=True) * (0.5 / D)
        loss_ref[...] += jnp.broadcast_to(part, loss_ref.shape)

    return pl.pallas_call(
        body, name="loss_head", grid=(T // TT,),
        in_specs=[_row_spec(D), _row_spec(D)],
        out_specs=[_full_spec((8, 128)), _row_spec(D)], out_shape=[SDS((8, 128), F32), SDS((T, D), F32)],
        compiler_params=_params(1))(xl, target)


def _gmlp_pre(u, v, lnw, lnb):
    vg = jax.nn.gelu(v)
    xc = vg - jnp.mean(vg, axis=-1, keepdims=True)
    vl = xc * lax.rsqrt(jnp.mean(xc * xc, axis=-1, keepdims=True) + EPS) * lnw + lnb
    return jax.nn.gelu(u), vl


def _tril(n, strict=False):
    r = lax.broadcasted_iota(jnp.int32, (n, n), 0)
    c = lax.broadcasted_iota(jnp.int32, (n, n), 1)
    return (r > c) if strict else (r >= c)


def _gmlp_fwd(pm, lnw, lnb, ws, bs3, name):
    T = pm.shape[0]

    def body(pm_ref, lnw_ref, lnb_ref, ws_ref, bs_ref, y_ref):
        ug, vl = _gmlp_pre(pm_ref[:, :D], pm_ref[:, D:], lnw_ref[...], lnb_ref[...])
        mask = _tril(HD)
        for g in range(8):
            w = jnp.where(mask, ws_ref[g], 0.0)
            for c in range(TT // HD):
                rows, cols = slice(c * HD, (c + 1) * HD), slice(g * HD, (g + 1) * HD)
                y_ref[rows, cols] = ug[rows, cols] * (_dot_nn(w, vl[rows, cols]) + bs_ref[g])

    return pl.pallas_call(
        body, name=name, grid=(T // TT,),
        in_specs=[_row_spec(2 * D), _full_spec((1, D)), _full_spec((1, D)), _full_spec((8, HD, HD)),
                  _full_spec((8, HD, HD))],
        out_specs=_row_spec(D), out_shape=SDS((T, D), F32),
        compiler_params=_params(1, VMEM_LIMIT))(pm, lnw, lnb, ws, bs3)


def _gmlp_bwd(dy, pm, lnw, lnb, ws, bs3, name):
    T = pm.shape[0]
    n_t = T // TT

    def body(dy_ref, pm_ref, lnw_ref, lnb_ref, ws_ref, bs_ref, dpm_ref, dlnw_ref, dlnb_ref, dws_ref, dbs_ref,
             dug_scr, dvl_scr, dbs_scr):
        i = pl.program_id(0)

        @pl.when(i == 0)
        def _():
            dlnw_ref[...] = jnp.zeros_like(dlnw_ref)
            dlnb_ref[...] = jnp.zeros_like(dlnb_ref)
            dws_ref[...] = jnp.zeros_like(dws_ref)
            dbs_scr[...] = jnp.zeros_like(dbs_scr)

        (ug, vl), vjp = jax.vjp(_gmlp_pre, pm_ref[:, :D], pm_ref[:, D:], lnw_ref[...], lnb_ref[...])
        mask = _tril(HD)
        for g in range(8):
            w = jnp.where(mask, ws_ref[g], 0.0)
            dw = jnp.zeros((HD, HD), F32)
            db = jnp.zeros((HD, HD), F32)
            for c in range(TT // HD):
                rows, cols = slice(c * HD, (c + 1) * HD), slice(g * HD, (g + 1) * HD)
                dyb, vlb = dy_ref[rows, cols], vl[rows, cols]
                sp = _dot_nn(w, vlb) + bs_ref[g]
                dsp = dyb * ug[rows, cols]
                dug_scr[rows, cols] = dyb * sp
                dvl_scr[rows, cols] = _dot_tn(w, dsp)
                dw += _dot_nt(dsp, vlb)
                db += dsp
            dws_ref[g] += jnp.where(mask, dw, 0.0)
            dbs_scr[g] += db
        du, dv, dlnw, dlnb = vjp((dug_scr[...], dvl_scr[...]))
        dpm_ref[:, :D] = du.astype(BF16)
        dpm_ref[:, D:] = dv.astype(BF16)
        dlnw_ref[...] += dlnw
        dlnb_ref[...] += dlnb

        @pl.when(i == n_t - 1)
        def _():
            for g in range(8):
                dbs_ref[g] = jnp.broadcast_to(jnp.sum(dbs_scr[g], axis=1, keepdims=True), (HD, HD))

    return pl.pallas_call(
        body, name=name, grid=(n_t,),
        in_specs=[_row_spec(D), _row_spec(2 * D), _full_spec((1, D)), _full_spec((1, D)), _full_spec((8, HD, HD)),
                  _full_spec((8, HD, HD))],
        out_specs=[_row_spec(2 * D), _full_spec((1, D)), _full_spec((1, D)), _full_spec((8, HD, HD)),
                   _full_spec((8, HD, HD))],
        out_shape=[SDS((T, 2 * D), BF16), SDS((1, D), F32), SDS((1, D), F32), SDS((8, HD, HD), F32),
                   SDS((8, HD, HD), F32)],
        scratch_shapes=[pltpu.VMEM((TT, D), F32), pltpu.VMEM((TT, D), F32), pltpu.VMEM((8, HD, HD), F32)],
        compiler_params=_params(1, VMEM_LIMIT))(dy, pm, lnw, lnb, ws, bs3)


def _prev_spec(width, T):
    return pl.BlockSpec((HALO, width), lambda i: (jnp.maximum(i * (TT // HALO) - 1, 0), 0))


def _next_spec(width, T):
    return pl.BlockSpec((HALO, width), lambda i: (jnp.minimum((i + 1) * (TT // HALO), T // HALO - 1), 0))


def _rows_before(ext, j):
    return ext[HALO:] if j == 0 else pltpu.roll(ext, j, 0)[HALO:]


def _rows_after(ext, j):
    n = ext.shape[0]
    return ext[:n - HALO] if j == 0 else pltpu.roll(ext, n - j, 0)[:n - HALO]


def _conv_apply(ext_s, w):
    K = w.shape[0]
    y = _rows_before(ext_s, K - 1) * w[0:1]
    for k in range(1, K):
        y = y + _rows_before(ext_s, K - 1 - k) * w[k:k + 1]
    return y


def _conv_grads(ext_s, ext_dy, w):
    K = w.shape[0]
    dy = ext_dy[:ext_dy.shape[0] - HALO]
    ds = _rows_after(ext_dy, K - 1) * w[0:1]
    dws = [jnp.sum(dy * _rows_before(ext_s, K - 1), axis=0, keepdims=True)]
    for k in range(1, K):
        ds = ds + _rows_after(ext_dy, K - 1 - k) * w[k:k + 1]
        dws.append(jnp.sum(dy * _rows_before(ext_s, K - 1 - k), axis=0, keepdims=True))
    return ds, jnp.concatenate(dws, axis=0)


def _sconv_fwd(pm, w, name):
    T = pm.shape[0]

    def body(pm_ref, prev_ref, w_ref, y_ref):
        s = pm_ref[:, D:2 * D] * pm_ref[:, 2 * D:]
        sp = jnp.where(pl.program_id(0) > 0, prev_ref[:, D:2 * D] * prev_ref[:, 2 * D:], 0.0)
        y_ref[...] = pm_ref[:, :D] * _conv_apply(jnp.concatenate([sp, s], axis=0), w_ref[...])

    return pl.pallas_call(
        body, name=name, grid=(T // TT,),
        in_specs=[_row_spec(3 * D), _prev_spec(3 * D, T), _full_spec((3, D))],
        out_specs=_row_spec(D), out_shape=SDS((T, D), F32),
        compiler_params=_params(1, VMEM_LIMIT))(pm, pm, w)


def _sconv_bwd(dy, pm, w, name):
    T = pm.shape[0]
    n_t = T // TT

    def body(dy_ref, dyn_ref, pm_ref, prev_ref, next_ref, w_ref, dpm_ref, dw_ref):
        i = pl.program_id(0)
        bg, cg, hv = pm_ref[:, :D], pm_ref[:, D:2 * D], pm_ref[:, 2 * D:]
        sp = jnp.where(i > 0, prev_ref[:, D:2 * D] * prev_ref[:, 2 * D:], 0.0)
        ext_s = jnp.concatenate([sp, cg * hv], axis=0)
        dyv = dy_ref[...]
        dcn = jnp.where(i < n_t - 1, dyn_ref[...] * next_ref[:, :D], 0.0)
        ds, dw = _conv_grads(ext_s, jnp.concatenate([dyv * bg, dcn], axis=0), w_ref[...])
        dpm_ref[:, :D] = (dyv * _conv_apply(ext_s, w_ref[...])).astype(BF16)
        dpm_ref[:, D:2 * D] = (ds * hv).astype(BF16)
        dpm_ref[:, 2 * D:] = (ds * cg).astype(BF16)

        @pl.when(i == 0)
        def _():
            dw_ref[...] = jnp.zeros_like(dw_ref)
        dw_ref[...] += dw

    return pl.pallas_call(
        body, name=name, grid=(n_t,),
        in_specs=[_row_spec(D), _next_spec(D, T), _row_spec(3 * D), _prev_spec(3 * D, T), _next_spec(3 * D, T),
                  _full_spec((3, D))],
        out_specs=[_row_spec(3 * D), _full_spec((3, D))],
        out_shape=[SDS((T, 3 * D), BF16), SDS((3, D), F32)],
        compiler_params=_params(1, VMEM_LIMIT))(dy, dy, pm, pm, pm, w)


def _dnconv_fwd(pm, w, name):
    T = pm.shape[0]

    def body(pm_ref, prev_ref, w_ref, c_ref):
        sp = jnp.where(pl.program_id(0) > 0, prev_ref[...], 0.0)
        c_ref[...] = _conv_apply(jnp.concatenate([sp, pm_ref[...]], axis=0), w_ref[...])

    return pl.pallas_call(
        body, name=name, grid=(T // TT,),
        in_specs=[_row_spec(3 * D), _prev_spec(3 * D, T), _full_spec((4, 3 * D))],
        out_specs=_row_spec(3 * D), out_shape=SDS((T, 3 * D), F32),
        compiler_params=_params(1, VMEM_LIMIT))(pm, pm, w)


def _dnconv_bwd(dcq, dck, dcv, dab, pm, w, name):
    T = pm.shape[0]
    n_t = T // TT

    def body(dq_ref, dk_ref, dv_ref, dqn_ref, dkn_ref, dvn_ref, dab_ref, pm_ref, prev_ref, w_ref, dpm_ref, dw_ref):
        i = pl.program_id(0)
        sp = jnp.where(i > 0, prev_ref[...], 0.0)
        ext_s = jnp.concatenate([sp, pm_ref[...]], axis=0)
        own = jnp.concatenate([dq_ref[...], dk_ref[...], dv_ref[...]], axis=1)
        nxt = jnp.where(i < n_t - 1, jnp.concatenate([dqn_ref[...], dkn_ref[...], dvn_ref[...]], axis=1), 0.0)
        ds, dw = _conv_grads(ext_s, jnp.concatenate([own, nxt], axis=0), w_ref[...])
        dpm_ref[:, :3 * D] = ds.astype(BF16)
        dpm_ref[:, 3 * D:] = dab_ref[...].astype(BF16)

        @pl.when(i == 0)
        def _():
            dw_ref[...] = jnp.zeros_like(dw_ref)
        dw_ref[...] += dw

    return pl.pallas_call(
        body, name=name, grid=(n_t,),
        in_specs=[_row_spec(D), _row_spec(D), _row_spec(D), _next_spec(D, T), _next_spec(D, T), _next_spec(D, T),
                  _row_spec(AB_PAD), _row_spec(3 * D), _prev_spec(3 * D, T), _full_spec((4, 3 * D))],
        out_specs=[_row_spec(3 * D + AB_PAD), _full_spec((4, 3 * D))],
        out_shape=[SDS((T, 3 * D + AB_PAD), BF16), SDS((4, 3 * D), F32)],
        compiler_params=_params(1, VMEM_LIMIT))(dcq, dck, dcv, dcq, dck, dcv, dab, pm, pm, w)


def _l2n(x):
    return x * lax.rsqrt(jnp.sum(x * x, axis=-1, keepdims=True) + EPS)


def _softplus(x):
    return jnp.maximum(x, 0.0) + jnp.log1p(jnp.exp(-jnp.abs(x)))


_BNN = (((2,), (1,)), ((0,), (0,)))
_BNT = (((2,), (2,)), ((0,), (0,)))
_BTN = (((1,), (1,)), ((0,), (0,)))


def _bdot(a, b, dims):
    return lax.dot_general(a.astype(BF16), b.astype(BF16), dims, preferred_element_type=F32)


def _bdot3(a, b, dims):
    ah, bh = a.astype(BF16), b.astype(BF16)
    al, bl = (a - ah.astype(F32)).astype(BF16), (b - bh.astype(F32)).astype(BF16)
    d = functools.partial(lax.dot_general, dimension_numbers=dims, preferred_element_type=F32)
    return d(ah, bh) + (d(ah, bl) + d(al, bh))


def _bdot_hi(a, b, dims):
    return lax.dot_general(a, b, dims, precision=HI, preferred_element_type=F32)


def _batched_matmuls(dot):
    @jax.custom_vjp
    def nn(a, b):
        return dot(a, b, _BNN)

    @jax.custom_vjp
    def nt(a, b):
        return dot(a, b, _BNT)

    @jax.custom_vjp
    def tn(a, b):
        return dot(a, b, _BTN)

    nn.defvjp(lambda a, b: (dot(a, b, _BNN), (a, b)), lambda r, g: (dot(g, r[1], _BNT), dot(r[0], g, _BTN)))
    nt.defvjp(lambda a, b: (dot(a, b, _BNT), (a, b)), lambda r, g: (dot(g, r[1], _BNN), dot(g, r[0], _BTN)))
    tn.defvjp(lambda a, b: (dot(a, b, _BTN), (a, b)), lambda r, g: (dot(r[1], g, _BNT), dot(r[0], g, _BNN)))
    return nn, nt, tn


bmm, bmm_nt, bmm_tn = _batched_matmuls(_bdot)
bmm_hi, _, _ = _batched_matmuls(_bdot_hi)

@jax.custom_vjp
def _neumann_inverse(n):
    C = n.shape[1]
    eye = lax.broadcasted_iota(jnp.int32, n.shape, 1) == lax.broadcasted_iota(jnp.int32, n.shape, 2)
    t = eye.astype(F32) + n
    for _ in range(5):
        n = _bdot3(n, n, _BNN)
        t = t + _bdot3(t, n, _BNN)
    return t


def _neumann_inverse_fwd(n):
    t = _neumann_inverse(n)
    return t, t


def _neumann_inverse_bwd(t, g):
    return (_bdot3(_bdot3(t, g, _BTN), t, _BNT),)


_neumann_inverse.defvjp(_neumann_inverse_fwd, _neumann_inverse_bwd)


@jax.custom_vjp
def _saved_inverse(n, t):
    return t


_saved_inverse.defvjp(lambda n, t: (t, t), lambda t, g: (_bdot3(_bdot3(t, g, _BTN), t, _BNT), jnp.zeros_like(t)))

DN_NCH = DN_TB // DN_C
DN_NH = 4


def _decay_terms(ab, alog, dtb, first_head, n_heads):
    C = DN_C
    lane = lax.broadcasted_iota(jnp.int32, ab.shape, 1)
    g_all = (-jnp.exp(alog) * _softplus(ab + dtb)).reshape(DN_NCH, C, HD)
    beta_all = jax.nn.sigmoid(ab)
    r = lax.broadcasted_iota(jnp.int32, (DN_NCH, C, C), 1)
    c = lax.broadcasted_iota(jnp.int32, (DN_NCH, C, C), 2)
    gc_all = bmm_hi((r >= c).astype(F32), g_all)
    gc_rows = [gc_all[i].T for i in range(DN_NCH)]
    lane3 = lax.broadcasted_iota(jnp.int32, (DN_NCH, C, HD), 2)
    row = lax.broadcasted_iota(jnp.int32, (HD, C), 0)
    ones = jnp.ones((1, HD), F32)
    gcs, gjs, betas = [], [], []
    for i in range(n_heads):
        h = first_head + i
        gcs.append(jnp.sum(jnp.where(lane3 == h, gc_all, 0.0), axis=2, keepdims=True) * ones)
        gjs.append(jnp.concatenate(
            [jnp.broadcast_to(jnp.sum(jnp.where(row == h, t, 0.0), axis=0, keepdims=True), (C, C))[None] for t in gc_rows],
            axis=0))
        beta = jnp.sum(jnp.where(lane == 8 + h, beta_all, 0.0), axis=1, keepdims=True) * ones
        betas.append(beta.reshape(DN_NCH, C, HD))
    return jnp.concatenate(gcs, axis=0), jnp.concatenate(gjs, axis=0), jnp.concatenate(betas, axis=0)


def _dn_prep(cq, ck, cv, gcum, gj, bb, t_saved=None):
    B, C = cq.shape[0], DN_C
    q = _l2n(jax.nn.silu(cq)) * (HD ** -0.5)
    k = _l2n(jax.nn.silu(ck))
    v = jax.nn.silu(cv)
    r = lax.broadcasted_iota(jnp.int32, (B, C, C), 1)
    c = lax.broadcasted_iota(jnp.int32, (B, C, C), 2)
    incl, strict = r >= c, r > c
    decay = jnp.where(incl, jnp.exp(jnp.where(incl, gcum[:, :, :C] - gj, 0.0)), 0.0)
    kb = k * bb
    n_mat = -jnp.where(strict, bmm_nt(kb, k) * decay, 0.0)
    t_mat = _neumann_inverse(n_mat) if t_saved is None else _saved_inverse(n_mat, t_saved)
    eg = jnp.exp(gcum)
    glast = gcum[:, C - 1:C, :]
    return (bmm(t_mat, v * bb), bmm(t_mat, kb * eg), bmm_nt(q, k) * decay, q * eg, k * jnp.exp(glast - gcum),
            jnp.exp(glast), t_mat)


def _dn_scan_step(u, w, qk, qd, kd, egl, S, onw):
    v_new = u - bmm(w, S)
    o = bmm(qd, S) + bmm(qk, v_new)
    return _rms(o, onw), S * egl + bmm_tn(kd, v_new)


def _to_batch(ref, n_heads):
    return jnp.concatenate([ref[:, i * HD:(i + 1) * HD].astype(F32).reshape(DN_NCH, DN_C, HD) for i in range(n_heads)],
                           axis=0)


def _from_batch(ref, val, n_heads):
    for i in range(n_heads):
        ref[:, i * HD:(i + 1) * HD] = val[i * DN_NCH:(i + 1) * DN_NCH].reshape(DN_TB, HD).astype(ref.dtype)


def _prep_specs(T, rev):
    nb = T // DN_TB
    blk = (lambda n: nb - 1 - n) if rev else (lambda n: n)
    ng = 8 // DN_NH
    head = [pl.BlockSpec((DN_TB, DN_NH * HD), functools.partial(lambda n, h, off: (blk(n), off + h), off=ng * s))
            for s in range(3)]
    ab = pl.BlockSpec((DN_TB, AB_PAD), lambda n, h: (blk(n), 3 * D // AB_PAD))
    row = pl.BlockSpec((1, HD), lambda n, h: (0, 0))
    wide = pl.BlockSpec((DN_TB, DN_NH * HD), lambda n, h: (blk(n), h))
    qk = pl.BlockSpec((DN_NCH, DN_NH, DN_C, DN_C), lambda n, h: (blk(n), h, 0, 0))
    eg = pl.BlockSpec((DN_NCH, DN_NH, 1, HD), lambda n, h: (blk(n), h, 0, 0))
    return nb, ng, head, ab, row, wide, qk, eg


def _dn_prep_fwd(cpre, pm, alog, dtb, name):
    T = cpre.shape[0]
    nb, ng, head, ab, row, wide, qks, egs = _prep_specs(T, False)

    def body(cq_ref, ck_ref, cv_ref, ab_ref, alog_ref, dtb_ref, u_ref, w_ref, qk_ref, qd_ref, kd_ref, e_ref, t_ref):
        gcum, gj, bb = _decay_terms(ab_ref[...], alog_ref[...], dtb_ref[...], pl.program_id(1) * DN_NH, DN_NH)
        u, w, qk, qd, kd, egl, t_mat = _dn_prep(_to_batch(cq_ref, DN_NH), _to_batch(ck_ref, DN_NH),
                                                _to_batch(cv_ref, DN_NH), gcum, gj, bb)
        _from_batch(u_ref, u, DN_NH)
        _from_batch(w_ref, w, DN_NH)
        _from_batch(qd_ref, qd, DN_NH)
        _from_batch(kd_ref, kd, DN_NH)
        for i in range(DN_NH):
            qk_ref[:, i] = qk[i * DN_NCH:(i + 1) * DN_NCH].astype(BF16)
            e_ref[:, i] = egl[i * DN_NCH:(i + 1) * DN_NCH]
            t_ref[:, i] = t_mat[i * DN_NCH:(i + 1) * DN_NCH]

    return pl.pallas_call(
        body, name=name, grid=(nb, ng), in_specs=head + [ab, row, row],
        out_specs=[wide, wide, qks, wide, wide, egs, qks],
        out_shape=[SDS((T, D), F32), SDS((T, D), BF16), SDS((T // DN_C, 8, DN_C, DN_C), BF16), SDS((T, D), BF16),
                   SDS((T, D), BF16), SDS((T // DN_C, 8, 1, HD), F32), SDS((T // DN_C, 8, DN_C, DN_C), F32)],
        compiler_params=_params(2, VMEM_LIMIT))(cpre, cpre, cpre, pm, alog, dtb)


def _dn_prep_bwd(du, dw, dqk, dqd, dkd, degl, t_mat, cpre, pm, alog, dtb, name):
    T = cpre.shape[0]
    nb, ng, head, ab, row, wide, qks, egs = _prep_specs(T, True)

    def body(du_ref, dw_ref, dqk_ref, dqd_ref, dkd_ref, de_ref, t_ref, cq_ref, ck_ref, cv_ref, ab_ref, alog_ref,
             dtb_ref, dcq_ref, dck_ref, dcv_ref, dab_ref, dalog_ref, ddtb_ref):
        n, h = pl.program_id(0), pl.program_id(1)

        @pl.when((n == 0) & (h == 0))
        def _():
            dalog_ref[...] = jnp.zeros_like(dalog_ref)
            ddtb_ref[...] = jnp.zeros_like(ddtb_ref)

        @pl.when(h == 0)
        def _():
            dab_ref[...] = jnp.zeros_like(dab_ref)

        t_saved = jnp.concatenate([t_ref[:, i] for i in range(DN_NH)], axis=0)

        def fwd(cq, ck, cv, ab_v, alog_v, dtb_v):
            gcum, gj, bb = _decay_terms(ab_v, alog_v, dtb_v, h * DN_NH, DN_NH)
            return _dn_prep(cq, ck, cv, gcum, gj, bb, t_saved)[:6]

        _, vjp = jax.vjp(fwd, _to_batch(cq_ref, DN_NH), _to_batch(ck_ref, DN_NH), _to_batch(cv_ref, DN_NH), ab_ref[...],
                         alog_ref[...], dtb_ref[...])
        cot = (_to_batch(du_ref, DN_NH), _to_batch(dw_ref, DN_NH),
               jnp.concatenate([dqk_ref[:, i] for i in range(DN_NH)], axis=0), _to_batch(dqd_ref, DN_NH),
               _to_batch(dkd_ref, DN_NH), jnp.concatenate([de_ref[:, i] for i in range(DN_NH)], axis=0))
        dcq, dck, dcv, dab, dalog, ddtb = vjp(cot)
        _from_batch(dcq_ref, dcq, DN_NH)
        _from_batch(dck_ref, dck, DN_NH)
        _from_batch(dcv_ref, dcv, DN_NH)
        dab_ref[...] += dab
        dalog_ref[...] += dalog
        ddtb_ref[...] += ddtb

    dabspec = pl.BlockSpec((DN_TB, AB_PAD), lambda n, h: (nb - 1 - n, 0))
    return pl.pallas_call(
        body, name=name, grid=(nb, ng),
        in_specs=[wide, wide, qks, wide, wide, egs, qks] + head + [ab, row, row],
        out_specs=[wide, wide, wide, dabspec, row, row],
        out_shape=[SDS((T, D), F32)] * 3 + [SDS((T, AB_PAD), F32)] + [SDS((1, HD), F32)] * 2,
        compiler_params=_params(2, VMEM_LIMIT))(du, dw, dqk, dqd, dkd, degl, t_mat, cpre, cpre, cpre, pm, alog, dtb)


def _scan_specs(T, rev):
    nb = T // DN_TB
    blk = (lambda n: nb - 1 - n) if rev else (lambda n: n)
    wide = pl.BlockSpec((DN_TB, D), lambda n: (blk(n), 0))
    qk = pl.BlockSpec((DN_NCH, 8, DN_C, DN_C), lambda n: (blk(n), 0, 0, 0))
    eg = pl.BlockSpec((DN_NCH, 8, 1, HD), lambda n: (blk(n), 0, 0, 0))
    st = pl.BlockSpec((DN_NCH, 8, HD, HD), lambda n: (blk(n), 0, 0, 0))
    row = pl.BlockSpec((1, HD), lambda n: (0, 0))
    return nb, wide, qk, eg, st, row


def _heads_of(ref, rows):
    return jnp.concatenate([ref[rows, h * HD:(h + 1) * HD].astype(F32)[None] for h in range(8)], axis=0)


def _dn_scan_fwd(u, w, qk, qd, kd, egl, onw, name):
    T = u.shape[0]
    nb, wide, qks, egs, sts, row = _scan_specs(T, False)

    def body(u_ref, w_ref, qk_ref, qd_ref, kd_ref, e_ref, onw_ref, o_ref, st_ref, s_scr):
        @pl.when(pl.program_id(0) == 0)
        def _():
            s_scr[...] = jnp.zeros_like(s_scr)
        S = s_scr[...]
        for c in range(DN_NCH):
            rows = slice(c * DN_C, (c + 1) * DN_C)
            st_ref[c] = S
            o, S = _dn_scan_step(_heads_of(u_ref, rows), _heads_of(w_ref, rows), qk_ref[c].astype(F32),
                                 _heads_of(qd_ref, rows), _heads_of(kd_ref, rows), e_ref[c], S, onw_ref[...])
            for h in range(8):
                o_ref[rows, h * HD:(h + 1) * HD] = o[h]
        s_scr[...] = S

    return pl.pallas_call(
        body, name=name, grid=(nb,), in_specs=[wide, wide, qks, wide, wide, egs, row], out_specs=[wide, sts],
        out_shape=[SDS((T, D), F32), SDS((T // DN_C, 8, HD, HD), F32)],
        scratch_shapes=[pltpu.VMEM((8, HD, HD), F32)],
        compiler_params=_params(1, VMEM_LIMIT))(u, w, qk, qd, kd, egl, onw)


def _dn_scan_bwd(do, u, w, qk, qd, kd, egl, st, onw, name):
    T = u.shape[0]
    nb, wide, qks, egs, sts, row = _scan_specs(T, True)

    def body(do_ref, u_ref, w_ref, qk_ref, qd_ref, kd_ref, e_ref, st_ref, onw_ref,
             du_ref, dw_ref, dqk_ref, dqd_ref, dkd_ref, de_ref, donw_ref, ds_scr):
        @pl.when(pl.program_id(0) == 0)
        def _():
            ds_scr[...] = jnp.zeros_like(ds_scr)
            donw_ref[...] = jnp.zeros_like(donw_ref)
        dS = ds_scr[...]
        donw = jnp.zeros((1, HD), F32)
        for c in reversed(range(DN_NCH)):
            rows = slice(c * DN_C, (c + 1) * DN_C)
            _, vjp = jax.vjp(_dn_scan_step, _heads_of(u_ref, rows), _heads_of(w_ref, rows), qk_ref[c].astype(F32),
                             _heads_of(qd_ref, rows), _heads_of(kd_ref, rows), e_ref[c], st_ref[c], onw_ref[...])
            du, dw, dqk, dqd, dkd, de, dS, dn = vjp((_heads_of(do_ref, rows), dS))
            for h in range(8):
                cols = slice(h * HD, (h + 1) * HD)
                du_ref[rows, cols] = du[h]
                dw_ref[rows, cols] = dw[h]
                dqd_ref[rows, cols] = dqd[h]
                dkd_ref[rows, cols] = dkd[h]
            dqk_ref[c] = dqk
            de_ref[c] = de
            donw += dn
        ds_scr[...] = dS
        donw_ref[...] += donw

    return pl.pallas_call(
        body, name=name, grid=(nb,), in_specs=[wide, wide, wide, qks, wide, wide, egs, sts, row],
        out_specs=[wide, wide, qks, wide, wide, egs, row],
        out_shape=[SDS((T, D), F32), SDS((T, D), F32), SDS((T // DN_C, 8, DN_C, DN_C), F32), SDS((T, D), F32),
                   SDS((T, D), F32), SDS((T // DN_C, 8, 1, HD), F32), SDS((1, HD), F32)],
        scratch_shapes=[pltpu.VMEM((8, HD, HD), F32)],
        compiler_params=_params(1, VMEM_LIMIT))(do, u, w, qk, qd, kd, egl, st, onw)


def _adamw(w, g, m, v, name):
    R, C = w.shape
    tr = 256 if R % 256 == 0 and R > 256 else R
    tc = 256 if tr == R and R > 256 and C % 256 == 0 else C
    c1 = 1.0 - ADAM_B1 ** ADAM_STEP
    c2 = 1.0 - ADAM_B2 ** ADAM_STEP

    def body(w_ref, g_ref, m_ref, v_ref, d_ref, nm_ref, nv_ref):
        gv = g_ref[...]
        nm = ADAM_B1 * m_ref[...] + (1.0 - ADAM_B1) * gv
        nv = ADAM_B2 * v_ref[...] + (1.0 - ADAM_B2) * (gv * gv)
        nm_ref[...] = nm
        nv_ref[...] = nv
        d_ref[...] = -ADAM_LR * ((nm / c1) / (jnp.sqrt(nv / c2) + ADAM_EPS) + ADAM_WD * w_ref[...])

    spec = pl.BlockSpec((tr, tc), lambda i, j: (i, j))
    return pl.pallas_call(
        body, name=name, grid=(R // tr, C // tc), in_specs=[spec] * 4, out_specs=[spec] * 3,
        out_shape=[SDS((R, C), F32)] * 3, compiler_params=_params(2, VMEM_LIMIT))(w, g, m, v)


def _local_step(x, mem, target, wts, sm):
    kinds = [i % 3 for i in range(DEPTH)]
    mnw = sm["mem_norm_w"].reshape(1, D)
    kv = _memkv_fwd(mem, mnw, wts["wkv"])
    saved, blocks = [], []
    for i, kind in enumerate(kinds):
        j = i // 3
        npre = sm["norm_pre"][i].reshape(1, D)
        npost = sm["norm_post"][i].reshape(1, D)
        mix, gate, wo = wts["blocks"](i, x)
        blocks.append((mix, gate, wo))
        pm, pg, h = _inproj_fwd(x, npre, mix, gate, kind == 2, f"inproj_fwd_{i}")
        extra = None
        if kind == 0:
            bs3 = jnp.broadcast_to(sm["a_b_s"][j][:, :, None], (8, HD, HD))
            ymix = _gmlp_fwd(pm, sm["a_ln_w"][j].reshape(1, D), sm["a_ln_b"][j].reshape(1, D), sm["a_w_s"][j], bs3,
                             f"gmlp_fwd_{i}")
            extra = bs3
        elif kind == 1:
            ymix = _sconv_fwd(pm, sm["b_conv_w"][j], f"sconv_fwd_{i}")
        else:
            cpre = _dnconv_fwd(pm, sm["c_conv_w"][j], f"dnconv_fwd_{i}")
            alog = jnp.pad(sm["c_a_log"][j], (0, HD - 8)).reshape(1, HD)
            dtb = jnp.pad(sm["c_dt_bias"][j], (0, HD - 8)).reshape(1, HD)
            onw = sm["c_o_norm_w"][j].reshape(1, HD)
            *prep, t_mat = _dn_prep_fwd(cpre, pm, alog, dtb, f"dn_prep_fwd_{i}")
            ymix, st = _dn_scan_fwd(*prep, onw, f"dn_scan_fwd_{i}")
            extra = (cpre, prep, t_mat, st, alog, dtb, onw)
        ycat = _ag_fwd(ymix, pg, kv, f"ag_fwd_{i}")
        o, xn = _outproj_fwd(ycat, wo, x, npost, f"outproj_fwd_{i}")
        saved.append((x, h, pm, pg, ymix, ycat, o, extra))
        x = xn

    loss, dx = _loss_head(x, target)

    g = {"wm": [None] * DEPTH, "wg": [None] * DEPTH, "wo": [None] * DEPTH, "norm_pre": [None] * DEPTH,
         "norm_post": [None] * DEPTH}
    dkv = jnp.zeros((N_MEM, 2 * D_XA), F32)
    sent = 0.0
    for i in reversed(range(DEPTH)):
        kind, j = kinds[i], i // 3
        xi, h, pm, pg, ymix, ycat, o, extra = saved[i]
        npre = sm["norm_pre"][i].reshape(1, D)
        npost = sm["norm_post"][i].reshape(1, D) + sent
        dycat, dobf, g["norm_post"][i] = _outproj_bwd(dx, o, npost, blocks[i][2], f"outproj_bwd_{i}")
        g["wo"][i] = _matmul_tn(ycat, dobf, f"dwo_{i}")
        dymix, dpg, dkv = _ag_bwd(dycat, ymix, pg, kv, dkv, f"ag_bwd_{i}")
        if kind == 0:
            dpm, dlnw, dlnb, dws, dbs3 = _gmlp_bwd(dymix, pm, sm["a_ln_w"][j].reshape(1, D),
                                                   sm["a_ln_b"][j].reshape(1, D), sm["a_w_s"][j], extra,
                                                   f"gmlp_bwd_{i}")
            g.setdefault("a_ln_w", {})[j] = dlnw.reshape(D)
            g.setdefault("a_ln_b", {})[j] = dlnb.reshape(D)
            g.setdefault("a_w_s", {})[j] = dws
            g.setdefault("a_b_s", {})[j] = dbs3[:, :, 0]
        elif kind == 1:
            dpm, dcw = _sconv_bwd(dymix, pm, sm["b_conv_w"][j], f"sconv_bwd_{i}")
            g.setdefault("b_conv_w", {})[j] = dcw
        else:
            cpre, prep, t_mat, st, alog, dtb, onw = extra
            *dprep, donw = _dn_scan_bwd(dymix, *prep, st, onw, f"dn_scan_bwd_{i}")
            dcq, dck, dcv, dab, dalog, ddtb = _dn_prep_bwd(*dprep, t_mat, cpre, pm, alog, dtb, f"dn_prep_bwd_{i}")
            dpm, dcw = _dnconv_bwd(dcq, dck, dcv, dab, pm, sm["c_conv_w"][j], f"dnconv_bwd_{i}")
            g.setdefault("c_conv_w", {})[j] = dcw
            g.setdefault("c_a_log", {})[j] = dalog[0, :8]
            g.setdefault("c_dt_bias", {})[j] = ddtb[0, :8]
            g.setdefault("c_o_norm_w", {})[j] = donw[0]
        if kind == 2:
            g["wm"][i] = _matmul_tn(dpm, h, f"dwm_{i}")
            g["wg"][i] = _matmul_tn(dpg, h, f"dwg_{i}")
        else:
            g["wm"][i] = _matmul_tn(h, dpm, f"dwm_{i}", GRAD_TILE[kind])
            g["wg"][i] = _matmul_tn(h, dpg, f"dwg_{i}", GRAD_TILE[kind])
        sent = wts["layer_done"](i, g)
        dx, g["norm_pre"][i] = _inproj_bwd(dpm, dpg, xi, npre + sent, blocks[i][0], blocks[i][1], kind == 2, dx,
                                           f"inproj_bwd_{i}")
    g["mem_norm_w"], g["wkv"] = _memkv_bwd(mem, mnw, wts["wkv"], dkv)
    return loss[0, 0], dx, g


ANY = pl.BlockSpec(memory_space=pl.ANY)


def _place():
    return lax.axis_index("x"), lax.axis_index("y"), lax.axis_index("c")


def _add(a, b, name):
    def body(a_ref, b_ref, o_ref):
        o_ref[...] = a_ref[...] + b_ref[...]

    return pl.pallas_call(body, name=name, out_shape=SDS(a.shape, a.dtype), compiler_params=_params(0, VMEM_LIMIT))(a, b)


def _sum4(own, land):
    def body(own_ref, l_ref, o_ref):
        chip = 2 * lax.axis_index("x") + lax.axis_index("y")
        acc = jnp.where(chip == 0, own_ref[...], l_ref[0])
        for s in range(1, 4):
            acc = acc + jnp.where(chip == s, own_ref[...], l_ref[s])
        o_ref[...] = acc

    return pl.pallas_call(body, name="sum_small", out_shape=SDS(own.shape, own.dtype),
                          compiler_params=_params(0, VMEM_LIMIT))(own, land)


C_ROWS = 1312
_REDUCE_CHUNK = {512: 256, 2560: 640}
W_CLASSES = {"in0": (512, 256), "out0": (192, 192), "kv": (128, 128)}


def _chunk_list(specs):
    return [(k, r, chunk) for k, (half, chunk) in enumerate(specs) for r in range(0, half, chunk)]


def _gather_classes(arrs, specs, vec):
    n = len(arrs)
    chunks = _chunk_list(specs)
    nc = len(chunks)

    def body(*refs):
        ins, vec_ref, outs, ov_ref = refs[:n], refs[n], refs[n + 1:2 * n + 1], refs[2 * n + 1]
        ici_send, ici_recv, d2d_send, d2d_recv, vec_send, vec_recv = refs[2 * n + 2:]
        x, y, c = _place()
        chip = 2 * x + y
        peers = [(1 - x, y), (x, 1 - y), (1 - x, 1 - y)]

        def rows(ci, half):
            k, r, cnt = chunks[ci]
            return k, pl.ds(half * specs[k][0] + r, cnt)

        def over_ici(j, ci, slab):
            px, py = peers[j]
            k, rs = rows(ci, c)
            return pltpu.make_async_remote_copy(
                src_ref=ins[k].at[rs], dst_ref=outs[k].at[slab, rs], send_sem=ici_send.at[j * nc + ci],
                recv_sem=ici_recv.at[j * nc + ci], device_id=(px, py, c), device_id_type=MESH)

        def over_d2d(j, ci, half):
            px, py = peers[j]
            k, rs = rows(ci, half)
            where = outs[k].at[2 * px + py, rs]
            return pltpu.make_async_remote_copy(
                src_ref=where, dst_ref=where, send_sem=d2d_send.at[j * nc + ci], recv_sem=d2d_recv.at[j * nc + ci],
                device_id=(x, y, 1 - c), device_id_type=MESH)

        def small(j, slab):
            px, py = peers[j]
            return pltpu.make_async_remote_copy(
                src_ref=vec_ref, dst_ref=ov_ref.at[slab], send_sem=vec_send.at[j], recv_sem=vec_recv.at[j],
                device_id=(px, py, c), device_id_type=MESH)

        sends = [small(j, chip) for j in range(3)] + [over_ici(j, ci, chip) for ci in range(nc) for j in range(3)]
        for cp in sends:
            cp.start()
        forwards = []
        for ci in range(nc):
            for j, (px, py) in enumerate(peers):
                over_ici(j, ci, 2 * px + py).wait_recv()
                forwards.append(over_d2d(j, ci, c))
                forwards[-1].start()
        for ci in range(nc):
            for j in range(3):
                over_d2d(j, ci, 1 - c).wait_recv()
        for j, (px, py) in enumerate(peers):
            small(j, 2 * px + py).wait_recv()
        for cp in sends + forwards:
            cp.wait_send()

    dma = pltpu.SemaphoreType.DMA
    return pl.pallas_call(
        body, name="gather_weights", in_specs=[ANY] * (n + 1), out_specs=[ANY] * (n + 1),
        out_shape=[SDS((4,) + a.shape, a.dtype) for a in arrs] + [SDS((4,) + vec.shape, vec.dtype)],
        scratch_shapes=[dma((3 * nc,)), dma((3 * nc,)), dma((3 * nc,)), dma((3 * nc,)), dma((3,)), dma((3,))])(*arrs, vec)


def _swap_classes(grads, specs, small):
    n = len(grads)
    chunks = _chunk_list(specs)

    def body(*refs):
        ins, s_ref, outs, os_ref, send_sems, recv_sems = refs[:n], refs[n], refs[n + 1:2 * n + 1], *refs[2 * n + 1:]
        x, y, c = _place()
        copies = []
        for s in range(4):
            for k, r, cnt in chunks:
                copies.append(pltpu.make_async_remote_copy(
                    src_ref=ins[k].at[s, 1 - c, pl.ds(r, cnt)], dst_ref=outs[k].at[s, pl.ds(r, cnt)],
                    send_sem=send_sems.at[len(copies)], recv_sem=recv_sems.at[len(copies)],
                    device_id=(x, y, 1 - c), device_id_type=MESH))
        copies.append(pltpu.make_async_remote_copy(
            src_ref=s_ref, dst_ref=os_ref, send_sem=send_sems.at[len(copies)], recv_sem=recv_sems.at[len(copies)],
            device_id=(x, y, 1 - c), device_id_type=MESH))
        for cp in copies:
            cp.start()
        for cp in copies:
            cp.wait_recv()
        for cp in copies:
            cp.wait_send()

    m = 4 * len(chunks) + 1
    return pl.pallas_call(
        body, name="swap_halves", in_specs=[ANY] * (n + 1), out_specs=[ANY] * (n + 1),
        out_shape=[SDS((4, g.shape[2], g.shape[3]), F32) for g in grads] + [SDS(small.shape, F32)],
        scratch_shapes=[pltpu.SemaphoreType.DMA((m,)), pltpu.SemaphoreType.DMA((m,))])(*grads, small)


def _pair_sum_class(g, other, chunk, name):
    _, _, half, w = g.shape

    def body(g_ref, o_ref, pb_ref, own_ref):
        x, y, c = _place()
        v = jnp.where(c == 0, g_ref[0], g_ref[1]) + o_ref[...]
        pb_ref[...] = v.astype(BF16)

        @pl.when(pl.program_id(1) == 2 * x + y)
        def _():
            own_ref[...] = v

    return pl.pallas_call(
        body, name=name, grid=(half // chunk, 4),
        in_specs=[pl.BlockSpec((None, 2, chunk, w), lambda i, s: (s, 0, i, 0)),
                  pl.BlockSpec((None, chunk, w), lambda i, s: (s, i, 0))],
        out_specs=[pl.BlockSpec((None, chunk, w), lambda i, s: (s, i, 0)), pl.BlockSpec((chunk, w), lambda i, s: (i, 0))],
        out_shape=[SDS((4, half, w), BF16), SDS((half, w), F32)],
        compiler_params=_params(2, VMEM_LIMIT))(g, other)


def _exchange_classes(pbs, specs, ps):
    n = len(pbs)
    chunks = _chunk_list(specs)
    per_peer = len(chunks) + 1

    def body(*refs):
        ins, ps_ref, outs, ls_ref, send_sems, recv_sems = refs[:n], refs[n], refs[n + 1:2 * n + 1], *refs[2 * n + 1:]
        x, y, c = _place()
        chip = 2 * x + y
        peers = [(1 - x, y), (x, 1 - y), (1 - x, 1 - y)]

        def copies(slab_of):
            out = []
            for j, (px, py) in enumerate(peers):
                for k, r, cnt in chunks:
                    out.append(pltpu.make_async_remote_copy(
                        src_ref=ins[k].at[2 * px + py, pl.ds(r, cnt)], dst_ref=outs[k].at[slab_of(j), pl.ds(r, cnt)],
                        send_sem=send_sems.at[len(out)], recv_sem=recv_sems.at[len(out)], device_id=(px, py, c),
                        device_id_type=MESH))
                out.append(pltpu.make_async_remote_copy(
                    src_ref=ps_ref, dst_ref=ls_ref.at[slab_of(j)], send_sem=send_sems.at[len(out)],
                    recv_sem=recv_sems.at[len(out)], device_id=(px, py, c), device_id_type=MESH))
            return out

        sends = copies(lambda j: chip)
        for cp in sends:
            cp.start()
        for cp in copies(lambda j: 2 * peers[j][0] + peers[j][1]):
            cp.wait_recv()
        for cp in sends:
            cp.wait_send()

    m = 3 * per_peer
    return pl.pallas_call(
        body, name="chip_exchange", in_specs=[ANY] * (n + 1), out_specs=[ANY] * (n + 1),
        out_shape=[SDS(p.shape, BF16) for p in pbs] + [SDS((4,) + ps.shape, F32)],
        scratch_shapes=[pltpu.SemaphoreType.DMA((m,)), pltpu.SemaphoreType.DMA((m,))])(*pbs, ps)


def _chip_sum_class(own, land, chunk, name):
    half, w = own.shape

    def body(own_ref, l_ref, o_ref):
        chip = 2 * lax.axis_index("x") + lax.axis_index("y")
        acc = jnp.where(chip == 0, own_ref[...], l_ref[0].astype(F32))
        for s in range(1, 4):
            acc = acc + jnp.where(chip == s, own_ref[...], l_ref[s].astype(F32))
        o_ref[...] = acc

    return pl.pallas_call(
        body, name=name, grid=(half // chunk,),
        in_specs=[pl.BlockSpec((chunk, w), lambda i: (i, 0)), pl.BlockSpec((4, chunk, w), lambda i: (0, i, 0))],
        out_specs=pl.BlockSpec((chunk, w), lambda i: (i, 0)), out_shape=SDS((half, w), F32),
        compiler_params=_params(1, VMEM_LIMIT))(own, land)


def _share_classes(rs, specs):
    n = len(rs)
    chunks = _chunk_list(specs)

    def body(*refs):
        ins, outs, send_sems, recv_sems = refs[:n], refs[n:2 * n], *refs[2 * n:]
        x, y, c = _place()
        copies = [pltpu.make_async_remote_copy(
            src_ref=ins[k].at[pl.ds(r, cnt)], dst_ref=outs[k].at[pl.ds(r, cnt)], send_sem=send_sems.at[i],
            recv_sem=recv_sems.at[i], device_id=(x, y, 1 - c), device_id_type=MESH)
            for i, (k, r, cnt) in enumerate(chunks)]
        for cp in copies:
            cp.start()
        for cp in copies:
            cp.wait_recv()
        for cp in copies:
            cp.wait_send()

    m = len(chunks)
    return pl.pallas_call(
        body, name="share_half", in_specs=[ANY] * n, out_specs=[ANY] * n, out_shape=[SDS(r.shape, F32) for r in rs],
        scratch_shapes=[pltpu.SemaphoreType.DMA((m,)), pltpu.SemaphoreType.DMA((m,))])(*rs)


_HBM = pl.BlockSpec(memory_space=pltpu.HBM)
_SEM = pl.BlockSpec(memory_space=pltpu.SEMAPHORE)
_EFFECT = pltpu.SideEffectType.DATAFLOW_SIDE_EFFECTING


def _chip_peers():
    x, y, c = _place()
    return [(1 - x, y, c), (x, 1 - y, c), (1 - x, 1 - y, c)]


def _send_shard_start(v, name):
    def body(v_ref, land_ref, send_sems, recv_sems, v_thru, land_thru, token):
        x, y, c = _place()
        for j, peer in enumerate(_chip_peers()):
            pltpu.make_async_remote_copy(src_ref=v_ref, dst_ref=land_ref.at[2 * x + y], send_sem=send_sems.at[j],
                                         recv_sem=recv_sems.at[j], device_id=peer, device_id_type=MESH).start()
        token[...] = jnp.zeros_like(token)

    land_shape = (4,) + v.shape
    return pl.pallas_call(
        body, name=name,
        out_shape=(pltpu.SemaphoreType.DMA((3,)), pltpu.SemaphoreType.DMA((3,)), pltpu.HBM(v.shape, v.dtype),
                   pltpu.HBM(land_shape, v.dtype), SDS((8, 128), F32)),
        in_specs=(_HBM, _HBM), out_specs=(_SEM, _SEM, _HBM, _HBM, pl.BlockSpec(memory_space=pltpu.VMEM)),
        input_output_aliases={0: 2, 1: 3}, compiler_params=pltpu.CompilerParams(has_side_effects=_EFFECT),
    )(pltpu.with_memory_space_constraint(v, pltpu.HBM),
      pltpu.with_memory_space_constraint(lax.empty(land_shape, v.dtype), pltpu.HBM))


def _xor_peer(r):
    x, y, c = _place()
    return (1 - x if (r >> 2) & 1 else x, 1 - y if (r >> 1) & 1 else y, 1 - c if r & 1 else c)


def _send_pieces_start(parts, name):
    n = len(parts)

    def body(*refs):
        ins, lands = refs[:n], refs[n:2 * n]
        send_sems, recv_sems = refs[2 * n:2 * n + 2]
        token = refs[-1]
        x, y, c = _place()
        for r in range(1, 8):
            px, py, pc = _xor_peer(r)
            for k in range(n):
                pltpu.make_async_remote_copy(
                    src_ref=ins[k].at[2 * px + py, pc], dst_ref=lands[k].at[4 * x + 2 * y + c],
                    send_sem=send_sems.at[(r - 1) * n + k], recv_sem=recv_sems.at[(r - 1) * n + k],
                    device_id=(px, py, pc), device_id_type=MESH).start()
        token[...] = jnp.zeros_like(token)

    land_shapes = [(8,) + p.shape[2:] for p in parts]
    hbm = [pltpu.HBM(p.shape, p.dtype) for p in parts] + [pltpu.HBM(s, p.dtype) for s, p in zip(land_shapes, parts)]
    operands = [pltpu.with_memory_space_constraint(p, pltpu.HBM) for p in parts]
    operands += [pltpu.with_memory_space_constraint(lax.empty(s, p.dtype), pltpu.HBM) for s, p in zip(land_shapes, parts)]
    return pl.pallas_call(
        body, name=name,
        out_shape=(pltpu.SemaphoreType.DMA((7 * n,)), pltpu.SemaphoreType.DMA((7 * n,)), *hbm, SDS((8, 128), F32)),
        in_specs=(_HBM,) * (2 * n), out_specs=(_SEM, _SEM) + (_HBM,) * (2 * n) + (pl.BlockSpec(memory_space=pltpu.VMEM),),
        input_output_aliases={i: 2 + i for i in range(2 * n)},
        compiler_params=pltpu.CompilerParams(has_side_effects=_EFFECT))(*operands)


def _send_pieces_wait(started, after, name):
    send_sems, recv_sems, *thru, _ = started
    n = len(thru) // 2

    def body(*refs):
        ins, lands = refs[:n], refs[n:2 * n]
        send_sems, recv_sems = refs[2 * n:2 * n + 2]
        for r in range(1, 8):
            px, py, pc = _xor_peer(r)
            for k in range(n):
                copy = pltpu.make_async_remote_copy(
                    src_ref=ins[k].at[2 * px + py, pc], dst_ref=lands[k].at[4 * px + 2 * py + pc],
                    send_sem=send_sems.at[(r - 1) * n + k], recv_sem=recv_sems.at[(r - 1) * n + k],
                    device_id=(px, py, pc), device_id_type=MESH)
                copy.wait_send()
                copy.wait_recv()

    return pl.pallas_call(
        body, name=name, out_shape=tuple(pltpu.HBM(t.shape, t.dtype) for t in thru),
        in_specs=(_HBM,) * (2 * n) + (_SEM, _SEM, pl.BlockSpec(memory_space=pl.ANY)), out_specs=(_HBM,) * (2 * n),
        input_output_aliases={i: i for i in range(2 * n)},
        compiler_params=pltpu.CompilerParams(has_side_effects=_EFFECT))(*thru, send_sems, recv_sems, after)[n:]


def _sum8_class(own, land, chunk, name):
    rows, w = own.shape

    def body(own_ref, l_ref, o_ref):
        x, y, c = _place()
        me = 4 * x + 2 * y + c
        acc = jnp.where(me == 0, own_ref[...], l_ref[0].astype(F32))
        for d in range(1, 8):
            acc = acc + jnp.where(me == d, own_ref[...], l_ref[d].astype(F32))
        o_ref[...] = acc

    return pl.pallas_call(
        body, name=name, grid=(rows // chunk,),
        in_specs=[pl.BlockSpec((chunk, w), lambda i: (i, 0)), pl.BlockSpec((8, chunk, w), lambda i: (0, i, 0))],
        out_specs=pl.BlockSpec((chunk, w), lambda i: (i, 0)), out_shape=SDS((rows, w), F32),
        compiler_params=_params(1, VMEM_LIMIT))(own, land)


def _send_shard_wait(send_sems, recv_sems, v_thru, land_thru, after, name):
    def body(v_ref, land_ref, send_sems, recv_sems, after_ref, v_dead, got_ref):
        for j, (px, py, pc) in enumerate(_chip_peers()):
            copy = pltpu.make_async_remote_copy(src_ref=v_ref, dst_ref=land_ref.at[2 * px + py], send_sem=send_sems.at[j],
                                                recv_sem=recv_sems.at[j], device_id=(px, py, pc), device_id_type=MESH)
            copy.wait_send()
            copy.wait_recv()

    return pl.pallas_call(
        body, name=name,
        out_shape=(pltpu.HBM(v_thru.shape, v_thru.dtype), pltpu.HBM(land_thru.shape, land_thru.dtype)),
        in_specs=(_HBM, _HBM, _SEM, _SEM, pl.BlockSpec(memory_space=pl.ANY)), out_specs=(_HBM, _HBM),
        input_output_aliases={0: 0, 1: 1}, compiler_params=pltpu.CompilerParams(has_side_effects=_EFFECT),
    )(v_thru, land_thru, send_sems, recv_sems, after)[1]


_SMALL = ["mem_norm_w", "norm_pre", "norm_post", "a_ln_w", "a_ln_b", "a_w_s", "a_b_s", "b_conv_w", "c_conv_w",
          "c_a_log", "c_dt_bias", "c_o_norm_w"]
_SMALL_SHAPES = {"mem_norm_w": (D,), "norm_pre": (4, D), "norm_post": (4, D), "a_ln_w": (2, D), "a_ln_b": (2, D),
                 "a_w_s": (2, 8, HD, HD), "a_b_s": (2, 8, HD), "b_conv_w": (1, 3, D), "c_conv_w": (1, 4, 3 * D),
                 "c_a_log": (1, 8), "c_dt_bias": (1, 8), "c_o_norm_w": (1, HD)}
_SHARDED_SMALL = {"a_ln_w": D // 4, "a_ln_b": D // 4, "b_conv_w": D // 4, "c_conv_w": 3 * D // 4}
_SMALL_ROWS = 288


def _size(shape):
    n = 1
    for d in shape:
        n *= d
    return n


def kernel(x, mem, mem_norm_w, w_mem_kv, norm_pre, norm_post, w_out, a_w_in, a_ln_w, a_ln_b, a_w_s, a_b_s, b_w_in, b_conv_w, c_w_in, c_conv_w, c_a_log, c_dt_bias, c_o_norm_w, loss_target, m_mem_norm_w, m_w_mem_kv, m_norm_pre, m_norm_post, m_w_out, m_a_w_in, m_a_ln_w, m_a_ln_b, m_a_w_s, m_a_b_s, m_b_w_in, m_b_conv_w, m_c_w_in, m_c_conv_w, m_c_a_log, m_c_dt_bias, m_c_o_norm_w, v_mem_norm_w, v_w_mem_kv, v_norm_pre, v_norm_post, v_w_out, v_a_w_in, v_a_ln_w, v_a_ln_b, v_a_w_s, v_a_b_s, v_b_w_in, v_b_conv_w, v_c_w_in, v_c_conv_w, v_c_a_log, v_c_dt_bias, v_c_o_norm_w):
    names = ["mem_norm_w", "w_mem_kv", "norm_pre", "norm_post", "w_out", "a_w_in", "a_ln_w", "a_ln_b", "a_w_s", "a_b_s",
             "b_w_in", "b_conv_w", "c_w_in", "c_conv_w", "c_a_log", "c_dt_bias", "c_o_norm_w"]
    w = dict(zip(names, [mem_norm_w, w_mem_kv, norm_pre, norm_post, w_out, a_w_in, a_ln_w, a_ln_b, a_w_s, a_b_s, b_w_in,
                         b_conv_w, c_w_in, c_conv_w, c_a_log, c_dt_bias, c_o_norm_w]))
    m = dict(zip(names, [m_mem_norm_w, m_w_mem_kv, m_norm_pre, m_norm_post, m_w_out, m_a_w_in, m_a_ln_w, m_a_ln_b, m_a_w_s,
                         m_a_b_s, m_b_w_in, m_b_conv_w, m_c_w_in, m_c_conv_w, m_c_a_log, m_c_dt_bias, m_c_o_norm_w]))
    v = dict(zip(names, [v_mem_norm_w, v_w_mem_kv, v_norm_pre, v_norm_post, v_w_out, v_a_w_in, v_a_ln_w, v_a_ln_b, v_a_w_s,
                         v_a_b_s, v_b_w_in, v_b_conv_w, v_c_w_in, v_c_conv_w, v_c_a_log, v_c_dt_bias, v_c_o_norm_w]))
    chip = 2 * lax.axis_index("x") + lax.axis_index("y")

    def rows_of_ct(a):
        return a[0].T

    def with_mine(gathered, own):
        return lax.dynamic_update_slice(gathered, own[None], (chip,) + (0,) * own.ndim)

    first = [a_w_in[0].astype(BF16), w_out[0].astype(BF16), w_mem_kv.astype(BF16)]
    vec = jnp.concatenate([a_ln_w.reshape(-1), a_ln_b.reshape(-1), b_conv_w.reshape(-1), c_conv_w.reshape(-1)])
    vec = jnp.pad(vec, (0, 8 * D - vec.shape[0])).reshape(8, D)
    *gathered, gvec = _gather_classes(first, [W_CLASSES[k] for k in W_CLASSES], vec)
    ga0, go0, gkv = [with_mine(g_, a) for g_, a in zip(gathered, first)]
    gvec = with_mine(gvec, vec)
    gv = gvec.reshape(4, 8 * D)
    later = {("in", 1): b_w_in[0], ("in", 2): jnp.pad(rows_of_ct(c_w_in), ((0, C_ROWS - 1284), (0, 0))),
             ("in", 3): a_w_in[1], ("out", 1): w_out[1], ("out", 2): w_out[2], ("out", 3): w_out[3]}
    later = {k: (a + 0.0 * gkv[0, 0, 0].astype(F32)).astype(BF16) for k, a in later.items()}
    sent = {k: _send_shard_start(a, f"send_w_{k[0]}_{k[1]}") for k, a in later.items()}
    started = sum(s[4][0, 0] for s in sent.values())

    def arrived(k, after):
        return with_mine(_send_shard_wait(*sent[k][:4], after, f"wait_w_{k[0]}_{k[1]}"), later[k])

    def blocks(i, after):
        if i == 0:
            return [ga0[0], ga0[1]], [ga0[2], ga0[3]], go0.reshape(D_CAT, D)
        got, wo = arrived(("in", i), after), arrived(("out", i), after).reshape(D_CAT, D)
        if i == 1:
            return [got[0], got[1], got[2][:, :512]], [got[2][:, 512:], got[3]], wo
        if i == 3:
            return [got[0], got[1]], [got[2], got[3]], wo
        fct = got[:, :1284].reshape(5136, D)
        c_ab = jnp.concatenate([fct[3 * D:3 * D + 16], jnp.zeros((AB_PAD - 16, D), BF16)], axis=0)
        return [fct[:3 * D], c_ab], [fct[3 * D + 16:]], wo
    sm = {"mem_norm_w": mem_norm_w, "norm_pre": norm_pre + started, "norm_post": norm_post, "a_w_s": a_w_s, "a_b_s": a_b_s,
          "c_a_log": c_a_log, "c_dt_bias": c_dt_bias, "c_o_norm_w": c_o_norm_w,
          "a_ln_w": gv[:, 0:512].reshape(4, 2, 256).transpose(1, 0, 2).reshape(2, D),
          "a_ln_b": gv[:, 512:1024].reshape(4, 2, 256).transpose(1, 0, 2).reshape(2, D),
          "b_conv_w": gv[:, 1024:1792].reshape(4, 1, 3, 256).transpose(1, 2, 0, 3).reshape(1, 3, D),
          "c_conv_w": gv[:, 1792:4864].reshape(4, 1, 4, 768).transpose(1, 2, 0, 3).reshape(1, 4, 3 * D)}
    wts = {"wkv": gkv.reshape(D, 2 * D_XA), "blocks": blocks}

    def layer_grads(i, g):
        if i % 3 == 2:
            gct = jnp.concatenate([g["wm"][i][:3 * D + 16], g["wg"][i]], axis=0).reshape(4, 1284, D)
            w_in = jnp.pad(gct, ((0, 0), (0, C_ROWS - 1284), (0, 0)))
        else:
            w_in = jnp.concatenate([g["wm"][i], g["wg"][i]], axis=0).reshape(4, -1, GRAD_TILE[i % 3])
        out = {f"in{i}": w_in, f"out{i}": g["wo"][i].reshape(4, 384, D)}
        return {k: a.reshape(4, 2, a.shape[1] // 2, a.shape[2]) for k, a in out.items()}

    pending = {}

    def layer_done(i, g):
        halves = layer_grads(i, g)
        started = _send_pieces_start([h.astype(BF16) for h in halves.values()], f"send_grads_{i}")
        pending[i] = (started, halves)
        return started[-1][0, 0]

    wts["layer_done"] = layer_done

    loss, dx, g = _local_step(x[0], mem[0], loss_target[0], wts, sm)
    loss = lax.psum(loss, ("x", "y", "c"))

    core = lax.axis_index("c")
    mine, specs = {}, {}
    for i in reversed(range(DEPTH)):
        started, halves = pending[i]
        lands = _send_pieces_wait(started, dx, f"wait_grads_{i}")
        for (k, h), land in zip(halves.items(), lands):
            own = lax.dynamic_index_in_dim(lax.dynamic_index_in_dim(h, chip, 0, False), core, 0, False)
            specs[k] = (own.shape[0], _REDUCE_CHUNK.get(own.shape[0], own.shape[0]))
            mine[k] = _sum8_class(own, land, specs[k][1], f"sum8_{k}")

    first = {"kv": g["wkv"].reshape(4, 2, 128, D)}
    first_specs = [(h.shape[2], _REDUCE_CHUNK.get(h.shape[2], h.shape[2])) for h in first.values()]
    halves = list(first.values())
    gs = {"mem_norm_w": g["mem_norm_w"], "norm_pre": jnp.concatenate(g["norm_pre"]),
          "norm_post": jnp.concatenate(g["norm_post"])}
    for n in _SMALL[3:]:
        gs[n] = jnp.stack([g[n][j] for j in sorted(g[n])])
    flat = jnp.concatenate([gs[n].reshape(-1) for n in _SMALL])
    small = jnp.pad(flat, (0, _SMALL_ROWS * D - flat.shape[0])).reshape(_SMALL_ROWS, D)
    *others, other_small = _swap_classes(halves, first_specs, small)
    pairs = [_pair_sum_class(h, o, s[1], f"pair_sum_{k}") for k, h, o, s in zip(first, halves, others, first_specs)]
    pair_small = _add(small, other_small, "pair_sum_small")
    *lands, land_small = _exchange_classes([p[0] for p in pairs], first_specs, pair_small)
    for k, p, land, s in zip(first, pairs, lands, first_specs):
        mine[k], specs[k] = _chip_sum_class(p[1], land, s[1], f"chip_sum_{k}"), s
    theirs = _share_classes(list(mine.values()), [specs[k] for k in mine])
    south = core == 0
    sh = {k: jnp.concatenate([jnp.where(south, a, b), jnp.where(south, b, a)], axis=0)
          for (k, a), b in zip(mine.items(), theirs)}
    grads = {"a_w_in": jnp.stack([sh["in0"], sh["in3"]]),
             "b_w_in": sh["in1"].reshape(5, D, 256).transpose(1, 0, 2).reshape(b_w_in.shape),
             "c_w_in": sh["in2"][:1284], "w_out": jnp.stack([sh[f"out{i}"] for i in range(DEPTH)]),
             "w_mem_kv": sh["kv"]}
    flat = _sum4(pair_small, land_small).reshape(-1)
    off = 0
    for n in _SMALL:
        shape = _SMALL_SHAPES[n]
        full = flat[off:off + _size(shape)].reshape(shape)
        off += _size(shape)
        if n in _SHARDED_SMALL:
            full = lax.dynamic_slice_in_dim(full, chip * _SHARDED_SMALL[n], _SHARDED_SMALL[n], axis=len(shape) - 1)
        grads[n] = full

    delta, new_m, new_v = {}, {}, {}
    for n in names:
        shape = w[n].shape
        if n == "c_w_in":
            d_, m_, v_ = _adamw(rows_of_ct(w[n]), grads[n], rows_of_ct(m[n]), rows_of_ct(v[n]), f"adamw_{n}")
            delta[n], new_m[n], new_v[n], grads[n] = d_.T[None], m_.T[None], v_.T[None], grads[n].T[None]
            continue
        view = (1, shape[0]) if len(shape) == 1 else (_size(shape[:-1]), shape[-1])
        d_, m_, v_ = _adamw(w[n].reshape(view), grads[n].reshape(view), m[n].reshape(view), v[n].reshape(view),
                            f"adamw_{n}")
        delta[n], new_m[n], new_v[n] = d_.reshape(shape), m_.reshape(shape), v_.reshape(shape)
    return (loss, dx[None], *[grads[n].reshape(w[n].shape) for n in names], *[delta[n] for n in names],
            *[new_m[n] for n in names], *[new_v[n] for n in names])
```

```python
import functools

import jax
import jax.numpy as jnp
from jax import lax
from jax.experimental import pallas as pl
from jax.experimental.pallas import tpu as pltpu

F32 = jnp.float32
BF16 = jnp.bfloat16
HI = lax.Precision.HIGHEST
MESH = pl.DeviceIdType.MESH
SDS = jax.ShapeDtypeStruct

D = 1024
D_XA = 512
D_CAT = 1536
N_MEM = 256
HD = 128
DEPTH = 4
EPS = 1e-6
TT = 512
DN_C = 64
DN_TB = 256
HALO = 8
AB_PAD = 128
VMEM_LIMIT = 56 * 1024 * 1024
GRAD_TILE = {0: 1024, 1: 256}

ADAM_LR, ADAM_B1, ADAM_B2, ADAM_EPS, ADAM_WD, ADAM_STEP = 0.001, 0.9, 0.999, 1e-08, 0.01, 10


def _params(n_grid, vmem=None):
    return pltpu.CompilerParams(dimension_semantics=("arbitrary",) * n_grid, vmem_limit_bytes=vmem)


def _rms(x, w):
    return x * lax.rsqrt(jnp.mean(x * x, axis=-1, keepdims=True) + EPS) * w


def _dot_nn(a, b):
    return jnp.dot(a.astype(BF16), b.astype(BF16), preferred_element_type=F32)


def _dot_nt(a, b):
    return lax.dot_general(a.astype(BF16), b.astype(BF16), (((1,), (1,)), ((), ())), preferred_element_type=F32)


def _dot_tn(a, b):
    return lax.dot_general(a.astype(BF16), b.astype(BF16), (((0,), (0,)), ((), ())), preferred_element_type=F32)


@jax.custom_vjp
def mm(a, b):
    return _dot_nn(a, b)


mm.defvjp(lambda a, b: (_dot_nn(a, b), (a, b)), lambda r, g: (_dot_nt(g, r[1]), _dot_tn(r[0], g)))


@jax.custom_vjp
def mm_nt(a, b):
    return _dot_nt(a, b)


mm_nt.defvjp(lambda a, b: (_dot_nt(a, b), (a, b)), lambda r, g: (_dot_nn(g, r[1]), _dot_tn(g, r[0])))


def _row_spec(width, tile=TT):
    return pl.BlockSpec((tile, width), lambda i: (i, 0))


def _full_spec(shape):
    return pl.BlockSpec(shape, lambda *_: (0,) * len(shape))


def _widths(blocks, transposed):
    return [b.shape[0 if transposed else 1] for b in blocks]


def _inproj_fwd(x, nw, mix, gate, transposed, name):
    T, nm = x.shape[0], len(mix)
    M, G = sum(_widths(mix, transposed)), sum(_widths(gate, transposed))

    def body(x_ref, nw_ref, *refs):
        blocks, (pm_ref, pg_ref, h_ref) = refs[:-3], refs[-3:]
        h = _rms(x_ref[...], nw_ref[...]).astype(BF16)
        h_ref[...] = h
        for p_ref, group in ((pm_ref, blocks[:nm]), (pg_ref, blocks[nm:])):
            off = 0
            for w_ref in group:
                w = w_ref.shape[0 if transposed else 1]
                p_ref[:, off:off + w] = _dot_nt(h, w_ref[...]) if transposed else _dot_nn(h, w_ref[...])
                off += w

    return pl.pallas_call(
        body, name=name, grid=(T // TT,),
        in_specs=[_row_spec(D), _full_spec((1, D))] + [_full_spec(b.shape) for b in mix + gate],
        out_specs=[_row_spec(M), _row_spec(G), _row_spec(D)],
        out_shape=[SDS((T, M), F32), SDS((T, G), F32), SDS((T, D), BF16)],
        compiler_params=_params(1, VMEM_LIMIT))(x, nw, *mix, *gate)


def _inproj_bwd(dpm, dpg, x, nw, mix, gate, transposed, dxc, name):
    T, nm = x.shape[0], len(mix)
    M, G = sum(_widths(mix, transposed)), sum(_widths(gate, transposed))

    def body(dpm_ref, dpg_ref, x_ref, nw_ref, *refs):
        blocks, (dxc_ref, dx_ref, dnw_ref) = refs[:-3], refs[-3:]
        dh = None
        for dp_ref, group in ((dpm_ref, blocks[:nm]), (dpg_ref, blocks[nm:])):
            off = 0
            for w_ref in group:
                w = w_ref.shape[0 if transposed else 1]
                dp = dp_ref[:, off:off + w]
                part = _dot_nn(dp, w_ref[...]) if transposed else _dot_nt(dp, w_ref[...])
                dh = part if dh is None else dh + part
                off += w
        _, vjp = jax.vjp(_rms, x_ref[...], nw_ref[...])
        dxr, dnw = vjp(dh)
        dx_ref[...] = dxc_ref[...] + dxr

        @pl.when(pl.program_id(0) == 0)
        def _():
            dnw_ref[...] = jnp.zeros_like(dnw_ref)
        dnw_ref[...] += dnw

    return pl.pallas_call(
        body, name=name, grid=(T // TT,),
        in_specs=[_row_spec(M), _row_spec(G), _row_spec(D), _full_spec((1, D))]
        + [_full_spec(b.shape) for b in mix + gate] + [_row_spec(D)],
        out_specs=[_row_spec(D), _full_spec((1, D))],
        out_shape=[SDS((T, D), F32), SDS((1, D), F32)],
        compiler_params=_params(1, VMEM_LIMIT))(dpm, dpg, x, nw, *mix, *gate, dxc)


def _matmul_tn(a, b, name, sub=None):
    T, K = a.shape
    N = b.shape[1]
    tn = 1024 if N % 1024 == 0 else (640 if N % 640 == 0 else N)
    tt = min(1024, T)
    n_sub = 1 if sub is None else tn // sub

    def body(a_ref, b_ref, o_ref):
        @pl.when(pl.program_id(1) == 0)
        def _():
            o_ref[...] = jnp.zeros_like(o_ref)
        res = _dot_tn(a_ref[...], b_ref[...])
        if sub is None:
            o_ref[...] += res
        else:
            for i in range(n_sub):
                o_ref[i] += res[:, i * sub:(i + 1) * sub]

    if sub is None:
        out_spec, out_shape = pl.BlockSpec((K, tn), lambda j, t: (0, j)), SDS((K, N), F32)
    else:
        out_spec, out_shape = pl.BlockSpec((n_sub, K, sub), lambda j, t: (j, 0, 0)), SDS((N // sub, K, sub), F32)
    return pl.pallas_call(
        body, name=name, grid=(N // tn, T // tt),
        in_specs=[pl.BlockSpec((tt, K), lambda j, t: (t, 0)), pl.BlockSpec((tt, tn), lambda j, t: (t, j))],
        out_specs=out_spec, out_shape=out_shape,
        compiler_params=_params(2, VMEM_LIMIT))(a, b)


def _memkv_fn(mem, w, wkv):
    return mm(_rms(mem, w), wkv)


def _memkv_fwd(mem, w, wkv):
    def body(mem_ref, w_ref, wkv_ref, kv_ref):
        kv_ref[...] = _memkv_fn(mem_ref[...], w_ref[...], wkv_ref[...])

    return pl.pallas_call(body, name="memkv_fwd", out_shape=SDS((N_MEM, 2 * D_XA), F32),
                          compiler_params=_params(0, VMEM_LIMIT))(mem, w, wkv)


def _memkv_bwd(mem, w, wkv, dkv):
    def body(mem_ref, w_ref, wkv_ref, dkv_ref, dw_ref, dwkv_ref):
        _, vjp = jax.vjp(functools.partial(_memkv_fn, mem_ref[...]), w_ref[...], wkv_ref[...].astype(F32))
        dw, dwkv = vjp(dkv_ref[...])
        dw_ref[...] = dw
        dwkv_ref[...] = dwkv

    return pl.pallas_call(body, name="memkv_bwd", out_shape=[SDS((1, D), F32), SDS((D, 2 * D_XA), F32)],
                          compiler_params=_params(0, VMEM_LIMIT))(mem, w, wkv, dkv)


def _attn_gate(ymix, qx, z, *kvs):
    outs = []
    for j in range(4):
        s = mm_nt(qx[:, j * HD:(j + 1) * HD], kvs[j]) * (HD ** -0.5)
        e = jnp.exp(s - lax.stop_gradient(jnp.max(s, axis=-1, keepdims=True)))
        outs.append(mm(e / jnp.sum(e, axis=-1, keepdims=True), kvs[4 + j]))
    return jnp.concatenate([ymix] + outs, axis=1) * jax.nn.silu(z)


def _kv_blocks(kv_ref):
    return [kv_ref[:, j * HD:(j + 1) * HD] for j in range(8)]


def _ag_fwd(ymix, pg, kv, name):
    T = ymix.shape[0]

    def body(ymix_ref, pg_ref, kv_ref, ycat_ref):
        ycat_ref[...] = _attn_gate(ymix_ref[...], pg_ref[:, :D_XA], pg_ref[:, D_XA:], *_kv_blocks(kv_ref)).astype(BF16)

    return pl.pallas_call(
        body, name=name, grid=(T // TT,),
        in_specs=[_row_spec(D), _row_spec(D_XA + D_CAT), _full_spec((N_MEM, 2 * D_XA))],
        out_specs=_row_spec(D_CAT), out_shape=SDS((T, D_CAT), BF16),
        compiler_params=_params(1, VMEM_LIMIT))(ymix, pg, kv)


def _ag_bwd(dycat, ymix, pg, kv, dkv_in, name):
    T = ymix.shape[0]

    def body(dycat_ref, ymix_ref, pg_ref, kv_ref, dkvin_ref, dymix_ref, dpg_ref, dkv_ref):
        _, vjp = jax.vjp(_attn_gate, ymix_ref[...], pg_ref[:, :D_XA], pg_ref[:, D_XA:], *_kv_blocks(kv_ref))
        g = vjp(dycat_ref[...])
        dymix_ref[...] = g[0]
        dpg_ref[:, :D_XA] = g[1].astype(BF16)
        dpg_ref[:, D_XA:] = g[2].astype(BF16)

        @pl.when(pl.program_id(0) == 0)
        def _():
            dkv_ref[...] = dkvin_ref[...]
        for j in range(8):
            dkv_ref[:, j * HD:(j + 1) * HD] += g[3 + j]

    return pl.pallas_call(
        body, name=name, grid=(T // TT,),
        in_specs=[_row_spec(D_CAT), _row_spec(D), _row_spec(D_XA + D_CAT), _full_spec((N_MEM, 2 * D_XA)),
                  _full_spec((N_MEM, 2 * D_XA))],
        out_specs=[_row_spec(D), _row_spec(D_XA + D_CAT), _full_spec((N_MEM, 2 * D_XA))],
        out_shape=[SDS((T, D), F32), SDS((T, D_XA + D_CAT), BF16), SDS((N_MEM, 2 * D_XA), F32)],
        compiler_params=_params(1, VMEM_LIMIT))(dycat, ymix, pg, kv, dkv_in)


def _outproj_fwd(ycat, wo, x, nw, name):
    T = x.shape[0]

    def body(ycat_ref, wo_ref, x_ref, nw_ref, o_ref, xn_ref):
        o = jnp.dot(ycat_ref[...], wo_ref[...], preferred_element_type=F32)
        o_ref[...] = o
        xn_ref[...] = x_ref[...] + _rms(o, nw_ref[...])

    return pl.pallas_call(
        body, name=name, grid=(T // TT,),
        in_specs=[_row_spec(D_CAT), _full_spec((D_CAT, D)), _row_spec(D), _full_spec((1, D))],
        out_specs=[_row_spec(D), _row_spec(D)], out_shape=[SDS((T, D), F32), SDS((T, D), F32)],
        compiler_params=_params(1, VMEM_LIMIT))(ycat, wo, x, nw)


def _outproj_bwd(dxo, o, nw, wo, name):
    T = dxo.shape[0]

    def body(dxo_ref, o_ref, nw_ref, wo_ref, dycat_ref, dobf_ref, dnw_ref):
        _, vjp = jax.vjp(_rms, o_ref[...], nw_ref[...])
        do, dnw = vjp(dxo_ref[...])
        dobf = do.astype(BF16)
        dobf_ref[...] = dobf
        dycat_ref[...] = _dot_nt(dobf, wo_ref[...])

        @pl.when(pl.program_id(0) == 0)
        def _():
            dnw_ref[...] = jnp.zeros_like(dnw_ref)
        dnw_ref[...] += dnw

    return pl.pallas_call(
        body, name=name, grid=(T // TT,),
        in_specs=[_row_spec(D), _row_spec(D), _full_spec((1, D)), _full_spec((D_CAT, D))],
        out_specs=[_row_spec(D_CAT), _row_spec(D), _full_spec((1, D))],
        out_shape=[SDS((T, D_CAT), F32), SDS((T, D), BF16), SDS((1, D), F32)],
        compiler_params=_params(1, VMEM_LIMIT))(dxo, o, nw, wo)


def _loss_head(xl, target):
    T = xl.shape[0]

    def body(x_ref, t_ref, loss_ref, dx_ref):
        err = x_ref[...] - t_ref[...]
        dx_ref[...] = err * (1.0 / D)

        @pl.when(pl.program_id(0) == 0)
        def _():
            loss_ref[...] = jnp.zeros_like(loss_ref)
        part = jnp.sum(jnp.sum(err * err, axis=1, keepdims=True), axis=0, keepdims=True) * (0.5 / D)
        loss_ref[...] += jnp.broadcast_to(part, loss_ref.shape)

    return pl.pallas_call(
        body, name="loss_head", grid=(T // TT,),
        in_specs=[_row_spec(D), _row_spec(D)],
        out_specs=[_full_spec((8, 128)), _row_spec(D)], out_shape=[SDS((8, 128), F32), SDS((T, D), F32)],
        compiler_params=_params(1))(xl, target)


def _gmlp_pre(u, v, lnw, lnb):
    vg = jax.nn.gelu(v)
    xc = vg - jnp.mean(vg, axis=-1, keepdims=True)
    vl = xc * lax.rsqrt(jnp.mean(xc * xc, axis=-1, keepdims=True) + EPS) * lnw + lnb
    return jax.nn.gelu(u), vl


def _tril(n, strict=False):
    r = lax.broadcasted_iota(jnp.int32, (n, n), 0)
    c = lax.broadcasted_iota(jnp.int32, (n, n), 1)
    return (r > c) if strict else (r >= c)


def _gmlp_fwd(pm, lnw, lnb, ws, bs3, name):
    T = pm.shape[0]

    def body(pm_ref, lnw_ref, lnb_ref, ws_ref, bs_ref, y_ref):
        ug, vl = _gmlp_pre(pm_ref[:, :D], pm_ref[:, D:], lnw_ref[...], lnb_ref[...])
        mask = _tril(HD)
        for g in range(8):
            w = jnp.where(mask, ws_ref[g], 0.0)
            for c in range(TT // HD):
                rows, cols = slice(c * HD, (c + 1) * HD), slice(g * HD, (g + 1) * HD)
                y_ref[rows, cols] = ug[rows, cols] * (_dot_nn(w, vl[rows, cols]) + bs_ref[g])

    return pl.pallas_call(
        body, name=name, grid=(T // TT,),
        in_specs=[_row_spec(2 * D), _full_spec((1, D)), _full_spec((1, D)), _full_spec((8, HD, HD)),
                  _full_spec((8, HD, HD))],
        out_specs=_row_spec(D), out_shape=SDS((T, D), F32),
        compiler_params=_params(1, VMEM_LIMIT))(pm, lnw, lnb, ws, bs3)


def _gmlp_bwd(dy, pm, lnw, lnb, ws, bs3, name):
    T = pm.shape[0]
    n_t = T // TT

    def body(dy_ref, pm_ref, lnw_ref, lnb_ref, ws_ref, bs_ref, dpm_ref, dlnw_ref, dlnb_ref, dws_ref, dbs_ref,
             dug_scr, dvl_scr, dbs_scr):
        i = pl.program_id(0)

        @pl.when(i == 0)
        def _():
            dlnw_ref[...] = jnp.zeros_like(dlnw_ref)
            dlnb_ref[...] = jnp.zeros_like(dlnb_ref)
            dws_ref[...] = jnp.zeros_like(dws_ref)
            dbs_scr[...] = jnp.zeros_like(dbs_scr)

        (ug, vl), vjp = jax.vjp(_gmlp_pre, pm_ref[:, :D], pm_ref[:, D:], lnw_ref[...], lnb_ref[...])
        mask = _tril(HD)
        for g in range(8):
            w = jnp.where(mask, ws_ref[g], 0.0)
            dw = jnp.zeros((HD, HD), F32)
            db = jnp.zeros((HD, HD), F32)
            for c in range(TT // HD):
                rows, cols = slice(c * HD, (c + 1) * HD), slice(g * HD, (g + 1) * HD)
                dyb, vlb = dy_ref[rows, cols], vl[rows, cols]
                sp = _dot_nn(w, vlb) + bs_ref[g]
                dsp = dyb * ug[rows, cols]
                dug_scr[rows, cols] = dyb * sp
                dvl_scr[rows, cols] = _dot_tn(w, dsp)
                dw += _dot_nt(dsp, vlb)
                db += dsp
            dws_ref[g] += jnp.where(mask, dw, 0.0)
            dbs_scr[g] += db
        du, dv, dlnw, dlnb = vjp((dug_scr[...], dvl_scr[...]))
        dpm_ref[:, :D] = du.astype(BF16)
        dpm_ref[:, D:] = dv.astype(BF16)
        dlnw_ref[...] += dlnw
        dlnb_ref[...] += dlnb

        @pl.when(i == n_t - 1)
        def _():
            for g in range(8):
                dbs_ref[g] = jnp.broadcast_to(jnp.sum(dbs_scr[g], axis=1, keepdims=True), (HD, HD))

    return pl.pallas_call(
        body, name=name, grid=(n_t,),
        in_specs=[_row_spec(D), _row_spec(2 * D), _full_spec((1, D)), _full_spec((1, D)), _full_spec((8, HD, HD)),
                  _full_spec((8, HD, HD))],
        out_specs=[_row_spec(2 * D), _full_spec((1, D)), _full_spec((1, D)), _full_spec((8, HD, HD)),
                   _full_spec((8, HD, HD))],
        out_shape=[SDS((T, 2 * D), BF16), SDS((1, D), F32), SDS((1, D), F32), SDS((8, HD, HD), F32),
                   SDS((8, HD, HD), F32)],
        scratch_shapes=[pltpu.VMEM((TT, D), F32), pltpu.VMEM((TT, D), F32), pltpu.VMEM((8, HD, HD), F32)],
        compiler_params=_params(1, VMEM_LIMIT))(dy, pm, lnw, lnb, ws, bs3)


def _prev_spec(width, T):
    return pl.BlockSpec((HALO, width), lambda i: (jnp.maximum(i * (TT // HALO) - 1, 0), 0))


def _next_spec(width, T):
    return pl.BlockSpec((HALO, width), lambda i: (jnp.minimum((i + 1) * (TT // HALO), T // HALO - 1), 0))


def _rows_before(ext, j):
    return ext[HALO:] if j == 0 else pltpu.roll(ext, j, 0)[HALO:]


def _rows_after(ext, j):
    n = ext.shape[0]
    return ext[:n - HALO] if j == 0 else pltpu.roll(ext, n - j, 0)[:n - HALO]


def _conv_apply(ext_s, w):
    K = w.shape[0]
    y = _rows_before(ext_s, K - 1) * w[0:1]
    for k in range(1, K):
        y = y + _rows_before(ext_s, K - 1 - k) * w[k:k + 1]
    return y


def _conv_grads(ext_s, ext_dy, w):
    K = w.shape[0]
    dy = ext_dy[:ext_dy.shape[0] - HALO]
    ds = _rows_after(ext_dy, K - 1) * w[0:1]
    dws = [jnp.sum(dy * _rows_before(ext_s, K - 1), axis=0, keepdims=True)]
    for k in range(1, K):
        ds = ds + _rows_after(ext_dy, K - 1 - k) * w[k:k + 1]
        dws.append(jnp.sum(dy * _rows_before(ext_s, K - 1 - k), axis=0, keepdims=True))
    return ds, jnp.concatenate(dws, axis=0)


def _sconv_fwd(pm, w, name):
    T = pm.shape[0]

    def body(pm_ref, prev_ref, w_ref, y_ref):
        s = pm_ref[:, D:2 * D] * pm_ref[:, 2 * D:]
        sp = jnp.where(pl.program_id(0) > 0, prev_ref[:, D:2 * D] * prev_ref[:, 2 * D:], 0.0)
        y_ref[...] = pm_ref[:, :D] * _conv_apply(jnp.concatenate([sp, s], axis=0), w_ref[...])

    return pl.pallas_call(
        body, name=name, grid=(T // TT,),
        in_specs=[_row_spec(3 * D), _prev_spec(3 * D, T), _full_spec((3, D))],
        out_specs=_row_spec(D), out_shape=SDS((T, D), F32),
        compiler_params=_params(1, VMEM_LIMIT))(pm, pm, w)


def _sconv_bwd(dy, pm, w, name):
    T = pm.shape[0]
    n_t = T // TT

    def body(dy_ref, dyn_ref, pm_ref, prev_ref, next_ref, w_ref, dpm_ref, dw_ref):
        i = pl.program_id(0)
        bg, cg, hv = pm_ref[:, :D], pm_ref[:, D:2 * D], pm_ref[:, 2 * D:]
        sp = jnp.where(i > 0, prev_ref[:, D:2 * D] * prev_ref[:, 2 * D:], 0.0)
        ext_s = jnp.concatenate([sp, cg * hv], axis=0)
        dyv = dy_ref[...]
        dcn = jnp.where(i < n_t - 1, dyn_ref[...] * next_ref[:, :D], 0.0)
        ds, dw = _conv_grads(ext_s, jnp.concatenate([dyv * bg, dcn], axis=0), w_ref[...])
        dpm_ref[:, :D] = (dyv * _conv_apply(ext_s, w_ref[...])).astype(BF16)
        dpm_ref[:, D:2 * D] = (ds * hv).astype(BF16)
        dpm_ref[:, 2 * D:] = (ds * cg).astype(BF16)

        @pl.when(i == 0)
        def _():
            dw_ref[...] = jnp.zeros_like(dw_ref)
        dw_ref[...] += dw

    return pl.pallas_call(
        body, name=name, grid=(n_t,),
        in_specs=[_row_spec(D), _next_spec(D, T), _row_spec(3 * D), _prev_spec(3 * D, T), _next_spec(3 * D, T),
                  _full_spec((3, D))],
        out_specs=[_row_spec(3 * D), _full_spec((3, D))],
        out_shape=[SDS((T, 3 * D), BF16), SDS((3, D), F32)],
        compiler_params=_params(1, VMEM_LIMIT))(dy, dy, pm, pm, pm, w)


def _dnconv_fwd(pm, w, name):
    T = pm.shape[0]

    def body(pm_ref, prev_ref, w_ref, c_ref):
        sp = jnp.where(pl.program_id(0) > 0, prev_ref[...], 0.0)
        c_ref[...] = _conv_apply(jnp.concatenate([sp, pm_ref[...]], axis=0), w_ref[...])

    return pl.pallas_call(
        body, name=name, grid=(T // TT,),
        in_specs=[_row_spec(3 * D), _prev_spec(3 * D, T), _full_spec((4, 3 * D))],
        out_specs=_row_spec(3 * D), out_shape=SDS((T, 3 * D), F32),
        compiler_params=_params(1, VMEM_LIMIT))(pm, pm, w)


def _dnconv_bwd(dcq, dck, dcv, dab, pm, w, name):
    T = pm.shape[0]
    n_t = T // TT

    def body(dq_ref, dk_ref, dv_ref, dqn_ref, dkn_ref, dvn_ref, dab_ref, pm_ref, prev_ref, w_ref, dpm_ref, dw_ref):
        i = pl.program_id(0)
        sp = jnp.where(i > 0, prev_ref[...], 0.0)
        ext_s = jnp.concatenate([sp, pm_ref[...]], axis=0)
        own = jnp.concatenate([dq_ref[...], dk_ref[...], dv_ref[...]], axis=1)
        nxt = jnp.where(i < n_t - 1, jnp.concatenate([dqn_ref[...], dkn_ref[...], dvn_ref[...]], axis=1), 0.0)
        ds, dw = _conv_grads(ext_s, jnp.concatenate([own, nxt], axis=0), w_ref[...])
        dpm_ref[:, :3 * D] = ds.astype(BF16)
        dpm_ref[:, 3 * D:] = dab_ref[...].astype(BF16)

        @pl.when(i == 0)
        def _():
            dw_ref[...] = jnp.zeros_like(dw_ref)
        dw_ref[...] += dw

    return pl.pallas_call(
        body, name=name, grid=(n_t,),
        in_specs=[_row_spec(D), _row_spec(D), _row_spec(D), _next_spec(D, T), _next_spec(D, T), _next_spec(D, T),
                  _row_spec(AB_PAD), _row_spec(3 * D), _prev_spec(3 * D, T), _full_spec((4, 3 * D))],
        out_specs=[_row_spec(3 * D + AB_PAD), _full_spec((4, 3 * D))],
        out_shape=[SDS((T, 3 * D + AB_PAD), BF16), SDS((4, 3 * D), F32)],
        compiler_params=_params(1, VMEM_LIMIT))(dcq, dck, dcv, dcq, dck, dcv, dab, pm, pm, w)


def _l2n(x):
    return x * lax.rsqrt(jnp.sum(x * x, axis=-1, keepdims=True) + EPS)


def _softplus(x):
    return jnp.maximum(x, 0.0) + jnp.log1p(jnp.exp(-jnp.abs(x)))


_BNN = (((2,), (1,)), ((0,), (0,)))
_BNT = (((2,), (2,)), ((0,), (0,)))
_BTN = (((1,), (1,)), ((0,), (0,)))


def _bdot(a, b, dims):
    return lax.dot_general(a.astype(BF16), b.astype(BF16), dims, preferred_element_type=F32)


def _bdot3(a, b, dims):
    ah, bh = a.astype(BF16), b.astype(BF16)
    al, bl = (a - ah.astype(F32)).astype(BF16), (b - bh.astype(F32)).astype(BF16)
    d = functools.partial(lax.dot_general, dimension_numbers=dims, preferred_element_type=F32)
    return d(ah, bh) + (d(ah, bl) + d(al, bh))


def _bdot_hi(a, b, dims):
    return lax.dot_general(a, b, dims, precision=HI, preferred_element_type=F32)


def _batched_matmuls(dot):
    @jax.custom_vjp
    def nn(a, b):
        return dot(a, b, _BNN)

    @jax.custom_vjp
    def nt(a, b):
        return dot(a, b, _BNT)

    @jax.custom_vjp
    def tn(a, b):
        return dot(a, b, _BTN)

    nn.defvjp(lambda a, b: (dot(a, b, _BNN), (a, b)), lambda r, g: (dot(g, r[1], _BNT), dot(r[0], g, _BTN)))
    nt.defvjp(lambda a, b: (dot(a, b, _BNT), (a, b)), lambda r, g: (dot(g, r[1], _BNN), dot(g, r[0], _BTN)))
    tn.defvjp(lambda a, b: (dot(a, b, _BTN), (a, b)), lambda r, g: (dot(r[1], g, _BNT), dot(r[0], g, _BNN)))
    return nn, nt, tn


bmm, bmm_nt, bmm_tn = _batched_matmuls(_bdot)
bmm_hi, _, _ = _batched_matmuls(_bdot_hi)

@jax.custom_vjp
def _neumann_inverse(n):
    C = n.shape[1]
    eye = lax.broadcasted_iota(jnp.int32, n.shape, 1) == lax.broadcasted_iota(jnp.int32, n.shape, 2)
    t = eye.astype(F32) + n
    for _ in range(5):
        n = _bdot3(n, n, _BNN)
        t = t + _bdot3(t, n, _BNN)
    return t


def _neumann_inverse_fwd(n):
    t = _neumann_inverse(n)
    return t, t


def _neumann_inverse_bwd(t, g):
    return (_bdot3(_bdot3(t, g, _BTN), t, _BNT),)


_neumann_inverse.defvjp(_neumann_inverse_fwd, _neumann_inverse_bwd)


@jax.custom_vjp
def _saved_inverse(n, t):
    return t


_saved_inverse.defvjp(lambda n, t: (t, t), lambda t, g: (_bdot3(_bdot3(t, g, _BTN), t, _BNT), jnp.zeros_like(t)))

DN_NCH = DN_TB // DN_C
DN_NH = 4


def _decay_terms(ab, alog, dtb, first_head, n_heads):
    C = DN_C
    lane = lax.broadcasted_iota(jnp.int32, ab.shape, 1)
    g_all = (-jnp.exp(alog) * _softplus(ab + dtb)).reshape(DN_NCH, C, HD)
    beta_all = jax.nn.sigmoid(ab)
    r = lax.broadcasted_iota(jnp.int32, (DN_NCH, C, C), 1)
    c = lax.broadcasted_iota(jnp.int32, (DN_NCH, C, C), 2)
    gc_all = bmm_hi((r >= c).astype(F32), g_all)
    gc_rows = [gc_all[i].T for i in range(DN_NCH)]
    lane3 = lax.broadcasted_iota(jnp.int32, (DN_NCH, C, HD), 2)
    row = lax.broadcasted_iota(jnp.int32, (HD, C), 0)
    ones = jnp.ones((1, HD), F32)
    gcs, gjs, betas = [], [], []
    for i in range(n_heads):
        h = first_head + i
        gcs.append(jnp.sum(jnp.where(lane3 == h, gc_all, 0.0), axis=2, keepdims=True) * ones)
        gjs.append(jnp.concatenate(
            [jnp.broadcast_to(jnp.sum(jnp.where(row == h, t, 0.0), axis=0, keepdims=True), (C, C))[None] for t in gc_rows],
            axis=0))
        beta = jnp.sum(jnp.where(lane == 8 + h, beta_all, 0.0), axis=1, keepdims=True) * ones
        betas.append(beta.reshape(DN_NCH, C, HD))
    return jnp.concatenate(gcs, axis=0), jnp.concatenate(gjs, axis=0), jnp.concatenate(betas, axis=0)


def _dn_prep(cq, ck, cv, gcum, gj, bb, t_saved=None):
    B, C = cq.shape[0], DN_C
    q = _l2n(jax.nn.silu(cq)) * (HD ** -0.5)
    k = _l2n(jax.nn.silu(ck))
    v = jax.nn.silu(cv)
    r = lax.broadcasted_iota(jnp.int32, (B, C, C), 1)
    c = lax.broadcasted_iota(jnp.int32, (B, C, C), 2)
    incl, strict = r >= c, r > c
    decay = jnp.where(incl, jnp.exp(jnp.where(incl, gcum[:, :, :C] - gj, 0.0)), 0.0)
    kb = k * bb
    n_mat = -jnp.where(strict, bmm_nt(kb, k) * decay, 0.0)
    t_mat = _neumann_inverse(n_mat) if t_saved is None else _saved_inverse(n_mat, t_saved)
    eg = jnp.exp(gcum)
    glast = gcum[:, C - 1:C, :]
    return (bmm(t_mat, v * bb), bmm(t_mat, kb * eg), bmm_nt(q, k) * decay, q * eg, k * jnp.exp(glast - gcum),
            jnp.exp(glast), t_mat)


def _dn_scan_step(u, w, qk, qd, kd, egl, S, onw):
    v_new = u - bmm(w, S)
    o = bmm(qd, S) + bmm(qk, v_new)
    return _rms(o, onw), S * egl + bmm_tn(kd, v_new)


def _to_batch(ref, n_heads):
    return jnp.concatenate([ref[:, i * HD:(i + 1) * HD].astype(F32).reshape(DN_NCH, DN_C, HD) for i in range(n_heads)],
                           axis=0)


def _from_batch(ref, val, n_heads):
    for i in range(n_heads):
        ref[:, i * HD:(i + 1) * HD] = val[i * DN_NCH:(i + 1) * DN_NCH].reshape(DN_TB, HD).astype(ref.dtype)


def _prep_specs(T, rev):
    nb = T // DN_TB
    blk = (lambda n: nb - 1 - n) if rev else (lambda n: n)
    ng = 8 // DN_NH
    head = [pl.BlockSpec((DN_TB, DN_NH * HD), functools.partial(lambda n, h, off: (blk(n), off + h), off=ng * s))
            for s in range(3)]
    ab = pl.BlockSpec((DN_TB, AB_PAD), lambda n, h: (blk(n), 3 * D // AB_PAD))
    row = pl.BlockSpec((1, HD), lambda n, h: (0, 0))
    wide = pl.BlockSpec((DN_TB, DN_NH * HD), lambda n, h: (blk(n), h))
    qk = pl.BlockSpec((DN_NCH, DN_NH, DN_C, DN_C), lambda n, h: (blk(n), h, 0, 0))
    eg = pl.BlockSpec((DN_NCH, DN_NH, 1, HD), lambda n, h: (blk(n), h, 0, 0))
    return nb, ng, head, ab, row, wide, qk, eg


def _dn_prep_fwd(cpre, pm, alog, dtb, name):
    T = cpre.shape[0]
    nb, ng, head, ab, row, wide, qks, egs = _prep_specs(T, False)

    def body(cq_ref, ck_ref, cv_ref, ab_ref, alog_ref, dtb_ref, u_ref, w_ref, qk_ref, qd_ref, kd_ref, e_ref, t_ref):
        gcum, gj, bb = _decay_terms(ab_ref[...], alog_ref[...], dtb_ref[...], pl.program_id(1) * DN_NH, DN_NH)
        u, w, qk, qd, kd, egl, t_mat = _dn_prep(_to_batch(cq_ref, DN_NH), _to_batch(ck_ref, DN_NH),
                                                _to_batch(cv_ref, DN_NH), gcum, gj, bb)
        _from_batch(u_ref, u, DN_NH)
        _from_batch(w_ref, w, DN_NH)
        _from_batch(qd_ref, qd, DN_NH)
        _from_batch(kd_ref, kd, DN_NH)
        for i in range(DN_NH):
            qk_ref[:, i] = qk[i * DN_NCH:(i + 1) * DN_NCH].astype(BF16)
            e_ref[:, i] = egl[i * DN_NCH:(i + 1) * DN_NCH]
            t_ref[:, i] = t_mat[i * DN_NCH:(i + 1) * DN_NCH]

    return pl.pallas_call(
        body, name=name, grid=(nb, ng), in_specs=head + [ab, row, row],
        out_specs=[wide, wide, qks, wide, wide, egs, qks],
        out_shape=[SDS((T, D), F32), SDS((T, D), BF16), SDS((T // DN_C, 8, DN_C, DN_C), BF16), SDS((T, D), BF16),
                   SDS((T, D), BF16), SDS((T // DN_C, 8, 1, HD), F32), SDS((T // DN_C, 8, DN_C, DN_C), F32)],
        compiler_params=_params(2, VMEM_LIMIT))(cpre, cpre, cpre, pm, alog, dtb)


def _dn_prep_bwd(du, dw, dqk, dqd, dkd, degl, t_mat, cpre, pm, alog, dtb, name):
    T = cpre.shape[0]
    nb, ng, head, ab, row, wide, qks, egs = _prep_specs(T, True)

    def body(du_ref, dw_ref, dqk_ref, dqd_ref, dkd_ref, de_ref, t_ref, cq_ref, ck_ref, cv_ref, ab_ref, alog_ref,
             dtb_ref, dcq_ref, dck_ref, dcv_ref, dab_ref, dalog_ref, ddtb_ref):
        n, h = pl.program_id(0), pl.program_id(1)

        @pl.when((n == 0) & (h == 0))
        def _():
            dalog_ref[...] = jnp.zeros_like(dalog_ref)
            ddtb_ref[...] = jnp.zeros_like(ddtb_ref)

        @pl.when(h == 0)
        def _():
            dab_ref[...] = jnp.zeros_like(dab_ref)

        t_saved = jnp.concatenate([t_ref[:, i] for i in range(DN_NH)], axis=0)

        def fwd(cq, ck, cv, ab_v, alog_v, dtb_v):
            gcum, gj, bb = _decay_terms(ab_v, alog_v, dtb_v, h * DN_NH, DN_NH)
            return _dn_prep(cq, ck, cv, gcum, gj, bb, t_saved)[:6]

        _, vjp = jax.vjp(fwd, _to_batch(cq_ref, DN_NH), _to_batch(ck_ref, DN_NH), _to_batch(cv_ref, DN_NH), ab_ref[...],
                         alog_ref[...], dtb_ref[...])
        cot = (_to_batch(du_ref, DN_NH), _to_batch(dw_ref, DN_NH),
               jnp.concatenate([dqk_ref[:, i] for i in range(DN_NH)], axis=0), _to_batch(dqd_ref, DN_NH),
               _to_batch(dkd_ref, DN_NH), jnp.concatenate([de_ref[:, i] for i in range(DN_NH)], axis=0))
        dcq, dck, dcv, dab, dalog, ddtb = vjp(cot)
        _from_batch(dcq_ref, dcq, DN_NH)
        _from_batch(dck_ref, dck, DN_NH)
        _from_batch(dcv_ref, dcv, DN_NH)
        dab_ref[...] += dab
        dalog_ref[...] += dalog
        ddtb_ref[...] += ddtb

    dabspec = pl.BlockSpec((DN_TB, AB_PAD), lambda n, h: (nb - 1 - n, 0))
    return pl.pallas_call(
        body, name=name, grid=(nb, ng),
        in_specs=[wide, wide, qks, wide, wide, egs, qks] + head + [ab, row, row],
        out_specs=[wide, wide, wide, dabspec, row, row],
        out_shape=[SDS((T, D), F32)] * 3 + [SDS((T, AB_PAD), F32)] + [SDS((1, HD), F32)] * 2,
        compiler_params=_params(2, VMEM_LIMIT))(du, dw, dqk, dqd, dkd, degl, t_mat, cpre, cpre, cpre, pm, alog, dtb)


def _scan_specs(T, rev):
    nb = T // DN_TB
    blk = (lambda n: nb - 1 - n) if rev else (lambda n: n)
    wide = pl.BlockSpec((DN_TB, D), lambda n: (blk(n), 0))
    qk = pl.BlockSpec((DN_NCH, 8, DN_C, DN_C), lambda n: (blk(n), 0, 0, 0))
    eg = pl.BlockSpec((DN_NCH, 8, 1, HD), lambda n: (blk(n), 0, 0, 0))
    st = pl.BlockSpec((DN_NCH, 8, HD, HD), lambda n: (blk(n), 0, 0, 0))
    row = pl.BlockSpec((1, HD), lambda n: (0, 0))
    return nb, wide, qk, eg, st, row


def _heads_of(ref, rows):
    return jnp.concatenate([ref[rows, h * HD:(h + 1) * HD].astype(F32)[None] for h in range(8)], axis=0)


def _dn_scan_fwd(u, w, qk, qd, kd, egl, onw, name):
    T = u.shape[0]
    nb, wide, qks, egs, sts, row = _scan_specs(T, False)

    def body(u_ref, w_ref, qk_ref, qd_ref, kd_ref, e_ref, onw_ref, o_ref, st_ref, s_scr):
        @pl.when(pl.program_id(0) == 0)
        def _():
            s_scr[...] = jnp.zeros_like(s_scr)
        S = s_scr[...]
        for c in range(DN_NCH):
            rows = slice(c * DN_C, (c + 1) * DN_C)
            st_ref[c] = S
            o, S = _dn_scan_step(_heads_of(u_ref, rows), _heads_of(w_ref, rows), qk_ref[c].astype(F32),
                                 _heads_of(qd_ref, rows), _heads_of(kd_ref, rows), e_ref[c], S, onw_ref[...])
            for h in range(8):
                o_ref[rows, h * HD:(h + 1) * HD] = o[h]
        s_scr[...] = S

    return pl.pallas_call(
        body, name=name, grid=(nb,), in_specs=[wide, wide, qks, wide, wide, egs, row], out_specs=[wide, sts],
        out_shape=[SDS((T, D), F32), SDS((T // DN_C, 8, HD, HD), F32)],
        scratch_shapes=[pltpu.VMEM((8, HD, HD), F32)],
        compiler_params=_params(1, VMEM_LIMIT))(u, w, qk, qd, kd, egl, onw)


def _dn_scan_bwd(do, u, w, qk, qd, kd, egl, st, onw, name):
    T = u.shape[0]
    nb, wide, qks, egs, sts, row = _scan_specs(T, True)

    def body(do_ref, u_ref, w_ref, qk_ref, qd_ref, kd_ref, e_ref, st_ref, onw_ref,
             du_ref, dw_ref, dqk_ref, dqd_ref, dkd_ref, de_ref, donw_ref, ds_scr):
        @pl.when(pl.program_id(0) == 0)
        def _():
            ds_scr[...] = jnp.zeros_like(ds_scr)
            donw_ref[...] = jnp.zeros_like(donw_ref)
        dS = ds_scr[...]
        donw = jnp.zeros((1, HD), F32)
        for c in reversed(range(DN_NCH)):
            rows = slice(c * DN_C, (c + 1) * DN_C)
            _, vjp = jax.vjp(_dn_scan_step, _heads_of(u_ref, rows), _heads_of(w_ref, rows), qk_ref[c].astype(F32),
                             _heads_of(qd_ref, rows), _heads_of(kd_ref, rows), e_ref[c], st_ref[c], onw_ref[...])
            du, dw, dqk, dqd, dkd, de, dS, dn = vjp((_heads_of(do_ref, rows), dS))
            for h in range(8):
                cols = slice(h * HD, (h + 1) * HD)
                du_ref[rows, cols] = du[h]
                dw_ref[rows, cols] = dw[h]
                dqd_ref[rows, cols] = dqd[h]
                dkd_ref[rows, cols] = dkd[h]
            dqk_ref[c] = dqk
            de_ref[c] = de
            donw += dn
        ds_scr[...] = dS
        donw_ref[...] += donw

    return pl.pallas_call(
        body, name=name, grid=(nb,), in_specs=[wide, wide, wide, qks, wide, wide, egs, sts, row],
        out_specs=[wide, wide, qks, wide, wide, egs, row],
        out_shape=[SDS((T, D), F32), SDS((T, D), F32), SDS((T // DN_C, 8, DN_C, DN_C), F32), SDS((T, D), F32),
                   SDS((T, D), F32), SDS((T // DN_C, 8, 1, HD), F32), SDS((1, HD), F32)],
        scratch_shapes=[pltpu.VMEM((8, HD, HD), F32)],
        compiler_params=_params(1, VMEM_LIMIT))(do, u, w, qk, qd, kd, egl, st, onw)


def _adamw(w, g, m, v, name):
    R, C = w.shape
    tr = 256 if R % 256 == 0 and R > 256 else R
    tc = 256 if tr == R and R > 256 and C % 256 == 0 else C
    c1 = 1.0 - ADAM_B1 ** ADAM_STEP
    c2 = 1.0 - ADAM_B2 ** ADAM_STEP

    def body(w_ref, g_ref, m_ref, v_ref, d_ref, nm_ref, nv_ref):
        gv = g_ref[...]
        nm = ADAM_B1 * m_ref[...] + (1.0 - ADAM_B1) * gv
        nv = ADAM_B2 * v_ref[...] + (1.0 - ADAM_B2) * (gv * gv)
        nm_ref[...] = nm
        nv_ref[...] = nv
        d_ref[...] = -ADAM_LR * ((nm / c1) / (jnp.sqrt(nv / c2) + ADAM_EPS) + ADAM_WD * w_ref[...])

    spec = pl.BlockSpec((tr, tc), lambda i, j: (i, j))
    return pl.pallas_call(
        body, name=name, grid=(R // tr, C // tc), in_specs=[spec] * 4, out_specs=[spec] * 3,
        out_shape=[SDS((R, C), F32)] * 3, compiler_params=_params(2, VMEM_LIMIT))(w, g, m, v)


def _local_step(x, mem, target, wts, sm):
    kinds = [i % 3 for i in range(DEPTH)]
    mnw = sm["mem_norm_w"].reshape(1, D)
    kv, wkv = None, None
    saved, blocks = [], []
    for i, kind in enumerate(kinds):
        j = i // 3
        npre = sm["norm_pre"][i].reshape(1, D)
        npost = sm["norm_post"][i].reshape(1, D)
        mix, gate, wo_after = wts["blocks"](i, x)
        pm, pg, h = _inproj_fwd(x, npre, mix, gate, kind == 2, f"inproj_fwd_{i}")
        if kv is None:
            wkv = wts["wkv_after"](h)
            kv = _memkv_fwd(mem, mnw, wkv)
        extra = None
        if kind == 0:
            bs3 = jnp.broadcast_to(sm["a_b_s"][j][:, :, None], (8, HD, HD))
            ymix = _gmlp_fwd(pm, sm["a_ln_w"][j].reshape(1, D), sm["a_ln_b"][j].reshape(1, D), sm["a_w_s"][j], bs3,
                             f"gmlp_fwd_{i}")
            extra = bs3
        elif kind == 1:
            ymix = _sconv_fwd(pm, sm["b_conv_w"][j], f"sconv_fwd_{i}")
        else:
            cpre = _dnconv_fwd(pm, sm["c_conv_w"][j], f"dnconv_fwd_{i}")
            alog = jnp.pad(sm["c_a_log"][j], (0, HD - 8)).reshape(1, HD)
            dtb = jnp.pad(sm["c_dt_bias"][j], (0, HD - 8)).reshape(1, HD)
            onw = sm["c_o_norm_w"][j].reshape(1, HD)
            *prep, t_mat = _dn_prep_fwd(cpre, pm, alog, dtb, f"dn_prep_fwd_{i}")
            ymix, st = _dn_scan_fwd(*prep, onw, f"dn_scan_fwd_{i}")
            extra = (cpre, prep, t_mat, st, alog, dtb, onw)
        ycat = _ag_fwd(ymix, pg, kv, f"ag_fwd_{i}")
        wo = wo_after(ycat)
        blocks.append((mix, gate, wo))
        o, xn = _outproj_fwd(ycat, wo, x, npost, f"outproj_fwd_{i}")
        saved.append((x, h, pm, pg, ymix, ycat, o, extra))
        x = xn

    loss, dx = _loss_head(x, target)

    g = {"wm": [None] * DEPTH, "wg": [None] * DEPTH, "wo": [None] * DEPTH, "norm_pre": [None] * DEPTH,
         "norm_post": [None] * DEPTH}
    dkv = jnp.zeros((N_MEM, 2 * D_XA), F32)
    sent = 0.0
    for i in reversed(range(DEPTH)):
        kind, j = kinds[i], i // 3
        xi, h, pm, pg, ymix, ycat, o, extra = saved[i]
        npre = sm["norm_pre"][i].reshape(1, D)
        npost = sm["norm_post"][i].reshape(1, D) + sent
        dycat, dobf, g["norm_post"][i] = _outproj_bwd(dx, o, npost, blocks[i][2], f"outproj_bwd_{i}")
        g["wo"][i] = _matmul_tn(ycat, dobf, f"dwo_{i}")
        dymix, dpg, dkv = _ag_bwd(dycat, ymix, pg, kv, dkv, f"ag_bwd_{i}")
        if kind == 0:
            dpm, dlnw, dlnb, dws, dbs3 = _gmlp_bwd(dymix, pm, sm["a_ln_w"][j].reshape(1, D),
                                                   sm["a_ln_b"][j].reshape(1, D), sm["a_w_s"][j], extra,
                                                   f"gmlp_bwd_{i}")
            g.setdefault("a_ln_w", {})[j] = dlnw.reshape(D)
            g.setdefault("a_ln_b", {})[j] = dlnb.reshape(D)
            g.setdefault("a_w_s", {})[j] = dws
            g.setdefault("a_b_s", {})[j] = dbs3[:, :, 0]
        elif kind == 1:
            dpm, dcw = _sconv_bwd(dymix, pm, sm["b_conv_w"][j], f"sconv_bwd_{i}")
            g.setdefault("b_conv_w", {})[j] = dcw
        else:
            cpre, prep, t_mat, st, alog, dtb, onw = extra
            *dprep, donw = _dn_scan_bwd(dymix, *prep, st, onw, f"dn_scan_bwd_{i}")
            dcq, dck, dcv, dab, dalog, ddtb = _dn_prep_bwd(*dprep, t_mat, cpre, pm, alog, dtb, f"dn_prep_bwd_{i}")
            dpm, dcw = _dnconv_bwd(dcq, dck, dcv, dab, pm, sm["c_conv_w"][j], f"dnconv_bwd_{i}")
            g.setdefault("c_conv_w", {})[j] = dcw
            g.setdefault("c_a_log", {})[j] = dalog[0, :8]
            g.setdefault("c_dt_bias", {})[j] = ddtb[0, :8]
            g.setdefault("c_o_norm_w", {})[j] = donw[0]
        if kind == 2:
            g["wm"][i] = _matmul_tn(dpm, h, f"dwm_{i}")
            g["wg"][i] = _matmul_tn(dpg, h, f"dwg_{i}")
        else:
            g["wm"][i] = _matmul_tn(h, dpm, f"dwm_{i}", GRAD_TILE[kind])
            g["wg"][i] = _matmul_tn(h, dpg, f"dwg_{i}", GRAD_TILE[kind])
        sent = wts["layer_done"](i, g)
        dx, g["norm_pre"][i] = _inproj_bwd(dpm, dpg, xi, npre + sent, blocks[i][0], blocks[i][1], kind == 2, dx,
                                           f"inproj_bwd_{i}")
    g["mem_norm_w"], g["wkv"] = _memkv_bwd(mem, mnw, wkv, dkv)
    return loss[0, 0], dx, g


ANY = pl.BlockSpec(memory_space=pl.ANY)


def _place():
    return lax.axis_index("x"), lax.axis_index("y"), lax.axis_index("c")


def _add(a, b, name):
    def body(a_ref, b_ref, o_ref):
        o_ref[...] = a_ref[...] + b_ref[...]

    return pl.pallas_call(body, name=name, out_shape=SDS(a.shape, a.dtype), compiler_params=_params(0, VMEM_LIMIT))(a, b)


def _sum4(own, land):
    def body(own_ref, l_ref, o_ref):
        chip = 2 * lax.axis_index("x") + lax.axis_index("y")
        acc = jnp.where(chip == 0, own_ref[...], l_ref[0])
        for s in range(1, 4):
            acc = acc + jnp.where(chip == s, own_ref[...], l_ref[s])
        o_ref[...] = acc

    return pl.pallas_call(body, name="sum_small", out_shape=SDS(own.shape, own.dtype),
                          compiler_params=_params(0, VMEM_LIMIT))(own, land)


C_ROWS = 1312
_REDUCE_CHUNK = {512: 256, 2560: 640}
W_CLASSES = {"in0": (512, 256)}


def _chunk_list(specs):
    return [(k, r, chunk) for k, (half, chunk) in enumerate(specs) for r in range(0, half, chunk)]


def _gather_classes(arrs, specs, vec):
    n = len(arrs)
    chunks = _chunk_list(specs)
    nc = len(chunks)

    def body(*refs):
        ins, vec_ref, outs, ov_ref = refs[:n], refs[n], refs[n + 1:2 * n + 1], refs[2 * n + 1]
        ici_send, ici_recv, d2d_send, d2d_recv, vec_send, vec_recv = refs[2 * n + 2:]
        x, y, c = _place()
        chip = 2 * x + y
        peers = [(1 - x, y), (x, 1 - y), (1 - x, 1 - y)]

        def rows(ci, half):
            k, r, cnt = chunks[ci]
            return k, pl.ds(half * specs[k][0] + r, cnt)

        def over_ici(j, ci, slab):
            px, py = peers[j]
            k, rs = rows(ci, c)
            return pltpu.make_async_remote_copy(
                src_ref=ins[k].at[rs], dst_ref=outs[k].at[slab, rs], send_sem=ici_send.at[j * nc + ci],
                recv_sem=ici_recv.at[j * nc + ci], device_id=(px, py, c), device_id_type=MESH)

        def over_d2d(j, ci, half):
            px, py = peers[j]
            k, rs = rows(ci, half)
            where = outs[k].at[2 * px + py, rs]
            return pltpu.make_async_remote_copy(
                src_ref=where, dst_ref=where, send_sem=d2d_send.at[j * nc + ci], recv_sem=d2d_recv.at[j * nc + ci],
                device_id=(x, y, 1 - c), device_id_type=MESH)

        def small(j, slab):
            px, py = peers[j]
            return pltpu.make_async_remote_copy(
                src_ref=vec_ref, dst_ref=ov_ref.at[slab], send_sem=vec_send.at[j], recv_sem=vec_recv.at[j],
                device_id=(px, py, c), device_id_type=MESH)

        sends = [small(j, chip) for j in range(3)] + [over_ici(j, ci, chip) for ci in range(nc) for j in range(3)]
        for cp in sends:
            cp.start()
        forwards = []
        for ci in range(nc):
            for j, (px, py) in enumerate(peers):
                over_ici(j, ci, 2 * px + py).wait_recv()
                forwards.append(over_d2d(j, ci, c))
                forwards[-1].start()
        for ci in range(nc):
            for j in range(3):
                over_d2d(j, ci, 1 - c).wait_recv()
        for j, (px, py) in enumerate(peers):
            small(j, 2 * px + py).wait_recv()
        for cp in sends + forwards:
            cp.wait_send()

    dma = pltpu.SemaphoreType.DMA
    return pl.pallas_call(
        body, name="gather_weights", in_specs=[ANY] * (n + 1), out_specs=[ANY] * (n + 1),
        out_shape=[SDS((4,) + a.shape, a.dtype) for a in arrs] + [SDS((4,) + vec.shape, vec.dtype)],
        scratch_shapes=[dma((3 * nc,)), dma((3 * nc,)), dma((3 * nc,)), dma((3 * nc,)), dma((3,)), dma((3,))])(*arrs, vec)


def _swap_classes(grads, specs, small):
    n = len(grads)
    chunks = _chunk_list(specs)

    def body(*refs):
        ins, s_ref, outs, os_ref, send_sems, recv_sems = refs[:n], refs[n], refs[n + 1:2 * n + 1], *refs[2 * n + 1:]
        x, y, c = _place()
        copies = []
        for s in range(4):
            for k, r, cnt in chunks:
                copies.append(pltpu.make_async_remote_copy(
                    src_ref=ins[k].at[s, 1 - c, pl.ds(r, cnt)], dst_ref=outs[k].at[s, pl.ds(r, cnt)],
                    send_sem=send_sems.at[len(copies)], recv_sem=recv_sems.at[len(copies)],
                    device_id=(x, y, 1 - c), device_id_type=MESH))
        copies.append(pltpu.make_async_remote_copy(
            src_ref=s_ref, dst_ref=os_ref, send_sem=send_sems.at[len(copies)], recv_sem=recv_sems.at[len(copies)],
            device_id=(x, y, 1 - c), device_id_type=MESH))
        for cp in copies:
            cp.start()
        for cp in copies:
            cp.wait_recv()
        for cp in copies:
            cp.wait_send()

    m = 4 * len(chunks) + 1
    return pl.pallas_call(
        body, name="swap_halves", in_specs=[ANY] * (n + 1), out_specs=[ANY] * (n + 1),
        out_shape=[SDS((4, g.shape[2], g.shape[3]), F32) for g in grads] + [SDS(small.shape, F32)],
        scratch_shapes=[pltpu.SemaphoreType.DMA((m,)), pltpu.SemaphoreType.DMA((m,))])(*grads, small)


def _pair_sum_class(g, other, chunk, name):
    _, _, half, w = g.shape

    def body(g_ref, o_ref, pb_ref, own_ref):
        x, y, c = _place()
        v = jnp.where(c == 0, g_ref[0], g_ref[1]) + o_ref[...]
        pb_ref[...] = v.astype(BF16)

        @pl.when(pl.program_id(1) == 2 * x + y)
        def _():
            own_ref[...] = v

    return pl.pallas_call(
        body, name=name, grid=(half // chunk, 4),
        in_specs=[pl.BlockSpec((None, 2, chunk, w), lambda i, s: (s, 0, i, 0)),
                  pl.BlockSpec((None, chunk, w), lambda i, s: (s, i, 0))],
        out_specs=[pl.BlockSpec((None, chunk, w), lambda i, s: (s, i, 0)), pl.BlockSpec((chunk, w), lambda i, s: (i, 0))],
        out_shape=[SDS((4, half, w), BF16), SDS((half, w), F32)],
        compiler_params=_params(2, VMEM_LIMIT))(g, other)


def _exchange_classes(pbs, specs, ps):
    n = len(pbs)
    chunks = _chunk_list(specs)
    per_peer = len(chunks) + 1

    def body(*refs):
        ins, ps_ref, outs, ls_ref, send_sems, recv_sems = refs[:n], refs[n], refs[n + 1:2 * n + 1], *refs[2 * n + 1:]
        x, y, c = _place()
        chip = 2 * x + y
        peers = [(1 - x, y), (x, 1 - y), (1 - x, 1 - y)]

        def copies(slab_of):
            out = []
            for j, (px, py) in enumerate(peers):
                for k, r, cnt in chunks:
                    out.append(pltpu.make_async_remote_copy(
                        src_ref=ins[k].at[2 * px + py, pl.ds(r, cnt)], dst_ref=outs[k].at[slab_of(j), pl.ds(r, cnt)],
                        send_sem=send_sems.at[len(out)], recv_sem=recv_sems.at[len(out)], device_id=(px, py, c),
                        device_id_type=MESH))
                out.append(pltpu.make_async_remote_copy(
                    src_ref=ps_ref, dst_ref=ls_ref.at[slab_of(j)], send_sem=send_sems.at[len(out)],
                    recv_sem=recv_sems.at[len(out)], device_id=(px, py, c), device_id_type=MESH))
            return out

        sends = copies(lambda j: chip)
        for cp in sends:
            cp.start()
        for cp in copies(lambda j: 2 * peers[j][0] + peers[j][1]):
            cp.wait_recv()
        for cp in sends:
            cp.wait_send()

    m = 3 * per_peer
    return pl.pallas_call(
        body, name="chip_exchange", in_specs=[ANY] * (n + 1), out_specs=[ANY] * (n + 1),
        out_shape=[SDS(p.shape, BF16) for p in pbs] + [SDS((4,) + ps.shape, F32)],
        scratch_shapes=[pltpu.SemaphoreType.DMA((m,)), pltpu.SemaphoreType.DMA((m,))])(*pbs, ps)


def _chip_sum_class(own, land, chunk, name):
    half, w = own.shape

    def body(own_ref, l_ref, o_ref):
        chip = 2 * lax.axis_index("x") + lax.axis_index("y")
        acc = jnp.where(chip == 0, own_ref[...], l_ref[0].astype(F32))
        for s in range(1, 4):
            acc = acc + jnp.where(chip == s, own_ref[...], l_ref[s].astype(F32))
        o_ref[...] = acc

    return pl.pallas_call(
        body, name=name, grid=(half // chunk,),
        in_specs=[pl.BlockSpec((chunk, w), lambda i: (i, 0)), pl.BlockSpec((4, chunk, w), lambda i: (0, i, 0))],
        out_specs=pl.BlockSpec((chunk, w), lambda i: (i, 0)), out_shape=SDS((half, w), F32),
        compiler_params=_params(1, VMEM_LIMIT))(own, land)


def _share_classes(rs, specs):
    n = len(rs)
    chunks = _chunk_list(specs)

    def body(*refs):
        ins, outs, send_sems, recv_sems = refs[:n], refs[n:2 * n], *refs[2 * n:]
        x, y, c = _place()
        copies = [pltpu.make_async_remote_copy(
            src_ref=ins[k].at[pl.ds(r, cnt)], dst_ref=outs[k].at[pl.ds(r, cnt)], send_sem=send_sems.at[i],
            recv_sem=recv_sems.at[i], device_id=(x, y, 1 - c), device_id_type=MESH)
            for i, (k, r, cnt) in enumerate(chunks)]
        for cp in copies:
            cp.start()
        for cp in copies:
            cp.wait_recv()
        for cp in copies:
            cp.wait_send()

    m = len(chunks)
    return pl.pallas_call(
        body, name="share_half", in_specs=[ANY] * n, out_specs=[ANY] * n, out_shape=[SDS(r.shape, F32) for r in rs],
        scratch_shapes=[pltpu.SemaphoreType.DMA((m,)), pltpu.SemaphoreType.DMA((m,))])(*rs)


_HBM = pl.BlockSpec(memory_space=pltpu.HBM)
_SEM = pl.BlockSpec(memory_space=pltpu.SEMAPHORE)
_EFFECT = pltpu.SideEffectType.DATAFLOW_SIDE_EFFECTING


def _chip_peers():
    x, y, c = _place()
    return [(1 - x, y, c), (x, 1 - y, c), (1 - x, 1 - y, c)]


def _send_shard_start(v, name):
    def body(v_ref, land_ref, send_sems, recv_sems, v_thru, land_thru, token):
        x, y, c = _place()
        for j, peer in enumerate(_chip_peers()):
            pltpu.make_async_remote_copy(src_ref=v_ref, dst_ref=land_ref.at[2 * x + y], send_sem=send_sems.at[j],
                                         recv_sem=recv_sems.at[j], device_id=peer, device_id_type=MESH).start()
        token[...] = jnp.zeros_like(token)

    land_shape = (4,) + v.shape
    return pl.pallas_call(
        body, name=name,
        out_shape=(pltpu.SemaphoreType.DMA((3,)), pltpu.SemaphoreType.DMA((3,)), pltpu.HBM(v.shape, v.dtype),
                   pltpu.HBM(land_shape, v.dtype), SDS((8, 128), F32)),
        in_specs=(_HBM, _HBM), out_specs=(_SEM, _SEM, _HBM, _HBM, pl.BlockSpec(memory_space=pltpu.VMEM)),
        input_output_aliases={0: 2, 1: 3}, compiler_params=pltpu.CompilerParams(has_side_effects=_EFFECT),
    )(pltpu.with_memory_space_constraint(v, pltpu.HBM),
      pltpu.with_memory_space_constraint(lax.empty(land_shape, v.dtype), pltpu.HBM))


def _xor_peer(r):
    x, y, c = _place()
    return (1 - x if (r >> 2) & 1 else x, 1 - y if (r >> 1) & 1 else y, 1 - c if r & 1 else c)


def _send_pieces_start(parts, name):
    n = len(parts)

    def body(*refs):
        ins, lands = refs[:n], refs[n:2 * n]
        send_sems, recv_sems = refs[2 * n:2 * n + 2]
        token = refs[-1]
        x, y, c = _place()
        for r in range(1, 8):
            px, py, pc = _xor_peer(r)
            for k in range(n):
                pltpu.make_async_remote_copy(
                    src_ref=ins[k].at[2 * px + py, pc], dst_ref=lands[k].at[4 * x + 2 * y + c],
                    send_sem=send_sems.at[(r - 1) * n + k], recv_sem=recv_sems.at[(r - 1) * n + k],
                    device_id=(px, py, pc), device_id_type=MESH).start()
        token[...] = jnp.zeros_like(token)

    land_shapes = [(8,) + p.shape[2:] for p in parts]
    hbm = [pltpu.HBM(p.shape, p.dtype) for p in parts] + [pltpu.HBM(s, p.dtype) for s, p in zip(land_shapes, parts)]
    operands = [pltpu.with_memory_space_constraint(p, pltpu.HBM) for p in parts]
    operands += [pltpu.with_memory_space_constraint(lax.empty(s, p.dtype), pltpu.HBM) for s, p in zip(land_shapes, parts)]
    return pl.pallas_call(
        body, name=name,
        out_shape=(pltpu.SemaphoreType.DMA((7 * n,)), pltpu.SemaphoreType.DMA((7 * n,)), *hbm, SDS((8, 128), F32)),
        in_specs=(_HBM,) * (2 * n), out_specs=(_SEM, _SEM) + (_HBM,) * (2 * n) + (pl.BlockSpec(memory_space=pltpu.VMEM),),
        input_output_aliases={i: 2 + i for i in range(2 * n)},
        compiler_params=pltpu.CompilerParams(has_side_effects=_EFFECT))(*operands)


def _send_pieces_wait(started, after, name):
    send_sems, recv_sems, *thru, _ = started
    n = len(thru) // 2

    def body(*refs):
        ins, lands = refs[:n], refs[n:2 * n]
        send_sems, recv_sems = refs[2 * n:2 * n + 2]
        for r in range(1, 8):
            px, py, pc = _xor_peer(r)
            for k in range(n):
                copy = pltpu.make_async_remote_copy(
                    src_ref=ins[k].at[2 * px + py, pc], dst_ref=lands[k].at[4 * px + 2 * py + pc],
                    send_sem=send_sems.at[(r - 1) * n + k], recv_sem=recv_sems.at[(r - 1) * n + k],
                    device_id=(px, py, pc), device_id_type=MESH)
                copy.wait_send()
                copy.wait_recv()

    return pl.pallas_call(
        body, name=name, out_shape=tuple(pltpu.HBM(t.shape, t.dtype) for t in thru),
        in_specs=(_HBM,) * (2 * n) + (_SEM, _SEM, pl.BlockSpec(memory_space=pl.ANY)), out_specs=(_HBM,) * (2 * n),
        input_output_aliases={i: i for i in range(2 * n)},
        compiler_params=pltpu.CompilerParams(has_side_effects=_EFFECT))(*thru, send_sems, recv_sems, after)[n:]


def _sum8_class(own, land, chunk, name):
    rows, w = own.shape

    def body(own_ref, l_ref, o_ref):
        x, y, c = _place()
        me = 4 * x + 2 * y + c
        acc = jnp.where(me == 0, own_ref[...], l_ref[0].astype(F32))
        for d in range(1, 8):
            acc = acc + jnp.where(me == d, own_ref[...], l_ref[d].astype(F32))
        o_ref[...] = acc

    return pl.pallas_call(
        body, name=name, grid=(rows // chunk,),
        in_specs=[pl.BlockSpec((chunk, w), lambda i: (i, 0)), pl.BlockSpec((8, chunk, w), lambda i: (0, i, 0))],
        out_specs=pl.BlockSpec((chunk, w), lambda i: (i, 0)), out_shape=SDS((rows, w), F32),
        compiler_params=_params(1, VMEM_LIMIT))(own, land)


def _send_shard_wait(send_sems, recv_sems, v_thru, land_thru, after, name):
    def body(v_ref, land_ref, send_sems, recv_sems, after_ref, v_dead, got_ref):
        for j, (px, py, pc) in enumerate(_chip_peers()):
            copy = pltpu.make_async_remote_copy(src_ref=v_ref, dst_ref=land_ref.at[2 * px + py], send_sem=send_sems.at[j],
                                                recv_sem=recv_sems.at[j], device_id=(px, py, pc), device_id_type=MESH)
            copy.wait_send()
            copy.wait_recv()

    return pl.pallas_call(
        body, name=name,
        out_shape=(pltpu.HBM(v_thru.shape, v_thru.dtype), pltpu.HBM(land_thru.shape, land_thru.dtype)),
        in_specs=(_HBM, _HBM, _SEM, _SEM, pl.BlockSpec(memory_space=pl.ANY)), out_specs=(_HBM, _HBM),
        input_output_aliases={0: 0, 1: 1}, compiler_params=pltpu.CompilerParams(has_side_effects=_EFFECT),
    )(v_thru, land_thru, send_sems, recv_sems, after)[1]


_SMALL = ["mem_norm_w", "norm_pre", "norm_post", "a_ln_w", "a_ln_b", "a_w_s", "a_b_s", "b_conv_w", "c_conv_w",
          "c_a_log", "c_dt_bias", "c_o_norm_w"]
_SMALL_SHAPES = {"mem_norm_w": (D,), "norm_pre": (4, D), "norm_post": (4, D), "a_ln_w": (2, D), "a_ln_b": (2, D),
                 "a_w_s": (2, 8, HD, HD), "a_b_s": (2, 8, HD), "b_conv_w": (1, 3, D), "c_conv_w": (1, 4, 3 * D),
                 "c_a_log": (1, 8), "c_dt_bias": (1, 8), "c_o_norm_w": (1, HD)}
_SHARDED_SMALL = {"a_ln_w": D // 4, "a_ln_b": D // 4, "b_conv_w": D // 4, "c_conv_w": 3 * D // 4}
_SMALL_ROWS = 288


def _size(shape):
    n = 1
    for d in shape:
        n *= d
    return n


def kernel(x, mem, mem_norm_w, w_mem_kv, norm_pre, norm_post, w_out, a_w_in, a_ln_w, a_ln_b, a_w_s, a_b_s, b_w_in, b_conv_w, c_w_in, c_conv_w, c_a_log, c_dt_bias, c_o_norm_w, loss_target, m_mem_norm_w, m_w_mem_kv, m_norm_pre, m_norm_post, m_w_out, m_a_w_in, m_a_ln_w, m_a_ln_b, m_a_w_s, m_a_b_s, m_b_w_in, m_b_conv_w, m_c_w_in, m_c_conv_w, m_c_a_log, m_c_dt_bias, m_c_o_norm_w, v_mem_norm_w, v_w_mem_kv, v_norm_pre, v_norm_post, v_w_out, v_a_w_in, v_a_ln_w, v_a_ln_b, v_a_w_s, v_a_b_s, v_b_w_in, v_b_conv_w, v_c_w_in, v_c_conv_w, v_c_a_log, v_c_dt_bias, v_c_o_norm_w):
    names = ["mem_norm_w", "w_mem_kv", "norm_pre", "norm_post", "w_out", "a_w_in", "a_ln_w", "a_ln_b", "a_w_s", "a_b_s",
             "b_w_in", "b_conv_w", "c_w_in", "c_conv_w", "c_a_log", "c_dt_bias", "c_o_norm_w"]
    w = dict(zip(names, [mem_norm_w, w_mem_kv, norm_pre, norm_post, w_out, a_w_in, a_ln_w, a_ln_b, a_w_s, a_b_s, b_w_in,
                         b_conv_w, c_w_in, c_conv_w, c_a_log, c_dt_bias, c_o_norm_w]))
    m = dict(zip(names, [m_mem_norm_w, m_w_mem_kv, m_norm_pre, m_norm_post, m_w_out, m_a_w_in, m_a_ln_w, m_a_ln_b, m_a_w_s,
                         m_a_b_s, m_b_w_in, m_b_conv_w, m_c_w_in, m_c_conv_w, m_c_a_log, m_c_dt_bias, m_c_o_norm_w]))
    v = dict(zip(names, [v_mem_norm_w, v_w_mem_kv, v_norm_pre, v_norm_post, v_w_out, v_a_w_in, v_a_ln_w, v_a_ln_b, v_a_w_s,
                         v_a_b_s, v_b_w_in, v_b_conv_w, v_c_w_in, v_c_conv_w, v_c_a_log, v_c_dt_bias, v_c_o_norm_w]))
    chip = 2 * lax.axis_index("x") + lax.axis_index("y")

    def rows_of_ct(a):
        return a[0].T

    def with_mine(gathered, own):
        return lax.dynamic_update_slice(gathered, own[None], (chip,) + (0,) * own.ndim)

    first = [a_w_in[0].astype(BF16)]
    vec = jnp.concatenate([a_ln_w.reshape(-1), a_ln_b.reshape(-1), b_conv_w.reshape(-1), c_conv_w.reshape(-1)])
    vec = jnp.pad(vec, (0, 8 * D - vec.shape[0])).reshape(8, D)
    ga0, gvec = _gather_classes(first, [W_CLASSES[k] for k in W_CLASSES], vec)
    ga0, gvec = with_mine(ga0, first[0]), with_mine(gvec, vec)
    gv = gvec.reshape(4, 8 * D)
    later = {("in", 1): b_w_in[0], ("in", 2): jnp.pad(rows_of_ct(c_w_in), ((0, C_ROWS - 1284), (0, 0))),
             ("in", 3): a_w_in[1], ("kv", 0): w_mem_kv}
    later.update({("out", i): w_out[i] for i in range(DEPTH)})
    later = {k: (a + 0.0 * ga0[0, 0, 0].astype(F32)).astype(BF16) for k, a in later.items()}
    sent = {k: _send_shard_start(a, f"send_w_{k[0]}_{k[1]}") for k, a in later.items()}
    started = sum(s[4][0, 0] for s in sent.values())

    def arrived(k, after):
        return with_mine(_send_shard_wait(*sent[k][:4], after, f"wait_w_{k[0]}_{k[1]}"), later[k])

    def blocks(i, after):
        def wo_after(later_value):
            return arrived(("out", i), later_value).reshape(D_CAT, D)

        if i == 0:
            return [ga0[0], ga0[1]], [ga0[2], ga0[3]], wo_after
        got = arrived(("in", i), after)
        if i == 1:
            return [got[0], got[1], got[2][:, :512]], [got[2][:, 512:], got[3]], wo_after
        if i == 3:
            return [got[0], got[1]], [got[2], got[3]], wo_after
        fct = got[:, :1284].reshape(5136, D)
        c_ab = jnp.concatenate([fct[3 * D:3 * D + 16], jnp.zeros((AB_PAD - 16, D), BF16)], axis=0)
        return [fct[:3 * D], c_ab], [fct[3 * D + 16:]], wo_after
    sm = {"mem_norm_w": mem_norm_w, "norm_pre": norm_pre + started, "norm_post": norm_post, "a_w_s": a_w_s, "a_b_s": a_b_s,
          "c_a_log": c_a_log, "c_dt_bias": c_dt_bias, "c_o_norm_w": c_o_norm_w,
          "a_ln_w": gv[:, 0:512].reshape(4, 2, 256).transpose(1, 0, 2).reshape(2, D),
          "a_ln_b": gv[:, 512:1024].reshape(4, 2, 256).transpose(1, 0, 2).reshape(2, D),
          "b_conv_w": gv[:, 1024:1792].reshape(4, 1, 3, 256).transpose(1, 2, 0, 3).reshape(1, 3, D),
          "c_conv_w": gv[:, 1792:4864].reshape(4, 1, 4, 768).transpose(1, 2, 0, 3).reshape(1, 4, 3 * D)}
    wts = {"wkv_after": lambda after: arrived(("kv", 0), after).reshape(D, 2 * D_XA), "blocks": blocks}

    def layer_grads(i, g):
        if i % 3 == 2:
            gct = jnp.concatenate([g["wm"][i][:3 * D + 16], g["wg"][i]], axis=0).reshape(4, 1284, D)
            w_in = jnp.pad(gct, ((0, 0), (0, C_ROWS - 1284), (0, 0)))
        else:
            w_in = jnp.concatenate([g["wm"][i], g["wg"][i]], axis=0).reshape(4, -1, GRAD_TILE[i % 3])
        out = {f"in{i}": w_in, f"out{i}": g["wo"][i].reshape(4, 384, D)}
        return {k: a.reshape(4, 2, a.shape[1] // 2, a.shape[2]) for k, a in out.items()}

    pending = {}

    def layer_done(i, g):
        halves = layer_grads(i, g)
        started = _send_pieces_start([h.astype(BF16) for h in halves.values()], f"send_grads_{i}")
        pending[i] = (started, halves)
        return started[-1][0, 0]

    wts["layer_done"] = layer_done

    loss, dx, g = _local_step(x[0], mem[0], loss_target[0], wts, sm)
    loss = lax.psum(loss, ("x", "y", "c"))

    core = lax.axis_index("c")
    mine, specs = {}, {}
    for i in reversed(range(DEPTH)):
        started, halves = pending[i]
        lands = _send_pieces_wait(started, dx, f"wait_grads_{i}")
        for (k, h), land in zip(halves.items(), lands):
            own = lax.dynamic_index_in_dim(lax.dynamic_index_in_dim(h, chip, 0, False), core, 0, False)
            specs[k] = (own.shape[0], _REDUCE_CHUNK.get(own.shape[0], own.shape[0]))
            mine[k] = _sum8_class(own, land, specs[k][1], f"sum8_{k}")

    first = {"kv": g["wkv"].reshape(4, 2, 128, D)}
    first_specs = [(h.shape[2], _REDUCE_CHUNK.get(h.shape[2], h.shape[2])) for h in first.values()]
    halves = list(first.values())
    gs = {"mem_norm_w": g["mem_norm_w"], "norm_pre": jnp.concatenate(g["norm_pre"]),
          "norm_post": jnp.concatenate(g["norm_post"])}
    for n in _SMALL[3:]:
        gs[n] = jnp.stack([g[n][j] for j in sorted(g[n])])
    flat = jnp.concatenate([gs[n].reshape(-1) for n in _SMALL])
    small = jnp.pad(flat, (0, _SMALL_ROWS * D - flat.shape[0])).reshape(_SMALL_ROWS, D)
    *others, other_small = _swap_classes(halves, first_specs, small)
    pairs = [_pair_sum_class(h, o, s[1], f"pair_sum_{k}") for k, h, o, s in zip(first, halves, others, first_specs)]
    pair_small = _add(small, other_small, "pair_sum_small")
    *lands, land_small = _exchange_classes([p[0] for p in pairs], first_specs, pair_small)
    for k, p, land, s in zip(first, pairs, lands, first_specs):
        mine[k], specs[k] = _chip_sum_class(p[1], land, s[1], f"chip_sum_{k}"), s
    theirs = _share_classes(list(mine.values()), [specs[k] for k in mine])
    south = core == 0
    sh = {k: jnp.concatenate([jnp.where(south, a, b), jnp.where(south, b, a)], axis=0)
          for (k, a), b in zip(mine.items(), theirs)}
    grads = {"a_w_in": jnp.stack([sh["in0"], sh["in3"]]),
             "b_w_in": sh["in1"].reshape(5, D, 256).transpose(1, 0, 2).reshape(b_w_in.shape),
             "c_w_in": sh["in2"][:1284], "w_out": jnp.stack([sh[f"out{i}"] for i in range(DEPTH)]),
             "w_mem_kv": sh["kv"]}
    flat = _sum4(pair_small, land_small).reshape(-1)
    off = 0
    for n in _SMALL:
        shape = _SMALL_SHAPES[n]
        full = flat[off:off + _size(shape)].reshape(shape)
        off += _size(shape)
        if n in _SHARDED_SMALL:
            full = lax.dynamic_slice_in_dim(full, chip * _SHARDED_SMALL[n], _SHARDED_SMALL[n], axis=len(shape) - 1)
        grads[n] = full

    delta, new_m, new_v = {}, {}, {}
    for n in names:
        shape = w[n].shape
        if n == "c_w_in":
            d_, m_, v_ = _adamw(rows_of_ct(w[n]), grads[n], rows_of_ct(m[n]), rows_of_ct(v[n]), f"adamw_{n}")
            delta[n], new_m[n], new_v[n], grads[n] = d_.T[None], m_.T[None], v_.T[None], grads[n].T[None]
            continue
        view = (1, shape[0]) if len(shape) == 1 else (_size(shape[:-1]), shape[-1])
        d_, m_, v_ = _adamw(w[n].reshape(view), grads[n].reshape(view), m[n].reshape(view), v[n].reshape(view),
                            f"adamw_{n}")
        delta[n], new_m[n], new_v[n] = d_.reshape(shape), m_.reshape(shape), v_.reshape(shape)
    return (loss, dx[None], *[grads[n].reshape(w[n].shape) for n in names], *[delta[n] for n in names],
            *[new_m[n] for n in names], *[new_v[n] for n in names])
```

```python
import functools

import jax
import jax.numpy as jnp
from jax import lax
from jax.experimental import pallas as pl
from jax.experimental.pallas import tpu as pltpu

F32 = jnp.float32
BF16 = jnp.bfloat16
HI = lax.Precision.HIGHEST
MESH = pl.DeviceIdType.MESH
SDS = jax.ShapeDtypeStruct

D = 1024
D_XA = 512
D_CAT = 1536
N_MEM = 256
HD = 128
DEPTH = 4
EPS = 1e-6
TT = 512
DN_C = 64
DN_TB = 256
HALO = 8
AB_PAD = 128
VMEM_LIMIT = 56 * 1024 * 1024
GRAD_TILE = {0: 1024, 1: 256}

ADAM_LR, ADAM_B1, ADAM_B2, ADAM_EPS, ADAM_WD, ADAM_STEP = 0.001, 0.9, 0.999, 1e-08, 0.01, 10


def _params(n_grid, vmem=None):
    return pltpu.CompilerParams(dimension_semantics=("arbitrary",) * n_grid, vmem_limit_bytes=vmem)


def _rms(x, w):
    return x * lax.rsqrt(jnp.mean(x * x, axis=-1, keepdims=True) + EPS) * w


def _dot_nn(a, b):
    return jnp.dot(a.astype(BF16), b.astype(BF16), preferred_element_type=F32)


def _dot_nt(a, b):
    return lax.dot_general(a.astype(BF16), b.astype(BF16), (((1,), (1,)), ((), ())), preferred_element_type=F32)


def _dot_tn(a, b):
    return lax.dot_general(a.astype(BF16), b.astype(BF16), (((0,), (0,)), ((), ())), preferred_element_type=F32)


@jax.custom_vjp
def mm(a, b):
    return _dot_nn(a, b)


mm.defvjp(lambda a, b: (_dot_nn(a, b), (a, b)), lambda r, g: (_dot_nt(g, r[1]), _dot_tn(r[0], g)))


@jax.custom_vjp
def mm_nt(a, b):
    return _dot_nt(a, b)


mm_nt.defvjp(lambda a, b: (_dot_nt(a, b), (a, b)), lambda r, g: (_dot_nn(g, r[1]), _dot_tn(g, r[0])))


def _row_spec(width, tile=TT):
    return pl.BlockSpec((tile, width), lambda i: (i, 0))


def _full_spec(shape):
    return pl.BlockSpec(shape, lambda *_: (0,) * len(shape))


def _widths(blocks, transposed):
    return [b.shape[0 if transposed else 1] for b in blocks]


def _inproj_fwd(x, nw, mix, gate, transposed, name):
    T, nm = x.shape[0], len(mix)
    M, G = sum(_widths(mix, transposed)), sum(_widths(gate, transposed))

    def body(x_ref, nw_ref, *refs):
        blocks, (pm_ref, pg_ref, h_ref) = refs[:-3], refs[-3:]
        h = _rms(x_ref[...], nw_ref[...]).astype(BF16)
        h_ref[...] = h
        for p_ref, group in ((pm_ref, blocks[:nm]), (pg_ref, blocks[nm:])):
            off = 0
            for w_ref in group:
                w = w_ref.shape[0 if transposed else 1]
                p_ref[:, off:off + w] = _dot_nt(h, w_ref[...]) if transposed else _dot_nn(h, w_ref[...])
                off += w

    return pl.pallas_call(
        body, name=name, grid=(T // TT,),
        in_specs=[_row_spec(D), _full_spec((1, D))] + [_full_spec(b.shape) for b in mix + gate],
        out_specs=[_row_spec(M), _row_spec(G), _row_spec(D)],
        out_shape=[SDS((T, M), F32), SDS((T, G), F32), SDS((T, D), BF16)],
        compiler_params=_params(1, VMEM_LIMIT))(x, nw, *mix, *gate)


def _inproj_bwd(dpm, dpg, x, nw, mix, gate, transposed, dxc, name):
    T, nm = x.shape[0], len(mix)
    M, G = sum(_widths(mix, transposed)), sum(_widths(gate, transposed))

    def body(dpm_ref, dpg_ref, x_ref, nw_ref, *refs):
        blocks, (dxc_ref, dx_ref, dnw_ref) = refs[:-3], refs[-3:]
        dh = None
        for dp_ref, group in ((dpm_ref, blocks[:nm]), (dpg_ref, blocks[nm:])):
            off = 0
            for w_ref in group:
                w = w_ref.shape[0 if transposed else 1]
                dp = dp_ref[:, off:off + w]
                part = _dot_nn(dp, w_ref[...]) if transposed else _dot_nt(dp, w_ref[...])
                dh = part if dh is None else dh + part
                off += w
        _, vjp = jax.vjp(_rms, x_ref[...], nw_ref[...])
        dxr, dnw = vjp(dh)
        dx_ref[...] = dxc_ref[...] + dxr

        @pl.when(pl.program_id(0) == 0)
        def _():
            dnw_ref[...] = jnp.zeros_like(dnw_ref)
        dnw_ref[...] += dnw

    return pl.pallas_call(
        body, name=name, grid=(T // TT,),
        in_specs=[_row_spec(M), _row_spec(G), _row_spec(D), _full_spec((1, D))]
        + [_full_spec(b.shape) for b in mix + gate] + [_row_spec(D)],
        out_specs=[_row_spec(D), _full_spec((1, D))],
        out_shape=[SDS((T, D), F32), SDS((1, D), F32)],
        compiler_params=_params(1, VMEM_LIMIT))(dpm, dpg, x, nw, *mix, *gate, dxc)


def _matmul_tn(a, b, name, sub=None):
    T, K = a.shape
    N = b.shape[1]
    tn = 1024 if N % 1024 == 0 else (640 if N % 640 == 0 else N)
    tt = min(1024, T)
    n_sub = 1 if sub is None else tn // sub

    def body(a_ref, b_ref, o_ref):
        @pl.when(pl.program_id(1) == 0)
        def _():
            o_ref[...] = jnp.zeros_like(o_ref)
        res = _dot_tn(a_ref[...], b_ref[...])
        if sub is None:
            o_ref[...] += res
        else:
            for i in range(n_sub):
                o_ref[i] += res[:, i * sub:(i + 1) * sub]

    if sub is None:
        out_spec, out_shape = pl.BlockSpec((K, tn), lambda j, t: (0, j)), SDS((K, N), F32)
    else:
        out_spec, out_shape = pl.BlockSpec((n_sub, K, sub), lambda j, t: (j, 0, 0)), SDS((N // sub, K, sub), F32)
    return pl.pallas_call(
        body, name=name, grid=(N // tn, T // tt),
        in_specs=[pl.BlockSpec((tt, K), lambda j, t: (t, 0)), pl.BlockSpec((tt, tn), lambda j, t: (t, j))],
        out_specs=out_spec, out_shape=out_shape,
        compiler_params=_params(2, VMEM_LIMIT))(a, b)


def _memkv_fn(mem, w, wkv):
    return mm(_rms(mem, w), wkv)


def _memkv_fwd(mem, w, wkv):
    def body(mem_ref, w_ref, wkv_ref, kv_ref):
        kv_ref[...] = _memkv_fn(mem_ref[...], w_ref[...], wkv_ref[...])

    return pl.pallas_call(body, name="memkv_fwd", out_shape=SDS((N_MEM, 2 * D_XA), F32),
                          compiler_params=_params(0, VMEM_LIMIT))(mem, w, wkv)


def _memkv_bwd(mem, w, wkv, dkv):
    def body(mem_ref, w_ref, wkv_ref, dkv_ref, dw_ref, dwkv_ref):
        _, vjp = jax.vjp(functools.partial(_memkv_fn, mem_ref[...]), w_ref[...], wkv_ref[...].astype(F32))
        dw, dwkv = vjp(dkv_ref[...])
        dw_ref[...] = dw
        dwkv_ref[...] = dwkv

    return pl.pallas_call(body, name="memkv_bwd", out_shape=[SDS((1, D), F32), SDS((D, 2 * D_XA), F32)],
                          compiler_params=_params(0, VMEM_LIMIT))(mem, w, wkv, dkv)


def _attn_gate(ymix, qx, z, *kvs):
    outs = []
    for j in range(4):
        s = mm_nt(qx[:, j * HD:(j + 1) * HD], kvs[j]) * (HD ** -0.5)
        e = jnp.exp(s - lax.stop_gradient(jnp.max(s, axis=-1, keepdims=True)))
        outs.append(mm(e / jnp.sum(e, axis=-1, keepdims=True), kvs[4 + j]))
    return jnp.concatenate([ymix] + outs, axis=1) * jax.nn.silu(z)


def _kv_blocks(kv_ref):
    return [kv_ref[:, j * HD:(j + 1) * HD] for j in range(8)]


def _ag_fwd(ymix, pg, kv, name):
    T = ymix.shape[0]

    def body(ymix_ref, pg_ref, kv_ref, ycat_ref):
        ycat_ref[...] = _attn_gate(ymix_ref[...], pg_ref[:, :D_XA], pg_ref[:, D_XA:], *_kv_blocks(kv_ref)).astype(BF16)

    return pl.pallas_call(
        body, name=name, grid=(T // TT,),
        in_specs=[_row_spec(D), _row_spec(D_XA + D_CAT), _full_spec((N_MEM, 2 * D_XA))],
        out_specs=_row_spec(D_CAT), out_shape=SDS((T, D_CAT), BF16),
        compiler_params=_params(1, VMEM_LIMIT))(ymix, pg, kv)


def _ag_bwd(dycat, ymix, pg, kv, dkv_in, name):
    T = ymix.shape[0]

    def body(dycat_ref, ymix_ref, pg_ref, kv_ref, dkvin_ref, dymix_ref, dpg_ref, dkv_ref):
        _, vjp = jax.vjp(_attn_gate, ymix_ref[...], pg_ref[:, :D_XA], pg_ref[:, D_XA:], *_kv_blocks(kv_ref))
        g = vjp(dycat_ref[...])
        dymix_ref[...] = g[0]
        dpg_ref[:, :D_XA] = g[1].astype(BF16)
        dpg_ref[:, D_XA:] = g[2].astype(BF16)

        @pl.when(pl.program_id(0) == 0)
        def _():
            dkv_ref[...] = dkvin_ref[...]
        for j in range(8):
            dkv_ref[:, j * HD:(j + 1) * HD] += g[3 + j]

    return pl.pallas_call(
        body, name=name, grid=(T // TT,),
        in_specs=[_row_spec(D_CAT), _row_spec(D), _row_spec(D_XA + D_CAT), _full_spec((N_MEM, 2 * D_XA)),
                  _full_spec((N_MEM, 2 * D_XA))],
        out_specs=[_row_spec(D), _row_spec(D_XA + D_CAT), _full_spec((N_MEM, 2 * D_XA))],
        out_shape=[SDS((T, D), F32), SDS((T, D_XA + D_CAT), BF16), SDS((N_MEM, 2 * D_XA), F32)],
        compiler_params=_params(1, VMEM_LIMIT))(dycat, ymix, pg, kv, dkv_in)


def _outproj_fwd(ycat, wo, x, nw, name):
    T = x.shape[0]

    def body(ycat_ref, wo_ref, x_ref, nw_ref, o_ref, xn_ref):
        o = jnp.dot(ycat_ref[...], wo_ref[...], preferred_element_type=F32)
        o_ref[...] = o
        xn_ref[...] = x_ref[...] + _rms(o, nw_ref[...])

    return pl.pallas_call(
        body, name=name, grid=(T // TT,),
        in_specs=[_row_spec(D_CAT), _full_spec((D_CAT, D)), _row_spec(D), _full_spec((1, D))],
        out_specs=[_row_spec(D), _row_spec(D)], out_shape=[SDS((T, D), F32), SDS((T, D), F32)],
        compiler_params=_params(1, VMEM_LIMIT))(ycat, wo, x, nw)


def _outproj_bwd(dxo, o, nw, wo, name):
    T = dxo.shape[0]

    def body(dxo_ref, o_ref, nw_ref, wo_ref, dycat_ref, dobf_ref, dnw_ref):
        _, vjp = jax.vjp(_rms, o_ref[...], nw_ref[...])
        do, dnw = vjp(dxo_ref[...])
        dobf = do.astype(BF16)
        dobf_ref[...] = dobf
        dycat_ref[...] = _dot_nt(dobf, wo_ref[...])

        @pl.when(pl.program_id(0) == 0)
        def _():
            dnw_ref[...] = jnp.zeros_like(dnw_ref)
        dnw_ref[...] += dnw

    return pl.pallas_call(
        body, name=name, grid=(T // TT,),
        in_specs=[_row_spec(D), _row_spec(D), _full_spec((1, D)), _full_spec((D_CAT, D))],
        out_specs=[_row_spec(D_CAT), _row_spec(D), _full_spec((1, D))],
        out_shape=[SDS((T, D_CAT), F32), SDS((T, D), BF16), SDS((1, D), F32)],
        compiler_params=_params(1, VMEM_LIMIT))(dxo, o, nw, wo)


def _loss_head(xl, target):
    T = xl.shape[0]

    def body(x_ref, t_ref, loss_ref, dx_ref):
        err = x_ref[...] - t_ref[...]
        dx_ref[...] = err * (1.0 / D)

        @pl.when(pl.program_id(0) == 0)
        def _():
            loss_ref[...] = jnp.zeros_like(loss_ref)
        part = jnp.sum(jnp.sum(err * err, axis=1, keepdims=True), axis=0, keepdims=True) * (0.5 / D)
        loss_ref[...] += jnp.broadcast_to(part, loss_ref.shape)

    return pl.pallas_call(
        body, name="loss_head", grid=(T // TT,),
        in_specs=[_row_spec(D), _row_spec(D)],
        out_specs=[_full_spec((8, 128)), _row_spec(D)], out_shape=[SDS((8, 128), F32), SDS((T, D), F32)],
        compiler_params=_params(1))(xl, target)


def _gmlp_pre(u, v, lnw, lnb):
    vg = jax.nn.gelu(v)
    xc = vg - jnp.mean(vg, axis=-1, keepdims=True)
    vl = xc * lax.rsqrt(jnp.mean(xc * xc, axis=-1, keepdims=True) + EPS) * lnw + lnb
    return jax.nn.gelu(u), vl


def _tril(n, strict=False):
    r = lax.broadcasted_iota(jnp.int32, (n, n), 0)
    c = lax.broadcasted_iota(jnp.int32, (n, n), 1)
    return (r > c) if strict else (r >= c)


def _gmlp_fwd(pm, lnw, lnb, ws, bs3, name):
    T = pm.shape[0]

    def body(pm_ref, lnw_ref, lnb_ref, ws_ref, bs_ref, y_ref):
        ug, vl = _gmlp_pre(pm_ref[:, :D], pm_ref[:, D:], lnw_ref[...], lnb_ref[...])
        mask = _tril(HD)
        for g in range(8):
            w = jnp.where(mask, ws_ref[g], 0.0)
            for c in range(TT // HD):
                rows, cols = slice(c * HD, (c + 1) * HD), slice(g * HD, (g + 1) * HD)
                y_ref[rows, cols] = ug[rows, cols] * (_dot_nn(w, vl[rows, cols]) + bs_ref[g])

    return pl.pallas_call(
        body, name=name, grid=(T // TT,),
        in_specs=[_row_spec(2 * D), _full_spec((1, D)), _full_spec((1, D)), _full_spec((8, HD, HD)),
                  _full_spec((8, HD, HD))],
        out_specs=_row_spec(D), out_shape=SDS((T, D), F32),
        compiler_params=_params(1, VMEM_LIMIT))(pm, lnw, lnb, ws, bs3)


def _gmlp_bwd(dy, pm, lnw, lnb, ws, bs3, name):
    T = pm.shape[0]
    n_t = T // TT

    def body(dy_ref, pm_ref, lnw_ref, lnb_ref, ws_ref, bs_ref, dpm_ref, dlnw_ref, dlnb_ref, dws_ref, dbs_ref,
             dug_scr, dvl_scr, dbs_scr):
        i = pl.program_id(0)

        @pl.when(i == 0)
        def _():
            dlnw_ref[...] = jnp.zeros_like(dlnw_ref)
            dlnb_ref[...] = jnp.zeros_like(dlnb_ref)
            dws_ref[...] = jnp.zeros_like(dws_ref)
            dbs_scr[...] = jnp.zeros_like(dbs_scr)

        (ug, vl), vjp = jax.vjp(_gmlp_pre, pm_ref[:, :D], pm_ref[:, D:], lnw_ref[...], lnb_ref[...])
        mask = _tril(HD)
        for g in range(8):
            w = jnp.where(mask, ws_ref[g], 0.0)
            dw = jnp.zeros((HD, HD), F32)
            db = jnp.zeros((HD, HD), F32)
            for c in range(TT // HD):
                rows, cols = slice(c * HD, (c + 1) * HD), slice(g * HD, (g + 1) * HD)
                dyb, vlb = dy_ref[rows, cols], vl[rows, cols]
                sp = _dot_nn(w, vlb) + bs_ref[g]
                dsp = dyb * ug[rows, cols]
                dug_scr[rows, cols] = dyb * sp
                dvl_scr[rows, cols] = _dot_tn(w, dsp)
                dw += _dot_nt(dsp, vlb)
                db += dsp
            dws_ref[g] += jnp.where(mask, dw, 0.0)
            dbs_scr[g] += db
        du, dv, dlnw, dlnb = vjp((dug_scr[...], dvl_scr[...]))
        dpm_ref[:, :D] = du.astype(BF16)
        dpm_ref[:, D:] = dv.astype(BF16)
        dlnw_ref[...] += dlnw
        dlnb_ref[...] += dlnb

        @pl.when(i == n_t - 1)
        def _():
            for g in range(8):
                dbs_ref[g] = jnp.broadcast_to(jnp.sum(dbs_scr[g], axis=1, keepdims=True), (HD, HD))

    return pl.pallas_call(
        body, name=name, grid=(n_t,),
        in_specs=[_row_spec(D), _row_spec(2 * D), _full_spec((1, D)), _full_spec((1, D)), _full_spec((8, HD, HD)),
                  _full_spec((8, HD, HD))],
        out_specs=[_row_spec(2 * D), _full_spec((1, D)), _full_spec((1, D)), _full_spec((8, HD, HD)),
                   _full_spec((8, HD, HD))],
        out_shape=[SDS((T, 2 * D), BF16), SDS((1, D), F32), SDS((1, D), F32), SDS((8, HD, HD), F32),
                   SDS((8, HD, HD), F32)],
        scratch_shapes=[pltpu.VMEM((TT, D), F32), pltpu.VMEM((TT, D), F32), pltpu.VMEM((8, HD, HD), F32)],
        compiler_params=_params(1, VMEM_LIMIT))(dy, pm, lnw, lnb, ws, bs3)


def _prev_spec(width, T):
    return pl.BlockSpec((HALO, width), lambda i: (jnp.maximum(i * (TT // HALO) - 1, 0), 0))


def _next_spec(width, T):
    return pl.BlockSpec((HALO, width), lambda i: (jnp.minimum((i + 1) * (TT // HALO), T // HALO - 1), 0))


def _rows_before(ext, j):
    return ext[HALO:] if j == 0 else pltpu.roll(ext, j, 0)[HALO:]


def _rows_after(ext, j):
    n = ext.shape[0]
    return ext[:n - HALO] if j == 0 else pltpu.roll(ext, n - j, 0)[:n - HALO]


def _conv_apply(ext_s, w):
    K = w.shape[0]
    y = _rows_before(ext_s, K - 1) * w[0:1]
    for k in range(1, K):
        y = y + _rows_before(ext_s, K - 1 - k) * w[k:k + 1]
    return y


def _conv_grads(ext_s, ext_dy, w):
    K = w.shape[0]
    dy = ext_dy[:ext_dy.shape[0] - HALO]
    ds = _rows_after(ext_dy, K - 1) * w[0:1]
    dws = [jnp.sum(dy * _rows_before(ext_s, K - 1), axis=0, keepdims=True)]
    for k in range(1, K):
        ds = ds + _rows_after(ext_dy, K - 1 - k) * w[k:k + 1]
        dws.append(jnp.sum(dy * _rows_before(ext_s, K - 1 - k), axis=0, keepdims=True))
    return ds, jnp.concatenate(dws, axis=0)


def _sconv_fwd(pm, w, name):
    T = pm.shape[0]

    def body(pm_ref, prev_ref, w_ref, y_ref):
        s = pm_ref[:, D:2 * D] * pm_ref[:, 2 * D:]
        sp = jnp.where(pl.program_id(0) > 0, prev_ref[:, D:2 * D] * prev_ref[:, 2 * D:], 0.0)
        y_ref[...] = pm_ref[:, :D] * _conv_apply(jnp.concatenate([sp, s], axis=0), w_ref[...])

    return pl.pallas_call(
        body, name=name, grid=(T // TT,),
        in_specs=[_row_spec(3 * D), _prev_spec(3 * D, T), _full_spec((3, D))],
        out_specs=_row_spec(D), out_shape=SDS((T, D), F32),
        compiler_params=_params(1, VMEM_LIMIT))(pm, pm, w)


def _sconv_bwd(dy, pm, w, name):
    T = pm.shape[0]
    n_t = T // TT

    def body(dy_ref, dyn_ref, pm_ref, prev_ref, next_ref, w_ref, dpm_ref, dw_ref):
        i = pl.program_id(0)
        bg, cg, hv = pm_ref[:, :D], pm_ref[:, D:2 * D], pm_ref[:, 2 * D:]
        sp = jnp.where(i > 0, prev_ref[:, D:2 * D] * prev_ref[:, 2 * D:], 0.0)
        ext_s = jnp.concatenate([sp, cg * hv], axis=0)
        dyv = dy_ref[...]
        dcn = jnp.where(i < n_t - 1, dyn_ref[...] * next_ref[:, :D], 0.0)
        ds, dw = _conv_grads(ext_s, jnp.concatenate([dyv * bg, dcn], axis=0), w_ref[...])
        dpm_ref[:, :D] = (dyv * _conv_apply(ext_s, w_ref[...])).astype(BF16)
        dpm_ref[:, D:2 * D] = (ds * hv).astype(BF16)
        dpm_ref[:, 2 * D:] = (ds * cg).astype(BF16)

        @pl.when(i == 0)
        def _():
            dw_ref[...] = jnp.zeros_like(dw_ref)
        dw_ref[...] += dw

    return pl.pallas_call(
        body, name=name, grid=(n_t,),
        in_specs=[_row_spec(D), _next_spec(D, T), _row_spec(3 * D), _prev_spec(3 * D, T), _next_spec(3 * D, T),
                  _full_spec((3, D))],
        out_specs=[_row_spec(3 * D), _full_spec((3, D))],
        out_shape=[SDS((T, 3 * D), BF16), SDS((3, D), F32)],
        compiler_params=_params(1, VMEM_LIMIT))(dy, dy, pm, pm, pm, w)


def _dnconv_fwd(pm, w, name):
    T = pm.shape[0]

    def body(pm_ref, prev_ref, w_ref, c_ref):
        sp = jnp.where(pl.program_id(0) > 0, prev_ref[...], 0.0)
        c_ref[...] = _conv_apply(jnp.concatenate([sp, pm_ref[...]], axis=0), w_ref[...])

    return pl.pallas_call(
        body, name=name, grid=(T // TT,),
        in_specs=[_row_spec(3 * D), _prev_spec(3 * D, T), _full_spec((4, 3 * D))],
        out_specs=_row_spec(3 * D), out_shape=SDS((T, 3 * D), F32),
        compiler_params=_params(1, VMEM_LIMIT))(pm, pm, w)


def _dnconv_bwd(dcq, dck, dcv, dab, pm, w, name):
    T = pm.shape[0]
    n_t = T // TT

    def body(dq_ref, dk_ref, dv_ref, dqn_ref, dkn_ref, dvn_ref, dab_ref, pm_ref, prev_ref, w_ref, dpm_ref, dw_ref):
        i = pl.program_id(0)
        sp = jnp.where(i > 0, prev_ref[...], 0.0)
        ext_s = jnp.concatenate([sp, pm_ref[...]], axis=0)
        own = jnp.concatenate([dq_ref[...], dk_ref[...], dv_ref[...]], axis=1)
        nxt = jnp.where(i < n_t - 1, jnp.concatenate([dqn_ref[...], dkn_ref[...], dvn_ref[...]], axis=1), 0.0)
        ds, dw = _conv_grads(ext_s, jnp.concatenate([own, nxt], axis=0), w_ref[...])
        dpm_ref[:, :3 * D] = ds.astype(BF16)
        dpm_ref[:, 3 * D:] = dab_ref[...].astype(BF16)

        @pl.when(i == 0)
        def _():
            dw_ref[...] = jnp.zeros_like(dw_ref)
        dw_ref[...] += dw

    return pl.pallas_call(
        body, name=name, grid=(n_t,),
        in_specs=[_row_spec(D), _row_spec(D), _row_spec(D), _next_spec(D, T), _next_spec(D, T), _next_spec(D, T),
                  _row_spec(AB_PAD), _row_spec(3 * D), _prev_spec(3 * D, T), _full_spec((4, 3 * D))],
        out_specs=[_row_spec(3 * D + AB_PAD), _full_spec((4, 3 * D))],
        out_shape=[SDS((T, 3 * D + AB_PAD), BF16), SDS((4, 3 * D), F32)],
        compiler_params=_params(1, VMEM_LIMIT))(dcq, dck, dcv, dcq, dck, dcv, dab, pm, pm, w)


def _l2n(x):
    return x * lax.rsqrt(jnp.sum(x * x, axis=-1, keepdims=True) + EPS)


def _softplus(x):
    return jnp.maximum(x, 0.0) + jnp.log1p(jnp.exp(-jnp.abs(x)))


_BNN = (((2,), (1,)), ((0,), (0,)))
_BNT = (((2,), (2,)), ((0,), (0,)))
_BTN = (((1,), (1,)), ((0,), (0,)))


def _bdot(a, b, dims):
    return lax.dot_general(a.astype(BF16), b.astype(BF16), dims, preferred_element_type=F32)


def _bdot3(a, b, dims):
    ah, bh = a.astype(BF16), b.astype(BF16)
    al, bl = (a - ah.astype(F32)).astype(BF16), (b - bh.astype(F32)).astype(BF16)
    d = functools.partial(lax.dot_general, dimension_numbers=dims, preferred_element_type=F32)
    return d(ah, bh) + (d(ah, bl) + d(al, bh))


def _bdot_hi(a, b, dims):
    return lax.dot_general(a, b, dims, precision=HI, preferred_element_type=F32)


def _batched_matmuls(dot):
    @jax.custom_vjp
    def nn(a, b):
        return dot(a, b, _BNN)

    @jax.custom_vjp
    def nt(a, b):
        return dot(a, b, _BNT)

    @jax.custom_vjp
    def tn(a, b):
        return dot(a, b, _BTN)

    nn.defvjp(lambda a, b: (dot(a, b, _BNN), (a, b)), lambda r, g: (dot(g, r[1], _BNT), dot(r[0], g, _BTN)))
    nt.defvjp(lambda a, b: (dot(a, b, _BNT), (a, b)), lambda r, g: (dot(g, r[1], _BNN), dot(g, r[0], _BTN)))
    tn.defvjp(lambda a, b: (dot(a, b, _BTN), (a, b)), lambda r, g: (dot(r[1], g, _BNT), dot(r[0], g, _BNN)))
    return nn, nt, tn


bmm, bmm_nt, bmm_tn = _batched_matmuls(_bdot)
bmm_hi, _, _ = _batched_matmuls(_bdot_hi)

@jax.custom_vjp
def _neumann_inverse(n):
    C = n.shape[1]
    eye = lax.broadcasted_iota(jnp.int32, n.shape, 1) == lax.broadcasted_iota(jnp.int32, n.shape, 2)
    t = eye.astype(F32) + n
    for _ in range(5):
        n = _bdot3(n, n, _BNN)
        t = t + _bdot3(t, n, _BNN)
    return t


def _neumann_inverse_fwd(n):
    t = _neumann_inverse(n)
    return t, t


def _neumann_inverse_bwd(t, g):
    return (_bdot3(_bdot3(t, g, _BTN), t, _BNT),)


_neumann_inverse.defvjp(_neumann_inverse_fwd, _neumann_inverse_bwd)


@jax.custom_vjp
def _saved_inverse(n, t):
    return t


_saved_inverse.defvjp(lambda n, t: (t, t), lambda t, g: (_bdot3(_bdot3(t, g, _BTN), t, _BNT), jnp.zeros_like(t)))

DN_NCH = DN_TB // DN_C
DN_NH = 4


def _decay_terms(ab, alog, dtb, first_head, n_heads):
    C = DN_C
    lane = lax.broadcasted_iota(jnp.int32, ab.shape, 1)
    g_all = (-jnp.exp(alog) * _softplus(ab + dtb)).reshape(DN_NCH, C, HD)
    beta_all = jax.nn.sigmoid(ab)
    r = lax.broadcasted_iota(jnp.int32, (DN_NCH, C, C), 1)
    c = lax.broadcasted_iota(jnp.int32, (DN_NCH, C, C), 2)
    gc_all = bmm_hi((r >= c).astype(F32), g_all)
    gc_rows = [gc_all[i].T for i in range(DN_NCH)]
    lane3 = lax.broadcasted_iota(jnp.int32, (DN_NCH, C, HD), 2)
    row = lax.broadcasted_iota(jnp.int32, (HD, C), 0)
    ones = jnp.ones((1, HD), F32)
    gcs, gjs, betas = [], [], []
    for i in range(n_heads):
        h = first_head + i
        gcs.append(jnp.sum(jnp.where(lane3 == h, gc_all, 0.0), axis=2, keepdims=True) * ones)
        gjs.append(jnp.concatenate(
            [jnp.broadcast_to(jnp.sum(jnp.where(row == h, t, 0.0), axis=0, keepdims=True), (C, C))[None] for t in gc_rows],
            axis=0))
        beta = jnp.sum(jnp.where(lane == 8 + h, beta_all, 0.0), axis=1, keepdims=True) * ones
        betas.append(beta.reshape(DN_NCH, C, HD))
    return jnp.concatenate(gcs, axis=0), jnp.concatenate(gjs, axis=0), jnp.concatenate(betas, axis=0)


def _dn_prep(cq, ck, cv, gcum, gj, bb, t_saved=None):
    B, C = cq.shape[0], DN_C
    q = _l2n(jax.nn.silu(cq)) * (HD ** -0.5)
    k = _l2n(jax.nn.silu(ck))
    v = jax.nn.silu(cv)
    r = lax.broadcasted_iota(jnp.int32, (B, C, C), 1)
    c = lax.broadcasted_iota(jnp.int32, (B, C, C), 2)
    incl, strict = r >= c, r > c
    decay = jnp.where(incl, jnp.exp(jnp.where(incl, gcum[:, :, :C] - gj, 0.0)), 0.0)
    kb = k * bb
    n_mat = -jnp.where(strict, bmm_nt(kb, k) * decay, 0.0)
    t_mat = _neumann_inverse(n_mat) if t_saved is None else _saved_inverse(n_mat, t_saved)
    eg = jnp.exp(gcum)
    glast = gcum[:, C - 1:C, :]
    return (bmm(t_mat, v * bb), bmm(t_mat, kb * eg), bmm_nt(q, k) * decay, q * eg, k * jnp.exp(glast - gcum),
            jnp.exp(glast), t_mat)


def _dn_scan_step(u, w, qk, qd, kd, egl, S, onw):
    v_new = u - bmm(w, S)
    o = bmm(qd, S) + bmm(qk, v_new)
    return _rms(o, onw), S * egl + bmm_tn(kd, v_new)


def _to_batch(ref, n_heads):
    return jnp.concatenate([ref[:, i * HD:(i + 1) * HD].astype(F32).reshape(DN_NCH, DN_C, HD) for i in range(n_heads)],
                           axis=0)


def _from_batch(ref, val, n_heads):
    for i in range(n_heads):
        ref[:, i * HD:(i + 1) * HD] = val[i * DN_NCH:(i + 1) * DN_NCH].reshape(DN_TB, HD).astype(ref.dtype)


def _prep_specs(T, rev):
    nb = T // DN_TB
    blk = (lambda n: nb - 1 - n) if rev else (lambda n: n)
    ng = 8 // DN_NH
    head = [pl.BlockSpec((DN_TB, DN_NH * HD), functools.partial(lambda n, h, off: (blk(n), off + h), off=ng * s))
            for s in range(3)]
    ab = pl.BlockSpec((DN_TB, AB_PAD), lambda n, h: (blk(n), 3 * D // AB_PAD))
    row = pl.BlockSpec((1, HD), lambda n, h: (0, 0))
    wide = pl.BlockSpec((DN_TB, DN_NH * HD), lambda n, h: (blk(n), h))
    qk = pl.BlockSpec((DN_NCH, DN_NH, DN_C, DN_C), lambda n, h: (blk(n), h, 0, 0))
    eg = pl.BlockSpec((DN_NCH, DN_NH, 1, HD), lambda n, h: (blk(n), h, 0, 0))
    return nb, ng, head, ab, row, wide, qk, eg


def _dn_prep_fwd(cpre, pm, alog, dtb, name):
    T = cpre.shape[0]
    nb, ng, head, ab, row, wide, qks, egs = _prep_specs(T, False)

    def body(cq_ref, ck_ref, cv_ref, ab_ref, alog_ref, dtb_ref, u_ref, w_ref, qk_ref, qd_ref, kd_ref, e_ref, t_ref):
        gcum, gj, bb = _decay_terms(ab_ref[...], alog_ref[...], dtb_ref[...], pl.program_id(1) * DN_NH, DN_NH)
        u, w, qk, qd, kd, egl, t_mat = _dn_prep(_to_batch(cq_ref, DN_NH), _to_batch(ck_ref, DN_NH),
                                                _to_batch(cv_ref, DN_NH), gcum, gj, bb)
        _from_batch(u_ref, u, DN_NH)
        _from_batch(w_ref, w, DN_NH)
        _from_batch(qd_ref, qd, DN_NH)
        _from_batch(kd_ref, kd, DN_NH)
        for i in range(DN_NH):
            qk_ref[:, i] = qk[i * DN_NCH:(i + 1) * DN_NCH].astype(BF16)
            e_ref[:, i] = egl[i * DN_NCH:(i + 1) * DN_NCH]
            t_ref[:, i] = t_mat[i * DN_NCH:(i + 1) * DN_NCH]

    return pl.pallas_call(
        body, name=name, grid=(nb, ng), in_specs=head + [ab, row, row],
        out_specs=[wide, wide, qks, wide, wide, egs, qks],
        out_shape=[SDS((T, D), F32), SDS((T, D), BF16), SDS((T // DN_C, 8, DN_C, DN_C), BF16), SDS((T, D), BF16),
                   SDS((T, D), BF16), SDS((T // DN_C, 8, 1, HD), F32), SDS((T // DN_C, 8, DN_C, DN_C), F32)],
        compiler_params=_params(2, VMEM_LIMIT))(cpre, cpre, cpre, pm, alog, dtb)


def _dn_prep_bwd(du, dw, dqk, dqd, dkd, degl, t_mat, cpre, pm, alog, dtb, name):
    T = cpre.shape[0]
    nb, ng, head, ab, row, wide, qks, egs = _prep_specs(T, True)

    def body(du_ref, dw_ref, dqk_ref, dqd_ref, dkd_ref, de_ref, t_ref, cq_ref, ck_ref, cv_ref, ab_ref, alog_ref,
             dtb_ref, dcq_ref, dck_ref, dcv_ref, dab_ref, dalog_ref, ddtb_ref):
        n, h = pl.program_id(0), pl.program_id(1)

        @pl.when((n == 0) & (h == 0))
        def _():
            dalog_ref[...] = jnp.zeros_like(dalog_ref)
            ddtb_ref[...] = jnp.zeros_like(ddtb_ref)

        @pl.when(h == 0)
        def _():
            dab_ref[...] = jnp.zeros_like(dab_ref)

        t_saved = jnp.concatenate([t_ref[:, i] for i in range(DN_NH)], axis=0)

        def fwd(cq, ck, cv, ab_v, alog_v, dtb_v):
            gcum, gj, bb = _decay_terms(ab_v, alog_v, dtb_v, h * DN_NH, DN_NH)
            return _dn_prep(cq, ck, cv, gcum, gj, bb, t_saved)[:6]

        _, vjp = jax.vjp(fwd, _to_batch(cq_ref, DN_NH), _to_batch(ck_ref, DN_NH), _to_batch(cv_ref, DN_NH), ab_ref[...],
                         alog_ref[...], dtb_ref[...])
        cot = (_to_batch(du_ref, DN_NH), _to_batch(dw_ref, DN_NH),
               jnp.concatenate([dqk_ref[:, i] for i in range(DN_NH)], axis=0), _to_batch(dqd_ref, DN_NH),
               _to_batch(dkd_ref, DN_NH), jnp.concatenate([de_ref[:, i] for i in range(DN_NH)], axis=0))
        dcq, dck, dcv, dab, dalog, ddtb = vjp(cot)
        _from_batch(dcq_ref, dcq, DN_NH)
        _from_batch(dck_ref, dck, DN_NH)
        _from_batch(dcv_ref, dcv, DN_NH)
        dab_ref[...] += dab
        dalog_ref[...] += dalog
        ddtb_ref[...] += ddtb

    dabspec = pl.BlockSpec((DN_TB, AB_PAD), lambda n, h: (nb - 1 - n, 0))
    return pl.pallas_call(
        body, name=name, grid=(nb, ng),
        in_specs=[wide, wide, qks, wide, wide, egs, qks] + head + [ab, row, row],
        out_specs=[wide, wide, wide, dabspec, row, row],
        out_shape=[SDS((T, D), F32)] * 3 + [SDS((T, AB_PAD), F32)] + [SDS((1, HD), F32)] * 2,
        compiler_params=_params(2, VMEM_LIMIT))(du, dw, dqk, dqd, dkd, degl, t_mat, cpre, cpre, cpre, pm, alog, dtb)


def _scan_specs(T, rev):
    nb = T // DN_TB
    blk = (lambda n: nb - 1 - n) if rev else (lambda n: n)
    wide = pl.BlockSpec((DN_TB, D), lambda n: (blk(n), 0))
    qk = pl.BlockSpec((DN_NCH, 8, DN_C, DN_C), lambda n: (blk(n), 0, 0, 0))
    eg = pl.BlockSpec((DN_NCH, 8, 1, HD), lambda n: (blk(n), 0, 0, 0))
    st = pl.BlockSpec((DN_NCH, 8, HD, HD), lambda n: (blk(n), 0, 0, 0))
    row = pl.BlockSpec((1, HD), lambda n: (0, 0))
    return nb, wide, qk, eg, st, row


def _heads_of(ref, rows):
    return jnp.concatenate([ref[rows, h * HD:(h + 1) * HD].astype(F32)[None] for h in range(8)], axis=0)


def _dn_scan_fwd(u, w, qk, qd, kd, egl, onw, name):
    T = u.shape[0]
    nb, wide, qks, egs, sts, row = _scan_specs(T, False)

    def body(u_ref, w_ref, qk_ref, qd_ref, kd_ref, e_ref, onw_ref, o_ref, st_ref, s_scr):
        @pl.when(pl.program_id(0) == 0)
        def _():
            s_scr[...] = jnp.zeros_like(s_scr)
        S = s_scr[...]
        for c in range(DN_NCH):
            rows = slice(c * DN_C, (c + 1) * DN_C)
            st_ref[c] = S
            o, S = _dn_scan_step(_heads_of(u_ref, rows), _heads_of(w_ref, rows), qk_ref[c].astype(F32),
                                 _heads_of(qd_ref, rows), _heads_of(kd_ref, rows), e_ref[c], S, onw_ref[...])
            for h in range(8):
                o_ref[rows, h * HD:(h + 1) * HD] = o[h]
        s_scr[...] = S

    return pl.pallas_call(
        body, name=name, grid=(nb,), in_specs=[wide, wide, qks, wide, wide, egs, row], out_specs=[wide, sts],
        out_shape=[SDS((T, D), F32), SDS((T // DN_C, 8, HD, HD), F32)],
        scratch_shapes=[pltpu.VMEM((8, HD, HD), F32)],
        compiler_params=_params(1, VMEM_LIMIT))(u, w, qk, qd, kd, egl, onw)


def _dn_scan_bwd(do, u, w, qk, qd, kd, egl, st, onw, name):
    T = u.shape[0]
    nb, wide, qks, egs, sts, row = _scan_specs(T, True)

    def body(do_ref, u_ref, w_ref, qk_ref, qd_ref, kd_ref, e_ref, st_ref, onw_ref,
             du_ref, dw_ref, dqk_ref, dqd_ref, dkd_ref, de_ref, donw_ref, ds_scr):
        @pl.when(pl.program_id(0) == 0)
        def _():
            ds_scr[...] = jnp.zeros_like(ds_scr)
            donw_ref[...] = jnp.zeros_like(donw_ref)
        dS = ds_scr[...]
        donw = jnp.zeros((1, HD), F32)
        for c in reversed(range(DN_NCH)):
            rows = slice(c * DN_C, (c + 1) * DN_C)
            _, vjp = jax.vjp(_dn_scan_step, _heads_of(u_ref, rows), _heads_of(w_ref, rows), qk_ref[c].astype(F32),
                             _heads_of(qd_ref, rows), _heads_of(kd_ref, rows), e_ref[c], st_ref[c], onw_ref[...])
            du, dw, dqk, dqd, dkd, de, dS, dn = vjp((_heads_of(do_ref, rows), dS))
            for h in range(8):
                cols = slice(h * HD, (h + 1) * HD)
                du_ref[rows, cols] = du[h]
                dw_ref[rows, cols] = dw[h]
                dqd_ref[rows, cols] = dqd[h]
                dkd_ref[rows, cols] = dkd[h]
            dqk_ref[c] = dqk
            de_ref[c] = de
            donw += dn
        ds_scr[...] = dS
        donw_ref[...] += donw

    return pl.pallas_call(
        body, name=name, grid=(nb,), in_specs=[wide, wide, wide, qks, wide, wide, egs, sts, row],
        out_specs=[wide, wide, qks, wide, wide, egs, row],
        out_shape=[SDS((T, D), F32), SDS((T, D), F32), SDS((T // DN_C, 8, DN_C, DN_C), F32), SDS((T, D), F32),
                   SDS((T, D), F32), SDS((T // DN_C, 8, 1, HD), F32), SDS((1, HD), F32)],
        scratch_shapes=[pltpu.VMEM((8, HD, HD), F32)],
        compiler_params=_params(1, VMEM_LIMIT))(do, u, w, qk, qd, kd, egl, st, onw)


def _adamw(w, g, m, v, name):
    R, C = w.shape
    tr = 256 if R % 256 == 0 and R > 256 else R
    tc = 256 if tr == R and R > 256 and C % 256 == 0 else C
    c1 = 1.0 - ADAM_B1 ** ADAM_STEP
    c2 = 1.0 - ADAM_B2 ** ADAM_STEP

    def body(w_ref, g_ref, m_ref, v_ref, d_ref, nm_ref, nv_ref):
        gv = g_ref[...]
        nm = ADAM_B1 * m_ref[...] + (1.0 - ADAM_B1) * gv
        nv = ADAM_B2 * v_ref[...] + (1.0 - ADAM_B2) * (gv * gv)
        nm_ref[...] = nm
        nv_ref[...] = nv
        d_ref[...] = -ADAM_LR * ((nm / c1) / (jnp.sqrt(nv / c2) + ADAM_EPS) + ADAM_WD * w_ref[...])

    spec = pl.BlockSpec((tr, tc), lambda i, j: (i, j))
    return pl.pallas_call(
        body, name=name, grid=(R // tr, C // tc), in_specs=[spec] * 4, out_specs=[spec] * 3,
        out_shape=[SDS((R, C), F32)] * 3, compiler_params=_params(2, VMEM_LIMIT))(w, g, m, v)


def _local_step(x, mem, target, wts, sm):
    kinds = [i % 3 for i in range(DEPTH)]
    mnw = sm["mem_norm_w"].reshape(1, D)
    kv, wkv = None, None
    saved, blocks = [], []
    for i, kind in enumerate(kinds):
        j = i // 3
        npre = sm["norm_pre"][i].reshape(1, D)
        npost = sm["norm_post"][i].reshape(1, D)
        mix, gate, wo_after = wts["blocks"](i, x)
        pm, pg, h = _inproj_fwd(x, npre, mix, gate, kind == 2, f"inproj_fwd_{i}")
        if kv is None:
            wkv = wts["wkv_after"](h)
            kv = _memkv_fwd(mem, mnw, wkv)
        extra = None
        if kind == 0:
            bs3 = jnp.broadcast_to(sm["a_b_s"][j][:, :, None], (8, HD, HD))
            ymix = _gmlp_fwd(pm, sm["a_ln_w"][j].reshape(1, D), sm["a_ln_b"][j].reshape(1, D), sm["a_w_s"][j], bs3,
                             f"gmlp_fwd_{i}")
            extra = bs3
        elif kind == 1:
            ymix = _sconv_fwd(pm, sm["b_conv_w"][j], f"sconv_fwd_{i}")
        else:
            cpre = _dnconv_fwd(pm, sm["c_conv_w"][j], f"dnconv_fwd_{i}")
            alog = jnp.pad(sm["c_a_log"][j], (0, HD - 8)).reshape(1, HD)
            dtb = jnp.pad(sm["c_dt_bias"][j], (0, HD - 8)).reshape(1, HD)
            onw = sm["c_o_norm_w"][j].reshape(1, HD)
            *prep, t_mat = _dn_prep_fwd(cpre, pm, alog, dtb, f"dn_prep_fwd_{i}")
            ymix, st = _dn_scan_fwd(*prep, onw, f"dn_scan_fwd_{i}")
            extra = (cpre, prep, t_mat, st, alog, dtb, onw)
        ycat = _ag_fwd(ymix, pg, kv, f"ag_fwd_{i}")
        wo = wo_after(ycat)
        blocks.append((mix, gate, wo))
        o, xn = _outproj_fwd(ycat, wo, x, npost, f"outproj_fwd_{i}")
        saved.append((x, h, pm, pg, ymix, ycat, o, extra))
        x = xn

    loss, dx = _loss_head(x, target)

    g = {"wm": [None] * DEPTH, "wg": [None] * DEPTH, "wo": [None] * DEPTH, "norm_pre": [None] * DEPTH,
         "norm_post": [None] * DEPTH}
    dkv = jnp.zeros((N_MEM, 2 * D_XA), F32)
    sent = 0.0
    for i in reversed(range(DEPTH)):
        kind, j = kinds[i], i // 3
        xi, h, pm, pg, ymix, ycat, o, extra = saved[i]
        npre = sm["norm_pre"][i].reshape(1, D)
        npost = sm["norm_post"][i].reshape(1, D) + sent
        dycat, dobf, g["norm_post"][i] = _outproj_bwd(dx, o, npost, blocks[i][2], f"outproj_bwd_{i}")
        g["wo"][i] = _matmul_tn(ycat, dobf, f"dwo_{i}")
        dymix, dpg, dkv = _ag_bwd(dycat, ymix, pg, kv, dkv, f"ag_bwd_{i}")
        if kind == 0:
            dpm, dlnw, dlnb, dws, dbs3 = _gmlp_bwd(dymix, pm, sm["a_ln_w"][j].reshape(1, D),
                                                   sm["a_ln_b"][j].reshape(1, D), sm["a_w_s"][j], extra,
                                                   f"gmlp_bwd_{i}")
            g.setdefault("a_ln_w", {})[j] = dlnw.reshape(D)
            g.setdefault("a_ln_b", {})[j] = dlnb.reshape(D)
            g.setdefault("a_w_s", {})[j] = dws
            g.setdefault("a_b_s", {})[j] = dbs3[:, :, 0]
        elif kind == 1:
            dpm, dcw = _sconv_bwd(dymix, pm, sm["b_conv_w"][j], f"sconv_bwd_{i}")
            g.setdefault("b_conv_w", {})[j] = dcw
        else:
            cpre, prep, t_mat, st, alog, dtb, onw = extra
            *dprep, donw = _dn_scan_bwd(dymix, *prep, st, onw, f"dn_scan_bwd_{i}")
            dcq, dck, dcv, dab, dalog, ddtb = _dn_prep_bwd(*dprep, t_mat, cpre, pm, alog, dtb, f"dn_prep_bwd_{i}")
            dpm, dcw = _dnconv_bwd(dcq, dck, dcv, dab, pm, sm["c_conv_w"][j], f"dnconv_bwd_{i}")
            g.setdefault("c_conv_w", {})[j] = dcw
            g.setdefault("c_a_log", {})[j] = dalog[0, :8]
            g.setdefault("c_dt_bias", {})[j] = ddtb[0, :8]
            g.setdefault("c_o_norm_w", {})[j] = donw[0]
        if kind == 2:
            g["wm"][i] = _matmul_tn(dpm, h, f"dwm_{i}")
            g["wg"][i] = _matmul_tn(dpg, h, f"dwg_{i}")
        else:
            g["wm"][i] = _matmul_tn(h, dpm, f"dwm_{i}", GRAD_TILE[kind])
            g["wg"][i] = _matmul_tn(h, dpg, f"dwg_{i}", GRAD_TILE[kind])
        sent = wts["layer_done"](i, g)
        dx, g["norm_pre"][i] = _inproj_bwd(dpm, dpg, xi, npre + sent, blocks[i][0], blocks[i][1], kind == 2, dx,
                                           f"inproj_bwd_{i}")
    g["mem_norm_w"], g["wkv"] = _memkv_bwd(mem, mnw, wkv, dkv)
    return loss[0, 0], dx, g


ANY = pl.BlockSpec(memory_space=pl.ANY)


def _place():
    return lax.axis_index("x"), lax.axis_index("y"), lax.axis_index("c")


def _add(a, b, name):
    def body(a_ref, b_ref, o_ref):
        o_ref[...] = a_ref[...] + b_ref[...]

    return pl.pallas_call(body, name=name, out_shape=SDS(a.shape, a.dtype), compiler_params=_params(0, VMEM_LIMIT))(a, b)


def _sum4(own, land):
    def body(own_ref, l_ref, o_ref):
        chip = 2 * lax.axis_index("x") + lax.axis_index("y")
        acc = jnp.where(chip == 0, own_ref[...], l_ref[0])
        for s in range(1, 4):
            acc = acc + jnp.where(chip == s, own_ref[...], l_ref[s])
        o_ref[...] = acc

    return pl.pallas_call(body, name="sum_small", out_shape=SDS(own.shape, own.dtype),
                          compiler_params=_params(0, VMEM_LIMIT))(own, land)


C_ROWS = 1312
_REDUCE_CHUNK = {512: 256, 2560: 640}
W_CLASSES = {"in0": (512, 256)}


def _chunk_list(specs):
    return [(k, r, chunk) for k, (half, chunk) in enumerate(specs) for r in range(0, half, chunk)]


def _gather_classes(arrs, specs, vec):
    n = len(arrs)
    chunks = _chunk_list(specs)
    nc = len(chunks)

    def body(*refs):
        ins, vec_ref, outs, ov_ref = refs[:n], refs[n], refs[n + 1:2 * n + 1], refs[2 * n + 1]
        ici_send, ici_recv, d2d_send, d2d_recv, vec_send, vec_recv = refs[2 * n + 2:]
        x, y, c = _place()
        chip = 2 * x + y
        peers = [(1 - x, y), (x, 1 - y), (1 - x, 1 - y)]

        def rows(ci, half):
            k, r, cnt = chunks[ci]
            return k, pl.ds(half * specs[k][0] + r, cnt)

        def over_ici(j, ci, slab):
            px, py = peers[j]
            k, rs = rows(ci, c)
            return pltpu.make_async_remote_copy(
                src_ref=ins[k].at[rs], dst_ref=outs[k].at[slab, rs], send_sem=ici_send.at[j * nc + ci],
                recv_sem=ici_recv.at[j * nc + ci], device_id=(px, py, c), device_id_type=MESH)

        def over_d2d(j, ci, half):
            px, py = peers[j]
            k, rs = rows(ci, half)
            where = outs[k].at[2 * px + py, rs]
            return pltpu.make_async_remote_copy(
                src_ref=where, dst_ref=where, send_sem=d2d_send.at[j * nc + ci], recv_sem=d2d_recv.at[j * nc + ci],
                device_id=(x, y, 1 - c), device_id_type=MESH)

        def small(j, slab):
            px, py = peers[j]
            return pltpu.make_async_remote_copy(
                src_ref=vec_ref, dst_ref=ov_ref.at[slab], send_sem=vec_send.at[j], recv_sem=vec_recv.at[j],
                device_id=(px, py, c), device_id_type=MESH)

        sends = [small(j, chip) for j in range(3)] + [over_ici(j, ci, chip) for ci in range(nc) for j in range(3)]
        for cp in sends:
            cp.start()
        forwards = []
        for ci in range(nc):
            for j, (px, py) in enumerate(peers):
                over_ici(j, ci, 2 * px + py).wait_recv()
                forwards.append(over_d2d(j, ci, c))
                forwards[-1].start()
        for ci in range(nc):
            for j in range(3):
                over_d2d(j, ci, 1 - c).wait_recv()
        for j, (px, py) in enumerate(peers):
            small(j, 2 * px + py).wait_recv()
        for cp in sends + forwards:
            cp.wait_send()

    dma = pltpu.SemaphoreType.DMA
    return pl.pallas_call(
        body, name="gather_weights", in_specs=[ANY] * (n + 1), out_specs=[ANY] * (n + 1),
        out_shape=[SDS((4,) + a.shape, a.dtype) for a in arrs] + [SDS((4,) + vec.shape, vec.dtype)],
        scratch_shapes=[dma((3 * nc,)), dma((3 * nc,)), dma((3 * nc,)), dma((3 * nc,)), dma((3,)), dma((3,))])(*arrs, vec)


def _swap_classes(grads, specs, small):
    n = len(grads)
    chunks = _chunk_list(specs)

    def body(*refs):
        ins, s_ref, outs, os_ref, send_sems, recv_sems = refs[:n], refs[n], refs[n + 1:2 * n + 1], *refs[2 * n + 1:]
        x, y, c = _place()
        copies = []
        for s in range(4):
            for k, r, cnt in chunks:
                copies.append(pltpu.make_async_remote_copy(
                    src_ref=ins[k].at[s, 1 - c, pl.ds(r, cnt)], dst_ref=outs[k].at[s, pl.ds(r, cnt)],
                    send_sem=send_sems.at[len(copies)], recv_sem=recv_sems.at[len(copies)],
                    device_id=(x, y, 1 - c), device_id_type=MESH))
        copies.append(pltpu.make_async_remote_copy(
            src_ref=s_ref, dst_ref=os_ref, send_sem=send_sems.at[len(copies)], recv_sem=recv_sems.at[len(copies)],
            device_id=(x, y, 1 - c), device_id_type=MESH))
        for cp in copies:
            cp.start()
        for cp in copies:
            cp.wait_recv()
        for cp in copies:
            cp.wait_send()

    m = 4 * len(chunks) + 1
    return pl.pallas_call(
        body, name="swap_halves", in_specs=[ANY] * (n + 1), out_specs=[ANY] * (n + 1),
        out_shape=[SDS((4, g.shape[2], g.shape[3]), F32) for g in grads] + [SDS(small.shape, F32)],
        scratch_shapes=[pltpu.SemaphoreType.DMA((m,)), pltpu.SemaphoreType.DMA((m,))])(*grads, small)


def _pair_sum_class(g, other, chunk, name):
    _, _, half, w = g.shape

    def body(g_ref, o_ref, pb_ref, own_ref):
        x, y, c = _place()
        v = jnp.where(c == 0, g_ref[0], g_ref[1]) + o_ref[...]
        pb_ref[...] = v.astype(BF16)

        @pl.when(pl.program_id(1) == 2 * x + y)
        def _():
            own_ref[...] = v

    return pl.pallas_call(
        body, name=name, grid=(half // chunk, 4),
        in_specs=[pl.BlockSpec((None, 2, chunk, w), lambda i, s: (s, 0, i, 0)),
                  pl.BlockSpec((None, chunk, w), lambda i, s: (s, i, 0))],
        out_specs=[pl.BlockSpec((None, chunk, w), lambda i, s: (s, i, 0)), pl.BlockSpec((chunk, w), lambda i, s: (i, 0))],
        out_shape=[SDS((4, half, w), BF16), SDS((half, w), F32)],
        compiler_params=_params(2, VMEM_LIMIT))(g, other)


def _exchange_classes(pbs, specs, ps):
    n = len(pbs)
    chunks = _chunk_list(specs)
    per_peer = len(chunks) + 1

    def body(*refs):
        ins, ps_ref, outs, ls_ref, send_sems, recv_sems = refs[:n], refs[n], refs[n + 1:2 * n + 1], *refs[2 * n + 1:]
        x, y, c = _place()
        chip = 2 * x + y
        peers = [(1 - x, y), (x, 1 - y), (1 - x, 1 - y)]

        def copies(slab_of):
            out = []
            for j, (px, py) in enumerate(peers):
                for k, r, cnt in chunks:
                    out.append(pltpu.make_async_remote_copy(
                        src_ref=ins[k].at[2 * px + py, pl.ds(r, cnt)], dst_ref=outs[k].at[slab_of(j), pl.ds(r, cnt)],
                        send_sem=send_sems.at[len(out)], recv_sem=recv_sems.at[len(out)], device_id=(px, py, c),
                        device_id_type=MESH))
                out.append(pltpu.make_async_remote_copy(
                    src_ref=ps_ref, dst_ref=ls_ref.at[slab_of(j)], send_sem=send_sems.at[len(out)],
                    recv_sem=recv_sems.at[len(out)], device_id=(px, py, c), device_id_type=MESH))
            return out

        sends = copies(lambda j: chip)
        for cp in sends:
            cp.start()
        for cp in copies(lambda j: 2 * peers[j][0] + peers[j][1]):
            cp.wait_recv()
        for cp in sends:
            cp.wait_send()

    m = 3 * per_peer
    return pl.pallas_call(
        body, name="chip_exchange", in_specs=[ANY] * (n + 1), out_specs=[ANY] * (n + 1),
        out_shape=[SDS(p.shape, BF16) for p in pbs] + [SDS((4,) + ps.shape, F32)],
        scratch_shapes=[pltpu.SemaphoreType.DMA((m,)), pltpu.SemaphoreType.DMA((m,))])(*pbs, ps)


def _chip_sum_class(own, land, chunk, name):
    half, w = own.shape

    def body(own_ref, l_ref, o_ref):
        chip = 2 * lax.axis_index("x") + lax.axis_index("y")
        acc = jnp.where(chip == 0, own_ref[...], l_ref[0].astype(F32))
        for s in range(1, 4):
            acc = acc + jnp.where(chip == s, own_ref[...], l_ref[s].astype(F32))
        o_ref[...] = acc

    return pl.pallas_call(
        body, name=name, grid=(half // chunk,),
        in_specs=[pl.BlockSpec((chunk, w), lambda i: (i, 0)), pl.BlockSpec((4, chunk, w), lambda i: (0, i, 0))],
        out_specs=pl.BlockSpec((chunk, w), lambda i: (i, 0)), out_shape=SDS((half, w), F32),
        compiler_params=_params(1, VMEM_LIMIT))(own, land)


def _share_classes(rs, specs):
    n = len(rs)
    chunks = _chunk_list(specs)

    def body(*refs):
        ins, outs, send_sems, recv_sems = refs[:n], refs[n:2 * n], *refs[2 * n:]
        x, y, c = _place()
        copies = [pltpu.make_async_remote_copy(
            src_ref=ins[k].at[pl.ds(r, cnt)], dst_ref=outs[k].at[pl.ds(r, cnt)], send_sem=send_sems.at[i],
            recv_sem=recv_sems.at[i], device_id=(x, y, 1 - c), device_id_type=MESH)
            for i, (k, r, cnt) in enumerate(chunks)]
        for cp in copies:
            cp.start()
        for cp in copies:
            cp.wait_recv()
        for cp in copies:
            cp.wait_send()

    m = len(chunks)
    return pl.pallas_call(
        body, name="share_half", in_specs=[ANY] * n, out_specs=[ANY] * n, out_shape=[SDS(r.shape, F32) for r in rs],
        scratch_shapes=[pltpu.SemaphoreType.DMA((m,)), pltpu.SemaphoreType.DMA((m,))])(*rs)


_HBM = pl.BlockSpec(memory_space=pltpu.HBM)
_SEM = pl.BlockSpec(memory_space=pltpu.SEMAPHORE)
_EFFECT = pltpu.SideEffectType.DATAFLOW_SIDE_EFFECTING


def _chip_peers():
    x, y, c = _place()
    return [(1 - x, y, c), (x, 1 - y, c), (1 - x, 1 - y, c)]


def _send_shard_start(v, name):
    def body(v_ref, land_ref, send_sems, recv_sems, v_thru, land_thru, token):
        x, y, c = _place()
        for j, peer in enumerate(_chip_peers()):
            pltpu.make_async_remote_copy(src_ref=v_ref, dst_ref=land_ref.at[2 * x + y], send_sem=send_sems.at[j],
                                         recv_sem=recv_sems.at[j], device_id=peer, device_id_type=MESH).start()
        token[...] = jnp.zeros_like(token)

    land_shape = (4,) + v.shape
    return pl.pallas_call(
        body, name=name,
        out_shape=(pltpu.SemaphoreType.DMA((3,)), pltpu.SemaphoreType.DMA((3,)), pltpu.HBM(v.shape, v.dtype),
                   pltpu.HBM(land_shape, v.dtype), SDS((8, 128), F32)),
        in_specs=(_HBM, _HBM), out_specs=(_SEM, _SEM, _HBM, _HBM, pl.BlockSpec(memory_space=pltpu.VMEM)),
        input_output_aliases={0: 2, 1: 3}, compiler_params=pltpu.CompilerParams(has_side_effects=_EFFECT),
    )(pltpu.with_memory_space_constraint(v, pltpu.HBM),
      pltpu.with_memory_space_constraint(lax.empty(land_shape, v.dtype), pltpu.HBM))


def _xor_peer(r):
    x, y, c = _place()
    return (1 - x if (r >> 2) & 1 else x, 1 - y if (r >> 1) & 1 else y, 1 - c if r & 1 else c)


def _send_pieces_start(parts, name):
    n = len(parts)

    def body(*refs):
        ins, lands = refs[:n], refs[n:2 * n]
        send_sems, recv_sems = refs[2 * n:2 * n + 2]
        token = refs[-1]
        x, y, c = _place()
        for r in range(1, 8):
            px, py, pc = _xor_peer(r)
            for k in range(n):
                pltpu.make_async_remote_copy(
                    src_ref=ins[k].at[2 * px + py, pc], dst_ref=lands[k].at[4 * x + 2 * y + c],
                    send_sem=send_sems.at[(r - 1) * n + k], recv_sem=recv_sems.at[(r - 1) * n + k],
                    device_id=(px, py, pc), device_id_type=MESH).start()
        token[...] = jnp.zeros_like(token)

    land_shapes = [(8,) + p.shape[2:] for p in parts]
    hbm = [pltpu.HBM(p.shape, p.dtype) for p in parts] + [pltpu.HBM(s, p.dtype) for s, p in zip(land_shapes, parts)]
    operands = [pltpu.with_memory_space_constraint(p, pltpu.HBM) for p in parts]
    operands += [pltpu.with_memory_space_constraint(lax.empty(s, p.dtype), pltpu.HBM) for s, p in zip(land_shapes, parts)]
    return pl.pallas_call(
        body, name=name,
        out_shape=(pltpu.SemaphoreType.DMA((7 * n,)), pltpu.SemaphoreType.DMA((7 * n,)), *hbm, SDS((8, 128), F32)),
        in_specs=(_HBM,) * (2 * n), out_specs=(_SEM, _SEM) + (_HBM,) * (2 * n) + (pl.BlockSpec(memory_space=pltpu.VMEM),),
        input_output_aliases={i: 2 + i for i in range(2 * n)},
        compiler_params=pltpu.CompilerParams(has_side_effects=_EFFECT))(*operands)


def _send_pieces_wait(started, after, name):
    send_sems, recv_sems, *thru, _ = started
    n = len(thru) // 2

    def body(*refs):
        ins, lands = refs[:n], refs[n:2 * n]
        send_sems, recv_sems = refs[2 * n:2 * n + 2]
        for r in range(1, 8):
            px, py, pc = _xor_peer(r)
            for k in range(n):
                copy = pltpu.make_async_remote_copy(
                    src_ref=ins[k].at[2 * px + py, pc], dst_ref=lands[k].at[4 * px + 2 * py + pc],
                    send_sem=send_sems.at[(r - 1) * n + k], recv_sem=recv_sems.at[(r - 1) * n + k],
                    device_id=(px, py, pc), device_id_type=MESH)
                copy.wait_send()
                copy.wait_recv()

    return pl.pallas_call(
        body, name=name, out_shape=tuple(pltpu.HBM(t.shape, t.dtype) for t in thru),
        in_specs=(_HBM,) * (2 * n) + (_SEM, _SEM, pl.BlockSpec(memory_space=pl.ANY)), out_specs=(_HBM,) * (2 * n),
        input_output_aliases={i: i for i in range(2 * n)},
        compiler_params=pltpu.CompilerParams(has_side_effects=_EFFECT))(*thru, send_sems, recv_sems, after)[n:]


def _sum8_class(own, land, chunk, name):
    rows, w = own.shape

    def body(own_ref, l_ref, o_ref):
        x, y, c = _place()
        me = 4 * x + 2 * y + c
        acc = jnp.where(me == 0, own_ref[...], l_ref[0].astype(F32))
        for d in range(1, 8):
            acc = acc + jnp.where(me == d, own_ref[...], l_ref[d].astype(F32))
        o_ref[...] = acc

    return pl.pallas_call(
        body, name=name, grid=(rows // chunk,),
        in_specs=[pl.BlockSpec((chunk, w), lambda i: (i, 0)), pl.BlockSpec((8, chunk, w), lambda i: (0, i, 0))],
        out_specs=pl.BlockSpec((chunk, w), lambda i: (i, 0)), out_shape=SDS((rows, w), F32),
        compiler_params=_params(1, VMEM_LIMIT))(own, land)


def _send_shard_wait(send_sems, recv_sems, v_thru, land_thru, after, name):
    def body(v_ref, land_ref, send_sems, recv_sems, after_ref, v_dead, got_ref):
        for j, (px, py, pc) in enumerate(_chip_peers()):
            copy = pltpu.make_async_remote_copy(src_ref=v_ref, dst_ref=land_ref.at[2 * px + py], send_sem=send_sems.at[j],
                                                recv_sem=recv_sems.at[j], device_id=(px, py, pc), device_id_type=MESH)
            copy.wait_send()
            copy.wait_recv()

    return pl.pallas_call(
        body, name=name,
        out_shape=(pltpu.HBM(v_thru.shape, v_thru.dtype), pltpu.HBM(land_thru.shape, land_thru.dtype)),
        in_specs=(_HBM, _HBM, _SEM, _SEM, pl.BlockSpec(memory_space=pl.ANY)), out_specs=(_HBM, _HBM),
        input_output_aliases={0: 0, 1: 1}, compiler_params=pltpu.CompilerParams(has_side_effects=_EFFECT),
    )(v_thru, land_thru, send_sems, recv_sems, after)[1]


_SMALL = ["mem_norm_w", "norm_pre", "norm_post", "a_ln_w", "a_ln_b", "a_w_s", "a_b_s", "b_conv_w", "c_conv_w",
          "c_a_log", "c_dt_bias", "c_o_norm_w"]
_SMALL_SHAPES = {"mem_norm_w": (D,), "norm_pre": (4, D), "norm_post": (4, D), "a_ln_w": (2, D), "a_ln_b": (2, D),
                 "a_w_s": (2, 8, HD, HD), "a_b_s": (2, 8, HD), "b_conv_w": (1, 3, D), "c_conv_w": (1, 4, 3 * D),
                 "c_a_log": (1, 8), "c_dt_bias": (1, 8), "c_o_norm_w": (1, HD)}
_SHARDED_SMALL = {"a_ln_w": D // 4, "a_ln_b": D // 4, "b_conv_w": D // 4, "c_conv_w": 3 * D // 4}
_SMALL_ROWS = 288


def _size(shape):
    n = 1
    for d in shape:
        n *= d
    return n


def kernel(x, mem, mem_norm_w, w_mem_kv, norm_pre, norm_post, w_out, a_w_in, a_ln_w, a_ln_b, a_w_s, a_b_s, b_w_in, b_conv_w, c_w_in, c_conv_w, c_a_log, c_dt_bias, c_o_norm_w, loss_target, m_mem_norm_w, m_w_mem_kv, m_norm_pre, m_norm_post, m_w_out, m_a_w_in, m_a_ln_w, m_a_ln_b, m_a_w_s, m_a_b_s, m_b_w_in, m_b_conv_w, m_c_w_in, m_c_conv_w, m_c_a_log, m_c_dt_bias, m_c_o_norm_w, v_mem_norm_w, v_w_mem_kv, v_norm_pre, v_norm_post, v_w_out, v_a_w_in, v_a_ln_w, v_a_ln_b, v_a_w_s, v_a_b_s, v_b_w_in, v_b_conv_w, v_c_w_in, v_c_conv_w, v_c_a_log, v_c_dt_bias, v_c_o_norm_w):
    names = ["mem_norm_w", "w_mem_kv", "norm_pre", "norm_post", "w_out", "a_w_in", "a_ln_w", "a_ln_b", "a_w_s", "a_b_s",
             "b_w_in", "b_conv_w", "c_w_in", "c_conv_w", "c_a_log", "c_dt_bias", "c_o_norm_w"]
    w = dict(zip(names, [mem_norm_w, w_mem_kv, norm_pre, norm_post, w_out, a_w_in, a_ln_w, a_ln_b, a_w_s, a_b_s, b_w_in,
                         b_conv_w, c_w_in, c_conv_w, c_a_log, c_dt_bias, c_o_norm_w]))
    m = dict(zip(names, [m_mem_norm_w, m_w_mem_kv, m_norm_pre, m_norm_post, m_w_out, m_a_w_in, m_a_ln_w, m_a_ln_b, m_a_w_s,
                         m_a_b_s, m_b_w_in, m_b_conv_w, m_c_w_in, m_c_conv_w, m_c_a_log, m_c_dt_bias, m_c_o_norm_w]))
    v = dict(zip(names, [v_mem_norm_w, v_w_mem_kv, v_norm_pre, v_norm_post, v_w_out, v_a_w_in, v_a_ln_w, v_a_ln_b, v_a_w_s,
                         v_a_b_s, v_b_w_in, v_b_conv_w, v_c_w_in, v_c_conv_w, v_c_a_log, v_c_dt_bias, v_c_o_norm_w]))
    chip = 2 * lax.axis_index("x") + lax.axis_index("y")

    def rows_of_ct(a):
        return a[0].T

    def with_mine(gathered, own):
        return lax.dynamic_update_slice(gathered, own[None], (chip,) + (0,) * own.ndim)

    first = [a_w_in[0].astype(BF16)]
    vec = jnp.concatenate([a_ln_w.reshape(-1), a_ln_b.reshape(-1), b_conv_w.reshape(-1), c_conv_w.reshape(-1)])
    vec = jnp.pad(vec, (0, 8 * D - vec.shape[0])).reshape(8, D)
    ga0, gvec = _gather_classes(first, [W_CLASSES[k] for k in W_CLASSES], vec)
    ga0, gvec = with_mine(ga0, first[0]), with_mine(gvec, vec)
    gv = gvec.reshape(4, 8 * D)
    w_in = {1: b_w_in[0], 2: jnp.pad(rows_of_ct(c_w_in), ((0, C_ROWS - 1284), (0, 0))), 3: a_w_in[1]}
    order = [("kv", 0, w_mem_kv), ("out", 0, w_out[0])]
    for i in range(1, DEPTH):
        order += [("in", i, w_in[i]), ("out", i, w_out[i])]
    later, sent, started = {}, {}, 0.0 * ga0[0, 0, 0].astype(F32)
    for kind, i, a in order:
        later[kind, i] = (a + started).astype(BF16)
        sent[kind, i] = _send_shard_start(later[kind, i], f"send_w_{kind}_{i}")
        started = sent[kind, i][4][0, 0]

    def arrived(k, after):
        return with_mine(_send_shard_wait(*sent[k][:4], after, f"wait_w_{k[0]}_{k[1]}"), later[k])

    def blocks(i, after):
        def wo_after(later_value):
            return arrived(("out", i), later_value).reshape(D_CAT, D)

        if i == 0:
            return [ga0[0], ga0[1]], [ga0[2], ga0[3]], wo_after
        got = arrived(("in", i), after)
        if i == 1:
            return [got[0], got[1], got[2][:, :512]], [got[2][:, 512:], got[3]], wo_after
        if i == 3:
            return [got[0], got[1]], [got[2], got[3]], wo_after
        fct = got[:, :1284].reshape(5136, D)
        c_ab = jnp.concatenate([fct[3 * D:3 * D + 16], jnp.zeros((AB_PAD - 16, D), BF16)], axis=0)
        return [fct[:3 * D], c_ab], [fct[3 * D + 16:]], wo_after
    sm = {"mem_norm_w": mem_norm_w, "norm_pre": norm_pre + started, "norm_post": norm_post, "a_w_s": a_w_s, "a_b_s": a_b_s,
          "c_a_log": c_a_log, "c_dt_bias": c_dt_bias, "c_o_norm_w": c_o_norm_w,
          "a_ln_w": gv[:, 0:512].reshape(4, 2, 256).transpose(1, 0, 2).reshape(2, D),
          "a_ln_b": gv[:, 512:1024].reshape(4, 2, 256).transpose(1, 0, 2).reshape(2, D),
          "b_conv_w": gv[:, 1024:1792].reshape(4, 1, 3, 256).transpose(1, 2, 0, 3).reshape(1, 3, D),
          "c_conv_w": gv[:, 1792:4864].reshape(4, 1, 4, 768).transpose(1, 2, 0, 3).reshape(1, 4, 3 * D)}
    wts = {"wkv_after": lambda after: arrived(("kv", 0), after).reshape(D, 2 * D_XA), "blocks": blocks}

    def layer_grads(i, g):
        if i % 3 == 2:
            gct = jnp.concatenate([g["wm"][i][:3 * D + 16], g["wg"][i]], axis=0).reshape(4, 1284, D)
            w_in = jnp.pad(gct, ((0, 0), (0, C_ROWS - 1284), (0, 0)))
        else:
            w_in = jnp.concatenate([g["wm"][i], g["wg"][i]], axis=0).reshape(4, -1, GRAD_TILE[i % 3])
        out = {f"in{i}": w_in, f"out{i}": g["wo"][i].reshape(4, 384, D)}
        return {k: a.reshape(4, 2, a.shape[1] // 2, a.shape[2]) for k, a in out.items()}

    pending = {}

    def layer_done(i, g):
        halves = layer_grads(i, g)
        started = _send_pieces_start([h.astype(BF16) for h in halves.values()], f"send_grads_{i}")
        pending[i] = (started, halves)
        return started[-1][0, 0]

    wts["layer_done"] = layer_done

    loss, dx, g = _local_step(x[0], mem[0], loss_target[0], wts, sm)
    loss = lax.psum(loss, ("x", "y", "c"))

    core = lax.axis_index("c")
    mine, specs = {}, {}
    for i in reversed(range(DEPTH)):
        started, halves = pending[i]
        lands = _send_pieces_wait(started, dx, f"wait_grads_{i}")
        for (k, h), land in zip(halves.items(), lands):
            own = lax.dynamic_index_in_dim(lax.dynamic_index_in_dim(h, chip, 0, False), core, 0, False)
            specs[k] = (own.shape[0], _REDUCE_CHUNK.get(own.shape[0], own.shape[0]))
            mine[k] = _sum8_class(own, land, specs[k][1], f"sum8_{k}")

    first = {"kv": g["wkv"].reshape(4, 2, 128, D)}
    first_specs = [(h.shape[2], _REDUCE_CHUNK.get(h.shape[2], h.shape[2])) for h in first.values()]
    halves = list(first.values())
    gs = {"mem_norm_w": g["mem_norm_w"], "norm_pre": jnp.concatenate(g["norm_pre"]),
          "norm_post": jnp.concatenate(g["norm_post"])}
    for n in _SMALL[3:]:
        gs[n] = jnp.stack([g[n][j] for j in sorted(g[n])])
    flat = jnp.concatenate([gs[n].reshape(-1) for n in _SMALL])
    small = jnp.pad(flat, (0, _SMALL_ROWS * D - flat.shape[0])).reshape(_SMALL_ROWS, D)
    *others, other_small = _swap_classes(halves, first_specs, small)
    pairs = [_pair_sum_class(h, o, s[1], f"pair_sum_{k}") for k, h, o, s in zip(first, halves, others, first_specs)]
    pair_small = _add(small, other_small, "pair_sum_small")
    *lands, land_small = _exchange_classes([p[0] for p in pairs], first_specs, pair_small)
    for k, p, land, s in zip(first, pairs, lands, first_specs):
        mine[k], specs[k] = _chip_sum_class(p[1], land, s[1], f"chip_sum_{k}"), s
    theirs = _share_classes(list(mine.values()), [specs[k] for k in mine])
    south = core == 0
    sh = {k: jnp.concatenate([jnp.where(south, a, b), jnp.where(south, b, a)], axis=0)
          for (k, a), b in zip(mine.items(), theirs)}
    grads = {"a_w_in": jnp.stack([sh["in0"], sh["in3"]]),
             "b_w_in": sh["in1"].reshape(5, D, 256).transpose(1, 0, 2).reshape(b_w_in.shape),
             "c_w_in": sh["in2"][:1284], "w_out": jnp.stack([sh[f"out{i}"] for i in range(DEPTH)]),
             "w_mem_kv": sh["kv"]}
    flat = _sum4(pair_small, land_small).reshape(-1)
    off = 0
    for n in _SMALL:
        shape = _SMALL_SHAPES[n]
        full = flat[off:off + _size(shape)].reshape(shape)
        off += _size(shape)
        if n in _SHARDED_SMALL:
            full = lax.dynamic_slice_in_dim(full, chip * _SHARDED_SMALL[n], _SHARDED_SMALL[n], axis=len(shape) - 1)
        grads[n] = full

    delta, new_m, new_v = {}, {}, {}
    for n in names:
        shape = w[n].shape
        if n == "c_w_in":
            d_, m_, v_ = _adamw(rows_of_ct(w[n]), grads[n], rows_of_ct(m[n]), rows_of_ct(v[n]), f"adamw_{n}")
            delta[n], new_m[n], new_v[n], grads[n] = d_.T[None], m_.T[None], v_.T[None], grads[n].T[None]
            continue
        view = (1, shape[0]) if len(shape) == 1 else (_size(shape[:-1]), shape[-1])
        d_, m_, v_ = _adamw(w[n].reshape(view), grads[n].reshape(view), m[n].reshape(view), v[n].reshape(view),
                            f"adamw_{n}")
        delta[n], new_m[n], new_v[n] = d_.reshape(shape), m_.reshape(shape), v_.reshape(shape)
    return (loss, dx[None], *[grads[n].reshape(w[n].shape) for n in names], *[delta[n] for n in names],
            *[new_m[n] for n in names], *[new_v[n] for n in names])
```

```python
import functools

import jax
import jax.numpy as jnp
from jax import lax
from jax.experimental import pallas as pl
from jax.experimental.pallas import tpu as pltpu

F32 = jnp.float32
BF16 = jnp.bfloat16
HI = lax.Precision.HIGHEST
MESH = pl.DeviceIdType.MESH
SDS = jax.ShapeDtypeStruct

D = 1024
D_XA = 512
D_CAT = 1536
N_MEM = 256
HD = 128
DEPTH = 4
EPS = 1e-6
TT = 512
DN_C = 64
DN_TB = 256
HALO = 8
AB_PAD = 128
VMEM_LIMIT = 56 * 1024 * 1024
GRAD_TILE = {0: 1024, 1: 256}

ADAM_LR, ADAM_B1, ADAM_B2, ADAM_EPS, ADAM_WD, ADAM_STEP = 0.001, 0.9, 0.999, 1e-08, 0.01, 10


def _params(n_grid, vmem=None):
    return pltpu.CompilerParams(dimension_semantics=("arbitrary",) * n_grid, vmem_limit_bytes=vmem)


def _rms(x, w):
    return x * lax.rsqrt(jnp.mean(x * x, axis=-1, keepdims=True) + EPS) * w


def _dot_nn(a, b):
    return jnp.dot(a.astype(BF16), b.astype(BF16), preferred_element_type=F32)


def _dot_nt(a, b):
    return lax.dot_general(a.astype(BF16), b.astype(BF16), (((1,), (1,)), ((), ())), preferred_element_type=F32)


def _dot_tn(a, b):
    return lax.dot_general(a.astype(BF16), b.astype(BF16), (((0,), (0,)), ((), ())), preferred_element_type=F32)


@jax.custom_vjp
def mm(a, b):
    return _dot_nn(a, b)


mm.defvjp(lambda a, b: (_dot_nn(a, b), (a, b)), lambda r, g: (_dot_nt(g, r[1]), _dot_tn(r[0], g)))


@jax.custom_vjp
def mm_nt(a, b):
    return _dot_nt(a, b)


mm_nt.defvjp(lambda a, b: (_dot_nt(a, b), (a, b)), lambda r, g: (_dot_nn(g, r[1]), _dot_tn(g, r[0])))


def _row_spec(width, tile=TT):
    return pl.BlockSpec((tile, width), lambda i: (i, 0))


def _full_spec(shape):
    return pl.BlockSpec(shape, lambda *_: (0,) * len(shape))


def _widths(blocks, transposed):
    return [b.shape[0 if transposed else 1] for b in blocks]


def _inproj_fwd(x, nw, mix, gate, transposed, name):
    T, nm = x.shape[0], len(mix)
    M, G = sum(_widths(mix, transposed)), sum(_widths(gate, transposed))

    def body(x_ref, nw_ref, *refs):
        blocks, (pm_ref, pg_ref, h_ref) = refs[:-3], refs[-3:]
        h = _rms(x_ref[...], nw_ref[...]).astype(BF16)
        h_ref[...] = h
        for p_ref, group in ((pm_ref, blocks[:nm]), (pg_ref, blocks[nm:])):
            off = 0
            for w_ref in group:
                w = w_ref.shape[0 if transposed else 1]
                p_ref[:, off:off + w] = _dot_nt(h, w_ref[...]) if transposed else _dot_nn(h, w_ref[...])
                off += w

    return pl.pallas_call(
        body, name=name, grid=(T // TT,),
        in_specs=[_row_spec(D), _full_spec((1, D))] + [_full_spec(b.shape) for b in mix + gate],
        out_specs=[_row_spec(M), _row_spec(G), _row_spec(D)],
        out_shape=[SDS((T, M), F32), SDS((T, G), F32), SDS((T, D), BF16)],
        compiler_params=_params(1, VMEM_LIMIT))(x, nw, *mix, *gate)


def _inproj_bwd(dpm, dpg, x, nw, mix, gate, transposed, dxc, name):
    T, nm = x.shape[0], len(mix)
    M, G = sum(_widths(mix, transposed)), sum(_widths(gate, transposed))

    def body(dpm_ref, dpg_ref, x_ref, nw_ref, *refs):
        blocks, (dxc_ref, dx_ref, dnw_ref) = refs[:-3], refs[-3:]
        dh = None
        for dp_ref, group in ((dpm_ref, blocks[:nm]), (dpg_ref, blocks[nm:])):
            off = 0
            for w_ref in group:
                w = w_ref.shape[0 if transposed else 1]
                dp = dp_ref[:, off:off + w]
                part = _dot_nn(dp, w_ref[...]) if transposed else _dot_nt(dp, w_ref[...])
                dh = part if dh is None else dh + part
                off += w
        _, vjp = jax.vjp(_rms, x_ref[...], nw_ref[...])
        dxr, dnw = vjp(dh)
        dx_ref[...] = dxc_ref[...] + dxr

        @pl.when(pl.program_id(0) == 0)
        def _():
            dnw_ref[...] = jnp.zeros_like(dnw_ref)
        dnw_ref[...] += dnw

    return pl.pallas_call(
        body, name=name, grid=(T // TT,),
        in_specs=[_row_spec(M), _row_spec(G), _row_spec(D), _full_spec((1, D))]
        + [_full_spec(b.shape) for b in mix + gate] + [_row_spec(D)],
        out_specs=[_row_spec(D), _full_spec((1, D))],
        out_shape=[SDS((T, D), F32), SDS((1, D), F32)],
        compiler_params=_params(1, VMEM_LIMIT))(dpm, dpg, x, nw, *mix, *gate, dxc)


def _matmul_tn(a, b, name, sub=None):
    T, K = a.shape
    N = b.shape[1]
    tn = 1024 if N % 1024 == 0 else (640 if N % 640 == 0 else N)
    tt = min(1024, T)
    n_sub = 1 if sub is None else tn // sub

    def body(a_ref, b_ref, o_ref):
        @pl.when(pl.program_id(1) == 0)
        def _():
            o_ref[...] = jnp.zeros_like(o_ref)
        res = _dot_tn(a_ref[...], b_ref[...])
        if sub is None:
            o_ref[...] += res
        else:
            for i in range(n_sub):
                o_ref[i] += res[:, i * sub:(i + 1) * sub]

    if sub is None:
        out_spec, out_shape = pl.BlockSpec((K, tn), lambda j, t: (0, j)), SDS((K, N), F32)
    else:
        out_spec, out_shape = pl.BlockSpec((n_sub, K, sub), lambda j, t: (j, 0, 0)), SDS((N // sub, K, sub), F32)
    return pl.pallas_call(
        body, name=name, grid=(N // tn, T // tt),
        in_specs=[pl.BlockSpec((tt, K), lambda j, t: (t, 0)), pl.BlockSpec((tt, tn), lambda j, t: (t, j))],
        out_specs=out_spec, out_shape=out_shape,
        compiler_params=_params(2, VMEM_LIMIT))(a, b)


def _memkv_fn(mem, w, wkv):
    return mm(_rms(mem, w), wkv)


def _memkv_fwd(mem, w, wkv):
    def body(mem_ref, w_ref, wkv_ref, kv_ref):
        kv_ref[...] = _memkv_fn(mem_ref[...], w_ref[...], wkv_ref[...])

    return pl.pallas_call(body, name="memkv_fwd", out_shape=SDS((N_MEM, 2 * D_XA), F32),
                          compiler_params=_params(0, VMEM_LIMIT))(mem, w, wkv)


def _memkv_bwd(mem, w, wkv, dkv):
    def body(mem_ref, w_ref, wkv_ref, dkv_ref, dw_ref, dwkv_ref):
        _, vjp = jax.vjp(functools.partial(_memkv_fn, mem_ref[...]), w_ref[...], wkv_ref[...].astype(F32))
        dw, dwkv = vjp(dkv_ref[...])
        dw_ref[...] = dw
        dwkv_ref[...] = dwkv

    return pl.pallas_call(body, name="memkv_bwd", out_shape=[SDS((1, D), F32), SDS((D, 2 * D_XA), F32)],
                          compiler_params=_params(0, VMEM_LIMIT))(mem, w, wkv, dkv)


def _attn_gate(ymix, qx, z, *kvs):
    outs = []
    for j in range(4):
        s = mm_nt(qx[:, j * HD:(j + 1) * HD], kvs[j]) * (HD ** -0.5)
        e = jnp.exp(s - lax.stop_gradient(jnp.max(s, axis=-1, keepdims=True)))
        outs.append(mm(e / jnp.sum(e, axis=-1, keepdims=True), kvs[4 + j]))
    return jnp.concatenate([ymix] + outs, axis=1) * jax.nn.silu(z)


def _kv_blocks(kv_ref):
    return [kv_ref[:, j * HD:(j + 1) * HD] for j in range(8)]


def _ag_fwd(ymix, pg, kv, name):
    T = ymix.shape[0]

    def body(ymix_ref, pg_ref, kv_ref, ycat_ref):
        ycat_ref[...] = _attn_gate(ymix_ref[...], pg_ref[:, :D_XA], pg_ref[:, D_XA:], *_kv_blocks(kv_ref)).astype(BF16)

    return pl.pallas_call(
        body, name=name, grid=(T // TT,),
        in_specs=[_row_spec(D), _row_spec(D_XA + D_CAT), _full_spec((N_MEM, 2 * D_XA))],
        out_specs=_row_spec(D_CAT), out_shape=SDS((T, D_CAT), BF16),
        compiler_params=_params(1, VMEM_LIMIT))(ymix, pg, kv)


def _outproj_fwd(ycat, wo, x, nw, name):
    T = x.shape[0]

    def body(ycat_ref, wo_ref, x_ref, nw_ref, o_ref, xn_ref):
        o = jnp.dot(ycat_ref[...], wo_ref[...], preferred_element_type=F32)
        o_ref[...] = o
        xn_ref[...] = x_ref[...] + _rms(o, nw_ref[...])

    return pl.pallas_call(
        body, name=name, grid=(T // TT,),
        in_specs=[_row_spec(D_CAT), _full_spec((D_CAT, D)), _row_spec(D), _full_spec((1, D))],
        out_specs=[_row_spec(D), _row_spec(D)], out_shape=[SDS((T, D), F32), SDS((T, D), F32)],
        compiler_params=_params(1, VMEM_LIMIT))(ycat, wo, x, nw)


def _outproj_gate_bwd(dxo, o, nw, wo, ymix, pg, kv, dkv_in, name):
    T = dxo.shape[0]
    G = D_XA + D_CAT

    def body(dxo_ref, o_ref, nw_ref, wo_ref, ymix_ref, pg_ref, kv_ref, dkvin_ref,
             dobf_ref, dnw_ref, dymix_ref, dpg_ref, dkv_ref):
        _, vjp = jax.vjp(_rms, o_ref[...], nw_ref[...])
        do, dnw = vjp(dxo_ref[...])
        dobf = do.astype(BF16)
        dobf_ref[...] = dobf
        dycat = _dot_nt(dobf, wo_ref[...])
        _, vjp = jax.vjp(_attn_gate, ymix_ref[...], pg_ref[:, :D_XA], pg_ref[:, D_XA:], *_kv_blocks(kv_ref))
        g = vjp(dycat)
        dymix_ref[...] = g[0]
        dpg_ref[:, :D_XA] = g[1].astype(BF16)
        dpg_ref[:, D_XA:] = g[2].astype(BF16)

        @pl.when(pl.program_id(0) == 0)
        def _():
            dnw_ref[...] = jnp.zeros_like(dnw_ref)
            dkv_ref[...] = dkvin_ref[...]
        dnw_ref[...] += dnw
        for j in range(8):
            dkv_ref[:, j * HD:(j + 1) * HD] += g[3 + j]

    return pl.pallas_call(
        body, name=name, grid=(T // TT,),
        in_specs=[_row_spec(D), _row_spec(D), _full_spec((1, D)), _full_spec((D_CAT, D)), _row_spec(D), _row_spec(G),
                  _full_spec((N_MEM, 2 * D_XA)), _full_spec((N_MEM, 2 * D_XA))],
        out_specs=[_row_spec(D), _full_spec((1, D)), _row_spec(D), _row_spec(G), _full_spec((N_MEM, 2 * D_XA))],
        out_shape=[SDS((T, D), BF16), SDS((1, D), F32), SDS((T, D), F32), SDS((T, G), BF16),
                   SDS((N_MEM, 2 * D_XA), F32)],
        compiler_params=_params(1, VMEM_LIMIT))(dxo, o, nw, wo, ymix, pg, kv, dkv_in)


def _outproj_loss(ycat, wo, x, nw, target, name):
    T = x.shape[0]

    def body(ycat_ref, wo_ref, x_ref, nw_ref, t_ref, o_ref, loss_ref, dy_ref):
        o = jnp.dot(ycat_ref[...], wo_ref[...], preferred_element_type=F32)
        o_ref[...] = o
        err = x_ref[...] + _rms(o, nw_ref[...]) - t_ref[...]
        dy_ref[...] = err * (1.0 / D)

        @pl.when(pl.program_id(0) == 0)
        def _():
            loss_ref[...] = jnp.zeros_like(loss_ref)
        part = jnp.sum(jnp.sum(err * err, axis=1, keepdims=True), axis=0, keepdims=True) * (0.5 / D)
        loss_ref[...] += jnp.broadcast_to(part, loss_ref.shape)

    return pl.pallas_call(
        body, name=name, grid=(T // TT,),
        in_specs=[_row_spec(D_CAT), _full_spec((D_CAT, D)), _row_spec(D), _full_spec((1, D)), _row_spec(D)],
        out_specs=[_row_spec(D), _full_spec((8, 128)), _row_spec(D)],
        out_shape=[SDS((T, D), F32), SDS((8, 128), F32), SDS((T, D), F32)],
        compiler_params=_params(1, VMEM_LIMIT))(ycat, wo, x, nw, target)


def _gmlp_pre(u, v, lnw, lnb):
    vg = jax.nn.gelu(v)
    xc = vg - jnp.mean(vg, axis=-1, keepdims=True)
    vl = xc * lax.rsqrt(jnp.mean(xc * xc, axis=-1, keepdims=True) + EPS) * lnw + lnb
    return jax.nn.gelu(u), vl


def _tril(n, strict=False):
    r = lax.broadcasted_iota(jnp.int32, (n, n), 0)
    c = lax.broadcasted_iota(jnp.int32, (n, n), 1)
    return (r > c) if strict else (r >= c)


def _gmlp_fwd(pm, lnw, lnb, ws, bs3, name):
    T = pm.shape[0]

    def body(pm_ref, lnw_ref, lnb_ref, ws_ref, bs_ref, y_ref):
        ug, vl = _gmlp_pre(pm_ref[:, :D], pm_ref[:, D:], lnw_ref[...], lnb_ref[...])
        mask = _tril(HD)
        for g in range(8):
            w = jnp.where(mask, ws_ref[g], 0.0)
            for c in range(TT // HD):
                rows, cols = slice(c * HD, (c + 1) * HD), slice(g * HD, (g + 1) * HD)
                y_ref[rows, cols] = ug[rows, cols] * (_dot_nn(w, vl[rows, cols]) + bs_ref[g])

    return pl.pallas_call(
        body, name=name, grid=(T // TT,),
        in_specs=[_row_spec(2 * D), _full_spec((1, D)), _full_spec((1, D)), _full_spec((8, HD, HD)),
                  _full_spec((8, HD, HD))],
        out_specs=_row_spec(D), out_shape=SDS((T, D), F32),
        compiler_params=_params(1, VMEM_LIMIT))(pm, lnw, lnb, ws, bs3)


def _gmlp_bwd(dy, pm, lnw, lnb, ws, bs3, name):
    T = pm.shape[0]
    n_t = T // TT

    def body(dy_ref, pm_ref, lnw_ref, lnb_ref, ws_ref, bs_ref, dpm_ref, dlnw_ref, dlnb_ref, dws_ref, dbs_ref,
             dug_scr, dvl_scr, dbs_scr):
        i = pl.program_id(0)

        @pl.when(i == 0)
        def _():
            dlnw_ref[...] = jnp.zeros_like(dlnw_ref)
            dlnb_ref[...] = jnp.zeros_like(dlnb_ref)
            dws_ref[...] = jnp.zeros_like(dws_ref)
            dbs_scr[...] = jnp.zeros_like(dbs_scr)

        (ug, vl), vjp = jax.vjp(_gmlp_pre, pm_ref[:, :D], pm_ref[:, D:], lnw_ref[...], lnb_ref[...])
        mask = _tril(HD)
        for g in range(8):
            w = jnp.where(mask, ws_ref[g], 0.0)
            dw = jnp.zeros((HD, HD), F32)
            db = jnp.zeros((HD, HD), F32)
            for c in range(TT // HD):
                rows, cols = slice(c * HD, (c + 1) * HD), slice(g * HD, (g + 1) * HD)
                dyb, vlb = dy_ref[rows, cols], vl[rows, cols]
                sp = _dot_nn(w, vlb) + bs_ref[g]
                dsp = dyb * ug[rows, cols]
                dug_scr[rows, cols] = dyb * sp
                dvl_scr[rows, cols] = _dot_tn(w, dsp)
                dw += _dot_nt(dsp, vlb)
                db += dsp
            dws_ref[g] += jnp.where(mask, dw, 0.0)
            dbs_scr[g] += db
        du, dv, dlnw, dlnb = vjp((dug_scr[...], dvl_scr[...]))
        dpm_ref[:, :D] = du.astype(BF16)
        dpm_ref[:, D:] = dv.astype(BF16)
        dlnw_ref[...] += dlnw
        dlnb_ref[...] += dlnb

        @pl.when(i == n_t - 1)
        def _():
            for g in range(8):
                dbs_ref[g] = jnp.broadcast_to(jnp.sum(dbs_scr[g], axis=1, keepdims=True), (HD, HD))

    return pl.pallas_call(
        body, name=name, grid=(n_t,),
        in_specs=[_row_spec(D), _row_spec(2 * D), _full_spec((1, D)), _full_spec((1, D)), _full_spec((8, HD, HD)),
                  _full_spec((8, HD, HD))],
        out_specs=[_row_spec(2 * D), _full_spec((1, D)), _full_spec((1, D)), _full_spec((8, HD, HD)),
                   _full_spec((8, HD, HD))],
        out_shape=[SDS((T, 2 * D), BF16), SDS((1, D), F32), SDS((1, D), F32), SDS((8, HD, HD), F32),
                   SDS((8, HD, HD), F32)],
        scratch_shapes=[pltpu.VMEM((TT, D), F32), pltpu.VMEM((TT, D), F32), pltpu.VMEM((8, HD, HD), F32)],
        compiler_params=_params(1, VMEM_LIMIT))(dy, pm, lnw, lnb, ws, bs3)


def _prev_spec(width, T):
    return pl.BlockSpec((HALO, width), lambda i: (jnp.maximum(i * (TT // HALO) - 1, 0), 0))


def _next_spec(width, T):
    return pl.BlockSpec((HALO, width), lambda i: (jnp.minimum((i + 1) * (TT // HALO), T // HALO - 1), 0))


def _rows_before(ext, j):
    return ext[HALO:] if j == 0 else pltpu.roll(ext, j, 0)[HALO:]


def _rows_after(ext, j):
    n = ext.shape[0]
    return ext[:n - HALO] if j == 0 else pltpu.roll(ext, n - j, 0)[:n - HALO]


def _conv_apply(ext_s, w):
    K = w.shape[0]
    y = _rows_before(ext_s, K - 1) * w[0:1]
    for k in range(1, K):
        y = y + _rows_before(ext_s, K - 1 - k) * w[k:k + 1]
    return y


def _conv_grads(ext_s, ext_dy, w):
    K = w.shape[0]
    dy = ext_dy[:ext_dy.shape[0] - HALO]
    ds = _rows_after(ext_dy, K - 1) * w[0:1]
    dws = [jnp.sum(dy * _rows_before(ext_s, K - 1), axis=0, keepdims=True)]
    for k in range(1, K):
        ds = ds + _rows_after(ext_dy, K - 1 - k) * w[k:k + 1]
        dws.append(jnp.sum(dy * _rows_before(ext_s, K - 1 - k), axis=0, keepdims=True))
    return ds, jnp.concatenate(dws, axis=0)


def _sconv_fwd(pm, w, name):
    T = pm.shape[0]

    def body(pm_ref, prev_ref, w_ref, y_ref):
        s = pm_ref[:, D:2 * D] * pm_ref[:, 2 * D:]
        sp = jnp.where(pl.program_id(0) > 0, prev_ref[:, D:2 * D] * prev_ref[:, 2 * D:], 0.0)
        y_ref[...] = pm_ref[:, :D] * _conv_apply(jnp.concatenate([sp, s], axis=0), w_ref[...])

    return pl.pallas_call(
        body, name=name, grid=(T // TT,),
        in_specs=[_row_spec(3 * D), _prev_spec(3 * D, T), _full_spec((3, D))],
        out_specs=_row_spec(D), out_shape=SDS((T, D), F32),
        compiler_params=_params(1, VMEM_LIMIT))(pm, pm, w)


def _sconv_bwd(dy, pm, w, name):
    T = pm.shape[0]
    n_t = T // TT

    def body(dy_ref, dyn_ref, pm_ref, prev_ref, next_ref, w_ref, dpm_ref, dw_ref):
        i = pl.program_id(0)
        bg, cg, hv = pm_ref[:, :D], pm_ref[:, D:2 * D], pm_ref[:, 2 * D:]
        sp = jnp.where(i > 0, prev_ref[:, D:2 * D] * prev_ref[:, 2 * D:], 0.0)
        ext_s = jnp.concatenate([sp, cg * hv], axis=0)
        dyv = dy_ref[...]
        dcn = jnp.where(i < n_t - 1, dyn_ref[...] * next_ref[:, :D], 0.0)
        ds, dw = _conv_grads(ext_s, jnp.concatenate([dyv * bg, dcn], axis=0), w_ref[...])
        dpm_ref[:, :D] = (dyv * _conv_apply(ext_s, w_ref[...])).astype(BF16)
        dpm_ref[:, D:2 * D] = (ds * hv).astype(BF16)
        dpm_ref[:, 2 * D:] = (ds * cg).astype(BF16)

        @pl.when(i == 0)
        def _():
            dw_ref[...] = jnp.zeros_like(dw_ref)
        dw_ref[...] += dw

    return pl.pallas_call(
        body, name=name, grid=(n_t,),
        in_specs=[_row_spec(D), _next_spec(D, T), _row_spec(3 * D), _prev_spec(3 * D, T), _next_spec(3 * D, T),
                  _full_spec((3, D))],
        out_specs=[_row_spec(3 * D), _full_spec((3, D))],
        out_shape=[SDS((T, 3 * D), BF16), SDS((3, D), F32)],
        compiler_params=_params(1, VMEM_LIMIT))(dy, dy, pm, pm, pm, w)


def _dnconv_fwd(pm, w, name):
    T = pm.shape[0]

    def body(pm_ref, prev_ref, w_ref, c_ref):
        sp = jnp.where(pl.program_id(0) > 0, prev_ref[...], 0.0)
        c_ref[...] = _conv_apply(jnp.concatenate([sp, pm_ref[...]], axis=0), w_ref[...])

    return pl.pallas_call(
        body, name=name, grid=(T // TT,),
        in_specs=[_row_spec(3 * D), _prev_spec(3 * D, T), _full_spec((4, 3 * D))],
        out_specs=_row_spec(3 * D), out_shape=SDS((T, 3 * D), F32),
        compiler_params=_params(1, VMEM_LIMIT))(pm, pm, w)


def _dnconv_bwd(dcq, dck, dcv, dab, pm, w, name):
    T = pm.shape[0]
    n_t = T // TT

    def body(dq_ref, dk_ref, dv_ref, dqn_ref, dkn_ref, dvn_ref, dab_ref, pm_ref, prev_ref, w_ref, dpm_ref, dw_ref):
        i = pl.program_id(0)
        sp = jnp.where(i > 0, prev_ref[...], 0.0)
        ext_s = jnp.concatenate([sp, pm_ref[...]], axis=0)
        own = jnp.concatenate([dq_ref[...], dk_ref[...], dv_ref[...]], axis=1)
        nxt = jnp.where(i < n_t - 1, jnp.concatenate([dqn_ref[...], dkn_ref[...], dvn_ref[...]], axis=1), 0.0)
        ds, dw = _conv_grads(ext_s, jnp.concatenate([own, nxt], axis=0), w_ref[...])
        dpm_ref[:, :3 * D] = ds.astype(BF16)
        dpm_ref[:, 3 * D:] = dab_ref[...].astype(BF16)

        @pl.when(i == 0)
        def _():
            dw_ref[...] = jnp.zeros_like(dw_ref)
        dw_ref[...] += dw

    return pl.pallas_call(
        body, name=name, grid=(n_t,),
        in_specs=[_row_spec(D), _row_spec(D), _row_spec(D), _next_spec(D, T), _next_spec(D, T), _next_spec(D, T),
                  _row_spec(AB_PAD), _row_spec(3 * D), _prev_spec(3 * D, T), _full_spec((4, 3 * D))],
        out_specs=[_row_spec(3 * D + AB_PAD), _full_spec((4, 3 * D))],
        out_shape=[SDS((T, 3 * D + AB_PAD), BF16), SDS((4, 3 * D), F32)],
        compiler_params=_params(1, VMEM_LIMIT))(dcq, dck, dcv, dcq, dck, dcv, dab, pm, pm, w)


def _l2n(x):
    return x * lax.rsqrt(jnp.sum(x * x, axis=-1, keepdims=True) + EPS)


def _softplus(x):
    return jnp.maximum(x, 0.0) + jnp.log1p(jnp.exp(-jnp.abs(x)))


_BNN = (((2,), (1,)), ((0,), (0,)))
_BNT = (((2,), (2,)), ((0,), (0,)))
_BTN = (((1,), (1,)), ((0,), (0,)))


def _bdot(a, b, dims):
    return lax.dot_general(a.astype(BF16), b.astype(BF16), dims, preferred_element_type=F32)


def _bdot3(a, b, dims):
    ah, bh = a.astype(BF16), b.astype(BF16)
    al, bl = (a - ah.astype(F32)).astype(BF16), (b - bh.astype(F32)).astype(BF16)
    d = functools.partial(lax.dot_general, dimension_numbers=dims, preferred_element_type=F32)
    return d(ah, bh) + (d(ah, bl) + d(al, bh))


def _bdot_hi(a, b, dims):
    return lax.dot_general(a, b, dims, precision=HI, preferred_element_type=F32)


def _batched_matmuls(dot):
    @jax.custom_vjp
    def nn(a, b):
        return dot(a, b, _BNN)

    @jax.custom_vjp
    def nt(a, b):
        return dot(a, b, _BNT)

    @jax.custom_vjp
    def tn(a, b):
        return dot(a, b, _BTN)

    nn.defvjp(lambda a, b: (dot(a, b, _BNN), (a, b)), lambda r, g: (dot(g, r[1], _BNT), dot(r[0], g, _BTN)))
    nt.defvjp(lambda a, b: (dot(a, b, _BNT), (a, b)), lambda r, g: (dot(g, r[1], _BNN), dot(g, r[0], _BTN)))
    tn.defvjp(lambda a, b: (dot(a, b, _BTN), (a, b)), lambda r, g: (dot(r[1], g, _BNT), dot(r[0], g, _BNN)))
    return nn, nt, tn


bmm, bmm_nt, bmm_tn = _batched_matmuls(_bdot)
bmm_hi, _, _ = _batched_matmuls(_bdot_hi)

@jax.custom_vjp
def _neumann_inverse(n):
    C = n.shape[1]
    eye = lax.broadcasted_iota(jnp.int32, n.shape, 1) == lax.broadcasted_iota(jnp.int32, n.shape, 2)
    t = eye.astype(F32) + n
    for _ in range(5):
        n = _bdot3(n, n, _BNN)
        t = t + _bdot3(t, n, _BNN)
    return t


def _neumann_inverse_fwd(n):
    t = _neumann_inverse(n)
    return t, t


def _neumann_inverse_bwd(t, g):
    return (_bdot3(_bdot3(t, g, _BTN), t, _BNT),)


_neumann_inverse.defvjp(_neumann_inverse_fwd, _neumann_inverse_bwd)


@jax.custom_vjp
def _saved_inverse(n, t):
    return t


_saved_inverse.defvjp(lambda n, t: (t, t), lambda t, g: (_bdot3(_bdot3(t, g, _BTN), t, _BNT), jnp.zeros_like(t)))

DN_NCH = DN_TB // DN_C
DN_NH = 4


def _decay_terms(ab, alog, dtb, first_head, n_heads):
    C = DN_C
    lane = lax.broadcasted_iota(jnp.int32, ab.shape, 1)
    g_all = (-jnp.exp(alog) * _softplus(ab + dtb)).reshape(DN_NCH, C, HD)
    beta_all = jax.nn.sigmoid(ab)
    r = lax.broadcasted_iota(jnp.int32, (DN_NCH, C, C), 1)
    c = lax.broadcasted_iota(jnp.int32, (DN_NCH, C, C), 2)
    gc_all = bmm_hi((r >= c).astype(F32), g_all)
    gc_rows = [gc_all[i].T for i in range(DN_NCH)]
    lane3 = lax.broadcasted_iota(jnp.int32, (DN_NCH, C, HD), 2)
    row = lax.broadcasted_iota(jnp.int32, (HD, C), 0)
    ones = jnp.ones((1, HD), F32)
    gcs, gjs, betas = [], [], []
    for i in range(n_heads):
        h = first_head + i
        gcs.append(jnp.sum(jnp.where(lane3 == h, gc_all, 0.0), axis=2, keepdims=True) * ones)
        gjs.append(jnp.concatenate(
            [jnp.broadcast_to(jnp.sum(jnp.where(row == h, t, 0.0), axis=0, keepdims=True), (C, C))[None] for t in gc_rows],
            axis=0))
        beta = jnp.sum(jnp.where(lane == 8 + h, beta_all, 0.0), axis=1, keepdims=True) * ones
        betas.append(beta.reshape(DN_NCH, C, HD))
    return jnp.concatenate(gcs, axis=0), jnp.concatenate(gjs, axis=0), jnp.concatenate(betas, axis=0)


def _dn_prep(cq, ck, cv, gcum, gj, bb, t_saved=None):
    B, C = cq.shape[0], DN_C
    q = _l2n(jax.nn.silu(cq)) * (HD ** -0.5)
    k = _l2n(jax.nn.silu(ck))
    v = jax.nn.silu(cv)
    r = lax.broadcasted_iota(jnp.int32, (B, C, C), 1)
    c = lax.broadcasted_iota(jnp.int32, (B, C, C), 2)
    incl, strict = r >= c, r > c
    decay = jnp.where(incl, jnp.exp(jnp.where(incl, gcum[:, :, :C] - gj, 0.0)), 0.0)
    kb = k * bb
    n_mat = -jnp.where(strict, bmm_nt(kb, k) * decay, 0.0)
    t_mat = _neumann_inverse(n_mat) if t_saved is None else _saved_inverse(n_mat, t_saved)
    eg = jnp.exp(gcum)
    glast = gcum[:, C - 1:C, :]
    return (bmm(t_mat, v * bb), bmm(t_mat, kb * eg), bmm_nt(q, k) * decay, q * eg, k * jnp.exp(glast - gcum),
            jnp.exp(glast), t_mat)


def _dn_scan_step(u, w, qk, qd, kd, egl, S, onw):
    v_new = u - bmm(w, S)
    o = bmm(qd, S) + bmm(qk, v_new)
    return _rms(o, onw), S * egl + bmm_tn(kd, v_new)


def _to_batch(ref, n_heads):
    return jnp.concatenate([ref[:, i * HD:(i + 1) * HD].astype(F32).reshape(DN_NCH, DN_C, HD) for i in range(n_heads)],
                           axis=0)


def _from_batch(ref, val, n_heads):
    for i in range(n_heads):
        ref[:, i * HD:(i + 1) * HD] = val[i * DN_NCH:(i + 1) * DN_NCH].reshape(DN_TB, HD).astype(ref.dtype)


def _prep_specs(T, rev):
    nb = T // DN_TB
    blk = (lambda n: nb - 1 - n) if rev else (lambda n: n)
    ng = 8 // DN_NH
    head = [pl.BlockSpec((DN_TB, DN_NH * HD), functools.partial(lambda n, h, off: (blk(n), off + h), off=ng * s))
            for s in range(3)]
    ab = pl.BlockSpec((DN_TB, AB_PAD), lambda n, h: (blk(n), 3 * D // AB_PAD))
    row = pl.BlockSpec((1, HD), lambda n, h: (0, 0))
    wide = pl.BlockSpec((DN_TB, DN_NH * HD), lambda n, h: (blk(n), h))
    qk = pl.BlockSpec((DN_NCH, DN_NH, DN_C, DN_C), lambda n, h: (blk(n), h, 0, 0))
    eg = pl.BlockSpec((DN_NCH, DN_NH, 1, HD), lambda n, h: (blk(n), h, 0, 0))
    return nb, ng, head, ab, row, wide, qk, eg


def _dn_prep_fwd(cpre, pm, alog, dtb, name):
    T = cpre.shape[0]
    nb, ng, head, ab, row, wide, qks, egs = _prep_specs(T, False)

    def body(cq_ref, ck_ref, cv_ref, ab_ref, alog_ref, dtb_ref, u_ref, w_ref, qk_ref, qd_ref, kd_ref, e_ref, t_ref):
        gcum, gj, bb = _decay_terms(ab_ref[...], alog_ref[...], dtb_ref[...], pl.program_id(1) * DN_NH, DN_NH)
        u, w, qk, qd, kd, egl, t_mat = _dn_prep(_to_batch(cq_ref, DN_NH), _to_batch(ck_ref, DN_NH),
                                                _to_batch(cv_ref, DN_NH), gcum, gj, bb)
        _from_batch(u_ref, u, DN_NH)
        _from_batch(w_ref, w, DN_NH)
        _from_batch(qd_ref, qd, DN_NH)
        _from_batch(kd_ref, kd, DN_NH)
        for i in range(DN_NH):
            qk_ref[:, i] = qk[i * DN_NCH:(i + 1) * DN_NCH].astype(BF16)
            e_ref[:, i] = egl[i * DN_NCH:(i + 1) * DN_NCH]
            t_ref[:, i] = t_mat[i * DN_NCH:(i + 1) * DN_NCH]

    return pl.pallas_call(
        body, name=name, grid=(nb, ng), in_specs=head + [ab, row, row],
        out_specs=[wide, wide, qks, wide, wide, egs, qks],
        out_shape=[SDS((T, D), F32), SDS((T, D), BF16), SDS((T // DN_C, 8, DN_C, DN_C), BF16), SDS((T, D), BF16),
                   SDS((T, D), BF16), SDS((T // DN_C, 8, 1, HD), F32), SDS((T // DN_C, 8, DN_C, DN_C), F32)],
        compiler_params=_params(2, VMEM_LIMIT))(cpre, cpre, cpre, pm, alog, dtb)


def _dn_prep_bwd(du, dw, dqk, dqd, dkd, degl, t_mat, cpre, pm, alog, dtb, name):
    T = cpre.shape[0]
    nb, ng, head, ab, row, wide, qks, egs = _prep_specs(T, True)

    def body(du_ref, dw_ref, dqk_ref, dqd_ref, dkd_ref, de_ref, t_ref, cq_ref, ck_ref, cv_ref, ab_ref, alog_ref,
             dtb_ref, dcq_ref, dck_ref, dcv_ref, dab_ref, dalog_ref, ddtb_ref):
        n, h = pl.program_id(0), pl.program_id(1)

        @pl.when((n == 0) & (h == 0))
        def _():
            dalog_ref[...] = jnp.zeros_like(dalog_ref)
            ddtb_ref[...] = jnp.zeros_like(ddtb_ref)

        @pl.when(h == 0)
        def _():
            dab_ref[...] = jnp.zeros_like(dab_ref)

        t_saved = jnp.concatenate([t_ref[:, i] for i in range(DN_NH)], axis=0)

        def fwd(cq, ck, cv, ab_v, alog_v, dtb_v):
            gcum, gj, bb = _decay_terms(ab_v, alog_v, dtb_v, h * DN_NH, DN_NH)
            return _dn_prep(cq, ck, cv, gcum, gj, bb, t_saved)[:6]

        _, vjp = jax.vjp(fwd, _to_batch(cq_ref, DN_NH), _to_batch(ck_ref, DN_NH), _to_batch(cv_ref, DN_NH), ab_ref[...],
                         alog_ref[...], dtb_ref[...])
        cot = (_to_batch(du_ref, DN_NH), _to_batch(dw_ref, DN_NH),
               jnp.concatenate([dqk_ref[:, i] for i in range(DN_NH)], axis=0), _to_batch(dqd_ref, DN_NH),
               _to_batch(dkd_ref, DN_NH), jnp.concatenate([de_ref[:, i] for i in range(DN_NH)], axis=0))
        dcq, dck, dcv, dab, dalog, ddtb = vjp(cot)
        _from_batch(dcq_ref, dcq, DN_NH)
        _from_batch(dck_ref, dck, DN_NH)
        _from_batch(dcv_ref, dcv, DN_NH)
        dab_ref[...] += dab
        dalog_ref[...] += dalog
        ddtb_ref[...] += ddtb

    dabspec = pl.BlockSpec((DN_TB, AB_PAD), lambda n, h: (nb - 1 - n, 0))
    return pl.pallas_call(
        body, name=name, grid=(nb, ng),
        in_specs=[wide, wide, qks, wide, wide, egs, qks] + head + [ab, row, row],
        out_specs=[wide, wide, wide, dabspec, row, row],
        out_shape=[SDS((T, D), F32)] * 3 + [SDS((T, AB_PAD), F32)] + [SDS((1, HD), F32)] * 2,
        compiler_params=_params(2, VMEM_LIMIT))(du, dw, dqk, dqd, dkd, degl, t_mat, cpre, cpre, cpre, pm, alog, dtb)


def _scan_specs(T, rev):
    nb = T // DN_TB
    blk = (lambda n: nb - 1 - n) if rev else (lambda n: n)
    wide = pl.BlockSpec((DN_TB, D), lambda n: (blk(n), 0))
    qk = pl.BlockSpec((DN_NCH, 8, DN_C, DN_C), lambda n: (blk(n), 0, 0, 0))
    eg = pl.BlockSpec((DN_NCH, 8, 1, HD), lambda n: (blk(n), 0, 0, 0))
    st = pl.BlockSpec((DN_NCH, 8, HD, HD), lambda n: (blk(n), 0, 0, 0))
    row = pl.BlockSpec((1, HD), lambda n: (0, 0))
    return nb, wide, qk, eg, st, row


def _heads_of(ref, rows):
    return jnp.concatenate([ref[rows, h * HD:(h + 1) * HD].astype(F32)[None] for h in range(8)], axis=0)


def _dn_scan_fwd(u, w, qk, qd, kd, egl, onw, name):
    T = u.shape[0]
    nb, wide, qks, egs, sts, row = _scan_specs(T, False)

    def body(u_ref, w_ref, qk_ref, qd_ref, kd_ref, e_ref, onw_ref, o_ref, st_ref, s_scr):
        @pl.when(pl.program_id(0) == 0)
        def _():
            s_scr[...] = jnp.zeros_like(s_scr)
        S = s_scr[...]
        for c in range(DN_NCH):
            rows = slice(c * DN_C, (c + 1) * DN_C)
            st_ref[c] = S
            o, S = _dn_scan_step(_heads_of(u_ref, rows), _heads_of(w_ref, rows), qk_ref[c].astype(F32),
                                 _heads_of(qd_ref, rows), _heads_of(kd_ref, rows), e_ref[c], S, onw_ref[...])
            for h in range(8):
                o_ref[rows, h * HD:(h + 1) * HD] = o[h]
        s_scr[...] = S

    return pl.pallas_call(
        body, name=name, grid=(nb,), in_specs=[wide, wide, qks, wide, wide, egs, row], out_specs=[wide, sts],
        out_shape=[SDS((T, D), F32), SDS((T // DN_C, 8, HD, HD), F32)],
        scratch_shapes=[pltpu.VMEM((8, HD, HD), F32)],
        compiler_params=_params(1, VMEM_LIMIT))(u, w, qk, qd, kd, egl, onw)


def _dn_scan_bwd(do, u, w, qk, qd, kd, egl, st, onw, name):
    T = u.shape[0]
    nb, wide, qks, egs, sts, row = _scan_specs(T, True)

    def body(do_ref, u_ref, w_ref, qk_ref, qd_ref, kd_ref, e_ref, st_ref, onw_ref,
             du_ref, dw_ref, dqk_ref, dqd_ref, dkd_ref, de_ref, donw_ref, ds_scr):
        @pl.when(pl.program_id(0) == 0)
        def _():
            ds_scr[...] = jnp.zeros_like(ds_scr)
            donw_ref[...] = jnp.zeros_like(donw_ref)
        dS = ds_scr[...]
        donw = jnp.zeros((1, HD), F32)
        for c in reversed(range(DN_NCH)):
            rows = slice(c * DN_C, (c + 1) * DN_C)
            _, vjp = jax.vjp(_dn_scan_step, _heads_of(u_ref, rows), _heads_of(w_ref, rows), qk_ref[c].astype(F32),
                             _heads_of(qd_ref, rows), _heads_of(kd_ref, rows), e_ref[c], st_ref[c], onw_ref[...])
            du, dw, dqk, dqd, dkd, de, dS, dn = vjp((_heads_of(do_ref, rows), dS))
            for h in range(8):
                cols = slice(h * HD, (h + 1) * HD)
                du_ref[rows, cols] = du[h]
                dw_ref[rows, cols] = dw[h]
                dqd_ref[rows, cols] = dqd[h]
                dkd_ref[rows, cols] = dkd[h]
            dqk_ref[c] = dqk
            de_ref[c] = de
            donw += dn
        ds_scr[...] = dS
        donw_ref[...] += donw

    return pl.pallas_call(
        body, name=name, grid=(nb,), in_specs=[wide, wide, wide, qks, wide, wide, egs, sts, row],
        out_specs=[wide, wide, qks, wide, wide, egs, row],
        out_shape=[SDS((T, D), F32), SDS((T, D), F32), SDS((T // DN_C, 8, DN_C, DN_C), F32), SDS((T, D), F32),
                   SDS((T, D), F32), SDS((T // DN_C, 8, 1, HD), F32), SDS((1, HD), F32)],
        scratch_shapes=[pltpu.VMEM((8, HD, HD), F32)],
        compiler_params=_params(1, VMEM_LIMIT))(do, u, w, qk, qd, kd, egl, st, onw)


def _adamw(w, g, m, v, name):
    R, C = w.shape
    tr = 256 if R % 256 == 0 and R > 256 else R
    tc = 256 if tr == R and R > 256 and C % 256 == 0 else C
    c1 = 1.0 - ADAM_B1 ** ADAM_STEP
    c2 = 1.0 - ADAM_B2 ** ADAM_STEP

    def body(w_ref, g_ref, m_ref, v_ref, d_ref, nm_ref, nv_ref):
        gv = g_ref[...]
        nm = ADAM_B1 * m_ref[...] + (1.0 - ADAM_B1) * gv
        nv = ADAM_B2 * v_ref[...] + (1.0 - ADAM_B2) * (gv * gv)
        nm_ref[...] = nm
        nv_ref[...] = nv
        d_ref[...] = -ADAM_LR * ((nm / c1) / (jnp.sqrt(nv / c2) + ADAM_EPS) + ADAM_WD * w_ref[...])

    spec = pl.BlockSpec((tr, tc), lambda i, j: (i, j))
    return pl.pallas_call(
        body, name=name, grid=(R // tr, C // tc), in_specs=[spec] * 4, out_specs=[spec] * 3,
        out_shape=[SDS((R, C), F32)] * 3, compiler_params=_params(2, VMEM_LIMIT))(w, g, m, v)


def _local_step(x, mem, target, wts, sm):
    kinds = [i % 3 for i in range(DEPTH)]
    mnw = sm["mem_norm_w"].reshape(1, D)
    kv, wkv = None, None
    saved, blocks = [], []
    for i, kind in enumerate(kinds):
        j = i // 3
        npre = sm["norm_pre"][i].reshape(1, D)
        npost = sm["norm_post"][i].reshape(1, D)
        mix, gate, wo_after = wts["blocks"](i, x)
        pm, pg, h = _inproj_fwd(x, npre, mix, gate, kind == 2, f"inproj_fwd_{i}")
        if kv is None:
            wkv = wts["wkv_after"](h)
            kv = _memkv_fwd(mem, mnw, wkv)
        extra = None
        if kind == 0:
            bs3 = jnp.broadcast_to(sm["a_b_s"][j][:, :, None], (8, HD, HD))
            ymix = _gmlp_fwd(pm, sm["a_ln_w"][j].reshape(1, D), sm["a_ln_b"][j].reshape(1, D), sm["a_w_s"][j], bs3,
                             f"gmlp_fwd_{i}")
            extra = bs3
        elif kind == 1:
            ymix = _sconv_fwd(pm, sm["b_conv_w"][j], f"sconv_fwd_{i}")
        else:
            cpre = _dnconv_fwd(pm, sm["c_conv_w"][j], f"dnconv_fwd_{i}")
            alog = jnp.pad(sm["c_a_log"][j], (0, HD - 8)).reshape(1, HD)
            dtb = jnp.pad(sm["c_dt_bias"][j], (0, HD - 8)).reshape(1, HD)
            onw = sm["c_o_norm_w"][j].reshape(1, HD)
            *prep, t_mat = _dn_prep_fwd(cpre, pm, alog, dtb, f"dn_prep_fwd_{i}")
            ymix, st = _dn_scan_fwd(*prep, onw, f"dn_scan_fwd_{i}")
            extra = (cpre, prep, t_mat, st, alog, dtb, onw)
        ycat = _ag_fwd(ymix, pg, kv, f"ag_fwd_{i}")
        wo = wo_after(ycat)
        blocks.append((mix, gate, wo))
        layer_in = x
        if i < DEPTH - 1:
            o, x = _outproj_fwd(ycat, wo, x, npost, f"outproj_fwd_{i}")
        else:
            o, loss, dx = _outproj_loss(ycat, wo, x, npost, target, f"outproj_loss_{i}")
        saved.append((layer_in, h, pm, pg, ymix, ycat, o, extra))

    g = {"wm": [None] * DEPTH, "wg": [None] * DEPTH, "wo": [None] * DEPTH, "norm_pre": [None] * DEPTH,
         "norm_post": [None] * DEPTH}
    dkv = jnp.zeros((N_MEM, 2 * D_XA), F32)
    sent = 0.0
    for i in reversed(range(DEPTH)):
        kind, j = kinds[i], i // 3
        xi, h, pm, pg, ymix, ycat, o, extra = saved[i]
        npre = sm["norm_pre"][i].reshape(1, D)
        npost = sm["norm_post"][i].reshape(1, D) + sent
        dobf, g["norm_post"][i], dymix, dpg, dkv = _outproj_gate_bwd(dx, o, npost, blocks[i][2], ymix, pg, kv, dkv,
                                                                     f"outproj_gate_bwd_{i}")
        g["wo"][i] = _matmul_tn(ycat, dobf, f"dwo_{i}")
        if kind == 0:
            dpm, dlnw, dlnb, dws, dbs3 = _gmlp_bwd(dymix, pm, sm["a_ln_w"][j].reshape(1, D),
                                                   sm["a_ln_b"][j].reshape(1, D), sm["a_w_s"][j], extra,
                                                   f"gmlp_bwd_{i}")
            g.setdefault("a_ln_w", {})[j] = dlnw.reshape(D)
            g.setdefault("a_ln_b", {})[j] = dlnb.reshape(D)
            g.setdefault("a_w_s", {})[j] = dws
            g.setdefault("a_b_s", {})[j] = dbs3[:, :, 0]
        elif kind == 1:
            dpm, dcw = _sconv_bwd(dymix, pm, sm["b_conv_w"][j], f"sconv_bwd_{i}")
            g.setdefault("b_conv_w", {})[j] = dcw
        else:
            cpre, prep, t_mat, st, alog, dtb, onw = extra
            *dprep, donw = _dn_scan_bwd(dymix, *prep, st, onw, f"dn_scan_bwd_{i}")
            dcq, dck, dcv, dab, dalog, ddtb = _dn_prep_bwd(*dprep, t_mat, cpre, pm, alog, dtb, f"dn_prep_bwd_{i}")
            dpm, dcw = _dnconv_bwd(dcq, dck, dcv, dab, pm, sm["c_conv_w"][j], f"dnconv_bwd_{i}")
            g.setdefault("c_conv_w", {})[j] = dcw
            g.setdefault("c_a_log", {})[j] = dalog[0, :8]
            g.setdefault("c_dt_bias", {})[j] = ddtb[0, :8]
            g.setdefault("c_o_norm_w", {})[j] = donw[0]
        if kind == 2:
            g["wm"][i] = _matmul_tn(dpm, h, f"dwm_{i}")
            g["wg"][i] = _matmul_tn(dpg, h, f"dwg_{i}")
        else:
            g["wm"][i] = _matmul_tn(h, dpm, f"dwm_{i}", GRAD_TILE[kind])
            g["wg"][i] = _matmul_tn(h, dpg, f"dwg_{i}", GRAD_TILE[kind])
        sent = wts["layer_done"](i, g)
        dx, g["norm_pre"][i] = _inproj_bwd(dpm, dpg, xi, npre + sent, blocks[i][0], blocks[i][1], kind == 2, dx,
                                           f"inproj_bwd_{i}")
    g["mem_norm_w"], g["wkv"] = _memkv_bwd(mem, mnw, wkv, dkv)
    return loss[0, 0], dx, g


ANY = pl.BlockSpec(memory_space=pl.ANY)


def _place():
    return lax.axis_index("x"), lax.axis_index("y"), lax.axis_index("c")


def _add(a, b, name):
    def body(a_ref, b_ref, o_ref):
        o_ref[...] = a_ref[...] + b_ref[...]

    return pl.pallas_call(body, name=name, out_shape=SDS(a.shape, a.dtype), compiler_params=_params(0, VMEM_LIMIT))(a, b)


def _sum4(own, land):
    def body(own_ref, l_ref, o_ref):
        chip = 2 * lax.axis_index("x") + lax.axis_index("y")
        acc = jnp.where(chip == 0, own_ref[...], l_ref[0])
        for s in range(1, 4):
            acc = acc + jnp.where(chip == s, own_ref[...], l_ref[s])
        o_ref[...] = acc

    return pl.pallas_call(body, name="sum_small", out_shape=SDS(own.shape, own.dtype),
                          compiler_params=_params(0, VMEM_LIMIT))(own, land)


C_ROWS = 1312
_REDUCE_CHUNK = {512: 256, 2560: 640}
W_CLASSES = {"in0": (512, 256)}


def _chunk_list(specs):
    return [(k, r, chunk) for k, (half, chunk) in enumerate(specs) for r in range(0, half, chunk)]


def _gather_classes(arrs, specs, vec):
    n = len(arrs)
    chunks = _chunk_list(specs)
    nc = len(chunks)

    def body(*refs):
        ins, vec_ref, outs, ov_ref = refs[:n], refs[n], refs[n + 1:2 * n + 1], refs[2 * n + 1]
        ici_send, ici_recv, d2d_send, d2d_recv, vec_send, vec_recv = refs[2 * n + 2:]
        x, y, c = _place()
        chip = 2 * x + y
        peers = [(1 - x, y), (x, 1 - y), (1 - x, 1 - y)]

        def rows(ci, half):
            k, r, cnt = chunks[ci]
            return k, pl.ds(half * specs[k][0] + r, cnt)

        def over_ici(j, ci, slab):
            px, py = peers[j]
            k, rs = rows(ci, c)
            return pltpu.make_async_remote_copy(
                src_ref=ins[k].at[rs], dst_ref=outs[k].at[slab, rs], send_sem=ici_send.at[j * nc + ci],
                recv_sem=ici_recv.at[j * nc + ci], device_id=(px, py, c), device_id_type=MESH)

        def over_d2d(j, ci, half):
            px, py = peers[j]
            k, rs = rows(ci, half)
            where = outs[k].at[2 * px + py, rs]
            return pltpu.make_async_remote_copy(
                src_ref=where, dst_ref=where, send_sem=d2d_send.at[j * nc + ci], recv_sem=d2d_recv.at[j * nc + ci],
                device_id=(x, y, 1 - c), device_id_type=MESH)

        def small(j, slab):
            px, py = peers[j]
            return pltpu.make_async_remote_copy(
                src_ref=vec_ref, dst_ref=ov_ref.at[slab], send_sem=vec_send.at[j], recv_sem=vec_recv.at[j],
                device_id=(px, py, c), device_id_type=MESH)

        sends = [small(j, chip) for j in range(3)] + [over_ici(j, ci, chip) for ci in range(nc) for j in range(3)]
        for cp in sends:
            cp.start()
        forwards = []
        for ci in range(nc):
            for j, (px, py) in enumerate(peers):
                over_ici(j, ci, 2 * px + py).wait_recv()
                forwards.append(over_d2d(j, ci, c))
                forwards[-1].start()
        for ci in range(nc):
            for j in range(3):
                over_d2d(j, ci, 1 - c).wait_recv()
        for j, (px, py) in enumerate(peers):
            small(j, 2 * px + py).wait_recv()
        for cp in sends + forwards:
            cp.wait_send()

    dma = pltpu.SemaphoreType.DMA
    return pl.pallas_call(
        body, name="gather_weights", in_specs=[ANY] * (n + 1), out_specs=[ANY] * (n + 1),
        out_shape=[SDS((4,) + a.shape, a.dtype) for a in arrs] + [SDS((4,) + vec.shape, vec.dtype)],
        scratch_shapes=[dma((3 * nc,)), dma((3 * nc,)), dma((3 * nc,)), dma((3 * nc,)), dma((3,)), dma((3,))])(*arrs, vec)


def _swap_classes(grads, specs, small):
    n = len(grads)
    chunks = _chunk_list(specs)

    def body(*refs):
        ins, s_ref, outs, os_ref, send_sems, recv_sems = refs[:n], refs[n], refs[n + 1:2 * n + 1], *refs[2 * n + 1:]
        x, y, c = _place()
        copies = []
        for s in range(4):
            for k, r, cnt in chunks:
                copies.append(pltpu.make_async_remote_copy(
                    src_ref=ins[k].at[s, 1 - c, pl.ds(r, cnt)], dst_ref=outs[k].at[s, pl.ds(r, cnt)],
                    send_sem=send_sems.at[len(copies)], recv_sem=recv_sems.at[len(copies)],
                    device_id=(x, y, 1 - c), device_id_type=MESH))
        copies.append(pltpu.make_async_remote_copy(
            src_ref=s_ref, dst_ref=os_ref, send_sem=send_sems.at[len(copies)], recv_sem=recv_sems.at[len(copies)],
            device_id=(x, y, 1 - c), device_id_type=MESH))
        for cp in copies:
            cp.start()
        for cp in copies:
            cp.wait_recv()
        for cp in copies:
            cp.wait_send()

    m = 4 * len(chunks) + 1
    return pl.pallas_call(
        body, name="swap_halves", in_specs=[ANY] * (n + 1), out_specs=[ANY] * (n + 1),
        out_shape=[SDS((4, g.shape[2], g.shape[3]), F32) for g in grads] + [SDS(small.shape, F32)],
        scratch_shapes=[pltpu.SemaphoreType.DMA((m,)), pltpu.SemaphoreType.DMA((m,))])(*grads, small)


def _pair_sum_class(g, other, chunk, name):
    _, _, half, w = g.shape

    def body(g_ref, o_ref, pb_ref, own_ref):
        x, y, c = _place()
        v = jnp.where(c == 0, g_ref[0], g_ref[1]) + o_ref[...]
        pb_ref[...] = v.astype(BF16)

        @pl.when(pl.program_id(1) == 2 * x + y)
        def _():
            own_ref[...] = v

    return pl.pallas_call(
        body, name=name, grid=(half // chunk, 4),
        in_specs=[pl.BlockSpec((None, 2, chunk, w), lambda i, s: (s, 0, i, 0)),
                  pl.BlockSpec((None, chunk, w), lambda i, s: (s, i, 0))],
        out_specs=[pl.BlockSpec((None, chunk, w), lambda i, s: (s, i, 0)), pl.BlockSpec((chunk, w), lambda i, s: (i, 0))],
        out_shape=[SDS((4, half, w), BF16), SDS((half, w), F32)],
        compiler_params=_params(2, VMEM_LIMIT))(g, other)


def _exchange_classes(pbs, specs, ps):
    n = len(pbs)
    chunks = _chunk_list(specs)
    per_peer = len(chunks) + 1

    def body(*refs):
        ins, ps_ref, outs, ls_ref, send_sems, recv_sems = refs[:n], refs[n], refs[n + 1:2 * n + 1], *refs[2 * n + 1:]
        x, y, c = _place()
        chip = 2 * x + y
        peers = [(1 - x, y), (x, 1 - y), (1 - x, 1 - y)]

        def copies(slab_of):
            out = []
            for j, (px, py) in enumerate(peers):
                for k, r, cnt in chunks:
                    out.append(pltpu.make_async_remote_copy(
                        src_ref=ins[k].at[2 * px + py, pl.ds(r, cnt)], dst_ref=outs[k].at[slab_of(j), pl.ds(r, cnt)],
                        send_sem=send_sems.at[len(out)], recv_sem=recv_sems.at[len(out)], device_id=(px, py, c),
                        device_id_type=MESH))
                out.append(pltpu.make_async_remote_copy(
                    src_ref=ps_ref, dst_ref=ls_ref.at[slab_of(j)], send_sem=send_sems.at[len(out)],
                    recv_sem=recv_sems.at[len(out)], device_id=(px, py, c), device_id_type=MESH))
            return out

        sends = copies(lambda j: chip)
        for cp in sends:
            cp.start()
        for cp in copies(lambda j: 2 * peers[j][0] + peers[j][1]):
            cp.wait_recv()
        for cp in sends:
            cp.wait_send()

    m = 3 * per_peer
    return pl.pallas_call(
        body, name="chip_exchange", in_specs=[ANY] * (n + 1), out_specs=[ANY] * (n + 1),
        out_shape=[SDS(p.shape, BF16) for p in pbs] + [SDS((4,) + ps.shape, F32)],
        scratch_shapes=[pltpu.SemaphoreType.DMA((m,)), pltpu.SemaphoreType.DMA((m,))])(*pbs, ps)


def _chip_sum_class(own, land, chunk, name):
    half, w = own.shape

    def body(own_ref, l_ref, o_ref):
        chip = 2 * lax.axis_index("x") + lax.axis_index("y")
        acc = jnp.where(chip == 0, own_ref[...], l_ref[0].astype(F32))
        for s in range(1, 4):
            acc = acc + jnp.where(chip == s, own_ref[...], l_ref[s].astype(F32))
        o_ref[...] = acc

    return pl.pallas_call(
        body, name=name, grid=(half // chunk,),
        in_specs=[pl.BlockSpec((chunk, w), lambda i: (i, 0)), pl.BlockSpec((4, chunk, w), lambda i: (0, i, 0))],
        out_specs=pl.BlockSpec((chunk, w), lambda i: (i, 0)), out_shape=SDS((half, w), F32),
        compiler_params=_params(1, VMEM_LIMIT))(own, land)


def _share_classes(rs, specs):
    n = len(rs)
    chunks = _chunk_list(specs)

    def body(*refs):
        ins, outs, send_sems, recv_sems = refs[:n], refs[n:2 * n], *refs[2 * n:]
        x, y, c = _place()
        copies = [pltpu.make_async_remote_copy(
            src_ref=ins[k].at[pl.ds(r, cnt)], dst_ref=outs[k].at[pl.ds(r, cnt)], send_sem=send_sems.at[i],
            recv_sem=recv_sems.at[i], device_id=(x, y, 1 - c), device_id_type=MESH)
            for i, (k, r, cnt) in enumerate(chunks)]
        for cp in copies:
            cp.start()
        for cp in copies:
            cp.wait_recv()
        for cp in copies:
            cp.wait_send()

    m = len(chunks)
    return pl.pallas_call(
        body, name="share_half", in_specs=[ANY] * n, out_specs=[ANY] * n, out_shape=[SDS(r.shape, F32) for r in rs],
        scratch_shapes=[pltpu.SemaphoreType.DMA((m,)), pltpu.SemaphoreType.DMA((m,))])(*rs)


_HBM = pl.BlockSpec(memory_space=pltpu.HBM)
_SEM = pl.BlockSpec(memory_space=pltpu.SEMAPHORE)
_EFFECT = pltpu.SideEffectType.DATAFLOW_SIDE_EFFECTING


def _chip_peers():
    x, y, c = _place()
    return [(1 - x, y, c), (x, 1 - y, c), (1 - x, 1 - y, c)]


def _send_shard_start(v, name):
    def body(v_ref, land_ref, send_sems, recv_sems, v_thru, land_thru, token):
        x, y, c = _place()
        for j, peer in enumerate(_chip_peers()):
            pltpu.make_async_remote_copy(src_ref=v_ref, dst_ref=land_ref.at[2 * x + y], send_sem=send_sems.at[j],
                                         recv_sem=recv_sems.at[j], device_id=peer, device_id_type=MESH).start()
        token[...] = jnp.zeros_like(token)

    land_shape = (4,) + v.shape
    return pl.pallas_call(
        body, name=name,
        out_shape=(pltpu.SemaphoreType.DMA((3,)), pltpu.SemaphoreType.DMA((3,)), pltpu.HBM(v.shape, v.dtype),
                   pltpu.HBM(land_shape, v.dtype), SDS((8, 128), F32)),
        in_specs=(_HBM, _HBM), out_specs=(_SEM, _SEM, _HBM, _HBM, pl.BlockSpec(memory_space=pltpu.VMEM)),
        input_output_aliases={0: 2, 1: 3}, compiler_params=pltpu.CompilerParams(has_side_effects=_EFFECT),
    )(pltpu.with_memory_space_constraint(v, pltpu.HBM),
      pltpu.with_memory_space_constraint(lax.empty(land_shape, v.dtype), pltpu.HBM))


def _xor_peer(r):
    x, y, c = _place()
    return (1 - x if (r >> 2) & 1 else x, 1 - y if (r >> 1) & 1 else y, 1 - c if r & 1 else c)


def _send_pieces_start(parts, name):
    n = len(parts)

    def body(*refs):
        ins, lands = refs[:n], refs[n:2 * n]
        send_sems, recv_sems = refs[2 * n:2 * n + 2]
        token = refs[-1]
        x, y, c = _place()
        for r in range(1, 8):
            px, py, pc = _xor_peer(r)
            for k in range(n):
                pltpu.make_async_remote_copy(
                    src_ref=ins[k].at[2 * px + py, pc], dst_ref=lands[k].at[4 * x + 2 * y + c],
                    send_sem=send_sems.at[(r - 1) * n + k], recv_sem=recv_sems.at[(r - 1) * n + k],
                    device_id=(px, py, pc), device_id_type=MESH).start()
        token[...] = jnp.zeros_like(token)

    land_shapes = [(8,) + p.shape[2:] for p in parts]
    hbm = [pltpu.HBM(p.shape, p.dtype) for p in parts] + [pltpu.HBM(s, p.dtype) for s, p in zip(land_shapes, parts)]
    operands = [pltpu.with_memory_space_constraint(p, pltpu.HBM) for p in parts]
    operands += [pltpu.with_memory_space_constraint(lax.empty(s, p.dtype), pltpu.HBM) for s, p in zip(land_shapes, parts)]
    return pl.pallas_call(
        body, name=name,
        out_shape=(pltpu.SemaphoreType.DMA((7 * n,)), pltpu.SemaphoreType.DMA((7 * n,)), *hbm, SDS((8, 128), F32)),
        in_specs=(_HBM,) * (2 * n), out_specs=(_SEM, _SEM) + (_HBM,) * (2 * n) + (pl.BlockSpec(memory_space=pltpu.VMEM),),
        input_output_aliases={i: 2 + i for i in range(2 * n)},
        compiler_params=pltpu.CompilerParams(has_side_effects=_EFFECT))(*operands)


def _send_pieces_wait(started, after, name):
    send_sems, recv_sems, *thru, _ = started
    n = len(thru) // 2

    def body(*refs):
        ins, lands = refs[:n], refs[n:2 * n]
        send_sems, recv_sems = refs[2 * n:2 * n + 2]
        for r in range(1, 8):
            px, py, pc = _xor_peer(r)
            for k in range(n):
                copy = pltpu.make_async_remote_copy(
                    src_ref=ins[k].at[2 * px + py, pc], dst_ref=lands[k].at[4 * px + 2 * py + pc],
                    send_sem=send_sems.at[(r - 1) * n + k], recv_sem=recv_sems.at[(r - 1) * n + k],
                    device_id=(px, py, pc), device_id_type=MESH)
                copy.wait_send()
                copy.wait_recv()

    return pl.pallas_call(
        body, name=name, out_shape=tuple(pltpu.HBM(t.shape, t.dtype) for t in thru),
        in_specs=(_HBM,) * (2 * n) + (_SEM, _SEM, pl.BlockSpec(memory_space=pl.ANY)), out_specs=(_HBM,) * (2 * n),
        input_output_aliases={i: i for i in range(2 * n)},
        compiler_params=pltpu.CompilerParams(has_side_effects=_EFFECT))(*thru, send_sems, recv_sems, after)[n:]


def _sum8_class(own, land, chunk, name):
    rows, w = own.shape

    def body(own_ref, l_ref, o_ref):
        x, y, c = _place()
        me = 4 * x + 2 * y + c
        acc = jnp.where(me == 0, own_ref[...], l_ref[0].astype(F32))
        for d in range(1, 8):
            acc = acc + jnp.where(me == d, own_ref[...], l_ref[d].astype(F32))
        o_ref[...] = acc

    return pl.pallas_call(
        body, name=name, grid=(rows // chunk,),
        in_specs=[pl.BlockSpec((chunk, w), lambda i: (i, 0)), pl.BlockSpec((8, chunk, w), lambda i: (0, i, 0))],
        out_specs=pl.BlockSpec((chunk, w), lambda i: (i, 0)), out_shape=SDS((rows, w), F32),
        compiler_params=_params(1, VMEM_LIMIT))(own, land)


def _send_shard_wait(send_sems, recv_sems, v_thru, land_thru, after, name):
    def body(v_ref, land_ref, send_sems, recv_sems, after_ref, v_dead, got_ref):
        for j, (px, py, pc) in enumerate(_chip_peers()):
            copy = pltpu.make_async_remote_copy(src_ref=v_ref, dst_ref=land_ref.at[2 * px + py], send_sem=send_sems.at[j],
                                                recv_sem=recv_sems.at[j], device_id=(px, py, pc), device_id_type=MESH)
            copy.wait_send()
            copy.wait_recv()

    return pl.pallas_call(
        body, name=name,
        out_shape=(pltpu.HBM(v_thru.shape, v_thru.dtype), pltpu.HBM(land_thru.shape, land_thru.dtype)),
        in_specs=(_HBM, _HBM, _SEM, _SEM, pl.BlockSpec(memory_space=pl.ANY)), out_specs=(_HBM, _HBM),
        input_output_aliases={0: 0, 1: 1}, compiler_params=pltpu.CompilerParams(has_side_effects=_EFFECT),
    )(v_thru, land_thru, send_sems, recv_sems, after)[1]


_SMALL = ["mem_norm_w", "norm_pre", "norm_post", "a_ln_w", "a_ln_b", "a_w_s", "a_b_s", "b_conv_w", "c_conv_w",
          "c_a_log", "c_dt_bias", "c_o_norm_w"]
_SMALL_SHAPES = {"mem_norm_w": (D,), "norm_pre": (4, D), "norm_post": (4, D), "a_ln_w": (2, D), "a_ln_b": (2, D),
                 "a_w_s": (2, 8, HD, HD), "a_b_s": (2, 8, HD), "b_conv_w": (1, 3, D), "c_conv_w": (1, 4, 3 * D),
                 "c_a_log": (1, 8), "c_dt_bias": (1, 8), "c_o_norm_w": (1, HD)}
_SHARDED_SMALL = {"a_ln_w": D // 4, "a_ln_b": D // 4, "b_conv_w": D // 4, "c_conv_w": 3 * D // 4}
_SMALL_ROWS = 288


def _size(shape):
    n = 1
    for d in shape:
        n *= d
    return n


def kernel(x, mem, mem_norm_w, w_mem_kv, norm_pre, norm_post, w_out, a_w_in, a_ln_w, a_ln_b, a_w_s, a_b_s, b_w_in, b_conv_w, c_w_in, c_conv_w, c_a_log, c_dt_bias, c_o_norm_w, loss_target, m_mem_norm_w, m_w_mem_kv, m_norm_pre, m_norm_post, m_w_out, m_a_w_in, m_a_ln_w, m_a_ln_b, m_a_w_s, m_a_b_s, m_b_w_in, m_b_conv_w, m_c_w_in, m_c_conv_w, m_c_a_log, m_c_dt_bias, m_c_o_norm_w, v_mem_norm_w, v_w_mem_kv, v_norm_pre, v_norm_post, v_w_out, v_a_w_in, v_a_ln_w, v_a_ln_b, v_a_w_s, v_a_b_s, v_b_w_in, v_b_conv_w, v_c_w_in, v_c_conv_w, v_c_a_log, v_c_dt_bias, v_c_o_norm_w):
    names = ["mem_norm_w", "w_mem_kv", "norm_pre", "norm_post", "w_out", "a_w_in", "a_ln_w", "a_ln_b", "a_w_s", "a_b_s",
             "b_w_in", "b_conv_w", "c_w_in", "c_conv_w", "c_a_log", "c_dt_bias", "c_o_norm_w"]
    w = dict(zip(names, [mem_norm_w, w_mem_kv, norm_pre, norm_post, w_out, a_w_in, a_ln_w, a_ln_b, a_w_s, a_b_s, b_w_in,
                         b_conv_w, c_w_in, c_conv_w, c_a_log, c_dt_bias, c_o_norm_w]))
    m = dict(zip(names, [m_mem_norm_w, m_w_mem_kv, m_norm_pre, m_norm_post, m_w_out, m_a_w_in, m_a_ln_w, m_a_ln_b, m_a_w_s,
                         m_a_b_s, m_b_w_in, m_b_conv_w, m_c_w_in, m_c_conv_w, m_c_a_log, m_c_dt_bias, m_c_o_norm_w]))
    v = dict(zip(names, [v_mem_norm_w, v_w_mem_kv, v_norm_pre, v_norm_post, v_w_out, v_a_w_in, v_a_ln_w, v_a_ln_b, v_a_w_s,
                         v_a_b_s, v_b_w_in, v_b_conv_w, v_c_w_in, v_c_conv_w, v_c_a_log, v_c_dt_bias, v_c_o_norm_w]))
    chip = 2 * lax.axis_index("x") + lax.axis_index("y")

    def rows_of_ct(a):
        return a[0].T

    def with_mine(gathered, own):
        return lax.dynamic_update_slice(gathered, own[None], (chip,) + (0,) * own.ndim)

    first = [a_w_in[0].astype(BF16)]
    vec = jnp.concatenate([a_ln_w.reshape(-1), a_ln_b.reshape(-1), b_conv_w.reshape(-1), c_conv_w.reshape(-1)])
    vec = jnp.pad(vec, (0, 8 * D - vec.shape[0])).reshape(8, D)
    ga0, gvec = _gather_classes(first, [W_CLASSES[k] for k in W_CLASSES], vec)
    ga0, gvec = with_mine(ga0, first[0]), with_mine(gvec, vec)
    gv = gvec.reshape(4, 8 * D)
    w_in = {1: b_w_in[0], 2: jnp.pad(rows_of_ct(c_w_in), ((0, C_ROWS - 1284), (0, 0))), 3: a_w_in[1]}
    order = [("kv", 0, w_mem_kv), ("out", 0, w_out[0])]
    for i in range(1, DEPTH):
        order += [("in", i, w_in[i]), ("out", i, w_out[i])]
    later, sent, started = {}, {}, 0.0 * ga0[0, 0, 0].astype(F32)
    for kind, i, a in order:
        later[kind, i] = (a + started).astype(BF16)
        sent[kind, i] = _send_shard_start(later[kind, i], f"send_w_{kind}_{i}")
        started = sent[kind, i][4][0, 0]

    def arrived(k, after):
        return with_mine(_send_shard_wait(*sent[k][:4], after, f"wait_w_{k[0]}_{k[1]}"), later[k])

    def blocks(i, after):
        def wo_after(later_value):
            return arrived(("out", i), later_value).reshape(D_CAT, D)

        if i == 0:
            return [ga0[0], ga0[1]], [ga0[2], ga0[3]], wo_after
        got = arrived(("in", i), after)
        if i == 1:
            return [got[0], got[1], got[2][:, :512]], [got[2][:, 512:], got[3]], wo_after
        if i == 3:
            return [got[0], got[1]], [got[2], got[3]], wo_after
        fct = got[:, :1284].reshape(5136, D)
        c_ab = jnp.concatenate([fct[3 * D:3 * D + 16], jnp.zeros((AB_PAD - 16, D), BF16)], axis=0)
        return [fct[:3 * D], c_ab], [fct[3 * D + 16:]], wo_after
    sm = {"mem_norm_w": mem_norm_w, "norm_pre": norm_pre + started, "norm_post": norm_post, "a_w_s": a_w_s, "a_b_s": a_b_s,
          "c_a_log": c_a_log, "c_dt_bias": c_dt_bias, "c_o_norm_w": c_o_norm_w,
          "a_ln_w": gv[:, 0:512].reshape(4, 2, 256).transpose(1, 0, 2).reshape(2, D),
          "a_ln_b": gv[:, 512:1024].reshape(4, 2, 256).transpose(1, 0, 2).reshape(2, D),
          "b_conv_w": gv[:, 1024:1792].reshape(4, 1, 3, 256).transpose(1, 2, 0, 3).reshape(1, 3, D),
          "c_conv_w": gv[:, 1792:4864].reshape(4, 1, 4, 768).transpose(1, 2, 0, 3).reshape(1, 4, 3 * D)}
    wts = {"wkv_after": lambda after: arrived(("kv", 0), after).reshape(D, 2 * D_XA), "blocks": blocks}

    def layer_grads(i, g):
        if i % 3 == 2:
            gct = jnp.concatenate([g["wm"][i][:3 * D + 16], g["wg"][i]], axis=0).reshape(4, 1284, D)
            w_in = jnp.pad(gct, ((0, 0), (0, C_ROWS - 1284), (0, 0)))
        else:
            w_in = jnp.concatenate([g["wm"][i], g["wg"][i]], axis=0).reshape(4, -1, GRAD_TILE[i % 3])
        out = {f"in{i}": w_in, f"out{i}": g["wo"][i].reshape(4, 384, D)}
        return {k: a.reshape(4, 2, a.shape[1] // 2, a.shape[2]) for k, a in out.items()}

    pending = {}

    def layer_done(i, g):
        halves = layer_grads(i, g)
        started = _send_pieces_start([h.astype(BF16) for h in halves.values()], f"send_grads_{i}")
        pending[i] = (started, halves)
        return started[-1][0, 0]

    wts["layer_done"] = layer_done

    loss, dx, g = _local_step(x[0], mem[0], loss_target[0], wts, sm)
    loss = lax.psum(loss, ("x", "y", "c"))

    core = lax.axis_index("c")
    mine, specs = {}, {}
    for i in reversed(range(DEPTH)):
        started, halves = pending[i]
        lands = _send_pieces_wait(started, dx, f"wait_grads_{i}")
        for (k, h), land in zip(halves.items(), lands):
            own = lax.dynamic_index_in_dim(lax.dynamic_index_in_dim(h, chip, 0, False), core, 0, False)
            specs[k] = (own.shape[0], _REDUCE_CHUNK.get(own.shape[0], own.shape[0]))
            mine[k] = _sum8_class(own, land, specs[k][1], f"sum8_{k}")

    first = {"kv": g["wkv"].reshape(4, 2, 128, D)}
    first_specs = [(h.shape[2], _REDUCE_CHUNK.get(h.shape[2], h.shape[2])) for h in first.values()]
    halves = list(first.values())
    gs = {"mem_norm_w": g["mem_norm_w"], "norm_pre": jnp.concatenate(g["norm_pre"]),
          "norm_post": jnp.concatenate(g["norm_post"])}
    for n in _SMALL[3:]:
        gs[n] = jnp.stack([g[n][j] for j in sorted(g[n])])
    flat = jnp.concatenate([gs[n].reshape(-1) for n in _SMALL])
    small = jnp.pad(flat, (0, _SMALL_ROWS * D - flat.shape[0])).reshape(_SMALL_ROWS, D)
    *others, other_small = _swap_classes(halves, first_specs, small)
    pairs = [_pair_sum_class(h, o, s[1], f"pair_sum_{k}") for k, h, o, s in zip(first, halves, others, first_specs)]
    pair_small = _add(small, other_small, "pair_sum_small")
    *lands, land_small = _exchange_classes([p[0] for p in pairs], first_specs, pair_small)
    for k, p, land, s in zip(first, pairs, lands, first_specs):
        mine[k], specs[k] = _chip_sum_class(p[1], land, s[1], f"chip_sum_{k}"), s
    theirs = _share_classes(list(mine.values()), [specs[k] for k in mine])
    south = core == 0
    sh = {k: jnp.concatenate([jnp.where(south, a, b), jnp.where(south, b, a)], axis=0)
          for (k, a), b in zip(mine.items(), theirs)}
    grads = {"a_w_in": jnp.stack([sh["in0"], sh["in3"]]),
             "b_w_in": sh["in1"].reshape(5, D, 256).transpose(1, 0, 2).reshape(b_w_in.shape),
             "c_w_in": sh["in2"][:1284], "w_out": jnp.stack([sh[f"out{i}"] for i in range(DEPTH)]),
             "w_mem_kv": sh["kv"]}
    flat = _sum4(pair_small, land_small).reshape(-1)
    off = 0
    for n in _SMALL:
        shape = _SMALL_SHAPES[n]
        full = flat[off:off + _size(shape)].reshape(shape)
        off += _size(shape)
        if n in _SHARDED_SMALL:
            full = lax.dynamic_slice_in_dim(full, chip * _SHARDED_SMALL[n], _SHARDED_SMALL[n], axis=len(shape) - 1)
        grads[n] = full

    delta, new_m, new_v = {}, {}, {}
    for n in names:
        shape = w[n].shape
        if n == "c_w_in":
            d_, m_, v_ = _adamw(rows_of_ct(w[n]), grads[n], rows_of_ct(m[n]), rows_of_ct(v[n]), f"adamw_{n}")
            delta[n], new_m[n], new_v[n], grads[n] = d_.T[None], m_.T[None], v_.T[None], grads[n].T[None]
            continue
        view = (1, shape[0]) if len(shape) == 1 else (_size(shape[:-1]), shape[-1])
        d_, m_, v_ = _adamw(w[n].reshape(view), grads[n].reshape(view), m[n].reshape(view), v[n].reshape(view),
                            f"adamw_{n}")
        delta[n], new_m[n], new_v[n] = d_.reshape(shape), m_.reshape(shape), v_.reshape(shape)
    return (loss, dx[None], *[grads[n].reshape(w[n].shape) for n in names], *[delta[n] for n in names],
            *[new_m[n] for n in names], *[new_v[n] for n in names])
```

```python
import functools

import jax
import jax.numpy as jnp
from jax import lax
from jax.experimental import pallas as pl
from jax.experimental.pallas import tpu as pltpu

F32 = jnp.float32
BF16 = jnp.bfloat16
HI = lax.Precision.HIGHEST
MESH = pl.DeviceIdType.MESH
SDS = jax.ShapeDtypeStruct

D = 1024
D_XA = 512
D_CAT = 1536
N_MEM = 256
HD = 128
DEPTH = 4
EPS = 1e-6
TT = 512
DN_C = 64
DN_TB = 256
HALO = 8
AB_PAD = 128
VMEM_LIMIT = 56 * 1024 * 1024
GRAD_TILE = {0: 1024, 1: 256}

ADAM_LR, ADAM_B1, ADAM_B2, ADAM_EPS, ADAM_WD, ADAM_STEP = 0.001, 0.9, 0.999, 1e-08, 0.01, 10


def _params(n_grid, vmem=None):
    return pltpu.CompilerParams(dimension_semantics=("arbitrary",) * n_grid, vmem_limit_bytes=vmem)


def _rms(x, w):
    return x * lax.rsqrt(jnp.mean(x * x, axis=-1, keepdims=True) + EPS) * w


def _dot_nn(a, b):
    return jnp.dot(a.astype(BF16), b.astype(BF16), preferred_element_type=F32)


def _dot_nt(a, b):
    return lax.dot_general(a.astype(BF16), b.astype(BF16), (((1,), (1,)), ((), ())), preferred_element_type=F32)


def _dot_tn(a, b):
    return lax.dot_general(a.astype(BF16), b.astype(BF16), (((0,), (0,)), ((), ())), preferred_element_type=F32)


@jax.custom_vjp
def mm(a, b):
    return _dot_nn(a, b)


mm.defvjp(lambda a, b: (_dot_nn(a, b), (a, b)), lambda r, g: (_dot_nt(g, r[1]), _dot_tn(r[0], g)))


@jax.custom_vjp
def mm_nt(a, b):
    return _dot_nt(a, b)


mm_nt.defvjp(lambda a, b: (_dot_nt(a, b), (a, b)), lambda r, g: (_dot_nn(g, r[1]), _dot_tn(g, r[0])))


def _row_spec(width, tile=TT):
    return pl.BlockSpec((tile, width), lambda i: (i, 0))


def _full_spec(shape):
    return pl.BlockSpec(shape, lambda *_: (0,) * len(shape))


def _widths(blocks, transposed):
    return [b.shape[0 if transposed else 1] for b in blocks]


def _inproj_fwd(x, nw, mix, gate, transposed, name):
    T, nm = x.shape[0], len(mix)
    M, G = sum(_widths(mix, transposed)), sum(_widths(gate, transposed))

    def body(x_ref, nw_ref, *refs):
        blocks, (pm_ref, pg_ref, h_ref) = refs[:-3], refs[-3:]
        h = _rms(x_ref[...], nw_ref[...]).astype(BF16)
        h_ref[...] = h
        for p_ref, group in ((pm_ref, blocks[:nm]), (pg_ref, blocks[nm:])):
            off = 0
            for w_ref in group:
                w = w_ref.shape[0 if transposed else 1]
                p_ref[:, off:off + w] = _dot_nt(h, w_ref[...]) if transposed else _dot_nn(h, w_ref[...])
                off += w

    return pl.pallas_call(
        body, name=name, grid=(T // TT,),
        in_specs=[_row_spec(D), _full_spec((1, D))] + [_full_spec(b.shape) for b in mix + gate],
        out_specs=[_row_spec(M), _row_spec(G), _row_spec(D)],
        out_shape=[SDS((T, M), F32), SDS((T, G), F32), SDS((T, D), BF16)],
        compiler_params=_params(1, VMEM_LIMIT))(x, nw, *mix, *gate)


def _inproj_bwd(dpm, dpg, x, nw, mix, gate, transposed, dxc, name):
    T, nm = x.shape[0], len(mix)
    M, G = sum(_widths(mix, transposed)), sum(_widths(gate, transposed))

    def body(dpm_ref, dpg_ref, x_ref, nw_ref, *refs):
        blocks, (dxc_ref, dx_ref, dnw_ref) = refs[:-3], refs[-3:]
        dh = None
        for dp_ref, group in ((dpm_ref, blocks[:nm]), (dpg_ref, blocks[nm:])):
            off = 0
            for w_ref in group:
                w = w_ref.shape[0 if transposed else 1]
                dp = dp_ref[:, off:off + w]
                part = _dot_nn(dp, w_ref[...]) if transposed else _dot_nt(dp, w_ref[...])
                dh = part if dh is None else dh + part
                off += w
        _, vjp = jax.vjp(_rms, x_ref[...], nw_ref[...])
        dxr, dnw = vjp(dh)
        dx_ref[...] = dxc_ref[...] + dxr

        @pl.when(pl.program_id(0) == 0)
        def _():
            dnw_ref[...] = jnp.zeros_like(dnw_ref)
        dnw_ref[...] += dnw

    return pl.pallas_call(
        body, name=name, grid=(T // TT,),
        in_specs=[_row_spec(M), _row_spec(G), _row_spec(D), _full_spec((1, D))]
        + [_full_spec(b.shape) for b in mix + gate] + [_row_spec(D)],
        out_specs=[_row_spec(D), _full_spec((1, D))],
        out_shape=[SDS((T, D), F32), SDS((1, D), F32)],
        compiler_params=_params(1, VMEM_LIMIT))(dpm, dpg, x, nw, *mix, *gate, dxc)


def _matmul_tn(a, b, name, sub=None):
    T, K = a.shape
    N = b.shape[1]
    tn = 1024 if N % 1024 == 0 else (640 if N % 640 == 0 else N)
    tt = min(1024, T)
    n_sub = 1 if sub is None else tn // sub

    def body(a_ref, b_ref, o_ref):
        @pl.when(pl.program_id(1) == 0)
        def _():
            o_ref[...] = jnp.zeros_like(o_ref)
        res = _dot_tn(a_ref[...], b_ref[...])
        if sub is None:
            o_ref[...] += res
        else:
            for i in range(n_sub):
                o_ref[i] += res[:, i * sub:(i + 1) * sub]

    if sub is None:
        out_spec, out_shape = pl.BlockSpec((K, tn), lambda j, t: (0, j)), SDS((K, N), F32)
    else:
        out_spec, out_shape = pl.BlockSpec((n_sub, K, sub), lambda j, t: (j, 0, 0)), SDS((N // sub, K, sub), F32)
    return pl.pallas_call(
        body, name=name, grid=(N // tn, T // tt),
        in_specs=[pl.BlockSpec((tt, K), lambda j, t: (t, 0)), pl.BlockSpec((tt, tn), lambda j, t: (t, j))],
        out_specs=out_spec, out_shape=out_shape,
        compiler_params=_params(2, VMEM_LIMIT))(a, b)


def _memkv_fn(mem, w, wkv):
    return mm(_rms(mem, w), wkv)


def _memkv_fwd(mem, w, wkv):
    def body(mem_ref, w_ref, wkv_ref, kv_ref):
        kv_ref[...] = _memkv_fn(mem_ref[...], w_ref[...], wkv_ref[...])

    return pl.pallas_call(body, name="memkv_fwd", out_shape=SDS((N_MEM, 2 * D_XA), F32),
                          compiler_params=_params(0, VMEM_LIMIT))(mem, w, wkv)


def _memkv_bwd(mem, w, wkv, dkv):
    def body(mem_ref, w_ref, wkv_ref, dkv_ref, dw_ref, dwkv_ref):
        _, vjp = jax.vjp(functools.partial(_memkv_fn, mem_ref[...]), w_ref[...], wkv_ref[...].astype(F32))
        dw, dwkv = vjp(dkv_ref[...])
        dw_ref[...] = dw
        dwkv_ref[...] = dwkv

    return pl.pallas_call(body, name="memkv_bwd", out_shape=[SDS((1, D), F32), SDS((D, 2 * D_XA), F32)],
                          compiler_params=_params(0, VMEM_LIMIT))(mem, w, wkv, dkv)


def _attn_gate(ymix, qx, z, *kvs):
    outs = []
    for j in range(4):
        s = mm_nt(qx[:, j * HD:(j + 1) * HD], kvs[j]) * (HD ** -0.5)
        e = jnp.exp(s - lax.stop_gradient(jnp.max(s, axis=-1, keepdims=True)))
        outs.append(mm(e / jnp.sum(e, axis=-1, keepdims=True), kvs[4 + j]))
    return jnp.concatenate([ymix] + outs, axis=1) * jax.nn.silu(z)


def _kv_blocks(kv_ref):
    return [kv_ref[:, j * HD:(j + 1) * HD] for j in range(8)]


def _gate_outproj_fwd(ymix, pg, kv, wo, x, nw, target, name):
    T = x.shape[0]
    last = target is not None

    def body(ymix_ref, pg_ref, kv_ref, wo_ref, x_ref, nw_ref, *refs):
        ycat = _attn_gate(ymix_ref[...], pg_ref[:, :D_XA], pg_ref[:, D_XA:], *_kv_blocks(kv_ref)).astype(BF16)
        o = jnp.dot(ycat, wo_ref[...], preferred_element_type=F32)
        y = x_ref[...] + _rms(o, nw_ref[...])
        if not last:
            ycat_ref, o_ref, y_ref = refs
            y_ref[...] = y
        else:
            t_ref, ycat_ref, o_ref, loss_ref, dy_ref = refs
            err = y - t_ref[...]
            dy_ref[...] = err * (1.0 / D)

            @pl.when(pl.program_id(0) == 0)
            def _():
                loss_ref[...] = jnp.zeros_like(loss_ref)
            part = jnp.sum(jnp.sum(err * err, axis=1, keepdims=True), axis=0, keepdims=True) * (0.5 / D)
            loss_ref[...] += jnp.broadcast_to(part, loss_ref.shape)
        ycat_ref[...] = ycat
        o_ref[...] = o

    in_specs = [_row_spec(D), _row_spec(D_XA + D_CAT), _full_spec((N_MEM, 2 * D_XA)), _full_spec((D_CAT, D)),
                _row_spec(D), _full_spec((1, D))]
    out_specs = [_row_spec(D_CAT), _row_spec(D)]
    out_shape = [SDS((T, D_CAT), BF16), SDS((T, D), F32)]
    if last:
        in_specs.append(_row_spec(D))
        out_specs += [_full_spec((8, 128)), _row_spec(D)]
        out_shape += [SDS((8, 128), F32), SDS((T, D), F32)]
    else:
        out_specs.append(_row_spec(D))
        out_shape.append(SDS((T, D), F32))
    args = (ymix, pg, kv, wo, x, nw) + ((target,) if last else ())
    return pl.pallas_call(body, name=name, grid=(T // TT,), in_specs=in_specs, out_specs=out_specs,
                          out_shape=out_shape, compiler_params=_params(1, VMEM_LIMIT))(*args)


def _outproj_gate_bwd(dxo, o, nw, wo, ymix, pg, kv, dkv_in, name):
    T = dxo.shape[0]
    G = D_XA + D_CAT

    def body(dxo_ref, o_ref, nw_ref, wo_ref, ymix_ref, pg_ref, kv_ref, dkvin_ref,
             dobf_ref, dnw_ref, dymix_ref, dpg_ref, dkv_ref):
        _, vjp = jax.vjp(_rms, o_ref[...], nw_ref[...])
        do, dnw = vjp(dxo_ref[...])
        dobf = do.astype(BF16)
        dobf_ref[...] = dobf
        dycat = _dot_nt(dobf, wo_ref[...])
        _, vjp = jax.vjp(_attn_gate, ymix_ref[...], pg_ref[:, :D_XA], pg_ref[:, D_XA:], *_kv_blocks(kv_ref))
        g = vjp(dycat)
        dymix_ref[...] = g[0]
        dpg_ref[:, :D_XA] = g[1].astype(BF16)
        dpg_ref[:, D_XA:] = g[2].astype(BF16)

        @pl.when(pl.program_id(0) == 0)
        def _():
            dnw_ref[...] = jnp.zeros_like(dnw_ref)
            dkv_ref[...] = dkvin_ref[...]
        dnw_ref[...] += dnw
        for j in range(8):
            dkv_ref[:, j * HD:(j + 1) * HD] += g[3 + j]

    return pl.pallas_call(
        body, name=name, grid=(T // TT,),
        in_specs=[_row_spec(D), _row_spec(D), _full_spec((1, D)), _full_spec((D_CAT, D)), _row_spec(D), _row_spec(G),
                  _full_spec((N_MEM, 2 * D_XA)), _full_spec((N_MEM, 2 * D_XA))],
        out_specs=[_row_spec(D), _full_spec((1, D)), _row_spec(D), _row_spec(G), _full_spec((N_MEM, 2 * D_XA))],
        out_shape=[SDS((T, D), BF16), SDS((1, D), F32), SDS((T, D), F32), SDS((T, G), BF16),
                   SDS((N_MEM, 2 * D_XA), F32)],
        compiler_params=_params(1, VMEM_LIMIT))(dxo, o, nw, wo, ymix, pg, kv, dkv_in)


def _gmlp_pre(u, v, lnw, lnb):
    vg = jax.nn.gelu(v)
    xc = vg - jnp.mean(vg, axis=-1, keepdims=True)
    vl = xc * lax.rsqrt(jnp.mean(xc * xc, axis=-1, keepdims=True) + EPS) * lnw + lnb
    return jax.nn.gelu(u), vl


def _tril(n, strict=False):
    r = lax.broadcasted_iota(jnp.int32, (n, n), 0)
    c = lax.broadcasted_iota(jnp.int32, (n, n), 1)
    return (r > c) if strict else (r >= c)


def _gmlp_fwd(pm, lnw, lnb, ws, bs3, name):
    T = pm.shape[0]

    def body(pm_ref, lnw_ref, lnb_ref, ws_ref, bs_ref, y_ref):
        ug, vl = _gmlp_pre(pm_ref[:, :D], pm_ref[:, D:], lnw_ref[...], lnb_ref[...])
        mask = _tril(HD)
        for g in range(8):
            w = jnp.where(mask, ws_ref[g], 0.0)
            for c in range(TT // HD):
                rows, cols = slice(c * HD, (c + 1) * HD), slice(g * HD, (g + 1) * HD)
                y_ref[rows, cols] = ug[rows, cols] * (_dot_nn(w, vl[rows, cols]) + bs_ref[g])

    return pl.pallas_call(
        body, name=name, grid=(T // TT,),
        in_specs=[_row_spec(2 * D), _full_spec((1, D)), _full_spec((1, D)), _full_spec((8, HD, HD)),
                  _full_spec((8, HD, HD))],
        out_specs=_row_spec(D), out_shape=SDS((T, D), F32),
        compiler_params=_params(1, VMEM_LIMIT))(pm, lnw, lnb, ws, bs3)


def _gmlp_bwd(dy, pm, lnw, lnb, ws, bs3, name):
    T = pm.shape[0]
    n_t = T // TT

    def body(dy_ref, pm_ref, lnw_ref, lnb_ref, ws_ref, bs_ref, dpm_ref, dlnw_ref, dlnb_ref, dws_ref, dbs_ref,
             dug_scr, dvl_scr, dbs_scr):
        i = pl.program_id(0)

        @pl.when(i == 0)
        def _():
            dlnw_ref[...] = jnp.zeros_like(dlnw_ref)
            dlnb_ref[...] = jnp.zeros_like(dlnb_ref)
            dws_ref[...] = jnp.zeros_like(dws_ref)
            dbs_scr[...] = jnp.zeros_like(dbs_scr)

        (ug, vl), vjp = jax.vjp(_gmlp_pre, pm_ref[:, :D], pm_ref[:, D:], lnw_ref[...], lnb_ref[...])
        mask = _tril(HD)
        for g in range(8):
            w = jnp.where(mask, ws_ref[g], 0.0)
            dw = jnp.zeros((HD, HD), F32)
            db = jnp.zeros((HD, HD), F32)
            for c in range(TT // HD):
                rows, cols = slice(c * HD, (c + 1) * HD), slice(g * HD, (g + 1) * HD)
                dyb, vlb = dy_ref[rows, cols], vl[rows, cols]
                sp = _dot_nn(w, vlb) + bs_ref[g]
                dsp = dyb * ug[rows, cols]
                dug_scr[rows, cols] = dyb * sp
                dvl_scr[rows, cols] = _dot_tn(w, dsp)
                dw += _dot_nt(dsp, vlb)
                db += dsp
            dws_ref[g] += jnp.where(mask, dw, 0.0)
            dbs_scr[g] += db
        du, dv, dlnw, dlnb = vjp((dug_scr[...], dvl_scr[...]))
        dpm_ref[:, :D] = du.astype(BF16)
        dpm_ref[:, D:] = dv.astype(BF16)
        dlnw_ref[...] += dlnw
        dlnb_ref[...] += dlnb

        @pl.when(i == n_t - 1)
        def _():
            for g in range(8):
                dbs_ref[g] = jnp.broadcast_to(jnp.sum(dbs_scr[g], axis=1, keepdims=True), (HD, HD))

    return pl.pallas_call(
        body, name=name, grid=(n_t,),
        in_specs=[_row_spec(D), _row_spec(2 * D), _full_spec((1, D)), _full_spec((1, D)), _full_spec((8, HD, HD)),
                  _full_spec((8, HD, HD))],
        out_specs=[_row_spec(2 * D), _full_spec((1, D)), _full_spec((1, D)), _full_spec((8, HD, HD)),
                   _full_spec((8, HD, HD))],
        out_shape=[SDS((T, 2 * D), BF16), SDS((1, D), F32), SDS((1, D), F32), SDS((8, HD, HD), F32),
                   SDS((8, HD, HD), F32)],
        scratch_shapes=[pltpu.VMEM((TT, D), F32), pltpu.VMEM((TT, D), F32), pltpu.VMEM((8, HD, HD), F32)],
        compiler_params=_params(1, VMEM_LIMIT))(dy, pm, lnw, lnb, ws, bs3)


def _prev_spec(width, T):
    return pl.BlockSpec((HALO, width), lambda i: (jnp.maximum(i * (TT // HALO) - 1, 0), 0))


def _next_spec(width, T):
    return pl.BlockSpec((HALO, width), lambda i: (jnp.minimum((i + 1) * (TT // HALO), T // HALO - 1), 0))


def _rows_before(ext, j):
    return ext[HALO:] if j == 0 else pltpu.roll(ext, j, 0)[HALO:]


def _rows_after(ext, j):
    n = ext.shape[0]
    return ext[:n - HALO] if j == 0 else pltpu.roll(ext, n - j, 0)[:n - HALO]


def _conv_apply(ext_s, w):
    K = w.shape[0]
    y = _rows_before(ext_s, K - 1) * w[0:1]
    for k in range(1, K):
        y = y + _rows_before(ext_s, K - 1 - k) * w[k:k + 1]
    return y


def _conv_grads(ext_s, ext_dy, w):
    K = w.shape[0]
    dy = ext_dy[:ext_dy.shape[0] - HALO]
    ds = _rows_after(ext_dy, K - 1) * w[0:1]
    dws = [jnp.sum(dy * _rows_before(ext_s, K - 1), axis=0, keepdims=True)]
    for k in range(1, K):
        ds = ds + _rows_after(ext_dy, K - 1 - k) * w[k:k + 1]
        dws.append(jnp.sum(dy * _rows_before(ext_s, K - 1 - k), axis=0, keepdims=True))
    return ds, jnp.concatenate(dws, axis=0)


def _sconv_fwd(pm, w, name):
    T = pm.shape[0]

    def body(pm_ref, prev_ref, w_ref, y_ref):
        s = pm_ref[:, D:2 * D] * pm_ref[:, 2 * D:]
        sp = jnp.where(pl.program_id(0) > 0, prev_ref[:, D:2 * D] * prev_ref[:, 2 * D:], 0.0)
        y_ref[...] = pm_ref[:, :D] * _conv_apply(jnp.concatenate([sp, s], axis=0), w_ref[...])

    return pl.pallas_call(
        body, name=name, grid=(T // TT,),
        in_specs=[_row_spec(3 * D), _prev_spec(3 * D, T), _full_spec((3, D))],
        out_specs=_row_spec(D), out_shape=SDS((T, D), F32),
        compiler_params=_params(1, VMEM_LIMIT))(pm, pm, w)


def _sconv_bwd(dy, pm, w, name):
    T = pm.shape[0]
    n_t = T // TT

    def body(dy_ref, dyn_ref, pm_ref, prev_ref, next_ref, w_ref, dpm_ref, dw_ref):
        i = pl.program_id(0)
        bg, cg, hv = pm_ref[:, :D], pm_ref[:, D:2 * D], pm_ref[:, 2 * D:]
        sp = jnp.where(i > 0, prev_ref[:, D:2 * D] * prev_ref[:, 2 * D:], 0.0)
        ext_s = jnp.concatenate([sp, cg * hv], axis=0)
        dyv = dy_ref[...]
        dcn = jnp.where(i < n_t - 1, dyn_ref[...] * next_ref[:, :D], 0.0)
        ds, dw = _conv_grads(ext_s, jnp.concatenate([dyv * bg, dcn], axis=0), w_ref[...])
        dpm_ref[:, :D] = (dyv * _conv_apply(ext_s, w_ref[...])).astype(BF16)
        dpm_ref[:, D:2 * D] = (ds * hv).astype(BF16)
        dpm_ref[:, 2 * D:] = (ds * cg).astype(BF16)

        @pl.when(i == 0)
        def _():
            dw_ref[...] = jnp.zeros_like(dw_ref)
        dw_ref[...] += dw

    return pl.pallas_call(
        body, name=name, grid=(n_t,),
        in_specs=[_row_spec(D), _next_spec(D, T), _row_spec(3 * D), _prev_spec(3 * D, T), _next_spec(3 * D, T),
                  _full_spec((3, D))],
        out_specs=[_row_spec(3 * D), _full_spec((3, D))],
        out_shape=[SDS((T, 3 * D), BF16), SDS((3, D), F32)],
        compiler_params=_params(1, VMEM_LIMIT))(dy, dy, pm, pm, pm, w)


def _dnconv_fwd(pm, w, name):
    T = pm.shape[0]

    def body(pm_ref, prev_ref, w_ref, c_ref):
        sp = jnp.where(pl.program_id(0) > 0, prev_ref[...], 0.0)
        c_ref[...] = _conv_apply(jnp.concatenate([sp, pm_ref[...]], axis=0), w_ref[...])

    return pl.pallas_call(
        body, name=name, grid=(T // TT,),
        in_specs=[_row_spec(3 * D), _prev_spec(3 * D, T), _full_spec((4, 3 * D))],
        out_specs=_row_spec(3 * D), out_shape=SDS((T, 3 * D), F32),
        compiler_params=_params(1, VMEM_LIMIT))(pm, pm, w)


def _dnconv_bwd(dcq, dck, dcv, dab, pm, w, name):
    T = pm.shape[0]
    n_t = T // TT

    def body(dq_ref, dk_ref, dv_ref, dqn_ref, dkn_ref, dvn_ref, dab_ref, pm_ref, prev_ref, w_ref, dpm_ref, dw_ref):
        i = pl.program_id(0)
        sp = jnp.where(i > 0, prev_ref[...], 0.0)
        ext_s = jnp.concatenate([sp, pm_ref[...]], axis=0)
        own = jnp.concatenate([dq_ref[...], dk_ref[...], dv_ref[...]], axis=1)
        nxt = jnp.where(i < n_t - 1, jnp.concatenate([dqn_ref[...], dkn_ref[...], dvn_ref[...]], axis=1), 0.0)
        ds, dw = _conv_grads(ext_s, jnp.concatenate([own, nxt], axis=0), w_ref[...])
        dpm_ref[:, :3 * D] = ds.astype(BF16)
        dpm_ref[:, 3 * D:] = dab_ref[...].astype(BF16)

        @pl.when(i == 0)
        def _():
            dw_ref[...] = jnp.zeros_like(dw_ref)
        dw_ref[...] += dw

    return pl.pallas_call(
        body, name=name, grid=(n_t,),
        in_specs=[_row_spec(D), _row_spec(D), _row_spec(D), _next_spec(D, T), _next_spec(D, T), _next_spec(D, T),
                  _row_spec(AB_PAD), _row_spec(3 * D), _prev_spec(3 * D, T), _full_spec((4, 3 * D))],
        out_specs=[_row_spec(3 * D + AB_PAD), _full_spec((4, 3 * D))],
        out_shape=[SDS((T, 3 * D + AB_PAD), BF16), SDS((4, 3 * D), F32)],
        compiler_params=_params(1, VMEM_LIMIT))(dcq, dck, dcv, dcq, dck, dcv, dab, pm, pm, w)


def _l2n(x):
    return x * lax.rsqrt(jnp.sum(x * x, axis=-1, keepdims=True) + EPS)


def _softplus(x):
    return jnp.maximum(x, 0.0) + jnp.log1p(jnp.exp(-jnp.abs(x)))


_BNN = (((2,), (1,)), ((0,), (0,)))
_BNT = (((2,), (2,)), ((0,), (0,)))
_BTN = (((1,), (1,)), ((0,), (0,)))


def _bdot(a, b, dims):
    return lax.dot_general(a.astype(BF16), b.astype(BF16), dims, preferred_element_type=F32)


def _bdot3(a, b, dims):
    ah, bh = a.astype(BF16), b.astype(BF16)
    al, bl = (a - ah.astype(F32)).astype(BF16), (b - bh.astype(F32)).astype(BF16)
    d = functools.partial(lax.dot_general, dimension_numbers=dims, preferred_element_type=F32)
    return d(ah, bh) + (d(ah, bl) + d(al, bh))


def _bdot_hi(a, b, dims):
    return lax.dot_general(a, b, dims, precision=HI, preferred_element_type=F32)


def _batched_matmuls(dot):
    @jax.custom_vjp
    def nn(a, b):
        return dot(a, b, _BNN)

    @jax.custom_vjp
    def nt(a, b):
        return dot(a, b, _BNT)

    @jax.custom_vjp
    def tn(a, b):
        return dot(a, b, _BTN)

    nn.defvjp(lambda a, b: (dot(a, b, _BNN), (a, b)), lambda r, g: (dot(g, r[1], _BNT), dot(r[0], g, _BTN)))
    nt.defvjp(lambda a, b: (dot(a, b, _BNT), (a, b)), lambda r, g: (dot(g, r[1], _BNN), dot(g, r[0], _BTN)))
    tn.defvjp(lambda a, b: (dot(a, b, _BTN), (a, b)), lambda r, g: (dot(r[1], g, _BNT), dot(r[0], g, _BNN)))
    return nn, nt, tn


bmm, bmm_nt, bmm_tn = _batched_matmuls(_bdot)
bmm_hi, _, _ = _batched_matmuls(_bdot_hi)

@jax.custom_vjp
def _neumann_inverse(n):
    C = n.shape[1]
    eye = lax.broadcasted_iota(jnp.int32, n.shape, 1) == lax.broadcasted_iota(jnp.int32, n.shape, 2)
    t = eye.astype(F32) + n
    for _ in range(5):
        n = _bdot3(n, n, _BNN)
        t = t + _bdot3(t, n, _BNN)
    return t


def _neumann_inverse_fwd(n):
    t = _neumann_inverse(n)
    return t, t


def _neumann_inverse_bwd(t, g):
    return (_bdot3(_bdot3(t, g, _BTN), t, _BNT),)


_neumann_inverse.defvjp(_neumann_inverse_fwd, _neumann_inverse_bwd)


@jax.custom_vjp
def _saved_inverse(n, t):
    return t


_saved_inverse.defvjp(lambda n, t: (t, t), lambda t, g: (_bdot3(_bdot3(t, g, _BTN), t, _BNT), jnp.zeros_like(t)))

DN_NCH = DN_TB // DN_C
DN_NH = 4


def _decay_terms(ab, alog, dtb, first_head, n_heads):
    C = DN_C
    lane = lax.broadcasted_iota(jnp.int32, ab.shape, 1)
    g_all = (-jnp.exp(alog) * _softplus(ab + dtb)).reshape(DN_NCH, C, HD)
    beta_all = jax.nn.sigmoid(ab)
    r = lax.broadcasted_iota(jnp.int32, (DN_NCH, C, C), 1)
    c = lax.broadcasted_iota(jnp.int32, (DN_NCH, C, C), 2)
    gc_all = bmm_hi((r >= c).astype(F32), g_all)
    gc_rows = [gc_all[i].T for i in range(DN_NCH)]
    lane3 = lax.broadcasted_iota(jnp.int32, (DN_NCH, C, HD), 2)
    row = lax.broadcasted_iota(jnp.int32, (HD, C), 0)
    ones = jnp.ones((1, HD), F32)
    gcs, gjs, betas = [], [], []
    for i in range(n_heads):
        h = first_head + i
        gcs.append(jnp.sum(jnp.where(lane3 == h, gc_all, 0.0), axis=2, keepdims=True) * ones)
        gjs.append(jnp.concatenate(
            [jnp.broadcast_to(jnp.sum(jnp.where(row == h, t, 0.0), axis=0, keepdims=True), (C, C))[None] for t in gc_rows],
            axis=0))
        beta = jnp.sum(jnp.where(lane == 8 + h, beta_all, 0.0), axis=1, keepdims=True) * ones
        betas.append(beta.reshape(DN_NCH, C, HD))
    return jnp.concatenate(gcs, axis=0), jnp.concatenate(gjs, axis=0), jnp.concatenate(betas, axis=0)


def _dn_prep(cq, ck, cv, gcum, gj, bb, t_saved=None):
    B, C = cq.shape[0], DN_C
    q = _l2n(jax.nn.silu(cq)) * (HD ** -0.5)
    k = _l2n(jax.nn.silu(ck))
    v = jax.nn.silu(cv)
    r = lax.broadcasted_iota(jnp.int32, (B, C, C), 1)
    c = lax.broadcasted_iota(jnp.int32, (B, C, C), 2)
    incl, strict = r >= c, r > c
    decay = jnp.where(incl, jnp.exp(jnp.where(incl, gcum[:, :, :C] - gj, 0.0)), 0.0)
    kb = k * bb
    n_mat = -jnp.where(strict, bmm_nt(kb, k) * decay, 0.0)
    t_mat = _neumann_inverse(n_mat) if t_saved is None else _saved_inverse(n_mat, t_saved)
    eg = jnp.exp(gcum)
    glast = gcum[:, C - 1:C, :]
    return (bmm(t_mat, v * bb), bmm(t_mat, kb * eg), bmm_nt(q, k) * decay, q * eg, k * jnp.exp(glast - gcum),
            jnp.exp(glast), t_mat)


def _dn_scan_step(u, w, qk, qd, kd, egl, S, onw):
    v_new = u - bmm(w, S)
    o = bmm(qd, S) + bmm(qk, v_new)
    return _rms(o, onw), S * egl + bmm_tn(kd, v_new)


def _to_batch(ref, n_heads):
    return jnp.concatenate([ref[:, i * HD:(i + 1) * HD].astype(F32).reshape(DN_NCH, DN_C, HD) for i in range(n_heads)],
                           axis=0)


def _from_batch(ref, val, n_heads):
    for i in range(n_heads):
        ref[:, i * HD:(i + 1) * HD] = val[i * DN_NCH:(i + 1) * DN_NCH].reshape(DN_TB, HD).astype(ref.dtype)


def _prep_specs(T, rev):
    nb = T // DN_TB
    blk = (lambda n: nb - 1 - n) if rev else (lambda n: n)
    ng = 8 // DN_NH
    head = [pl.BlockSpec((DN_TB, DN_NH * HD), functools.partial(lambda n, h, off: (blk(n), off + h), off=ng * s))
            for s in range(3)]
    ab = pl.BlockSpec((DN_TB, AB_PAD), lambda n, h: (blk(n), 3 * D // AB_PAD))
    row = pl.BlockSpec((1, HD), lambda n, h: (0, 0))
    wide = pl.BlockSpec((DN_TB, DN_NH * HD), lambda n, h: (blk(n), h))
    qk = pl.BlockSpec((DN_NCH, DN_NH, DN_C, DN_C), lambda n, h: (blk(n), h, 0, 0))
    eg = pl.BlockSpec((DN_NCH, DN_NH, 1, HD), lambda n, h: (blk(n), h, 0, 0))
    return nb, ng, head, ab, row, wide, qk, eg


def _dn_prep_fwd(cpre, pm, alog, dtb, name):
    T = cpre.shape[0]
    nb, ng, head, ab, row, wide, qks, egs = _prep_specs(T, False)

    def body(cq_ref, ck_ref, cv_ref, ab_ref, alog_ref, dtb_ref, u_ref, w_ref, qk_ref, qd_ref, kd_ref, e_ref, t_ref):
        gcum, gj, bb = _decay_terms(ab_ref[...], alog_ref[...], dtb_ref[...], pl.program_id(1) * DN_NH, DN_NH)
        u, w, qk, qd, kd, egl, t_mat = _dn_prep(_to_batch(cq_ref, DN_NH), _to_batch(ck_ref, DN_NH),
                                                _to_batch(cv_ref, DN_NH), gcum, gj, bb)
        _from_batch(u_ref, u, DN_NH)
        _from_batch(w_ref, w, DN_NH)
        _from_batch(qd_ref, qd, DN_NH)
        _from_batch(kd_ref, kd, DN_NH)
        for i in range(DN_NH):
            qk_ref[:, i] = qk[i * DN_NCH:(i + 1) * DN_NCH].astype(BF16)
            e_ref[:, i] = egl[i * DN_NCH:(i + 1) * DN_NCH]
            t_ref[:, i] = t_mat[i * DN_NCH:(i + 1) * DN_NCH]

    return pl.pallas_call(
        body, name=name, grid=(nb, ng), in_specs=head + [ab, row, row],
        out_specs=[wide, wide, qks, wide, wide, egs, qks],
        out_shape=[SDS((T, D), F32), SDS((T, D), BF16), SDS((T // DN_C, 8, DN_C, DN_C), BF16), SDS((T, D), BF16),
                   SDS((T, D), BF16), SDS((T // DN_C, 8, 1, HD), F32), SDS((T // DN_C, 8, DN_C, DN_C), F32)],
        compiler_params=_params(2, VMEM_LIMIT))(cpre, cpre, cpre, pm, alog, dtb)


def _dn_prep_bwd(du, dw, dqk, dqd, dkd, degl, t_mat, cpre, pm, alog, dtb, name):
    T = cpre.shape[0]
    nb, ng, head, ab, row, wide, qks, egs = _prep_specs(T, True)

    def body(du_ref, dw_ref, dqk_ref, dqd_ref, dkd_ref, de_ref, t_ref, cq_ref, ck_ref, cv_ref, ab_ref, alog_ref,
             dtb_ref, dcq_ref, dck_ref, dcv_ref, dab_ref, dalog_ref, ddtb_ref):
        n, h = pl.program_id(0), pl.program_id(1)

        @pl.when((n == 0) & (h == 0))
        def _():
            dalog_ref[...] = jnp.zeros_like(dalog_ref)
            ddtb_ref[...] = jnp.zeros_like(ddtb_ref)

        @pl.when(h == 0)
        def _():
            dab_ref[...] = jnp.zeros_like(dab_ref)

        t_saved = jnp.concatenate([t_ref[:, i] for i in range(DN_NH)], axis=0)

        def fwd(cq, ck, cv, ab_v, alog_v, dtb_v):
            gcum, gj, bb = _decay_terms(ab_v, alog_v, dtb_v, h * DN_NH, DN_NH)
            return _dn_prep(cq, ck, cv, gcum, gj, bb, t_saved)[:6]

        _, vjp = jax.vjp(fwd, _to_batch(cq_ref, DN_NH), _to_batch(ck_ref, DN_NH), _to_batch(cv_ref, DN_NH), ab_ref[...],
                         alog_ref[...], dtb_ref[...])
        cot = (_to_batch(du_ref, DN_NH), _to_batch(dw_ref, DN_NH),
               jnp.concatenate([dqk_ref[:, i] for i in range(DN_NH)], axis=0), _to_batch(dqd_ref, DN_NH),
               _to_batch(dkd_ref, DN_NH), jnp.concatenate([de_ref[:, i] for i in range(DN_NH)], axis=0))
        dcq, dck, dcv, dab, dalog, ddtb = vjp(cot)
        _from_batch(dcq_ref, dcq, DN_NH)
        _from_batch(dck_ref, dck, DN_NH)
        _from_batch(dcv_ref, dcv, DN_NH)
        dab_ref[...] += dab
        dalog_ref[...] += dalog
        ddtb_ref[...] += ddtb

    dabspec = pl.BlockSpec((DN_TB, AB_PAD), lambda n, h: (nb - 1 - n, 0))
    return pl.pallas_call(
        body, name=name, grid=(nb, ng),
        in_specs=[wide, wide, qks, wide, wide, egs, qks] + head + [ab, row, row],
        out_specs=[wide, wide, wide, dabspec, row, row],
        out_shape=[SDS((T, D), F32)] * 3 + [SDS((T, AB_PAD), F32)] + [SDS((1, HD), F32)] * 2,
        compiler_params=_params(2, VMEM_LIMIT))(du, dw, dqk, dqd, dkd, degl, t_mat, cpre, cpre, cpre, pm, alog, dtb)


def _scan_specs(T, rev):
    nb = T // DN_TB
    blk = (lambda n: nb - 1 - n) if rev else (lambda n: n)
    wide = pl.BlockSpec((DN_TB, D), lambda n: (blk(n), 0))
    qk = pl.BlockSpec((DN_NCH, 8, DN_C, DN_C), lambda n: (blk(n), 0, 0, 0))
    eg = pl.BlockSpec((DN_NCH, 8, 1, HD), lambda n: (blk(n), 0, 0, 0))
    st = pl.BlockSpec((DN_NCH, 8, HD, HD), lambda n: (blk(n), 0, 0, 0))
    row = pl.BlockSpec((1, HD), lambda n: (0, 0))
    return nb, wide, qk, eg, st, row


def _heads_of(ref, rows):
    return jnp.concatenate([ref[rows, h * HD:(h + 1) * HD].astype(F32)[None] for h in range(8)], axis=0)


def _dn_scan_fwd(u, w, qk, qd, kd, egl, onw, name):
    T = u.shape[0]
    nb, wide, qks, egs, sts, row = _scan_specs(T, False)

    def body(u_ref, w_ref, qk_ref, qd_ref, kd_ref, e_ref, onw_ref, o_ref, st_ref, s_scr):
        @pl.when(pl.program_id(0) == 0)
        def _():
            s_scr[...] = jnp.zeros_like(s_scr)
        S = s_scr[...]
        for c in range(DN_NCH):
            rows = slice(c * DN_C, (c + 1) * DN_C)
            st_ref[c] = S
            o, S = _dn_scan_step(_heads_of(u_ref, rows), _heads_of(w_ref, rows), qk_ref[c].astype(F32),
                                 _heads_of(qd_ref, rows), _heads_of(kd_ref, rows), e_ref[c], S, onw_ref[...])
            for h in range(8):
                o_ref[rows, h * HD:(h + 1) * HD] = o[h]
        s_scr[...] = S

    return pl.pallas_call(
        body, name=name, grid=(nb,), in_specs=[wide, wide, qks, wide, wide, egs, row], out_specs=[wide, sts],
        out_shape=[SDS((T, D), F32), SDS((T // DN_C, 8, HD, HD), F32)],
        scratch_shapes=[pltpu.VMEM((8, HD, HD), F32)],
        compiler_params=_params(1, VMEM_LIMIT))(u, w, qk, qd, kd, egl, onw)


def _dn_scan_bwd(do, u, w, qk, qd, kd, egl, st, onw, name):
    T = u.shape[0]
    nb, wide, qks, egs, sts, row = _scan_specs(T, True)

    def body(do_ref, u_ref, w_ref, qk_ref, qd_ref, kd_ref, e_ref, st_ref, onw_ref,
             du_ref, dw_ref, dqk_ref, dqd_ref, dkd_ref, de_ref, donw_ref, ds_scr):
        @pl.when(pl.program_id(0) == 0)
        def _():
            ds_scr[...] = jnp.zeros_like(ds_scr)
            donw_ref[...] = jnp.zeros_like(donw_ref)
        dS = ds_scr[...]
        donw = jnp.zeros((1, HD), F32)
        for c in reversed(range(DN_NCH)):
            rows = slice(c * DN_C, (c + 1) * DN_C)
            _, vjp = jax.vjp(_dn_scan_step, _heads_of(u_ref, rows), _heads_of(w_ref, rows), qk_ref[c].astype(F32),
                             _heads_of(qd_ref, rows), _heads_of(kd_ref, rows), e_ref[c], st_ref[c], onw_ref[...])
            du, dw, dqk, dqd, dkd, de, dS, dn = vjp((_heads_of(do_ref, rows), dS))
            for h in range(8):
                cols = slice(h * HD, (h + 1) * HD)
                du_ref[rows, cols] = du[h]
                dw_ref[rows, cols] = dw[h]
                dqd_ref[rows, cols] = dqd[h]
                dkd_ref[rows, cols] = dkd[h]
            dqk_ref[c] = dqk
            de_ref[c] = de
            donw += dn
        ds_scr[...] = dS
        donw_ref[...] += donw

    return pl.pallas_call(
        body, name=name, grid=(nb,), in_specs=[wide, wide, wide, qks, wide, wide, egs, sts, row],
        out_specs=[wide, wide, qks, wide, wide, egs, row],
        out_shape=[SDS((T, D), F32), SDS((T, D), F32), SDS((T // DN_C, 8, DN_C, DN_C), F32), SDS((T, D), F32),
                   SDS((T, D), F32), SDS((T // DN_C, 8, 1, HD), F32), SDS((1, HD), F32)],
        scratch_shapes=[pltpu.VMEM((8, HD, HD), F32)],
        compiler_params=_params(1, VMEM_LIMIT))(do, u, w, qk, qd, kd, egl, st, onw)


def _adamw(w, g, m, v, name):
    R, C = w.shape
    tr = 256 if R % 256 == 0 and R > 256 else R
    tc = 256 if tr == R and R > 256 and C % 256 == 0 else C
    c1 = 1.0 - ADAM_B1 ** ADAM_STEP
    c2 = 1.0 - ADAM_B2 ** ADAM_STEP

    def body(w_ref, g_ref, m_ref, v_ref, d_ref, nm_ref, nv_ref):
        gv = g_ref[...]
        nm = ADAM_B1 * m_ref[...] + (1.0 - ADAM_B1) * gv
        nv = ADAM_B2 * v_ref[...] + (1.0 - ADAM_B2) * (gv * gv)
        nm_ref[...] = nm
        nv_ref[...] = nv
        d_ref[...] = -ADAM_LR * ((nm / c1) / (jnp.sqrt(nv / c2) + ADAM_EPS) + ADAM_WD * w_ref[...])

    spec = pl.BlockSpec((tr, tc), lambda i, j: (i, j))
    return pl.pallas_call(
        body, name=name, grid=(R // tr, C // tc), in_specs=[spec] * 4, out_specs=[spec] * 3,
        out_shape=[SDS((R, C), F32)] * 3, compiler_params=_params(2, VMEM_LIMIT))(w, g, m, v)


def _local_step(x, mem, target, wts, sm):
    kinds = [i % 3 for i in range(DEPTH)]
    mnw = sm["mem_norm_w"].reshape(1, D)
    kv, wkv = None, None
    saved, blocks = [], []
    for i, kind in enumerate(kinds):
        j = i // 3
        npre = sm["norm_pre"][i].reshape(1, D)
        npost = sm["norm_post"][i].reshape(1, D)
        mix, gate, wo_after = wts["blocks"](i, x)
        pm, pg, h = _inproj_fwd(x, npre, mix, gate, kind == 2, f"inproj_fwd_{i}")
        if kv is None:
            wkv = wts["wkv_after"](h)
            kv = _memkv_fwd(mem, mnw, wkv)
        extra = None
        if kind == 0:
            bs3 = jnp.broadcast_to(sm["a_b_s"][j][:, :, None], (8, HD, HD))
            ymix = _gmlp_fwd(pm, sm["a_ln_w"][j].reshape(1, D), sm["a_ln_b"][j].reshape(1, D), sm["a_w_s"][j], bs3,
                             f"gmlp_fwd_{i}")
            extra = bs3
        elif kind == 1:
            ymix = _sconv_fwd(pm, sm["b_conv_w"][j], f"sconv_fwd_{i}")
        else:
            cpre = _dnconv_fwd(pm, sm["c_conv_w"][j], f"dnconv_fwd_{i}")
            alog = jnp.pad(sm["c_a_log"][j], (0, HD - 8)).reshape(1, HD)
            dtb = jnp.pad(sm["c_dt_bias"][j], (0, HD - 8)).reshape(1, HD)
            onw = sm["c_o_norm_w"][j].reshape(1, HD)
            *prep, t_mat = _dn_prep_fwd(cpre, pm, alog, dtb, f"dn_prep_fwd_{i}")
            ymix, st = _dn_scan_fwd(*prep, onw, f"dn_scan_fwd_{i}")
            extra = (cpre, prep, t_mat, st, alog, dtb, onw)
        wo = wo_after(ymix)
        blocks.append((mix, gate, wo))
        layer_in = x
        if i < DEPTH - 1:
            ycat, o, x = _gate_outproj_fwd(ymix, pg, kv, wo, x, npost, None, f"gate_outproj_fwd_{i}")
        else:
            ycat, o, loss, dx = _gate_outproj_fwd(ymix, pg, kv, wo, x, npost, target, f"gate_outproj_fwd_{i}")
        saved.append((layer_in, h, pm, pg, ymix, ycat, o, extra))

    g = {"wm": [None] * DEPTH, "wg": [None] * DEPTH, "wo": [None] * DEPTH, "norm_pre": [None] * DEPTH,
         "norm_post": [None] * DEPTH}
    dkv = jnp.zeros((N_MEM, 2 * D_XA), F32)
    sent = 0.0
    for i in reversed(range(DEPTH)):
        kind, j = kinds[i], i // 3
        xi, h, pm, pg, ymix, ycat, o, extra = saved[i]
        npre = sm["norm_pre"][i].reshape(1, D)
        npost = sm["norm_post"][i].reshape(1, D) + sent
        dobf, g["norm_post"][i], dymix, dpg, dkv = _outproj_gate_bwd(dx, o, npost, blocks[i][2], ymix, pg, kv, dkv,
                                                                     f"outproj_gate_bwd_{i}")
        g["wo"][i] = _matmul_tn(ycat, dobf, f"dwo_{i}")
        if kind == 0:
            dpm, dlnw, dlnb, dws, dbs3 = _gmlp_bwd(dymix, pm, sm["a_ln_w"][j].reshape(1, D),
                                                   sm["a_ln_b"][j].reshape(1, D), sm["a_w_s"][j], extra,
                                                   f"gmlp_bwd_{i}")
            g.setdefault("a_ln_w", {})[j] = dlnw.reshape(D)
            g.setdefault("a_ln_b", {})[j] = dlnb.reshape(D)
            g.setdefault("a_w_s", {})[j] = dws
            g.setdefault("a_b_s", {})[j] = dbs3[:, :, 0]
        elif kind == 1:
            dpm, dcw = _sconv_bwd(dymix, pm, sm["b_conv_w"][j], f"sconv_bwd_{i}")
            g.setdefault("b_conv_w", {})[j] = dcw
        else:
            cpre, prep, t_mat, st, alog, dtb, onw = extra
            *dprep, donw = _dn_scan_bwd(dymix, *prep, st, onw, f"dn_scan_bwd_{i}")
            dcq, dck, dcv, dab, dalog, ddtb = _dn_prep_bwd(*dprep, t_mat, cpre, pm, alog, dtb, f"dn_prep_bwd_{i}")
            dpm, dcw = _dnconv_bwd(dcq, dck, dcv, dab, pm, sm["c_conv_w"][j], f"dnconv_bwd_{i}")
            g.setdefault("c_conv_w", {})[j] = dcw
            g.setdefault("c_a_log", {})[j] = dalog[0, :8]
            g.setdefault("c_dt_bias", {})[j] = ddtb[0, :8]
            g.setdefault("c_o_norm_w", {})[j] = donw[0]
        if kind == 2:
            g["wm"][i] = _matmul_tn(dpm, h, f"dwm_{i}")
            g["wg"][i] = _matmul_tn(dpg, h, f"dwg_{i}")
        else:
            g["wm"][i] = _matmul_tn(h, dpm, f"dwm_{i}", GRAD_TILE[kind])
            g["wg"][i] = _matmul_tn(h, dpg, f"dwg_{i}", GRAD_TILE[kind])
        sent = wts["layer_done"](i, g)
        dx, g["norm_pre"][i] = _inproj_bwd(dpm, dpg, xi, npre + sent, blocks[i][0], blocks[i][1], kind == 2, dx,
                                           f"inproj_bwd_{i}")
    g["mem_norm_w"], g["wkv"] = _memkv_bwd(mem, mnw, wkv, dkv)
    return loss[0, 0], dx, g


ANY = pl.BlockSpec(memory_space=pl.ANY)


def _place():
    return lax.axis_index("x"), lax.axis_index("y"), lax.axis_index("c")


def _add(a, b, name):
    def body(a_ref, b_ref, o_ref):
        o_ref[...] = a_ref[...] + b_ref[...]

    return pl.pallas_call(body, name=name, out_shape=SDS(a.shape, a.dtype), compiler_params=_params(0, VMEM_LIMIT))(a, b)


def _sum4(own, land):
    def body(own_ref, l_ref, o_ref):
        chip = 2 * lax.axis_index("x") + lax.axis_index("y")
        acc = jnp.where(chip == 0, own_ref[...], l_ref[0])
        for s in range(1, 4):
            acc = acc + jnp.where(chip == s, own_ref[...], l_ref[s])
        o_ref[...] = acc

    return pl.pallas_call(body, name="sum_small", out_shape=SDS(own.shape, own.dtype),
                          compiler_params=_params(0, VMEM_LIMIT))(own, land)


C_ROWS = 1312
_REDUCE_CHUNK = {512: 256, 2560: 640}
W_CLASSES = {"in0": (512, 256)}


def _chunk_list(specs):
    return [(k, r, chunk) for k, (half, chunk) in enumerate(specs) for r in range(0, half, chunk)]


def _gather_classes(arrs, specs, vec):
    n = len(arrs)
    chunks = _chunk_list(specs)
    nc = len(chunks)

    def body(*refs):
        ins, vec_ref, outs, ov_ref = refs[:n], refs[n], refs[n + 1:2 * n + 1], refs[2 * n + 1]
        ici_send, ici_recv, d2d_send, d2d_recv, vec_send, vec_recv = refs[2 * n + 2:]
        x, y, c = _place()
        chip = 2 * x + y
        peers = [(1 - x, y), (x, 1 - y), (1 - x, 1 - y)]

        def rows(ci, half):
            k, r, cnt = chunks[ci]
            return k, pl.ds(half * specs[k][0] + r, cnt)

        def over_ici(j, ci, slab):
            px, py = peers[j]
            k, rs = rows(ci, c)
            return pltpu.make_async_remote_copy(
                src_ref=ins[k].at[rs], dst_ref=outs[k].at[slab, rs], send_sem=ici_send.at[j * nc + ci],
                recv_sem=ici_recv.at[j * nc + ci], device_id=(px, py, c), device_id_type=MESH)

        def over_d2d(j, ci, half):
            px, py = peers[j]
            k, rs = rows(ci, half)
            where = outs[k].at[2 * px + py, rs]
            return pltpu.make_async_remote_copy(
                src_ref=where, dst_ref=where, send_sem=d2d_send.at[j * nc + ci], recv_sem=d2d_recv.at[j * nc + ci],
                device_id=(x, y, 1 - c), device_id_type=MESH)

        def small(j, slab):
            px, py = peers[j]
            return pltpu.make_async_remote_copy(
                src_ref=vec_ref, dst_ref=ov_ref.at[slab], send_sem=vec_send.at[j], recv_sem=vec_recv.at[j],
                device_id=(px, py, c), device_id_type=MESH)

        sends = [small(j, chip) for j in range(3)] + [over_ici(j, ci, chip) for ci in range(nc) for j in range(3)]
        for cp in sends:
            cp.start()
        forwards = []
        for ci in range(nc):
            for j, (px, py) in enumerate(peers):
                over_ici(j, ci, 2 * px + py).wait_recv()
                forwards.append(over_d2d(j, ci, c))
                forwards[-1].start()
        for ci in range(nc):
            for j in range(3):
                over_d2d(j, ci, 1 - c).wait_recv()
        for j, (px, py) in enumerate(peers):
            small(j, 2 * px + py).wait_recv()
        for cp in sends + forwards:
            cp.wait_send()

    dma = pltpu.SemaphoreType.DMA
    return pl.pallas_call(
        body, name="gather_weights", in_specs=[ANY] * (n + 1), out_specs=[ANY] * (n + 1),
        out_shape=[SDS((4,) + a.shape, a.dtype) for a in arrs] + [SDS((4,) + vec.shape, vec.dtype)],
        scratch_shapes=[dma((3 * nc,)), dma((3 * nc,)), dma((3 * nc,)), dma((3 * nc,)), dma((3,)), dma((3,))])(*arrs, vec)


def _swap_classes(grads, specs, small):
    n = len(grads)
    chunks = _chunk_list(specs)

    def body(*refs):
        ins, s_ref, outs, os_ref, send_sems, recv_sems = refs[:n], refs[n], refs[n + 1:2 * n + 1], *refs[2 * n + 1:]
        x, y, c = _place()
        copies = []
        for s in range(4):
            for k, r, cnt in chunks:
                copies.append(pltpu.make_async_remote_copy(
                    src_ref=ins[k].at[s, 1 - c, pl.ds(r, cnt)], dst_ref=outs[k].at[s, pl.ds(r, cnt)],
                    send_sem=send_sems.at[len(copies)], recv_sem=recv_sems.at[len(copies)],
                    device_id=(x, y, 1 - c), device_id_type=MESH))
        copies.append(pltpu.make_async_remote_copy(
            src_ref=s_ref, dst_ref=os_ref, send_sem=send_sems.at[len(copies)], recv_sem=recv_sems.at[len(copies)],
            device_id=(x, y, 1 - c), device_id_type=MESH))
        for cp in copies:
            cp.start()
        for cp in copies:
            cp.wait_recv()
        for cp in copies:
            cp.wait_send()

    m = 4 * len(chunks) + 1
    return pl.pallas_call(
        body, name="swap_halves", in_specs=[ANY] * (n + 1), out_specs=[ANY] * (n + 1),
        out_shape=[SDS((4, g.shape[2], g.shape[3]), F32) for g in grads] + [SDS(small.shape, F32)],
        scratch_shapes=[pltpu.SemaphoreType.DMA((m,)), pltpu.SemaphoreType.DMA((m,))])(*grads, small)


def _pair_sum_class(g, other, chunk, name):
    _, _, half, w = g.shape

    def body(g_ref, o_ref, pb_ref, own_ref):
        x, y, c = _place()
        v = jnp.where(c == 0, g_ref[0], g_ref[1]) + o_ref[...]
        pb_ref[...] = v.astype(BF16)

        @pl.when(pl.program_id(1) == 2 * x + y)
        def _():
            own_ref[...] = v

    return pl.pallas_call(
        body, name=name, grid=(half // chunk, 4),
        in_specs=[pl.BlockSpec((None, 2, chunk, w), lambda i, s: (s, 0, i, 0)),
                  pl.BlockSpec((None, chunk, w), lambda i, s: (s, i, 0))],
        out_specs=[pl.BlockSpec((None, chunk, w), lambda i, s: (s, i, 0)), pl.BlockSpec((chunk, w), lambda i, s: (i, 0))],
        out_shape=[SDS((4, half, w), BF16), SDS((half, w), F32)],
        compiler_params=_params(2, VMEM_LIMIT))(g, other)


def _exchange_classes(pbs, specs, ps):
    n = len(pbs)
    chunks = _chunk_list(specs)
    per_peer = len(chunks) + 1

    def body(*refs):
        ins, ps_ref, outs, ls_ref, send_sems, recv_sems = refs[:n], refs[n], refs[n + 1:2 * n + 1], *refs[2 * n + 1:]
        x, y, c = _place()
        chip = 2 * x + y
        peers = [(1 - x, y), (x, 1 - y), (1 - x, 1 - y)]

        def copies(slab_of):
            out = []
            for j, (px, py) in enumerate(peers):
                for k, r, cnt in chunks:
                    out.append(pltpu.make_async_remote_copy(
                        src_ref=ins[k].at[2 * px + py, pl.ds(r, cnt)], dst_ref=outs[k].at[slab_of(j), pl.ds(r, cnt)],
                        send_sem=send_sems.at[len(out)], recv_sem=recv_sems.at[len(out)], device_id=(px, py, c),
                        device_id_type=MESH))
                out.append(pltpu.make_async_remote_copy(
                    src_ref=ps_ref, dst_ref=ls_ref.at[slab_of(j)], send_sem=send_sems.at[len(out)],
                    recv_sem=recv_sems.at[len(out)], device_id=(px, py, c), device_id_type=MESH))
            return out

        sends = copies(lambda j: chip)
        for cp in sends:
            cp.start()
        for cp in copies(lambda j: 2 * peers[j][0] + peers[j][1]):
            cp.wait_recv()
        for cp in sends:
            cp.wait_send()

    m = 3 * per_peer
    return pl.pallas_call(
        body, name="chip_exchange", in_specs=[ANY] * (n + 1), out_specs=[ANY] * (n + 1),
        out_shape=[SDS(p.shape, BF16) for p in pbs] + [SDS((4,) + ps.shape, F32)],
        scratch_shapes=[pltpu.SemaphoreType.DMA((m,)), pltpu.SemaphoreType.DMA((m,))])(*pbs, ps)


def _chip_sum_class(own, land, chunk, name):
    half, w = own.shape

    def body(own_ref, l_ref, o_ref):
        chip = 2 * lax.axis_index("x") + lax.axis_index("y")
        acc = jnp.where(chip == 0, own_ref[...], l_ref[0].astype(F32))
        for s in range(1, 4):
            acc = acc + jnp.where(chip == s, own_ref[...], l_ref[s].astype(F32))
        o_ref[...] = acc

    return pl.pallas_call(
        body, name=name, grid=(half // chunk,),
        in_specs=[pl.BlockSpec((chunk, w), lambda i: (i, 0)), pl.BlockSpec((4, chunk, w), lambda i: (0, i, 0))],
        out_specs=pl.BlockSpec((chunk, w), lambda i: (i, 0)), out_shape=SDS((half, w), F32),
        compiler_params=_params(1, VMEM_LIMIT))(own, land)


def _share_classes(rs, specs):
    n = len(rs)
    chunks = _chunk_list(specs)

    def body(*refs):
        ins, outs, send_sems, recv_sems = refs[:n], refs[n:2 * n], *refs[2 * n:]
        x, y, c = _place()
        copies = [pltpu.make_async_remote_copy(
            src_ref=ins[k].at[pl.ds(r, cnt)], dst_ref=outs[k].at[pl.ds(r, cnt)], send_sem=send_sems.at[i],
            recv_sem=recv_sems.at[i], device_id=(x, y, 1 - c), device_id_type=MESH)
            for i, (k, r, cnt) in enumerate(chunks)]
        for cp in copies:
            cp.start()
        for cp in copies:
            cp.wait_recv()
        for cp in copies:
            cp.wait_send()

    m = len(chunks)
    return pl.pallas_call(
        body, name="share_half", in_specs=[ANY] * n, out_specs=[ANY] * n, out_shape=[SDS(r.shape, F32) for r in rs],
        scratch_shapes=[pltpu.SemaphoreType.DMA((m,)), pltpu.SemaphoreType.DMA((m,))])(*rs)


_HBM = pl.BlockSpec(memory_space=pltpu.HBM)
_SEM = pl.BlockSpec(memory_space=pltpu.SEMAPHORE)
_EFFECT = pltpu.SideEffectType.DATAFLOW_SIDE_EFFECTING


def _chip_peers():
    x, y, c = _place()
    return [(1 - x, y, c), (x, 1 - y, c), (1 - x, 1 - y, c)]


def _send_shard_start(v, name):
    def body(v_ref, land_ref, send_sems, recv_sems, v_thru, land_thru, token):
        x, y, c = _place()
        for j, peer in enumerate(_chip_peers()):
            pltpu.make_async_remote_copy(src_ref=v_ref, dst_ref=land_ref.at[2 * x + y], send_sem=send_sems.at[j],
                                         recv_sem=recv_sems.at[j], device_id=peer, device_id_type=MESH).start()
        token[...] = jnp.zeros_like(token)

    land_shape = (4,) + v.shape
    return pl.pallas_call(
        body, name=name,
        out_shape=(pltpu.SemaphoreType.DMA((3,)), pltpu.SemaphoreType.DMA((3,)), pltpu.HBM(v.shape, v.dtype),
                   pltpu.HBM(land_shape, v.dtype), SDS((8, 128), F32)),
        in_specs=(_HBM, _HBM), out_specs=(_SEM, _SEM, _HBM, _HBM, pl.BlockSpec(memory_space=pltpu.VMEM)),
        input_output_aliases={0: 2, 1: 3}, compiler_params=pltpu.CompilerParams(has_side_effects=_EFFECT),
    )(pltpu.with_memory_space_constraint(v, pltpu.HBM),
      pltpu.with_memory_space_constraint(lax.empty(land_shape, v.dtype), pltpu.HBM))


def _xor_peer(r):
    x, y, c = _place()
    return (1 - x if (r >> 2) & 1 else x, 1 - y if (r >> 1) & 1 else y, 1 - c if r & 1 else c)


def _send_pieces_start(parts, name):
    n = len(parts)

    def body(*refs):
        ins, lands = refs[:n], refs[n:2 * n]
        send_sems, recv_sems = refs[2 * n:2 * n + 2]
        token = refs[-1]
        x, y, c = _place()
        for r in range(1, 8):
            px, py, pc = _xor_peer(r)
            for k in range(n):
                pltpu.make_async_remote_copy(
                    src_ref=ins[k].at[2 * px + py, pc], dst_ref=lands[k].at[4 * x + 2 * y + c],
                    send_sem=send_sems.at[(r - 1) * n + k], recv_sem=recv_sems.at[(r - 1) * n + k],
                    device_id=(px, py, pc), device_id_type=MESH).start()
        token[...] = jnp.zeros_like(token)

    land_shapes = [(8,) + p.shape[2:] for p in parts]
    hbm = [pltpu.HBM(p.shape, p.dtype) for p in parts] + [pltpu.HBM(s, p.dtype) for s, p in zip(land_shapes, parts)]
    operands = [pltpu.with_memory_space_constraint(p, pltpu.HBM) for p in parts]
    operands += [pltpu.with_memory_space_constraint(lax.empty(s, p.dtype), pltpu.HBM) for s, p in zip(land_shapes, parts)]
    return pl.pallas_call(
        body, name=name,
        out_shape=(pltpu.SemaphoreType.DMA((7 * n,)), pltpu.SemaphoreType.DMA((7 * n,)), *hbm, SDS((8, 128), F32)),
        in_specs=(_HBM,) * (2 * n), out_specs=(_SEM, _SEM) + (_HBM,) * (2 * n) + (pl.BlockSpec(memory_space=pltpu.VMEM),),
        input_output_aliases={i: 2 + i for i in range(2 * n)},
        compiler_params=pltpu.CompilerParams(has_side_effects=_EFFECT))(*operands)


def _send_pieces_wait(started, after, name):
    send_sems, recv_sems, *thru, _ = started
    n = len(thru) // 2

    def body(*refs):
        ins, lands = refs[:n], refs[n:2 * n]
        send_sems, recv_sems = refs[2 * n:2 * n + 2]
        for r in range(1, 8):
            px, py, pc = _xor_peer(r)
            for k in range(n):
                copy = pltpu.make_async_remote_copy(
                    src_ref=ins[k].at[2 * px + py, pc], dst_ref=lands[k].at[4 * px + 2 * py + pc],
                    send_sem=send_sems.at[(r - 1) * n + k], recv_sem=recv_sems.at[(r - 1) * n + k],
                    device_id=(px, py, pc), device_id_type=MESH)
                copy.wait_send()
                copy.wait_recv()

    return pl.pallas_call(
        body, name=name, out_shape=tuple(pltpu.HBM(t.shape, t.dtype) for t in thru),
        in_specs=(_HBM,) * (2 * n) + (_SEM, _SEM, pl.BlockSpec(memory_space=pl.ANY)), out_specs=(_HBM,) * (2 * n),
        input_output_aliases={i: i for i in range(2 * n)},
        compiler_params=pltpu.CompilerParams(has_side_effects=_EFFECT))(*thru, send_sems, recv_sems, after)[n:]


def _sum8_class(own, land, chunk, name):
    rows, w = own.shape

    def body(own_ref, l_ref, o_ref):
        x, y, c = _place()
        me = 4 * x + 2 * y + c
        acc = jnp.where(me == 0, own_ref[...], l_ref[0].astype(F32))
        for d in range(1, 8):
            acc = acc + jnp.where(me == d, own_ref[...], l_ref[d].astype(F32))
        o_ref[...] = acc

    return pl.pallas_call(
        body, name=name, grid=(rows // chunk,),
        in_specs=[pl.BlockSpec((chunk, w), lambda i: (i, 0)), pl.BlockSpec((8, chunk, w), lambda i: (0, i, 0))],
        out_specs=pl.BlockSpec((chunk, w), lambda i: (i, 0)), out_shape=SDS((rows, w), F32),
        compiler_params=_params(1, VMEM_LIMIT))(own, land)


def _send_shard_wait(send_sems, recv_sems, v_thru, land_thru, after, name):
    def body(v_ref, land_ref, send_sems, recv_sems, after_ref, v_dead, got_ref):
        for j, (px, py, pc) in enumerate(_chip_peers()):
            copy = pltpu.make_async_remote_copy(src_ref=v_ref, dst_ref=land_ref.at[2 * px + py], send_sem=send_sems.at[j],
                                                recv_sem=recv_sems.at[j], device_id=(px, py, pc), device_id_type=MESH)
            copy.wait_send()
            copy.wait_recv()

    return pl.pallas_call(
        body, name=name,
        out_shape=(pltpu.HBM(v_thru.shape, v_thru.dtype), pltpu.HBM(land_thru.shape, land_thru.dtype)),
        in_specs=(_HBM, _HBM, _SEM, _SEM, pl.BlockSpec(memory_space=pl.ANY)), out_specs=(_HBM, _HBM),
        input_output_aliases={0: 0, 1: 1}, compiler_params=pltpu.CompilerParams(has_side_effects=_EFFECT),
    )(v_thru, land_thru, send_sems, recv_sems, after)[1]


_SMALL = ["mem_norm_w", "norm_pre", "norm_post", "a_ln_w", "a_ln_b", "a_w_s", "a_b_s", "b_conv_w", "c_conv_w",
          "c_a_log", "c_dt_bias", "c_o_norm_w"]
_SMALL_SHAPES = {"mem_norm_w": (D,), "norm_pre": (4, D), "norm_post": (4, D), "a_ln_w": (2, D), "a_ln_b": (2, D),
                 "a_w_s": (2, 8, HD, HD), "a_b_s": (2, 8, HD), "b_conv_w": (1, 3, D), "c_conv_w": (1, 4, 3 * D),
                 "c_a_log": (1, 8), "c_dt_bias": (1, 8), "c_o_norm_w": (1, HD)}
_SHARDED_SMALL = {"a_ln_w": D // 4, "a_ln_b": D // 4, "b_conv_w": D // 4, "c_conv_w": 3 * D // 4}
_SMALL_ROWS = 288


def _size(shape):
    n = 1
    for d in shape:
        n *= d
    return n


def kernel(x, mem, mem_norm_w, w_mem_kv, norm_pre, norm_post, w_out, a_w_in, a_ln_w, a_ln_b, a_w_s, a_b_s, b_w_in, b_conv_w, c_w_in, c_conv_w, c_a_log, c_dt_bias, c_o_norm_w, loss_target, m_mem_norm_w, m_w_mem_kv, m_norm_pre, m_norm_post, m_w_out, m_a_w_in, m_a_ln_w, m_a_ln_b, m_a_w_s, m_a_b_s, m_b_w_in, m_b_conv_w, m_c_w_in, m_c_conv_w, m_c_a_log, m_c_dt_bias, m_c_o_norm_w, v_mem_norm_w, v_w_mem_kv, v_norm_pre, v_norm_post, v_w_out, v_a_w_in, v_a_ln_w, v_a_ln_b, v_a_w_s, v_a_b_s, v_b_w_in, v_b_conv_w, v_c_w_in, v_c_conv_w, v_c_a_log, v_c_dt_bias, v_c_o_norm_w):
    names = ["mem_norm_w", "w_mem_kv", "norm_pre", "norm_post", "w_out", "a_w_in", "a_ln_w", "a_ln_b", "a_w_s", "a_b_s",
             "b_w_in", "b_conv_w", "c_w_in", "c_conv_w", "c_a_log", "c_dt_bias", "c_o_norm_w"]
    w = dict(zip(names, [mem_norm_w, w_mem_kv, norm_pre, norm_post, w_out, a_w_in, a_ln_w, a_ln_b, a_w_s, a_b_s, b_w_in,
                         b_conv_w, c_w_in, c_conv_w, c_a_log, c_dt_bias, c_o_norm_w]))
    m = dict(zip(names, [m_mem_norm_w, m_w_mem_kv, m_norm_pre, m_norm_post, m_w_out, m_a_w_in, m_a_ln_w, m_a_ln_b, m_a_w_s,
                         m_a_b_s, m_b_w_in, m_b_conv_w, m_c_w_in, m_c_conv_w, m_c_a_log, m_c_dt_bias, m_c_o_norm_w]))
    v = dict(zip(names, [v_mem_norm_w, v_w_mem_kv, v_norm_pre, v_norm_post, v_w_out, v_a_w_in, v_a_ln_w, v_a_ln_b, v_a_w_s,
                         v_a_b_s, v_b_w_in, v_b_conv_w, v_c_w_in, v_c_conv_w, v_c_a_log, v_c_dt_bias, v_c_o_norm_w]))
    chip = 2 * lax.axis_index("x") + lax.axis_index("y")

    def rows_of_ct(a):
        return a[0].T

    def with_mine(gathered, own):
        return lax.dynamic_update_slice(gathered, own[None], (chip,) + (0,) * own.ndim)

    first = [a_w_in[0].astype(BF16)]
    vec = jnp.concatenate([a_ln_w.reshape(-1), a_ln_b.reshape(-1), b_conv_w.reshape(-1), c_conv_w.reshape(-1)])
    vec = jnp.pad(vec, (0, 8 * D - vec.shape[0])).reshape(8, D)
    ga0, gvec = _gather_classes(first, [W_CLASSES[k] for k in W_CLASSES], vec)
    ga0, gvec = with_mine(ga0, first[0]), with_mine(gvec, vec)
    gv = gvec.reshape(4, 8 * D)
    w_in = {1: b_w_in[0], 2: jnp.pad(rows_of_ct(c_w_in), ((0, C_ROWS - 1284), (0, 0))), 3: a_w_in[1]}
    order = [("kv", 0, w_mem_kv), ("out", 0, w_out[0])]
    for i in range(1, DEPTH):
        order += [("in", i, w_in[i]), ("out", i, w_out[i])]
    later, sent, started = {}, {}, 0.0 * ga0[0, 0, 0].astype(F32)
    for kind, i, a in order:
        later[kind, i] = (a + started).astype(BF16)
        sent[kind, i] = _send_shard_start(later[kind, i], f"send_w_{kind}_{i}")
        started = sent[kind, i][4][0, 0]

    def arrived(k, after):
        return with_mine(_send_shard_wait(*sent[k][:4], after, f"wait_w_{k[0]}_{k[1]}"), later[k])

    def blocks(i, after):
        def wo_after(later_value):
            return arrived(("out", i), later_value).reshape(D_CAT, D)

        if i == 0:
            return [ga0[0], ga0[1]], [ga0[2], ga0[3]], wo_after
        got = arrived(("in", i), after)
        if i == 1:
            return [got[0], got[1], got[2][:, :512]], [got[2][:, 512:], got[3]], wo_after
        if i == 3:
            return [got[0], got[1]], [got[2], got[3]], wo_after
        fct = got[:, :1284].reshape(5136, D)
        c_ab = jnp.concatenate([fct[3 * D:3 * D + 16], jnp.zeros((AB_PAD - 16, D), BF16)], axis=0)
        return [fct[:3 * D], c_ab], [fct[3 * D + 16:]], wo_after
    sm = {"mem_norm_w": mem_norm_w, "norm_pre": norm_pre + started, "norm_post": norm_post, "a_w_s": a_w_s, "a_b_s": a_b_s,
          "c_a_log": c_a_log, "c_dt_bias": c_dt_bias, "c_o_norm_w": c_o_norm_w,
          "a_ln_w": gv[:, 0:512].reshape(4, 2, 256).transpose(1, 0, 2).reshape(2, D),
          "a_ln_b": gv[:, 512:1024].reshape(4, 2, 256).transpose(1, 0, 2).reshape(2, D),
          "b_conv_w": gv[:, 1024:1792].reshape(4, 1, 3, 256).transpose(1, 2, 0, 3).reshape(1, 3, D),
          "c_conv_w": gv[:, 1792:4864].reshape(4, 1, 4, 768).transpose(1, 2, 0, 3).reshape(1, 4, 3 * D)}
    wts = {"wkv_after": lambda after: arrived(("kv", 0), after).reshape(D, 2 * D_XA), "blocks": blocks}

    def layer_grads(i, g):
        if i % 3 == 2:
            gct = jnp.concatenate([g["wm"][i][:3 * D + 16], g["wg"][i]], axis=0).reshape(4, 1284, D)
            w_in = jnp.pad(gct, ((0, 0), (0, C_ROWS - 1284), (0, 0)))
        else:
            w_in = jnp.concatenate([g["wm"][i], g["wg"][i]], axis=0).reshape(4, -1, GRAD_TILE[i % 3])
        out = {f"in{i}": w_in, f"out{i}": g["wo"][i].reshape(4, 384, D)}
        return {k: a.reshape(4, 2, a.shape[1] // 2, a.shape[2]) for k, a in out.items()}

    pending = {}

    def layer_done(i, g):
        halves = layer_grads(i, g)
        started = _send_pieces_start([h.astype(BF16) for h in halves.values()], f"send_grads_{i}")
        pending[i] = (started, halves)
        return started[-1][0, 0]

    wts["layer_done"] = layer_done

    loss, dx, g = _local_step(x[0], mem[0], loss_target[0], wts, sm)
    loss = lax.psum(loss, ("x", "y", "c"))

    core = lax.axis_index("c")
    mine, specs = {}, {}
    for i in reversed(range(DEPTH)):
        started, halves = pending[i]
        lands = _send_pieces_wait(started, dx, f"wait_grads_{i}")
        for (k, h), land in zip(halves.items(), lands):
            own = lax.dynamic_index_in_dim(lax.dynamic_index_in_dim(h, chip, 0, False), core, 0, False)
            specs[k] = (own.shape[0], _REDUCE_CHUNK.get(own.shape[0], own.shape[0]))
            mine[k] = _sum8_class(own, land, specs[k][1], f"sum8_{k}")

    first = {"kv": g["wkv"].reshape(4, 2, 128, D)}
    first_specs = [(h.shape[2], _REDUCE_CHUNK.get(h.shape[2], h.shape[2])) for h in first.values()]
    halves = list(first.values())
    gs = {"mem_norm_w": g["mem_norm_w"], "norm_pre": jnp.concatenate(g["norm_pre"]),
          "norm_post": jnp.concatenate(g["norm_post"])}
    for n in _SMALL[3:]:
        gs[n] = jnp.stack([g[n][j] for j in sorted(g[n])])
    flat = jnp.concatenate([gs[n].reshape(-1) for n in _SMALL])
    small = jnp.pad(flat, (0, _SMALL_ROWS * D - flat.shape[0])).reshape(_SMALL_ROWS, D)
    *others, other_small = _swap_classes(halves, first_specs, small)
    pairs = [_pair_sum_class(h, o, s[1], f"pair_sum_{k}") for k, h, o, s in zip(first, halves, others, first_specs)]
    pair_small = _add(small, other_small, "pair_sum_small")
    *lands, land_small = _exchange_classes([p[0] for p in pairs], first_specs, pair_small)
    for k, p, land, s in zip(first, pairs, lands, first_specs):
        mine[k], specs[k] = _chip_sum_class(p[1], land, s[1], f"chip_sum_{k}"), s
    theirs = _share_classes(list(mine.values()), [specs[k] for k in mine])
    south = core == 0
    sh = {k: jnp.concatenate([jnp.where(south, a, b), jnp.where(south, b, a)], axis=0)
          for (k, a), b in zip(mine.items(), theirs)}
    grads = {"a_w_in": jnp.stack([sh["in0"], sh["in3"]]),
             "b_w_in": sh["in1"].reshape(5, D, 256).transpose(1, 0, 2).reshape(b_w_in.shape),
             "c_w_in": sh["in2"][:1284], "w_out": jnp.stack([sh[f"out{i}"] for i in range(DEPTH)]),
             "w_mem_kv": sh["kv"]}
    flat = _sum4(pair_small, land_small).reshape(-1)
    off = 0
    for n in _SMALL:
        shape = _SMALL_SHAPES[n]
        full = flat[off:off + _size(shape)].reshape(shape)
        off += _size(shape)
        if n in _SHARDED_SMALL:
            full = lax.dynamic_slice_in_dim(full, chip * _SHARDED_SMALL[n], _SHARDED_SMALL[n], axis=len(shape) - 1)
        grads[n] = full

    delta, new_m, new_v = {}, {}, {}
    for n in names:
        shape = w[n].shape
        if n == "c_w_in":
            d_, m_, v_ = _adamw(rows_of_ct(w[n]), grads[n], rows_of_ct(m[n]), rows_of_ct(v[n]), f"adamw_{n}")
            delta[n], new_m[n], new_v[n], grads[n] = d_.T[None], m_.T[None], v_.T[None], grads[n].T[None]
            continue
        view = (1, shape[0]) if len(shape) == 1 else (_size(shape[:-1]), shape[-1])
        d_, m_, v_ = _adamw(w[n].reshape(view), grads[n].reshape(view), m[n].reshape(view), v[n].reshape(view),
                            f"adamw_{n}")
        delta[n], new_m[n], new_v[n] = d_.reshape(shape), m_.reshape(shape), v_.reshape(shape)
    return (loss, dx[None], *[grads[n].reshape(w[n].shape) for n in names], *[delta[n] for n in names],
            *[new_m[n] for n in names], *[new_v[n] for n in names])
```

```python
import functools

import jax
import jax.numpy as jnp
from jax import lax
from jax.experimental import pallas as pl
from jax.experimental.pallas import tpu as pltpu

F32 = jnp.float32
BF16 = jnp.bfloat16
HI = lax.Precision.HIGHEST
MESH = pl.DeviceIdType.MESH
SDS = jax.ShapeDtypeStruct

D = 1024
D_XA = 512
D_CAT = 1536
N_MEM = 256
HD = 128
DEPTH = 4
EPS = 1e-6
TT = 512
DN_C = 64
DN_TB = 256
HALO = 8
AB_PAD = 128
VMEM_LIMIT = 56 * 1024 * 1024
GRAD_TILE = {0: 1024, 1: 256}

ADAM_LR, ADAM_B1, ADAM_B2, ADAM_EPS, ADAM_WD, ADAM_STEP = 0.001, 0.9, 0.999, 1e-08, 0.01, 10


def _params(n_grid, vmem=None):
    return pltpu.CompilerParams(dimension_semantics=("arbitrary",) * n_grid, vmem_limit_bytes=vmem)


def _rms(x, w):
    return x * lax.rsqrt(jnp.mean(x * x, axis=-1, keepdims=True) + EPS) * w


def _dot_nn(a, b):
    return jnp.dot(a.astype(BF16), b.astype(BF16), preferred_element_type=F32)


def _dot_nt(a, b):
    return lax.dot_general(a.astype(BF16), b.astype(BF16), (((1,), (1,)), ((), ())), preferred_element_type=F32)


def _dot_tn(a, b):
    return lax.dot_general(a.astype(BF16), b.astype(BF16), (((0,), (0,)), ((), ())), preferred_element_type=F32)


@jax.custom_vjp
def mm(a, b):
    return _dot_nn(a, b)


mm.defvjp(lambda a, b: (_dot_nn(a, b), (a, b)), lambda r, g: (_dot_nt(g, r[1]), _dot_tn(r[0], g)))


@jax.custom_vjp
def mm_nt(a, b):
    return _dot_nt(a, b)


mm_nt.defvjp(lambda a, b: (_dot_nt(a, b), (a, b)), lambda r, g: (_dot_nn(g, r[1]), _dot_tn(g, r[0])))


def _row_spec(width, tile=TT):
    return pl.BlockSpec((tile, width), lambda i: (i, 0))


def _full_spec(shape):
    return pl.BlockSpec(shape, lambda *_: (0,) * len(shape))


def _widths(blocks, transposed):
    return [b.shape[0 if transposed else 1] for b in blocks]


def _inproj_fwd(x, nw, mix, gate, transposed, name):
    T, nm = x.shape[0], len(mix)
    M, G = sum(_widths(mix, transposed)), sum(_widths(gate, transposed))

    def body(x_ref, nw_ref, *refs):
        blocks, (pm_ref, pg_ref, h_ref) = refs[:-3], refs[-3:]
        h = _rms(x_ref[...], nw_ref[...]).astype(BF16)
        h_ref[...] = h
        for p_ref, group in ((pm_ref, blocks[:nm]), (pg_ref, blocks[nm:])):
            off = 0
            for w_ref in group:
                w = w_ref.shape[0 if transposed else 1]
                p_ref[:, off:off + w] = _dot_nt(h, w_ref[...]) if transposed else _dot_nn(h, w_ref[...])
                off += w

    return pl.pallas_call(
        body, name=name, grid=(T // TT,),
        in_specs=[_row_spec(D), _full_spec((1, D))] + [_full_spec(b.shape) for b in mix + gate],
        out_specs=[_row_spec(M), _row_spec(G), _row_spec(D)],
        out_shape=[SDS((T, M), F32), SDS((T, G), F32), SDS((T, D), BF16)],
        compiler_params=_params(1, VMEM_LIMIT))(x, nw, *mix, *gate)


def _inproj_bwd(dpm, dpg, x, nw, mix, gate, transposed, dxc, name):
    T, nm = x.shape[0], len(mix)
    M, G = sum(_widths(mix, transposed)), sum(_widths(gate, transposed))

    def body(dpm_ref, dpg_ref, x_ref, nw_ref, *refs):
        blocks, (dxc_ref, dx_ref, dnw_ref) = refs[:-3], refs[-3:]
        dh = None
        for dp_ref, group in ((dpm_ref, blocks[:nm]), (dpg_ref, blocks[nm:])):
            off = 0
            for w_ref in group:
                w = w_ref.shape[0 if transposed else 1]
                dp = dp_ref[:, off:off + w]
                part = _dot_nn(dp, w_ref[...]) if transposed else _dot_nt(dp, w_ref[...])
                dh = part if dh is None else dh + part
                off += w
        _, vjp = jax.vjp(_rms, x_ref[...], nw_ref[...])
        dxr, dnw = vjp(dh)
        dx_ref[...] = dxc_ref[...] + dxr

        @pl.when(pl.program_id(0) == 0)
        def _():
            dnw_ref[...] = jnp.zeros_like(dnw_ref)
        dnw_ref[...] += dnw

    return pl.pallas_call(
        body, name=name, grid=(T // TT,),
        in_specs=[_row_spec(M), _row_spec(G), _row_spec(D), _full_spec((1, D))]
        + [_full_spec(b.shape) for b in mix + gate] + [_row_spec(D)],
        out_specs=[_row_spec(D), _full_spec((1, D))],
        out_shape=[SDS((T, D), F32), SDS((1, D), F32)],
        compiler_params=_params(1, VMEM_LIMIT))(dpm, dpg, x, nw, *mix, *gate, dxc)


def _matmul_tn(a, b, name, sub=None):
    T, K = a.shape
    N = b.shape[1]
    tn = 1024 if N % 1024 == 0 else (640 if N % 640 == 0 else N)
    tt = min(1024, T)
    n_sub = 1 if sub is None else tn // sub

    def body(a_ref, b_ref, o_ref):
        @pl.when(pl.program_id(1) == 0)
        def _():
            o_ref[...] = jnp.zeros_like(o_ref)
        res = _dot_tn(a_ref[...], b_ref[...])
        if sub is None:
            o_ref[...] += res
        else:
            for i in range(n_sub):
                o_ref[i] += res[:, i * sub:(i + 1) * sub]

    if sub is None:
        out_spec, out_shape = pl.BlockSpec((K, tn), lambda j, t: (0, j)), SDS((K, N), F32)
    else:
        out_spec, out_shape = pl.BlockSpec((n_sub, K, sub), lambda j, t: (j, 0, 0)), SDS((N // sub, K, sub), F32)
    return pl.pallas_call(
        body, name=name, grid=(N // tn, T // tt),
        in_specs=[pl.BlockSpec((tt, K), lambda j, t: (t, 0)), pl.BlockSpec((tt, tn), lambda j, t: (t, j))],
        out_specs=out_spec, out_shape=out_shape,
        compiler_params=_params(2, VMEM_LIMIT))(a, b)


def _memkv_fn(mem, w, wkv):
    return mm(_rms(mem, w), wkv)


def _memkv_fwd(mem, w, wkv):
    def body(mem_ref, w_ref, wkv_ref, kv_ref):
        kv_ref[...] = _memkv_fn(mem_ref[...], w_ref[...], wkv_ref[...])

    return pl.pallas_call(body, name="memkv_fwd", out_shape=SDS((N_MEM, 2 * D_XA), F32),
                          compiler_params=_params(0, VMEM_LIMIT))(mem, w, wkv)


def _memkv_bwd(mem, w, wkv, dkv):
    def body(mem_ref, w_ref, wkv_ref, dkv_ref, dw_ref, dwkv_ref):
        _, vjp = jax.vjp(functools.partial(_memkv_fn, mem_ref[...]), w_ref[...], wkv_ref[...].astype(F32))
        dw, dwkv = vjp(dkv_ref[...])
        dw_ref[...] = dw
        dwkv_ref[...] = dwkv

    return pl.pallas_call(body, name="memkv_bwd", out_shape=[SDS((1, D), F32), SDS((D, 2 * D_XA), F32)],
                          compiler_params=_params(0, VMEM_LIMIT))(mem, w, wkv, dkv)


def _attn_gate(ymix, qx, z, *kvs):
    outs = []
    for j in range(4):
        s = mm_nt(qx[:, j * HD:(j + 1) * HD], kvs[j]) * (HD ** -0.5)
        e = jnp.exp(s - lax.stop_gradient(jnp.max(s, axis=-1, keepdims=True)))
        outs.append(mm(e / jnp.sum(e, axis=-1, keepdims=True), kvs[4 + j]))
    return jnp.concatenate([ymix] + outs, axis=1) * jax.nn.silu(z)


def _kv_blocks(kv_ref):
    return [kv_ref[:, j * HD:(j + 1) * HD] for j in range(8)]


def _gate_outproj_fwd(ymix, pg, kv, wo, x, nw, target, name):
    T = x.shape[0]
    last = target is not None

    def body(ymix_ref, pg_ref, kv_ref, wo_ref, x_ref, nw_ref, *refs):
        ycat = _attn_gate(ymix_ref[...], pg_ref[:, :D_XA], pg_ref[:, D_XA:], *_kv_blocks(kv_ref)).astype(BF16)
        o = jnp.dot(ycat, wo_ref[...], preferred_element_type=F32)
        y = x_ref[...] + _rms(o, nw_ref[...])
        if not last:
            ycat_ref, o_ref, y_ref = refs
            y_ref[...] = y
        else:
            t_ref, ycat_ref, o_ref, loss_ref, dy_ref = refs
            err = y - t_ref[...]
            dy_ref[...] = err * (1.0 / D)

            @pl.when(pl.program_id(0) == 0)
            def _():
                loss_ref[...] = jnp.zeros_like(loss_ref)
            part = jnp.sum(jnp.sum(err * err, axis=1, keepdims=True), axis=0, keepdims=True) * (0.5 / D)
            loss_ref[...] += jnp.broadcast_to(part, loss_ref.shape)
        ycat_ref[...] = ycat
        o_ref[...] = o

    in_specs = [_row_spec(D), _row_spec(D_XA + D_CAT), _full_spec((N_MEM, 2 * D_XA)), _full_spec((D_CAT, D)),
                _row_spec(D), _full_spec((1, D))]
    out_specs = [_row_spec(D_CAT), _row_spec(D)]
    out_shape = [SDS((T, D_CAT), BF16), SDS((T, D), F32)]
    if last:
        in_specs.append(_row_spec(D))
        out_specs += [_full_spec((8, 128)), _row_spec(D)]
        out_shape += [SDS((8, 128), F32), SDS((T, D), F32)]
    else:
        out_specs.append(_row_spec(D))
        out_shape.append(SDS((T, D), F32))
    args = (ymix, pg, kv, wo, x, nw) + ((target,) if last else ())
    return pl.pallas_call(body, name=name, grid=(T // TT,), in_specs=in_specs, out_specs=out_specs,
                          out_shape=out_shape, compiler_params=_params(1, VMEM_LIMIT))(*args)


def _outproj_gate_bwd(dxo, o, nw, wo, ymix, pg, kv, dkv_in, name, gmlp=None):
    T = dxo.shape[0]
    G = D_XA + D_CAT
    tile = TT if gmlp is None else TT // 2
    n_t = T // tile
    rows = functools.partial(_row_spec, tile=tile)

    def body(dxo_ref, o_ref, nw_ref, wo_ref, ymix_ref, pg_ref, kv_ref, dkvin_ref, *refs):
        if gmlp is None:
            dobf_ref, dnw_ref, dymix_ref, dpg_ref, dkv_ref = refs
        else:
            mixer_in, (dobf_ref, dnw_ref, dpg_ref, dkv_ref), rest = refs[:5], refs[5:9], refs[9:]
            mixer_out, (dymix_ref, *mixer_scr) = rest[:5], rest[5:]
        _, vjp = jax.vjp(_rms, o_ref[...], nw_ref[...])
        do, dnw = vjp(dxo_ref[...])
        dobf = do.astype(BF16)
        dobf_ref[...] = dobf
        dycat = _dot_nt(dobf, wo_ref[...])
        _, vjp = jax.vjp(_attn_gate, ymix_ref[...], pg_ref[:, :D_XA], pg_ref[:, D_XA:], *_kv_blocks(kv_ref))
        g = vjp(dycat)
        dymix_ref[...] = g[0]
        dpg_ref[:, :D_XA] = g[1].astype(BF16)
        dpg_ref[:, D_XA:] = g[2].astype(BF16)

        @pl.when(pl.program_id(0) == 0)
        def _():
            dnw_ref[...] = jnp.zeros_like(dnw_ref)
            dkv_ref[...] = dkvin_ref[...]
        dnw_ref[...] += dnw
        for j in range(8):
            dkv_ref[:, j * HD:(j + 1) * HD] += g[3 + j]
        if gmlp is not None:
            _gmlp_bwd_tile(n_t, dymix_ref, *mixer_in, *mixer_out, *mixer_scr)

    kv_spec, vec, grp = _full_spec((N_MEM, 2 * D_XA)), _full_spec((1, D)), _full_spec((8, HD, HD))
    in_specs = [rows(D), rows(D), vec, _full_spec((D_CAT, D)), rows(D), rows(G), kv_spec, kv_spec]
    out_specs = [rows(D), vec, rows(G), kv_spec]
    out_shape = [SDS((T, D), BF16), SDS((1, D), F32), SDS((T, G), BF16), SDS((N_MEM, 2 * D_XA), F32)]
    if gmlp is None:
        out_specs.insert(2, rows(D))
        out_shape.insert(2, SDS((T, D), F32))
        scratch, args = [], ()
    else:
        in_specs += [rows(2 * D), vec, vec, grp, grp]
        out_specs += [rows(2 * D), vec, vec, grp, grp]
        out_shape += [SDS((T, 2 * D), BF16), SDS((1, D), F32), SDS((1, D), F32), SDS((8, HD, HD), F32),
                      SDS((8, HD, HD), F32)]
        scratch = [pltpu.VMEM((tile, D), F32), pltpu.VMEM((tile, D), F32), pltpu.VMEM((tile, D), F32),
                   pltpu.VMEM((8, HD, HD), F32)]
        args = tuple(gmlp)
    return pl.pallas_call(
        body, name=name, grid=(n_t,), in_specs=in_specs, out_specs=out_specs, out_shape=out_shape,
        scratch_shapes=scratch, compiler_params=_params(1, VMEM_LIMIT))(dxo, o, nw, wo, ymix, pg, kv, dkv_in, *args)


def _gmlp_pre(u, v, lnw, lnb):
    vg = jax.nn.gelu(v)
    xc = vg - jnp.mean(vg, axis=-1, keepdims=True)
    vl = xc * lax.rsqrt(jnp.mean(xc * xc, axis=-1, keepdims=True) + EPS) * lnw + lnb
    return jax.nn.gelu(u), vl


def _tril(n, strict=False):
    r = lax.broadcasted_iota(jnp.int32, (n, n), 0)
    c = lax.broadcasted_iota(jnp.int32, (n, n), 1)
    return (r > c) if strict else (r >= c)


def _gmlp_fwd(pm, lnw, lnb, ws, bs3, name):
    T = pm.shape[0]

    def body(pm_ref, lnw_ref, lnb_ref, ws_ref, bs_ref, y_ref):
        ug, vl = _gmlp_pre(pm_ref[:, :D], pm_ref[:, D:], lnw_ref[...], lnb_ref[...])
        mask = _tril(HD)
        for g in range(8):
            w = jnp.where(mask, ws_ref[g], 0.0)
            for c in range(TT // HD):
                rows, cols = slice(c * HD, (c + 1) * HD), slice(g * HD, (g + 1) * HD)
                y_ref[rows, cols] = ug[rows, cols] * (_dot_nn(w, vl[rows, cols]) + bs_ref[g])

    return pl.pallas_call(
        body, name=name, grid=(T // TT,),
        in_specs=[_row_spec(2 * D), _full_spec((1, D)), _full_spec((1, D)), _full_spec((8, HD, HD)),
                  _full_spec((8, HD, HD))],
        out_specs=_row_spec(D), out_shape=SDS((T, D), F32),
        compiler_params=_params(1, VMEM_LIMIT))(pm, lnw, lnb, ws, bs3)


def _gmlp_bwd_tile(n_t, dy_ref, pm_ref, lnw_ref, lnb_ref, ws_ref, bs_ref, dpm_ref, dlnw_ref, dlnb_ref, dws_ref,
                   dbs_ref, dug_scr, dvl_scr, dbs_scr):
    i = pl.program_id(0)

    @pl.when(i == 0)
    def _():
        dlnw_ref[...] = jnp.zeros_like(dlnw_ref)
        dlnb_ref[...] = jnp.zeros_like(dlnb_ref)
        dws_ref[...] = jnp.zeros_like(dws_ref)
        dbs_scr[...] = jnp.zeros_like(dbs_scr)

    (ug, vl), vjp = jax.vjp(_gmlp_pre, pm_ref[:, :D], pm_ref[:, D:], lnw_ref[...], lnb_ref[...])
    mask = _tril(HD)
    for g in range(8):
        w = jnp.where(mask, ws_ref[g], 0.0)
        dw = jnp.zeros((HD, HD), F32)
        db = jnp.zeros((HD, HD), F32)
        for c in range(dy_ref.shape[0] // HD):
            rows, cols = slice(c * HD, (c + 1) * HD), slice(g * HD, (g + 1) * HD)
            dyb, vlb = dy_ref[rows, cols], vl[rows, cols]
            sp = _dot_nn(w, vlb) + bs_ref[g]
            dsp = dyb * ug[rows, cols]
            dug_scr[rows, cols] = dyb * sp
            dvl_scr[rows, cols] = _dot_tn(w, dsp)
            dw += _dot_nt(dsp, vlb)
            db += dsp
        dws_ref[g] += jnp.where(mask, dw, 0.0)
        dbs_scr[g] += db
    du, dv, dlnw, dlnb = vjp((dug_scr[...], dvl_scr[...]))
    dpm_ref[:, :D] = du.astype(BF16)
    dpm_ref[:, D:] = dv.astype(BF16)
    dlnw_ref[...] += dlnw
    dlnb_ref[...] += dlnb

    @pl.when(i == n_t - 1)
    def _():
        for g in range(8):
            dbs_ref[g] = jnp.broadcast_to(jnp.sum(dbs_scr[g], axis=1, keepdims=True), (HD, HD))


def _prev_spec(width, T):
    return pl.BlockSpec((HALO, width), lambda i: (jnp.maximum(i * (TT // HALO) - 1, 0), 0))


def _next_spec(width, T):
    return pl.BlockSpec((HALO, width), lambda i: (jnp.minimum((i + 1) * (TT // HALO), T // HALO - 1), 0))


def _rows_before(ext, j):
    return ext[HALO:] if j == 0 else pltpu.roll(ext, j, 0)[HALO:]


def _rows_after(ext, j):
    n = ext.shape[0]
    return ext[:n - HALO] if j == 0 else pltpu.roll(ext, n - j, 0)[:n - HALO]


def _conv_apply(ext_s, w):
    K = w.shape[0]
    y = _rows_before(ext_s, K - 1) * w[0:1]
    for k in range(1, K):
        y = y + _rows_before(ext_s, K - 1 - k) * w[k:k + 1]
    return y


def _conv_grads(ext_s, ext_dy, w):
    K = w.shape[0]
    dy = ext_dy[:ext_dy.shape[0] - HALO]
    ds = _rows_after(ext_dy, K - 1) * w[0:1]
    dws = [jnp.sum(dy * _rows_before(ext_s, K - 1), axis=0, keepdims=True)]
    for k in range(1, K):
        ds = ds + _rows_after(ext_dy, K - 1 - k) * w[k:k + 1]
        dws.append(jnp.sum(dy * _rows_before(ext_s, K - 1 - k), axis=0, keepdims=True))
    return ds, jnp.concatenate(dws, axis=0)


def _sconv_fwd(pm, w, name):
    T = pm.shape[0]

    def body(pm_ref, prev_ref, w_ref, y_ref):
        s = pm_ref[:, D:2 * D] * pm_ref[:, 2 * D:]
        sp = jnp.where(pl.program_id(0) > 0, prev_ref[:, D:2 * D] * prev_ref[:, 2 * D:], 0.0)
        y_ref[...] = pm_ref[:, :D] * _conv_apply(jnp.concatenate([sp, s], axis=0), w_ref[...])

    return pl.pallas_call(
        body, name=name, grid=(T // TT,),
        in_specs=[_row_spec(3 * D), _prev_spec(3 * D, T), _full_spec((3, D))],
        out_specs=_row_spec(D), out_shape=SDS((T, D), F32),
        compiler_params=_params(1, VMEM_LIMIT))(pm, pm, w)


def _sconv_bwd(dy, pm, w, name):
    T = pm.shape[0]
    n_t = T // TT

    def body(dy_ref, dyn_ref, pm_ref, prev_ref, next_ref, w_ref, dpm_ref, dw_ref):
        i = pl.program_id(0)
        bg, cg, hv = pm_ref[:, :D], pm_ref[:, D:2 * D], pm_ref[:, 2 * D:]
        sp = jnp.where(i > 0, prev_ref[:, D:2 * D] * prev_ref[:, 2 * D:], 0.0)
        ext_s = jnp.concatenate([sp, cg * hv], axis=0)
        dyv = dy_ref[...]
        dcn = jnp.where(i < n_t - 1, dyn_ref[...] * next_ref[:, :D], 0.0)
        ds, dw = _conv_grads(ext_s, jnp.concatenate([dyv * bg, dcn], axis=0), w_ref[...])
        dpm_ref[:, :D] = (dyv * _conv_apply(ext_s, w_ref[...])).astype(BF16)
        dpm_ref[:, D:2 * D] = (ds * hv).astype(BF16)
        dpm_ref[:, 2 * D:] = (ds * cg).astype(BF16)

        @pl.when(i == 0)
        def _():
            dw_ref[...] = jnp.zeros_like(dw_ref)
        dw_ref[...] += dw

    return pl.pallas_call(
        body, name=name, grid=(n_t,),
        in_specs=[_row_spec(D), _next_spec(D, T), _row_spec(3 * D), _prev_spec(3 * D, T), _next_spec(3 * D, T),
                  _full_spec((3, D))],
        out_specs=[_row_spec(3 * D), _full_spec((3, D))],
        out_shape=[SDS((T, 3 * D), BF16), SDS((3, D), F32)],
        compiler_params=_params(1, VMEM_LIMIT))(dy, dy, pm, pm, pm, w)


def _dnconv_fwd(pm, w, name):
    T = pm.shape[0]

    def body(pm_ref, prev_ref, w_ref, c_ref):
        sp = jnp.where(pl.program_id(0) > 0, prev_ref[...], 0.0)
        c_ref[...] = _conv_apply(jnp.concatenate([sp, pm_ref[...]], axis=0), w_ref[...])

    return pl.pallas_call(
        body, name=name, grid=(T // TT,),
        in_specs=[_row_spec(3 * D), _prev_spec(3 * D, T), _full_spec((4, 3 * D))],
        out_specs=_row_spec(3 * D), out_shape=SDS((T, 3 * D), F32),
        compiler_params=_params(1, VMEM_LIMIT))(pm, pm, w)


def _dnconv_bwd(dcq, dck, dcv, dab, pm, w, name):
    T = pm.shape[0]
    n_t = T // TT

    def body(dq_ref, dk_ref, dv_ref, dqn_ref, dkn_ref, dvn_ref, dab_ref, pm_ref, prev_ref, w_ref, dpm_ref, dw_ref):
        i = pl.program_id(0)
        sp = jnp.where(i > 0, prev_ref[...], 0.0)
        ext_s = jnp.concatenate([sp, pm_ref[...]], axis=0)
        own = jnp.concatenate([dq_ref[...], dk_ref[...], dv_ref[...]], axis=1)
        nxt = jnp.where(i < n_t - 1, jnp.concatenate([dqn_ref[...], dkn_ref[...], dvn_ref[...]], axis=1), 0.0)
        ds, dw = _conv_grads(ext_s, jnp.concatenate([own, nxt], axis=0), w_ref[...])
        dpm_ref[:, :3 * D] = ds.astype(BF16)
        dpm_ref[:, 3 * D:] = dab_ref[...].astype(BF16)

        @pl.when(i == 0)
        def _():
            dw_ref[...] = jnp.zeros_like(dw_ref)
        dw_ref[...] += dw

    return pl.pallas_call(
        body, name=name, grid=(n_t,),
        in_specs=[_row_spec(D), _row_spec(D), _row_spec(D), _next_spec(D, T), _next_spec(D, T), _next_spec(D, T),
                  _row_spec(AB_PAD), _row_spec(3 * D), _prev_spec(3 * D, T), _full_spec((4, 3 * D))],
        out_specs=[_row_spec(3 * D + AB_PAD), _full_spec((4, 3 * D))],
        out_shape=[SDS((T, 3 * D + AB_PAD), BF16), SDS((4, 3 * D), F32)],
        compiler_params=_params(1, VMEM_LIMIT))(dcq, dck, dcv, dcq, dck, dcv, dab, pm, pm, w)


def _l2n(x):
    return x * lax.rsqrt(jnp.sum(x * x, axis=-1, keepdims=True) + EPS)


def _softplus(x):
    return jnp.maximum(x, 0.0) + jnp.log1p(jnp.exp(-jnp.abs(x)))


_BNN = (((2,), (1,)), ((0,), (0,)))
_BNT = (((2,), (2,)), ((0,), (0,)))
_BTN = (((1,), (1,)), ((0,), (0,)))


def _bdot(a, b, dims):
    return lax.dot_general(a.astype(BF16), b.astype(BF16), dims, preferred_element_type=F32)


def _bdot3(a, b, dims):
    ah, bh = a.astype(BF16), b.astype(BF16)
    al, bl = (a - ah.astype(F32)).astype(BF16), (b - bh.astype(F32)).astype(BF16)
    d = functools.partial(lax.dot_general, dimension_numbers=dims, preferred_element_type=F32)
    return d(ah, bh) + (d(ah, bl) + d(al, bh))


def _bdot_hi(a, b, dims):
    return lax.dot_general(a, b, dims, precision=HI, preferred_element_type=F32)


def _batched_matmuls(dot):
    @jax.custom_vjp
    def nn(a, b):
        return dot(a, b, _BNN)

    @jax.custom_vjp
    def nt(a, b):
        return dot(a, b, _BNT)

    @jax.custom_vjp
    def tn(a, b):
        return dot(a, b, _BTN)

    nn.defvjp(lambda a, b: (dot(a, b, _BNN), (a, b)), lambda r, g: (dot(g, r[1], _BNT), dot(r[0], g, _BTN)))
    nt.defvjp(lambda a, b: (dot(a, b, _BNT), (a, b)), lambda r, g: (dot(g, r[1], _BNN), dot(g, r[0], _BTN)))
    tn.defvjp(lambda a, b: (dot(a, b, _BTN), (a, b)), lambda r, g: (dot(r[1], g, _BNT), dot(r[0], g, _BNN)))
    return nn, nt, tn


bmm, bmm_nt, bmm_tn = _batched_matmuls(_bdot)
bmm_hi, _, _ = _batched_matmuls(_bdot_hi)

@jax.custom_vjp
def _neumann_inverse(n):
    C = n.shape[1]
    eye = lax.broadcasted_iota(jnp.int32, n.shape, 1) == lax.broadcasted_iota(jnp.int32, n.shape, 2)
    t = eye.astype(F32) + n
    for _ in range(5):
        n = _bdot3(n, n, _BNN)
        t = t + _bdot3(t, n, _BNN)
    return t


def _neumann_inverse_fwd(n):
    t = _neumann_inverse(n)
    return t, t


def _neumann_inverse_bwd(t, g):
    return (_bdot3(_bdot3(t, g, _BTN), t, _BNT),)


_neumann_inverse.defvjp(_neumann_inverse_fwd, _neumann_inverse_bwd)


@jax.custom_vjp
def _saved_inverse(n, t):
    return t


_saved_inverse.defvjp(lambda n, t: (t, t), lambda t, g: (_bdot3(_bdot3(t, g, _BTN), t, _BNT), jnp.zeros_like(t)))

DN_NCH = DN_TB // DN_C
DN_NH = 4


def _decay_terms(ab, alog, dtb, first_head, n_heads):
    C = DN_C
    lane = lax.broadcasted_iota(jnp.int32, ab.shape, 1)
    g_all = (-jnp.exp(alog) * _softplus(ab + dtb)).reshape(DN_NCH, C, HD)
    beta_all = jax.nn.sigmoid(ab)
    r = lax.broadcasted_iota(jnp.int32, (DN_NCH, C, C), 1)
    c = lax.broadcasted_iota(jnp.int32, (DN_NCH, C, C), 2)
    gc_all = bmm_hi((r >= c).astype(F32), g_all)
    gc_rows = [gc_all[i].T for i in range(DN_NCH)]
    lane3 = lax.broadcasted_iota(jnp.int32, (DN_NCH, C, HD), 2)
    row = lax.broadcasted_iota(jnp.int32, (HD, C), 0)
    ones = jnp.ones((1, HD), F32)
    gcs, gjs, betas = [], [], []
    for i in range(n_heads):
        h = first_head + i
        gcs.append(jnp.sum(jnp.where(lane3 == h, gc_all, 0.0), axis=2, keepdims=True) * ones)
        gjs.append(jnp.concatenate(
            [jnp.broadcast_to(jnp.sum(jnp.where(row == h, t, 0.0), axis=0, keepdims=True), (C, C))[None] for t in gc_rows],
            axis=0))
        beta = jnp.sum(jnp.where(lane == 8 + h, beta_all, 0.0), axis=1, keepdims=True) * ones
        betas.append(beta.reshape(DN_NCH, C, HD))
    return jnp.concatenate(gcs, axis=0), jnp.concatenate(gjs, axis=0), jnp.concatenate(betas, axis=0)


def _dn_prep(cq, ck, cv, gcum, gj, bb, t_saved=None):
    B, C = cq.shape[0], DN_C
    q = _l2n(jax.nn.silu(cq)) * (HD ** -0.5)
    k = _l2n(jax.nn.silu(ck))
    v = jax.nn.silu(cv)
    r = lax.broadcasted_iota(jnp.int32, (B, C, C), 1)
    c = lax.broadcasted_iota(jnp.int32, (B, C, C), 2)
    incl, strict = r >= c, r > c
    decay = jnp.where(incl, jnp.exp(jnp.where(incl, gcum[:, :, :C] - gj, 0.0)), 0.0)
    kb = k * bb
    n_mat = -jnp.where(strict, bmm_nt(kb, k) * decay, 0.0)
    t_mat = _neumann_inverse(n_mat) if t_saved is None else _saved_inverse(n_mat, t_saved)
    eg = jnp.exp(gcum)
    glast = gcum[:, C - 1:C, :]
    return (bmm(t_mat, v * bb), bmm(t_mat, kb * eg), bmm_nt(q, k) * decay, q * eg, k * jnp.exp(glast - gcum),
            jnp.exp(glast), t_mat)


def _dn_scan_step(u, w, qk, qd, kd, egl, S, onw):
    v_new = u - bmm(w, S)
    o = bmm(qd, S) + bmm(qk, v_new)
    return _rms(o, onw), S * egl + bmm_tn(kd, v_new)


def _to_batch(ref, n_heads):
    return jnp.concatenate([ref[:, i * HD:(i + 1) * HD].astype(F32).reshape(DN_NCH, DN_C, HD) for i in range(n_heads)],
                           axis=0)


def _from_batch(ref, val, n_heads):
    for i in range(n_heads):
        ref[:, i * HD:(i + 1) * HD] = val[i * DN_NCH:(i + 1) * DN_NCH].reshape(DN_TB, HD).astype(ref.dtype)


def _prep_specs(T, rev):
    nb = T // DN_TB
    blk = (lambda n: nb - 1 - n) if rev else (lambda n: n)
    ng = 8 // DN_NH
    head = [pl.BlockSpec((DN_TB, DN_NH * HD), functools.partial(lambda n, h, off: (blk(n), off + h), off=ng * s))
            for s in range(3)]
    ab = pl.BlockSpec((DN_TB, AB_PAD), lambda n, h: (blk(n), 3 * D // AB_PAD))
    row = pl.BlockSpec((1, HD), lambda n, h: (0, 0))
    wide = pl.BlockSpec((DN_TB, DN_NH * HD), lambda n, h: (blk(n), h))
    qk = pl.BlockSpec((DN_NCH, DN_NH, DN_C, DN_C), lambda n, h: (blk(n), h, 0, 0))
    eg = pl.BlockSpec((DN_NCH, DN_NH, 1, HD), lambda n, h: (blk(n), h, 0, 0))
    return nb, ng, head, ab, row, wide, qk, eg


def _dn_prep_fwd(cpre, pm, alog, dtb, name):
    T = cpre.shape[0]
    nb, ng, head, ab, row, wide, qks, egs = _prep_specs(T, False)

    def body(cq_ref, ck_ref, cv_ref, ab_ref, alog_ref, dtb_ref, u_ref, w_ref, qk_ref, qd_ref, kd_ref, e_ref, t_ref):
        gcum, gj, bb = _decay_terms(ab_ref[...], alog_ref[...], dtb_ref[...], pl.program_id(1) * DN_NH, DN_NH)
        u, w, qk, qd, kd, egl, t_mat = _dn_prep(_to_batch(cq_ref, DN_NH), _to_batch(ck_ref, DN_NH),
                                                _to_batch(cv_ref, DN_NH), gcum, gj, bb)
        _from_batch(u_ref, u, DN_NH)
        _from_batch(w_ref, w, DN_NH)
        _from_batch(qd_ref, qd, DN_NH)
        _from_batch(kd_ref, kd, DN_NH)
        for i in range(DN_NH):
            qk_ref[:, i] = qk[i * DN_NCH:(i + 1) * DN_NCH].astype(BF16)
            e_ref[:, i] = egl[i * DN_NCH:(i + 1) * DN_NCH]
            t_ref[:, i] = t_mat[i * DN_NCH:(i + 1) * DN_NCH]

    return pl.pallas_call(
        body, name=name, grid=(nb, ng), in_specs=head + [ab, row, row],
        out_specs=[wide, wide, qks, wide, wide, egs, qks],
        out_shape=[SDS((T, D), F32), SDS((T, D), BF16), SDS((T // DN_C, 8, DN_C, DN_C), BF16), SDS((T, D), BF16),
                   SDS((T, D), BF16), SDS((T // DN_C, 8, 1, HD), F32), SDS((T // DN_C, 8, DN_C, DN_C), F32)],
        compiler_params=_params(2, VMEM_LIMIT))(cpre, cpre, cpre, pm, alog, dtb)


def _dn_prep_bwd(du, dw, dqk, dqd, dkd, degl, t_mat, cpre, pm, alog, dtb, name):
    T = cpre.shape[0]
    nb, ng, head, ab, row, wide, qks, egs = _prep_specs(T, True)

    def body(du_ref, dw_ref, dqk_ref, dqd_ref, dkd_ref, de_ref, t_ref, cq_ref, ck_ref, cv_ref, ab_ref, alog_ref,
             dtb_ref, dcq_ref, dck_ref, dcv_ref, dab_ref, dalog_ref, ddtb_ref):
        n, h = pl.program_id(0), pl.program_id(1)

        @pl.when((n == 0) & (h == 0))
        def _():
            dalog_ref[...] = jnp.zeros_like(dalog_ref)
            ddtb_ref[...] = jnp.zeros_like(ddtb_ref)

        @pl.when(h == 0)
        def _():
            dab_ref[...] = jnp.zeros_like(dab_ref)

        t_saved = jnp.concatenate([t_ref[:, i] for i in range(DN_NH)], axis=0)

        def fwd(cq, ck, cv, ab_v, alog_v, dtb_v):
            gcum, gj, bb = _decay_terms(ab_v, alog_v, dtb_v, h * DN_NH, DN_NH)
            return _dn_prep(cq, ck, cv, gcum, gj, bb, t_saved)[:6]

        _, vjp = jax.vjp(fwd, _to_batch(cq_ref, DN_NH), _to_batch(ck_ref, DN_NH), _to_batch(cv_ref, DN_NH), ab_ref[...],
                         alog_ref[...], dtb_ref[...])
        cot = (_to_batch(du_ref, DN_NH), _to_batch(dw_ref, DN_NH),
               jnp.concatenate([dqk_ref[:, i] for i in range(DN_NH)], axis=0), _to_batch(dqd_ref, DN_NH),
               _to_batch(dkd_ref, DN_NH), jnp.concatenate([de_ref[:, i] for i in range(DN_NH)], axis=0))
        dcq, dck, dcv, dab, dalog, ddtb = vjp(cot)
        _from_batch(dcq_ref, dcq, DN_NH)
        _from_batch(dck_ref, dck, DN_NH)
        _from_batch(dcv_ref, dcv, DN_NH)
        dab_ref[...] += dab
        dalog_ref[...] += dalog
        ddtb_ref[...] += ddtb

    dabspec = pl.BlockSpec((DN_TB, AB_PAD), lambda n, h: (nb - 1 - n, 0))
    return pl.pallas_call(
        body, name=name, grid=(nb, ng),
        in_specs=[wide, wide, qks, wide, wide, egs, qks] + head + [ab, row, row],
        out_specs=[wide, wide, wide, dabspec, row, row],
        out_shape=[SDS((T, D), F32)] * 3 + [SDS((T, AB_PAD), F32)] + [SDS((1, HD), F32)] * 2,
        compiler_params=_params(2, VMEM_LIMIT))(du, dw, dqk, dqd, dkd, degl, t_mat, cpre, cpre, cpre, pm, alog, dtb)


def _scan_specs(T, rev):
    nb = T // DN_TB
    blk = (lambda n: nb - 1 - n) if rev else (lambda n: n)
    wide = pl.BlockSpec((DN_TB, D), lambda n: (blk(n), 0))
    qk = pl.BlockSpec((DN_NCH, 8, DN_C, DN_C), lambda n: (blk(n), 0, 0, 0))
    eg = pl.BlockSpec((DN_NCH, 8, 1, HD), lambda n: (blk(n), 0, 0, 0))
    st = pl.BlockSpec((DN_NCH, 8, HD, HD), lambda n: (blk(n), 0, 0, 0))
    row = pl.BlockSpec((1, HD), lambda n: (0, 0))
    return nb, wide, qk, eg, st, row


def _heads_of(ref, rows):
    return jnp.concatenate([ref[rows, h * HD:(h + 1) * HD].astype(F32)[None] for h in range(8)], axis=0)


def _dn_scan_fwd(u, w, qk, qd, kd, egl, onw, name):
    T = u.shape[0]
    nb, wide, qks, egs, sts, row = _scan_specs(T, False)

    def body(u_ref, w_ref, qk_ref, qd_ref, kd_ref, e_ref, onw_ref, o_ref, st_ref, s_scr):
        @pl.when(pl.program_id(0) == 0)
        def _():
            s_scr[...] = jnp.zeros_like(s_scr)
        S = s_scr[...]
        for c in range(DN_NCH):
            rows = slice(c * DN_C, (c + 1) * DN_C)
            st_ref[c] = S
            o, S = _dn_scan_step(_heads_of(u_ref, rows), _heads_of(w_ref, rows), qk_ref[c].astype(F32),
                                 _heads_of(qd_ref, rows), _heads_of(kd_ref, rows), e_ref[c], S, onw_ref[...])
            for h in range(8):
                o_ref[rows, h * HD:(h + 1) * HD] = o[h]
        s_scr[...] = S

    return pl.pallas_call(
        body, name=name, grid=(nb,), in_specs=[wide, wide, qks, wide, wide, egs, row], out_specs=[wide, sts],
        out_shape=[SDS((T, D), F32), SDS((T // DN_C, 8, HD, HD), F32)],
        scratch_shapes=[pltpu.VMEM((8, HD, HD), F32)],
        compiler_params=_params(1, VMEM_LIMIT))(u, w, qk, qd, kd, egl, onw)


def _dn_scan_bwd(do, u, w, qk, qd, kd, egl, st, onw, name):
    T = u.shape[0]
    nb, wide, qks, egs, sts, row = _scan_specs(T, True)

    def body(do_ref, u_ref, w_ref, qk_ref, qd_ref, kd_ref, e_ref, st_ref, onw_ref,
             du_ref, dw_ref, dqk_ref, dqd_ref, dkd_ref, de_ref, donw_ref, ds_scr):
        @pl.when(pl.program_id(0) == 0)
        def _():
            ds_scr[...] = jnp.zeros_like(ds_scr)
            donw_ref[...] = jnp.zeros_like(donw_ref)
        dS = ds_scr[...]
        donw = jnp.zeros((1, HD), F32)
        for c in reversed(range(DN_NCH)):
            rows = slice(c * DN_C, (c + 1) * DN_C)
            _, vjp = jax.vjp(_dn_scan_step, _heads_of(u_ref, rows), _heads_of(w_ref, rows), qk_ref[c].astype(F32),
                             _heads_of(qd_ref, rows), _heads_of(kd_ref, rows), e_ref[c], st_ref[c], onw_ref[...])
            du, dw, dqk, dqd, dkd, de, dS, dn = vjp((_heads_of(do_ref, rows), dS))
            for h in range(8):
                cols = slice(h * HD, (h + 1) * HD)
                du_ref[rows, cols] = du[h]
                dw_ref[rows, cols] = dw[h]
                dqd_ref[rows, cols] = dqd[h]
                dkd_ref[rows, cols] = dkd[h]
            dqk_ref[c] = dqk
            de_ref[c] = de
            donw += dn
        ds_scr[...] = dS
        donw_ref[...] += donw

    return pl.pallas_call(
        body, name=name, grid=(nb,), in_specs=[wide, wide, wide, qks, wide, wide, egs, sts, row],
        out_specs=[wide, wide, qks, wide, wide, egs, row],
        out_shape=[SDS((T, D), F32), SDS((T, D), F32), SDS((T // DN_C, 8, DN_C, DN_C), F32), SDS((T, D), F32),
                   SDS((T, D), F32), SDS((T // DN_C, 8, 1, HD), F32), SDS((1, HD), F32)],
        scratch_shapes=[pltpu.VMEM((8, HD, HD), F32)],
        compiler_params=_params(1, VMEM_LIMIT))(do, u, w, qk, qd, kd, egl, st, onw)


def _adamw(w, g, m, v, name):
    R, C = w.shape
    tr = 256 if R % 256 == 0 and R > 256 else R
    tc = 256 if tr == R and R > 256 and C % 256 == 0 else C
    c1 = 1.0 - ADAM_B1 ** ADAM_STEP
    c2 = 1.0 - ADAM_B2 ** ADAM_STEP

    def body(w_ref, g_ref, m_ref, v_ref, d_ref, nm_ref, nv_ref):
        gv = g_ref[...]
        nm = ADAM_B1 * m_ref[...] + (1.0 - ADAM_B1) * gv
        nv = ADAM_B2 * v_ref[...] + (1.0 - ADAM_B2) * (gv * gv)
        nm_ref[...] = nm
        nv_ref[...] = nv
        d_ref[...] = -ADAM_LR * ((nm / c1) / (jnp.sqrt(nv / c2) + ADAM_EPS) + ADAM_WD * w_ref[...])

    spec = pl.BlockSpec((tr, tc), lambda i, j: (i, j))
    return pl.pallas_call(
        body, name=name, grid=(R // tr, C // tc), in_specs=[spec] * 4, out_specs=[spec] * 3,
        out_shape=[SDS((R, C), F32)] * 3, compiler_params=_params(2, VMEM_LIMIT))(w, g, m, v)


def _local_step(x, mem, target, wts, sm):
    kinds = [i % 3 for i in range(DEPTH)]
    mnw = sm["mem_norm_w"].reshape(1, D)
    kv, wkv = None, None
    saved, blocks = [], []
    for i, kind in enumerate(kinds):
        j = i // 3
        npre = sm["norm_pre"][i].reshape(1, D)
        npost = sm["norm_post"][i].reshape(1, D)
        mix, gate, wo_after = wts["blocks"](i, x)
        pm, pg, h = _inproj_fwd(x, npre, mix, gate, kind == 2, f"inproj_fwd_{i}")
        if kv is None:
            wkv = wts["wkv_after"](h)
            kv = _memkv_fwd(mem, mnw, wkv)
        extra = None
        if kind == 0:
            bs3 = jnp.broadcast_to(sm["a_b_s"][j][:, :, None], (8, HD, HD))
            ymix = _gmlp_fwd(pm, sm["a_ln_w"][j].reshape(1, D), sm["a_ln_b"][j].reshape(1, D), sm["a_w_s"][j], bs3,
                             f"gmlp_fwd_{i}")
            extra = bs3
        elif kind == 1:
            ymix = _sconv_fwd(pm, sm["b_conv_w"][j], f"sconv_fwd_{i}")
        else:
            cpre = _dnconv_fwd(pm, sm["c_conv_w"][j], f"dnconv_fwd_{i}")
            alog = jnp.pad(sm["c_a_log"][j], (0, HD - 8)).reshape(1, HD)
            dtb = jnp.pad(sm["c_dt_bias"][j], (0, HD - 8)).reshape(1, HD)
            onw = sm["c_o_norm_w"][j].reshape(1, HD)
            *prep, t_mat = _dn_prep_fwd(cpre, pm, alog, dtb, f"dn_prep_fwd_{i}")
            ymix, st = _dn_scan_fwd(*prep, onw, f"dn_scan_fwd_{i}")
            extra = (cpre, prep, t_mat, st, alog, dtb, onw)
        wo = wo_after(ymix)
        blocks.append((mix, gate, wo))
        layer_in = x
        if i < DEPTH - 1:
            ycat, o, x = _gate_outproj_fwd(ymix, pg, kv, wo, x, npost, None, f"gate_outproj_fwd_{i}")
        else:
            ycat, o, loss, dx = _gate_outproj_fwd(ymix, pg, kv, wo, x, npost, target, f"gate_outproj_fwd_{i}")
        saved.append((layer_in, h, pm, pg, ymix, ycat, o, extra))

    g = {"wm": [None] * DEPTH, "wg": [None] * DEPTH, "wo": [None] * DEPTH, "norm_pre": [None] * DEPTH,
         "norm_post": [None] * DEPTH}
    dkv = jnp.zeros((N_MEM, 2 * D_XA), F32)
    sent = 0.0
    for i in reversed(range(DEPTH)):
        kind, j = kinds[i], i // 3
        xi, h, pm, pg, ymix, ycat, o, extra = saved[i]
        npre = sm["norm_pre"][i].reshape(1, D)
        npost = sm["norm_post"][i].reshape(1, D) + sent
        if kind == 0:
            mixer = (pm, sm["a_ln_w"][j].reshape(1, D), sm["a_ln_b"][j].reshape(1, D), sm["a_w_s"][j], extra)
            dobf, g["norm_post"][i], dpg, dkv, dpm, dlnw, dlnb, dws, dbs3 = _outproj_gate_bwd(
                dx, o, npost, blocks[i][2], ymix, pg, kv, dkv, f"outproj_gate_gmlp_bwd_{i}", mixer)
        else:
            dobf, g["norm_post"][i], dymix, dpg, dkv = _outproj_gate_bwd(dx, o, npost, blocks[i][2], ymix, pg, kv, dkv,
                                                                         f"outproj_gate_bwd_{i}")
        g["wo"][i] = _matmul_tn(ycat, dobf, f"dwo_{i}")
        if kind == 0:
            g.setdefault("a_ln_w", {})[j] = dlnw.reshape(D)
            g.setdefault("a_ln_b", {})[j] = dlnb.reshape(D)
            g.setdefault("a_w_s", {})[j] = dws
            g.setdefault("a_b_s", {})[j] = dbs3[:, :, 0]
        elif kind == 1:
            dpm, dcw = _sconv_bwd(dymix, pm, sm["b_conv_w"][j], f"sconv_bwd_{i}")
            g.setdefault("b_conv_w", {})[j] = dcw
        else:
            cpre, prep, t_mat, st, alog, dtb, onw = extra
            *dprep, donw = _dn_scan_bwd(dymix, *prep, st, onw, f"dn_scan_bwd_{i}")
            dcq, dck, dcv, dab, dalog, ddtb = _dn_prep_bwd(*dprep, t_mat, cpre, pm, alog, dtb, f"dn_prep_bwd_{i}")
            dpm, dcw = _dnconv_bwd(dcq, dck, dcv, dab, pm, sm["c_conv_w"][j], f"dnconv_bwd_{i}")
            g.setdefault("c_conv_w", {})[j] = dcw
            g.setdefault("c_a_log", {})[j] = dalog[0, :8]
            g.setdefault("c_dt_bias", {})[j] = ddtb[0, :8]
            g.setdefault("c_o_norm_w", {})[j] = donw[0]
        if kind == 2:
            g["wm"][i] = _matmul_tn(dpm, h, f"dwm_{i}")
            g["wg"][i] = _matmul_tn(dpg, h, f"dwg_{i}")
        else:
            g["wm"][i] = _matmul_tn(h, dpm, f"dwm_{i}", GRAD_TILE[kind])
            g["wg"][i] = _matmul_tn(h, dpg, f"dwg_{i}", GRAD_TILE[kind])
        sent = wts["layer_done"](i, g)
        dx, g["norm_pre"][i] = _inproj_bwd(dpm, dpg, xi, npre + sent, blocks[i][0], blocks[i][1], kind == 2, dx,
                                           f"inproj_bwd_{i}")
    g["mem_norm_w"], g["wkv"] = _memkv_bwd(mem, mnw, wkv, dkv)
    return loss[0, 0], dx, g


ANY = pl.BlockSpec(memory_space=pl.ANY)


def _place():
    return lax.axis_index("x"), lax.axis_index("y"), lax.axis_index("c")


def _add(a, b, name):
    def body(a_ref, b_ref, o_ref):
        o_ref[...] = a_ref[...] + b_ref[...]

    return pl.pallas_call(body, name=name, out_shape=SDS(a.shape, a.dtype), compiler_params=_params(0, VMEM_LIMIT))(a, b)


def _sum4(own, land):
    def body(own_ref, l_ref, o_ref):
        chip = 2 * lax.axis_index("x") + lax.axis_index("y")
        acc = jnp.where(chip == 0, own_ref[...], l_ref[0])
        for s in range(1, 4):
            acc = acc + jnp.where(chip == s, own_ref[...], l_ref[s])
        o_ref[...] = acc

    return pl.pallas_call(body, name="sum_small", out_shape=SDS(own.shape, own.dtype),
                          compiler_params=_params(0, VMEM_LIMIT))(own, land)


C_ROWS = 1312
_REDUCE_CHUNK = {512: 256, 2560: 640}
W_CLASSES = {"in0": (512, 256)}


def _chunk_list(specs):
    return [(k, r, chunk) for k, (half, chunk) in enumerate(specs) for r in range(0, half, chunk)]


def _gather_classes(arrs, specs, vec):
    n = len(arrs)
    chunks = _chunk_list(specs)
    nc = len(chunks)

    def body(*refs):
        ins, vec_ref, outs, ov_ref = refs[:n], refs[n], refs[n + 1:2 * n + 1], refs[2 * n + 1]
        ici_send, ici_recv, d2d_send, d2d_recv, vec_send, vec_recv = refs[2 * n + 2:]
        x, y, c = _place()
        chip = 2 * x + y
        peers = [(1 - x, y), (x, 1 - y), (1 - x, 1 - y)]

        def rows(ci, half):
            k, r, cnt = chunks[ci]
            return k, pl.ds(half * specs[k][0] + r, cnt)

        def over_ici(j, ci, slab):
            px, py = peers[j]
            k, rs = rows(ci, c)
            return pltpu.make_async_remote_copy(
                src_ref=ins[k].at[rs], dst_ref=outs[k].at[slab, rs], send_sem=ici_send.at[j * nc + ci],
                recv_sem=ici_recv.at[j * nc + ci], device_id=(px, py, c), device_id_type=MESH)

        def over_d2d(j, ci, half):
            px, py = peers[j]
            k, rs = rows(ci, half)
            where = outs[k].at[2 * px + py, rs]
            return pltpu.make_async_remote_copy(
                src_ref=where, dst_ref=where, send_sem=d2d_send.at[j * nc + ci], recv_sem=d2d_recv.at[j * nc + ci],
                device_id=(x, y, 1 - c), device_id_type=MESH)

        def small(j, slab):
            px, py = peers[j]
            return pltpu.make_async_remote_copy(
                src_ref=vec_ref, dst_ref=ov_ref.at[slab], send_sem=vec_send.at[j], recv_sem=vec_recv.at[j],
                device_id=(px, py, c), device_id_type=MESH)

        sends = [small(j, chip) for j in range(3)] + [over_ici(j, ci, chip) for ci in range(nc) for j in range(3)]
        for cp in sends:
            cp.start()
        forwards = []
        for ci in range(nc):
            for j, (px, py) in enumerate(peers):
                over_ici(j, ci, 2 * px + py).wait_recv()
                forwards.append(over_d2d(j, ci, c))
                forwards[-1].start()
        for ci in range(nc):
            for j in range(3):
                over_d2d(j, ci, 1 - c).wait_recv()
        for j, (px, py) in enumerate(peers):
            small(j, 2 * px + py).wait_recv()
        for cp in sends + forwards:
            cp.wait_send()

    dma = pltpu.SemaphoreType.DMA
    return pl.pallas_call(
        body, name="gather_weights", in_specs=[ANY] * (n + 1), out_specs=[ANY] * (n + 1),
        out_shape=[SDS((4,) + a.shape, a.dtype) for a in arrs] + [SDS((4,) + vec.shape, vec.dtype)],
        scratch_shapes=[dma((3 * nc,)), dma((3 * nc,)), dma((3 * nc,)), dma((3 * nc,)), dma((3,)), dma((3,))])(*arrs, vec)


def _swap_classes(grads, specs, small):
    n = len(grads)
    chunks = _chunk_list(specs)

    def body(*refs):
        ins, s_ref, outs, os_ref, send_sems, recv_sems = refs[:n], refs[n], refs[n + 1:2 * n + 1], *refs[2 * n + 1:]
        x, y, c = _place()
        copies = []
        for s in range(4):
            for k, r, cnt in chunks:
                copies.append(pltpu.make_async_remote_copy(
                    src_ref=ins[k].at[s, 1 - c, pl.ds(r, cnt)], dst_ref=outs[k].at[s, pl.ds(r, cnt)],
                    send_sem=send_sems.at[len(copies)], recv_sem=recv_sems.at[len(copies)],
                    device_id=(x, y, 1 - c), device_id_type=MESH))
        copies.append(pltpu.make_async_remote_copy(
            src_ref=s_ref, dst_ref=os_ref, send_sem=send_sems.at[len(copies)], recv_sem=recv_sems.at[len(copies)],
            device_id=(x, y, 1 - c), device_id_type=MESH))
        for cp in copies:
            cp.start()
        for cp in copies:
            cp.wait_recv()
        for cp in copies:
            cp.wait_send()

    m = 4 * len(chunks) + 1
    return pl.pallas_call(
        body, name="swap_halves", in_specs=[ANY] * (n + 1), out_specs=[ANY] * (n + 1),
        out_shape=[SDS((4, g.shape[2], g.shape[3]), F32) for g in grads] + [SDS(small.shape, F32)],
        scratch_shapes=[pltpu.SemaphoreType.DMA((m,)), pltpu.SemaphoreType.DMA((m,))])(*grads, small)


def _pair_sum_class(g, other, chunk, name):
    _, _, half, w = g.shape

    def body(g_ref, o_ref, pb_ref, own_ref):
        x, y, c = _place()
        v = jnp.where(c == 0, g_ref[0], g_ref[1]) + o_ref[...]
        pb_ref[...] = v.astype(BF16)

        @pl.when(pl.program_id(1) == 2 * x + y)
        def _():
            own_ref[...] = v

    return pl.pallas_call(
        body, name=name, grid=(half // chunk, 4),
        in_specs=[pl.BlockSpec((None, 2, chunk, w), lambda i, s: (s, 0, i, 0)),
                  pl.BlockSpec((None, chunk, w), lambda i, s: (s, i, 0))],
        out_specs=[pl.BlockSpec((None, chunk, w), lambda i, s: (s, i, 0)), pl.BlockSpec((chunk, w), lambda i, s: (i, 0))],
        out_shape=[SDS((4, half, w), BF16), SDS((half, w), F32)],
        compiler_params=_params(2, VMEM_LIMIT))(g, other)


def _exchange_classes(pbs, specs, ps):
    n = len(pbs)
    chunks = _chunk_list(specs)
    per_peer = len(chunks) + 1

    def body(*refs):
        ins, ps_ref, outs, ls_ref, send_sems, recv_sems = refs[:n], refs[n], refs[n + 1:2 * n + 1], *refs[2 * n + 1:]
        x, y, c = _place()
        chip = 2 * x + y
        peers = [(1 - x, y), (x, 1 - y), (1 - x, 1 - y)]

        def copies(slab_of):
            out = []
            for j, (px, py) in enumerate(peers):
                for k, r, cnt in chunks:
                    out.append(pltpu.make_async_remote_copy(
                        src_ref=ins[k].at[2 * px + py, pl.ds(r, cnt)], dst_ref=outs[k].at[slab_of(j), pl.ds(r, cnt)],
                        send_sem=send_sems.at[len(out)], recv_sem=recv_sems.at[len(out)], device_id=(px, py, c),
                        device_id_type=MESH))
                out.append(pltpu.make_async_remote_copy(
                    src_ref=ps_ref, dst_ref=ls_ref.at[slab_of(j)], send_sem=send_sems.at[len(out)],
                    recv_sem=recv_sems.at[len(out)], device_id=(px, py, c), device_id_type=MESH))
            return out

        sends = copies(lambda j: chip)
        for cp in sends:
            cp.start()
        for cp in copies(lambda j: 2 * peers[j][0] + peers[j][1]):
            cp.wait_recv()
        for cp in sends:
            cp.wait_send()

    m = 3 * per_peer
    return pl.pallas_call(
        body, name="chip_exchange", in_specs=[ANY] * (n + 1), out_specs=[ANY] * (n + 1),
        out_shape=[SDS(p.shape, BF16) for p in pbs] + [SDS((4,) + ps.shape, F32)],
        scratch_shapes=[pltpu.SemaphoreType.DMA((m,)), pltpu.SemaphoreType.DMA((m,))])(*pbs, ps)


def _chip_sum_class(own, land, chunk, name):
    half, w = own.shape

    def body(own_ref, l_ref, o_ref):
        chip = 2 * lax.axis_index("x") + lax.axis_index("y")
        acc = jnp.where(chip == 0, own_ref[...], l_ref[0].astype(F32))
        for s in range(1, 4):
            acc = acc + jnp.where(chip == s, own_ref[...], l_ref[s].astype(F32))
        o_ref[...] = acc

    return pl.pallas_call(
        body, name=name, grid=(half // chunk,),
        in_specs=[pl.BlockSpec((chunk, w), lambda i: (i, 0)), pl.BlockSpec((4, chunk, w), lambda i: (0, i, 0))],
        out_specs=pl.BlockSpec((chunk, w), lambda i: (i, 0)), out_shape=SDS((half, w), F32),
        compiler_params=_params(1, VMEM_LIMIT))(own, land)


def _share_classes(rs, specs):
    n = len(rs)
    chunks = _chunk_list(specs)

    def body(*refs):
        ins, outs, send_sems, recv_sems = refs[:n], refs[n:2 * n], *refs[2 * n:]
        x, y, c = _place()
        copies = [pltpu.make_async_remote_copy(
            src_ref=ins[k].at[pl.ds(r, cnt)], dst_ref=outs[k].at[pl.ds(r, cnt)], send_sem=send_sems.at[i],
            recv_sem=recv_sems.at[i], device_id=(x, y, 1 - c), device_id_type=MESH)
            for i, (k, r, cnt) in enumerate(chunks)]
        for cp in copies:
            cp.start()
        for cp in copies:
            cp.wait_recv()
        for cp in copies:
            cp.wait_send()

    m = len(chunks)
    return pl.pallas_call(
        body, name="share_half", in_specs=[ANY] * n, out_specs=[ANY] * n, out_shape=[SDS(r.shape, F32) for r in rs],
        scratch_shapes=[pltpu.SemaphoreType.DMA((m,)), pltpu.SemaphoreType.DMA((m,))])(*rs)


_HBM = pl.BlockSpec(memory_space=pltpu.HBM)
_SEM = pl.BlockSpec(memory_space=pltpu.SEMAPHORE)
_EFFECT = pltpu.SideEffectType.DATAFLOW_SIDE_EFFECTING


def _chip_peers():
    x, y, c = _place()
    return [(1 - x, y, c), (x, 1 - y, c), (1 - x, 1 - y, c)]


def _send_shard_start(v, name):
    def body(v_ref, land_ref, send_sems, recv_sems, v_thru, land_thru, token):
        x, y, c = _place()
        for j, peer in enumerate(_chip_peers()):
            pltpu.make_async_remote_copy(src_ref=v_ref, dst_ref=land_ref.at[2 * x + y], send_sem=send_sems.at[j],
                                         recv_sem=recv_sems.at[j], device_id=peer, device_id_type=MESH).start()
        token[...] = jnp.zeros_like(token)

    land_shape = (4,) + v.shape
    return pl.pallas_call(
        body, name=name,
        out_shape=(pltpu.SemaphoreType.DMA((3,)), pltpu.SemaphoreType.DMA((3,)), pltpu.HBM(v.shape, v.dtype),
                   pltpu.HBM(land_shape, v.dtype), SDS((8, 128), F32)),
        in_specs=(_HBM, _HBM), out_specs=(_SEM, _SEM, _HBM, _HBM, pl.BlockSpec(memory_space=pltpu.VMEM)),
        input_output_aliases={0: 2, 1: 3}, compiler_params=pltpu.CompilerParams(has_side_effects=_EFFECT),
    )(pltpu.with_memory_space_constraint(v, pltpu.HBM),
      pltpu.with_memory_space_constraint(lax.empty(land_shape, v.dtype), pltpu.HBM))


def _xor_peer(r):
    x, y, c = _place()
    return (1 - x if (r >> 2) & 1 else x, 1 - y if (r >> 1) & 1 else y, 1 - c if r & 1 else c)


def _send_pieces_start(parts, name):
    n = len(parts)

    def body(*refs):
        ins, lands = refs[:n], refs[n:2 * n]
        send_sems, recv_sems = refs[2 * n:2 * n + 2]
        token = refs[-1]
        x, y, c = _place()
        for r in range(1, 8):
            px, py, pc = _xor_peer(r)
            for k in range(n):
                pltpu.make_async_remote_copy(
                    src_ref=ins[k].at[2 * px + py, pc], dst_ref=lands[k].at[4 * x + 2 * y + c],
                    send_sem=send_sems.at[(r - 1) * n + k], recv_sem=recv_sems.at[(r - 1) * n + k],
                    device_id=(px, py, pc), device_id_type=MESH).start()
        token[...] = jnp.zeros_like(token)

    land_shapes = [(8,) + p.shape[2:] for p in parts]
    hbm = [pltpu.HBM(p.shape, p.dtype) for p in parts] + [pltpu.HBM(s, p.dtype) for s, p in zip(land_shapes, parts)]
    operands = [pltpu.with_memory_space_constraint(p, pltpu.HBM) for p in parts]
    operands += [pltpu.with_memory_space_constraint(lax.empty(s, p.dtype), pltpu.HBM) for s, p in zip(land_shapes, parts)]
    return pl.pallas_call(
        body, name=name,
        out_shape=(pltpu.SemaphoreType.DMA((7 * n,)), pltpu.SemaphoreType.DMA((7 * n,)), *hbm, SDS((8, 128), F32)),
        in_specs=(_HBM,) * (2 * n), out_specs=(_SEM, _SEM) + (_HBM,) * (2 * n) + (pl.BlockSpec(memory_space=pltpu.VMEM),),
        input_output_aliases={i: 2 + i for i in range(2 * n)},
        compiler_params=pltpu.CompilerParams(has_side_effects=_EFFECT))(*operands)


def _send_pieces_wait(started, after, name):
    send_sems, recv_sems, *thru, _ = started
    n = len(thru) // 2

    def body(*refs):
        ins, lands = refs[:n], refs[n:2 * n]
        send_sems, recv_sems = refs[2 * n:2 * n + 2]
        for r in range(1, 8):
            px, py, pc = _xor_peer(r)
            for k in range(n):
                copy = pltpu.make_async_remote_copy(
                    src_ref=ins[k].at[2 * px + py, pc], dst_ref=lands[k].at[4 * px + 2 * py + pc],
                    send_sem=send_sems.at[(r - 1) * n + k], recv_sem=recv_sems.at[(r - 1) * n + k],
                    device_id=(px, py, pc), device_id_type=MESH)
                copy.wait_send()
                copy.wait_recv()

    return pl.pallas_call(
        body, name=name, out_shape=tuple(pltpu.HBM(t.shape, t.dtype) for t in thru),
        in_specs=(_HBM,) * (2 * n) + (_SEM, _SEM, pl.BlockSpec(memory_space=pl.ANY)), out_specs=(_HBM,) * (2 * n),
        input_output_aliases={i: i for i in range(2 * n)},
        compiler_params=pltpu.CompilerParams(has_side_effects=_EFFECT))(*thru, send_sems, recv_sems, after)[n:]


def _sum8_class(own, land, chunk, name):
    rows, w = own.shape

    def body(own_ref, l_ref, o_ref):
        x, y, c = _place()
        me = 4 * x + 2 * y + c
        acc = jnp.where(me == 0, own_ref[...], l_ref[0].astype(F32))
        for d in range(1, 8):
            acc = acc + jnp.where(me == d, own_ref[...], l_ref[d].astype(F32))
        o_ref[...] = acc

    return pl.pallas_call(
        body, name=name, grid=(rows // chunk,),
        in_specs=[pl.BlockSpec((chunk, w), lambda i: (i, 0)), pl.BlockSpec((8, chunk, w), lambda i: (0, i, 0))],
        out_specs=pl.BlockSpec((chunk, w), lambda i: (i, 0)), out_shape=SDS((rows, w), F32),
        compiler_params=_params(1, VMEM_LIMIT))(own, land)


def _send_shard_wait(send_sems, recv_sems, v_thru, land_thru, after, name):
    def body(v_ref, land_ref, send_sems, recv_sems, after_ref, v_dead, got_ref):
        for j, (px, py, pc) in enumerate(_chip_peers()):
            copy = pltpu.make_async_remote_copy(src_ref=v_ref, dst_ref=land_ref.at[2 * px + py], send_sem=send_sems.at[j],
                                                recv_sem=recv_sems.at[j], device_id=(px, py, pc), device_id_type=MESH)
            copy.wait_send()
            copy.wait_recv()

    return pl.pallas_call(
        body, name=name,
        out_shape=(pltpu.HBM(v_thru.shape, v_thru.dtype), pltpu.HBM(land_thru.shape, land_thru.dtype)),
        in_specs=(_HBM, _HBM, _SEM, _SEM, pl.BlockSpec(memory_space=pl.ANY)), out_specs=(_HBM, _HBM),
        input_output_aliases={0: 0, 1: 1}, compiler_params=pltpu.CompilerParams(has_side_effects=_EFFECT),
    )(v_thru, land_thru, send_sems, recv_sems, after)[1]


_SMALL = ["mem_norm_w", "norm_pre", "norm_post", "a_ln_w", "a_ln_b", "a_w_s", "a_b_s", "b_conv_w", "c_conv_w",
          "c_a_log", "c_dt_bias", "c_o_norm_w"]
_SMALL_SHAPES = {"mem_norm_w": (D,), "norm_pre": (4, D), "norm_post": (4, D), "a_ln_w": (2, D), "a_ln_b": (2, D),
                 "a_w_s": (2, 8, HD, HD), "a_b_s": (2, 8, HD), "b_conv_w": (1, 3, D), "c_conv_w": (1, 4, 3 * D),
                 "c_a_log": (1, 8), "c_dt_bias": (1, 8), "c_o_norm_w": (1, HD)}
_SHARDED_SMALL = {"a_ln_w": D // 4, "a_ln_b": D // 4, "b_conv_w": D // 4, "c_conv_w": 3 * D // 4}
_SMALL_ROWS = 288


def _size(shape):
    n = 1
    for d in shape:
        n *= d
    return n


def kernel(x, mem, mem_norm_w, w_mem_kv, norm_pre, norm_post, w_out, a_w_in, a_ln_w, a_ln_b, a_w_s, a_b_s, b_w_in, b_conv_w, c_w_in, c_conv_w, c_a_log, c_dt_bias, c_o_norm_w, loss_target, m_mem_norm_w, m_w_mem_kv, m_norm_pre, m_norm_post, m_w_out, m_a_w_in, m_a_ln_w, m_a_ln_b, m_a_w_s, m_a_b_s, m_b_w_in, m_b_conv_w, m_c_w_in, m_c_conv_w, m_c_a_log, m_c_dt_bias, m_c_o_norm_w, v_mem_norm_w, v_w_mem_kv, v_norm_pre, v_norm_post, v_w_out, v_a_w_in, v_a_ln_w, v_a_ln_b, v_a_w_s, v_a_b_s, v_b_w_in, v_b_conv_w, v_c_w_in, v_c_conv_w, v_c_a_log, v_c_dt_bias, v_c_o_norm_w):
    names = ["mem_norm_w", "w_mem_kv", "norm_pre", "norm_post", "w_out", "a_w_in", "a_ln_w", "a_ln_b", "a_w_s", "a_b_s",
             "b_w_in", "b_conv_w", "c_w_in", "c_conv_w", "c_a_log", "c_dt_bias", "c_o_norm_w"]
    w = dict(zip(names, [mem_norm_w, w_mem_kv, norm_pre, norm_post, w_out, a_w_in, a_ln_w, a_ln_b, a_w_s, a_b_s, b_w_in,
                         b_conv_w, c_w_in, c_conv_w, c_a_log, c_dt_bias, c_o_norm_w]))
    m = dict(zip(names, [m_mem_norm_w, m_w_mem_kv, m_norm_pre, m_norm_post, m_w_out, m_a_w_in, m_a_ln_w, m_a_ln_b, m_a_w_s,
                         m_a_b_s, m_b_w_in, m_b_conv_w, m_c_w_in, m_c_conv_w, m_c_a_log, m_c_dt_bias, m_c_o_norm_w]))
    v = dict(zip(names, [v_mem_norm_w, v_w_mem_kv, v_norm_pre, v_norm_post, v_w_out, v_a_w_in, v_a_ln_w, v_a_ln_b, v_a_w_s,
                         v_a_b_s, v_b_w_in, v_b_conv_w, v_c_w_in, v_c_conv_w, v_c_a_log, v_c_dt_bias, v_c_o_norm_w]))
    chip = 2 * lax.axis_index("x") + lax.axis_index("y")

    def rows_of_ct(a):
        return a[0].T

    def with_mine(gathered, own):
        return lax.dynamic_update_slice(gathered, own[None], (chip,) + (0,) * own.ndim)

    first = [a_w_in[0].astype(BF16)]
    vec = jnp.concatenate([a_ln_w.reshape(-1), a_ln_b.reshape(-1), b_conv_w.reshape(-1), c_conv_w.reshape(-1)])
    vec = jnp.pad(vec, (0, 8 * D - vec.shape[0])).reshape(8, D)
    ga0, gvec = _gather_classes(first, [W_CLASSES[k] for k in W_CLASSES], vec)
    ga0, gvec = with_mine(ga0, first[0]), with_mine(gvec, vec)
    gv = gvec.reshape(4, 8 * D)
    w_in = {1: b_w_in[0], 2: jnp.pad(rows_of_ct(c_w_in), ((0, C_ROWS - 1284), (0, 0))), 3: a_w_in[1]}
    order = [("kv", 0, w_mem_kv), ("out", 0, w_out[0])]
    for i in range(1, DEPTH):
        order += [("in", i, w_in[i]), ("out", i, w_out[i])]
    later, sent, started = {}, {}, 0.0 * ga0[0, 0, 0].astype(F32)
    for kind, i, a in order:
        later[kind, i] = (a + started).astype(BF16)
        sent[kind, i] = _send_shard_start(later[kind, i], f"send_w_{kind}_{i}")
        started = sent[kind, i][4][0, 0]

    def arrived(k, after):
        return with_mine(_send_shard_wait(*sent[k][:4], after, f"wait_w_{k[0]}_{k[1]}"), later[k])

    def blocks(i, after):
        def wo_after(later_value):
            return arrived(("out", i), later_value).reshape(D_CAT, D)

        if i == 0:
            return [ga0[0], ga0[1]], [ga0[2], ga0[3]], wo_after
        got = arrived(("in", i), after)
        if i == 1:
            return [got[0], got[1], got[2][:, :512]], [got[2][:, 512:], got[3]], wo_after
        if i == 3:
            return [got[0], got[1]], [got[2], got[3]], wo_after
        fct = got[:, :1284].reshape(5136, D)
        c_ab = jnp.concatenate([fct[3 * D:3 * D + 16], jnp.zeros((AB_PAD - 16, D), BF16)], axis=0)
        return [fct[:3 * D], c_ab], [fct[3 * D + 16:]], wo_after
    sm = {"mem_norm_w": mem_norm_w, "norm_pre": norm_pre + started, "norm_post": norm_post, "a_w_s": a_w_s, "a_b_s": a_b_s,
          "c_a_log": c_a_log, "c_dt_bias": c_dt_bias, "c_o_norm_w": c_o_norm_w,
          "a_ln_w": gv[:, 0:512].reshape(4, 2, 256).transpose(1, 0, 2).reshape(2, D),
          "a_ln_b": gv[:, 512:1024].reshape(4, 2, 256).transpose(1, 0, 2).reshape(2, D),
          "b_conv_w": gv[:, 1024:1792].reshape(4, 1, 3, 256).transpose(1, 2, 0, 3).reshape(1, 3, D),
          "c_conv_w": gv[:, 1792:4864].reshape(4, 1, 4, 768).transpose(1, 2, 0, 3).reshape(1, 4, 3 * D)}
    wts = {"wkv_after": lambda after: arrived(("kv", 0), after).reshape(D, 2 * D_XA), "blocks": blocks}

    def layer_grads(i, g):
        if i % 3 == 2:
            gct = jnp.concatenate([g["wm"][i][:3 * D + 16], g["wg"][i]], axis=0).reshape(4, 1284, D)
            w_in = jnp.pad(gct, ((0, 0), (0, C_ROWS - 1284), (0, 0)))
        else:
            w_in = jnp.concatenate([g["wm"][i], g["wg"][i]], axis=0).reshape(4, -1, GRAD_TILE[i % 3])
        out = {f"in{i}": w_in, f"out{i}": g["wo"][i].reshape(4, 384, D)}
        return {k: a.reshape(4, 2, a.shape[1] // 2, a.shape[2]) for k, a in out.items()}

    pending = {}

    def layer_done(i, g):
        halves = layer_grads(i, g)
        started = _send_pieces_start([h.astype(BF16) for h in halves.values()], f"send_grads_{i}")
        pending[i] = (started, halves)
        return started[-1][0, 0]

    wts["layer_done"] = layer_done

    loss, dx, g = _local_step(x[0], mem[0], loss_target[0], wts, sm)
    loss = lax.psum(loss, ("x", "y", "c"))

    core = lax.axis_index("c")
    mine, specs = {}, {}
    for i in reversed(range(DEPTH)):
        started, halves = pending[i]
        lands = _send_pieces_wait(started, dx, f"wait_grads_{i}")
        for (k, h), land in zip(halves.items(), lands):
            own = lax.dynamic_index_in_dim(lax.dynamic_index_in_dim(h, chip, 0, False), core, 0, False)
            specs[k] = (own.shape[0], _REDUCE_CHUNK.get(own.shape[0], own.shape[0]))
            mine[k] = _sum8_class(own, land, specs[k][1], f"sum8_{k}")

    first = {"kv": g["wkv"].reshape(4, 2, 128, D)}
    first_specs = [(h.shape[2], _REDUCE_CHUNK.get(h.shape[2], h.shape[2])) for h in first.values()]
    halves = list(first.values())
    gs = {"mem_norm_w": g["mem_norm_w"], "norm_pre": jnp.concatenate(g["norm_pre"]),
          "norm_post": jnp.concatenate(g["norm_post"])}
    for n in _SMALL[3:]:
        gs[n] = jnp.stack([g[n][j] for j in sorted(g[n])])
    flat = jnp.concatenate([gs[n].reshape(-1) for n in _SMALL])
    small = jnp.pad(flat, (0, _SMALL_ROWS * D - flat.shape[0])).reshape(_SMALL_ROWS, D)
    *others, other_small = _swap_classes(halves, first_specs, small)
    pairs = [_pair_sum_class(h, o, s[1], f"pair_sum_{k}") for k, h, o, s in zip(first, halves, others, first_specs)]
    pair_small = _add(small, other_small, "pair_sum_small")
    *lands, land_small = _exchange_classes([p[0] for p in pairs], first_specs, pair_small)
    for k, p, land, s in zip(first, pairs, lands, first_specs):
        mine[k], specs[k] = _chip_sum_class(p[1], land, s[1], f"chip_sum_{k}"), s
    theirs = _share_classes(list(mine.values()), [specs[k] for k in mine])
    south = core == 0
    sh = {k: jnp.concatenate([jnp.where(south, a, b), jnp.where(south, b, a)], axis=0)
          for (k, a), b in zip(mine.items(), theirs)}
    grads = {"a_w_in": jnp.stack([sh["in0"], sh["in3"]]),
             "b_w_in": sh["in1"].reshape(5, D, 256).transpose(1, 0, 2).reshape(b_w_in.shape),
             "c_w_in": sh["in2"][:1284], "w_out": jnp.stack([sh[f"out{i}"] for i in range(DEPTH)]),
             "w_mem_kv": sh["kv"]}
    flat = _sum4(pair_small, land_small).reshape(-1)
    off = 0
    for n in _SMALL:
        shape = _SMALL_SHAPES[n]
        full = flat[off:off + _size(shape)].reshape(shape)
        off += _size(shape)
        if n in _SHARDED_SMALL:
            full = lax.dynamic_slice_in_dim(full, chip * _SHARDED_SMALL[n], _SHARDED_SMALL[n], axis=len(shape) - 1)
        grads[n] = full

    delta, new_m, new_v = {}, {}, {}
    for n in names:
        shape = w[n].shape
        if n == "c_w_in":
            d_, m_, v_ = _adamw(rows_of_ct(w[n]), grads[n], rows_of_ct(m[n]), rows_of_ct(v[n]), f"adamw_{n}")
            delta[n], new_m[n], new_v[n], grads[n] = d_.T[None], m_.T[None], v_.T[None], grads[n].T[None]
            continue
        view = (1, shape[0]) if len(shape) == 1 else (_size(shape[:-1]), shape[-1])
        d_, m_, v_ = _adamw(w[n].reshape(view), grads[n].reshape(view), m[n].reshape(view), v[n].reshape(view),
                            f"adamw_{n}")
        delta[n], new_m[n], new_v[n] = d_.reshape(shape), m_.reshape(shape), v_.reshape(shape)
    return (loss, dx[None], *[grads[n].reshape(w[n].shape) for n in names], *[delta[n] for n in names],
            *[new_m[n] for n in names], *[new_v[n] for n in names])
```

```python
import functools

import jax
import jax.numpy as jnp
from jax import lax
from jax.experimental import pallas as pl
from jax.experimental.pallas import tpu as pltpu

F32 = jnp.float32
BF16 = jnp.bfloat16
HI = lax.Precision.HIGHEST
MESH = pl.DeviceIdType.MESH
SDS = jax.ShapeDtypeStruct

D = 1024
D_XA = 512
D_CAT = 1536
N_MEM = 256
HD = 128
DEPTH = 4
EPS = 1e-6
TT = 512
DN_C = 64
DN_TB = 512
HALO = 8
AB_PAD = 128
VMEM_LIMIT = 56 * 1024 * 1024
GRAD_TILE = {0: 1024, 1: 256}

ADAM_LR, ADAM_B1, ADAM_B2, ADAM_EPS, ADAM_WD, ADAM_STEP = 0.001, 0.9, 0.999, 1e-08, 0.01, 10


def _params(n_grid, vmem=None):
    return pltpu.CompilerParams(dimension_semantics=("arbitrary",) * n_grid, vmem_limit_bytes=vmem)


def _rms(x, w):
    return x * lax.rsqrt(jnp.mean(x * x, axis=-1, keepdims=True) + EPS) * w


def _dot_nn(a, b):
    return jnp.dot(a.astype(BF16), b.astype(BF16), preferred_element_type=F32)


def _dot_nt(a, b):
    return lax.dot_general(a.astype(BF16), b.astype(BF16), (((1,), (1,)), ((), ())), preferred_element_type=F32)


def _dot_tn(a, b):
    return lax.dot_general(a.astype(BF16), b.astype(BF16), (((0,), (0,)), ((), ())), preferred_element_type=F32)


@jax.custom_vjp
def mm(a, b):
    return _dot_nn(a, b)


mm.defvjp(lambda a, b: (_dot_nn(a, b), (a, b)), lambda r, g: (_dot_nt(g, r[1]), _dot_tn(r[0], g)))


@jax.custom_vjp
def mm_nt(a, b):
    return _dot_nt(a, b)


mm_nt.defvjp(lambda a, b: (_dot_nt(a, b), (a, b)), lambda r, g: (_dot_nn(g, r[1]), _dot_tn(g, r[0])))


def _row_spec(width, tile=TT):
    return pl.BlockSpec((tile, width), lambda i: (i, 0))


def _full_spec(shape):
    return pl.BlockSpec(shape, lambda *_: (0,) * len(shape))


def _widths(blocks, transposed):
    return [b.shape[0 if transposed else 1] for b in blocks]


def _inproj_fwd(x, nw, mix, gate, transposed, name):
    T, nm = x.shape[0], len(mix)
    M, G = sum(_widths(mix, transposed)), sum(_widths(gate, transposed))

    def body(x_ref, nw_ref, *refs):
        blocks, (pm_ref, pg_ref, h_ref) = refs[:-3], refs[-3:]
        h = _rms(x_ref[...], nw_ref[...]).astype(BF16)
        h_ref[...] = h
        for p_ref, group in ((pm_ref, blocks[:nm]), (pg_ref, blocks[nm:])):
            off = 0
            for w_ref in group:
                w = w_ref.shape[0 if transposed else 1]
                p_ref[:, off:off + w] = _dot_nt(h, w_ref[...]) if transposed else _dot_nn(h, w_ref[...])
                off += w

    return pl.pallas_call(
        body, name=name, grid=(T // TT,),
        in_specs=[_row_spec(D), _full_spec((1, D))] + [_full_spec(b.shape) for b in mix + gate],
        out_specs=[_row_spec(M), _row_spec(G), _row_spec(D)],
        out_shape=[SDS((T, M), F32), SDS((T, G), F32), SDS((T, D), BF16)],
        compiler_params=_params(1, VMEM_LIMIT))(x, nw, *mix, *gate)


def _inproj_bwd(dpm, dpg, x, nw, mix, gate, transposed, dxc, name):
    T, nm = x.shape[0], len(mix)
    M, G = sum(_widths(mix, transposed)), sum(_widths(gate, transposed))

    def body(dpm_ref, dpg_ref, x_ref, nw_ref, *refs):
        blocks, (dxc_ref, dx_ref, dnw_ref) = refs[:-3], refs[-3:]
        dh = None
        for dp_ref, group in ((dpm_ref, blocks[:nm]), (dpg_ref, blocks[nm:])):
            off = 0
            for w_ref in group:
                w = w_ref.shape[0 if transposed else 1]
                dp = dp_ref[:, off:off + w]
                part = _dot_nn(dp, w_ref[...]) if transposed else _dot_nt(dp, w_ref[...])
                dh = part if dh is None else dh + part
                off += w
        _, vjp = jax.vjp(_rms, x_ref[...], nw_ref[...])
        dxr, dnw = vjp(dh)
        dx_ref[...] = dxc_ref[...] + dxr

        @pl.when(pl.program_id(0) == 0)
        def _():
            dnw_ref[...] = jnp.zeros_like(dnw_ref)
        dnw_ref[...] += dnw

    return pl.pallas_call(
        body, name=name, grid=(T // TT,),
        in_specs=[_row_spec(M), _row_spec(G), _row_spec(D), _full_spec((1, D))]
        + [_full_spec(b.shape) for b in mix + gate] + [_row_spec(D)],
        out_specs=[_row_spec(D), _full_spec((1, D))],
        out_shape=[SDS((T, D), F32), SDS((1, D), F32)],
        compiler_params=_params(1, VMEM_LIMIT))(dpm, dpg, x, nw, *mix, *gate, dxc)


def _matmul_tn(a, b, name, sub=None):
    T, K = a.shape
    N = b.shape[1]
    tn = 1024 if N % 1024 == 0 else (640 if N % 640 == 0 else N)
    tt = min(1024, T)
    n_sub = 1 if sub is None else tn // sub

    def body(a_ref, b_ref, o_ref):
        @pl.when(pl.program_id(1) == 0)
        def _():
            o_ref[...] = jnp.zeros_like(o_ref)
        res = _dot_tn(a_ref[...], b_ref[...])
        if sub is None:
            o_ref[...] += res
        else:
            for i in range(n_sub):
                o_ref[i] += res[:, i * sub:(i + 1) * sub]

    if sub is None:
        out_spec, out_shape = pl.BlockSpec((K, tn), lambda j, t: (0, j)), SDS((K, N), F32)
    else:
        out_spec, out_shape = pl.BlockSpec((n_sub, K, sub), lambda j, t: (j, 0, 0)), SDS((N // sub, K, sub), F32)
    return pl.pallas_call(
        body, name=name, grid=(N // tn, T // tt),
        in_specs=[pl.BlockSpec((tt, K), lambda j, t: (t, 0)), pl.BlockSpec((tt, tn), lambda j, t: (t, j))],
        out_specs=out_spec, out_shape=out_shape,
        compiler_params=_params(2, VMEM_LIMIT))(a, b)


def _memkv_fn(mem, w, wkv):
    return mm(_rms(mem, w), wkv)


def _memkv_fwd(mem, w, wkv):
    def body(mem_ref, w_ref, wkv_ref, kv_ref):
        kv_ref[...] = _memkv_fn(mem_ref[...], w_ref[...], wkv_ref[...])

    return pl.pallas_call(body, name="memkv_fwd", out_shape=SDS((N_MEM, 2 * D_XA), F32),
                          compiler_params=_params(0, VMEM_LIMIT))(mem, w, wkv)


def _memkv_bwd(mem, w, wkv, dkv):
    def body(mem_ref, w_ref, wkv_ref, dkv_ref, dw_ref, dwkv_ref):
        _, vjp = jax.vjp(functools.partial(_memkv_fn, mem_ref[...]), w_ref[...], wkv_ref[...].astype(F32))
        dw, dwkv = vjp(dkv_ref[...])
        dw_ref[...] = dw
        dwkv_ref[...] = dwkv

    return pl.pallas_call(body, name="memkv_bwd", out_shape=[SDS((1, D), F32), SDS((D, 2 * D_XA), F32)],
                          compiler_params=_params(0, VMEM_LIMIT))(mem, w, wkv, dkv)


def _attn_gate(ymix, qx, z, *kvs):
    outs = []
    for j in range(4):
        s = mm_nt(qx[:, j * HD:(j + 1) * HD], kvs[j]) * (HD ** -0.5)
        e = jnp.exp(s - lax.stop_gradient(jnp.max(s, axis=-1, keepdims=True)))
        outs.append(mm(e / jnp.sum(e, axis=-1, keepdims=True), kvs[4 + j]))
    return jnp.concatenate([ymix] + outs, axis=1) * jax.nn.silu(z)


def _kv_blocks(kv_ref):
    return [kv_ref[:, j * HD:(j + 1) * HD] for j in range(8)]


def _gate_outproj_fwd(ymix, pg, kv, wo, x, nw, target, name):
    T = x.shape[0]
    last = target is not None

    def body(ymix_ref, pg_ref, kv_ref, wo_ref, x_ref, nw_ref, *refs):
        ycat = _attn_gate(ymix_ref[...], pg_ref[:, :D_XA], pg_ref[:, D_XA:], *_kv_blocks(kv_ref)).astype(BF16)
        o = jnp.dot(ycat, wo_ref[...], preferred_element_type=F32)
        y = x_ref[...] + _rms(o, nw_ref[...])
        if not last:
            ycat_ref, o_ref, y_ref = refs
            y_ref[...] = y
        else:
            t_ref, ycat_ref, o_ref, loss_ref, dy_ref = refs
            err = y - t_ref[...]
            dy_ref[...] = err * (1.0 / D)

            @pl.when(pl.program_id(0) == 0)
            def _():
                loss_ref[...] = jnp.zeros_like(loss_ref)
            part = jnp.sum(jnp.sum(err * err, axis=1, keepdims=True), axis=0, keepdims=True) * (0.5 / D)
            loss_ref[...] += jnp.broadcast_to(part, loss_ref.shape)
        ycat_ref[...] = ycat
        o_ref[...] = o

    in_specs = [_row_spec(D), _row_spec(D_XA + D_CAT), _full_spec((N_MEM, 2 * D_XA)), _full_spec((D_CAT, D)),
                _row_spec(D), _full_spec((1, D))]
    out_specs = [_row_spec(D_CAT), _row_spec(D)]
    out_shape = [SDS((T, D_CAT), BF16), SDS((T, D), F32)]
    if last:
        in_specs.append(_row_spec(D))
        out_specs += [_full_spec((8, 128)), _row_spec(D)]
        out_shape += [SDS((8, 128), F32), SDS((T, D), F32)]
    else:
        out_specs.append(_row_spec(D))
        out_shape.append(SDS((T, D), F32))
    args = (ymix, pg, kv, wo, x, nw) + ((target,) if last else ())
    return pl.pallas_call(body, name=name, grid=(T // TT,), in_specs=in_specs, out_specs=out_specs,
                          out_shape=out_shape, compiler_params=_params(1, VMEM_LIMIT))(*args)


def _outproj_gate_bwd(dxo, o, nw, wo, ymix, pg, kv, dkv_in, name):
    T = dxo.shape[0]
    G = D_XA + D_CAT

    def body(dxo_ref, o_ref, nw_ref, wo_ref, ymix_ref, pg_ref, kv_ref, dkvin_ref,
             dobf_ref, dnw_ref, dymix_ref, dpg_ref, dkv_ref):
        _, vjp = jax.vjp(_rms, o_ref[...], nw_ref[...])
        do, dnw = vjp(dxo_ref[...])
        dobf = do.astype(BF16)
        dobf_ref[...] = dobf
        dycat = _dot_nt(dobf, wo_ref[...])
        _, vjp = jax.vjp(_attn_gate, ymix_ref[...], pg_ref[:, :D_XA], pg_ref[:, D_XA:], *_kv_blocks(kv_ref))
        g = vjp(dycat)
        dymix_ref[...] = g[0]
        dpg_ref[:, :D_XA] = g[1].astype(BF16)
        dpg_ref[:, D_XA:] = g[2].astype(BF16)

        @pl.when(pl.program_id(0) == 0)
        def _():
            dnw_ref[...] = jnp.zeros_like(dnw_ref)
            dkv_ref[...] = dkvin_ref[...]
        dnw_ref[...] += dnw
        for j in range(8):
            dkv_ref[:, j * HD:(j + 1) * HD] += g[3 + j]

    return pl.pallas_call(
        body, name=name, grid=(T // TT,),
        in_specs=[_row_spec(D), _row_spec(D), _full_spec((1, D)), _full_spec((D_CAT, D)), _row_spec(D), _row_spec(G),
                  _full_spec((N_MEM, 2 * D_XA)), _full_spec((N_MEM, 2 * D_XA))],
        out_specs=[_row_spec(D), _full_spec((1, D)), _row_spec(D), _row_spec(G), _full_spec((N_MEM, 2 * D_XA))],
        out_shape=[SDS((T, D), BF16), SDS((1, D), F32), SDS((T, D), F32), SDS((T, G), BF16),
                   SDS((N_MEM, 2 * D_XA), F32)],
        compiler_params=_params(1, VMEM_LIMIT))(dxo, o, nw, wo, ymix, pg, kv, dkv_in)


def _gmlp_pre(u, v, lnw, lnb):
    vg = jax.nn.gelu(v)
    xc = vg - jnp.mean(vg, axis=-1, keepdims=True)
    vl = xc * lax.rsqrt(jnp.mean(xc * xc, axis=-1, keepdims=True) + EPS) * lnw + lnb
    return jax.nn.gelu(u), vl


def _tril(n, strict=False):
    r = lax.broadcasted_iota(jnp.int32, (n, n), 0)
    c = lax.broadcasted_iota(jnp.int32, (n, n), 1)
    return (r > c) if strict else (r >= c)


def _gmlp_fwd(pm, lnw, lnb, ws, bs3, name):
    T = pm.shape[0]

    def body(pm_ref, lnw_ref, lnb_ref, ws_ref, bs_ref, y_ref):
        ug, vl = _gmlp_pre(pm_ref[:, :D], pm_ref[:, D:], lnw_ref[...], lnb_ref[...])
        mask = _tril(HD)
        for g in range(8):
            w = jnp.where(mask, ws_ref[g], 0.0)
            for c in range(TT // HD):
                rows, cols = slice(c * HD, (c + 1) * HD), slice(g * HD, (g + 1) * HD)
                y_ref[rows, cols] = ug[rows, cols] * (_dot_nn(w, vl[rows, cols]) + bs_ref[g])

    return pl.pallas_call(
        body, name=name, grid=(T // TT,),
        in_specs=[_row_spec(2 * D), _full_spec((1, D)), _full_spec((1, D)), _full_spec((8, HD, HD)),
                  _full_spec((8, HD, HD))],
        out_specs=_row_spec(D), out_shape=SDS((T, D), F32),
        compiler_params=_params(1, VMEM_LIMIT))(pm, lnw, lnb, ws, bs3)


def _gmlp_bwd(dy, pm, lnw, lnb, ws, bs3, name):
    T = pm.shape[0]
    n_t = T // TT

    def body(dy_ref, pm_ref, lnw_ref, lnb_ref, ws_ref, bs_ref, dpm_ref, dlnw_ref, dlnb_ref, dws_ref, dbs_ref,
             dug_scr, dvl_scr, dbs_scr):
        i = pl.program_id(0)

        @pl.when(i == 0)
        def _():
            dlnw_ref[...] = jnp.zeros_like(dlnw_ref)
            dlnb_ref[...] = jnp.zeros_like(dlnb_ref)
            dws_ref[...] = jnp.zeros_like(dws_ref)
            dbs_scr[...] = jnp.zeros_like(dbs_scr)

        (ug, vl), vjp = jax.vjp(_gmlp_pre, pm_ref[:, :D], pm_ref[:, D:], lnw_ref[...], lnb_ref[...])
        mask = _tril(HD)
        for g in range(8):
            w = jnp.where(mask, ws_ref[g], 0.0)
            dw = jnp.zeros((HD, HD), F32)
            db = jnp.zeros((HD, HD), F32)
            for c in range(TT // HD):
                rows, cols = slice(c * HD, (c + 1) * HD), slice(g * HD, (g + 1) * HD)
                dyb, vlb = dy_ref[rows, cols], vl[rows, cols]
                sp = _dot_nn(w, vlb) + bs_ref[g]
                dsp = dyb * ug[rows, cols]
                dug_scr[rows, cols] = dyb * sp
                dvl_scr[rows, cols] = _dot_tn(w, dsp)
                dw += _dot_nt(dsp, vlb)
                db += dsp
            dws_ref[g] += jnp.where(mask, dw, 0.0)
            dbs_scr[g] += db
        du, dv, dlnw, dlnb = vjp((dug_scr[...], dvl_scr[...]))
        dpm_ref[:, :D] = du.astype(BF16)
        dpm_ref[:, D:] = dv.astype(BF16)
        dlnw_ref[...] += dlnw
        dlnb_ref[...] += dlnb

        @pl.when(i == n_t - 1)
        def _():
            for g in range(8):
                dbs_ref[g] = jnp.broadcast_to(jnp.sum(dbs_scr[g], axis=1, keepdims=True), (HD, HD))

    return pl.pallas_call(
        body, name=name, grid=(n_t,),
        in_specs=[_row_spec(D), _row_spec(2 * D), _full_spec((1, D)), _full_spec((1, D)), _full_spec((8, HD, HD)),
                  _full_spec((8, HD, HD))],
        out_specs=[_row_spec(2 * D), _full_spec((1, D)), _full_spec((1, D)), _full_spec((8, HD, HD)),
                   _full_spec((8, HD, HD))],
        out_shape=[SDS((T, 2 * D), BF16), SDS((1, D), F32), SDS((1, D), F32), SDS((8, HD, HD), F32),
                   SDS((8, HD, HD), F32)],
        scratch_shapes=[pltpu.VMEM((TT, D), F32), pltpu.VMEM((TT, D), F32), pltpu.VMEM((8, HD, HD), F32)],
        compiler_params=_params(1, VMEM_LIMIT))(dy, pm, lnw, lnb, ws, bs3)


def _prev_spec(width, T):
    return pl.BlockSpec((HALO, width), lambda i: (jnp.maximum(i * (TT // HALO) - 1, 0), 0))


def _next_spec(width, T):
    return pl.BlockSpec((HALO, width), lambda i: (jnp.minimum((i + 1) * (TT // HALO), T // HALO - 1), 0))


def _rows_before(ext, j):
    return ext[HALO:] if j == 0 else pltpu.roll(ext, j, 0)[HALO:]


def _rows_after(ext, j):
    n = ext.shape[0]
    return ext[:n - HALO] if j == 0 else pltpu.roll(ext, n - j, 0)[:n - HALO]


def _conv_apply(ext_s, w):
    K = w.shape[0]
    y = _rows_before(ext_s, K - 1) * w[0:1]
    for k in range(1, K):
        y = y + _rows_before(ext_s, K - 1 - k) * w[k:k + 1]
    return y


def _conv_grads(ext_s, ext_dy, w):
    K = w.shape[0]
    dy = ext_dy[:ext_dy.shape[0] - HALO]
    ds = _rows_after(ext_dy, K - 1) * w[0:1]
    dws = [jnp.sum(dy * _rows_before(ext_s, K - 1), axis=0, keepdims=True)]
    for k in range(1, K):
        ds = ds + _rows_after(ext_dy, K - 1 - k) * w[k:k + 1]
        dws.append(jnp.sum(dy * _rows_before(ext_s, K - 1 - k), axis=0, keepdims=True))
    return ds, jnp.concatenate(dws, axis=0)


def _sconv_fwd(pm, w, name):
    T = pm.shape[0]

    def body(pm_ref, prev_ref, w_ref, y_ref):
        s = pm_ref[:, D:2 * D] * pm_ref[:, 2 * D:]
        sp = jnp.where(pl.program_id(0) > 0, prev_ref[:, D:2 * D] * prev_ref[:, 2 * D:], 0.0)
        y_ref[...] = pm_ref[:, :D] * _conv_apply(jnp.concatenate([sp, s], axis=0), w_ref[...])

    return pl.pallas_call(
        body, name=name, grid=(T // TT,),
        in_specs=[_row_spec(3 * D), _prev_spec(3 * D, T), _full_spec((3, D))],
        out_specs=_row_spec(D), out_shape=SDS((T, D), F32),
        compiler_params=_params(1, VMEM_LIMIT))(pm, pm, w)


def _sconv_bwd(dy, pm, w, name):
    T = pm.shape[0]
    n_t = T // TT

    def body(dy_ref, dyn_ref, pm_ref, prev_ref, next_ref, w_ref, dpm_ref, dw_ref):
        i = pl.program_id(0)
        bg, cg, hv = pm_ref[:, :D], pm_ref[:, D:2 * D], pm_ref[:, 2 * D:]
        sp = jnp.where(i > 0, prev_ref[:, D:2 * D] * prev_ref[:, 2 * D:], 0.0)
        ext_s = jnp.concatenate([sp, cg * hv], axis=0)
        dyv = dy_ref[...]
        dcn = jnp.where(i < n_t - 1, dyn_ref[...] * next_ref[:, :D], 0.0)
        ds, dw = _conv_grads(ext_s, jnp.concatenate([dyv * bg, dcn], axis=0), w_ref[...])
        dpm_ref[:, :D] = (dyv * _conv_apply(ext_s, w_ref[...])).astype(BF16)
        dpm_ref[:, D:2 * D] = (ds * hv).astype(BF16)
        dpm_ref[:, 2 * D:] = (ds * cg).astype(BF16)

        @pl.when(i == 0)
        def _():
            dw_ref[...] = jnp.zeros_like(dw_ref)
        dw_ref[...] += dw

    return pl.pallas_call(
        body, name=name, grid=(n_t,),
        in_specs=[_row_spec(D), _next_spec(D, T), _row_spec(3 * D), _prev_spec(3 * D, T), _next_spec(3 * D, T),
                  _full_spec((3, D))],
        out_specs=[_row_spec(3 * D), _full_spec((3, D))],
        out_shape=[SDS((T, 3 * D), BF16), SDS((3, D), F32)],
        compiler_params=_params(1, VMEM_LIMIT))(dy, dy, pm, pm, pm, w)


def _dnconv_fwd(pm, w, name):
    T = pm.shape[0]

    def body(pm_ref, prev_ref, w_ref, c_ref):
        sp = jnp.where(pl.program_id(0) > 0, prev_ref[...], 0.0)
        c_ref[...] = _conv_apply(jnp.concatenate([sp, pm_ref[...]], axis=0), w_ref[...])

    return pl.pallas_call(
        body, name=name, grid=(T // TT,),
        in_specs=[_row_spec(3 * D), _prev_spec(3 * D, T), _full_spec((4, 3 * D))],
        out_specs=_row_spec(3 * D), out_shape=SDS((T, 3 * D), F32),
        compiler_params=_params(1, VMEM_LIMIT))(pm, pm, w)


def _dnconv_bwd(dcq, dck, dcv, dab, pm, w, name):
    T = pm.shape[0]
    n_t = T // TT

    def body(dq_ref, dk_ref, dv_ref, dqn_ref, dkn_ref, dvn_ref, dab_ref, pm_ref, prev_ref, w_ref, dpm_ref, dw_ref):
        i = pl.program_id(0)
        sp = jnp.where(i > 0, prev_ref[...], 0.0)
        ext_s = jnp.concatenate([sp, pm_ref[...]], axis=0)
        own = jnp.concatenate([dq_ref[...], dk_ref[...], dv_ref[...]], axis=1)
        nxt = jnp.where(i < n_t - 1, jnp.concatenate([dqn_ref[...], dkn_ref[...], dvn_ref[...]], axis=1), 0.0)
        ds, dw = _conv_grads(ext_s, jnp.concatenate([own, nxt], axis=0), w_ref[...])
        dpm_ref[:, :3 * D] = ds.astype(BF16)
        dpm_ref[:, 3 * D:] = dab_ref[...].astype(BF16)

        @pl.when(i == 0)
        def _():
            dw_ref[...] = jnp.zeros_like(dw_ref)
        dw_ref[...] += dw

    return pl.pallas_call(
        body, name=name, grid=(n_t,),
        in_specs=[_row_spec(D), _row_spec(D), _row_spec(D), _next_spec(D, T), _next_spec(D, T), _next_spec(D, T),
                  _row_spec(AB_PAD), _row_spec(3 * D), _prev_spec(3 * D, T), _full_spec((4, 3 * D))],
        out_specs=[_row_spec(3 * D + AB_PAD), _full_spec((4, 3 * D))],
        out_shape=[SDS((T, 3 * D + AB_PAD), BF16), SDS((4, 3 * D), F32)],
        compiler_params=_params(1, VMEM_LIMIT))(dcq, dck, dcv, dcq, dck, dcv, dab, pm, pm, w)


def _l2n(x):
    return x * lax.rsqrt(jnp.sum(x * x, axis=-1, keepdims=True) + EPS)


def _softplus(x):
    return jnp.maximum(x, 0.0) + jnp.log1p(jnp.exp(-jnp.abs(x)))


_BNN = (((2,), (1,)), ((0,), (0,)))
_BNT = (((2,), (2,)), ((0,), (0,)))
_BTN = (((1,), (1,)), ((0,), (0,)))


def _bdot(a, b, dims):
    return lax.dot_general(a.astype(BF16), b.astype(BF16), dims, preferred_element_type=F32)


def _bdot3(a, b, dims):
    ah, bh = a.astype(BF16), b.astype(BF16)
    al, bl = (a - ah.astype(F32)).astype(BF16), (b - bh.astype(F32)).astype(BF16)
    d = functools.partial(lax.dot_general, dimension_numbers=dims, preferred_element_type=F32)
    return d(ah, bh) + (d(ah, bl) + d(al, bh))


def _bdot_hi(a, b, dims):
    return lax.dot_general(a, b, dims, precision=HI, preferred_element_type=F32)


def _batched_matmuls(dot):
    @jax.custom_vjp
    def nn(a, b):
        return dot(a, b, _BNN)

    @jax.custom_vjp
    def nt(a, b):
        return dot(a, b, _BNT)

    @jax.custom_vjp
    def tn(a, b):
        return dot(a, b, _BTN)

    nn.defvjp(lambda a, b: (dot(a, b, _BNN), (a, b)), lambda r, g: (dot(g, r[1], _BNT), dot(r[0], g, _BTN)))
    nt.defvjp(lambda a, b: (dot(a, b, _BNT), (a, b)), lambda r, g: (dot(g, r[1], _BNN), dot(g, r[0], _BTN)))
    tn.defvjp(lambda a, b: (dot(a, b, _BTN), (a, b)), lambda r, g: (dot(r[1], g, _BNT), dot(r[0], g, _BNN)))
    return nn, nt, tn


bmm, bmm_nt, bmm_tn = _batched_matmuls(_bdot)
bmm_hi, _, _ = _batched_matmuls(_bdot_hi)

@jax.custom_vjp
def _neumann_inverse(n):
    C = n.shape[1]
    eye = lax.broadcasted_iota(jnp.int32, n.shape, 1) == lax.broadcasted_iota(jnp.int32, n.shape, 2)
    t = eye.astype(F32) + n
    for _ in range(5):
        n = _bdot3(n, n, _BNN)
        t = t + _bdot3(t, n, _BNN)
    return t


def _neumann_inverse_fwd(n):
    t = _neumann_inverse(n)
    return t, t


def _neumann_inverse_bwd(t, g):
    return (_bdot3(_bdot3(t, g, _BTN), t, _BNT),)


_neumann_inverse.defvjp(_neumann_inverse_fwd, _neumann_inverse_bwd)


@jax.custom_vjp
def _saved_inverse(n, t):
    return t


_saved_inverse.defvjp(lambda n, t: (t, t), lambda t, g: (_bdot3(_bdot3(t, g, _BTN), t, _BNT), jnp.zeros_like(t)))

DN_NCH = DN_TB // DN_C
DN_NH = 4


def _decay_terms(ab, alog, dtb, first_head, n_heads):
    C = DN_C
    lane = lax.broadcasted_iota(jnp.int32, ab.shape, 1)
    g_all = (-jnp.exp(alog) * _softplus(ab + dtb)).reshape(DN_NCH, C, HD)
    beta_all = jax.nn.sigmoid(ab)
    r = lax.broadcasted_iota(jnp.int32, (DN_NCH, C, C), 1)
    c = lax.broadcasted_iota(jnp.int32, (DN_NCH, C, C), 2)
    gc_all = bmm_hi((r >= c).astype(F32), g_all)
    gc_rows = [gc_all[i].T for i in range(DN_NCH)]
    lane3 = lax.broadcasted_iota(jnp.int32, (DN_NCH, C, HD), 2)
    row = lax.broadcasted_iota(jnp.int32, (HD, C), 0)
    ones = jnp.ones((1, HD), F32)
    gcs, gjs, betas = [], [], []
    for i in range(n_heads):
        h = first_head + i
        gcs.append(jnp.sum(jnp.where(lane3 == h, gc_all, 0.0), axis=2, keepdims=True) * ones)
        gjs.append(jnp.concatenate(
            [jnp.broadcast_to(jnp.sum(jnp.where(row == h, t, 0.0), axis=0, keepdims=True), (C, C))[None] for t in gc_rows],
            axis=0))
        beta = jnp.sum(jnp.where(lane == 8 + h, beta_all, 0.0), axis=1, keepdims=True) * ones
        betas.append(beta.reshape(DN_NCH, C, HD))
    return jnp.concatenate(gcs, axis=0), jnp.concatenate(gjs, axis=0), jnp.concatenate(betas, axis=0)


def _dn_prep(cq, ck, cv, gcum, gj, bb, t_saved=None):
    B, C = cq.shape[0], DN_C
    q = _l2n(jax.nn.silu(cq)) * (HD ** -0.5)
    k = _l2n(jax.nn.silu(ck))
    v = jax.nn.silu(cv)
    r = lax.broadcasted_iota(jnp.int32, (B, C, C), 1)
    c = lax.broadcasted_iota(jnp.int32, (B, C, C), 2)
    incl, strict = r >= c, r > c
    decay = jnp.where(incl, jnp.exp(jnp.where(incl, gcum[:, :, :C] - gj, 0.0)), 0.0)
    kb = k * bb
    n_mat = -jnp.where(strict, bmm_nt(kb, k) * decay, 0.0)
    t_mat = _neumann_inverse(n_mat) if t_saved is None else _saved_inverse(n_mat, t_saved)
    eg = jnp.exp(gcum)
    glast = gcum[:, C - 1:C, :]
    return (bmm(t_mat, v * bb), bmm(t_mat, kb * eg), bmm_nt(q, k) * decay, q * eg, k * jnp.exp(glast - gcum),
            jnp.exp(glast), t_mat)


def _dn_scan_step(u, w, qk, qd, kd, egl, S, onw):
    v_new = u - bmm(w, S)
    o = bmm(qd, S) + bmm(qk, v_new)
    return _rms(o, onw), S * egl + bmm_tn(kd, v_new)


def _to_batch(ref, n_heads):
    return jnp.concatenate([ref[:, i * HD:(i + 1) * HD].astype(F32).reshape(DN_NCH, DN_C, HD) for i in range(n_heads)],
                           axis=0)


def _from_batch(ref, val, n_heads):
    for i in range(n_heads):
        ref[:, i * HD:(i + 1) * HD] = val[i * DN_NCH:(i + 1) * DN_NCH].reshape(DN_TB, HD).astype(ref.dtype)


def _prep_specs(T, rev):
    nb = T // DN_TB
    blk = (lambda n: nb - 1 - n) if rev else (lambda n: n)
    ng = 8 // DN_NH
    head = [pl.BlockSpec((DN_TB, DN_NH * HD), functools.partial(lambda n, h, off: (blk(n), off + h), off=ng * s))
            for s in range(3)]
    ab = pl.BlockSpec((DN_TB, AB_PAD), lambda n, h: (blk(n), 3 * D // AB_PAD))
    row = pl.BlockSpec((1, HD), lambda n, h: (0, 0))
    wide = pl.BlockSpec((DN_TB, DN_NH * HD), lambda n, h: (blk(n), h))
    qk = pl.BlockSpec((DN_NCH, DN_NH, DN_C, DN_C), lambda n, h: (blk(n), h, 0, 0))
    eg = pl.BlockSpec((DN_NCH, DN_NH, 1, HD), lambda n, h: (blk(n), h, 0, 0))
    return nb, ng, head, ab, row, wide, qk, eg


def _dn_prep_fwd(cpre, pm, alog, dtb, name):
    T = cpre.shape[0]
    nb, ng, head, ab, row, wide, qks, egs = _prep_specs(T, False)

    def body(cq_ref, ck_ref, cv_ref, ab_ref, alog_ref, dtb_ref, u_ref, w_ref, qk_ref, qd_ref, kd_ref, e_ref, t_ref):
        gcum, gj, bb = _decay_terms(ab_ref[...], alog_ref[...], dtb_ref[...], pl.program_id(1) * DN_NH, DN_NH)
        u, w, qk, qd, kd, egl, t_mat = _dn_prep(_to_batch(cq_ref, DN_NH), _to_batch(ck_ref, DN_NH),
                                                _to_batch(cv_ref, DN_NH), gcum, gj, bb)
        _from_batch(u_ref, u, DN_NH)
        _from_batch(w_ref, w, DN_NH)
        _from_batch(qd_ref, qd, DN_NH)
        _from_batch(kd_ref, kd, DN_NH)
        for i in range(DN_NH):
            qk_ref[:, i] = qk[i * DN_NCH:(i + 1) * DN_NCH].astype(BF16)
            e_ref[:, i] = egl[i * DN_NCH:(i + 1) * DN_NCH]
            t_ref[:, i] = t_mat[i * DN_NCH:(i + 1) * DN_NCH]

    return pl.pallas_call(
        body, name=name, grid=(nb, ng), in_specs=head + [ab, row, row],
        out_specs=[wide, wide, qks, wide, wide, egs, qks],
        out_shape=[SDS((T, D), F32), SDS((T, D), BF16), SDS((T // DN_C, 8, DN_C, DN_C), BF16), SDS((T, D), BF16),
                   SDS((T, D), BF16), SDS((T // DN_C, 8, 1, HD), F32), SDS((T // DN_C, 8, DN_C, DN_C), F32)],
        compiler_params=_params(2, VMEM_LIMIT))(cpre, cpre, cpre, pm, alog, dtb)


def _dn_prep_bwd(du, dw, dqk, dqd, dkd, degl, t_mat, cpre, pm, alog, dtb, name):
    T = cpre.shape[0]
    nb, ng, head, ab, row, wide, qks, egs = _prep_specs(T, True)

    def body(du_ref, dw_ref, dqk_ref, dqd_ref, dkd_ref, de_ref, t_ref, cq_ref, ck_ref, cv_ref, ab_ref, alog_ref,
             dtb_ref, dcq_ref, dck_ref, dcv_ref, dab_ref, dalog_ref, ddtb_ref):
        n, h = pl.program_id(0), pl.program_id(1)

        @pl.when((n == 0) & (h == 0))
        def _():
            dalog_ref[...] = jnp.zeros_like(dalog_ref)
            ddtb_ref[...] = jnp.zeros_like(ddtb_ref)

        @pl.when(h == 0)
        def _():
            dab_ref[...] = jnp.zeros_like(dab_ref)

        t_saved = jnp.concatenate([t_ref[:, i] for i in range(DN_NH)], axis=0)

        def fwd(cq, ck, cv, ab_v, alog_v, dtb_v):
            gcum, gj, bb = _decay_terms(ab_v, alog_v, dtb_v, h * DN_NH, DN_NH)
            return _dn_prep(cq, ck, cv, gcum, gj, bb, t_saved)[:6]

        _, vjp = jax.vjp(fwd, _to_batch(cq_ref, DN_NH), _to_batch(ck_ref, DN_NH), _to_batch(cv_ref, DN_NH), ab_ref[...],
                         alog_ref[...], dtb_ref[...])
        cot = (_to_batch(du_ref, DN_NH), _to_batch(dw_ref, DN_NH),
               jnp.concatenate([dqk_ref[:, i] for i in range(DN_NH)], axis=0), _to_batch(dqd_ref, DN_NH),
               _to_batch(dkd_ref, DN_NH), jnp.concatenate([de_ref[:, i] for i in range(DN_NH)], axis=0))
        dcq, dck, dcv, dab, dalog, ddtb = vjp(cot)
        _from_batch(dcq_ref, dcq, DN_NH)
        _from_batch(dck_ref, dck, DN_NH)
        _from_batch(dcv_ref, dcv, DN_NH)
        dab_ref[...] += dab
        dalog_ref[...] += dalog
        ddtb_ref[...] += ddtb

    dabspec = pl.BlockSpec((DN_TB, AB_PAD), lambda n, h: (nb - 1 - n, 0))
    return pl.pallas_call(
        body, name=name, grid=(nb, ng),
        in_specs=[wide, wide, qks, wide, wide, egs, qks] + head + [ab, row, row],
        out_specs=[wide, wide, wide, dabspec, row, row],
        out_shape=[SDS((T, D), F32)] * 3 + [SDS((T, AB_PAD), F32)] + [SDS((1, HD), F32)] * 2,
        compiler_params=_params(2, VMEM_LIMIT))(du, dw, dqk, dqd, dkd, degl, t_mat, cpre, cpre, cpre, pm, alog, dtb)


def _scan_specs(T, rev):
    nb = T // DN_TB
    blk = (lambda n: nb - 1 - n) if rev else (lambda n: n)
    wide = pl.BlockSpec((DN_TB, D), lambda n: (blk(n), 0))
    qk = pl.BlockSpec((DN_NCH, 8, DN_C, DN_C), lambda n: (blk(n), 0, 0, 0))
    eg = pl.BlockSpec((DN_NCH, 8, 1, HD), lambda n: (blk(n), 0, 0, 0))
    st = pl.BlockSpec((DN_NCH, 8, HD, HD), lambda n: (blk(n), 0, 0, 0))
    row = pl.BlockSpec((1, HD), lambda n: (0, 0))
    return nb, wide, qk, eg, st, row


def _heads_of(ref, rows):
    return jnp.concatenate([ref[rows, h * HD:(h + 1) * HD].astype(F32)[None] for h in range(8)], axis=0)


def _dn_scan_fwd(u, w, qk, qd, kd, egl, onw, name):
    T = u.shape[0]
    nb, wide, qks, egs, sts, row = _scan_specs(T, False)

    def body(u_ref, w_ref, qk_ref, qd_ref, kd_ref, e_ref, onw_ref, o_ref, st_ref, s_scr):
        @pl.when(pl.program_id(0) == 0)
        def _():
            s_scr[...] = jnp.zeros_like(s_scr)
        S = s_scr[...]
        for c in range(DN_NCH):
            rows = slice(c * DN_C, (c + 1) * DN_C)
            st_ref[c] = S
            o, S = _dn_scan_step(_heads_of(u_ref, rows), _heads_of(w_ref, rows), qk_ref[c].astype(F32),
                                 _heads_of(qd_ref, rows), _heads_of(kd_ref, rows), e_ref[c], S, onw_ref[...])
            for h in range(8):
                o_ref[rows, h * HD:(h + 1) * HD] = o[h]
        s_scr[...] = S

    return pl.pallas_call(
        body, name=name, grid=(nb,), in_specs=[wide, wide, qks, wide, wide, egs, row], out_specs=[wide, sts],
        out_shape=[SDS((T, D), F32), SDS((T // DN_C, 8, HD, HD), F32)],
        scratch_shapes=[pltpu.VMEM((8, HD, HD), F32)],
        compiler_params=_params(1, VMEM_LIMIT))(u, w, qk, qd, kd, egl, onw)


def _dn_scan_bwd(do, u, w, qk, qd, kd, egl, st, onw, name):
    T = u.shape[0]
    nb, wide, qks, egs, sts, row = _scan_specs(T, True)

    def body(do_ref, u_ref, w_ref, qk_ref, qd_ref, kd_ref, e_ref, st_ref, onw_ref,
             du_ref, dw_ref, dqk_ref, dqd_ref, dkd_ref, de_ref, donw_ref, ds_scr):
        @pl.when(pl.program_id(0) == 0)
        def _():
            ds_scr[...] = jnp.zeros_like(ds_scr)
            donw_ref[...] = jnp.zeros_like(donw_ref)
        dS = ds_scr[...]
        donw = jnp.zeros((1, HD), F32)
        for c in reversed(range(DN_NCH)):
            rows = slice(c * DN_C, (c + 1) * DN_C)
            _, vjp = jax.vjp(_dn_scan_step, _heads_of(u_ref, rows), _heads_of(w_ref, rows), qk_ref[c].astype(F32),
                             _heads_of(qd_ref, rows), _heads_of(kd_ref, rows), e_ref[c], st_ref[c], onw_ref[...])
            du, dw, dqk, dqd, dkd, de, dS, dn = vjp((_heads_of(do_ref, rows), dS))
            for h in range(8):
                cols = slice(h * HD, (h + 1) * HD)
                du_ref[rows, cols] = du[h]
                dw_ref[rows, cols] = dw[h]
                dqd_ref[rows, cols] = dqd[h]
                dkd_ref[rows, cols] = dkd[h]
            dqk_ref[c] = dqk
            de_ref[c] = de
            donw += dn
        ds_scr[...] = dS
        donw_ref[...] += donw

    return pl.pallas_call(
        body, name=name, grid=(nb,), in_specs=[wide, wide, wide, qks, wide, wide, egs, sts, row],
        out_specs=[wide, wide, qks, wide, wide, egs, row],
        out_shape=[SDS((T, D), F32), SDS((T, D), F32), SDS((T // DN_C, 8, DN_C, DN_C), F32), SDS((T, D), F32),
                   SDS((T, D), F32), SDS((T // DN_C, 8, 1, HD), F32), SDS((1, HD), F32)],
        scratch_shapes=[pltpu.VMEM((8, HD, HD), F32)],
        compiler_params=_params(1, VMEM_LIMIT))(do, u, w, qk, qd, kd, egl, st, onw)


def _adamw(w, g, m, v, name):
    R, C = w.shape
    tr = 256 if R % 256 == 0 and R > 256 else R
    tc = 256 if tr == R and R > 256 and C % 256 == 0 else C
    c1 = 1.0 - ADAM_B1 ** ADAM_STEP
    c2 = 1.0 - ADAM_B2 ** ADAM_STEP

    def body(w_ref, g_ref, m_ref, v_ref, d_ref, nm_ref, nv_ref):
        gv = g_ref[...]
        nm = ADAM_B1 * m_ref[...] + (1.0 - ADAM_B1) * gv
        nv = ADAM_B2 * v_ref[...] + (1.0 - ADAM_B2) * (gv * gv)
        nm_ref[...] = nm
        nv_ref[...] = nv
        d_ref[...] = -ADAM_LR * ((nm / c1) / (jnp.sqrt(nv / c2) + ADAM_EPS) + ADAM_WD * w_ref[...])

    spec = pl.BlockSpec((tr, tc), lambda i, j: (i, j))
    return pl.pallas_call(
        body, name=name, grid=(R // tr, C // tc), in_specs=[spec] * 4, out_specs=[spec] * 3,
        out_shape=[SDS((R, C), F32)] * 3, compiler_params=_params(2, VMEM_LIMIT))(w, g, m, v)


def _local_step(x, mem, target, wts, sm):
    kinds = [i % 3 for i in range(DEPTH)]
    mnw = sm["mem_norm_w"].reshape(1, D)
    kv, wkv = None, None
    saved, blocks = [], []
    for i, kind in enumerate(kinds):
        j = i // 3
        npre = sm["norm_pre"][i].reshape(1, D)
        npost = sm["norm_post"][i].reshape(1, D)
        mix, gate, wo_after = wts["blocks"](i, x)
        pm, pg, h = _inproj_fwd(x, npre, mix, gate, kind == 2, f"inproj_fwd_{i}")
        if kv is None:
            wkv = wts["wkv_after"](h)
            kv = _memkv_fwd(mem, mnw, wkv)
        extra = None
        if kind == 0:
            bs3 = jnp.broadcast_to(sm["a_b_s"][j][:, :, None], (8, HD, HD))
            ymix = _gmlp_fwd(pm, sm["a_ln_w"][j].reshape(1, D), sm["a_ln_b"][j].reshape(1, D), sm["a_w_s"][j], bs3,
                             f"gmlp_fwd_{i}")
            extra = bs3
        elif kind == 1:
            ymix = _sconv_fwd(pm, sm["b_conv_w"][j], f"sconv_fwd_{i}")
        else:
            cpre = _dnconv_fwd(pm, sm["c_conv_w"][j], f"dnconv_fwd_{i}")
            alog = jnp.pad(sm["c_a_log"][j], (0, HD - 8)).reshape(1, HD)
            dtb = jnp.pad(sm["c_dt_bias"][j], (0, HD - 8)).reshape(1, HD)
            onw = sm["c_o_norm_w"][j].reshape(1, HD)
            *prep, t_mat = _dn_prep_fwd(cpre, pm, alog, dtb, f"dn_prep_fwd_{i}")
            ymix, st = _dn_scan_fwd(*prep, onw, f"dn_scan_fwd_{i}")
            extra = (cpre, prep, t_mat, st, alog, dtb, onw)
        wo = wo_after(ymix)
        blocks.append((mix, gate, wo))
        layer_in = x
        if i < DEPTH - 1:
            ycat, o, x = _gate_outproj_fwd(ymix, pg, kv, wo, x, npost, None, f"gate_outproj_fwd_{i}")
        else:
            ycat, o, loss, dx = _gate_outproj_fwd(ymix, pg, kv, wo, x, npost, target, f"gate_outproj_fwd_{i}")
        saved.append((layer_in, h, pm, pg, ymix, ycat, o, extra))

    g = {"wm": [None] * DEPTH, "wg": [None] * DEPTH, "wo": [None] * DEPTH, "norm_pre": [None] * DEPTH,
         "norm_post": [None] * DEPTH}
    dkv = jnp.zeros((N_MEM, 2 * D_XA), F32)
    sent = 0.0
    for i in reversed(range(DEPTH)):
        kind, j = kinds[i], i // 3
        xi, h, pm, pg, ymix, ycat, o, extra = saved[i]
        npre = sm["norm_pre"][i].reshape(1, D)
        npost = sm["norm_post"][i].reshape(1, D) + sent
        dobf, g["norm_post"][i], dymix, dpg, dkv = _outproj_gate_bwd(dx, o, npost, blocks[i][2], ymix, pg, kv, dkv,
                                                                     f"outproj_gate_bwd_{i}")
        g["wo"][i] = _matmul_tn(ycat, dobf, f"dwo_{i}")
        if kind == 0:
            dpm, dlnw, dlnb, dws, dbs3 = _gmlp_bwd(dymix, pm, sm["a_ln_w"][j].reshape(1, D),
                                                   sm["a_ln_b"][j].reshape(1, D), sm["a_w_s"][j], extra,
                                                   f"gmlp_bwd_{i}")
            g.setdefault("a_ln_w", {})[j] = dlnw.reshape(D)
            g.setdefault("a_ln_b", {})[j] = dlnb.reshape(D)
            g.setdefault("a_w_s", {})[j] = dws
            g.setdefault("a_b_s", {})[j] = dbs3[:, :, 0]
        elif kind == 1:
            dpm, dcw = _sconv_bwd(dymix, pm, sm["b_conv_w"][j], f"sconv_bwd_{i}")
            g.setdefault("b_conv_w", {})[j] = dcw
        else:
            cpre, prep, t_mat, st, alog, dtb, onw = extra
            *dprep, donw = _dn_scan_bwd(dymix, *prep, st, onw, f"dn_scan_bwd_{i}")
            dcq, dck, dcv, dab, dalog, ddtb = _dn_prep_bwd(*dprep, t_mat, cpre, pm, alog, dtb, f"dn_prep_bwd_{i}")
            dpm, dcw = _dnconv_bwd(dcq, dck, dcv, dab, pm, sm["c_conv_w"][j], f"dnconv_bwd_{i}")
            g.setdefault("c_conv_w", {})[j] = dcw
            g.setdefault("c_a_log", {})[j] = dalog[0, :8]
            g.setdefault("c_dt_bias", {})[j] = ddtb[0, :8]
            g.setdefault("c_o_norm_w", {})[j] = donw[0]
        if kind == 2:
            g["wm"][i] = _matmul_tn(dpm, h, f"dwm_{i}")
            g["wg"][i] = _matmul_tn(dpg, h, f"dwg_{i}")
        else:
            g["wm"][i] = _matmul_tn(h, dpm, f"dwm_{i}", GRAD_TILE[kind])
            g["wg"][i] = _matmul_tn(h, dpg, f"dwg_{i}", GRAD_TILE[kind])
        sent = wts["layer_done"](i, g)
        dx, g["norm_pre"][i] = _inproj_bwd(dpm, dpg, xi, npre + sent, blocks[i][0], blocks[i][1], kind == 2, dx,
                                           f"inproj_bwd_{i}")
    g["mem_norm_w"], g["wkv"] = _memkv_bwd(mem, mnw, wkv, dkv)
    return loss[0, 0], dx, g


ANY = pl.BlockSpec(memory_space=pl.ANY)


def _place():
    return lax.axis_index("x"), lax.axis_index("y"), lax.axis_index("c")


def _add(a, b, name):
    def body(a_ref, b_ref, o_ref):
        o_ref[...] = a_ref[...] + b_ref[...]

    return pl.pallas_call(body, name=name, out_shape=SDS(a.shape, a.dtype), compiler_params=_params(0, VMEM_LIMIT))(a, b)


def _sum4(own, land):
    def body(own_ref, l_ref, o_ref):
        chip = 2 * lax.axis_index("x") + lax.axis_index("y")
        acc = jnp.where(chip == 0, own_ref[...], l_ref[0])
        for s in range(1, 4):
            acc = acc + jnp.where(chip == s, own_ref[...], l_ref[s])
        o_ref[...] = acc

    return pl.pallas_call(body, name="sum_small", out_shape=SDS(own.shape, own.dtype),
                          compiler_params=_params(0, VMEM_LIMIT))(own, land)


C_ROWS = 1312
_REDUCE_CHUNK = {512: 256, 2560: 640}
W_CLASSES = {"in0": (512, 256)}


def _chunk_list(specs):
    return [(k, r, chunk) for k, (half, chunk) in enumerate(specs) for r in range(0, half, chunk)]


def _gather_classes(arrs, specs, vec):
    n = len(arrs)
    chunks = _chunk_list(specs)
    nc = len(chunks)

    def body(*refs):
        ins, vec_ref, outs, ov_ref = refs[:n], refs[n], refs[n + 1:2 * n + 1], refs[2 * n + 1]
        ici_send, ici_recv, d2d_send, d2d_recv, vec_send, vec_recv = refs[2 * n + 2:]
        x, y, c = _place()
        chip = 2 * x + y
        peers = [(1 - x, y), (x, 1 - y), (1 - x, 1 - y)]

        def rows(ci, half):
            k, r, cnt = chunks[ci]
            return k, pl.ds(half * specs[k][0] + r, cnt)

        def over_ici(j, ci, slab):
            px, py = peers[j]
            k, rs = rows(ci, c)
            return pltpu.make_async_remote_copy(
                src_ref=ins[k].at[rs], dst_ref=outs[k].at[slab, rs], send_sem=ici_send.at[j * nc + ci],
                recv_sem=ici_recv.at[j * nc + ci], device_id=(px, py, c), device_id_type=MESH)

        def over_d2d(j, ci, half):
            px, py = peers[j]
            k, rs = rows(ci, half)
            where = outs[k].at[2 * px + py, rs]
            return pltpu.make_async_remote_copy(
                src_ref=where, dst_ref=where, send_sem=d2d_send.at[j * nc + ci], recv_sem=d2d_recv.at[j * nc + ci],
                device_id=(x, y, 1 - c), device_id_type=MESH)

        def small(j, slab):
            px, py = peers[j]
            return pltpu.make_async_remote_copy(
                src_ref=vec_ref, dst_ref=ov_ref.at[slab], send_sem=vec_send.at[j], recv_sem=vec_recv.at[j],
                device_id=(px, py, c), device_id_type=MESH)

        sends = [small(j, chip) for j in range(3)] + [over_ici(j, ci, chip) for ci in range(nc) for j in range(3)]
        for cp in sends:
            cp.start()
        forwards = []
        for ci in range(nc):
            for j, (px, py) in enumerate(peers):
                over_ici(j, ci, 2 * px + py).wait_recv()
                forwards.append(over_d2d(j, ci, c))
                forwards[-1].start()
        for ci in range(nc):
            for j in range(3):
                over_d2d(j, ci, 1 - c).wait_recv()
        for j, (px, py) in enumerate(peers):
            small(j, 2 * px + py).wait_recv()
        for cp in sends + forwards:
            cp.wait_send()

    dma = pltpu.SemaphoreType.DMA
    return pl.pallas_call(
        body, name="gather_weights", in_specs=[ANY] * (n + 1), out_specs=[ANY] * (n + 1),
        out_shape=[SDS((4,) + a.shape, a.dtype) for a in arrs] + [SDS((4,) + vec.shape, vec.dtype)],
        scratch_shapes=[dma((3 * nc,)), dma((3 * nc,)), dma((3 * nc,)), dma((3 * nc,)), dma((3,)), dma((3,))])(*arrs, vec)


def _swap_classes(grads, specs, small):
    n = len(grads)
    chunks = _chunk_list(specs)

    def body(*refs):
        ins, s_ref, outs, os_ref, send_sems, recv_sems = refs[:n], refs[n], refs[n + 1:2 * n + 1], *refs[2 * n + 1:]
        x, y, c = _place()
        copies = []
        for s in range(4):
            for k, r, cnt in chunks:
                copies.append(pltpu.make_async_remote_copy(
                    src_ref=ins[k].at[s, 1 - c, pl.ds(r, cnt)], dst_ref=outs[k].at[s, pl.ds(r, cnt)],
                    send_sem=send_sems.at[len(copies)], recv_sem=recv_sems.at[len(copies)],
                    device_id=(x, y, 1 - c), device_id_type=MESH))
        copies.append(pltpu.make_async_remote_copy(
            src_ref=s_ref, dst_ref=os_ref, send_sem=send_sems.at[len(copies)], recv_sem=recv_sems.at[len(copies)],
            device_id=(x, y, 1 - c), device_id_type=MESH))
        for cp in copies:
            cp.start()
        for cp in copies:
            cp.wait_recv()
        for cp in copies:
            cp.wait_send()

    m = 4 * len(chunks) + 1
    return pl.pallas_call(
        body, name="swap_halves", in_specs=[ANY] * (n + 1), out_specs=[ANY] * (n + 1),
        out_shape=[SDS((4, g.shape[2], g.shape[3]), F32) for g in grads] + [SDS(small.shape, F32)],
        scratch_shapes=[pltpu.SemaphoreType.DMA((m,)), pltpu.SemaphoreType.DMA((m,))])(*grads, small)


def _pair_sum_class(g, other, chunk, name):
    _, _, half, w = g.shape

    def body(g_ref, o_ref, pb_ref, own_ref):
        x, y, c = _place()
        v = jnp.where(c == 0, g_ref[0], g_ref[1]) + o_ref[...]
        pb_ref[...] = v.astype(BF16)

        @pl.when(pl.program_id(1) == 2 * x + y)
        def _():
            own_ref[...] = v

    return pl.pallas_call(
        body, name=name, grid=(half // chunk, 4),
        in_specs=[pl.BlockSpec((None, 2, chunk, w), lambda i, s: (s, 0, i, 0)),
                  pl.BlockSpec((None, chunk, w), lambda i, s: (s, i, 0))],
        out_specs=[pl.BlockSpec((None, chunk, w), lambda i, s: (s, i, 0)), pl.BlockSpec((chunk, w), lambda i, s: (i, 0))],
        out_shape=[SDS((4, half, w), BF16), SDS((half, w), F32)],
        compiler_params=_params(2, VMEM_LIMIT))(g, other)


def _exchange_classes(pbs, specs, ps):
    n = len(pbs)
    chunks = _chunk_list(specs)
    per_peer = len(chunks) + 1

    def body(*refs):
        ins, ps_ref, outs, ls_ref, send_sems, recv_sems = refs[:n], refs[n], refs[n + 1:2 * n + 1], *refs[2 * n + 1:]
        x, y, c = _place()
        chip = 2 * x + y
        peers = [(1 - x, y), (x, 1 - y), (1 - x, 1 - y)]

        def copies(slab_of):
            out = []
            for j, (px, py) in enumerate(peers):
                for k, r, cnt in chunks:
                    out.append(pltpu.make_async_remote_copy(
                        src_ref=ins[k].at[2 * px + py, pl.ds(r, cnt)], dst_ref=outs[k].at[slab_of(j), pl.ds(r, cnt)],
                        send_sem=send_sems.at[len(out)], recv_sem=recv_sems.at[len(out)], device_id=(px, py, c),
                        device_id_type=MESH))
                out.append(pltpu.make_async_remote_copy(
                    src_ref=ps_ref, dst_ref=ls_ref.at[slab_of(j)], send_sem=send_sems.at[len(out)],
                    recv_sem=recv_sems.at[len(out)], device_id=(px, py, c), device_id_type=MESH))
            return out

        sends = copies(lambda j: chip)
        for cp in sends:
            cp.start()
        for cp in copies(lambda j: 2 * peers[j][0] + peers[j][1]):
            cp.wait_recv()
        for cp in sends:
            cp.wait_send()

    m = 3 * per_peer
    return pl.pallas_call(
        body, name="chip_exchange", in_specs=[ANY] * (n + 1), out_specs=[ANY] * (n + 1),
        out_shape=[SDS(p.shape, BF16) for p in pbs] + [SDS((4,) + ps.shape, F32)],
        scratch_shapes=[pltpu.SemaphoreType.DMA((m,)), pltpu.SemaphoreType.DMA((m,))])(*pbs, ps)


def _chip_sum_class(own, land, chunk, name):
    half, w = own.shape

    def body(own_ref, l_ref, o_ref):
        chip = 2 * lax.axis_index("x") + lax.axis_index("y")
        acc = jnp.where(chip == 0, own_ref[...], l_ref[0].astype(F32))
        for s in range(1, 4):
            acc = acc + jnp.where(chip == s, own_ref[...], l_ref[s].astype(F32))
        o_ref[...] = acc

    return pl.pallas_call(
        body, name=name, grid=(half // chunk,),
        in_specs=[pl.BlockSpec((chunk, w), lambda i: (i, 0)), pl.BlockSpec((4, chunk, w), lambda i: (0, i, 0))],
        out_specs=pl.BlockSpec((chunk, w), lambda i: (i, 0)), out_shape=SDS((half, w), F32),
        compiler_params=_params(1, VMEM_LIMIT))(own, land)


def _share_classes(rs, specs):
    n = len(rs)
    chunks = _chunk_list(specs)

    def body(*refs):
        ins, outs, send_sems, recv_sems = refs[:n], refs[n:2 * n], *refs[2 * n:]
        x, y, c = _place()
        copies = [pltpu.make_async_remote_copy(
            src_ref=ins[k].at[pl.ds(r, cnt)], dst_ref=outs[k].at[pl.ds(r, cnt)], send_sem=send_sems.at[i],
            recv_sem=recv_sems.at[i], device_id=(x, y, 1 - c), device_id_type=MESH)
            for i, (k, r, cnt) in enumerate(chunks)]
        for cp in copies:
            cp.start()
        for cp in copies:
            cp.wait_recv()
        for cp in copies:
            cp.wait_send()

    m = len(chunks)
    return pl.pallas_call(
        body, name="share_half", in_specs=[ANY] * n, out_specs=[ANY] * n, out_shape=[SDS(r.shape, F32) for r in rs],
        scratch_shapes=[pltpu.SemaphoreType.DMA((m,)), pltpu.SemaphoreType.DMA((m,))])(*rs)


_HBM = pl.BlockSpec(memory_space=pltpu.HBM)
_SEM = pl.BlockSpec(memory_space=pltpu.SEMAPHORE)
_EFFECT = pltpu.SideEffectType.DATAFLOW_SIDE_EFFECTING


def _chip_peers():
    x, y, c = _place()
    return [(1 - x, y, c), (x, 1 - y, c), (1 - x, 1 - y, c)]


def _send_shard_start(v, name):
    def body(v_ref, land_ref, send_sems, recv_sems, v_thru, land_thru, token):
        x, y, c = _place()
        for j, peer in enumerate(_chip_peers()):
            pltpu.make_async_remote_copy(src_ref=v_ref, dst_ref=land_ref.at[2 * x + y], send_sem=send_sems.at[j],
                                         recv_sem=recv_sems.at[j], device_id=peer, device_id_type=MESH).start()
        token[...] = jnp.zeros_like(token)

    land_shape = (4,) + v.shape
    return pl.pallas_call(
        body, name=name,
        out_shape=(pltpu.SemaphoreType.DMA((3,)), pltpu.SemaphoreType.DMA((3,)), pltpu.HBM(v.shape, v.dtype),
                   pltpu.HBM(land_shape, v.dtype), SDS((8, 128), F32)),
        in_specs=(_HBM, _HBM), out_specs=(_SEM, _SEM, _HBM, _HBM, pl.BlockSpec(memory_space=pltpu.VMEM)),
        input_output_aliases={0: 2, 1: 3}, compiler_params=pltpu.CompilerParams(has_side_effects=_EFFECT),
    )(pltpu.with_memory_space_constraint(v, pltpu.HBM),
      pltpu.with_memory_space_constraint(lax.empty(land_shape, v.dtype), pltpu.HBM))


def _xor_peer(r):
    x, y, c = _place()
    return (1 - x if (r >> 2) & 1 else x, 1 - y if (r >> 1) & 1 else y, 1 - c if r & 1 else c)


def _send_pieces_start(parts, name):
    n = len(parts)

    def body(*refs):
        ins, lands = refs[:n], refs[n:2 * n]
        send_sems, recv_sems = refs[2 * n:2 * n + 2]
        token = refs[-1]
        x, y, c = _place()
        for r in range(1, 8):
            px, py, pc = _xor_peer(r)
            for k in range(n):
                pltpu.make_async_remote_copy(
                    src_ref=ins[k].at[2 * px + py, pc], dst_ref=lands[k].at[4 * x + 2 * y + c],
                    send_sem=send_sems.at[(r - 1) * n + k], recv_sem=recv_sems.at[(r - 1) * n + k],
                    device_id=(px, py, pc), device_id_type=MESH).start()
        token[...] = jnp.zeros_like(token)

    land_shapes = [(8,) + p.shape[2:] for p in parts]
    hbm = [pltpu.HBM(p.shape, p.dtype) for p in parts] + [pltpu.HBM(s, p.dtype) for s, p in zip(land_shapes, parts)]
    operands = [pltpu.with_memory_space_constraint(p, pltpu.HBM) for p in parts]
    operands += [pltpu.with_memory_space_constraint(lax.empty(s, p.dtype), pltpu.HBM) for s, p in zip(land_shapes, parts)]
    return pl.pallas_call(
        body, name=name,
        out_shape=(pltpu.SemaphoreType.DMA((7 * n,)), pltpu.SemaphoreType.DMA((7 * n,)), *hbm, SDS((8, 128), F32)),
        in_specs=(_HBM,) * (2 * n), out_specs=(_SEM, _SEM) + (_HBM,) * (2 * n) + (pl.BlockSpec(memory_space=pltpu.VMEM),),
        input_output_aliases={i: 2 + i for i in range(2 * n)},
        compiler_params=pltpu.CompilerParams(has_side_effects=_EFFECT))(*operands)


def _send_pieces_wait(started, after, name):
    send_sems, recv_sems, *thru, _ = started
    n = len(thru) // 2

    def body(*refs):
        ins, lands = refs[:n], refs[n:2 * n]
        send_sems, recv_sems = refs[2 * n:2 * n + 2]
        for r in range(1, 8):
            px, py, pc = _xor_peer(r)
            for k in range(n):
                copy = pltpu.make_async_remote_copy(
                    src_ref=ins[k].at[2 * px + py, pc], dst_ref=lands[k].at[4 * px + 2 * py + pc],
                    send_sem=send_sems.at[(r - 1) * n + k], recv_sem=recv_sems.at[(r - 1) * n + k],
                    device_id=(px, py, pc), device_id_type=MESH)
                copy.wait_send()
                copy.wait_recv()

    return pl.pallas_call(
        body, name=name, out_shape=tuple(pltpu.HBM(t.shape, t.dtype) for t in thru),
        in_specs=(_HBM,) * (2 * n) + (_SEM, _SEM, pl.BlockSpec(memory_space=pl.ANY)), out_specs=(_HBM,) * (2 * n),
        input_output_aliases={i: i for i in range(2 * n)},
        compiler_params=pltpu.CompilerParams(has_side_effects=_EFFECT))(*thru, send_sems, recv_sems, after)[n:]


def _sum8_class(own, land, chunk, name):
    rows, w = own.shape

    def body(own_ref, l_ref, o_ref):
        x, y, c = _place()
        me = 4 * x + 2 * y + c
        acc = jnp.where(me == 0, own_ref[...], l_ref[0].astype(F32))
        for d in range(1, 8):
            acc = acc + jnp.where(me == d, own_ref[...], l_ref[d].astype(F32))
        o_ref[...] = acc

    return pl.pallas_call(
        body, name=name, grid=(rows // chunk,),
        in_specs=[pl.BlockSpec((chunk, w), lambda i: (i, 0)), pl.BlockSpec((8, chunk, w), lambda i: (0, i, 0))],
        out_specs=pl.BlockSpec((chunk, w), lambda i: (i, 0)), out_shape=SDS((rows, w), F32),
        compiler_params=_params(1, VMEM_LIMIT))(own, land)


def _send_shard_wait(send_sems, recv_sems, v_thru, land_thru, after, name):
    def body(v_ref, land_ref, send_sems, recv_sems, after_ref, v_dead, got_ref):
        for j, (px, py, pc) in enumerate(_chip_peers()):
            copy = pltpu.make_async_remote_copy(src_ref=v_ref, dst_ref=land_ref.at[2 * px + py], send_sem=send_sems.at[j],
                                                recv_sem=recv_sems.at[j], device_id=(px, py, pc), device_id_type=MESH)
            copy.wait_send()
            copy.wait_recv()

    return pl.pallas_call(
        body, name=name,
        out_shape=(pltpu.HBM(v_thru.shape, v_thru.dtype), pltpu.HBM(land_thru.shape, land_thru.dtype)),
        in_specs=(_HBM, _HBM, _SEM, _SEM, pl.BlockSpec(memory_space=pl.ANY)), out_specs=(_HBM, _HBM),
        input_output_aliases={0: 0, 1: 1}, compiler_params=pltpu.CompilerParams(has_side_effects=_EFFECT),
    )(v_thru, land_thru, send_sems, recv_sems, after)[1]


_SMALL = ["mem_norm_w", "norm_pre", "norm_post", "a_ln_w", "a_ln_b", "a_w_s", "a_b_s", "b_conv_w", "c_conv_w",
          "c_a_log", "c_dt_bias", "c_o_norm_w"]
_SMALL_SHAPES = {"mem_norm_w": (D,), "norm_pre": (4, D), "norm_post": (4, D), "a_ln_w": (2, D), "a_ln_b": (2, D),
                 "a_w_s": (2, 8, HD, HD), "a_b_s": (2, 8, HD), "b_conv_w": (1, 3, D), "c_conv_w": (1, 4, 3 * D),
                 "c_a_log": (1, 8), "c_dt_bias": (1, 8), "c_o_norm_w": (1, HD)}
_SHARDED_SMALL = {"a_ln_w": D // 4, "a_ln_b": D // 4, "b_conv_w": D // 4, "c_conv_w": 3 * D // 4}
_SMALL_ROWS = 288


def _size(shape):
    n = 1
    for d in shape:
        n *= d
    return n


def kernel(x, mem, mem_norm_w, w_mem_kv, norm_pre, norm_post, w_out, a_w_in, a_ln_w, a_ln_b, a_w_s, a_b_s, b_w_in, b_conv_w, c_w_in, c_conv_w, c_a_log, c_dt_bias, c_o_norm_w, loss_target, m_mem_norm_w, m_w_mem_kv, m_norm_pre, m_norm_post, m_w_out, m_a_w_in, m_a_ln_w, m_a_ln_b, m_a_w_s, m_a_b_s, m_b_w_in, m_b_conv_w, m_c_w_in, m_c_conv_w, m_c_a_log, m_c_dt_bias, m_c_o_norm_w, v_mem_norm_w, v_w_mem_kv, v_norm_pre, v_norm_post, v_w_out, v_a_w_in, v_a_ln_w, v_a_ln_b, v_a_w_s, v_a_b_s, v_b_w_in, v_b_conv_w, v_c_w_in, v_c_conv_w, v_c_a_log, v_c_dt_bias, v_c_o_norm_w):
    names = ["mem_norm_w", "w_mem_kv", "norm_pre", "norm_post", "w_out", "a_w_in", "a_ln_w", "a_ln_b", "a_w_s", "a_b_s",
             "b_w_in", "b_conv_w", "c_w_in", "c_conv_w", "c_a_log", "c_dt_bias", "c_o_norm_w"]
    w = dict(zip(names, [mem_norm_w, w_mem_kv, norm_pre, norm_post, w_out, a_w_in, a_ln_w, a_ln_b, a_w_s, a_b_s, b_w_in,
                         b_conv_w, c_w_in, c_conv_w, c_a_log, c_dt_bias, c_o_norm_w]))
    m = dict(zip(names, [m_mem_norm_w, m_w_mem_kv, m_norm_pre, m_norm_post, m_w_out, m_a_w_in, m_a_ln_w, m_a_ln_b, m_a_w_s,
                         m_a_b_s, m_b_w_in, m_b_conv_w, m_c_w_in, m_c_conv_w, m_c_a_log, m_c_dt_bias, m_c_o_norm_w]))
    v = dict(zip(names, [v_mem_norm_w, v_w_mem_kv, v_norm_pre, v_norm_post, v_w_out, v_a_w_in, v_a_ln_w, v_a_ln_b, v_a_w_s,
                         v_a_b_s, v_b_w_in, v_b_conv_w, v_c_w_in, v_c_conv_w, v_c_a_log, v_c_dt_bias, v_c_o_norm_w]))
    chip = 2 * lax.axis_index("x") + lax.axis_index("y")

    def rows_of_ct(a):
        return a[0].T

    def with_mine(gathered, own):
        return lax.dynamic_update_slice(gathered, own[None], (chip,) + (0,) * own.ndim)

    first = [a_w_in[0].astype(BF16)]
    vec = jnp.concatenate([a_ln_w.reshape(-1), a_ln_b.reshape(-1), b_conv_w.reshape(-1), c_conv_w.reshape(-1)])
    vec = jnp.pad(vec, (0, 8 * D - vec.shape[0])).reshape(8, D)
    ga0, gvec = _gather_classes(first, [W_CLASSES[k] for k in W_CLASSES], vec)
    ga0, gvec = with_mine(ga0, first[0]), with_mine(gvec, vec)
    gv = gvec.reshape(4, 8 * D)
    w_in = {1: b_w_in[0], 2: jnp.pad(rows_of_ct(c_w_in), ((0, C_ROWS - 1284), (0, 0))), 3: a_w_in[1]}
    order = [("kv", 0, w_mem_kv), ("out", 0, w_out[0])]
    for i in range(1, DEPTH):
        order += [("in", i, w_in[i]), ("out", i, w_out[i])]
    later, sent, started = {}, {}, 0.0 * ga0[0, 0, 0].astype(F32)
    for kind, i, a in order:
        later[kind, i] = (a + started).astype(BF16)
        sent[kind, i] = _send_shard_start(later[kind, i], f"send_w_{kind}_{i}")
        started = sent[kind, i][4][0, 0]

    def arrived(k, after):
        return with_mine(_send_shard_wait(*sent[k][:4], after, f"wait_w_{k[0]}_{k[1]}"), later[k])

    def blocks(i, after):
        def wo_after(later_value):
            return arrived(("out", i), later_value).reshape(D_CAT, D)

        if i == 0:
            return [ga0[0], ga0[1]], [ga0[2], ga0[3]], wo_after
        got = arrived(("in", i), after)
        if i == 1:
            return [got[0], got[1], got[2][:, :512]], [got[2][:, 512:], got[3]], wo_after
        if i == 3:
            return [got[0], got[1]], [got[2], got[3]], wo_after
        fct = got[:, :1284].reshape(5136, D)
        c_ab = jnp.concatenate([fct[3 * D:3 * D + 16], jnp.zeros((AB_PAD - 16, D), BF16)], axis=0)
        return [fct[:3 * D], c_ab], [fct[3 * D + 16:]], wo_after
    sm = {"mem_norm_w": mem_norm_w, "norm_pre": norm_pre + started, "norm_post": norm_post, "a_w_s": a_w_s, "a_b_s": a_b_s,
          "c_a_log": c_a_log, "c_dt_bias": c_dt_bias, "c_o_norm_w": c_o_norm_w,
          "a_ln_w": gv[:, 0:512].reshape(4, 2, 256).transpose(1, 0, 2).reshape(2, D),
          "a_ln_b": gv[:, 512:1024].reshape(4, 2, 256).transpose(1, 0, 2).reshape(2, D),
          "b_conv_w": gv[:, 1024:1792].reshape(4, 1, 3, 256).transpose(1, 2, 0, 3).reshape(1, 3, D),
          "c_conv_w": gv[:, 1792:4864].reshape(4, 1, 4, 768).transpose(1, 2, 0, 3).reshape(1, 4, 3 * D)}
    wts = {"wkv_after": lambda after: arrived(("kv", 0), after).reshape(D, 2 * D_XA), "blocks": blocks}

    def layer_grads(i, g):
        if i % 3 == 2:
            gct = jnp.concatenate([g["wm"][i][:3 * D + 16], g["wg"][i]], axis=0).reshape(4, 1284, D)
            w_in = jnp.pad(gct, ((0, 0), (0, C_ROWS - 1284), (0, 0)))
        else:
            w_in = jnp.concatenate([g["wm"][i], g["wg"][i]], axis=0).reshape(4, -1, GRAD_TILE[i % 3])
        out = {f"in{i}": w_in, f"out{i}": g["wo"][i].reshape(4, 384, D)}
        return {k: a.reshape(4, 2, a.shape[1] // 2, a.shape[2]) for k, a in out.items()}

    pending = {}

    def layer_done(i, g):
        halves = layer_grads(i, g)
        started = _send_pieces_start([h.astype(BF16) for h in halves.values()], f"send_grads_{i}")
        pending[i] = (started, halves)
        return started[-1][0, 0]

    wts["layer_done"] = layer_done

    loss, dx, g = _local_step(x[0], mem[0], loss_target[0], wts, sm)
    loss = lax.psum(loss, ("x", "y", "c"))

    core = lax.axis_index("c")
    mine, specs = {}, {}
    for i in reversed(range(DEPTH)):
        started, halves = pending[i]
        lands = _send_pieces_wait(started, dx, f"wait_grads_{i}")
        for (k, h), land in zip(halves.items(), lands):
            own = lax.dynamic_index_in_dim(lax.dynamic_index_in_dim(h, chip, 0, False), core, 0, False)
            specs[k] = (own.shape[0], _REDUCE_CHUNK.get(own.shape[0], own.shape[0]))
            mine[k] = _sum8_class(own, land, specs[k][1], f"sum8_{k}")

    first = {"kv": g["wkv"].reshape(4, 2, 128, D)}
    first_specs = [(h.shape[2], _REDUCE_CHUNK.get(h.shape[2], h.shape[2])) for h in first.values()]
    halves = list(first.values())
    gs = {"mem_norm_w": g["mem_norm_w"], "norm_pre": jnp.concatenate(g["norm_pre"]),
          "norm_post": jnp.concatenate(g["norm_post"])}
    for n in _SMALL[3:]:
        gs[n] = jnp.stack([g[n][j] for j in sorted(g[n])])
    flat = jnp.concatenate([gs[n].reshape(-1) for n in _SMALL])
    small = jnp.pad(flat, (0, _SMALL_ROWS * D - flat.shape[0])).reshape(_SMALL_ROWS, D)
    *others, other_small = _swap_classes(halves, first_specs, small)
    pairs = [_pair_sum_class(h, o, s[1], f"pair_sum_{k}") for k, h, o, s in zip(first, halves, others, first_specs)]
    pair_small = _add(small, other_small, "pair_sum_small")
    *lands, land_small = _exchange_classes([p[0] for p in pairs], first_specs, pair_small)
    for k, p, land, s in zip(first, pairs, lands, first_specs):
        mine[k], specs[k] = _chip_sum_class(p[1], land, s[1], f"chip_sum_{k}"), s
    theirs = _share_classes(list(mine.values()), [specs[k] for k in mine])
    south = core == 0
    sh = {k: jnp.concatenate([jnp.where(south, a, b), jnp.where(south, b, a)], axis=0)
          for (k, a), b in zip(mine.items(), theirs)}
    grads = {"a_w_in": jnp.stack([sh["in0"], sh["in3"]]),
             "b_w_in": sh["in1"].reshape(5, D, 256).transpose(1, 0, 2).reshape(b_w_in.shape),
             "c_w_in": sh["in2"][:1284], "w_out": jnp.stack([sh[f"out{i}"] for i in range(DEPTH)]),
             "w_mem_kv": sh["kv"]}
    flat = _sum4(pair_small, land_small).reshape(-1)
    off = 0
    for n in _SMALL:
        shape = _SMALL_SHAPES[n]
        full = flat[off:off + _size(shape)].reshape(shape)
        off += _size(shape)
        if n in _SHARDED_SMALL:
            full = lax.dynamic_slice_in_dim(full, chip * _SHARDED_SMALL[n], _SHARDED_SMALL[n], axis=len(shape) - 1)
        grads[n] = full

    delta, new_m, new_v = {}, {}, {}
    for n in names:
        shape = w[n].shape
        if n == "c_w_in":
            d_, m_, v_ = _adamw(rows_of_ct(w[n]), grads[n], rows_of_ct(m[n]), rows_of_ct(v[n]), f"adamw_{n}")
            delta[n], new_m[n], new_v[n], grads[n] = d_.T[None], m_.T[None], v_.T[None], grads[n].T[None]
            continue
        view = (1, shape[0]) if len(shape) == 1 else (_size(shape[:-1]), shape[-1])
        d_, m_, v_ = _adamw(w[n].reshape(view), grads[n].reshape(view), m[n].reshape(view), v[n].reshape(view),
                            f"adamw_{n}")
        delta[n], new_m[n], new_v[n] = d_.reshape(shape), m_.reshape(shape), v_.reshape(shape)
    return (loss, dx[None], *[grads[n].reshape(w[n].shape) for n in names], *[delta[n] for n in names],
            *[new_m[n] for n in names], *[new_v[n] for n in names])
```

```python
import functools

import jax
import jax.numpy as jnp
from jax import lax
from jax.experimental import pallas as pl
from jax.experimental.pallas import tpu as pltpu

F32 = jnp.float32
BF16 = jnp.bfloat16
HI = lax.Precision.HIGHEST
MESH = pl.DeviceIdType.MESH
SDS = jax.ShapeDtypeStruct

D = 1024
D_XA = 512
D_CAT = 1536
N_MEM = 256
HD = 128
DEPTH = 4
EPS = 1e-6
TT = 512
DN_C = 64
DN_TB = 512
HALO = 8
AB_PAD = 128
VMEM_LIMIT = 56 * 1024 * 1024
GRAD_TILE = {0: 1024, 1: 256}

ADAM_LR, ADAM_B1, ADAM_B2, ADAM_EPS, ADAM_WD, ADAM_STEP = 0.001, 0.9, 0.999, 1e-08, 0.01, 10


def _params(n_grid, vmem=None):
    return pltpu.CompilerParams(dimension_semantics=("arbitrary",) * n_grid, vmem_limit_bytes=vmem)


def _rms(x, w):
    return x * lax.rsqrt(jnp.mean(x * x, axis=-1, keepdims=True) + EPS) * w


def _dot_nn(a, b):
    return jnp.dot(a.astype(BF16), b.astype(BF16), preferred_element_type=F32)


def _dot_nt(a, b):
    return lax.dot_general(a.astype(BF16), b.astype(BF16), (((1,), (1,)), ((), ())), preferred_element_type=F32)


def _dot_tn(a, b):
    return lax.dot_general(a.astype(BF16), b.astype(BF16), (((0,), (0,)), ((), ())), preferred_element_type=F32)


@jax.custom_vjp
def mm(a, b):
    return _dot_nn(a, b)


mm.defvjp(lambda a, b: (_dot_nn(a, b), (a, b)), lambda r, g: (_dot_nt(g, r[1]), _dot_tn(r[0], g)))


@jax.custom_vjp
def mm_nt(a, b):
    return _dot_nt(a, b)


mm_nt.defvjp(lambda a, b: (_dot_nt(a, b), (a, b)), lambda r, g: (_dot_nn(g, r[1]), _dot_tn(g, r[0])))


def _row_spec(width, tile=TT):
    return pl.BlockSpec((tile, width), lambda i: (i, 0))


def _full_spec(shape):
    return pl.BlockSpec(shape, lambda *_: (0,) * len(shape))


def _widths(blocks, transposed):
    return [b.shape[0 if transposed else 1] for b in blocks]


def _inproj_fwd(x, nw, mix, gate, transposed, name):
    T, nm = x.shape[0], len(mix)
    M, G = sum(_widths(mix, transposed)), sum(_widths(gate, transposed))

    def body(x_ref, nw_ref, *refs):
        blocks, (pm_ref, pg_ref, h_ref) = refs[:-3], refs[-3:]
        h = _rms(x_ref[...], nw_ref[...]).astype(BF16)
        h_ref[...] = h
        for p_ref, group in ((pm_ref, blocks[:nm]), (pg_ref, blocks[nm:])):
            off = 0
            for w_ref in group:
                w = w_ref.shape[0 if transposed else 1]
                p_ref[:, off:off + w] = _dot_nt(h, w_ref[...]) if transposed else _dot_nn(h, w_ref[...])
                off += w

    return pl.pallas_call(
        body, name=name, grid=(T // TT,),
        in_specs=[_row_spec(D), _full_spec((1, D))] + [_full_spec(b.shape) for b in mix + gate],
        out_specs=[_row_spec(M), _row_spec(G), _row_spec(D)],
        out_shape=[SDS((T, M), F32), SDS((T, G), F32), SDS((T, D), BF16)],
        compiler_params=_params(1, VMEM_LIMIT))(x, nw, *mix, *gate)


def _inproj_bwd(dpm, dpg, x, nw, mix, gate, transposed, dxc, name):
    T, nm = x.shape[0], len(mix)
    M, G = sum(_widths(mix, transposed)), sum(_widths(gate, transposed))

    def body(dpm_ref, dpg_ref, x_ref, nw_ref, *refs):
        blocks, (dxc_ref, dx_ref, dnw_ref) = refs[:-3], refs[-3:]
        dh = None
        for dp_ref, group in ((dpm_ref, blocks[:nm]), (dpg_ref, blocks[nm:])):
            off = 0
            for w_ref in group:
                w = w_ref.shape[0 if transposed else 1]
                dp = dp_ref[:, off:off + w]
                part = _dot_nn(dp, w_ref[...]) if transposed else _dot_nt(dp, w_ref[...])
                dh = part if dh is None else dh + part
                off += w
        _, vjp = jax.vjp(_rms, x_ref[...], nw_ref[...])
        dxr, dnw = vjp(dh)
        dx_ref[...] = dxc_ref[...] + dxr

        @pl.when(pl.program_id(0) == 0)
        def _():
            dnw_ref[...] = jnp.zeros_like(dnw_ref)
        dnw_ref[...] += dnw

    return pl.pallas_call(
        body, name=name, grid=(T // TT,),
        in_specs=[_row_spec(M), _row_spec(G), _row_spec(D), _full_spec((1, D))]
        + [_full_spec(b.shape) for b in mix + gate] + [_row_spec(D)],
        out_specs=[_row_spec(D), _full_spec((1, D))],
        out_shape=[SDS((T, D), F32), SDS((1, D), F32)],
        compiler_params=_params(1, VMEM_LIMIT))(dpm, dpg, x, nw, *mix, *gate, dxc)


def _matmul_tn(a, b, name, sub=None):
    T, K = a.shape
    N = b.shape[1]
    tn = 1024 if N % 1024 == 0 else (640 if N % 640 == 0 else N)
    tt = min(2048 if K <= D_CAT else 1024, T)
    n_sub = 1 if sub is None else tn // sub

    def body(a_ref, b_ref, o_ref):
        @pl.when(pl.program_id(1) == 0)
        def _():
            o_ref[...] = jnp.zeros_like(o_ref)
        res = _dot_tn(a_ref[...], b_ref[...])
        if sub is None:
            o_ref[...] += res
        else:
            for i in range(n_sub):
                o_ref[i] += res[:, i * sub:(i + 1) * sub]

    if sub is None:
        out_spec, out_shape = pl.BlockSpec((K, tn), lambda j, t: (0, j)), SDS((K, N), F32)
    else:
        out_spec, out_shape = pl.BlockSpec((n_sub, K, sub), lambda j, t: (j, 0, 0)), SDS((N // sub, K, sub), F32)
    return pl.pallas_call(
        body, name=name, grid=(N // tn, T // tt),
        in_specs=[pl.BlockSpec((tt, K), lambda j, t: (t, 0)), pl.BlockSpec((tt, tn), lambda j, t: (t, j))],
        out_specs=out_spec, out_shape=out_shape,
        compiler_params=_params(2, VMEM_LIMIT))(a, b)


def _memkv_fn(mem, w, wkv):
    return mm(_rms(mem, w), wkv)


def _memkv_fwd(mem, w, wkv):
    def body(mem_ref, w_ref, wkv_ref, kv_ref):
        kv_ref[...] = _memkv_fn(mem_ref[...], w_ref[...], wkv_ref[...])

    return pl.pallas_call(body, name="memkv_fwd", out_shape=SDS((N_MEM, 2 * D_XA), F32),
                          compiler_params=_params(0, VMEM_LIMIT))(mem, w, wkv)


def _memkv_bwd(mem, w, wkv, dkv):
    def body(mem_ref, w_ref, wkv_ref, dkv_ref, dw_ref, dwkv_ref):
        _, vjp = jax.vjp(functools.partial(_memkv_fn, mem_ref[...]), w_ref[...], wkv_ref[...].astype(F32))
        dw, dwkv = vjp(dkv_ref[...])
        dw_ref[...] = dw
        dwkv_ref[...] = dwkv

    return pl.pallas_call(body, name="memkv_bwd", out_shape=[SDS((1, D), F32), SDS((D, 2 * D_XA), F32)],
                          compiler_params=_params(0, VMEM_LIMIT))(mem, w, wkv, dkv)


def _attn_gate(ymix, qx, z, *kvs):
    outs = []
    for j in range(4):
        s = mm_nt(qx[:, j * HD:(j + 1) * HD], kvs[j]) * (HD ** -0.5)
        e = jnp.exp(s - lax.stop_gradient(jnp.max(s, axis=-1, keepdims=True)))
        outs.append(mm(e / jnp.sum(e, axis=-1, keepdims=True), kvs[4 + j]))
    return jnp.concatenate([ymix] + outs, axis=1) * jax.nn.silu(z)


def _kv_blocks(kv_ref):
    return [kv_ref[:, j * HD:(j + 1) * HD] for j in range(8)]


def _gate_outproj_fwd(ymix, pg, kv, wo, x, nw, target, name):
    T = x.shape[0]
    last = target is not None

    def body(ymix_ref, pg_ref, kv_ref, wo_ref, x_ref, nw_ref, *refs):
        ycat = _attn_gate(ymix_ref[...], pg_ref[:, :D_XA], pg_ref[:, D_XA:], *_kv_blocks(kv_ref)).astype(BF16)
        o = jnp.dot(ycat, wo_ref[...], preferred_element_type=F32)
        y = x_ref[...] + _rms(o, nw_ref[...])
        if not last:
            ycat_ref, o_ref, y_ref = refs
            y_ref[...] = y
        else:
            t_ref, ycat_ref, o_ref, loss_ref, dy_ref = refs
            err = y - t_ref[...]
            dy_ref[...] = err * (1.0 / D)

            @pl.when(pl.program_id(0) == 0)
            def _():
                loss_ref[...] = jnp.zeros_like(loss_ref)
            part = jnp.sum(jnp.sum(err * err, axis=1, keepdims=True), axis=0, keepdims=True) * (0.5 / D)
            loss_ref[...] += jnp.broadcast_to(part, loss_ref.shape)
        ycat_ref[...] = ycat
        o_ref[...] = o

    in_specs = [_row_spec(D), _row_spec(D_XA + D_CAT), _full_spec((N_MEM, 2 * D_XA)), _full_spec((D_CAT, D)),
                _row_spec(D), _full_spec((1, D))]
    out_specs = [_row_spec(D_CAT), _row_spec(D)]
    out_shape = [SDS((T, D_CAT), BF16), SDS((T, D), F32)]
    if last:
        in_specs.append(_row_spec(D))
        out_specs += [_full_spec((8, 128)), _row_spec(D)]
        out_shape += [SDS((8, 128), F32), SDS((T, D), F32)]
    else:
        out_specs.append(_row_spec(D))
        out_shape.append(SDS((T, D), F32))
    args = (ymix, pg, kv, wo, x, nw) + ((target,) if last else ())
    return pl.pallas_call(body, name=name, grid=(T // TT,), in_specs=in_specs, out_specs=out_specs,
                          out_shape=out_shape, compiler_params=_params(1, VMEM_LIMIT))(*args)


def _outproj_gate_bwd(dxo, o, nw, wo, ymix, pg, kv, dkv_in, name):
    T = dxo.shape[0]
    G = D_XA + D_CAT

    def body(dxo_ref, o_ref, nw_ref, wo_ref, ymix_ref, pg_ref, kv_ref, dkvin_ref,
             dobf_ref, dnw_ref, dymix_ref, dpg_ref, dkv_ref):
        _, vjp = jax.vjp(_rms, o_ref[...], nw_ref[...])
        do, dnw = vjp(dxo_ref[...])
        dobf = do.astype(BF16)
        dobf_ref[...] = dobf
        dycat = _dot_nt(dobf, wo_ref[...])
        _, vjp = jax.vjp(_attn_gate, ymix_ref[...], pg_ref[:, :D_XA], pg_ref[:, D_XA:], *_kv_blocks(kv_ref))
        g = vjp(dycat)
        dymix_ref[...] = g[0]
        dpg_ref[:, :D_XA] = g[1].astype(BF16)
        dpg_ref[:, D_XA:] = g[2].astype(BF16)

        @pl.when(pl.program_id(0) == 0)
        def _():
            dnw_ref[...] = jnp.zeros_like(dnw_ref)
            dkv_ref[...] = dkvin_ref[...]
        dnw_ref[...] += dnw
        for j in range(8):
            dkv_ref[:, j * HD:(j + 1) * HD] += g[3 + j]

    return pl.pallas_call(
        body, name=name, grid=(T // TT,),
        in_specs=[_row_spec(D), _row_spec(D), _full_spec((1, D)), _full_spec((D_CAT, D)), _row_spec(D), _row_spec(G),
                  _full_spec((N_MEM, 2 * D_XA)), _full_spec((N_MEM, 2 * D_XA))],
        out_specs=[_row_spec(D), _full_spec((1, D)), _row_spec(D), _row_spec(G), _full_spec((N_MEM, 2 * D_XA))],
        out_shape=[SDS((T, D), BF16), SDS((1, D), F32), SDS((T, D), F32), SDS((T, G), BF16),
                   SDS((N_MEM, 2 * D_XA), F32)],
        compiler_params=_params(1, VMEM_LIMIT))(dxo, o, nw, wo, ymix, pg, kv, dkv_in)


def _gmlp_pre(u, v, lnw, lnb):
    vg = jax.nn.gelu(v)
    xc = vg - jnp.mean(vg, axis=-1, keepdims=True)
    vl = xc * lax.rsqrt(jnp.mean(xc * xc, axis=-1, keepdims=True) + EPS) * lnw + lnb
    return jax.nn.gelu(u), vl


def _tril(n, strict=False):
    r = lax.broadcasted_iota(jnp.int32, (n, n), 0)
    c = lax.broadcasted_iota(jnp.int32, (n, n), 1)
    return (r > c) if strict else (r >= c)


def _gmlp_fwd(pm, lnw, lnb, ws, bs3, name):
    T = pm.shape[0]

    def body(pm_ref, lnw_ref, lnb_ref, ws_ref, bs_ref, y_ref):
        ug, vl = _gmlp_pre(pm_ref[:, :D], pm_ref[:, D:], lnw_ref[...], lnb_ref[...])
        mask = _tril(HD)
        for g in range(8):
            w = jnp.where(mask, ws_ref[g], 0.0)
            for c in range(TT // HD):
                rows, cols = slice(c * HD, (c + 1) * HD), slice(g * HD, (g + 1) * HD)
                y_ref[rows, cols] = ug[rows, cols] * (_dot_nn(w, vl[rows, cols]) + bs_ref[g])

    return pl.pallas_call(
        body, name=name, grid=(T // TT,),
        in_specs=[_row_spec(2 * D), _full_spec((1, D)), _full_spec((1, D)), _full_spec((8, HD, HD)),
                  _full_spec((8, HD, HD))],
        out_specs=_row_spec(D), out_shape=SDS((T, D), F32),
        compiler_params=_params(1, VMEM_LIMIT))(pm, lnw, lnb, ws, bs3)


def _gmlp_bwd(dy, pm, lnw, lnb, ws, bs3, name):
    T = pm.shape[0]
    n_t = T // TT

    def body(dy_ref, pm_ref, lnw_ref, lnb_ref, ws_ref, bs_ref, dpm_ref, dlnw_ref, dlnb_ref, dws_ref, dbs_ref,
             dug_scr, dvl_scr, dbs_scr):
        i = pl.program_id(0)

        @pl.when(i == 0)
        def _():
            dlnw_ref[...] = jnp.zeros_like(dlnw_ref)
            dlnb_ref[...] = jnp.zeros_like(dlnb_ref)
            dws_ref[...] = jnp.zeros_like(dws_ref)
            dbs_scr[...] = jnp.zeros_like(dbs_scr)

        (ug, vl), vjp = jax.vjp(_gmlp_pre, pm_ref[:, :D], pm_ref[:, D:], lnw_ref[...], lnb_ref[...])
        mask = _tril(HD)
        for g in range(8):
            w = jnp.where(mask, ws_ref[g], 0.0)
            dw = jnp.zeros((HD, HD), F32)
            db = jnp.zeros((HD, HD), F32)
            for c in range(TT // HD):
                rows, cols = slice(c * HD, (c + 1) * HD), slice(g * HD, (g + 1) * HD)
                dyb, vlb = dy_ref[rows, cols], vl[rows, cols]
                sp = _dot_nn(w, vlb) + bs_ref[g]
                dsp = dyb * ug[rows, cols]
                dug_scr[rows, cols] = dyb * sp
                dvl_scr[rows, cols] = _dot_tn(w, dsp)
                dw += _dot_nt(dsp, vlb)
                db += dsp
            dws_ref[g] += jnp.where(mask, dw, 0.0)
            dbs_scr[g] += db
        du, dv, dlnw, dlnb = vjp((dug_scr[...], dvl_scr[...]))
        dpm_ref[:, :D] = du.astype(BF16)
        dpm_ref[:, D:] = dv.astype(BF16)
        dlnw_ref[...] += dlnw
        dlnb_ref[...] += dlnb

        @pl.when(i == n_t - 1)
        def _():
            for g in range(8):
                dbs_ref[g] = jnp.broadcast_to(jnp.sum(dbs_scr[g], axis=1, keepdims=True), (HD, HD))

    return pl.pallas_call(
        body, name=name, grid=(n_t,),
        in_specs=[_row_spec(D), _row_spec(2 * D), _full_spec((1, D)), _full_spec((1, D)), _full_spec((8, HD, HD)),
                  _full_spec((8, HD, HD))],
        out_specs=[_row_spec(2 * D), _full_spec((1, D)), _full_spec((1, D)), _full_spec((8, HD, HD)),
                   _full_spec((8, HD, HD))],
        out_shape=[SDS((T, 2 * D), BF16), SDS((1, D), F32), SDS((1, D), F32), SDS((8, HD, HD), F32),
                   SDS((8, HD, HD), F32)],
        scratch_shapes=[pltpu.VMEM((TT, D), F32), pltpu.VMEM((TT, D), F32), pltpu.VMEM((8, HD, HD), F32)],
        compiler_params=_params(1, VMEM_LIMIT))(dy, pm, lnw, lnb, ws, bs3)


def _prev_spec(width, T):
    return pl.BlockSpec((HALO, width), lambda i: (jnp.maximum(i * (TT // HALO) - 1, 0), 0))


def _next_spec(width, T):
    return pl.BlockSpec((HALO, width), lambda i: (jnp.minimum((i + 1) * (TT // HALO), T // HALO - 1), 0))


def _rows_before(ext, j):
    return ext[HALO:] if j == 0 else pltpu.roll(ext, j, 0)[HALO:]


def _rows_after(ext, j):
    n = ext.shape[0]
    return ext[:n - HALO] if j == 0 else pltpu.roll(ext, n - j, 0)[:n - HALO]


def _conv_apply(ext_s, w):
    K = w.shape[0]
    y = _rows_before(ext_s, K - 1) * w[0:1]
    for k in range(1, K):
        y = y + _rows_before(ext_s, K - 1 - k) * w[k:k + 1]
    return y


def _conv_grads(ext_s, ext_dy, w):
    K = w.shape[0]
    dy = ext_dy[:ext_dy.shape[0] - HALO]
    ds = _rows_after(ext_dy, K - 1) * w[0:1]
    dws = [jnp.sum(dy * _rows_before(ext_s, K - 1), axis=0, keepdims=True)]
    for k in range(1, K):
        ds = ds + _rows_after(ext_dy, K - 1 - k) * w[k:k + 1]
        dws.append(jnp.sum(dy * _rows_before(ext_s, K - 1 - k), axis=0, keepdims=True))
    return ds, jnp.concatenate(dws, axis=0)


def _sconv_fwd(pm, w, name):
    T = pm.shape[0]

    def body(pm_ref, prev_ref, w_ref, y_ref):
        s = pm_ref[:, D:2 * D] * pm_ref[:, 2 * D:]
        sp = jnp.where(pl.program_id(0) > 0, prev_ref[:, D:2 * D] * prev_ref[:, 2 * D:], 0.0)
        y_ref[...] = pm_ref[:, :D] * _conv_apply(jnp.concatenate([sp, s], axis=0), w_ref[...])

    return pl.pallas_call(
        body, name=name, grid=(T // TT,),
        in_specs=[_row_spec(3 * D), _prev_spec(3 * D, T), _full_spec((3, D))],
        out_specs=_row_spec(D), out_shape=SDS((T, D), F32),
        compiler_params=_params(1, VMEM_LIMIT))(pm, pm, w)


def _sconv_bwd(dy, pm, w, name):
    T = pm.shape[0]
    n_t = T // TT

    def body(dy_ref, dyn_ref, pm_ref, prev_ref, next_ref, w_ref, dpm_ref, dw_ref):
        i = pl.program_id(0)
        bg, cg, hv = pm_ref[:, :D], pm_ref[:, D:2 * D], pm_ref[:, 2 * D:]
        sp = jnp.where(i > 0, prev_ref[:, D:2 * D] * prev_ref[:, 2 * D:], 0.0)
        ext_s = jnp.concatenate([sp, cg * hv], axis=0)
        dyv = dy_ref[...]
        dcn = jnp.where(i < n_t - 1, dyn_ref[...] * next_ref[:, :D], 0.0)
        ds, dw = _conv_grads(ext_s, jnp.concatenate([dyv * bg, dcn], axis=0), w_ref[...])
        dpm_ref[:, :D] = (dyv * _conv_apply(ext_s, w_ref[...])).astype(BF16)
        dpm_ref[:, D:2 * D] = (ds * hv).astype(BF16)
        dpm_ref[:, 2 * D:] = (ds * cg).astype(BF16)

        @pl.when(i == 0)
        def _():
            dw_ref[...] = jnp.zeros_like(dw_ref)
        dw_ref[...] += dw

    return pl.pallas_call(
        body, name=name, grid=(n_t,),
        in_specs=[_row_spec(D), _next_spec(D, T), _row_spec(3 * D), _prev_spec(3 * D, T), _next_spec(3 * D, T),
                  _full_spec((3, D))],
        out_specs=[_row_spec(3 * D), _full_spec((3, D))],
        out_shape=[SDS((T, 3 * D), BF16), SDS((3, D), F32)],
        compiler_params=_params(1, VMEM_LIMIT))(dy, dy, pm, pm, pm, w)


def _dnconv_fwd(pm, w, name):
    T = pm.shape[0]

    def body(pm_ref, prev_ref, w_ref, c_ref):
        sp = jnp.where(pl.program_id(0) > 0, prev_ref[...], 0.0)
        c_ref[...] = _conv_apply(jnp.concatenate([sp, pm_ref[...]], axis=0), w_ref[...])

    return pl.pallas_call(
        body, name=name, grid=(T // TT,),
        in_specs=[_row_spec(3 * D), _prev_spec(3 * D, T), _full_spec((4, 3 * D))],
        out_specs=_row_spec(3 * D), out_shape=SDS((T, 3 * D), F32),
        compiler_params=_params(1, VMEM_LIMIT))(pm, pm, w)


def _dnconv_bwd(dcq, dck, dcv, dab, pm, w, name):
    T = pm.shape[0]
    n_t = T // TT

    def body(dq_ref, dk_ref, dv_ref, dqn_ref, dkn_ref, dvn_ref, dab_ref, pm_ref, prev_ref, w_ref, dpm_ref, dw_ref):
        i = pl.program_id(0)
        sp = jnp.where(i > 0, prev_ref[...], 0.0)
        ext_s = jnp.concatenate([sp, pm_ref[...]], axis=0)
        own = jnp.concatenate([dq_ref[...], dk_ref[...], dv_ref[...]], axis=1)
        nxt = jnp.where(i < n_t - 1, jnp.concatenate([dqn_ref[...], dkn_ref[...], dvn_ref[...]], axis=1), 0.0)
        ds, dw = _conv_grads(ext_s, jnp.concatenate([own, nxt], axis=0), w_ref[...])
        dpm_ref[:, :3 * D] = ds.astype(BF16)
        dpm_ref[:, 3 * D:] = dab_ref[...].astype(BF16)

        @pl.when(i == 0)
        def _():
            dw_ref[...] = jnp.zeros_like(dw_ref)
        dw_ref[...] += dw

    return pl.pallas_call(
        body, name=name, grid=(n_t,),
        in_specs=[_row_spec(D), _row_spec(D), _row_spec(D), _next_spec(D, T), _next_spec(D, T), _next_spec(D, T),
                  _row_spec(AB_PAD), _row_spec(3 * D), _prev_spec(3 * D, T), _full_spec((4, 3 * D))],
        out_specs=[_row_spec(3 * D + AB_PAD), _full_spec((4, 3 * D))],
        out_shape=[SDS((T, 3 * D + AB_PAD), BF16), SDS((4, 3 * D), F32)],
        compiler_params=_params(1, VMEM_LIMIT))(dcq, dck, dcv, dcq, dck, dcv, dab, pm, pm, w)


def _l2n(x):
    return x * lax.rsqrt(jnp.sum(x * x, axis=-1, keepdims=True) + EPS)


def _softplus(x):
    return jnp.maximum(x, 0.0) + jnp.log1p(jnp.exp(-jnp.abs(x)))


_BNN = (((2,), (1,)), ((0,), (0,)))
_BNT = (((2,), (2,)), ((0,), (0,)))
_BTN = (((1,), (1,)), ((0,), (0,)))


def _bdot(a, b, dims):
    return lax.dot_general(a.astype(BF16), b.astype(BF16), dims, preferred_element_type=F32)


def _bdot3(a, b, dims):
    ah, bh = a.astype(BF16), b.astype(BF16)
    al, bl = (a - ah.astype(F32)).astype(BF16), (b - bh.astype(F32)).astype(BF16)
    d = functools.partial(lax.dot_general, dimension_numbers=dims, preferred_element_type=F32)
    return d(ah, bh) + (d(ah, bl) + d(al, bh))


def _bdot_hi(a, b, dims):
    return lax.dot_general(a, b, dims, precision=HI, preferred_element_type=F32)


def _batched_matmuls(dot):
    @jax.custom_vjp
    def nn(a, b):
        return dot(a, b, _BNN)

    @jax.custom_vjp
    def nt(a, b):
        return dot(a, b, _BNT)

    @jax.custom_vjp
    def tn(a, b):
        return dot(a, b, _BTN)

    nn.defvjp(lambda a, b: (dot(a, b, _BNN), (a, b)), lambda r, g: (dot(g, r[1], _BNT), dot(r[0], g, _BTN)))
    nt.defvjp(lambda a, b: (dot(a, b, _BNT), (a, b)), lambda r, g: (dot(g, r[1], _BNN), dot(g, r[0], _BTN)))
    tn.defvjp(lambda a, b: (dot(a, b, _BTN), (a, b)), lambda r, g: (dot(r[1], g, _BNT), dot(r[0], g, _BNN)))
    return nn, nt, tn


bmm, bmm_nt, bmm_tn = _batched_matmuls(_bdot)
bmm_hi, _, _ = _batched_matmuls(_bdot_hi)

@jax.custom_vjp
def _neumann_inverse(n):
    C = n.shape[1]
    eye = lax.broadcasted_iota(jnp.int32, n.shape, 1) == lax.broadcasted_iota(jnp.int32, n.shape, 2)
    t = eye.astype(F32) + n
    for _ in range(5):
        n = _bdot3(n, n, _BNN)
        t = t + _bdot3(t, n, _BNN)
    return t


def _neumann_inverse_fwd(n):
    t = _neumann_inverse(n)
    return t, t


def _neumann_inverse_bwd(t, g):
    return (_bdot3(_bdot3(t, g, _BTN), t, _BNT),)


_neumann_inverse.defvjp(_neumann_inverse_fwd, _neumann_inverse_bwd)


@jax.custom_vjp
def _saved_inverse(n, t):
    return t


_saved_inverse.defvjp(lambda n, t: (t, t), lambda t, g: (_bdot3(_bdot3(t, g, _BTN), t, _BNT), jnp.zeros_like(t)))

DN_NCH = DN_TB // DN_C
DN_NH = 4


def _decay_terms(ab, alog, dtb, first_head, n_heads):
    C = DN_C
    lane = lax.broadcasted_iota(jnp.int32, ab.shape, 1)
    g_all = (-jnp.exp(alog) * _softplus(ab + dtb)).reshape(DN_NCH, C, HD)
    beta_all = jax.nn.sigmoid(ab)
    r = lax.broadcasted_iota(jnp.int32, (DN_NCH, C, C), 1)
    c = lax.broadcasted_iota(jnp.int32, (DN_NCH, C, C), 2)
    gc_all = bmm_hi((r >= c).astype(F32), g_all)
    gc_rows = [gc_all[i].T for i in range(DN_NCH)]
    lane3 = lax.broadcasted_iota(jnp.int32, (DN_NCH, C, HD), 2)
    row = lax.broadcasted_iota(jnp.int32, (HD, C), 0)
    ones = jnp.ones((1, HD), F32)
    gcs, gjs, betas = [], [], []
    for i in range(n_heads):
        h = first_head + i
        gcs.append(jnp.sum(jnp.where(lane3 == h, gc_all, 0.0), axis=2, keepdims=True) * ones)
        gjs.append(jnp.concatenate(
            [jnp.broadcast_to(jnp.sum(jnp.where(row == h, t, 0.0), axis=0, keepdims=True), (C, C))[None] for t in gc_rows],
            axis=0))
        beta = jnp.sum(jnp.where(lane == 8 + h, beta_all, 0.0), axis=1, keepdims=True) * ones
        betas.append(beta.reshape(DN_NCH, C, HD))
    return jnp.concatenate(gcs, axis=0), jnp.concatenate(gjs, axis=0), jnp.concatenate(betas, axis=0)


def _dn_prep(cq, ck, cv, gcum, gj, bb, t_saved=None):
    B, C = cq.shape[0], DN_C
    q = _l2n(jax.nn.silu(cq)) * (HD ** -0.5)
    k = _l2n(jax.nn.silu(ck))
    v = jax.nn.silu(cv)
    r = lax.broadcasted_iota(jnp.int32, (B, C, C), 1)
    c = lax.broadcasted_iota(jnp.int32, (B, C, C), 2)
    incl, strict = r >= c, r > c
    decay = jnp.where(incl, jnp.exp(jnp.where(incl, gcum[:, :, :C] - gj, 0.0)), 0.0)
    kb = k * bb
    n_mat = -jnp.where(strict, bmm_nt(kb, k) * decay, 0.0)
    t_mat = _neumann_inverse(n_mat) if t_saved is None else _saved_inverse(n_mat, t_saved)
    eg = jnp.exp(gcum)
    glast = gcum[:, C - 1:C, :]
    return (bmm(t_mat, v * bb), bmm(t_mat, kb * eg), bmm_nt(q, k) * decay, q * eg, k * jnp.exp(glast - gcum),
            jnp.exp(glast), t_mat)


def _dn_scan_step(u, w, qk, qd, kd, egl, S, onw):
    v_new = u - bmm(w, S)
    o = bmm(qd, S) + bmm(qk, v_new)
    return _rms(o, onw), S * egl + bmm_tn(kd, v_new)


def _to_batch(ref, n_heads):
    return jnp.concatenate([ref[:, i * HD:(i + 1) * HD].astype(F32).reshape(DN_NCH, DN_C, HD) for i in range(n_heads)],
                           axis=0)


def _from_batch(ref, val, n_heads):
    for i in range(n_heads):
        ref[:, i * HD:(i + 1) * HD] = val[i * DN_NCH:(i + 1) * DN_NCH].reshape(DN_TB, HD).astype(ref.dtype)


def _prep_specs(T, rev):
    nb = T // DN_TB
    blk = (lambda n: nb - 1 - n) if rev else (lambda n: n)
    ng = 8 // DN_NH
    head = [pl.BlockSpec((DN_TB, DN_NH * HD), functools.partial(lambda n, h, off: (blk(n), off + h), off=ng * s))
            for s in range(3)]
    ab = pl.BlockSpec((DN_TB, AB_PAD), lambda n, h: (blk(n), 3 * D // AB_PAD))
    row = pl.BlockSpec((1, HD), lambda n, h: (0, 0))
    wide = pl.BlockSpec((DN_TB, DN_NH * HD), lambda n, h: (blk(n), h))
    qk = pl.BlockSpec((DN_NCH, DN_NH, DN_C, DN_C), lambda n, h: (blk(n), h, 0, 0))
    eg = pl.BlockSpec((DN_NCH, DN_NH, 1, HD), lambda n, h: (blk(n), h, 0, 0))
    return nb, ng, head, ab, row, wide, qk, eg


def _dn_prep_fwd(cpre, pm, alog, dtb, name):
    T = cpre.shape[0]
    nb, ng, head, ab, row, wide, qks, egs = _prep_specs(T, False)

    def body(cq_ref, ck_ref, cv_ref, ab_ref, alog_ref, dtb_ref, u_ref, w_ref, qk_ref, qd_ref, kd_ref, e_ref, t_ref):
        gcum, gj, bb = _decay_terms(ab_ref[...], alog_ref[...], dtb_ref[...], pl.program_id(1) * DN_NH, DN_NH)
        u, w, qk, qd, kd, egl, t_mat = _dn_prep(_to_batch(cq_ref, DN_NH), _to_batch(ck_ref, DN_NH),
                                                _to_batch(cv_ref, DN_NH), gcum, gj, bb)
        _from_batch(u_ref, u, DN_NH)
        _from_batch(w_ref, w, DN_NH)
        _from_batch(qd_ref, qd, DN_NH)
        _from_batch(kd_ref, kd, DN_NH)
        for i in range(DN_NH):
            qk_ref[:, i] = qk[i * DN_NCH:(i + 1) * DN_NCH].astype(BF16)
            e_ref[:, i] = egl[i * DN_NCH:(i + 1) * DN_NCH]
            t_ref[:, i] = t_mat[i * DN_NCH:(i + 1) * DN_NCH]

    return pl.pallas_call(
        body, name=name, grid=(nb, ng), in_specs=head + [ab, row, row],
        out_specs=[wide, wide, qks, wide, wide, egs, qks],
        out_shape=[SDS((T, D), F32), SDS((T, D), BF16), SDS((T // DN_C, 8, DN_C, DN_C), BF16), SDS((T, D), BF16),
                   SDS((T, D), BF16), SDS((T // DN_C, 8, 1, HD), F32), SDS((T // DN_C, 8, DN_C, DN_C), F32)],
        compiler_params=_params(2, VMEM_LIMIT))(cpre, cpre, cpre, pm, alog, dtb)


def _dn_prep_bwd(du, dw, dqk, dqd, dkd, degl, t_mat, cpre, pm, alog, dtb, name):
    T = cpre.shape[0]
    nb, ng, head, ab, row, wide, qks, egs = _prep_specs(T, True)

    def body(du_ref, dw_ref, dqk_ref, dqd_ref, dkd_ref, de_ref, t_ref, cq_ref, ck_ref, cv_ref, ab_ref, alog_ref,
             dtb_ref, dcq_ref, dck_ref, dcv_ref, dab_ref, dalog_ref, ddtb_ref):
        n, h = pl.program_id(0), pl.program_id(1)

        @pl.when((n == 0) & (h == 0))
        def _():
            dalog_ref[...] = jnp.zeros_like(dalog_ref)
            ddtb_ref[...] = jnp.zeros_like(ddtb_ref)

        @pl.when(h == 0)
        def _():
            dab_ref[...] = jnp.zeros_like(dab_ref)

        t_saved = jnp.concatenate([t_ref[:, i] for i in range(DN_NH)], axis=0)

        def fwd(cq, ck, cv, ab_v, alog_v, dtb_v):
            gcum, gj, bb = _decay_terms(ab_v, alog_v, dtb_v, h * DN_NH, DN_NH)
            return _dn_prep(cq, ck, cv, gcum, gj, bb, t_saved)[:6]

        _, vjp = jax.vjp(fwd, _to_batch(cq_ref, DN_NH), _to_batch(ck_ref, DN_NH), _to_batch(cv_ref, DN_NH), ab_ref[...],
                         alog_ref[...], dtb_ref[...])
        cot = (_to_batch(du_ref, DN_NH), _to_batch(dw_ref, DN_NH),
               jnp.concatenate([dqk_ref[:, i] for i in range(DN_NH)], axis=0), _to_batch(dqd_ref, DN_NH),
               _to_batch(dkd_ref, DN_NH), jnp.concatenate([de_ref[:, i] for i in range(DN_NH)], axis=0))
        dcq, dck, dcv, dab, dalog, ddtb = vjp(cot)
        _from_batch(dcq_ref, dcq, DN_NH)
        _from_batch(dck_ref, dck, DN_NH)
        _from_batch(dcv_ref, dcv, DN_NH)
        dab_ref[...] += dab
        dalog_ref[...] += dalog
        ddtb_ref[...] += ddtb

    dabspec = pl.BlockSpec((DN_TB, AB_PAD), lambda n, h: (nb - 1 - n, 0))
    return pl.pallas_call(
        body, name=name, grid=(nb, ng),
        in_specs=[wide, wide, qks, wide, wide, egs, qks] + head + [ab, row, row],
        out_specs=[wide, wide, wide, dabspec, row, row],
        out_shape=[SDS((T, D), F32)] * 3 + [SDS((T, AB_PAD), F32)] + [SDS((1, HD), F32)] * 2,
        compiler_params=_params(2, VMEM_LIMIT))(du, dw, dqk, dqd, dkd, degl, t_mat, cpre, cpre, cpre, pm, alog, dtb)


def _scan_specs(T, rev):
    nb = T // DN_TB
    blk = (lambda n: nb - 1 - n) if rev else (lambda n: n)
    wide = pl.BlockSpec((DN_TB, D), lambda n: (blk(n), 0))
    qk = pl.BlockSpec((DN_NCH, 8, DN_C, DN_C), lambda n: (blk(n), 0, 0, 0))
    eg = pl.BlockSpec((DN_NCH, 8, 1, HD), lambda n: (blk(n), 0, 0, 0))
    st = pl.BlockSpec((DN_NCH, 8, HD, HD), lambda n: (blk(n), 0, 0, 0))
    row = pl.BlockSpec((1, HD), lambda n: (0, 0))
    return nb, wide, qk, eg, st, row


def _heads_of(ref, rows):
    return jnp.concatenate([ref[rows, h * HD:(h + 1) * HD].astype(F32)[None] for h in range(8)], axis=0)


def _dn_scan_fwd(u, w, qk, qd, kd, egl, onw, name):
    T = u.shape[0]
    nb, wide, qks, egs, sts, row = _scan_specs(T, False)

    def body(u_ref, w_ref, qk_ref, qd_ref, kd_ref, e_ref, onw_ref, o_ref, st_ref, s_scr):
        @pl.when(pl.program_id(0) == 0)
        def _():
            s_scr[...] = jnp.zeros_like(s_scr)
        S = s_scr[...]
        for c in range(DN_NCH):
            rows = slice(c * DN_C, (c + 1) * DN_C)
            st_ref[c] = S
            o, S = _dn_scan_step(_heads_of(u_ref, rows), _heads_of(w_ref, rows), qk_ref[c].astype(F32),
                                 _heads_of(qd_ref, rows), _heads_of(kd_ref, rows), e_ref[c], S, onw_ref[...])
            for h in range(8):
                o_ref[rows, h * HD:(h + 1) * HD] = o[h]
        s_scr[...] = S

    return pl.pallas_call(
        body, name=name, grid=(nb,), in_specs=[wide, wide, qks, wide, wide, egs, row], out_specs=[wide, sts],
        out_shape=[SDS((T, D), F32), SDS((T // DN_C, 8, HD, HD), F32)],
        scratch_shapes=[pltpu.VMEM((8, HD, HD), F32)],
        compiler_params=_params(1, VMEM_LIMIT))(u, w, qk, qd, kd, egl, onw)


def _dn_scan_bwd(do, u, w, qk, qd, kd, egl, st, onw, name):
    T = u.shape[0]
    nb, wide, qks, egs, sts, row = _scan_specs(T, True)

    def body(do_ref, u_ref, w_ref, qk_ref, qd_ref, kd_ref, e_ref, st_ref, onw_ref,
             du_ref, dw_ref, dqk_ref, dqd_ref, dkd_ref, de_ref, donw_ref, ds_scr):
        @pl.when(pl.program_id(0) == 0)
        def _():
            ds_scr[...] = jnp.zeros_like(ds_scr)
            donw_ref[...] = jnp.zeros_like(donw_ref)
        dS = ds_scr[...]
        donw = jnp.zeros((1, HD), F32)
        for c in reversed(range(DN_NCH)):
            rows = slice(c * DN_C, (c + 1) * DN_C)
            _, vjp = jax.vjp(_dn_scan_step, _heads_of(u_ref, rows), _heads_of(w_ref, rows), qk_ref[c].astype(F32),
                             _heads_of(qd_ref, rows), _heads_of(kd_ref, rows), e_ref[c], st_ref[c], onw_ref[...])
            du, dw, dqk, dqd, dkd, de, dS, dn = vjp((_heads_of(do_ref, rows), dS))
            for h in range(8):
                cols = slice(h * HD, (h + 1) * HD)
                du_ref[rows, cols] = du[h]
                dw_ref[rows, cols] = dw[h]
                dqd_ref[rows, cols] = dqd[h]
                dkd_ref[rows, cols] = dkd[h]
            dqk_ref[c] = dqk
            de_ref[c] = de
            donw += dn
        ds_scr[...] = dS
        donw_ref[...] += donw

    return pl.pallas_call(
        body, name=name, grid=(nb,), in_specs=[wide, wide, wide, qks, wide, wide, egs, sts, row],
        out_specs=[wide, wide, qks, wide, wide, egs, row],
        out_shape=[SDS((T, D), F32), SDS((T, D), F32), SDS((T // DN_C, 8, DN_C, DN_C), F32), SDS((T, D), F32),
                   SDS((T, D), F32), SDS((T // DN_C, 8, 1, HD), F32), SDS((1, HD), F32)],
        scratch_shapes=[pltpu.VMEM((8, HD, HD), F32)],
        compiler_params=_params(1, VMEM_LIMIT))(do, u, w, qk, qd, kd, egl, st, onw)


def _adamw(w, g, m, v, name):
    R, C = w.shape
    tr = 256 if R % 256 == 0 and R > 256 else R
    tc = 256 if tr == R and R > 256 and C % 256 == 0 else C
    c1 = 1.0 - ADAM_B1 ** ADAM_STEP
    c2 = 1.0 - ADAM_B2 ** ADAM_STEP

    def body(w_ref, g_ref, m_ref, v_ref, d_ref, nm_ref, nv_ref):
        gv = g_ref[...]
        nm = ADAM_B1 * m_ref[...] + (1.0 - ADAM_B1) * gv
        nv = ADAM_B2 * v_ref[...] + (1.0 - ADAM_B2) * (gv * gv)
        nm_ref[...] = nm
        nv_ref[...] = nv
        d_ref[...] = -ADAM_LR * ((nm / c1) / (jnp.sqrt(nv / c2) + ADAM_EPS) + ADAM_WD * w_ref[...])

    spec = pl.BlockSpec((tr, tc), lambda i, j: (i, j))
    return pl.pallas_call(
        body, name=name, grid=(R // tr, C // tc), in_specs=[spec] * 4, out_specs=[spec] * 3,
        out_shape=[SDS((R, C), F32)] * 3, compiler_params=_params(2, VMEM_LIMIT))(w, g, m, v)


def _local_step(x, mem, target, wts, sm):
    kinds = [i % 3 for i in range(DEPTH)]
    mnw = sm["mem_norm_w"].reshape(1, D)
    kv, wkv = None, None
    saved, blocks = [], []
    for i, kind in enumerate(kinds):
        j = i // 3
        npre = sm["norm_pre"][i].reshape(1, D)
        npost = sm["norm_post"][i].reshape(1, D)
        mix, gate, wo_after = wts["blocks"](i, x)
        pm, pg, h = _inproj_fwd(x, npre, mix, gate, kind == 2, f"inproj_fwd_{i}")
        if kv is None:
            wkv = wts["wkv_after"](h)
            kv = _memkv_fwd(mem, mnw, wkv)
        extra = None
        if kind == 0:
            bs3 = jnp.broadcast_to(sm["a_b_s"][j][:, :, None], (8, HD, HD))
            ymix = _gmlp_fwd(pm, sm["a_ln_w"][j].reshape(1, D), sm["a_ln_b"][j].reshape(1, D), sm["a_w_s"][j], bs3,
                             f"gmlp_fwd_{i}")
            extra = bs3
        elif kind == 1:
            ymix = _sconv_fwd(pm, sm["b_conv_w"][j], f"sconv_fwd_{i}")
        else:
            cpre = _dnconv_fwd(pm, sm["c_conv_w"][j], f"dnconv_fwd_{i}")
            alog = jnp.pad(sm["c_a_log"][j], (0, HD - 8)).reshape(1, HD)
            dtb = jnp.pad(sm["c_dt_bias"][j], (0, HD - 8)).reshape(1, HD)
            onw = sm["c_o_norm_w"][j].reshape(1, HD)
            *prep, t_mat = _dn_prep_fwd(cpre, pm, alog, dtb, f"dn_prep_fwd_{i}")
            ymix, st = _dn_scan_fwd(*prep, onw, f"dn_scan_fwd_{i}")
            extra = (cpre, prep, t_mat, st, alog, dtb, onw)
        wo = wo_after(ymix)
        blocks.append((mix, gate, wo))
        layer_in = x
        if i < DEPTH - 1:
            ycat, o, x = _gate_outproj_fwd(ymix, pg, kv, wo, x, npost, None, f"gate_outproj_fwd_{i}")
        else:
            ycat, o, loss, dx = _gate_outproj_fwd(ymix, pg, kv, wo, x, npost, target, f"gate_outproj_fwd_{i}")
        saved.append((layer_in, h, pm, pg, ymix, ycat, o, extra))

    g = {"wm": [None] * DEPTH, "wg": [None] * DEPTH, "wo": [None] * DEPTH, "norm_pre": [None] * DEPTH,
         "norm_post": [None] * DEPTH}
    dkv = jnp.zeros((N_MEM, 2 * D_XA), F32)
    sent = 0.0
    for i in reversed(range(DEPTH)):
        kind, j = kinds[i], i // 3
        xi, h, pm, pg, ymix, ycat, o, extra = saved[i]
        npre = sm["norm_pre"][i].reshape(1, D)
        npost = sm["norm_post"][i].reshape(1, D) + sent
        dobf, g["norm_post"][i], dymix, dpg, dkv = _outproj_gate_bwd(dx, o, npost, blocks[i][2], ymix, pg, kv, dkv,
                                                                     f"outproj_gate_bwd_{i}")
        g["wo"][i] = _matmul_tn(ycat, dobf, f"dwo_{i}")
        if kind == 0:
            dpm, dlnw, dlnb, dws, dbs3 = _gmlp_bwd(dymix, pm, sm["a_ln_w"][j].reshape(1, D),
                                                   sm["a_ln_b"][j].reshape(1, D), sm["a_w_s"][j], extra,
                                                   f"gmlp_bwd_{i}")
            g.setdefault("a_ln_w", {})[j] = dlnw.reshape(D)
            g.setdefault("a_ln_b", {})[j] = dlnb.reshape(D)
            g.setdefault("a_w_s", {})[j] = dws
            g.setdefault("a_b_s", {})[j] = dbs3[:, :, 0]
        elif kind == 1:
            dpm, dcw = _sconv_bwd(dymix, pm, sm["b_conv_w"][j], f"sconv_bwd_{i}")
            g.setdefault("b_conv_w", {})[j] = dcw
        else:
            cpre, prep, t_mat, st, alog, dtb, onw = extra
            *dprep, donw = _dn_scan_bwd(dymix, *prep, st, onw, f"dn_scan_bwd_{i}")
            dcq, dck, dcv, dab, dalog, ddtb = _dn_prep_bwd(*dprep, t_mat, cpre, pm, alog, dtb, f"dn_prep_bwd_{i}")
            dpm, dcw = _dnconv_bwd(dcq, dck, dcv, dab, pm, sm["c_conv_w"][j], f"dnconv_bwd_{i}")
            g.setdefault("c_conv_w", {})[j] = dcw
            g.setdefault("c_a_log", {})[j] = dalog[0, :8]
            g.setdefault("c_dt_bias", {})[j] = ddtb[0, :8]
            g.setdefault("c_o_norm_w", {})[j] = donw[0]
        if kind == 2:
            g["wm"][i] = _matmul_tn(dpm, h, f"dwm_{i}")
            g["wg"][i] = _matmul_tn(dpg, h, f"dwg_{i}")
        else:
            g["wm"][i] = _matmul_tn(h, dpm, f"dwm_{i}", GRAD_TILE[kind])
            g["wg"][i] = _matmul_tn(h, dpg, f"dwg_{i}", GRAD_TILE[kind])
        sent = wts["layer_done"](i, g)
        dx, g["norm_pre"][i] = _inproj_bwd(dpm, dpg, xi, npre + sent, blocks[i][0], blocks[i][1], kind == 2, dx,
                                           f"inproj_bwd_{i}")
    g["mem_norm_w"], g["wkv"] = _memkv_bwd(mem, mnw, wkv, dkv)
    return loss[0, 0], dx, g


ANY = pl.BlockSpec(memory_space=pl.ANY)


def _place():
    return lax.axis_index("x"), lax.axis_index("y"), lax.axis_index("c")


def _add(a, b, name):
    def body(a_ref, b_ref, o_ref):
        o_ref[...] = a_ref[...] + b_ref[...]

    return pl.pallas_call(body, name=name, out_shape=SDS(a.shape, a.dtype), compiler_params=_params(0, VMEM_LIMIT))(a, b)


def _sum4(own, land):
    def body(own_ref, l_ref, o_ref):
        chip = 2 * lax.axis_index("x") + lax.axis_index("y")
        acc = jnp.where(chip == 0, own_ref[...], l_ref[0])
        for s in range(1, 4):
            acc = acc + jnp.where(chip == s, own_ref[...], l_ref[s])
        o_ref[...] = acc

    return pl.pallas_call(body, name="sum_small", out_shape=SDS(own.shape, own.dtype),
                          compiler_params=_params(0, VMEM_LIMIT))(own, land)


C_ROWS = 1312
_REDUCE_CHUNK = {512: 256, 2560: 640}
W_CLASSES = {"in0": (512, 256)}


def _chunk_list(specs):
    return [(k, r, chunk) for k, (half, chunk) in enumerate(specs) for r in range(0, half, chunk)]


def _gather_classes(arrs, specs, vec):
    n = len(arrs)
    chunks = _chunk_list(specs)
    nc = len(chunks)

    def body(*refs):
        ins, vec_ref, outs, ov_ref = refs[:n], refs[n], refs[n + 1:2 * n + 1], refs[2 * n + 1]
        ici_send, ici_recv, d2d_send, d2d_recv, vec_send, vec_recv = refs[2 * n + 2:]
        x, y, c = _place()
        chip = 2 * x + y
        peers = [(1 - x, y), (x, 1 - y), (1 - x, 1 - y)]

        def rows(ci, half):
            k, r, cnt = chunks[ci]
            return k, pl.ds(half * specs[k][0] + r, cnt)

        def over_ici(j, ci, slab):
            px, py = peers[j]
            k, rs = rows(ci, c)
            return pltpu.make_async_remote_copy(
                src_ref=ins[k].at[rs], dst_ref=outs[k].at[slab, rs], send_sem=ici_send.at[j * nc + ci],
                recv_sem=ici_recv.at[j * nc + ci], device_id=(px, py, c), device_id_type=MESH)

        def over_d2d(j, ci, half):
            px, py = peers[j]
            k, rs = rows(ci, half)
            where = outs[k].at[2 * px + py, rs]
            return pltpu.make_async_remote_copy(
                src_ref=where, dst_ref=where, send_sem=d2d_send.at[j * nc + ci], recv_sem=d2d_recv.at[j * nc + ci],
                device_id=(x, y, 1 - c), device_id_type=MESH)

        def small(j, slab):
            px, py = peers[j]
            return pltpu.make_async_remote_copy(
                src_ref=vec_ref, dst_ref=ov_ref.at[slab], send_sem=vec_send.at[j], recv_sem=vec_recv.at[j],
                device_id=(px, py, c), device_id_type=MESH)

        sends = [small(j, chip) for j in range(3)] + [over_ici(j, ci, chip) for ci in range(nc) for j in range(3)]
        for cp in sends:
            cp.start()
        forwards = []
        for ci in range(nc):
            for j, (px, py) in enumerate(peers):
                over_ici(j, ci, 2 * px + py).wait_recv()
                forwards.append(over_d2d(j, ci, c))
                forwards[-1].start()
        for ci in range(nc):
            for j in range(3):
                over_d2d(j, ci, 1 - c).wait_recv()
        for j, (px, py) in enumerate(peers):
            small(j, 2 * px + py).wait_recv()
        for cp in sends + forwards:
            cp.wait_send()

    dma = pltpu.SemaphoreType.DMA
    return pl.pallas_call(
        body, name="gather_weights", in_specs=[ANY] * (n + 1), out_specs=[ANY] * (n + 1),
        out_shape=[SDS((4,) + a.shape, a.dtype) for a in arrs] + [SDS((4,) + vec.shape, vec.dtype)],
        scratch_shapes=[dma((3 * nc,)), dma((3 * nc,)), dma((3 * nc,)), dma((3 * nc,)), dma((3,)), dma((3,))])(*arrs, vec)


def _swap_classes(grads, specs, small):
    n = len(grads)
    chunks = _chunk_list(specs)

    def body(*refs):
        ins, s_ref, outs, os_ref, send_sems, recv_sems = refs[:n], refs[n], refs[n + 1:2 * n + 1], *refs[2 * n + 1:]
        x, y, c = _place()
        copies = []
        for s in range(4):
            for k, r, cnt in chunks:
                copies.append(pltpu.make_async_remote_copy(
                    src_ref=ins[k].at[s, 1 - c, pl.ds(r, cnt)], dst_ref=outs[k].at[s, pl.ds(r, cnt)],
                    send_sem=send_sems.at[len(copies)], recv_sem=recv_sems.at[len(copies)],
                    device_id=(x, y, 1 - c), device_id_type=MESH))
        copies.append(pltpu.make_async_remote_copy(
            src_ref=s_ref, dst_ref=os_ref, send_sem=send_sems.at[len(copies)], recv_sem=recv_sems.at[len(copies)],
            device_id=(x, y, 1 - c), device_id_type=MESH))
        for cp in copies:
            cp.start()
        for cp in copies:
            cp.wait_recv()
        for cp in copies:
            cp.wait_send()

    m = 4 * len(chunks) + 1
    return pl.pallas_call(
        body, name="swap_halves", in_specs=[ANY] * (n + 1), out_specs=[ANY] * (n + 1),
        out_shape=[SDS((4, g.shape[2], g.shape[3]), F32) for g in grads] + [SDS(small.shape, F32)],
        scratch_shapes=[pltpu.SemaphoreType.DMA((m,)), pltpu.SemaphoreType.DMA((m,))])(*grads, small)


def _pair_sum_class(g, other, chunk, name):
    _, _, half, w = g.shape

    def body(g_ref, o_ref, pb_ref, own_ref):
        x, y, c = _place()
        v = jnp.where(c == 0, g_ref[0], g_ref[1]) + o_ref[...]
        pb_ref[...] = v.astype(BF16)

        @pl.when(pl.program_id(1) == 2 * x + y)
        def _():
            own_ref[...] = v

    return pl.pallas_call(
        body, name=name, grid=(half // chunk, 4),
        in_specs=[pl.BlockSpec((None, 2, chunk, w), lambda i, s: (s, 0, i, 0)),
                  pl.BlockSpec((None, chunk, w), lambda i, s: (s, i, 0))],
        out_specs=[pl.BlockSpec((None, chunk, w), lambda i, s: (s, i, 0)), pl.BlockSpec((chunk, w), lambda i, s: (i, 0))],
        out_shape=[SDS((4, half, w), BF16), SDS((half, w), F32)],
        compiler_params=_params(2, VMEM_LIMIT))(g, other)


def _exchange_classes(pbs, specs, ps):
    n = len(pbs)
    chunks = _chunk_list(specs)
    per_peer = len(chunks) + 1

    def body(*refs):
        ins, ps_ref, outs, ls_ref, send_sems, recv_sems = refs[:n], refs[n], refs[n + 1:2 * n + 1], *refs[2 * n + 1:]
        x, y, c = _place()
        chip = 2 * x + y
        peers = [(1 - x, y), (x, 1 - y), (1 - x, 1 - y)]

        def copies(slab_of):
            out = []
            for j, (px, py) in enumerate(peers):
                for k, r, cnt in chunks:
                    out.append(pltpu.make_async_remote_copy(
                        src_ref=ins[k].at[2 * px + py, pl.ds(r, cnt)], dst_ref=outs[k].at[slab_of(j), pl.ds(r, cnt)],
                        send_sem=send_sems.at[len(out)], recv_sem=recv_sems.at[len(out)], device_id=(px, py, c),
                        device_id_type=MESH))
                out.append(pltpu.make_async_remote_copy(
                    src_ref=ps_ref, dst_ref=ls_ref.at[slab_of(j)], send_sem=send_sems.at[len(out)],
                    recv_sem=recv_sems.at[len(out)], device_id=(px, py, c), device_id_type=MESH))
            return out

        sends = copies(lambda j: chip)
        for cp in sends:
            cp.start()
        for cp in copies(lambda j: 2 * peers[j][0] + peers[j][1]):
            cp.wait_recv()
        for cp in sends:
            cp.wait_send()

    m = 3 * per_peer
    return pl.pallas_call(
        body, name="chip_exchange", in_specs=[ANY] * (n + 1), out_specs=[ANY] * (n + 1),
        out_shape=[SDS(p.shape, BF16) for p in pbs] + [SDS((4,) + ps.shape, F32)],
        scratch_shapes=[pltpu.SemaphoreType.DMA((m,)), pltpu.SemaphoreType.DMA((m,))])(*pbs, ps)


def _chip_sum_class(own, land, chunk, name):
    half, w = own.shape

    def body(own_ref, l_ref, o_ref):
        chip = 2 * lax.axis_index("x") + lax.axis_index("y")
        acc = jnp.where(chip == 0, own_ref[...], l_ref[0].astype(F32))
        for s in range(1, 4):
            acc = acc + jnp.where(chip == s, own_ref[...], l_ref[s].astype(F32))
        o_ref[...] = acc

    return pl.pallas_call(
        body, name=name, grid=(half // chunk,),
        in_specs=[pl.BlockSpec((chunk, w), lambda i: (i, 0)), pl.BlockSpec((4, chunk, w), lambda i: (0, i, 0))],
        out_specs=pl.BlockSpec((chunk, w), lambda i: (i, 0)), out_shape=SDS((half, w), F32),
        compiler_params=_params(1, VMEM_LIMIT))(own, land)


def _share_classes(rs, specs):
    n = len(rs)
    chunks = _chunk_list(specs)

    def body(*refs):
        ins, outs, send_sems, recv_sems = refs[:n], refs[n:2 * n], *refs[2 * n:]
        x, y, c = _place()
        copies = [pltpu.make_async_remote_copy(
            src_ref=ins[k].at[pl.ds(r, cnt)], dst_ref=outs[k].at[pl.ds(r, cnt)], send_sem=send_sems.at[i],
            recv_sem=recv_sems.at[i], device_id=(x, y, 1 - c), device_id_type=MESH)
            for i, (k, r, cnt) in enumerate(chunks)]
        for cp in copies:
            cp.start()
        for cp in copies:
            cp.wait_recv()
        for cp in copies:
            cp.wait_send()

    m = len(chunks)
    return pl.pallas_call(
        body, name="share_half", in_specs=[ANY] * n, out_specs=[ANY] * n, out_shape=[SDS(r.shape, F32) for r in rs],
        scratch_shapes=[pltpu.SemaphoreType.DMA((m,)), pltpu.SemaphoreType.DMA((m,))])(*rs)


_HBM = pl.BlockSpec(memory_space=pltpu.HBM)
_SEM = pl.BlockSpec(memory_space=pltpu.SEMAPHORE)
_EFFECT = pltpu.SideEffectType.DATAFLOW_SIDE_EFFECTING


def _chip_peers():
    x, y, c = _place()
    return [(1 - x, y, c), (x, 1 - y, c), (1 - x, 1 - y, c)]


def _send_shard_start(v, name):
    def body(v_ref, land_ref, send_sems, recv_sems, v_thru, land_thru, token):
        x, y, c = _place()
        for j, peer in enumerate(_chip_peers()):
            pltpu.make_async_remote_copy(src_ref=v_ref, dst_ref=land_ref.at[2 * x + y], send_sem=send_sems.at[j],
                                         recv_sem=recv_sems.at[j], device_id=peer, device_id_type=MESH).start()
        token[...] = jnp.zeros_like(token)

    land_shape = (4,) + v.shape
    return pl.pallas_call(
        body, name=name,
        out_shape=(pltpu.SemaphoreType.DMA((3,)), pltpu.SemaphoreType.DMA((3,)), pltpu.HBM(v.shape, v.dtype),
                   pltpu.HBM(land_shape, v.dtype), SDS((8, 128), F32)),
        in_specs=(_HBM, _HBM), out_specs=(_SEM, _SEM, _HBM, _HBM, pl.BlockSpec(memory_space=pltpu.VMEM)),
        input_output_aliases={0: 2, 1: 3}, compiler_params=pltpu.CompilerParams(has_side_effects=_EFFECT),
    )(pltpu.with_memory_space_constraint(v, pltpu.HBM),
      pltpu.with_memory_space_constraint(lax.empty(land_shape, v.dtype), pltpu.HBM))


def _xor_peer(r):
    x, y, c = _place()
    return (1 - x if (r >> 2) & 1 else x, 1 - y if (r >> 1) & 1 else y, 1 - c if r & 1 else c)


def _send_pieces_start(parts, name):
    n = len(parts)

    def body(*refs):
        ins, lands = refs[:n], refs[n:2 * n]
        send_sems, recv_sems = refs[2 * n:2 * n + 2]
        token = refs[-1]
        x, y, c = _place()
        for r in range(1, 8):
            px, py, pc = _xor_peer(r)
            for k in range(n):
                pltpu.make_async_remote_copy(
                    src_ref=ins[k].at[2 * px + py, pc], dst_ref=lands[k].at[4 * x + 2 * y + c],
                    send_sem=send_sems.at[(r - 1) * n + k], recv_sem=recv_sems.at[(r - 1) * n + k],
                    device_id=(px, py, pc), device_id_type=MESH).start()
        token[...] = jnp.zeros_like(token)

    land_shapes = [(8,) + p.shape[2:] for p in parts]
    hbm = [pltpu.HBM(p.shape, p.dtype) for p in parts] + [pltpu.HBM(s, p.dtype) for s, p in zip(land_shapes, parts)]
    operands = [pltpu.with_memory_space_constraint(p, pltpu.HBM) for p in parts]
    operands += [pltpu.with_memory_space_constraint(lax.empty(s, p.dtype), pltpu.HBM) for s, p in zip(land_shapes, parts)]
    return pl.pallas_call(
        body, name=name,
        out_shape=(pltpu.SemaphoreType.DMA((7 * n,)), pltpu.SemaphoreType.DMA((7 * n,)), *hbm, SDS((8, 128), F32)),
        in_specs=(_HBM,) * (2 * n), out_specs=(_SEM, _SEM) + (_HBM,) * (2 * n) + (pl.BlockSpec(memory_space=pltpu.VMEM),),
        input_output_aliases={i: 2 + i for i in range(2 * n)},
        compiler_params=pltpu.CompilerParams(has_side_effects=_EFFECT))(*operands)


def _send_pieces_wait(started, after, name):
    send_sems, recv_sems, *thru, _ = started
    n = len(thru) // 2

    def body(*refs):
        ins, lands = refs[:n], refs[n:2 * n]
        send_sems, recv_sems = refs[2 * n:2 * n + 2]
        for r in range(1, 8):
            px, py, pc = _xor_peer(r)
            for k in range(n):
                copy = pltpu.make_async_remote_copy(
                    src_ref=ins[k].at[2 * px + py, pc], dst_ref=lands[k].at[4 * px + 2 * py + pc],
                    send_sem=send_sems.at[(r - 1) * n + k], recv_sem=recv_sems.at[(r - 1) * n + k],
                    device_id=(px, py, pc), device_id_type=MESH)
                copy.wait_send()
                copy.wait_recv()

    return pl.pallas_call(
        body, name=name, out_shape=tuple(pltpu.HBM(t.shape, t.dtype) for t in thru),
        in_specs=(_HBM,) * (2 * n) + (_SEM, _SEM, pl.BlockSpec(memory_space=pl.ANY)), out_specs=(_HBM,) * (2 * n),
        input_output_aliases={i: i for i in range(2 * n)},
        compiler_params=pltpu.CompilerParams(has_side_effects=_EFFECT))(*thru, send_sems, recv_sems, after)[n:]


def _sum8_class(own, land, chunk, name):
    rows, w = own.shape

    def body(own_ref, l_ref, o_ref):
        x, y, c = _place()
        me = 4 * x + 2 * y + c
        acc = jnp.where(me == 0, own_ref[...], l_ref[0].astype(F32))
        for d in range(1, 8):
            acc = acc + jnp.where(me == d, own_ref[...], l_ref[d].astype(F32))
        o_ref[...] = acc

    return pl.pallas_call(
        body, name=name, grid=(rows // chunk,),
        in_specs=[pl.BlockSpec((chunk, w), lambda i: (i, 0)), pl.BlockSpec((8, chunk, w), lambda i: (0, i, 0))],
        out_specs=pl.BlockSpec((chunk, w), lambda i: (i, 0)), out_shape=SDS((rows, w), F32),
        compiler_params=_params(1, VMEM_LIMIT))(own, land)


def _send_shard_wait(send_sems, recv_sems, v_thru, land_thru, after, name):
    def body(v_ref, land_ref, send_sems, recv_sems, after_ref, v_dead, got_ref):
        for j, (px, py, pc) in enumerate(_chip_peers()):
            copy = pltpu.make_async_remote_copy(src_ref=v_ref, dst_ref=land_ref.at[2 * px + py], send_sem=send_sems.at[j],
                                                recv_sem=recv_sems.at[j], device_id=(px, py, pc), device_id_type=MESH)
            copy.wait_send()
            copy.wait_recv()

    return pl.pallas_call(
        body, name=name,
        out_shape=(pltpu.HBM(v_thru.shape, v_thru.dtype), pltpu.HBM(land_thru.shape, land_thru.dtype)),
        in_specs=(_HBM, _HBM, _SEM, _SEM, pl.BlockSpec(memory_space=pl.ANY)), out_specs=(_HBM, _HBM),
        input_output_aliases={0: 0, 1: 1}, compiler_params=pltpu.CompilerParams(has_side_effects=_EFFECT),
    )(v_thru, land_thru, send_sems, recv_sems, after)[1]


_SMALL = ["mem_norm_w", "norm_pre", "norm_post", "a_ln_w", "a_ln_b", "a_w_s", "a_b_s", "b_conv_w", "c_conv_w",
          "c_a_log", "c_dt_bias", "c_o_norm_w"]
_SMALL_SHAPES = {"mem_norm_w": (D,), "norm_pre": (4, D), "norm_post": (4, D), "a_ln_w": (2, D), "a_ln_b": (2, D),
                 "a_w_s": (2, 8, HD, HD), "a_b_s": (2, 8, HD), "b_conv_w": (1, 3, D), "c_conv_w": (1, 4, 3 * D),
                 "c_a_log": (1, 8), "c_dt_bias": (1, 8), "c_o_norm_w": (1, HD)}
_SHARDED_SMALL = {"a_ln_w": D // 4, "a_ln_b": D // 4, "b_conv_w": D // 4, "c_conv_w": 3 * D // 4}
_SMALL_ROWS = 288


def _size(shape):
    n = 1
    for d in shape:
        n *= d
    return n


def kernel(x, mem, mem_norm_w, w_mem_kv, norm_pre, norm_post, w_out, a_w_in, a_ln_w, a_ln_b, a_w_s, a_b_s, b_w_in, b_conv_w, c_w_in, c_conv_w, c_a_log, c_dt_bias, c_o_norm_w, loss_target, m_mem_norm_w, m_w_mem_kv, m_norm_pre, m_norm_post, m_w_out, m_a_w_in, m_a_ln_w, m_a_ln_b, m_a_w_s, m_a_b_s, m_b_w_in, m_b_conv_w, m_c_w_in, m_c_conv_w, m_c_a_log, m_c_dt_bias, m_c_o_norm_w, v_mem_norm_w, v_w_mem_kv, v_norm_pre, v_norm_post, v_w_out, v_a_w_in, v_a_ln_w, v_a_ln_b, v_a_w_s, v_a_b_s, v_b_w_in, v_b_conv_w, v_c_w_in, v_c_conv_w, v_c_a_log, v_c_dt_bias, v_c_o_norm_w):
    names = ["mem_norm_w", "w_mem_kv", "norm_pre", "norm_post", "w_out", "a_w_in", "a_ln_w", "a_ln_b", "a_w_s", "a_b_s",
             "b_w_in", "b_conv_w", "c_w_in", "c_conv_w", "c_a_log", "c_dt_bias", "c_o_norm_w"]
    w = dict(zip(names, [mem_norm_w, w_mem_kv, norm_pre, norm_post, w_out, a_w_in, a_ln_w, a_ln_b, a_w_s, a_b_s, b_w_in,
                         b_conv_w, c_w_in, c_conv_w, c_a_log, c_dt_bias, c_o_norm_w]))
    m = dict(zip(names, [m_mem_norm_w, m_w_mem_kv, m_norm_pre, m_norm_post, m_w_out, m_a_w_in, m_a_ln_w, m_a_ln_b, m_a_w_s,
                         m_a_b_s, m_b_w_in, m_b_conv_w, m_c_w_in, m_c_conv_w, m_c_a_log, m_c_dt_bias, m_c_o_norm_w]))
    v = dict(zip(names, [v_mem_norm_w, v_w_mem_kv, v_norm_pre, v_norm_post, v_w_out, v_a_w_in, v_a_ln_w, v_a_ln_b, v_a_w_s,
                         v_a_b_s, v_b_w_in, v_b_conv_w, v_c_w_in, v_c_conv_w, v_c_a_log, v_c_dt_bias, v_c_o_norm_w]))
    chip = 2 * lax.axis_index("x") + lax.axis_index("y")

    def rows_of_ct(a):
        return a[0].T

    def with_mine(gathered, own):
        return lax.dynamic_update_slice(gathered, own[None], (chip,) + (0,) * own.ndim)

    first = [a_w_in[0].astype(BF16)]
    vec = jnp.concatenate([a_ln_w.reshape(-1), a_ln_b.reshape(-1), b_conv_w.reshape(-1), c_conv_w.reshape(-1)])
    vec = jnp.pad(vec, (0, 8 * D - vec.shape[0])).reshape(8, D)
    ga0, gvec = _gather_classes(first, [W_CLASSES[k] for k in W_CLASSES], vec)
    ga0, gvec = with_mine(ga0, first[0]), with_mine(gvec, vec)
    gv = gvec.reshape(4, 8 * D)
    w_in = {1: b_w_in[0], 2: jnp.pad(rows_of_ct(c_w_in), ((0, C_ROWS - 1284), (0, 0))), 3: a_w_in[1]}
    order = [("kv", 0, w_mem_kv), ("out", 0, w_out[0])]
    for i in range(1, DEPTH):
        order += [("in", i, w_in[i]), ("out", i, w_out[i])]
    later, sent, started = {}, {}, 0.0 * ga0[0, 0, 0].astype(F32)
    for kind, i, a in order:
        later[kind, i] = (a + started).astype(BF16)
        sent[kind, i] = _send_shard_start(later[kind, i], f"send_w_{kind}_{i}")
        started = sent[kind, i][4][0, 0]

    def arrived(k, after):
        return with_mine(_send_shard_wait(*sent[k][:4], after, f"wait_w_{k[0]}_{k[1]}"), later[k])

    def blocks(i, after):
        def wo_after(later_value):
            return arrived(("out", i), later_value).reshape(D_CAT, D)

        if i == 0:
            return [ga0[0], ga0[1]], [ga0[2], ga0[3]], wo_after
        got = arrived(("in", i), after)
        if i == 1:
            return [got[0], got[1], got[2][:, :512]], [got[2][:, 512:], got[3]], wo_after
        if i == 3:
            return [got[0], got[1]], [got[2], got[3]], wo_after
        fct = got[:, :1284].reshape(5136, D)
        c_ab = jnp.concatenate([fct[3 * D:3 * D + 16], jnp.zeros((AB_PAD - 16, D), BF16)], axis=0)
        return [fct[:3 * D], c_ab], [fct[3 * D + 16:]], wo_after
    sm = {"mem_norm_w": mem_norm_w, "norm_pre": norm_pre + started, "norm_post": norm_post, "a_w_s": a_w_s, "a_b_s": a_b_s,
          "c_a_log": c_a_log, "c_dt_bias": c_dt_bias, "c_o_norm_w": c_o_norm_w,
          "a_ln_w": gv[:, 0:512].reshape(4, 2, 256).transpose(1, 0, 2).reshape(2, D),
          "a_ln_b": gv[:, 512:1024].reshape(4, 2, 256).transpose(1, 0, 2).reshape(2, D),
          "b_conv_w": gv[:, 1024:1792].reshape(4, 1, 3, 256).transpose(1, 2, 0, 3).reshape(1, 3, D),
          "c_conv_w": gv[:, 1792:4864].reshape(4, 1, 4, 768).transpose(1, 2, 0, 3).reshape(1, 4, 3 * D)}
    wts = {"wkv_after": lambda after: arrived(("kv", 0), after).reshape(D, 2 * D_XA), "blocks": blocks}

    def layer_grads(i, g):
        if i % 3 == 2:
            gct = jnp.concatenate([g["wm"][i][:3 * D + 16], g["wg"][i]], axis=0).reshape(4, 1284, D)
            w_in = jnp.pad(gct, ((0, 0), (0, C_ROWS - 1284), (0, 0)))
        else:
            w_in = jnp.concatenate([g["wm"][i], g["wg"][i]], axis=0).reshape(4, -1, GRAD_TILE[i % 3])
        out = {f"in{i}": w_in, f"out{i}": g["wo"][i].reshape(4, 384, D)}
        return {k: a.reshape(4, 2, a.shape[1] // 2, a.shape[2]) for k, a in out.items()}

    pending = {}

    def layer_done(i, g):
        halves = layer_grads(i, g)
        started = _send_pieces_start([h.astype(BF16) for h in halves.values()], f"send_grads_{i}")
        pending[i] = (started, halves)
        return started[-1][0, 0]

    wts["layer_done"] = layer_done

    loss, dx, g = _local_step(x[0], mem[0], loss_target[0], wts, sm)
    loss = lax.psum(loss, ("x", "y", "c"))

    core = lax.axis_index("c")
    mine, specs = {}, {}
    for i in reversed(range(DEPTH)):
        started, halves = pending[i]
        lands = _send_pieces_wait(started, dx, f"wait_grads_{i}")
        for (k, h), land in zip(halves.items(), lands):
            own = lax.dynamic_index_in_dim(lax.dynamic_index_in_dim(h, chip, 0, False), core, 0, False)
            specs[k] = (own.shape[0], _REDUCE_CHUNK.get(own.shape[0], own.shape[0]))
            mine[k] = _sum8_class(own, land, specs[k][1], f"sum8_{k}")

    first = {"kv": g["wkv"].reshape(4, 2, 128, D)}
    first_specs = [(h.shape[2], _REDUCE_CHUNK.get(h.shape[2], h.shape[2])) for h in first.values()]
    halves = list(first.values())
    gs = {"mem_norm_w": g["mem_norm_w"], "norm_pre": jnp.concatenate(g["norm_pre"]),
          "norm_post": jnp.concatenate(g["norm_post"])}
    for n in _SMALL[3:]:
        gs[n] = jnp.stack([g[n][j] for j in sorted(g[n])])
    flat = jnp.concatenate([gs[n].reshape(-1) for n in _SMALL])
    small = jnp.pad(flat, (0, _SMALL_ROWS * D - flat.shape[0])).reshape(_SMALL_ROWS, D)
    *others, other_small = _swap_classes(halves, first_specs, small)
    pairs = [_pair_sum_class(h, o, s[1], f"pair_sum_{k}") for k, h, o, s in zip(first, halves, others, first_specs)]
    pair_small = _add(small, other_small, "pair_sum_small")
    *lands, land_small = _exchange_classes([p[0] for p in pairs], first_specs, pair_small)
    for k, p, land, s in zip(first, pairs, lands, first_specs):
        mine[k], specs[k] = _chip_sum_class(p[1], land, s[1], f"chip_sum_{k}"), s
    theirs = _share_classes(list(mine.values()), [specs[k] for k in mine])
    south = core == 0
    sh = {k: jnp.concatenate([jnp.where(south, a, b), jnp.where(south, b, a)], axis=0)
          for (k, a), b in zip(mine.items(), theirs)}
    grads = {"a_w_in": jnp.stack([sh["in0"], sh["in3"]]),
             "b_w_in": sh["in1"].reshape(5, D, 256).transpose(1, 0, 2).reshape(b_w_in.shape),
             "c_w_in": sh["in2"][:1284], "w_out": jnp.stack([sh[f"out{i}"] for i in range(DEPTH)]),
             "w_mem_kv": sh["kv"]}
    flat = _sum4(pair_small, land_small).reshape(-1)
    off = 0
    for n in _SMALL:
        shape = _SMALL_SHAPES[n]
        full = flat[off:off + _size(shape)].reshape(shape)
        off += _size(shape)
        if n in _SHARDED_SMALL:
            full = lax.dynamic_slice_in_dim(full, chip * _SHARDED_SMALL[n], _SHARDED_SMALL[n], axis=len(shape) - 1)
        grads[n] = full

    delta, new_m, new_v = {}, {}, {}
    for n in names:
        shape = w[n].shape
        if n == "c_w_in":
            d_, m_, v_ = _adamw(rows_of_ct(w[n]), grads[n], rows_of_ct(m[n]), rows_of_ct(v[n]), f"adamw_{n}")
            delta[n], new_m[n], new_v[n], grads[n] = d_.T[None], m_.T[None], v_.T[None], grads[n].T[None]
            continue
        view = (1, shape[0]) if len(shape) == 1 else (_size(shape[:-1]), shape[-1])
        d_, m_, v_ = _adamw(w[n].reshape(view), grads[n].reshape(view), m[n].reshape(view), v[n].reshape(view),
                            f"adamw_{n}")
        delta[n], new_m[n], new_v[n] = d_.reshape(shape), m_.reshape(shape), v_.reshape(shape)
    return (loss, dx[None], *[grads[n].reshape(w[n].shape) for n in names], *[delta[n] for n in names],
            *[new_m[n] for n in names], *[new_v[n] for n in names])
```

```python
import functools

import jax
import jax.numpy as jnp
from jax import lax
from jax.experimental import pallas as pl
from jax.experimental.pallas import tpu as pltpu

F32 = jnp.float32
BF16 = jnp.bfloat16
HI = lax.Precision.HIGHEST
MESH = pl.DeviceIdType.MESH
SDS = jax.ShapeDtypeStruct

D = 1024
D_XA = 512
D_CAT = 1536
N_MEM = 256
HD = 128
DEPTH = 4
EPS = 1e-6
TT = 512
DN_C = 64
DN_TB = 512
HALO = 8
AB_PAD = 128
VMEM_LIMIT = 56 * 1024 * 1024
GRAD_TILE = {0: 1024, 1: 256}

ADAM_LR, ADAM_B1, ADAM_B2, ADAM_EPS, ADAM_WD, ADAM_STEP = 0.001, 0.9, 0.999, 1e-08, 0.01, 10


def _params(n_grid, vmem=None):
    return pltpu.CompilerParams(dimension_semantics=("arbitrary",) * n_grid, vmem_limit_bytes=vmem)


def _rms(x, w):
    return x * lax.rsqrt(jnp.mean(x * x, axis=-1, keepdims=True) + EPS) * w


def _dot_nn(a, b):
    return jnp.dot(a.astype(BF16), b.astype(BF16), preferred_element_type=F32)


def _dot_nt(a, b):
    return lax.dot_general(a.astype(BF16), b.astype(BF16), (((1,), (1,)), ((), ())), preferred_element_type=F32)


def _dot_tn(a, b):
    return lax.dot_general(a.astype(BF16), b.astype(BF16), (((0,), (0,)), ((), ())), preferred_element_type=F32)


@jax.custom_vjp
def mm(a, b):
    return _dot_nn(a, b)


mm.defvjp(lambda a, b: (_dot_nn(a, b), (a, b)), lambda r, g: (_dot_nt(g, r[1]), _dot_tn(r[0], g)))


@jax.custom_vjp
def mm_nt(a, b):
    return _dot_nt(a, b)


mm_nt.defvjp(lambda a, b: (_dot_nt(a, b), (a, b)), lambda r, g: (_dot_nn(g, r[1]), _dot_tn(g, r[0])))


def _row_spec(width, tile=TT):
    return pl.BlockSpec((tile, width), lambda i: (i, 0))


def _full_spec(shape):
    return pl.BlockSpec(shape, lambda *_: (0,) * len(shape))


def _widths(blocks, transposed):
    return [b.shape[0 if transposed else 1] for b in blocks]


def _inproj_fwd(x, nw, mix, gate, transposed, name):
    T, nm = x.shape[0], len(mix)
    M, G = sum(_widths(mix, transposed)), sum(_widths(gate, transposed))

    def body(x_ref, nw_ref, *refs):
        blocks, (pm_ref, pg_ref, h_ref) = refs[:-3], refs[-3:]
        h = _rms(x_ref[...], nw_ref[...]).astype(BF16)
        h_ref[...] = h
        for p_ref, group in ((pm_ref, blocks[:nm]), (pg_ref, blocks[nm:])):
            off = 0
            for w_ref in group:
                w = w_ref.shape[0 if transposed else 1]
                p_ref[:, off:off + w] = _dot_nt(h, w_ref[...]) if transposed else _dot_nn(h, w_ref[...])
                off += w

    return pl.pallas_call(
        body, name=name, grid=(T // TT,),
        in_specs=[_row_spec(D), _full_spec((1, D))] + [_full_spec(b.shape) for b in mix + gate],
        out_specs=[_row_spec(M), _row_spec(G), _row_spec(D)],
        out_shape=[SDS((T, M), F32), SDS((T, G), F32), SDS((T, D), BF16)],
        compiler_params=_params(1, VMEM_LIMIT))(x, nw, *mix, *gate)


def _inproj_bwd(dpm, dpg, x, nw, mix, gate, transposed, dxc, name):
    T, nm = x.shape[0], len(mix)
    M, G = sum(_widths(mix, transposed)), sum(_widths(gate, transposed))

    def body(dpm_ref, dpg_ref, x_ref, nw_ref, *refs):
        blocks, (dxc_ref, dx_ref, dnw_ref) = refs[:-3], refs[-3:]
        dh = None
        for dp_ref, group in ((dpm_ref, blocks[:nm]), (dpg_ref, blocks[nm:])):
            off = 0
            for w_ref in group:
                w = w_ref.shape[0 if transposed else 1]
                dp = dp_ref[:, off:off + w]
                part = _dot_nn(dp, w_ref[...]) if transposed else _dot_nt(dp, w_ref[...])
                dh = part if dh is None else dh + part
                off += w
        _, vjp = jax.vjp(_rms, x_ref[...], nw_ref[...])
        dxr, dnw = vjp(dh)
        dx_ref[...] = dxc_ref[...] + dxr

        @pl.when(pl.program_id(0) == 0)
        def _():
            dnw_ref[...] = jnp.zeros_like(dnw_ref)
        dnw_ref[...] += dnw

    return pl.pallas_call(
        body, name=name, grid=(T // TT,),
        in_specs=[_row_spec(M), _row_spec(G), _row_spec(D), _full_spec((1, D))]
        + [_full_spec(b.shape) for b in mix + gate] + [_row_spec(D)],
        out_specs=[_row_spec(D), _full_spec((1, D))],
        out_shape=[SDS((T, D), F32), SDS((1, D), F32)],
        compiler_params=_params(1, VMEM_LIMIT))(dpm, dpg, x, nw, *mix, *gate, dxc)


def _matmul_tn(a, b, name, sub=None):
    T, K = a.shape
    N = b.shape[1]
    tn = 1024 if N % 1024 == 0 else (640 if N % 640 == 0 else N)
    tt = min(2048 if K <= D_CAT else 1024, T)
    n_sub = 1 if sub is None else tn // sub

    def body(a_ref, b_ref, o_ref):
        @pl.when(pl.program_id(1) == 0)
        def _():
            o_ref[...] = jnp.zeros_like(o_ref)
        res = _dot_tn(a_ref[...], b_ref[...])
        if sub is None:
            o_ref[...] += res
        else:
            for i in range(n_sub):
                o_ref[i] += res[:, i * sub:(i + 1) * sub]

    if sub is None:
        out_spec, out_shape = pl.BlockSpec((K, tn), lambda j, t: (0, j)), SDS((K, N), F32)
    else:
        out_spec, out_shape = pl.BlockSpec((n_sub, K, sub), lambda j, t: (j, 0, 0)), SDS((N // sub, K, sub), F32)
    return pl.pallas_call(
        body, name=name, grid=(N // tn, T // tt),
        in_specs=[pl.BlockSpec((tt, K), lambda j, t: (t, 0)), pl.BlockSpec((tt, tn), lambda j, t: (t, j))],
        out_specs=out_spec, out_shape=out_shape,
        compiler_params=_params(2, VMEM_LIMIT))(a, b)


def _memkv_fn(mem, w, wkv):
    return mm(_rms(mem, w), wkv)


def _memkv_fwd(mem, w, wkv):
    def body(mem_ref, w_ref, wkv_ref, kv_ref):
        kv_ref[...] = _memkv_fn(mem_ref[...], w_ref[...], wkv_ref[...])

    return pl.pallas_call(body, name="memkv_fwd", out_shape=SDS((N_MEM, 2 * D_XA), F32),
                          compiler_params=_params(0, VMEM_LIMIT))(mem, w, wkv)


def _memkv_bwd(mem, w, wkv, dkv):
    def body(mem_ref, w_ref, wkv_ref, dkv_ref, dw_ref, dwkv_ref):
        _, vjp = jax.vjp(functools.partial(_memkv_fn, mem_ref[...]), w_ref[...], wkv_ref[...].astype(F32))
        dw, dwkv = vjp(dkv_ref[...])
        dw_ref[...] = dw
        dwkv_ref[...] = dwkv

    return pl.pallas_call(body, name="memkv_bwd", out_shape=[SDS((1, D), F32), SDS((D, 2 * D_XA), F32)],
                          compiler_params=_params(0, VMEM_LIMIT))(mem, w, wkv, dkv)


def _attn_gate(ymix, qx, z, *kvs):
    outs = []
    for j in range(4):
        s = mm_nt(qx[:, j * HD:(j + 1) * HD], kvs[j]) * (HD ** -0.5)
        e = jnp.exp(s - lax.stop_gradient(jnp.max(s, axis=-1, keepdims=True)))
        outs.append(mm(e / jnp.sum(e, axis=-1, keepdims=True), kvs[4 + j]))
    return jnp.concatenate([ymix] + outs, axis=1) * jax.nn.silu(z)


def _kv_blocks(kv_ref):
    return [kv_ref[:, j * HD:(j + 1) * HD] for j in range(8)]


def _gate_outproj_fwd(ymix, pg, kv, wo, x, nw, target, name):
    T = x.shape[0]
    last = target is not None

    def body(ymix_ref, pg_ref, kv_ref, wo_ref, x_ref, nw_ref, *refs):
        ycat = _attn_gate(ymix_ref[...], pg_ref[:, :D_XA], pg_ref[:, D_XA:], *_kv_blocks(kv_ref)).astype(BF16)
        o = jnp.dot(ycat, wo_ref[...], preferred_element_type=F32)
        y = x_ref[...] + _rms(o, nw_ref[...])
        if not last:
            ycat_ref, o_ref, y_ref = refs
            y_ref[...] = y
        else:
            t_ref, ycat_ref, o_ref, loss_ref, dy_ref = refs
            err = y - t_ref[...]
            dy_ref[...] = err * (1.0 / D)

            @pl.when(pl.program_id(0) == 0)
            def _():
                loss_ref[...] = jnp.zeros_like(loss_ref)
            part = jnp.sum(jnp.sum(err * err, axis=1, keepdims=True), axis=0, keepdims=True) * (0.5 / D)
            loss_ref[...] += jnp.broadcast_to(part, loss_ref.shape)
        ycat_ref[...] = ycat
        o_ref[...] = o

    in_specs = [_row_spec(D), _row_spec(D_XA + D_CAT), _full_spec((N_MEM, 2 * D_XA)), _full_spec((D_CAT, D)),
                _row_spec(D), _full_spec((1, D))]
    out_specs = [_row_spec(D_CAT), _row_spec(D)]
    out_shape = [SDS((T, D_CAT), BF16), SDS((T, D), F32)]
    if last:
        in_specs.append(_row_spec(D))
        out_specs += [_full_spec((8, 128)), _row_spec(D)]
        out_shape += [SDS((8, 128), F32), SDS((T, D), F32)]
    else:
        out_specs.append(_row_spec(D))
        out_shape.append(SDS((T, D), F32))
    args = (ymix, pg, kv, wo, x, nw) + ((target,) if last else ())
    return pl.pallas_call(body, name=name, grid=(T // TT,), in_specs=in_specs, out_specs=out_specs,
                          out_shape=out_shape, compiler_params=_params(1, VMEM_LIMIT))(*args)


def _outproj_gate_bwd(dxo, o, nw, wo, ymix, pg, kv, dkv_in, name):
    T = dxo.shape[0]
    G = D_XA + D_CAT

    def body(dxo_ref, o_ref, nw_ref, wo_ref, ymix_ref, pg_ref, kv_ref, dkvin_ref,
             dobf_ref, dnw_ref, dymix_ref, dpg_ref, dkv_ref):
        _, vjp = jax.vjp(_rms, o_ref[...], nw_ref[...])
        do, dnw = vjp(dxo_ref[...])
        dobf = do.astype(BF16)
        dobf_ref[...] = dobf
        dycat = _dot_nt(dobf, wo_ref[...])
        _, vjp = jax.vjp(_attn_gate, ymix_ref[...], pg_ref[:, :D_XA], pg_ref[:, D_XA:], *_kv_blocks(kv_ref))
        g = vjp(dycat)
        dymix_ref[...] = g[0]
        dpg_ref[:, :D_XA] = g[1].astype(BF16)
        dpg_ref[:, D_XA:] = g[2].astype(BF16)

        @pl.when(pl.program_id(0) == 0)
        def _():
            dnw_ref[...] = jnp.zeros_like(dnw_ref)
            dkv_ref[...] = dkvin_ref[...]
        dnw_ref[...] += dnw
        for j in range(8):
            dkv_ref[:, j * HD:(j + 1) * HD] += g[3 + j]

    return pl.pallas_call(
        body, name=name, grid=(T // TT,),
        in_specs=[_row_spec(D), _row_spec(D), _full_spec((1, D)), _full_spec((D_CAT, D)), _row_spec(D), _row_spec(G),
                  _full_spec((N_MEM, 2 * D_XA)), _full_spec((N_MEM, 2 * D_XA))],
        out_specs=[_row_spec(D), _full_spec((1, D)), _row_spec(D), _row_spec(G), _full_spec((N_MEM, 2 * D_XA))],
        out_shape=[SDS((T, D), BF16), SDS((1, D), F32), SDS((T, D), F32), SDS((T, G), BF16),
                   SDS((N_MEM, 2 * D_XA), F32)],
        compiler_params=_params(1, VMEM_LIMIT))(dxo, o, nw, wo, ymix, pg, kv, dkv_in)


def _gmlp_pre(u, v, lnw, lnb):
    vg = jax.nn.gelu(v)
    xc = vg - jnp.mean(vg, axis=-1, keepdims=True)
    vl = xc * lax.rsqrt(jnp.mean(xc * xc, axis=-1, keepdims=True) + EPS) * lnw + lnb
    return jax.nn.gelu(u), vl


def _tril(n, strict=False):
    r = lax.broadcasted_iota(jnp.int32, (n, n), 0)
    c = lax.broadcasted_iota(jnp.int32, (n, n), 1)
    return (r > c) if strict else (r >= c)


def _gmlp_fwd(pm, lnw, lnb, ws, bs3, name):
    T = pm.shape[0]

    def body(pm_ref, lnw_ref, lnb_ref, ws_ref, bs_ref, y_ref):
        ug, vl = _gmlp_pre(pm_ref[:, :D], pm_ref[:, D:], lnw_ref[...], lnb_ref[...])
        mask = _tril(HD)
        for g in range(8):
            w = jnp.where(mask, ws_ref[g], 0.0)
            for c in range(TT // HD):
                rows, cols = slice(c * HD, (c + 1) * HD), slice(g * HD, (g + 1) * HD)
                y_ref[rows, cols] = ug[rows, cols] * (_dot_nn(w, vl[rows, cols]) + bs_ref[g])

    return pl.pallas_call(
        body, name=name, grid=(T // TT,),
        in_specs=[_row_spec(2 * D), _full_spec((1, D)), _full_spec((1, D)), _full_spec((8, HD, HD)),
                  _full_spec((8, HD, HD))],
        out_specs=_row_spec(D), out_shape=SDS((T, D), F32),
        compiler_params=_params(1, VMEM_LIMIT))(pm, lnw, lnb, ws, bs3)


def _gmlp_bwd(dy, pm, lnw, lnb, ws, bs3, name):
    T = pm.shape[0]
    n_t = T // TT

    def body(dy_ref, pm_ref, lnw_ref, lnb_ref, ws_ref, bs_ref, dpm_ref, dlnw_ref, dlnb_ref, dws_ref, dbs_ref,
             dug_scr, dvl_scr, dbs_scr):
        i = pl.program_id(0)

        @pl.when(i == 0)
        def _():
            dlnw_ref[...] = jnp.zeros_like(dlnw_ref)
            dlnb_ref[...] = jnp.zeros_like(dlnb_ref)
            dws_ref[...] = jnp.zeros_like(dws_ref)
            dbs_scr[...] = jnp.zeros_like(dbs_scr)

        (ug, vl), vjp = jax.vjp(_gmlp_pre, pm_ref[:, :D], pm_ref[:, D:], lnw_ref[...], lnb_ref[...])
        mask = _tril(HD)
        for g in range(8):
            w = jnp.where(mask, ws_ref[g], 0.0)
            dw = jnp.zeros((HD, HD), F32)
            db = jnp.zeros((HD, HD), F32)
            for c in range(TT // HD):
                rows, cols = slice(c * HD, (c + 1) * HD), slice(g * HD, (g + 1) * HD)
                dyb, vlb = dy_ref[rows, cols], vl[rows, cols]
                sp = _dot_nn(w, vlb) + bs_ref[g]
                dsp = dyb * ug[rows, cols]
                dug_scr[rows, cols] = dyb * sp
                dvl_scr[rows, cols] = _dot_tn(w, dsp)
                dw += _dot_nt(dsp, vlb)
                db += dsp
            dws_ref[g] += jnp.where(mask, dw, 0.0)
            dbs_scr[g] += db
        du, dv, dlnw, dlnb = vjp((dug_scr[...], dvl_scr[...]))
        dpm_ref[:, :D] = du.astype(BF16)
        dpm_ref[:, D:] = dv.astype(BF16)
        dlnw_ref[...] += dlnw
        dlnb_ref[...] += dlnb

        @pl.when(i == n_t - 1)
        def _():
            for g in range(8):
                dbs_ref[g] = jnp.broadcast_to(jnp.sum(dbs_scr[g], axis=1, keepdims=True), (HD, HD))

    return pl.pallas_call(
        body, name=name, grid=(n_t,),
        in_specs=[_row_spec(D), _row_spec(2 * D), _full_spec((1, D)), _full_spec((1, D)), _full_spec((8, HD, HD)),
                  _full_spec((8, HD, HD))],
        out_specs=[_row_spec(2 * D), _full_spec((1, D)), _full_spec((1, D)), _full_spec((8, HD, HD)),
                   _full_spec((8, HD, HD))],
        out_shape=[SDS((T, 2 * D), BF16), SDS((1, D), F32), SDS((1, D), F32), SDS((8, HD, HD), F32),
                   SDS((8, HD, HD), F32)],
        scratch_shapes=[pltpu.VMEM((TT, D), F32), pltpu.VMEM((TT, D), F32), pltpu.VMEM((8, HD, HD), F32)],
        compiler_params=_params(1, VMEM_LIMIT))(dy, pm, lnw, lnb, ws, bs3)


def _prev_spec(width, T):
    return pl.BlockSpec((HALO, width), lambda i: (jnp.maximum(i * (TT // HALO) - 1, 0), 0))


def _next_spec(width, T):
    return pl.BlockSpec((HALO, width), lambda i: (jnp.minimum((i + 1) * (TT // HALO), T // HALO - 1), 0))


def _rows_before(ext, j):
    return ext[HALO:] if j == 0 else pltpu.roll(ext, j, 0)[HALO:]


def _rows_after(ext, j):
    n = ext.shape[0]
    return ext[:n - HALO] if j == 0 else pltpu.roll(ext, n - j, 0)[:n - HALO]


def _conv_apply(ext_s, w):
    K = w.shape[0]
    y = _rows_before(ext_s, K - 1) * w[0:1]
    for k in range(1, K):
        y = y + _rows_before(ext_s, K - 1 - k) * w[k:k + 1]
    return y


def _conv_grads(ext_s, ext_dy, w):
    K = w.shape[0]
    dy = ext_dy[:ext_dy.shape[0] - HALO]
    ds = _rows_after(ext_dy, K - 1) * w[0:1]
    dws = [jnp.sum(dy * _rows_before(ext_s, K - 1), axis=0, keepdims=True)]
    for k in range(1, K):
        ds = ds + _rows_after(ext_dy, K - 1 - k) * w[k:k + 1]
        dws.append(jnp.sum(dy * _rows_before(ext_s, K - 1 - k), axis=0, keepdims=True))
    return ds, jnp.concatenate(dws, axis=0)


def _sconv_fwd(pm, w, name):
    T = pm.shape[0]

    def body(pm_ref, prev_ref, w_ref, y_ref):
        s = pm_ref[:, D:2 * D] * pm_ref[:, 2 * D:]
        sp = jnp.where(pl.program_id(0) > 0, prev_ref[:, D:2 * D] * prev_ref[:, 2 * D:], 0.0)
        y_ref[...] = pm_ref[:, :D] * _conv_apply(jnp.concatenate([sp, s], axis=0), w_ref[...])

    return pl.pallas_call(
        body, name=name, grid=(T // TT,),
        in_specs=[_row_spec(3 * D), _prev_spec(3 * D, T), _full_spec((3, D))],
        out_specs=_row_spec(D), out_shape=SDS((T, D), F32),
        compiler_params=_params(1, VMEM_LIMIT))(pm, pm, w)


def _sconv_bwd(dy, pm, w, name):
    T = pm.shape[0]
    n_t = T // TT

    def body(dy_ref, dyn_ref, pm_ref, prev_ref, next_ref, w_ref, dpm_ref, dw_ref):
        i = pl.program_id(0)
        bg, cg, hv = pm_ref[:, :D], pm_ref[:, D:2 * D], pm_ref[:, 2 * D:]
        sp = jnp.where(i > 0, prev_ref[:, D:2 * D] * prev_ref[:, 2 * D:], 0.0)
        ext_s = jnp.concatenate([sp, cg * hv], axis=0)
        dyv = dy_ref[...]
        dcn = jnp.where(i < n_t - 1, dyn_ref[...] * next_ref[:, :D], 0.0)
        ds, dw = _conv_grads(ext_s, jnp.concatenate([dyv * bg, dcn], axis=0), w_ref[...])
        dpm_ref[:, :D] = (dyv * _conv_apply(ext_s, w_ref[...])).astype(BF16)
        dpm_ref[:, D:2 * D] = (ds * hv).astype(BF16)
        dpm_ref[:, 2 * D:] = (ds * cg).astype(BF16)

        @pl.when(i == 0)
        def _():
            dw_ref[...] = jnp.zeros_like(dw_ref)
        dw_ref[...] += dw

    return pl.pallas_call(
        body, name=name, grid=(n_t,),
        in_specs=[_row_spec(D), _next_spec(D, T), _row_spec(3 * D), _prev_spec(3 * D, T), _next_spec(3 * D, T),
                  _full_spec((3, D))],
        out_specs=[_row_spec(3 * D), _full_spec((3, D))],
        out_shape=[SDS((T, 3 * D), BF16), SDS((3, D), F32)],
        compiler_params=_params(1, VMEM_LIMIT))(dy, dy, pm, pm, pm, w)


def _dnconv_fwd(pm, w, name):
    T = pm.shape[0]

    def body(pm_ref, prev_ref, w_ref, c_ref):
        sp = jnp.where(pl.program_id(0) > 0, prev_ref[...], 0.0)
        c_ref[...] = _conv_apply(jnp.concatenate([sp, pm_ref[...]], axis=0), w_ref[...])

    return pl.pallas_call(
        body, name=name, grid=(T // TT,),
        in_specs=[_row_spec(3 * D), _prev_spec(3 * D, T), _full_spec((4, 3 * D))],
        out_specs=_row_spec(3 * D), out_shape=SDS((T, 3 * D), F32),
        compiler_params=_params(1, VMEM_LIMIT))(pm, pm, w)


def _dnconv_bwd(dcq, dck, dcv, dab, pm, w, name):
    T = pm.shape[0]
    n_t = T // TT

    def body(dq_ref, dk_ref, dv_ref, dqn_ref, dkn_ref, dvn_ref, dab_ref, pm_ref, prev_ref, w_ref, dpm_ref, dw_ref):
        i = pl.program_id(0)
        sp = jnp.where(i > 0, prev_ref[...], 0.0)
        ext_s = jnp.concatenate([sp, pm_ref[...]], axis=0)
        own = jnp.concatenate([dq_ref[...], dk_ref[...], dv_ref[...]], axis=1)
        nxt = jnp.where(i < n_t - 1, jnp.concatenate([dqn_ref[...], dkn_ref[...], dvn_ref[...]], axis=1), 0.0)
        ds, dw = _conv_grads(ext_s, jnp.concatenate([own, nxt], axis=0), w_ref[...])
        dpm_ref[:, :3 * D] = ds.astype(BF16)
        dpm_ref[:, 3 * D:] = dab_ref[...].astype(BF16)

        @pl.when(i == 0)
        def _():
            dw_ref[...] = jnp.zeros_like(dw_ref)
        dw_ref[...] += dw

    return pl.pallas_call(
        body, name=name, grid=(n_t,),
        in_specs=[_row_spec(D), _row_spec(D), _row_spec(D), _next_spec(D, T), _next_spec(D, T), _next_spec(D, T),
                  _row_spec(AB_PAD), _row_spec(3 * D), _prev_spec(3 * D, T), _full_spec((4, 3 * D))],
        out_specs=[_row_spec(3 * D + AB_PAD), _full_spec((4, 3 * D))],
        out_shape=[SDS((T, 3 * D + AB_PAD), BF16), SDS((4, 3 * D), F32)],
        compiler_params=_params(1, VMEM_LIMIT))(dcq, dck, dcv, dcq, dck, dcv, dab, pm, pm, w)


def _l2n(x):
    return x * lax.rsqrt(jnp.sum(x * x, axis=-1, keepdims=True) + EPS)


def _softplus(x):
    return jnp.maximum(x, 0.0) + jnp.log1p(jnp.exp(-jnp.abs(x)))


_BNN = (((2,), (1,)), ((0,), (0,)))
_BNT = (((2,), (2,)), ((0,), (0,)))
_BTN = (((1,), (1,)), ((0,), (0,)))


def _bdot(a, b, dims):
    return lax.dot_general(a.astype(BF16), b.astype(BF16), dims, preferred_element_type=F32)


def _bdot3(a, b, dims):
    ah, bh = a.astype(BF16), b.astype(BF16)
    al, bl = (a - ah.astype(F32)).astype(BF16), (b - bh.astype(F32)).astype(BF16)
    d = functools.partial(lax.dot_general, dimension_numbers=dims, preferred_element_type=F32)
    return d(ah, bh) + (d(ah, bl) + d(al, bh))


def _bdot_hi(a, b, dims):
    return lax.dot_general(a, b, dims, precision=HI, preferred_element_type=F32)


def _batched_matmuls(dot):
    @jax.custom_vjp
    def nn(a, b):
        return dot(a, b, _BNN)

    @jax.custom_vjp
    def nt(a, b):
        return dot(a, b, _BNT)

    @jax.custom_vjp
    def tn(a, b):
        return dot(a, b, _BTN)

    nn.defvjp(lambda a, b: (dot(a, b, _BNN), (a, b)), lambda r, g: (dot(g, r[1], _BNT), dot(r[0], g, _BTN)))
    nt.defvjp(lambda a, b: (dot(a, b, _BNT), (a, b)), lambda r, g: (dot(g, r[1], _BNN), dot(g, r[0], _BTN)))
    tn.defvjp(lambda a, b: (dot(a, b, _BTN), (a, b)), lambda r, g: (dot(r[1], g, _BNT), dot(r[0], g, _BNN)))
    return nn, nt, tn


bmm, bmm_nt, bmm_tn = _batched_matmuls(_bdot)
bmm_hi, _, _ = _batched_matmuls(_bdot_hi)

@jax.custom_vjp
def _neumann_inverse(n):
    C = n.shape[1]
    eye = lax.broadcasted_iota(jnp.int32, n.shape, 1) == lax.broadcasted_iota(jnp.int32, n.shape, 2)
    t = eye.astype(F32) + n
    for _ in range(5):
        n = _bdot3(n, n, _BNN)
        t = t + _bdot3(t, n, _BNN)
    return t


def _neumann_inverse_fwd(n):
    t = _neumann_inverse(n)
    return t, t


def _neumann_inverse_bwd(t, g):
    return (_bdot3(_bdot3(t, g, _BTN), t, _BNT),)


_neumann_inverse.defvjp(_neumann_inverse_fwd, _neumann_inverse_bwd)


@jax.custom_vjp
def _saved_inverse(n, t):
    return t


_saved_inverse.defvjp(lambda n, t: (t, t), lambda t, g: (_bdot3(_bdot3(t, g, _BTN), t, _BNT), jnp.zeros_like(t)))

DN_NCH = DN_TB // DN_C
DN_NH = 4


def _decay_terms(ab, alog, dtb, first_head, n_heads):
    C = DN_C
    lane = lax.broadcasted_iota(jnp.int32, ab.shape, 1)
    g_all = (-jnp.exp(alog) * _softplus(ab + dtb)).reshape(DN_NCH, C, HD)
    beta_all = jax.nn.sigmoid(ab)
    r = lax.broadcasted_iota(jnp.int32, (DN_NCH, C, C), 1)
    c = lax.broadcasted_iota(jnp.int32, (DN_NCH, C, C), 2)
    gc_all = bmm_hi((r >= c).astype(F32), g_all)
    gc_rows = [gc_all[i].T for i in range(DN_NCH)]
    lane3 = lax.broadcasted_iota(jnp.int32, (DN_NCH, C, HD), 2)
    row = lax.broadcasted_iota(jnp.int32, (HD, C), 0)
    ones = jnp.ones((1, HD), F32)
    gcs, gjs, betas = [], [], []
    for i in range(n_heads):
        h = first_head + i
        gcs.append(jnp.sum(jnp.where(lane3 == h, gc_all, 0.0), axis=2, keepdims=True) * ones)
        gjs.append(jnp.concatenate(
            [jnp.broadcast_to(jnp.sum(jnp.where(row == h, t, 0.0), axis=0, keepdims=True), (C, C))[None] for t in gc_rows],
            axis=0))
        beta = jnp.sum(jnp.where(lane == 8 + h, beta_all, 0.0), axis=1, keepdims=True) * ones
        betas.append(beta.reshape(DN_NCH, C, HD))
    return jnp.concatenate(gcs, axis=0), jnp.concatenate(gjs, axis=0), jnp.concatenate(betas, axis=0)


def _dn_prep(cq, ck, cv, gcum, gj, bb, t_saved=None):
    B, C = cq.shape[0], DN_C
    q = _l2n(jax.nn.silu(cq)) * (HD ** -0.5)
    k = _l2n(jax.nn.silu(ck))
    v = jax.nn.silu(cv)
    r = lax.broadcasted_iota(jnp.int32, (B, C, C), 1)
    c = lax.broadcasted_iota(jnp.int32, (B, C, C), 2)
    incl, strict = r >= c, r > c
    decay = jnp.where(incl, jnp.exp(jnp.where(incl, gcum[:, :, :C] - gj, 0.0)), 0.0)
    kb = k * bb
    n_mat = -jnp.where(strict, bmm_nt(kb, k) * decay, 0.0)
    t_mat = _neumann_inverse(n_mat) if t_saved is None else _saved_inverse(n_mat, t_saved)
    eg = jnp.exp(gcum)
    glast = gcum[:, C - 1:C, :]
    return (bmm(t_mat, v * bb), bmm(t_mat, kb * eg), bmm_nt(q, k) * decay, q * eg, k * jnp.exp(glast - gcum),
            jnp.exp(glast), t_mat)


def _dn_scan_step(u, w, qk, qd, kd, egl, S, onw):
    v_new = u - bmm(w, S)
    o = bmm(qd, S) + bmm(qk, v_new)
    return _rms(o, onw), S * egl + bmm_tn(kd, v_new)


def _to_batch(ref, n_heads):
    return jnp.concatenate([ref[:, i * HD:(i + 1) * HD].astype(F32).reshape(DN_NCH, DN_C, HD) for i in range(n_heads)],
                           axis=0)


def _from_batch(ref, val, n_heads):
    for i in range(n_heads):
        ref[:, i * HD:(i + 1) * HD] = val[i * DN_NCH:(i + 1) * DN_NCH].reshape(DN_TB, HD).astype(ref.dtype)


def _prep_specs(T, rev):
    nb = T // DN_TB
    blk = (lambda n: nb - 1 - n) if rev else (lambda n: n)
    ng = 8 // DN_NH
    head = [pl.BlockSpec((DN_TB, DN_NH * HD), functools.partial(lambda n, h, off: (blk(n), off + h), off=ng * s))
            for s in range(3)]
    ab = pl.BlockSpec((DN_TB, AB_PAD), lambda n, h: (blk(n), 3 * D // AB_PAD))
    row = pl.BlockSpec((1, HD), lambda n, h: (0, 0))
    wide = pl.BlockSpec((DN_TB, DN_NH * HD), lambda n, h: (blk(n), h))
    qk = pl.BlockSpec((DN_NCH, DN_NH, DN_C, DN_C), lambda n, h: (blk(n), h, 0, 0))
    eg = pl.BlockSpec((DN_NCH, DN_NH, 1, HD), lambda n, h: (blk(n), h, 0, 0))
    return nb, ng, head, ab, row, wide, qk, eg


def _dn_prep_fwd(cpre, pm, alog, dtb, name):
    T = cpre.shape[0]
    nb, ng, head, ab, row, wide, qks, egs = _prep_specs(T, False)

    def body(cq_ref, ck_ref, cv_ref, ab_ref, alog_ref, dtb_ref, u_ref, w_ref, qk_ref, qd_ref, kd_ref, e_ref, t_ref):
        gcum, gj, bb = _decay_terms(ab_ref[...], alog_ref[...], dtb_ref[...], pl.program_id(1) * DN_NH, DN_NH)
        u, w, qk, qd, kd, egl, t_mat = _dn_prep(_to_batch(cq_ref, DN_NH), _to_batch(ck_ref, DN_NH),
                                                _to_batch(cv_ref, DN_NH), gcum, gj, bb)
        _from_batch(u_ref, u, DN_NH)
        _from_batch(w_ref, w, DN_NH)
        _from_batch(qd_ref, qd, DN_NH)
        _from_batch(kd_ref, kd, DN_NH)
        for i in range(DN_NH):
            qk_ref[:, i] = qk[i * DN_NCH:(i + 1) * DN_NCH].astype(BF16)
            e_ref[:, i] = egl[i * DN_NCH:(i + 1) * DN_NCH]
            t_ref[:, i] = t_mat[i * DN_NCH:(i + 1) * DN_NCH]

    return pl.pallas_call(
        body, name=name, grid=(nb, ng), in_specs=head + [ab, row, row],
        out_specs=[wide, wide, qks, wide, wide, egs, qks],
        out_shape=[SDS((T, D), F32), SDS((T, D), BF16), SDS((T // DN_C, 8, DN_C, DN_C), BF16), SDS((T, D), BF16),
                   SDS((T, D), BF16), SDS((T // DN_C, 8, 1, HD), F32), SDS((T // DN_C, 8, DN_C, DN_C), F32)],
        compiler_params=_params(2, VMEM_LIMIT))(cpre, cpre, cpre, pm, alog, dtb)


def _dn_prep_bwd(du, dw, dqk, dqd, dkd, degl, t_mat, cpre, pm, alog, dtb, name):
    T = cpre.shape[0]
    nb, ng, head, ab, row, wide, qks, egs = _prep_specs(T, True)

    def body(du_ref, dw_ref, dqk_ref, dqd_ref, dkd_ref, de_ref, t_ref, cq_ref, ck_ref, cv_ref, ab_ref, alog_ref,
             dtb_ref, dcq_ref, dck_ref, dcv_ref, dab_ref, dalog_ref, ddtb_ref):
        n, h = pl.program_id(0), pl.program_id(1)

        @pl.when((n == 0) & (h == 0))
        def _():
            dalog_ref[...] = jnp.zeros_like(dalog_ref)
            ddtb_ref[...] = jnp.zeros_like(ddtb_ref)

        @pl.when(h == 0)
        def _():
            dab_ref[...] = jnp.zeros_like(dab_ref)

        t_saved = jnp.concatenate([t_ref[:, i] for i in range(DN_NH)], axis=0)

        def fwd(cq, ck, cv, ab_v, alog_v, dtb_v):
            gcum, gj, bb = _decay_terms(ab_v, alog_v, dtb_v, h * DN_NH, DN_NH)
            return _dn_prep(cq, ck, cv, gcum, gj, bb, t_saved)[:6]

        _, vjp = jax.vjp(fwd, _to_batch(cq_ref, DN_NH), _to_batch(ck_ref, DN_NH), _to_batch(cv_ref, DN_NH), ab_ref[...],
                         alog_ref[...], dtb_ref[...])
        cot = (_to_batch(du_ref, DN_NH), _to_batch(dw_ref, DN_NH),
               jnp.concatenate([dqk_ref[:, i] for i in range(DN_NH)], axis=0), _to_batch(dqd_ref, DN_NH),
               _to_batch(dkd_ref, DN_NH), jnp.concatenate([de_ref[:, i] for i in range(DN_NH)], axis=0))
        dcq, dck, dcv, dab, dalog, ddtb = vjp(cot)
        _from_batch(dcq_ref, dcq, DN_NH)
        _from_batch(dck_ref, dck, DN_NH)
        _from_batch(dcv_ref, dcv, DN_NH)
        dab_ref[...] += dab
        dalog_ref[...] += dalog
        ddtb_ref[...] += ddtb

    dabspec = pl.BlockSpec((DN_TB, AB_PAD), lambda n, h: (nb - 1 - n, 0))
    return pl.pallas_call(
        body, name=name, grid=(nb, ng),
        in_specs=[wide, wide, qks, wide, wide, egs, qks] + head + [ab, row, row],
        out_specs=[wide, wide, wide, dabspec, row, row],
        out_shape=[SDS((T, D), F32)] * 3 + [SDS((T, AB_PAD), F32)] + [SDS((1, HD), F32)] * 2,
        compiler_params=_params(2, VMEM_LIMIT))(du, dw, dqk, dqd, dkd, degl, t_mat, cpre, cpre, cpre, pm, alog, dtb)


def _scan_specs(T, rev):
    nb = T // DN_TB
    blk = (lambda n: nb - 1 - n) if rev else (lambda n: n)
    wide = pl.BlockSpec((DN_TB, D), lambda n: (blk(n), 0))
    qk = pl.BlockSpec((DN_NCH, 8, DN_C, DN_C), lambda n: (blk(n), 0, 0, 0))
    eg = pl.BlockSpec((DN_NCH, 8, 1, HD), lambda n: (blk(n), 0, 0, 0))
    st = pl.BlockSpec((DN_NCH, 8, HD, HD), lambda n: (blk(n), 0, 0, 0))
    row = pl.BlockSpec((1, HD), lambda n: (0, 0))
    return nb, wide, qk, eg, st, row


def _heads_of(ref, rows):
    return jnp.concatenate([ref[rows, h * HD:(h + 1) * HD].astype(F32)[None] for h in range(8)], axis=0)


def _dn_scan_fwd(u, w, qk, qd, kd, egl, onw, name):
    T = u.shape[0]
    nb, wide, qks, egs, sts, row = _scan_specs(T, False)

    def body(u_ref, w_ref, qk_ref, qd_ref, kd_ref, e_ref, onw_ref, o_ref, st_ref, s_scr):
        @pl.when(pl.program_id(0) == 0)
        def _():
            s_scr[...] = jnp.zeros_like(s_scr)
        S = s_scr[...]
        for c in range(DN_NCH):
            rows = slice(c * DN_C, (c + 1) * DN_C)
            st_ref[c] = S
            o, S = _dn_scan_step(_heads_of(u_ref, rows), _heads_of(w_ref, rows), qk_ref[c].astype(F32),
                                 _heads_of(qd_ref, rows), _heads_of(kd_ref, rows), e_ref[c], S, onw_ref[...])
            for h in range(8):
                o_ref[rows, h * HD:(h + 1) * HD] = o[h]
        s_scr[...] = S

    return pl.pallas_call(
        body, name=name, grid=(nb,), in_specs=[wide, wide, qks, wide, wide, egs, row], out_specs=[wide, sts],
        out_shape=[SDS((T, D), F32), SDS((T // DN_C, 8, HD, HD), F32)],
        scratch_shapes=[pltpu.VMEM((8, HD, HD), F32)],
        compiler_params=_params(1, VMEM_LIMIT))(u, w, qk, qd, kd, egl, onw)


def _dn_scan_bwd(do, u, w, qk, qd, kd, egl, st, onw, name):
    T = u.shape[0]
    nb, wide, qks, egs, sts, row = _scan_specs(T, True)

    def body(do_ref, u_ref, w_ref, qk_ref, qd_ref, kd_ref, e_ref, st_ref, onw_ref,
             du_ref, dw_ref, dqk_ref, dqd_ref, dkd_ref, de_ref, donw_ref, ds_scr):
        @pl.when(pl.program_id(0) == 0)
        def _():
            ds_scr[...] = jnp.zeros_like(ds_scr)
            donw_ref[...] = jnp.zeros_like(donw_ref)
        dS = ds_scr[...]
        donw = jnp.zeros((1, HD), F32)
        for c in reversed(range(DN_NCH)):
            rows = slice(c * DN_C, (c + 1) * DN_C)
            _, vjp = jax.vjp(_dn_scan_step, _heads_of(u_ref, rows), _heads_of(w_ref, rows), qk_ref[c].astype(F32),
                             _heads_of(qd_ref, rows), _heads_of(kd_ref, rows), e_ref[c], st_ref[c], onw_ref[...])
            du, dw, dqk, dqd, dkd, de, dS, dn = vjp((_heads_of(do_ref, rows), dS))
            for h in range(8):
                cols = slice(h * HD, (h + 1) * HD)
                du_ref[rows, cols] = du[h]
                dw_ref[rows, cols] = dw[h]
                dqd_ref[rows, cols] = dqd[h]
                dkd_ref[rows, cols] = dkd[h]
            dqk_ref[c] = dqk
            de_ref[c] = de
            donw += dn
        ds_scr[...] = dS
        donw_ref[...] += donw

    return pl.pallas_call(
        body, name=name, grid=(nb,), in_specs=[wide, wide, wide, qks, wide, wide, egs, sts, row],
        out_specs=[wide, wide, qks, wide, wide, egs, row],
        out_shape=[SDS((T, D), F32), SDS((T, D), F32), SDS((T // DN_C, 8, DN_C, DN_C), F32), SDS((T, D), F32),
                   SDS((T, D), F32), SDS((T // DN_C, 8, 1, HD), F32), SDS((1, HD), F32)],
        scratch_shapes=[pltpu.VMEM((8, HD, HD), F32)],
        compiler_params=_params(1, VMEM_LIMIT))(do, u, w, qk, qd, kd, egl, st, onw)


def _adamw(w, g, m, v, name):
    R, C = w.shape
    tr = 256 if R % 256 == 0 and R > 256 else R
    tc = 256 if tr == R and R > 256 and C % 256 == 0 else C
    c1 = 1.0 - ADAM_B1 ** ADAM_STEP
    c2 = 1.0 - ADAM_B2 ** ADAM_STEP

    def body(w_ref, g_ref, m_ref, v_ref, d_ref, nm_ref, nv_ref):
        gv = g_ref[...]
        nm = ADAM_B1 * m_ref[...] + (1.0 - ADAM_B1) * gv
        nv = ADAM_B2 * v_ref[...] + (1.0 - ADAM_B2) * (gv * gv)
        nm_ref[...] = nm
        nv_ref[...] = nv
        d_ref[...] = -ADAM_LR * ((nm / c1) / (jnp.sqrt(nv / c2) + ADAM_EPS) + ADAM_WD * w_ref[...])

    spec = pl.BlockSpec((tr, tc), lambda i, j: (i, j))
    return pl.pallas_call(
        body, name=name, grid=(R // tr, C // tc), in_specs=[spec] * 4, out_specs=[spec] * 3,
        out_shape=[SDS((R, C), F32)] * 3, compiler_params=_params(2, VMEM_LIMIT))(w, g, m, v)


def _local_step(x, mem, target, wts, sm):
    kinds = [i % 3 for i in range(DEPTH)]
    mnw = sm["mem_norm_w"].reshape(1, D)
    kv, wkv = None, None
    saved, blocks = [], []
    for i, kind in enumerate(kinds):
        j = i // 3
        npre = sm["norm_pre"][i].reshape(1, D)
        npost = sm["norm_post"][i].reshape(1, D)
        mix, gate, wo_after = wts["blocks"](i, x)
        pm, pg, h = _inproj_fwd(x, npre, mix, gate, kind == 2, f"inproj_fwd_{i}")
        if kv is None:
            wkv = wts["wkv_after"](h)
            kv = _memkv_fwd(mem, mnw, wkv)
        extra = None
        if kind == 0:
            bs3 = jnp.broadcast_to(sm["a_b_s"][j][:, :, None], (8, HD, HD))
            ymix = _gmlp_fwd(pm, sm["a_ln_w"][j].reshape(1, D), sm["a_ln_b"][j].reshape(1, D), sm["a_w_s"][j], bs3,
                             f"gmlp_fwd_{i}")
            extra = bs3
        elif kind == 1:
            ymix = _sconv_fwd(pm, sm["b_conv_w"][j], f"sconv_fwd_{i}")
        else:
            cpre = _dnconv_fwd(pm, sm["c_conv_w"][j], f"dnconv_fwd_{i}")
            alog = jnp.pad(sm["c_a_log"][j], (0, HD - 8)).reshape(1, HD)
            dtb = jnp.pad(sm["c_dt_bias"][j], (0, HD - 8)).reshape(1, HD)
            onw = sm["c_o_norm_w"][j].reshape(1, HD)
            *prep, t_mat = _dn_prep_fwd(cpre, pm, alog, dtb, f"dn_prep_fwd_{i}")
            ymix, st = _dn_scan_fwd(*prep, onw, f"dn_scan_fwd_{i}")
            extra = (cpre, prep, t_mat, st, alog, dtb, onw)
        wo = wo_after(ymix)
        blocks.append((mix, gate, wo))
        layer_in = x
        if i < DEPTH - 1:
            ycat, o, x = _gate_outproj_fwd(ymix, pg, kv, wo, x, npost, None, f"gate_outproj_fwd_{i}")
        else:
            ycat, o, loss, dx = _gate_outproj_fwd(ymix, pg, kv, wo, x, npost, target, f"gate_outproj_fwd_{i}")
        saved.append((layer_in, h, pm, pg, ymix, ycat, o, extra))

    g = {"wm": [None] * DEPTH, "wg": [None] * DEPTH, "wo": [None] * DEPTH, "norm_pre": [None] * DEPTH,
         "norm_post": [None] * DEPTH}
    dkv = jnp.zeros((N_MEM, 2 * D_XA), F32)
    sent = 0.0
    for i in reversed(range(DEPTH)):
        kind, j = kinds[i], i // 3
        xi, h, pm, pg, ymix, ycat, o, extra = saved[i]
        npre = sm["norm_pre"][i].reshape(1, D)
        npost = sm["norm_post"][i].reshape(1, D) + sent
        dobf, g["norm_post"][i], dymix, dpg, dkv = _outproj_gate_bwd(dx, o, npost, blocks[i][2], ymix, pg, kv, dkv,
                                                                     f"outproj_gate_bwd_{i}")
        g["wo"][i] = _matmul_tn(ycat, dobf, f"dwo_{i}")
        if kind == 0:
            dpm, dlnw, dlnb, dws, dbs3 = _gmlp_bwd(dymix, pm, sm["a_ln_w"][j].reshape(1, D),
                                                   sm["a_ln_b"][j].reshape(1, D), sm["a_w_s"][j], extra,
                                                   f"gmlp_bwd_{i}")
            g.setdefault("a_ln_w", {})[j] = dlnw.reshape(D)
            g.setdefault("a_ln_b", {})[j] = dlnb.reshape(D)
            g.setdefault("a_w_s", {})[j] = dws
            g.setdefault("a_b_s", {})[j] = dbs3[:, :, 0]
        elif kind == 1:
            dpm, dcw = _sconv_bwd(dymix, pm, sm["b_conv_w"][j], f"sconv_bwd_{i}")
            g.setdefault("b_conv_w", {})[j] = dcw
        else:
            cpre, prep, t_mat, st, alog, dtb, onw = extra
            *dprep, donw = _dn_scan_bwd(dymix, *prep, st, onw, f"dn_scan_bwd_{i}")
            dcq, dck, dcv, dab, dalog, ddtb = _dn_prep_bwd(*dprep, t_mat, cpre, pm, alog, dtb, f"dn_prep_bwd_{i}")
            dpm, dcw = _dnconv_bwd(dcq, dck, dcv, dab, pm, sm["c_conv_w"][j], f"dnconv_bwd_{i}")
            g.setdefault("c_conv_w", {})[j] = dcw
            g.setdefault("c_a_log", {})[j] = dalog[0, :8]
            g.setdefault("c_dt_bias", {})[j] = ddtb[0, :8]
            g.setdefault("c_o_norm_w", {})[j] = donw[0]
        if kind == 2:
            g["wm"][i] = _matmul_tn(dpm, h, f"dwm_{i}")
            g["wg"][i] = _matmul_tn(dpg, h, f"dwg_{i}")
        else:
            g["wm"][i] = _matmul_tn(h, dpm, f"dwm_{i}", GRAD_TILE[kind])
            g["wg"][i] = _matmul_tn(h, dpg, f"dwg_{i}", GRAD_TILE[kind])
        sent = wts["layer_done"](i, g)
        dx, g["norm_pre"][i] = _inproj_bwd(dpm, dpg, xi, npre + sent, blocks[i][0], blocks[i][1], kind == 2, dx,
                                           f"inproj_bwd_{i}")
    g["mem_norm_w"], g["wkv"] = _memkv_bwd(mem, mnw, wkv, dkv)
    return loss[0, 0], dx, g


ANY = pl.BlockSpec(memory_space=pl.ANY)


def _place():
    return lax.axis_index("x"), lax.axis_index("y"), lax.axis_index("c")


def _add(a, b, name):
    def body(a_ref, b_ref, o_ref):
        o_ref[...] = a_ref[...] + b_ref[...]

    return pl.pallas_call(body, name=name, out_shape=SDS(a.shape, a.dtype), compiler_params=_params(0, VMEM_LIMIT))(a, b)


def _sum4(own, land):
    def body(own_ref, l_ref, o_ref):
        chip = 2 * lax.axis_index("x") + lax.axis_index("y")
        acc = jnp.where(chip == 0, own_ref[...], l_ref[0])
        for s in range(1, 4):
            acc = acc + jnp.where(chip == s, own_ref[...], l_ref[s])
        o_ref[...] = acc

    return pl.pallas_call(body, name="sum_small", out_shape=SDS(own.shape, own.dtype),
                          compiler_params=_params(0, VMEM_LIMIT))(own, land)


C_ROWS = 1312
_REDUCE_CHUNK = {512: 256, 2560: 640}
W_CLASSES = {"in0": (512, 256)}


def _chunk_list(specs):
    return [(k, r, chunk) for k, (half, chunk) in enumerate(specs) for r in range(0, half, chunk)]


def _gather_classes(arrs, specs, vec):
    n = len(arrs)
    chunks = _chunk_list(specs)
    nc = len(chunks)

    def body(*refs):
        ins, vec_ref, outs, ov_ref = refs[:n], refs[n], refs[n + 1:2 * n + 1], refs[2 * n + 1]
        ici_send, ici_recv, d2d_send, d2d_recv, vec_send, vec_recv = refs[2 * n + 2:]
        x, y, c = _place()
        chip = 2 * x + y
        peers = [(1 - x, y), (x, 1 - y), (1 - x, 1 - y)]

        def rows(ci, half):
            k, r, cnt = chunks[ci]
            return k, pl.ds(half * specs[k][0] + r, cnt)

        def over_ici(j, ci, slab):
            px, py = peers[j]
            k, rs = rows(ci, c)
            return pltpu.make_async_remote_copy(
                src_ref=ins[k].at[rs], dst_ref=outs[k].at[slab, rs], send_sem=ici_send.at[j * nc + ci],
                recv_sem=ici_recv.at[j * nc + ci], device_id=(px, py, c), device_id_type=MESH)

        def over_d2d(j, ci, half):
            px, py = peers[j]
            k, rs = rows(ci, half)
            where = outs[k].at[2 * px + py, rs]
            return pltpu.make_async_remote_copy(
                src_ref=where, dst_ref=where, send_sem=d2d_send.at[j * nc + ci], recv_sem=d2d_recv.at[j * nc + ci],
                device_id=(x, y, 1 - c), device_id_type=MESH)

        def small(j, slab):
            px, py = peers[j]
            return pltpu.make_async_remote_copy(
                src_ref=vec_ref, dst_ref=ov_ref.at[slab], send_sem=vec_send.at[j], recv_sem=vec_recv.at[j],
                device_id=(px, py, c), device_id_type=MESH)

        sends = [small(j, chip) for j in range(3)] + [over_ici(j, ci, chip) for ci in range(nc) for j in range(3)]
        for cp in sends:
            cp.start()
        forwards = []
        for ci in range(nc):
            for j, (px, py) in enumerate(peers):
                over_ici(j, ci, 2 * px + py).wait_recv()
                forwards.append(over_d2d(j, ci, c))
                forwards[-1].start()
        for ci in range(nc):
            for j in range(3):
                over_d2d(j, ci, 1 - c).wait_recv()
        for j, (px, py) in enumerate(peers):
            small(j, 2 * px + py).wait_recv()
        for cp in sends + forwards:
            cp.wait_send()

    dma = pltpu.SemaphoreType.DMA
    return pl.pallas_call(
        body, name="gather_weights", in_specs=[ANY] * (n + 1), out_specs=[ANY] * (n + 1),
        out_shape=[SDS((4,) + a.shape, a.dtype) for a in arrs] + [SDS((4,) + vec.shape, vec.dtype)],
        scratch_shapes=[dma((3 * nc,)), dma((3 * nc,)), dma((3 * nc,)), dma((3 * nc,)), dma((3,)), dma((3,))])(*arrs, vec)


def _swap_classes(grads, specs, small):
    n = len(grads)
    chunks = _chunk_list(specs)

    def body(*refs):
        ins, s_ref, outs, os_ref, send_sems, recv_sems = refs[:n], refs[n], refs[n + 1:2 * n + 1], *refs[2 * n + 1:]
        x, y, c = _place()
        copies = []
        for s in range(4):
            for k, r, cnt in chunks:
                copies.append(pltpu.make_async_remote_copy(
                    src_ref=ins[k].at[s, 1 - c, pl.ds(r, cnt)], dst_ref=outs[k].at[s, pl.ds(r, cnt)],
                    send_sem=send_sems.at[len(copies)], recv_sem=recv_sems.at[len(copies)],
                    device_id=(x, y, 1 - c), device_id_type=MESH))
        copies.append(pltpu.make_async_remote_copy(
            src_ref=s_ref, dst_ref=os_ref, send_sem=send_sems.at[len(copies)], recv_sem=recv_sems.at[len(copies)],
            device_id=(x, y, 1 - c), device_id_type=MESH))
        for cp in copies:
            cp.start()
        for cp in copies:
            cp.wait_recv()
        for cp in copies:
            cp.wait_send()

    m = 4 * len(chunks) + 1
    return pl.pallas_call(
        body, name="swap_halves", in_specs=[ANY] * (n + 1), out_specs=[ANY] * (n + 1),
        out_shape=[SDS((4, g.shape[2], g.shape[3]), F32) for g in grads] + [SDS(small.shape, F32)],
        scratch_shapes=[pltpu.SemaphoreType.DMA((m,)), pltpu.SemaphoreType.DMA((m,))])(*grads, small)


def _pair_sum_class(g, other, chunk, name):
    _, _, half, w = g.shape

    def body(g_ref, o_ref, pb_ref, own_ref):
        x, y, c = _place()
        v = jnp.where(c == 0, g_ref[0], g_ref[1]) + o_ref[...]
        pb_ref[...] = v.astype(BF16)

        @pl.when(pl.program_id(1) == 2 * x + y)
        def _():
            own_ref[...] = v

    return pl.pallas_call(
        body, name=name, grid=(half // chunk, 4),
        in_specs=[pl.BlockSpec((None, 2, chunk, w), lambda i, s: (s, 0, i, 0)),
                  pl.BlockSpec((None, chunk, w), lambda i, s: (s, i, 0))],
        out_specs=[pl.BlockSpec((None, chunk, w), lambda i, s: (s, i, 0)), pl.BlockSpec((chunk, w), lambda i, s: (i, 0))],
        out_shape=[SDS((4, half, w), BF16), SDS((half, w), F32)],
        compiler_params=_params(2, VMEM_LIMIT))(g, other)


def _exchange_classes(pbs, specs, ps):
    n = len(pbs)
    chunks = _chunk_list(specs)
    per_peer = len(chunks) + 1

    def body(*refs):
        ins, ps_ref, outs, ls_ref, send_sems, recv_sems = refs[:n], refs[n], refs[n + 1:2 * n + 1], *refs[2 * n + 1:]
        x, y, c = _place()
        chip = 2 * x + y
        peers = [(1 - x, y), (x, 1 - y), (1 - x, 1 - y)]

        def copies(slab_of):
            out = []
            for j, (px, py) in enumerate(peers):
                for k, r, cnt in chunks:
                    out.append(pltpu.make_async_remote_copy(
                        src_ref=ins[k].at[2 * px + py, pl.ds(r, cnt)], dst_ref=outs[k].at[slab_of(j), pl.ds(r, cnt)],
                        send_sem=send_sems.at[len(out)], recv_sem=recv_sems.at[len(out)], device_id=(px, py, c),
                        device_id_type=MESH))
                out.append(pltpu.make_async_remote_copy(
                    src_ref=ps_ref, dst_ref=ls_ref.at[slab_of(j)], send_sem=send_sems.at[len(out)],
                    recv_sem=recv_sems.at[len(out)], device_id=(px, py, c), device_id_type=MESH))
            return out

        sends = copies(lambda j: chip)
        for cp in sends:
            cp.start()
        for cp in copies(lambda j: 2 * peers[j][0] + peers[j][1]):
            cp.wait_recv()
        for cp in sends:
            cp.wait_send()

    m = 3 * per_peer
    return pl.pallas_call(
        body, name="chip_exchange", in_specs=[ANY] * (n + 1), out_specs=[ANY] * (n + 1),
        out_shape=[SDS(p.shape, BF16) for p in pbs] + [SDS((4,) + ps.shape, F32)],
        scratch_shapes=[pltpu.SemaphoreType.DMA((m,)), pltpu.SemaphoreType.DMA((m,))])(*pbs, ps)


def _chip_sum_class(own, land, chunk, name):
    half, w = own.shape

    def body(own_ref, l_ref, o_ref):
        chip = 2 * lax.axis_index("x") + lax.axis_index("y")
        acc = jnp.where(chip == 0, own_ref[...], l_ref[0].astype(F32))
        for s in range(1, 4):
            acc = acc + jnp.where(chip == s, own_ref[...], l_ref[s].astype(F32))
        o_ref[...] = acc

    return pl.pallas_call(
        body, name=name, grid=(half // chunk,),
        in_specs=[pl.BlockSpec((chunk, w), lambda i: (i, 0)), pl.BlockSpec((4, chunk, w), lambda i: (0, i, 0))],
        out_specs=pl.BlockSpec((chunk, w), lambda i: (i, 0)), out_shape=SDS((half, w), F32),
        compiler_params=_params(1, VMEM_LIMIT))(own, land)


def _share_classes(rs, specs):
    n = len(rs)
    chunks = _chunk_list(specs)

    def body(*refs):
        ins, outs, send_sems, recv_sems = refs[:n], refs[n:2 * n], *refs[2 * n:]
        x, y, c = _place()
        copies = [pltpu.make_async_remote_copy(
            src_ref=ins[k].at[pl.ds(r, cnt)], dst_ref=outs[k].at[pl.ds(r, cnt)], send_sem=send_sems.at[i],
            recv_sem=recv_sems.at[i], device_id=(x, y, 1 - c), device_id_type=MESH)
            for i, (k, r, cnt) in enumerate(chunks)]
        for cp in copies:
            cp.start()
        for cp in copies:
            cp.wait_recv()
        for cp in copies:
            cp.wait_send()

    m = len(chunks)
    return pl.pallas_call(
        body, name="share_half", in_specs=[ANY] * n, out_specs=[ANY] * n, out_shape=[SDS(r.shape, F32) for r in rs],
        scratch_shapes=[pltpu.SemaphoreType.DMA((m,)), pltpu.SemaphoreType.DMA((m,))])(*rs)


_HBM = pl.BlockSpec(memory_space=pltpu.HBM)
_SEM = pl.BlockSpec(memory_space=pltpu.SEMAPHORE)
_EFFECT = pltpu.SideEffectType.DATAFLOW_SIDE_EFFECTING


def _chip_peers():
    x, y, c = _place()
    return [(1 - x, y, c), (x, 1 - y, c), (1 - x, 1 - y, c)]


def _send_shard_start(v, name):
    def body(v_ref, land_ref, send_sems, recv_sems, v_thru, land_thru, token):
        x, y, c = _place()
        for j, peer in enumerate(_chip_peers()):
            pltpu.make_async_remote_copy(src_ref=v_ref, dst_ref=land_ref.at[2 * x + y], send_sem=send_sems.at[j],
                                         recv_sem=recv_sems.at[j], device_id=peer, device_id_type=MESH).start()
        token[...] = jnp.zeros_like(token)

    land_shape = (4,) + v.shape
    return pl.pallas_call(
        body, name=name,
        out_shape=(pltpu.SemaphoreType.DMA((3,)), pltpu.SemaphoreType.DMA((3,)), pltpu.HBM(v.shape, v.dtype),
                   pltpu.HBM(land_shape, v.dtype), SDS((8, 128), F32)),
        in_specs=(_HBM, _HBM), out_specs=(_SEM, _SEM, _HBM, _HBM, pl.BlockSpec(memory_space=pltpu.VMEM)),
        input_output_aliases={0: 2, 1: 3}, compiler_params=pltpu.CompilerParams(has_side_effects=_EFFECT),
    )(pltpu.with_memory_space_constraint(v, pltpu.HBM),
      pltpu.with_memory_space_constraint(lax.empty(land_shape, v.dtype), pltpu.HBM))


def _xor_peer(r):
    x, y, c = _place()
    return (1 - x if (r >> 2) & 1 else x, 1 - y if (r >> 1) & 1 else y, 1 - c if r & 1 else c)


def _send_pieces_start(parts, name):
    n = len(parts)

    def body(*refs):
        ins, lands = refs[:n], refs[n:2 * n]
        send_sems, recv_sems = refs[2 * n:2 * n + 2]
        token = refs[-1]
        x, y, c = _place()
        for r in range(1, 8):
            px, py, pc = _xor_peer(r)
            for k in range(n):
                pltpu.make_async_remote_copy(
                    src_ref=ins[k].at[2 * px + py, pc], dst_ref=lands[k].at[4 * x + 2 * y + c],
                    send_sem=send_sems.at[(r - 1) * n + k], recv_sem=recv_sems.at[(r - 1) * n + k],
                    device_id=(px, py, pc), device_id_type=MESH).start()
        token[...] = jnp.zeros_like(token)

    land_shapes = [(8,) + p.shape[2:] for p in parts]
    hbm = [pltpu.HBM(p.shape, p.dtype) for p in parts] + [pltpu.HBM(s, p.dtype) for s, p in zip(land_shapes, parts)]
    operands = [pltpu.with_memory_space_constraint(p, pltpu.HBM) for p in parts]
    operands += [pltpu.with_memory_space_constraint(lax.empty(s, p.dtype), pltpu.HBM) for s, p in zip(land_shapes, parts)]
    return pl.pallas_call(
        body, name=name,
        out_shape=(pltpu.SemaphoreType.DMA((7 * n,)), pltpu.SemaphoreType.DMA((7 * n,)), *hbm, SDS((8, 128), F32)),
        in_specs=(_HBM,) * (2 * n), out_specs=(_SEM, _SEM) + (_HBM,) * (2 * n) + (pl.BlockSpec(memory_space=pltpu.VMEM),),
        input_output_aliases={i: 2 + i for i in range(2 * n)},
        compiler_params=pltpu.CompilerParams(has_side_effects=_EFFECT))(*operands)


def _send_pieces_wait(started, after, name):
    send_sems, recv_sems, *thru, _ = started
    n = len(thru) // 2

    def body(*refs):
        ins, lands = refs[:n], refs[n:2 * n]
        send_sems, recv_sems = refs[2 * n:2 * n + 2]
        for r in range(1, 8):
            px, py, pc = _xor_peer(r)
            for k in range(n):
                copy = pltpu.make_async_remote_copy(
                    src_ref=ins[k].at[2 * px + py, pc], dst_ref=lands[k].at[4 * px + 2 * py + pc],
                    send_sem=send_sems.at[(r - 1) * n + k], recv_sem=recv_sems.at[(r - 1) * n + k],
                    device_id=(px, py, pc), device_id_type=MESH)
                copy.wait_send()
                copy.wait_recv()

    return pl.pallas_call(
        body, name=name, out_shape=tuple(pltpu.HBM(t.shape, t.dtype) for t in thru),
        in_specs=(_HBM,) * (2 * n) + (_SEM, _SEM, pl.BlockSpec(memory_space=pl.ANY)), out_specs=(_HBM,) * (2 * n),
        input_output_aliases={i: i for i in range(2 * n)},
        compiler_params=pltpu.CompilerParams(has_side_effects=_EFFECT))(*thru, send_sems, recv_sems, after)[n:]


def _sum8_class(own, land, chunk, name):
    rows, w = own.shape

    def body(own_ref, l_ref, o_ref):
        x, y, c = _place()
        me = 4 * x + 2 * y + c
        acc = jnp.where(me == 0, own_ref[...], l_ref[0].astype(F32))
        for d in range(1, 8):
            acc = acc + jnp.where(me == d, own_ref[...], l_ref[d].astype(F32))
        o_ref[...] = acc

    return pl.pallas_call(
        body, name=name, grid=(rows // chunk,),
        in_specs=[pl.BlockSpec((chunk, w), lambda i: (i, 0)), pl.BlockSpec((8, chunk, w), lambda i: (0, i, 0))],
        out_specs=pl.BlockSpec((chunk, w), lambda i: (i, 0)), out_shape=SDS((rows, w), F32),
        compiler_params=_params(1, VMEM_LIMIT))(own, land)


def _send_shard_wait(send_sems, recv_sems, v_thru, land_thru, after, name):
    def body(v_ref, land_ref, send_sems, recv_sems, after_ref, v_dead, got_ref):
        for j, (px, py, pc) in enumerate(_chip_peers()):
            copy = pltpu.make_async_remote_copy(src_ref=v_ref, dst_ref=land_ref.at[2 * px + py], send_sem=send_sems.at[j],
                                                recv_sem=recv_sems.at[j], device_id=(px, py, pc), device_id_type=MESH)
            copy.wait_send()
            copy.wait_recv()

    return pl.pallas_call(
        body, name=name,
        out_shape=(pltpu.HBM(v_thru.shape, v_thru.dtype), pltpu.HBM(land_thru.shape, land_thru.dtype)),
        in_specs=(_HBM, _HBM, _SEM, _SEM, pl.BlockSpec(memory_space=pl.ANY)), out_specs=(_HBM, _HBM),
        input_output_aliases={0: 0, 1: 1}, compiler_params=pltpu.CompilerParams(has_side_effects=_EFFECT),
    )(v_thru, land_thru, send_sems, recv_sems, after)[1]


_SMALL = ["mem_norm_w", "norm_pre", "norm_post", "a_ln_w", "a_ln_b", "a_w_s", "a_b_s", "b_conv_w", "c_conv_w",
          "c_a_log", "c_dt_bias", "c_o_norm_w"]
_SMALL_SHAPES = {"mem_norm_w": (D,), "norm_pre": (4, D), "norm_post": (4, D), "a_ln_w": (2, D), "a_ln_b": (2, D),
                 "a_w_s": (2, 8, HD, HD), "a_b_s": (2, 8, HD), "b_conv_w": (1, 3, D), "c_conv_w": (1, 4, 3 * D),
                 "c_a_log": (1, 8), "c_dt_bias": (1, 8), "c_o_norm_w": (1, HD)}
_SHARDED_SMALL = {"a_ln_w": D // 4, "a_ln_b": D // 4, "b_conv_w": D // 4, "c_conv_w": 3 * D // 4}
_SMALL_ROWS = 288


def _size(shape):
    n = 1
    for d in shape:
        n *= d
    return n


def kernel(x, mem, mem_norm_w, w_mem_kv, norm_pre, norm_post, w_out, a_w_in, a_ln_w, a_ln_b, a_w_s, a_b_s, b_w_in, b_conv_w, c_w_in, c_conv_w, c_a_log, c_dt_bias, c_o_norm_w, loss_target, m_mem_norm_w, m_w_mem_kv, m_norm_pre, m_norm_post, m_w_out, m_a_w_in, m_a_ln_w, m_a_ln_b, m_a_w_s, m_a_b_s, m_b_w_in, m_b_conv_w, m_c_w_in, m_c_conv_w, m_c_a_log, m_c_dt_bias, m_c_o_norm_w, v_mem_norm_w, v_w_mem_kv, v_norm_pre, v_norm_post, v_w_out, v_a_w_in, v_a_ln_w, v_a_ln_b, v_a_w_s, v_a_b_s, v_b_w_in, v_b_conv_w, v_c_w_in, v_c_conv_w, v_c_a_log, v_c_dt_bias, v_c_o_norm_w):
    names = ["mem_norm_w", "w_mem_kv", "norm_pre", "norm_post", "w_out", "a_w_in", "a_ln_w", "a_ln_b", "a_w_s", "a_b_s",
             "b_w_in", "b_conv_w", "c_w_in", "c_conv_w", "c_a_log", "c_dt_bias", "c_o_norm_w"]
    w = dict(zip(names, [mem_norm_w, w_mem_kv, norm_pre, norm_post, w_out, a_w_in, a_ln_w, a_ln_b, a_w_s, a_b_s, b_w_in,
                         b_conv_w, c_w_in, c_conv_w, c_a_log, c_dt_bias, c_o_norm_w]))
    m = dict(zip(names, [m_mem_norm_w, m_w_mem_kv, m_norm_pre, m_norm_post, m_w_out, m_a_w_in, m_a_ln_w, m_a_ln_b, m_a_w_s,
                         m_a_b_s, m_b_w_in, m_b_conv_w, m_c_w_in, m_c_conv_w, m_c_a_log, m_c_dt_bias, m_c_o_norm_w]))
    v = dict(zip(names, [v_mem_norm_w, v_w_mem_kv, v_norm_pre, v_norm_post, v_w_out, v_a_w_in, v_a_ln_w, v_a_ln_b, v_a_w_s,
                         v_a_b_s, v_b_w_in, v_b_conv_w, v_c_w_in, v_c_conv_w, v_c_a_log, v_c_dt_bias, v_c_o_norm_w]))
    chip = 2 * lax.axis_index("x") + lax.axis_index("y")

    def rows_of_ct(a):
        return a[0].T

    def with_mine(gathered, own):
        return lax.dynamic_update_slice(gathered, own[None], (chip,) + (0,) * own.ndim)

    first = [a_w_in[0].astype(BF16)]
    vec = jnp.concatenate([a_ln_w.reshape(-1), a_ln_b.reshape(-1), b_conv_w.reshape(-1), c_conv_w.reshape(-1)])
    vec = jnp.pad(vec, (0, 8 * D - vec.shape[0])).reshape(8, D)
    ga0, gvec = _gather_classes(first, [W_CLASSES[k] for k in W_CLASSES], vec)
    ga0, gvec = with_mine(ga0, first[0]), with_mine(gvec, vec)
    gv = gvec.reshape(4, 8 * D)
    w_in = {1: b_w_in[0], 2: jnp.pad(rows_of_ct(c_w_in), ((0, C_ROWS - 1284), (0, 0))), 3: a_w_in[1]}
    order = [("kv", 0, w_mem_kv), ("out", 0, w_out[0])]
    for i in range(1, DEPTH):
        order += [("in", i, w_in[i]), ("out", i, w_out[i])]
    later, sent, started = {}, {}, 0.0 * ga0[0, 0, 0].astype(F32)
    for kind, i, a in order:
        later[kind, i] = (a + started).astype(BF16)
        sent[kind, i] = _send_shard_start(later[kind, i], f"send_w_{kind}_{i}")
        started = sent[kind, i][4][0, 0]

    def arrived(k, after):
        return with_mine(_send_shard_wait(*sent[k][:4], after, f"wait_w_{k[0]}_{k[1]}"), later[k])

    def blocks(i, after):
        def wo_after(later_value):
            return arrived(("out", i), later_value).reshape(D_CAT, D)

        if i == 0:
            return [ga0[0], ga0[1]], [ga0[2], ga0[3]], wo_after
        got = arrived(("in", i), after)
        if i == 1:
            return [got[0], got[1], got[2][:, :512]], [got[2][:, 512:], got[3]], wo_after
        if i == 3:
            return [got[0], got[1]], [got[2], got[3]], wo_after
        fct = got[:, :1284].reshape(5136, D)
        c_ab = jnp.concatenate([fct[3 * D:3 * D + 16], jnp.zeros((AB_PAD - 16, D), BF16)], axis=0)
        return [fct[:3 * D], c_ab], [fct[3 * D + 16:]], wo_after
    sm = {"mem_norm_w": mem_norm_w, "norm_pre": norm_pre + started, "norm_post": norm_post, "a_w_s": a_w_s, "a_b_s": a_b_s,
          "c_a_log": c_a_log, "c_dt_bias": c_dt_bias, "c_o_norm_w": c_o_norm_w,
          "a_ln_w": gv[:, 0:512].reshape(4, 2, 256).transpose(1, 0, 2).reshape(2, D),
          "a_ln_b": gv[:, 512:1024].reshape(4, 2, 256).transpose(1, 0, 2).reshape(2, D),
          "b_conv_w": gv[:, 1024:1792].reshape(4, 1, 3, 256).transpose(1, 2, 0, 3).reshape(1, 3, D),
          "c_conv_w": gv[:, 1792:4864].reshape(4, 1, 4, 768).transpose(1, 2, 0, 3).reshape(1, 4, 3 * D)}
    wts = {"wkv_after": lambda after: arrived(("kv", 0), after).reshape(D, 2 * D_XA), "blocks": blocks}

    def layer_grads(i, g):
        if i % 3 == 2:
            gct = jnp.concatenate([g["wm"][i][:3 * D + 16], g["wg"][i]], axis=0).reshape(4, 1284, D)
            w_in = jnp.pad(gct, ((0, 0), (0, C_ROWS - 1284), (0, 0)))
        else:
            w_in = jnp.concatenate([g["wm"][i], g["wg"][i]], axis=0).reshape(4, -1, GRAD_TILE[i % 3])
        out = {f"in{i}": w_in, f"out{i}": g["wo"][i].reshape(4, 384, D)}
        return {k: a.reshape(4, 2, a.shape[1] // 2, a.shape[2]) for k, a in out.items()}

    pending = {}

    def layer_done(i, g):
        halves = layer_grads(i, g)
        started = _send_pieces_start([h.astype(BF16) for h in halves.values()], f"send_grads_{i}")
        pending[i] = (started, halves)
        return started[-1][0, 0]

    wts["layer_done"] = layer_done

    loss, dx, g = _local_step(x[0], mem[0], loss_target[0], wts, sm)

    core = lax.axis_index("c")
    mine, specs = {}, {}
    for i in reversed(range(DEPTH)):
        started, halves = pending[i]
        lands = _send_pieces_wait(started, dx, f"wait_grads_{i}")
        for (k, h), land in zip(halves.items(), lands):
            own = lax.dynamic_index_in_dim(lax.dynamic_index_in_dim(h, chip, 0, False), core, 0, False)
            specs[k] = (own.shape[0], _REDUCE_CHUNK.get(own.shape[0], own.shape[0]))
            mine[k] = _sum8_class(own, land, specs[k][1], f"sum8_{k}")

    first = {"kv": g["wkv"].reshape(4, 2, 128, D)}
    first_specs = [(h.shape[2], _REDUCE_CHUNK.get(h.shape[2], h.shape[2])) for h in first.values()]
    halves = list(first.values())
    gs = {"mem_norm_w": g["mem_norm_w"], "norm_pre": jnp.concatenate(g["norm_pre"]),
          "norm_post": jnp.concatenate(g["norm_post"])}
    for n in _SMALL[3:]:
        gs[n] = jnp.stack([g[n][j] for j in sorted(g[n])])
    flat = jnp.concatenate([gs[n].reshape(-1) for n in _SMALL] + [loss.reshape(1)])
    small = jnp.pad(flat, (0, _SMALL_ROWS * D - flat.shape[0])).reshape(_SMALL_ROWS, D)
    *others, other_small = _swap_classes(halves, first_specs, small)
    pairs = [_pair_sum_class(h, o, s[1], f"pair_sum_{k}") for k, h, o, s in zip(first, halves, others, first_specs)]
    pair_small = _add(small, other_small, "pair_sum_small")
    *lands, land_small = _exchange_classes([p[0] for p in pairs], first_specs, pair_small)
    for k, p, land, s in zip(first, pairs, lands, first_specs):
        mine[k], specs[k] = _chip_sum_class(p[1], land, s[1], f"chip_sum_{k}"), s
    theirs = _share_classes(list(mine.values()), [specs[k] for k in mine])
    south = core == 0
    sh = {k: jnp.concatenate([jnp.where(south, a, b), jnp.where(south, b, a)], axis=0)
          for (k, a), b in zip(mine.items(), theirs)}
    grads = {"a_w_in": jnp.stack([sh["in0"], sh["in3"]]),
             "b_w_in": sh["in1"].reshape(5, D, 256).transpose(1, 0, 2).reshape(b_w_in.shape),
             "c_w_in": sh["in2"][:1284], "w_out": jnp.stack([sh[f"out{i}"] for i in range(DEPTH)]),
             "w_mem_kv": sh["kv"]}
    flat = _sum4(pair_small, land_small).reshape(-1)
    loss = flat[sum(_size(s) for s in _SMALL_SHAPES.values())]
    off = 0
    for n in _SMALL:
        shape = _SMALL_SHAPES[n]
        full = flat[off:off + _size(shape)].reshape(shape)
        off += _size(shape)
        if n in _SHARDED_SMALL:
            full = lax.dynamic_slice_in_dim(full, chip * _SHARDED_SMALL[n], _SHARDED_SMALL[n], axis=len(shape) - 1)
        grads[n] = full

    delta, new_m, new_v = {}, {}, {}
    for n in names:
        shape = w[n].shape
        if n == "c_w_in":
            d_, m_, v_ = _adamw(rows_of_ct(w[n]), grads[n], rows_of_ct(m[n]), rows_of_ct(v[n]), f"adamw_{n}")
            delta[n], new_m[n], new_v[n], grads[n] = d_.T[None], m_.T[None], v_.T[None], grads[n].T[None]
            continue
        view = (1, shape[0]) if len(shape) == 1 else (_size(shape[:-1]), shape[-1])
        d_, m_, v_ = _adamw(w[n].reshape(view), grads[n].reshape(view), m[n].reshape(view), v[n].reshape(view),
                            f"adamw_{n}")
        delta[n], new_m[n], new_v[n] = d_.reshape(shape), m_.reshape(shape), v_.reshape(shape)
    return (loss, dx[None], *[grads[n].reshape(w[n].shape) for n in names], *[delta[n] for n in names],
            *[new_m[n] for n in names], *[new_v[n] for n in names])
```
